```python
import jax, jax.numpy as jnp
from jax import lax
import numpy as np

D_MODEL = 1024
BATCH = 16
SEQ = 2048
DEPTH = 1

PLE_DIM = 256
SSM_EXPAND = 2
D_INNER = SSM_EXPAND * D_MODEL
SSM_HEAD_DIM = 64
SSM_HEADS = D_INNER // SSM_HEAD_DIM
SSM_GROUPS = 4
HEADS_PER_GROUP = SSM_HEADS // SSM_GROUPS
D_STATE = 128
CONV_WIDTH = 4
CHUNK = 128
D_CONV_CH = D_INNER + 2 * SSM_GROUPS * D_STATE
D_POOL = D_MODEL
POOL_WINDOWS = (2, 4, 8, 16)
POOL_GROUPS = len(POOL_WINDOWS)
POOL_GROUP_DIM = D_POOL // POOL_GROUPS
IN_SPLITS = (D_INNER, D_CONV_CH, SSM_HEADS, D_POOL, D_POOL, D_MODEL, D_MODEL)
N_IN = sum(IN_SPLITS)
EPS = 1e-6

kernel_name = "hybrid_ssd_multipool_gated_merge"


def rms_norm(x, g):
    xf = x.astype(jnp.float32)
    y = xf * lax.rsqrt(jnp.mean(xf * xf, axis=-1, keepdims=True) + EPS)
    return (y * g.astype(jnp.float32)).astype(x.dtype)


def causal_depthwise_conv(u, w, b):
    out = lax.conv_general_dilated(
        u, w[:, None, :].astype(u.dtype), window_strides=(1,),
        padding=[(CONV_WIDTH - 1, 0)],
        dimension_numbers=("NWC", "WIO", "NWC"),
        feature_group_count=u.shape[-1])
    return out + b


def ssd_chunked(xs, dt, a_neg, bm, cm):
    bsz, seqlen = xs.shape[0], xs.shape[1]
    nc = seqlen // CHUNK
    dtype = xs.dtype
    x = xs.reshape(bsz, nc, CHUNK, SSM_GROUPS, HEADS_PER_GROUP, SSM_HEAD_DIM)
    dtc = dt.reshape(bsz, nc, CHUNK, SSM_GROUPS, HEADS_PER_GROUP)
    bc = bm.reshape(bsz, nc, CHUNK, SSM_GROUPS, D_STATE)
    cc = cm.reshape(bsz, nc, CHUNK, SSM_GROUPS, D_STATE)
    xdt = x * dtc[..., None]
    a_dt = dtc.astype(jnp.float32) * a_neg.astype(jnp.float32).reshape(SSM_GROUPS, HEADS_PER_GROUP)
    a_cs = jnp.cumsum(a_dt, axis=2)

    seg = a_cs[:, :, :, None] - a_cs[:, :, None, :]
    mask = jnp.tril(jnp.ones((CHUNK, CHUNK), dtype=bool))[:, :, None, None]
    decay_ls = jnp.exp(jnp.where(mask, seg, -jnp.inf)).astype(dtype)
    cb = jnp.einsum("bclgn,bcsgn->bclsg", cc, bc)
    m = cb[..., None] * decay_ls
    y_diag = jnp.einsum("bclsge,bcsgep->bclgep", m, xdt)

    decay_to_end = jnp.exp(a_cs[:, :, -1:] - a_cs).astype(dtype)
    states = jnp.einsum("bclgn,bclgep->bcgepn", bc, xdt * decay_to_end[..., None])
    chunk_decay = jnp.exp(a_cs[:, :, -1]).astype(dtype)

    def step(h, inp):
        s_c, d_c = inp
        return h * d_c[..., None, None] + s_c, h

    h0 = jnp.zeros_like(states[:, 0])
    _, prev = lax.scan(step, h0, (jnp.moveaxis(states, 1, 0), jnp.moveaxis(chunk_decay, 1, 0)))
    prev = jnp.moveaxis(prev, 0, 1)

    y_off = jnp.einsum("bclgn,bcgepn->bclgep", cc, prev) * jnp.exp(a_cs).astype(dtype)[..., None]
    return (y_diag + y_off).reshape(bsz, seqlen, D_INNER)


def causal_multiscale_pool(u, mix_w, mix_b, scale):
    bsz, seqlen = u.shape[0], u.shape[1]
    ug = u.reshape(bsz, seqlen, POOL_GROUPS, POOL_GROUP_DIM)
    csum = jnp.cumsum(ug.astype(jnp.float32), axis=1)
    pos = jnp.arange(1, seqlen + 1, dtype=jnp.float32)
    means = []
    for gi, w in enumerate(POOL_WINDOWS):
        cs = csum[:, :, gi]
        lag = jnp.pad(cs, ((0, 0), (w, 0), (0, 0)))[:, :seqlen]
        means.append((cs - lag) / jnp.minimum(pos, float(w))[None, :, None])
    pooled = jnp.stack(means, axis=2).astype(u.dtype) - ug
    mixed = jnp.einsum("bsgc,gcd->bsgd", pooled, mix_w).reshape(bsz, seqlen, D_POOL) + mix_b
    return mixed * scale


def _fwd_setup_inputs(seed: int = 0) -> dict:
    key = jax.random.key(seed)
    ks = jax.random.split(key, 20)
    f32 = jnp.float32
    nrm = lambda k, shape, s: jax.random.normal(k, shape, f32) * s
    dt0 = jnp.exp(jax.random.uniform(ks[6], (DEPTH, SSM_HEADS), f32, np.log(1e-3), np.log(1e-1)))
    return {
        "x": nrm(ks[0], (BATCH, SEQ, D_MODEL), 1.0),
        "p": nrm(ks[1], (DEPTH, BATCH, SEQ, PLE_DIM), 1.0),
        "norm_g": 1.0 + nrm(ks[2], (DEPTH, D_MODEL), 0.05),
        "w_in": nrm(ks[3], (DEPTH, D_MODEL, N_IN), D_MODEL ** -0.5),
        "conv_w": nrm(ks[4], (DEPTH, CONV_WIDTH, D_CONV_CH), CONV_WIDTH ** -0.5),
        "conv_b": nrm(ks[5], (DEPTH, D_CONV_CH), 0.02),
        "dt_bias": dt0 + jnp.log(-jnp.expm1(-dt0)),
        "a_log": jnp.log(jax.random.uniform(ks[7], (DEPTH, SSM_HEADS), f32, 1.0, 16.0)),
        "d_skip": 1.0 + nrm(ks[8], (DEPTH, SSM_HEADS), 0.1),
        "gnorm_g": 1.0 + nrm(ks[9], (DEPTH, D_INNER), 0.05),
        "pool_mix_w": nrm(ks[10], (DEPTH, POOL_GROUPS, POOL_GROUP_DIM, POOL_GROUP_DIM), POOL_GROUP_DIM ** -0.5),
        "pool_mix_b": nrm(ks[11], (DEPTH, D_POOL), 0.02),
        "pool_scale": 1.0 + nrm(ks[12], (DEPTH, D_POOL), 0.1),
        "w_branch_a": nrm(ks[13], (DEPTH, D_INNER, D_MODEL), D_INNER ** -0.5),
        "w_branch_b": nrm(ks[14], (DEPTH, D_POOL, D_MODEL), D_POOL ** -0.5),
        "w_out": nrm(ks[15], (DEPTH, D_MODEL, D_MODEL), D_MODEL ** -0.5),
        "ple_norm_g": 1.0 + nrm(ks[16], (DEPTH, D_MODEL), 0.05),
        "w_ple_gate": nrm(ks[17], (DEPTH, D_MODEL, D_MODEL), D_MODEL ** -0.5),
        "w_ple_up": nrm(ks[18], (DEPTH, PLE_DIM, D_MODEL), PLE_DIM ** -0.5),
        "final_g": 1.0 + nrm(ks[19], (D_MODEL,), 0.05),
    }


def _fwd_reference(x, p, norm_g, w_in, conv_w, conv_b, dt_bias, a_log, d_skip, gnorm_g,
              pool_mix_w, pool_mix_b, pool_scale, w_branch_a, w_branch_b, w_out,
              ple_norm_g, w_ple_gate, w_ple_up, final_g):
    split_idx = [int(v) for v in np.cumsum(IN_SPLITS)[:-1]]
    bsz, seqlen = x.shape[0], x.shape[1]
    for i in range(DEPTH):
        h = rms_norm(x, norm_g[i])
        proj = h @ w_in[i]
        z, xbc, dt_raw, u, z_pool, g_a, g_b = jnp.split(proj, split_idx, axis=-1)

        xbc = jax.nn.silu(causal_depthwise_conv(xbc, conv_w[i], conv_b[i]))
        x_ssm, b_ssm, c_ssm = jnp.split(xbc, [D_INNER, D_INNER + SSM_GROUPS * D_STATE], axis=-1)
        dt = jax.nn.softplus(dt_raw + dt_bias[i])
        a_neg = -jnp.exp(a_log[i])
        y_a = ssd_chunked(x_ssm, dt, a_neg, b_ssm, c_ssm)
        y_a = y_a + x_ssm * jnp.repeat(d_skip[i], SSM_HEAD_DIM)
        y_a = (y_a * jax.nn.silu(z)).reshape(bsz, seqlen, SSM_GROUPS, D_INNER // SSM_GROUPS)
        y_a = rms_norm(y_a, gnorm_g[i].reshape(SSM_GROUPS, D_INNER // SSM_GROUPS)).reshape(bsz, seqlen, D_INNER)

        y_b = causal_multiscale_pool(u, pool_mix_w[i], pool_mix_b[i], pool_scale[i]) * jax.nn.silu(z_pool)

        merged = jax.nn.sigmoid(g_a) * (y_a @ w_branch_a[i]) + jax.nn.sigmoid(g_b) * (y_b @ w_branch_b[i])
        x = x + merged @ w_out[i]

        gate = jax.nn.sigmoid(rms_norm(x, ple_norm_g[i]) @ w_ple_gate[i])
        x = x + gate * (p[i] @ w_ple_up[i])
    return rms_norm(x, final_g)


import jax as _jax
import jax.numpy as _jnp

TWIN_FORMAT = 'train_step'
FWD_PARAMS = ['x', 'p', 'norm_g', 'w_in', 'conv_w', 'conv_b', 'dt_bias', 'a_log', 'd_skip', 'gnorm_g', 'pool_mix_w', 'pool_mix_b', 'pool_scale', 'w_branch_a', 'w_branch_b', 'w_out', 'ple_norm_g', 'w_ple_gate', 'w_ple_up', 'final_g']
TWIN_WEIGHTS = ['norm_g', 'w_in', 'conv_w', 'conv_b', 'dt_bias', 'a_log', 'd_skip', 'gnorm_g', 'pool_mix_w', 'pool_mix_b', 'pool_scale', 'w_branch_a', 'w_branch_b', 'w_out', 'ple_norm_g', 'w_ple_gate', 'w_ple_up', 'final_g']
TWIN_DIFF_INPUT = 'x'
TWIN_INPUTS = ['x', 'p', 'norm_g', 'w_in', 'conv_w', 'conv_b', 'dt_bias', 'a_log', 'd_skip', 'gnorm_g', 'pool_mix_w', 'pool_mix_b', 'pool_scale', 'w_branch_a', 'w_branch_b', 'w_out', 'ple_norm_g', 'w_ple_gate', 'w_ple_up', 'final_g', 'loss_target', 'm_norm_g', 'm_w_in', 'm_conv_w', 'm_conv_b', 'm_dt_bias', 'm_a_log', 'm_d_skip', 'm_gnorm_g', 'm_pool_mix_w', 'm_pool_mix_b', 'm_pool_scale', 'm_w_branch_a', 'm_w_branch_b', 'm_w_out', 'm_ple_norm_g', 'm_w_ple_gate', 'm_w_ple_up', 'm_final_g', 'v_norm_g', 'v_w_in', 'v_conv_w', 'v_conv_b', 'v_dt_bias', 'v_a_log', 'v_d_skip', 'v_gnorm_g', 'v_pool_mix_w', 'v_pool_mix_b', 'v_pool_scale', 'v_w_branch_a', 'v_w_branch_b', 'v_w_out', 'v_ple_norm_g', 'v_w_ple_gate', 'v_w_ple_up', 'v_final_g']
TWIN_OUTPUTS = ['loss', 'grad_x', 'grad_norm_g', 'grad_w_in', 'grad_conv_w', 'grad_conv_b', 'grad_dt_bias', 'grad_a_log', 'grad_d_skip', 'grad_gnorm_g', 'grad_pool_mix_w', 'grad_pool_mix_b', 'grad_pool_scale', 'grad_w_branch_a', 'grad_w_branch_b', 'grad_w_out', 'grad_ple_norm_g', 'grad_w_ple_gate', 'grad_w_ple_up', 'grad_final_g', 'delta_norm_g', 'delta_w_in', 'delta_conv_w', 'delta_conv_b', 'delta_dt_bias', 'delta_a_log', 'delta_d_skip', 'delta_gnorm_g', 'delta_pool_mix_w', 'delta_pool_mix_b', 'delta_pool_scale', 'delta_w_branch_a', 'delta_w_branch_b', 'delta_w_out', 'delta_ple_norm_g', 'delta_w_ple_gate', 'delta_w_ple_up', 'delta_final_g', 'new_m_norm_g', 'new_m_w_in', 'new_m_conv_w', 'new_m_conv_b', 'new_m_dt_bias', 'new_m_a_log', 'new_m_d_skip', 'new_m_gnorm_g', 'new_m_pool_mix_w', 'new_m_pool_mix_b', 'new_m_pool_scale', 'new_m_w_branch_a', 'new_m_w_branch_b', 'new_m_w_out', 'new_m_ple_norm_g', 'new_m_w_ple_gate', 'new_m_w_ple_up', 'new_m_final_g', 'new_v_norm_g', 'new_v_w_in', 'new_v_conv_w', 'new_v_conv_b', 'new_v_dt_bias', 'new_v_a_log', 'new_v_d_skip', 'new_v_gnorm_g', 'new_v_pool_mix_w', 'new_v_pool_mix_b', 'new_v_pool_scale', 'new_v_w_branch_a', 'new_v_w_branch_b', 'new_v_w_out', 'new_v_ple_norm_g', 'new_v_w_ple_gate', 'new_v_w_ple_up', 'new_v_final_g']
TWIN_LEAF_KINDS = {'loss': 'loss', 'grad_x': 'grad_x', 'grad_norm_g': 'grad_w', 'grad_w_in': 'grad_w', 'grad_conv_w': 'grad_w', 'grad_conv_b': 'grad_w', 'grad_dt_bias': 'grad_w', 'grad_a_log': 'grad_w', 'grad_d_skip': 'grad_w', 'grad_gnorm_g': 'grad_w', 'grad_pool_mix_w': 'grad_w', 'grad_pool_mix_b': 'grad_w', 'grad_pool_scale': 'grad_w', 'grad_w_branch_a': 'grad_w', 'grad_w_branch_b': 'grad_w', 'grad_w_out': 'grad_w', 'grad_ple_norm_g': 'grad_w', 'grad_w_ple_gate': 'grad_w', 'grad_w_ple_up': 'grad_w', 'grad_final_g': 'grad_w', 'delta_norm_g': 'delta_w', 'delta_w_in': 'delta_w', 'delta_conv_w': 'delta_w', 'delta_conv_b': 'delta_w', 'delta_dt_bias': 'delta_w', 'delta_a_log': 'delta_w', 'delta_d_skip': 'delta_w', 'delta_gnorm_g': 'delta_w', 'delta_pool_mix_w': 'delta_w', 'delta_pool_mix_b': 'delta_w', 'delta_pool_scale': 'delta_w', 'delta_w_branch_a': 'delta_w', 'delta_w_branch_b': 'delta_w', 'delta_w_out': 'delta_w', 'delta_ple_norm_g': 'delta_w', 'delta_w_ple_gate': 'delta_w', 'delta_w_ple_up': 'delta_w', 'delta_final_g': 'delta_w', 'new_m_norm_g': 'new_m', 'new_m_w_in': 'new_m', 'new_m_conv_w': 'new_m', 'new_m_conv_b': 'new_m', 'new_m_dt_bias': 'new_m', 'new_m_a_log': 'new_m', 'new_m_d_skip': 'new_m', 'new_m_gnorm_g': 'new_m', 'new_m_pool_mix_w': 'new_m', 'new_m_pool_mix_b': 'new_m', 'new_m_pool_scale': 'new_m', 'new_m_w_branch_a': 'new_m', 'new_m_w_branch_b': 'new_m', 'new_m_w_out': 'new_m', 'new_m_ple_norm_g': 'new_m', 'new_m_w_ple_gate': 'new_m', 'new_m_w_ple_up': 'new_m', 'new_m_final_g': 'new_m', 'new_v_norm_g': 'new_v', 'new_v_w_in': 'new_v', 'new_v_conv_w': 'new_v', 'new_v_conv_b': 'new_v', 'new_v_dt_bias': 'new_v', 'new_v_a_log': 'new_v', 'new_v_d_skip': 'new_v', 'new_v_gnorm_g': 'new_v', 'new_v_pool_mix_w': 'new_v', 'new_v_pool_mix_b': 'new_v', 'new_v_pool_scale': 'new_v', 'new_v_w_branch_a': 'new_v', 'new_v_w_branch_b': 'new_v', 'new_v_w_out': 'new_v', 'new_v_ple_norm_g': 'new_v', 'new_v_w_ple_gate': 'new_v', 'new_v_w_ple_up': 'new_v', 'new_v_final_g': 'new_v'}


def _forward(args):
    return _fwd_reference(*[args[k] for k in FWD_PARAMS])


def _output_shape():
    out = _jax.eval_shape(lambda: _forward(_fwd_setup_inputs(0)))
    return out.shape, out.dtype

N_MICROBATCH = 1
ADAM_LR = 0.001
ADAM_B1 = 0.9
ADAM_B2 = 0.999
ADAM_EPS = 1e-08
ADAM_WD = 0.01
ADAM_STEP = 10
PER_EXAMPLE_BATCH_AXIS = {'x': 0, 'p': 1, 'loss_target': 0}
SHARED_INPUTS = []
_WEIGHT_DTYPES = {'norm_g': _jnp.float32, 'w_in': _jnp.float32, 'conv_w': _jnp.float32, 'conv_b': _jnp.float32, 'dt_bias': _jnp.float32, 'a_log': _jnp.float32, 'd_skip': _jnp.float32, 'gnorm_g': _jnp.float32, 'pool_mix_w': _jnp.float32, 'pool_mix_b': _jnp.float32, 'pool_scale': _jnp.float32, 'w_branch_a': _jnp.float32, 'w_branch_b': _jnp.float32, 'w_out': _jnp.float32, 'ple_norm_g': _jnp.float32, 'w_ple_gate': _jnp.float32, 'w_ple_up': _jnp.float32, 'final_g': _jnp.float32}
MOMENT_SCALE = {'norm_g': 1.361300e-01, 'w_in': 4.413780e-02, 'conv_w': 4.727915e-02, 'conv_b': 6.923046e-02, 'dt_bias': 2.224653e-01, 'a_log': 1.563926e-01, 'd_skip': 5.614121e-01, 'gnorm_g': 5.587268e-02, 'pool_mix_w': 4.018446e-02, 'pool_mix_b': 4.726291e-02, 'pool_scale': 3.958037e-02, 'w_branch_a': 7.692151e-02, 'w_branch_b': 4.031514e-02, 'w_out': 8.741512e-02, 'ple_norm_g': 3.424866e-02, 'w_ple_gate': 3.050777e-02, 'w_ple_up': 7.604520e-02, 'final_g': 3.199328e+01}


def _to_microbatches(a, axis):
    t = _jnp.moveaxis(a, axis, 0)
    t = t.reshape((N_MICROBATCH, t.shape[0] // N_MICROBATCH) + t.shape[1:])
    return _jnp.moveaxis(t, 1, axis + 1)


def setup_inputs(seed: int = 0) -> dict:
    inp = _fwd_setup_inputs(seed)
    key = _jax.random.fold_in(_jax.random.key(seed), 7919)
    shape, _ = _output_shape()
    out = dict(inp)
    out["loss_target"] = _jax.random.normal(_jax.random.fold_in(key, 0), shape, _jnp.float32)
    for i, name in enumerate(TWIN_WEIGHTS):
        w = inp[name].astype(_jnp.float32)
        if MOMENT_SCALE is None:
            s = _jnp.sqrt(_jnp.mean(_jnp.square(w)) + 1e-30)
        else:
            s = MOMENT_SCALE[name]
        km, kv = _jax.random.split(_jax.random.fold_in(key, i + 1))
        out[name] = w
        out["m_" + name] = s * _jax.random.normal(km, w.shape, _jnp.float32)
        out["v_" + name] = (s * s) * _jax.random.uniform(kv, w.shape, _jnp.float32, 0.5, 1.5)
    if N_MICROBATCH > 1:
        for name, axis in PER_EXAMPLE_BATCH_AXIS.items():
            out[name] = _to_microbatches(out[name], axis)
    return {'x': out['x'], 'p': out['p'], 'norm_g': out['norm_g'], 'w_in': out['w_in'], 'conv_w': out['conv_w'], 'conv_b': out['conv_b'], 'dt_bias': out['dt_bias'], 'a_log': out['a_log'], 'd_skip': out['d_skip'], 'gnorm_g': out['gnorm_g'], 'pool_mix_w': out['pool_mix_w'], 'pool_mix_b': out['pool_mix_b'], 'pool_scale': out['pool_scale'], 'w_branch_a': out['w_branch_a'], 'w_branch_b': out['w_branch_b'], 'w_out': out['w_out'], 'ple_norm_g': out['ple_norm_g'], 'w_ple_gate': out['w_ple_gate'], 'w_ple_up': out['w_ple_up'], 'final_g': out['final_g'], 'loss_target': out['loss_target'], 'm_norm_g': out['m_norm_g'], 'm_w_in': out['m_w_in'], 'm_conv_w': out['m_conv_w'], 'm_conv_b': out['m_conv_b'], 'm_dt_bias': out['m_dt_bias'], 'm_a_log': out['m_a_log'], 'm_d_skip': out['m_d_skip'], 'm_gnorm_g': out['m_gnorm_g'], 'm_pool_mix_w': out['m_pool_mix_w'], 'm_pool_mix_b': out['m_pool_mix_b'], 'm_pool_scale': out['m_pool_scale'], 'm_w_branch_a': out['m_w_branch_a'], 'm_w_branch_b': out['m_w_branch_b'], 'm_w_out': out['m_w_out'], 'm_ple_norm_g': out['m_ple_norm_g'], 'm_w_ple_gate': out['m_w_ple_gate'], 'm_w_ple_up': out['m_w_ple_up'], 'm_final_g': out['m_final_g'], 'v_norm_g': out['v_norm_g'], 'v_w_in': out['v_w_in'], 'v_conv_w': out['v_conv_w'], 'v_conv_b': out['v_conv_b'], 'v_dt_bias': out['v_dt_bias'], 'v_a_log': out['v_a_log'], 'v_d_skip': out['v_d_skip'], 'v_gnorm_g': out['v_gnorm_g'], 'v_pool_mix_w': out['v_pool_mix_w'], 'v_pool_mix_b': out['v_pool_mix_b'], 'v_pool_scale': out['v_pool_scale'], 'v_w_branch_a': out['v_w_branch_a'], 'v_w_branch_b': out['v_w_branch_b'], 'v_w_out': out['v_w_out'], 'v_ple_norm_g': out['v_ple_norm_g'], 'v_w_ple_gate': out['v_w_ple_gate'], 'v_w_ple_up': out['v_w_ple_up'], 'v_final_g': out['v_final_g']}


def _loss(weights, diff, rest, loss_target):
    with _jax.named_scope("forward"):
        args = {**rest, TWIN_DIFF_INPUT: diff, **{k: w.astype(_WEIGHT_DTYPES[k]) for k, w in weights.items()}}
        y = _forward(args)
    with _jax.named_scope("loss_head"):
        err = _jnp.square(y.astype(_jnp.float32) - loss_target)
        return 0.5 * _jnp.sum(_jnp.mean(err, axis=-1)) if err.ndim else 0.5 * err


def _adamw(w, g, m, v):
    m = ADAM_B1 * m + (1.0 - ADAM_B1) * g
    v = ADAM_B2 * v + (1.0 - ADAM_B2) * _jnp.square(g)
    m_hat = m / (1.0 - ADAM_B1 ** ADAM_STEP)
    v_hat = v / (1.0 - ADAM_B2 ** ADAM_STEP)
    delta = -ADAM_LR * (m_hat / (_jnp.sqrt(v_hat) + ADAM_EPS) + ADAM_WD * w)
    return delta, m, v


def reference(x, p, norm_g, w_in, conv_w, conv_b, dt_bias, a_log, d_skip, gnorm_g, pool_mix_w, pool_mix_b, pool_scale, w_branch_a, w_branch_b, w_out, ple_norm_g, w_ple_gate, w_ple_up, final_g, loss_target, m_norm_g, m_w_in, m_conv_w, m_conv_b, m_dt_bias, m_a_log, m_d_skip, m_gnorm_g, m_pool_mix_w, m_pool_mix_b, m_pool_scale, m_w_branch_a, m_w_branch_b, m_w_out, m_ple_norm_g, m_w_ple_gate, m_w_ple_up, m_final_g, v_norm_g, v_w_in, v_conv_w, v_conv_b, v_dt_bias, v_a_log, v_d_skip, v_gnorm_g, v_pool_mix_w, v_pool_mix_b, v_pool_scale, v_w_branch_a, v_w_branch_b, v_w_out, v_ple_norm_g, v_w_ple_gate, v_w_ple_up, v_final_g):
    given = dict(x=x, p=p, norm_g=norm_g, w_in=w_in, conv_w=conv_w, conv_b=conv_b, dt_bias=dt_bias, a_log=a_log, d_skip=d_skip, gnorm_g=gnorm_g, pool_mix_w=pool_mix_w, pool_mix_b=pool_mix_b, pool_scale=pool_scale, w_branch_a=w_branch_a, w_branch_b=w_branch_b, w_out=w_out, ple_norm_g=ple_norm_g, w_ple_gate=w_ple_gate, w_ple_up=w_ple_up, final_g=final_g, loss_target=loss_target, m_norm_g=m_norm_g, m_w_in=m_w_in, m_conv_w=m_conv_w, m_conv_b=m_conv_b, m_dt_bias=m_dt_bias, m_a_log=m_a_log, m_d_skip=m_d_skip, m_gnorm_g=m_gnorm_g, m_pool_mix_w=m_pool_mix_w, m_pool_mix_b=m_pool_mix_b, m_pool_scale=m_pool_scale, m_w_branch_a=m_w_branch_a, m_w_branch_b=m_w_branch_b, m_w_out=m_w_out, m_ple_norm_g=m_ple_norm_g, m_w_ple_gate=m_w_ple_gate, m_w_ple_up=m_w_ple_up, m_final_g=m_final_g, v_norm_g=v_norm_g, v_w_in=v_w_in, v_conv_w=v_conv_w, v_conv_b=v_conv_b, v_dt_bias=v_dt_bias, v_a_log=v_a_log, v_d_skip=v_d_skip, v_gnorm_g=v_gnorm_g, v_pool_mix_w=v_pool_mix_w, v_pool_mix_b=v_pool_mix_b, v_pool_scale=v_pool_scale, v_w_branch_a=v_w_branch_a, v_w_branch_b=v_w_branch_b, v_w_out=v_w_out, v_ple_norm_g=v_ple_norm_g, v_w_ple_gate=v_w_ple_gate, v_w_ple_up=v_w_ple_up, v_final_g=v_final_g)
    weights = {n: given[n] for n in TWIN_WEIGHTS}
    shared = {n: given[n] for n in SHARED_INPUTS}
    per_example = {n: given[n] for n in ['x', 'p']}
    grad_fn = _jax.value_and_grad(_loss, argnums=(0, 1))

    def one_microbatch(ex, loss_target):
        ex = dict(ex)
        diff = ex.pop(TWIN_DIFF_INPUT)
        return grad_fn(weights, diff, {**shared, **ex}, loss_target)

    if N_MICROBATCH == 1:
        loss, (grad_w, grad_x) = one_microbatch(per_example, given["loss_target"])
    else:
        def body(carry, xs):
            loss_sum, grad_sum = carry
            l_k, (gw_k, gx_k) = one_microbatch(xs[0], xs[1])
            with _jax.named_scope("update"):
                return (loss_sum + l_k, _jax.tree.map(_jnp.add, grad_sum, gw_k)), gx_k

        init = (_jnp.zeros((), _jnp.float32), _jax.tree.map(_jnp.zeros_like, weights))
        (loss, grad_w), grad_x = _jax.lax.scan(body, init, (per_example, given["loss_target"]))
    with _jax.named_scope("update"):
        delta_w, new_m, new_v = {}, {}, {}
        for n in TWIN_WEIGHTS:
            delta_w[n], new_m[n], new_v[n] = _adamw(weights[n], grad_w[n], given["m_" + n], given["v_" + n])
    return (loss, grad_x, *[grad_w[n] for n in TWIN_WEIGHTS], *[delta_w[n] for n in TWIN_WEIGHTS],
            *[new_m[n] for n in TWIN_WEIGHTS], *[new_v[n] for n in TWIN_WEIGHTS])
```

```python
import functools

import jax
import jax.numpy as jnp
from jax import lax
from jax.experimental import pallas as pl
from jax.experimental.pallas import tpu as pltpu

F32 = jnp.float32
BF16 = jnp.bfloat16
I32 = jnp.int32
MESH = pl.DeviceIdType.MESH
ANY = pl.BlockSpec(memory_space=pl.ANY)

EPS = 1e-6
CHUNK = 128
HEAD_DIM = 64
D_STATE = 128
GROUPS = 4
CONV_W = 4
POOL_WINDOWS = (2, 4, 8, 16)
LANES = 128
HEAD_PAD = 128
NEG = -1e30
N_DEV = 8

ADAM_LR = 0.001
ADAM_B1 = 0.9
ADAM_B2 = 0.999
ADAM_EPS = 1e-08
ADAM_WD = 0.01
ADAM_STEP = 10

NN = (((1,), (0,)), ((), ()))
NT = (((1,), (1,)), ((), ()))
TN = (((0,), (0,)), ((), ()))


def _pallas(body, **kw):
    return pl.pallas_call(body, **kw)


def _cparams(dims=None, vmem_mb=None):
    kw = {}
    if dims is not None:
        kw["dimension_semantics"] = dims
    if vmem_mb is not None:
        kw["vmem_limit_bytes"] = vmem_mb << 20
    return pltpu.CompilerParams(**kw)


def _dot(a, b, dims=NN):
    return lax.dot_general(a, b, dims, preferred_element_type=F32)


def _split3(v):
    hi = v.astype(BF16)
    r = v - hi.astype(F32)
    mid = r.astype(BF16)
    lo = (r - mid.astype(F32)).astype(BF16)
    return hi, mid, lo


def _exact_r(v, m, dims=NN):
    hi, mid, lo = _split3(v)
    return _dot(hi, m, dims) + _dot(mid, m, dims) + _dot(lo, m, dims)


def _exact_l(m, v, dims=NN):
    hi, mid, lo = _split3(v)
    return _dot(m, hi, dims) + _dot(m, mid, dims) + _dot(m, lo, dims)


def _row_tile(rows, target):
    best = None
    for cand in range(8, min(rows, target) + 1, 8):
        if rows % cand == 0:
            best = cand
    assert best is not None, rows
    return best


def _sum_all(v):
    return jnp.sum(jnp.sum(v, axis=1, keepdims=True), axis=0, keepdims=True)


def _iota(shape, dim):
    return lax.broadcasted_iota(I32, shape, dim)


def _sigmoid(x):
    return 1.0 / (1.0 + jnp.exp(-x))


def _softplus(x):
    return jnp.maximum(x, 0.0) + jnp.log(1.0 + jnp.exp(-jnp.abs(x)))


def _shift_down(v, j, row):
    return jnp.where(row >= j, pltpu.roll(v, j, 0), 0.0)


def _shift_up(v, j, row):
    n = v.shape[0]
    return jnp.where(row < n - j, pltpu.roll(v, n - j, 0), 0.0)


def _matmul(a, b, mode, out_dtype, name, tm, tn, tk):
    if mode == "nn":
        (m, k), n = a.shape, b.shape[1]
    elif mode == "nt":
        (m, k), n = a.shape, b.shape[0]
    else:
        (k, m), n = a.shape, b.shape[1]
    tm, tn, tk = min(tm, m), min(tn, n), min(tk, k)
    assert m % tm == 0 and n % tn == 0 and k % tk == 0, (name, m, n, k)
    nk = k // tk
    dims = {"nn": NN, "nt": NT, "tn": TN}[mode]

    def body(a_ref, b_ref, o_ref, acc_ref):
        kk = pl.program_id(2)

        @pl.when(kk == 0)
        def _():
            acc_ref[...] = jnp.zeros_like(acc_ref)

        acc_ref[...] += _dot(a_ref[...].astype(BF16), b_ref[...].astype(BF16), dims)

        @pl.when(kk == nk - 1)
        def _():
            o_ref[...] = acc_ref[...].astype(o_ref.dtype)

    if mode == "tn":
        a_spec = pl.BlockSpec((tk, tm), lambda i, j, kk: (kk, i))
    else:
        a_spec = pl.BlockSpec((tm, tk), lambda i, j, kk: (i, kk))
    if mode == "nt":
        b_spec = pl.BlockSpec((tn, tk), lambda i, j, kk: (j, kk))
    else:
        b_spec = pl.BlockSpec((tk, tn), lambda i, j, kk: (kk, j))
    return _pallas(
        body, name=name, grid=(m // tm, n // tn, nk),
        in_specs=[a_spec, b_spec],
        out_specs=pl.BlockSpec((tm, tn), lambda i, j, kk: (i, j)),
        out_shape=jax.ShapeDtypeStruct((m, n), out_dtype),
        scratch_shapes=[pltpu.VMEM((tm, tn), F32)],
        compiler_params=_cparams(("parallel", "parallel", "arbitrary"), 48),
    )(a, b)


def _rms_fwd(x, g, name):
    t, d = x.shape
    tt = 512

    def body(x_ref, g_ref, h_ref):
        xv = x_ref[...]
        r = lax.rsqrt(jnp.mean(xv * xv, axis=-1, keepdims=True) + EPS)
        h_ref[...] = (xv * r * g_ref[...]).astype(h_ref.dtype)

    return _pallas(
        body, name=name, grid=(t // tt,),
        in_specs=[pl.BlockSpec((tt, d), lambda i: (i, 0)), pl.BlockSpec((1, d), lambda i: (0, 0))],
        out_specs=pl.BlockSpec((tt, d), lambda i: (i, 0)),
        out_shape=jax.ShapeDtypeStruct((t, d), BF16),
        compiler_params=_cparams(("parallel",)),
    )(x, g)


def _conv_fwd(proj, conv_w, conv_b, nb, s, off_xbc, n_conv):
    cw = 256

    def body(x_ref, w_ref, b_ref, o_ref):
        xv = x_ref[...]
        row = _iota(xv.shape, 0)
        acc = xv * w_ref[CONV_W - 1:CONV_W, :] + b_ref[...]
        for j in range(1, CONV_W):
            acc = acc + _shift_down(xv, j, row) * w_ref[CONV_W - 1 - j:CONV_W - j, :]
        o_ref[...] = acc * _sigmoid(acc)

    return _pallas(
        body, name="conv_fwd", grid=(nb, n_conv // cw),
        in_specs=[pl.BlockSpec((s, cw), lambda b, j: (b, off_xbc // cw + j)),
                  pl.BlockSpec((CONV_W, cw), lambda b, j: (0, j)),
                  pl.BlockSpec((1, cw), lambda b, j: (0, j))],
        out_specs=pl.BlockSpec((s, cw), lambda b, j: (b, j)),
        out_shape=jax.ShapeDtypeStruct((nb * s, n_conv), F32),
        compiler_params=_cparams(("parallel", "parallel"), 48),
    )(proj, conv_w, conv_b)


def _ssd_common(dtraw, bias, alog, n_heads):
    l = dtraw.shape[0]
    lane = _iota((l, HEAD_PAD), 1)
    lane1 = _iota((1, HEAD_PAD), 1)
    apad = jnp.where(lane1 < n_heads, -jnp.exp(alog), 0.0)
    dt = jnp.where(lane < n_heads, _softplus(dtraw + bias), 0.0)
    a = dt * apad
    tri = (_iota((l, l), 1) <= _iota((l, l), 0)).astype(BF16)
    acs = _exact_l(tri, a)
    width = n_heads * HEAD_DIM
    ex = (_iota((HEAD_PAD, width), 1) // HEAD_DIM == _iota((HEAD_PAD, width), 0)).astype(BF16)
    dt_full = _exact_r(dt, ex)
    acs_full = _exact_r(acs, ex)
    return dt, apad, acs, dt_full, acs_full, tri, ex


def _decay_mats(acs, acs_t, h, mask):
    col = acs[:, h:h + 1]
    row = acs_t[h:h + 1, :]
    return jnp.exp(jnp.where(mask, col - row, NEG))


def _ssd_fwd(xbc, proj, dtraw, dt_bias, a_log, dskip_full, gnorm, nb, s, d_inner):
    l = CHUNK
    nc = s // l
    n_heads = d_inner // HEAD_DIM
    gw = d_inner // GROUPS
    pairs_per_group = gw // LANES
    gn_cols = GROUPS * D_STATE

    def body(xs_ref, bm_ref, cm_ref, z_ref, dtr_ref, bias_ref, alog_ref, dsk_ref, gn_ref,
             y_ref, ya3_ref, prev_ref, state_ref):
        @pl.when(pl.program_id(1) == 0)
        def _():
            state_ref[...] = jnp.zeros_like(state_ref)

        xs = xs_ref[...]
        dt, apad, acs, dt_full, acs_full, tri, ex = _ssd_common(
            dtr_ref[...], bias_ref[...], alog_ref[...], n_heads)
        acs_t = acs.T
        e_full = jnp.exp(acs_full)
        last = acs_full[l - 1:l, :]
        dte = jnp.exp(last - acs_full)
        cd = jnp.exp(last)
        xdt = xs * dt_full
        xd = xdt * dte
        mask = _iota((l, l), 1) <= _iota((l, l), 0)
        lane = _iota((l, LANES), 1)
        prev_ref[0] = state_ref[...]
        for g in range(GROUPS):
            gsl = slice(g * gw, (g + 1) * gw)
            bg = bm_ref[:, g * D_STATE:(g + 1) * D_STATE].astype(BF16)
            cg = cm_ref[:, g * D_STATE:(g + 1) * D_STATE].astype(BF16)
            cb = _dot(cg, bg, NT)
            outs = []
            for q in range(pairs_per_group):
                pair = g * pairs_per_group + q
                h0, h1 = 2 * pair, 2 * pair + 1
                xpb = xdt[:, pair * LANES:(pair + 1) * LANES].astype(BF16)
                m0 = (cb * _decay_mats(acs, acs_t, h0, mask)).astype(BF16)
                m1 = (cb * _decay_mats(acs, acs_t, h1, mask)).astype(BF16)
                zero = jnp.zeros_like(xpb)
                rhs = jnp.concatenate([jnp.where(lane < HEAD_DIM, xpb, zero),
                                       jnp.where(lane >= HEAD_DIM, xpb, zero)], axis=0)
                outs.append(_dot(jnp.concatenate([m0, m1], axis=1), rhs))
            prev_g = state_ref[:, gsl]
            y_off = _dot(cg, prev_g.astype(BF16)) * e_full[:, gsl]
            y_ref[:, gsl] = jnp.concatenate(outs, axis=1) + y_off
            st = _dot(bg, xd[:, gsl].astype(BF16), TN)
            state_ref[:, gsl] = prev_g * cd[:, gsl] + st

        zv = z_ref[...]
        ya2 = (y_ref[...] + xs * dsk_ref[...]) * (zv * _sigmoid(zv))
        for g in range(GROUPS):
            gsl = slice(g * gw, (g + 1) * gw)
            blk = ya2[:, gsl]
            rg = lax.rsqrt(jnp.mean(blk * blk, axis=-1, keepdims=True) + EPS)
            ya3_ref[:, gsl] = (blk * rg * gn_ref[:, gsl]).astype(ya3_ref.dtype)

    row = lambda b, c: b * nc + c
    small = lambda n: pl.BlockSpec((1, n), lambda b, c: (0, 0))
    return _pallas(
        body, name="ssd_fwd", grid=(nb, nc),
        in_specs=[pl.BlockSpec((l, d_inner), lambda b, c: (row(b, c), 0)),
                  pl.BlockSpec((l, gn_cols), lambda b, c: (row(b, c), d_inner // gn_cols)),
                  pl.BlockSpec((l, gn_cols), lambda b, c: (row(b, c), d_inner // gn_cols + 1)),
                  pl.BlockSpec((l, d_inner), lambda b, c: (row(b, c), 0)),
                  pl.BlockSpec((l, HEAD_PAD), lambda b, c: (row(b, c), 0)),
                  small(HEAD_PAD), small(HEAD_PAD), small(d_inner), small(d_inner)],
        out_specs=[pl.BlockSpec((l, d_inner), lambda b, c: (row(b, c), 0)),
                   pl.BlockSpec((l, d_inner), lambda b, c: (row(b, c), 0)),
                   pl.BlockSpec((1, D_STATE, d_inner), lambda b, c: (row(b, c), 0, 0))],
        out_shape=[jax.ShapeDtypeStruct((nb * s, d_inner), F32),
                   jax.ShapeDtypeStruct((nb * s, d_inner), BF16),
                   jax.ShapeDtypeStruct((nb * nc, D_STATE, d_inner), F32)],
        scratch_shapes=[pltpu.VMEM((D_STATE, d_inner), F32)],
        compiler_params=_cparams(("parallel", "arbitrary"), 48),
    )(xbc, xbc, xbc, proj, dtraw, dt_bias, a_log, dskip_full, gnorm)


def _pool_fwd(proj, nb, s, off_u, d_pool):
    pg = d_pool // len(POOL_WINDOWS)

    def body(u_ref, o_ref):
        g = pl.program_id(1)
        uv = u_ref[...]
        row = _iota(uv.shape, 0)
        pos = (_iota((uv.shape[0], 1), 0) + 1).astype(F32)
        for gi, w in enumerate(POOL_WINDOWS):
            @pl.when(g == gi)
            def _():
                acc, span = uv, 1
                while span < w:
                    acc = acc + _shift_down(acc, span, row)
                    span *= 2
                o_ref[...] = (acc / jnp.minimum(pos, float(w)) - uv).astype(o_ref.dtype)

    return _pallas(
        body, name="pool_fwd", grid=(nb, len(POOL_WINDOWS)),
        in_specs=[pl.BlockSpec((s, pg), lambda b, g: (b, off_u // pg + g))],
        out_specs=pl.BlockSpec((s, pg), lambda b, g: (b, g)),
        out_shape=jax.ShapeDtypeStruct((nb * s, d_pool), BF16),
        compiler_params=_cparams(("parallel", "parallel"), 48),
    )(proj)


def _mix_fwd(pooled, mixw, proj, mix_b, scale, off_zp):
    t, d_pool = pooled.shape
    ng, pg = mixw.shape[0], mixw.shape[1]
    tm = 512

    def body(a_ref, w_ref, zp_ref, b_ref, s_ref, mixed_ref, yb_ref):
        mixed = _dot(a_ref[...], w_ref[0])
        mixed_ref[...] = mixed
        zp = zp_ref[...]
        yb_ref[...] = ((mixed + b_ref[...]) * s_ref[...] * (zp * _sigmoid(zp))).astype(yb_ref.dtype)

    blk = lambda off: pl.BlockSpec((tm, pg), lambda i, g: (i, off + g))
    vec = pl.BlockSpec((1, pg), lambda i, g: (0, g))
    return _pallas(
        body, name="mix_fwd", grid=(t // tm, ng),
        in_specs=[blk(0), pl.BlockSpec((1, pg, pg), lambda i, g: (g, 0, 0)), blk(off_zp // pg), vec, vec],
        out_specs=[blk(0), blk(0)],
        out_shape=[jax.ShapeDtypeStruct((t, d_pool), F32), jax.ShapeDtypeStruct((t, d_pool), BF16)],
        compiler_params=_cparams(("parallel", "parallel")),
    )(pooled, mixw, proj, mix_b, scale)


def _merge_fwd(ya3, wa, yb, wb, proj, off_ga, off_gb):
    t = ya3.shape[0]
    d = wa.shape[1]
    tm, tn = 256, 512

    def body(a_ref, wa_ref, b_ref, wb_ref, ga_ref, gb_ref, pa_ref, pb_ref, mg_ref):
        pa = _dot(a_ref[...], wa_ref[...])
        pb = _dot(b_ref[...], wb_ref[...])
        pa_ref[...] = pa
        pb_ref[...] = pb
        mg_ref[...] = (_sigmoid(ga_ref[...]) * pa + _sigmoid(gb_ref[...]) * pb).astype(mg_ref.dtype)

    o = pl.BlockSpec((tm, tn), lambda j, i: (i, j))
    return _pallas(
        body, name="merge_fwd", grid=(d // tn, t // tm),
        in_specs=[pl.BlockSpec((tm, ya3.shape[1]), lambda j, i: (i, 0)),
                  pl.BlockSpec((wa.shape[0], tn), lambda j, i: (0, j)),
                  pl.BlockSpec((tm, yb.shape[1]), lambda j, i: (i, 0)),
                  pl.BlockSpec((wb.shape[0], tn), lambda j, i: (0, j)),
                  pl.BlockSpec((tm, tn), lambda j, i: (i, off_ga // tn + j)),
                  pl.BlockSpec((tm, tn), lambda j, i: (i, off_gb // tn + j))],
        out_specs=[o, o, o],
        out_shape=[jax.ShapeDtypeStruct((t, d), F32), jax.ShapeDtypeStruct((t, d), F32),
                   jax.ShapeDtypeStruct((t, d), BF16)],
        compiler_params=_cparams(("parallel", "parallel"), 48),
    )(ya3, wa, yb, wb, proj, proj)


def _out_fwd(merged, wout, x, ple_g):
    t, d = x.shape
    tm = 256

    def body(m_ref, w_ref, x_ref, g_ref, x1_ref, h2_ref):
        x1 = x_ref[...] + _dot(m_ref[...], w_ref[...])
        x1_ref[...] = x1
        r = lax.rsqrt(jnp.mean(x1 * x1, axis=-1, keepdims=True) + EPS)
        h2_ref[...] = (x1 * r * g_ref[...]).astype(h2_ref.dtype)

    rowblk = pl.BlockSpec((tm, d), lambda i: (i, 0))
    return _pallas(
        body, name="out_fwd", grid=(t // tm,),
        in_specs=[rowblk, pl.BlockSpec((d, d), lambda i: (0, 0)), rowblk, pl.BlockSpec((1, d), lambda i: (0, 0))],
        out_specs=[rowblk, rowblk],
        out_shape=[jax.ShapeDtypeStruct((t, d), F32), jax.ShapeDtypeStruct((t, d), BF16)],
        compiler_params=_cparams(("parallel",), 48),
    )(merged, wout, x, ple_g)


def _final(h2, wg, p, wup, x1, target, final_g):
    t, d = x1.shape
    pd = p.shape[1]
    tm = 256

    def body(h2_ref, wg_ref, p_ref, wup_ref, x1_ref, tg_ref, g_ref,
             dx2_ref, dgl_ref, dpu_ref, loss_ref, dfg_ref):
        @pl.when(pl.program_id(0) == 0)
        def _():
            loss_ref[...] = jnp.zeros_like(loss_ref)
            dfg_ref[...] = jnp.zeros_like(dfg_ref)

        gate = _sigmoid(_dot(h2_ref[...], wg_ref[...]))
        pu = _dot(p_ref[...].astype(BF16), wup_ref[...])
        x2 = x1_ref[...] + gate * pu
        r = lax.rsqrt(jnp.mean(x2 * x2, axis=-1, keepdims=True) + EPS)
        xh = x2 * r
        gfin = g_ref[...]
        err = xh * gfin - tg_ref[...]
        loss_ref[...] += _sum_all(err * err) * (0.5 / d)
        dy = err * (1.0 / d)
        dfg_ref[...] += jnp.sum(dy * xh, axis=0, keepdims=True)
        dg = dy * gfin
        dx2 = r * (dg - xh * jnp.mean(dg * xh, axis=-1, keepdims=True))
        dx2_ref[...] = dx2
        dgl_ref[...] = (dx2 * pu * gate * (1.0 - gate)).astype(dgl_ref.dtype)
        dpu_ref[...] = (dx2 * gate).astype(dpu_ref.dtype)

    rowblk = pl.BlockSpec((tm, d), lambda i: (i, 0))
    full = lambda a, b: pl.BlockSpec((a, b), lambda i: (0, 0))
    return _pallas(
        body, name="final_fwd_bwd", grid=(t // tm,),
        in_specs=[rowblk, full(d, d), pl.BlockSpec((tm, pd), lambda i: (i, 0)), full(pd, d), rowblk, rowblk,
                  full(1, d)],
        out_specs=[rowblk, rowblk, rowblk, full(1, LANES), full(1, d)],
        out_shape=[jax.ShapeDtypeStruct((t, d), F32), jax.ShapeDtypeStruct((t, d), BF16),
                   jax.ShapeDtypeStruct((t, d), BF16), jax.ShapeDtypeStruct((1, LANES), F32),
                   jax.ShapeDtypeStruct((1, d), F32)],
        compiler_params=_cparams(("arbitrary",), 48),
    )(h2, wg, p, wup, x1, target, final_g)


def _ple_bwd(dgl, wg, dx2, x1, ple_g):
    t, d = x1.shape
    tm = 256

    def body(dgl_ref, wg_ref, dx2_ref, x1_ref, g_ref, dx1_ref, dx1b_ref, dg_ref):
        @pl.when(pl.program_id(0) == 0)
        def _():
            dg_ref[...] = jnp.zeros_like(dg_ref)

        dh2 = _dot(dgl_ref[...], wg_ref[...], NT)
        x1 = x1_ref[...]
        r = lax.rsqrt(jnp.mean(x1 * x1, axis=-1, keepdims=True) + EPS)
        xh = x1 * r
        dg_ref[...] += jnp.sum(dh2 * xh, axis=0, keepdims=True)
        dg = dh2 * g_ref[...]
        dx1 = dx2_ref[...] + r * (dg - xh * jnp.mean(dg * xh, axis=-1, keepdims=True))
        dx1_ref[...] = dx1
        dx1b_ref[...] = dx1.astype(dx1b_ref.dtype)

    rowblk = pl.BlockSpec((tm, d), lambda i: (i, 0))
    full = lambda a, b: pl.BlockSpec((a, b), lambda i: (0, 0))
    return _pallas(
        body, name="ple_bwd", grid=(t // tm,),
        in_specs=[rowblk, full(d, d), rowblk, rowblk, full(1, d)],
        out_specs=[rowblk, rowblk, full(1, d)],
        out_shape=[jax.ShapeDtypeStruct((t, d), F32), jax.ShapeDtypeStruct((t, d), BF16),
                   jax.ShapeDtypeStruct((1, d), F32)],
        compiler_params=_cparams(("arbitrary",), 48),
    )(dgl, wg, dx2, x1, ple_g)


def _merge_bwd(dx1b, wout, proj, pa, pb, n_main, off_ga, off_gb):
    t, d = pa.shape
    tm = 256

    def body(dx_ref, w_ref, ga_ref, gb_ref, pa_ref, pb_ref, dpa_ref, dpb_ref, dgate_ref):
        dm = _dot(dx_ref[...], w_ref[...], NT)
        sa = _sigmoid(ga_ref[...])
        sb = _sigmoid(gb_ref[...])
        dpa_ref[...] = (dm * sa).astype(dpa_ref.dtype)
        dpb_ref[...] = (dm * sb).astype(dpb_ref.dtype)
        dgate_ref[:, :d] = (dm * pa_ref[...] * sa * (1.0 - sa)).astype(dgate_ref.dtype)
        dgate_ref[:, d:] = (dm * pb_ref[...] * sb * (1.0 - sb)).astype(dgate_ref.dtype)

    assert off_gb == off_ga + d and off_ga % (2 * d) == 0
    rowblk = pl.BlockSpec((tm, d), lambda i: (i, 0))
    return _pallas(
        body, name="merge_bwd", grid=(t // tm,),
        in_specs=[rowblk, pl.BlockSpec((d, d), lambda i: (0, 0)),
                  pl.BlockSpec((tm, d), lambda i: (i, off_ga // d)),
                  pl.BlockSpec((tm, d), lambda i: (i, off_gb // d)), rowblk, rowblk],
        out_specs=[rowblk, rowblk, pl.BlockSpec((tm, 2 * d), lambda i: (i, off_ga // (2 * d)))],
        out_shape=[jax.ShapeDtypeStruct((t, d), BF16), jax.ShapeDtypeStruct((t, d), BF16),
                   jax.ShapeDtypeStruct((t, n_main), BF16)],
        compiler_params=_cparams(("parallel",), 48),
    )(dx1b, wout, proj, proj, pa, pb)


def _yb_bwd(dproj, dpb, wb, proj, mixed, mix_b, scale, off_zp):
    t, dp = mixed.shape
    tm = 256

    def body(_, dpb_ref, w_ref, zp_ref, mx_ref, b_ref, s_ref, dzp_ref, dmx_ref, dsc_ref, db_ref):
        @pl.when(pl.program_id(0) == 0)
        def _():
            dsc_ref[...] = jnp.zeros_like(dsc_ref)
            db_ref[...] = jnp.zeros_like(db_ref)

        dyb = _dot(dpb_ref[...], w_ref[...], NT)
        zp = zp_ref[...]
        sg = _sigmoid(zp)
        mb = mx_ref[...] + b_ref[...]
        yb0 = mb * s_ref[...]
        dyb0 = dyb * (zp * sg)
        dzp_ref[...] = (dyb * yb0 * (sg * (1.0 + zp * (1.0 - sg)))).astype(dzp_ref.dtype)
        dsc_ref[...] += jnp.sum(dyb0 * mb, axis=0, keepdims=True)
        dmx = dyb0 * s_ref[...]
        db_ref[...] += jnp.sum(dmx, axis=0, keepdims=True)
        dmx_ref[...] = dmx.astype(dmx_ref.dtype)

    rowblk = pl.BlockSpec((tm, dp), lambda i: (i, 0))
    vec = pl.BlockSpec((1, dp), lambda i: (0, 0))
    seg = pl.BlockSpec((tm, dp), lambda i: (i, off_zp // dp))
    return _pallas(
        body, name="yb_bwd", grid=(t // tm,),
        in_specs=[ANY, pl.BlockSpec((tm, wb.shape[1]), lambda i: (i, 0)),
                  pl.BlockSpec(wb.shape, lambda i: (0, 0)), seg, rowblk, vec, vec],
        out_specs=[seg, rowblk, vec, vec],
        out_shape=[jax.ShapeDtypeStruct(dproj.shape, dproj.dtype), jax.ShapeDtypeStruct((t, dp), BF16),
                   jax.ShapeDtypeStruct((1, dp), F32), jax.ShapeDtypeStruct((1, dp), F32)],
        input_output_aliases={0: 0},
        compiler_params=_cparams(("arbitrary",), 48),
    )(dproj, dpb, wb, proj, mixed, mix_b, scale)


def _dpooled(dmixed, mixw):
    t, dp = dmixed.shape
    ng, pg = mixw.shape[0], mixw.shape[1]
    tm = 512

    def body(a_ref, w_ref, o_ref):
        o_ref[...] = _dot(a_ref[...], w_ref[0], NT)

    blk = pl.BlockSpec((tm, pg), lambda i, g: (i, g))
    return _pallas(
        body, name="dpooled", grid=(t // tm, ng),
        in_specs=[blk, pl.BlockSpec((1, pg, pg), lambda i, g: (g, 0, 0))],
        out_specs=blk, out_shape=jax.ShapeDtypeStruct((t, dp), F32),
        compiler_params=_cparams(("parallel", "parallel")),
    )(dmixed, mixw)


def _dmixw(pooled, dmixed, ng):
    t, dp = pooled.shape
    pg = dp // ng
    tk = 512
    nk = t // tk

    def body(a_ref, b_ref, o_ref):
        @pl.when(pl.program_id(1) == 0)
        def _():
            o_ref[...] = jnp.zeros_like(o_ref)

        o_ref[0] += _dot(a_ref[...], b_ref[...], TN)

    blk = pl.BlockSpec((tk, pg), lambda g, k: (k, g))
    return _pallas(
        body, name="dmixw", grid=(ng, nk),
        in_specs=[blk, blk],
        out_specs=pl.BlockSpec((1, pg, pg), lambda g, k: (g, 0, 0)),
        out_shape=jax.ShapeDtypeStruct((ng, pg, pg), F32),
        compiler_params=_cparams(("parallel", "arbitrary")),
    )(pooled, dmixed)


def _pool_bwd(dproj, dpooled, nb, s, off_u):
    dp = dpooled.shape[1]
    pg = dp // len(POOL_WINDOWS)

    def body(_, d_ref, o_ref):
        g = pl.program_id(1)
        dv = d_ref[...]
        row = _iota(dv.shape, 0)
        pos = (_iota((dv.shape[0], 1), 0) + 1).astype(F32)
        for gi, w in enumerate(POOL_WINDOWS):
            @pl.when(g == gi)
            def _():
                acc, span = dv / jnp.minimum(pos, float(w)), 1
                while span < w:
                    acc = acc + _shift_up(acc, span, row)
                    span *= 2
                o_ref[...] = (acc - dv).astype(o_ref.dtype)

    return _pallas(
        body, name="pool_bwd", grid=(nb, len(POOL_WINDOWS)),
        in_specs=[ANY, pl.BlockSpec((s, pg), lambda b, g: (b, g))],
        out_specs=pl.BlockSpec((s, pg), lambda b, g: (b, off_u // pg + g)),
        out_shape=jax.ShapeDtypeStruct(dproj.shape, dproj.dtype),
        input_output_aliases={0: 0},
        compiler_params=_cparams(("parallel", "parallel"), 48),
    )(dproj, dpooled)


def _ssd_bwd(dproj, dya3, y, xbc, proj, dtraw, prev, dt_bias, a_log, dskip_full, gnorm, nb, s, d_inner):
    l = CHUNK
    nc = s // l
    n_heads = d_inner // HEAD_DIM
    gw = d_inner // GROUPS
    pairs_per_group = gw // LANES
    gn_cols = GROUPS * D_STATE
    n_conv = d_inner + 2 * gn_cols
    last_step = nb * nc - 1

    def body(_, dya3_ref, y_ref, xs_ref, bm_ref, cm_ref, z_ref, dtr_ref, prev_ref, bias_ref, alog_ref,
             dsk_ref, gn_ref,
             dz_ref, dxbc_ref, ddt_ref, dgn_ref, dskh_ref, dalog_ref, dbias_ref, g_ref, dskacc_ref):
        step = pl.program_id(0) * nc + pl.program_id(1)

        @pl.when(pl.program_id(1) == 0)
        def _():
            g_ref[...] = jnp.zeros_like(g_ref)

        @pl.when(step == 0)
        def _():
            dgn_ref[...] = jnp.zeros_like(dgn_ref)
            dalog_ref[...] = jnp.zeros_like(dalog_ref)
            dbias_ref[...] = jnp.zeros_like(dbias_ref)
            dskacc_ref[...] = jnp.zeros_like(dskacc_ref)

        xs = xs_ref[...]
        zv = z_ref[...]
        dsk = dsk_ref[...]
        sg = _sigmoid(zv)
        sz = zv * sg
        ya1 = y_ref[...] + xs * dsk
        ya2 = ya1 * sz
        parts = []
        for g in range(GROUPS):
            gsl = slice(g * gw, (g + 1) * gw)
            blk = ya2[:, gsl]
            rg = lax.rsqrt(jnp.mean(blk * blk, axis=-1, keepdims=True) + EPS)
            xh = blk * rg
            d3 = dya3_ref[:, gsl]
            dgn_ref[:, gsl] += jnp.sum(d3 * xh, axis=0, keepdims=True)
            dg = d3 * gn_ref[:, gsl]
            parts.append(rg * (dg - xh * jnp.mean(dg * xh, axis=-1, keepdims=True)))
        dya2 = jnp.concatenate(parts, axis=1)
        dy = dya2 * sz
        dz_ref[...] = (dya2 * ya1 * (sg * (1.0 + zv * (1.0 - sg)))).astype(dz_ref.dtype)
        dskacc_ref[...] += jnp.sum(dy * xs, axis=0, keepdims=True)

        dtraw_v = dtr_ref[...]
        bias = bias_ref[...]
        dt, apad, acs, dt_full, acs_full, tri, ex = _ssd_common(dtraw_v, bias, alog_ref[...], n_heads)
        acs_t = acs.T
        e_full = jnp.exp(acs_full)
        last = acs_full[l - 1:l, :]
        dte = jnp.exp(last - acs_full)
        cd = jnp.exp(last)
        xdt = xs * dt_full
        xd = xdt * dte
        dy_e = dy * e_full
        mask = _iota((l, l), 1) <= _iota((l, l), 0)
        lane = _iota((l, LANES), 1)
        hlane = _iota((l, HEAD_PAD), 1)
        hsub = _iota((HEAD_PAD, l), 0)
        dacs = jnp.zeros((l, HEAD_PAD), F32)
        dacs_t = jnp.zeros((HEAD_PAD, l), F32)
        dacs_full, dlast, dxdt_all, db_all, dc_all = [], [], [], [], []
        for g in range(GROUPS):
            gsl = slice(g * gw, (g + 1) * gw)
            bg = bm_ref[:, g * D_STATE:(g + 1) * D_STATE].astype(BF16)
            cg = cm_ref[:, g * D_STATE:(g + 1) * D_STATE].astype(BF16)
            prev_g = prev_ref[0, :, gsl]
            prev_b = prev_g.astype(BF16)
            gst = g_ref[:, gsl]
            gst_b = gst.astype(BF16)
            dye_b = dy_e[:, gsl].astype(BF16)
            dc = _dot(dye_b, prev_b, NT)
            dprev = _dot(cg, dye_b, TN)
            y_off = _dot(cg, prev_b) * e_full[:, gsl]
            xd_g = xd[:, gsl]
            db = _dot(xd_g.astype(BF16), gst_b, NT)
            dxd = _dot(bg, gst_b)
            wgt = dxd * xd_g
            dacs_full.append(dy[:, gsl] * y_off - wgt)
            dlast.append(jnp.sum(wgt, axis=0, keepdims=True)
                         + jnp.sum(prev_g * gst, axis=0, keepdims=True) * cd[:, gsl])
            g_ref[:, gsl] = dprev + cd[:, gsl] * gst
            dxdt_g = dxd * dte[:, gsl]
            cb = _dot(cg, bg, NT)
            dcb = jnp.zeros((l, l), F32)
            for q in range(pairs_per_group):
                pair = g * pairs_per_group + q
                psl = slice(pair * LANES, (pair + 1) * LANES)
                xpb = xdt[:, psl].astype(BF16)
                dyp = dy[:, psl].astype(BF16)
                zero = jnp.zeros_like(dyp)
                halves = (jnp.where(lane < HEAD_DIM, dyp, zero), jnp.where(lane >= HEAD_DIM, dyp, zero))
                acc = dxdt_g[:, q * LANES:(q + 1) * LANES]
                for k, dyh in enumerate(halves):
                    h = 2 * pair + k
                    dk = _decay_mats(acs, acs_t, h, mask)
                    mm = cb * dk
                    dm = _dot(dyh, xpb, NT)
                    dcb = dcb + dm * dk
                    dseg = dm * mm
                    dacs = dacs + jnp.where(hlane == h, jnp.sum(dseg, axis=1, keepdims=True), 0.0)
                    dacs_t = dacs_t + jnp.where(hsub == h, jnp.sum(dseg, axis=0, keepdims=True), 0.0)
                    acc = acc + _dot(mm.astype(BF16), dyh, TN)
                dxdt_all.append(acc)
            dcb_b = dcb.astype(BF16)
            dc_all.append(dc + _dot(dcb_b, bg))
            db_all.append(db + _dot(dcb_b, cg, TN))

        dxdt = jnp.concatenate(dxdt_all, axis=1)
        dxbc_ref[:, :d_inner] = dy * dsk + dxdt * dt_full
        dxbc_ref[:, d_inner:d_inner + gn_cols] = jnp.concatenate(db_all, axis=1)
        dxbc_ref[:, d_inner + gn_cols:] = jnp.concatenate(dc_all, axis=1)
        ddt = _exact_r(dxdt * xs, ex, NT)
        rowi = _iota((l, d_inner), 0)
        dacs_f = jnp.concatenate(dacs_full, axis=1) + jnp.where(rowi == l - 1, jnp.concatenate(dlast, axis=1), 0.0)
        dacs = dacs + _exact_r(dacs_f, ex, NT) - dacs_t.T
        da = _exact_l(tri, dacs, TN)
        ddt = ddt + da * apad
        dalog_ref[...] += jnp.sum(da * dt, axis=0, keepdims=True) * apad
        ddt_raw = ddt * _sigmoid(dtraw_v + bias)
        dbias_ref[...] += jnp.sum(ddt_raw, axis=0, keepdims=True)
        ddt_ref[...] = ddt_raw.astype(ddt_ref.dtype)

        @pl.when(step == last_step)
        def _():
            dskh_ref[...] = _exact_r(dskacc_ref[...], ex, NT)[0:1, :]

    row = lambda b, c: b * nc + (nc - 1 - c)
    small = lambda n: pl.BlockSpec((1, n), lambda b, c: (0, 0))
    wide = pl.BlockSpec((l, d_inner), lambda b, c: (row(b, c), 0))
    return _pallas(
        body, name="ssd_bwd", grid=(nb, nc),
        in_specs=[ANY, wide, wide, wide,
                  pl.BlockSpec((l, gn_cols), lambda b, c: (row(b, c), d_inner // gn_cols)),
                  pl.BlockSpec((l, gn_cols), lambda b, c: (row(b, c), d_inner // gn_cols + 1)),
                  wide, pl.BlockSpec((l, HEAD_PAD), lambda b, c: (row(b, c), 0)),
                  pl.BlockSpec((1, D_STATE, d_inner), lambda b, c: (row(b, c), 0, 0)),
                  small(HEAD_PAD), small(HEAD_PAD), small(d_inner), small(d_inner)],
        out_specs=[wide, pl.BlockSpec((l, n_conv), lambda b, c: (row(b, c), 0)),
                   pl.BlockSpec((l, HEAD_PAD), lambda b, c: (row(b, c), 0)),
                   small(d_inner), small(HEAD_PAD), small(HEAD_PAD), small(HEAD_PAD)],
        out_shape=[jax.ShapeDtypeStruct(dproj.shape, dproj.dtype),
                   jax.ShapeDtypeStruct((nb * s, n_conv), F32),
                   jax.ShapeDtypeStruct((nb * s, HEAD_PAD), BF16),
                   jax.ShapeDtypeStruct((1, d_inner), F32), jax.ShapeDtypeStruct((1, HEAD_PAD), F32),
                   jax.ShapeDtypeStruct((1, HEAD_PAD), F32), jax.ShapeDtypeStruct((1, HEAD_PAD), F32)],
        scratch_shapes=[pltpu.VMEM((D_STATE, d_inner), F32), pltpu.VMEM((8, d_inner), F32)],
        input_output_aliases={0: 0},
        compiler_params=_cparams(("arbitrary", "arbitrary"), 56),
    )(dproj, dya3, y, xbc, xbc, xbc, proj, dtraw, prev, dt_bias, a_log, dskip_full, gnorm)


def _conv_bwd(dproj, dxbc, proj, conv_w, conv_b, nb, s, off_xbc):
    n_conv = dxbc.shape[1]
    cw = 256

    def body(_, d_ref, x_ref, w_ref, b_ref, dx_ref, dw_ref, db_ref):
        @pl.when(pl.program_id(1) == 0)
        def _():
            dw_ref[...] = jnp.zeros_like(dw_ref)
            db_ref[...] = jnp.zeros_like(db_ref)

        xv = x_ref[...]
        row = _iota(xv.shape, 0)
        shifted = [xv] + [_shift_down(xv, j, row) for j in range(1, CONV_W)]
        acc = b_ref[...] + shifted[0] * w_ref[CONV_W - 1:CONV_W, :]
        for j in range(1, CONV_W):
            acc = acc + shifted[j] * w_ref[CONV_W - 1 - j:CONV_W - j, :]
        sg = _sigmoid(acc)
        dc = d_ref[...] * (sg * (1.0 + acc * (1.0 - sg)))
        db_ref[...] += jnp.sum(dc, axis=0, keepdims=True)
        dxv = dc * w_ref[CONV_W - 1:CONV_W, :]
        for j in range(CONV_W):
            dw_ref[CONV_W - 1 - j:CONV_W - j, :] += jnp.sum(dc * shifted[j], axis=0, keepdims=True)
            if j:
                dxv = dxv + _shift_up(dc, j, row) * w_ref[CONV_W - 1 - j:CONV_W - j, :]
        dx_ref[...] = dxv.astype(dx_ref.dtype)

    return _pallas(
        body, name="conv_bwd", grid=(n_conv // cw, nb),
        in_specs=[ANY, pl.BlockSpec((s, cw), lambda j, b: (b, j)),
                  pl.BlockSpec((s, cw), lambda j, b: (b, off_xbc // cw + j)),
                  pl.BlockSpec((CONV_W, cw), lambda j, b: (0, j)), pl.BlockSpec((1, cw), lambda j, b: (0, j))],
        out_specs=[pl.BlockSpec((s, cw), lambda j, b: (b, off_xbc // cw + j)),
                   pl.BlockSpec((CONV_W, cw), lambda j, b: (0, j)), pl.BlockSpec((1, cw), lambda j, b: (0, j))],
        out_shape=[jax.ShapeDtypeStruct(dproj.shape, dproj.dtype),
                   jax.ShapeDtypeStruct((CONV_W, n_conv), F32), jax.ShapeDtypeStruct((1, n_conv), F32)],
        input_output_aliases={0: 0},
        compiler_params=_cparams(("parallel", "arbitrary"), 48),
    )(dproj, dxbc, proj, conv_w, conv_b)


def _in_bwd(dproj, w_main, ddt, w_dt, x, norm_g, dx1):
    t, d = x.shape
    n_main = dproj.shape[1]
    tm, tk = min(1024, t), 512
    nk = n_main // tk

    def body(dp_ref, w_ref, ddt_ref, wdt_ref, x_ref, g_ref, dx1_ref, dx_ref, dg_ref, acc_ref):
        i, kk = pl.program_id(0), pl.program_id(1)

        @pl.when(kk == 0)
        def _():
            acc_ref[...] = jnp.zeros_like(acc_ref)

        @pl.when((kk == 0) & (i == 0))
        def _():
            dg_ref[...] = jnp.zeros_like(dg_ref)

        acc_ref[...] += _dot(dp_ref[...], w_ref[...], NT)

        @pl.when(kk == nk - 1)
        def _():
            dh = acc_ref[...] + _dot(ddt_ref[...], wdt_ref[...], NT)
            xv = x_ref[...]
            r = lax.rsqrt(jnp.mean(xv * xv, axis=-1, keepdims=True) + EPS)
            xh = xv * r
            dg_ref[...] += jnp.sum(dh * xh, axis=0, keepdims=True)
            dg = dh * g_ref[...]
            dx_ref[...] = dx1_ref[...] + r * (dg - xh * jnp.mean(dg * xh, axis=-1, keepdims=True))

    rowblk = pl.BlockSpec((tm, d), lambda i, k: (i, 0))
    vec = pl.BlockSpec((1, d), lambda i, k: (0, 0))
    return _pallas(
        body, name="in_bwd", grid=(t // tm, nk),
        in_specs=[pl.BlockSpec((tm, tk), lambda i, k: (i, k)), pl.BlockSpec((d, tk), lambda i, k: (0, k)),
                  pl.BlockSpec((tm, HEAD_PAD), lambda i, k: (i, 0)), pl.BlockSpec((d, HEAD_PAD), lambda i, k: (0, 0)),
                  rowblk, vec, rowblk],
        out_specs=[rowblk, vec],
        out_shape=[jax.ShapeDtypeStruct((t, d), F32), jax.ShapeDtypeStruct((1, d), F32)],
        scratch_shapes=[pltpu.VMEM((tm, d), F32)],
        compiler_params=_cparams(("arbitrary", "arbitrary"), 56),
    )(dproj, w_main, ddt, w_dt, x, norm_g, dx1)


def _adamw_math(w, g, m, v):
    m = ADAM_B1 * m + (1.0 - ADAM_B1) * g
    v = ADAM_B2 * v + (1.0 - ADAM_B2) * (g * g)
    m_hat = m / (1.0 - ADAM_B1 ** ADAM_STEP)
    v_hat = v / (1.0 - ADAM_B2 ** ADAM_STEP)
    delta = -ADAM_LR * (m_hat / (jnp.sqrt(v_hat) + ADAM_EPS) + ADAM_WD * w)
    return delta, m, v


def _adamw(parts, w, m, v, name):
    n, rows, cols = parts.shape
    tr = _row_tile(rows, 128) if rows % 8 == 0 else rows

    def body(p_ref, w_ref, m_ref, v_ref, g_ref, d_ref, mo_ref, vo_ref):
        g = p_ref[0].astype(F32)
        for k in range(1, n):
            g = g + p_ref[k].astype(F32)
        d, mn, vn = _adamw_math(w_ref[...], g, m_ref[...], v_ref[...])
        g_ref[...] = g
        d_ref[...] = d
        mo_ref[...] = mn
        vo_ref[...] = vn

    blk = pl.BlockSpec((tr, cols), lambda i: (i, 0))
    shp = jax.ShapeDtypeStruct((rows, cols), F32)
    return _pallas(
        body, name=name, grid=(rows // tr,),
        in_specs=[pl.BlockSpec((n, tr, cols), lambda i: (0, i, 0)), blk, blk, blk],
        out_specs=[blk, blk, blk, blk], out_shape=[shp, shp, shp, shp],
        compiler_params=_cparams(("parallel",), 48),
    )(parts, w, m, v)


def _coords():
    return lax.axis_index("x"), lax.axis_index("y"), lax.axis_index("c")


def _all_gather_two_level(blocks, name):
    n = len(blocks)

    def body(*refs):
        x_refs, out_refs = refs[:n], refs[n:2 * n]
        send_sems, recv_sems, local_sems = refs[2 * n:]
        x, y, c = _coords()
        me, sibling = (x, y, c), (x, y, 1 - c)
        chips = [(1 - x, y), (x, 1 - y), (1 - x, 1 - y)]

        def slot(a, px, py, pc):
            return out_refs[a].at[4 * px + 2 * py + pc]

        def copy(a, k, block, to, src=None):
            return pltpu.make_async_remote_copy(
                src_ref=slot(a, *block) if src is None else src, dst_ref=slot(a, *block),
                send_sem=send_sems.at[a, k], recv_sem=recv_sems.at[a, k], device_id=to, device_id_type=MESH)

        local = [pltpu.make_async_copy(x_refs[a], slot(a, *me), local_sems.at[a]) for a in range(n)]
        first = []
        for a in range(n):
            first.append(copy(a, 0, me, sibling, src=x_refs[a]))
            first += [copy(a, 1 + j, me, (*chip, c), src=x_refs[a]) for j, chip in enumerate(chips)]
        for cp in local + first:
            cp.start()
        passed = []
        for j, chip in enumerate(chips):
            for a in range(n):
                copy(a, 1 + j, (*chip, c), me).wait_recv()
                passed.append(copy(a, 4 + j, (*chip, c), sibling))
                passed[-1].start()
        for a in range(n):
            copy(a, 0, sibling, me).wait_recv()
            for j, chip in enumerate(chips):
                copy(a, 4 + j, (*chip, 1 - c), me).wait_recv()
        for cp in first + passed:
            cp.wait_send()
        for cp in local:
            cp.wait()

    return _pallas(
        body, name=name, in_specs=[ANY] * n, out_specs=[ANY] * n,
        out_shape=[jax.ShapeDtypeStruct((N_DEV,) + b.shape, b.dtype) for b in blocks],
        scratch_shapes=[pltpu.SemaphoreType.DMA((n, 7)), pltpu.SemaphoreType.DMA((n, 7)),
                        pltpu.SemaphoreType.DMA((n,))],
    )(*blocks)


def _all_gather_direct(blk, name):
    rows, cols = blk.shape

    def body(x_ref, out_ref, send_sems, recv_sems, local_sem):
        x, y, c = _coords()
        me = 4 * x + 2 * y + c
        mine = pltpu.make_async_copy(x_ref, out_ref.at[me], local_sem)
        mine.start()
        copies = []
        for k in range(1, N_DEV):
            fx, fy, fc = (k >> 2) & 1, (k >> 1) & 1, k & 1
            peer = (x ^ fx, y ^ fy, c ^ fc)
            copies.append(pltpu.make_async_remote_copy(
                src_ref=x_ref, dst_ref=out_ref.at[me], send_sem=send_sems.at[k - 1], recv_sem=recv_sems.at[k - 1],
                device_id=peer, device_id_type=MESH))
        for cp in copies:
            cp.start()
        for k in range(1, N_DEV):
            fx, fy, fc = (k >> 2) & 1, (k >> 1) & 1, k & 1
            src = 4 * (x ^ fx) + 2 * (y ^ fy) + (c ^ fc)
            pltpu.make_async_remote_copy(
                src_ref=x_ref, dst_ref=out_ref.at[src], send_sem=send_sems.at[k - 1], recv_sem=recv_sems.at[k - 1],
                device_id=(x, y, c), device_id_type=MESH).wait_recv()
        for cp in copies:
            cp.wait_send()
        mine.wait()

    return _pallas(
        body, name=name, in_specs=[ANY], out_specs=ANY,
        out_shape=jax.ShapeDtypeStruct((N_DEV, rows, cols), blk.dtype),
        scratch_shapes=[pltpu.SemaphoreType.DMA((7,)), pltpu.SemaphoreType.DMA((7,)), pltpu.SemaphoreType.DMA],
    )(blk)


def _pair_exchange(grads):
    n = len(grads)

    def body(*refs):
        g_refs, out_refs = refs[:n], refs[n:2 * n]
        send_sems, recv_sems = refs[2 * n:]
        x, y, c = _coords()
        copies = []
        for a in range(n):
            for k in range(4):
                copies.append(pltpu.make_async_remote_copy(
                    src_ref=g_refs[a].at[2 * k + (1 - c)], dst_ref=out_refs[a].at[k],
                    send_sem=send_sems.at[a, k], recv_sem=recv_sems.at[a, k],
                    device_id=(x, y, 1 - c), device_id_type=MESH))
        for cp in copies:
            cp.start()
        for cp in copies:
            cp.wait()

    return _pallas(
        body, name="rs_pair_exchange", in_specs=[ANY] * n, out_specs=[ANY] * n,
        out_shape=[jax.ShapeDtypeStruct((4,) + g.shape[1:], g.dtype) for g in grads],
        scratch_shapes=[pltpu.SemaphoreType.DMA((n, 4)), pltpu.SemaphoreType.DMA((n, 4))],
    )(*grads)


def _pair_add(grads, got, core, out_dtype, name):
    _, rows, cols = grads.shape
    tr = _row_tile(rows, 256) if rows % 8 == 0 else rows

    def body(c_ref, a_ref, b_ref, o_ref):
        o_ref[...] = (a_ref[...] + b_ref[...]).astype(o_ref.dtype)

    grid_spec = pltpu.PrefetchScalarGridSpec(
        num_scalar_prefetch=1, grid=(4, rows // tr),
        in_specs=[pl.BlockSpec((1, tr, cols), lambda k, i, c_ref: (2 * k + c_ref[0], i, 0)),
                  pl.BlockSpec((1, tr, cols), lambda k, i, c_ref: (k, i, 0))],
        out_specs=pl.BlockSpec((1, tr, cols), lambda k, i, c_ref: (k, i, 0)))
    return _pallas(
        body, name=name, grid_spec=grid_spec,
        out_shape=jax.ShapeDtypeStruct((4, rows, cols), out_dtype),
        compiler_params=_cparams(("parallel", "parallel"), 48),
    )(core, grads, got)


def _chip_exchange(partials):
    n = len(partials)

    def body(*refs):
        p_refs, out_refs = refs[:n], refs[n:2 * n]
        send_sems, recv_sems, local_sems = refs[2 * n:]
        x, y, c = _coords()
        mychip = 2 * x + y
        local = [pltpu.make_async_copy(p_refs[a].at[mychip], out_refs[a].at[mychip], local_sems.at[a])
                 for a in range(n)]
        copies = []
        for a in range(n):
            for k in range(1, 4):
                fx, fy = (k >> 1) & 1, k & 1
                px, py = x ^ fx, y ^ fy
                copies.append(pltpu.make_async_remote_copy(
                    src_ref=p_refs[a].at[2 * px + py], dst_ref=out_refs[a].at[mychip],
                    send_sem=send_sems.at[a, k - 1], recv_sem=recv_sems.at[a, k - 1],
                    device_id=(px, py, c), device_id_type=MESH))
        for cp in local + copies:
            cp.start()
        for a in range(n):
            for k in range(1, 4):
                fx, fy = (k >> 1) & 1, k & 1
                src = 2 * (x ^ fx) + (y ^ fy)
                pltpu.make_async_remote_copy(
                    src_ref=p_refs[a].at[src], dst_ref=out_refs[a].at[src],
                    send_sem=send_sems.at[a, k - 1], recv_sem=recv_sems.at[a, k - 1],
                    device_id=(x, y, c), device_id_type=MESH).wait_recv()
        for cp in copies:
            cp.wait_send()
        for cp in local:
            cp.wait()

    return _pallas(
        body, name="rs_chip_exchange", in_specs=[ANY] * n, out_specs=[ANY] * n,
        out_shape=[jax.ShapeDtypeStruct(p.shape, p.dtype) for p in partials],
        scratch_shapes=[pltpu.SemaphoreType.DMA((n, 3)), pltpu.SemaphoreType.DMA((n, 3)),
                        pltpu.SemaphoreType.DMA((n,))],
    )(*partials)


def _pad_lanes(v, n=HEAD_PAD):
    return jnp.pad(v, ((0, 0), (0, n - v.shape[1])))


def kernel(x, p, norm_g, w_in, conv_w, conv_b, dt_bias, a_log, d_skip, gnorm_g, pool_mix_w, pool_mix_b, pool_scale, w_branch_a, w_branch_b, w_out, ple_norm_g, w_ple_gate, w_ple_up, final_g, loss_target, m_norm_g, m_w_in, m_conv_w, m_conv_b, m_dt_bias, m_a_log, m_d_skip, m_gnorm_g, m_pool_mix_w, m_pool_mix_b, m_pool_scale, m_w_branch_a, m_w_branch_b, m_w_out, m_ple_norm_g, m_w_ple_gate, m_w_ple_up, m_final_g, v_norm_g, v_w_in, v_conv_w, v_conv_b, v_dt_bias, v_a_log, v_d_skip, v_gnorm_g, v_pool_mix_w, v_pool_mix_b, v_pool_scale, v_w_branch_a, v_w_branch_b, v_w_out, v_ple_norm_g, v_w_ple_gate, v_w_ple_up, v_final_g):
    nb, s, d = x.shape
    t = nb * s
    n_heads = dt_bias.shape[1]
    d_inner = n_heads * HEAD_DIM
    n_conv = conv_b.shape[1]
    d_pool = pool_mix_b.shape[1]
    n_in = w_in.shape[2] * N_DEV
    ple_dim = p.shape[-1]
    o_xbc, o_dt = d_inner, d_inner + n_conv
    o_u = o_dt + n_heads
    o_zp = o_u + d_pool
    o_ga, o_gb = o_zp + d_pool, o_zp + d_pool + d
    n_main = n_in - n_heads
    off_ga, off_gb, off_xbc = d_inner, d_inner + d, d_inner + 2 * d
    off_u = off_xbc + n_conv
    off_zp = off_u + d_pool

    sharded = [w_in[0], conv_w[0], pool_mix_w[0], w_branch_a[0], w_branch_b[0], w_out[0], w_ple_gate[0], w_ple_up[0]]
    sharded_m = [m_w_in[0], m_conv_w[0], m_pool_mix_w[0], m_w_branch_a[0], m_w_branch_b[0], m_w_out[0],
                 m_w_ple_gate[0], m_w_ple_up[0]]
    sharded_v = [v_w_in[0], v_conv_w[0], v_pool_mix_w[0], v_w_branch_a[0], v_w_branch_b[0], v_w_out[0],
                 v_w_ple_gate[0], v_w_ple_up[0]]
    ng = pool_mix_w.shape[1]
    pg = d_pool // ng
    two_d = lambda w: w.reshape(-1, w.shape[-1])
    sharded, sharded_m, sharded_v = ([two_d(w) for w in ws] for ws in (sharded, sharded_m, sharded_v))

    send = [w if i == 1 else w.astype(BF16) for i, w in enumerate(sharded)]
    seg = _all_gather_two_level(send, "weights_all_gather")
    w_full = seg[0].transpose(1, 0, 2).reshape(d, n_in)
    w_main = jnp.concatenate([w_full[:, :o_xbc], w_full[:, o_ga:], w_full[:, o_xbc:o_dt], w_full[:, o_u:o_ga]], axis=1)
    w_dt = _pad_lanes(w_full[:, o_dt:o_u])
    cw_full = seg[1].transpose(1, 0, 2).reshape(CONV_W, n_conv)
    mixw_full = seg[2].reshape(N_DEV, ng, pg // N_DEV, pg).transpose(1, 0, 2, 3).reshape(ng, pg, pg)
    wa_full = seg[3].reshape(d_inner, d)
    wb_full = seg[4].reshape(d_pool, d)
    wout_full = seg[5].reshape(d, d)
    wg_full = seg[6].reshape(d, d)
    wup_full = seg[7].transpose(1, 0, 2).reshape(ple_dim, d)

    x2 = x.reshape(t, d)
    p2 = p.reshape(t, ple_dim)
    tgt = loss_target.reshape(t, d)
    bias_p, alog_p = _pad_lanes(dt_bias), _pad_lanes(a_log)
    dskip_full = jnp.repeat(d_skip, HEAD_DIM, axis=1)

    h = _rms_fwd(x2, norm_g, "rms_in")
    proj = _matmul(h, w_main, "nn", F32, "in_proj", 1024, 512, d)
    dtraw = _matmul(h, w_dt, "nn", F32, "in_proj_dt", 1024, HEAD_PAD, d)
    xbc = _conv_fwd(proj, cw_full, conv_b, nb, s, off_xbc, n_conv)
    y, ya3, prev = _ssd_fwd(xbc, proj, dtraw, bias_p, alog_p, dskip_full, gnorm_g, nb, s, d_inner)
    pooled = _pool_fwd(proj, nb, s, off_u, d_pool)
    mixed, yb = _mix_fwd(pooled, mixw_full, proj, pool_mix_b, pool_scale, off_zp)
    pa, pb, merged = _merge_fwd(ya3, wa_full, yb, wb_full, proj, off_ga, off_gb)
    x1, h2 = _out_fwd(merged, wout_full, x2, ple_norm_g)
    dx2, dgl, dpu, loss_part, dfinal_g = _final(h2, wg_full, p2, wup_full, x1, tgt, final_g.reshape(1, d))

    g_wup = _matmul(p2, dpu, "tn", F32, "dw_ple_up", 256, 512, 512)
    g_wg = _matmul(h2, dgl, "tn", F32, "dw_ple_gate", 512, 512, 512)
    dx1, dx1b, d_ple_g = _ple_bwd(dgl, wg_full, dx2, x1, ple_norm_g)
    g_wout = _matmul(merged, dx1b, "tn", F32, "dw_out", 512, 512, 512)
    dpa, dpb, dproj = _merge_bwd(dx1b, wout_full, proj, pa, pb, n_main, off_ga, off_gb)
    g_wa = _matmul(ya3, dpa, "tn", F32, "dw_branch_a", 512, 512, 512)
    g_wb = _matmul(yb, dpb, "tn", F32, "dw_branch_b", 512, 512, 512)
    dya3 = _matmul(dpa, wa_full, "nt", F32, "d_ya3", 512, 512, d)
    dproj, dmixed, d_scale, d_mixb = _yb_bwd(dproj, dpb, wb_full, proj, mixed, pool_mix_b, pool_scale, off_zp)
    g_mixw = _dmixw(pooled, dmixed, ng)
    dpool = _dpooled(dmixed, mixw_full)
    dproj = _pool_bwd(dproj, dpool, nb, s, off_u)
    dproj, dxbc, ddt, d_gn, d_dsk, d_alog, d_dtb = _ssd_bwd(
        dproj, dya3, y, xbc, proj, dtraw, prev, bias_p, alog_p, dskip_full, gnorm_g, nb, s, d_inner)
    dproj, g_cw, d_cb = _conv_bwd(dproj, dxbc, proj, cw_full, conv_b, nb, s, off_xbc)
    g_wmain = _matmul(h, dproj, "tn", F32, "dw_in", 512, 512, 512)
    g_wdt = _matmul(h, ddt, "tn", F32, "dw_in_dt", 512, HEAD_PAD, 512)
    grad_x, d_ng = _in_bwd(dproj, w_main, ddt, w_dt, x2, norm_g, dx1)

    g_win = jnp.concatenate([g_wmain[:, :off_ga], g_wmain[:, off_xbc:off_u], g_wdt[:, :n_heads],
                             g_wmain[:, off_u:], g_wmain[:, off_ga:off_xbc]], axis=1)
    by_dest = [
        g_win.reshape(d, N_DEV, n_in // N_DEV).transpose(1, 0, 2),
        g_cw.reshape(CONV_W, N_DEV, n_conv // N_DEV).transpose(1, 0, 2),
        g_mixw.reshape(ng, N_DEV, pg // N_DEV, pg).transpose(1, 0, 2, 3).reshape(N_DEV, ng * pg // N_DEV, pg),
        g_wa.reshape(N_DEV, d_inner // N_DEV, d), g_wb.reshape(N_DEV, d_pool // N_DEV, d),
        g_wout.reshape(N_DEV, d // N_DEV, d), g_wg.reshape(N_DEV, d // N_DEV, d),
        g_wup.reshape(ple_dim, N_DEV, d // N_DEV).transpose(1, 0, 2),
    ]
    core = lax.axis_index("c").astype(I32).reshape(1)
    got = _pair_exchange(by_dest)
    names = ["w_in", "conv_w", "pool_mix_w", "w_branch_a", "w_branch_b", "w_out", "w_ple_gate", "w_ple_up"]
    partials = [_pair_add(g, r, core, F32, "rs_pair_add_" + nm) for g, r, nm in zip(by_dest, got, names)]
    parts = _chip_exchange(partials)
    sh_g, sh_d, sh_m, sh_v = [], [], [], []
    shapes = [w_in.shape, conv_w.shape, pool_mix_w.shape, w_branch_a.shape, w_branch_b.shape, w_out.shape,
              w_ple_gate.shape, w_ple_up.shape]
    for pt, w, m, v, nm, shp in zip(parts, sharded, sharded_m, sharded_v, names, shapes):
        res = _adamw(pt, w, m, v, "adamw_" + nm)
        for dst, r in zip((sh_g, sh_d, sh_m, sh_v), res):
            dst.append(r.reshape(shp))

    rep_w = [norm_g, conv_b, gnorm_g, pool_mix_b, pool_scale, ple_norm_g, final_g.reshape(1, d),
             bias_p, alog_p, _pad_lanes(d_skip)]
    rep_m = [m_norm_g, m_conv_b, m_gnorm_g, m_pool_mix_b, m_pool_scale, m_ple_norm_g, m_final_g.reshape(1, d),
             _pad_lanes(m_dt_bias), _pad_lanes(m_a_log), _pad_lanes(m_d_skip)]
    rep_v = [v_norm_g, v_conv_b, v_gnorm_g, v_pool_mix_b, v_pool_scale, v_ple_norm_g, v_final_g.reshape(1, d),
             _pad_lanes(v_dt_bias), _pad_lanes(v_a_log), _pad_lanes(v_d_skip)]
    rep_g = [d_ng, d_cb, d_gn, d_mixb, d_scale, d_ple_g, dfinal_g, d_dtb, d_alog, d_dsk]
    zero = jnp.zeros((1, LANES), F32)
    small = _all_gather_direct(jnp.concatenate(rep_g + [loss_part], axis=1), "small_all_gather")
    pk = lambda ws: jnp.concatenate(ws + [zero], axis=1)
    rg, rd, rm, rv = _adamw(small, pk(rep_w), pk(rep_m), pk(rep_v), "adamw_replicated")
    loss = rg[0, sum(int(w.size) for w in rep_w)]

    def unpack_rep(buf):
        out, pos_ = [], 0
        for w in rep_w:
            out.append(buf[:, pos_:pos_ + w.size])
            pos_ += w.size
        return out

    def arrange(sh, rep):
        rep = list(rep)
        rep[6] = rep[6].reshape(d)
        for i in (7, 8, 9):
            rep[i] = rep[i][:, :n_heads]
        return [rep[0], sh[0], sh[1], rep[1], rep[7], rep[8], rep[9], rep[2], sh[2], rep[3], rep[4],
                sh[3], sh[4], sh[5], rep[5], sh[6], sh[7], rep[6]]

    outs = [loss, grad_x.reshape(nb, s, d)]
    for sh, rep in ((sh_g, rg), (sh_d, rd), (sh_m, rm), (sh_v, rv)):
        outs += arrange(sh, unpack_rep(rep))
    return tuple(outs)
```

```python
import functools

import jax
import jax.numpy as jnp
from jax import lax
from jax.experimental import pallas as pl
from jax.experimental.pallas import tpu as pltpu

F32 = jnp.float32
BF16 = jnp.bfloat16
I32 = jnp.int32
MESH = pl.DeviceIdType.MESH
ANY = pl.BlockSpec(memory_space=pl.ANY)

EPS = 1e-6
CHUNK = 128
HEAD_DIM = 64
D_STATE = 128
GROUPS = 4
CONV_W = 4
POOL_WINDOWS = (2, 4, 8, 16)
LANES = 128
HEAD_PAD = 128
NEG = -1e30
N_DEV = 8

ADAM_LR = 0.001
ADAM_B1 = 0.9
ADAM_B2 = 0.999
ADAM_EPS = 1e-08
ADAM_WD = 0.01
ADAM_STEP = 10

NN = (((1,), (0,)), ((), ()))
NT = (((1,), (1,)), ((), ()))
TN = (((0,), (0,)), ((), ()))


def _pallas(body, **kw):
    return pl.pallas_call(body, **kw)


def _cparams(dims=None, vmem_mb=None):
    kw = {}
    if dims is not None:
        kw["dimension_semantics"] = dims
    if vmem_mb is not None:
        kw["vmem_limit_bytes"] = vmem_mb << 20
    return pltpu.CompilerParams(**kw)


def _dot(a, b, dims=NN):
    return lax.dot_general(a, b, dims, preferred_element_type=F32)


def _split3(v):
    hi = v.astype(BF16)
    r = v - hi.astype(F32)
    mid = r.astype(BF16)
    lo = (r - mid.astype(F32)).astype(BF16)
    return hi, mid, lo


def _exact_r(v, m, dims=NN):
    hi, mid, lo = _split3(v)
    return _dot(hi, m, dims) + _dot(mid, m, dims) + _dot(lo, m, dims)


def _exact_l(m, v, dims=NN):
    hi, mid, lo = _split3(v)
    return _dot(m, hi, dims) + _dot(m, mid, dims) + _dot(m, lo, dims)


def _row_tile(rows, target):
    best = None
    for cand in range(8, min(rows, target) + 1, 8):
        if rows % cand == 0:
            best = cand
    assert best is not None, rows
    return best


def _sum_all(v):
    return jnp.sum(jnp.sum(v, axis=1, keepdims=True), axis=0, keepdims=True)


def _iota(shape, dim):
    return lax.broadcasted_iota(I32, shape, dim)


def _sigmoid(x):
    return 1.0 / (1.0 + jnp.exp(-x))


def _softplus(x):
    return jnp.maximum(x, 0.0) + jnp.log(1.0 + jnp.exp(-jnp.abs(x)))


def _shift_down(v, j, row):
    return jnp.where(row >= j, pltpu.roll(v, j, 0), 0.0)


def _shift_up(v, j, row):
    n = v.shape[0]
    return jnp.where(row < n - j, pltpu.roll(v, n - j, 0), 0.0)


def _matmul(a, b, mode, out_dtype, name, tm, tn, tk):
    if mode == "nn":
        (m, k), n = a.shape, b.shape[1]
    elif mode == "nt":
        (m, k), n = a.shape, b.shape[0]
    else:
        (k, m), n = a.shape, b.shape[1]
    tm, tn, tk = min(tm, m), min(tn, n), min(tk, k)
    assert m % tm == 0 and n % tn == 0 and k % tk == 0, (name, m, n, k)
    nk = k // tk
    dims = {"nn": NN, "nt": NT, "tn": TN}[mode]

    def body_acc(a_ref, b_ref, o_ref, acc_ref):
        kk = pl.program_id(2)

        @pl.when(kk == 0)
        def _():
            acc_ref[...] = jnp.zeros_like(acc_ref)

        acc_ref[...] += _dot(a_ref[...].astype(BF16), b_ref[...].astype(BF16), dims)

        @pl.when(kk == nk - 1)
        def _():
            o_ref[...] = acc_ref[...].astype(o_ref.dtype)

    def body_one(a_ref, b_ref, o_ref):
        o_ref[...] = _dot(a_ref[...].astype(BF16), b_ref[...].astype(BF16), dims).astype(o_ref.dtype)

    body = body_one if nk == 1 else body_acc
    if mode == "tn":
        a_spec = pl.BlockSpec((tk, tm), lambda i, j, kk: (kk, i))
    else:
        a_spec = pl.BlockSpec((tm, tk), lambda i, j, kk: (i, kk))
    if mode == "nt":
        b_spec = pl.BlockSpec((tn, tk), lambda i, j, kk: (j, kk))
    else:
        b_spec = pl.BlockSpec((tk, tn), lambda i, j, kk: (kk, j))
    return _pallas(
        body, name=name, grid=(m // tm, n // tn, nk),
        in_specs=[a_spec, b_spec],
        out_specs=pl.BlockSpec((tm, tn), lambda i, j, kk: (i, j)),
        out_shape=jax.ShapeDtypeStruct((m, n), out_dtype),
        scratch_shapes=[] if nk == 1 else [pltpu.VMEM((tm, tn), F32)],
        compiler_params=_cparams(("parallel", "parallel", "arbitrary"), 56),
    )(a, b)


def _rms_fwd(x, g, name):
    t, d = x.shape
    tt = 512

    def body(x_ref, g_ref, h_ref):
        xv = x_ref[...]
        r = lax.rsqrt(jnp.mean(xv * xv, axis=-1, keepdims=True) + EPS)
        h_ref[...] = (xv * r * g_ref[...]).astype(h_ref.dtype)

    return _pallas(
        body, name=name, grid=(t // tt,),
        in_specs=[pl.BlockSpec((tt, d), lambda i: (i, 0)), pl.BlockSpec((1, d), lambda i: (0, 0))],
        out_specs=pl.BlockSpec((tt, d), lambda i: (i, 0)),
        out_shape=jax.ShapeDtypeStruct((t, d), BF16),
        compiler_params=_cparams(("parallel",)),
    )(x, g)


def _conv_fwd(proj, conv_w, conv_b, nb, s, off_xbc, n_conv):
    cw = 256

    def body(x_ref, w_ref, b_ref, o_ref):
        xv = x_ref[...]
        row = _iota(xv.shape, 0)
        acc = xv * w_ref[CONV_W - 1:CONV_W, :] + b_ref[...]
        for j in range(1, CONV_W):
            acc = acc + _shift_down(xv, j, row) * w_ref[CONV_W - 1 - j:CONV_W - j, :]
        o_ref[...] = acc * _sigmoid(acc)

    return _pallas(
        body, name="conv_fwd", grid=(nb, n_conv // cw),
        in_specs=[pl.BlockSpec((s, cw), lambda b, j: (b, off_xbc // cw + j)),
                  pl.BlockSpec((CONV_W, cw), lambda b, j: (0, j)),
                  pl.BlockSpec((1, cw), lambda b, j: (0, j))],
        out_specs=pl.BlockSpec((s, cw), lambda b, j: (b, j)),
        out_shape=jax.ShapeDtypeStruct((nb * s, n_conv), F32),
        compiler_params=_cparams(("parallel", "parallel"), 48),
    )(proj, conv_w, conv_b)


def _ssd_common(dtraw, bias, alog, n_heads):
    l = dtraw.shape[0]
    lane = _iota((l, HEAD_PAD), 1)
    lane1 = _iota((1, HEAD_PAD), 1)
    apad = jnp.where(lane1 < n_heads, -jnp.exp(alog), 0.0)
    dt = jnp.where(lane < n_heads, _softplus(dtraw + bias), 0.0)
    a = dt * apad
    tri = (_iota((l, l), 1) <= _iota((l, l), 0)).astype(BF16)
    acs = _exact_l(tri, a)
    width = n_heads * HEAD_DIM
    ex = (_iota((HEAD_PAD, width), 1) // HEAD_DIM == _iota((HEAD_PAD, width), 0)).astype(BF16)
    dt_full = _exact_r(dt, ex)
    acs_full = _exact_r(acs, ex)
    return dt, apad, acs, dt_full, acs_full, tri, ex


def _decay_mats(acs, acs_t, h, mask):
    col = acs[:, h:h + 1]
    row = acs_t[h:h + 1, :]
    return jnp.exp(jnp.where(mask, col - row, NEG))


def _ssd_fwd(xbc, proj, dtraw, dt_bias, a_log, dskip_full, gnorm, nb, s, d_inner):
    l = CHUNK
    nc = s // l
    n_heads = d_inner // HEAD_DIM
    gw = d_inner // GROUPS
    pairs_per_group = gw // LANES
    gn_cols = GROUPS * D_STATE

    def body(xs_ref, bm_ref, cm_ref, z_ref, dtr_ref, bias_ref, alog_ref, dsk_ref, gn_ref,
             y_ref, ya3_ref, prev_ref, state_ref):
        @pl.when(pl.program_id(1) == 0)
        def _():
            state_ref[...] = jnp.zeros_like(state_ref)

        xs = xs_ref[...]
        dt, apad, acs, dt_full, acs_full, tri, ex = _ssd_common(
            dtr_ref[...], bias_ref[...], alog_ref[...], n_heads)
        acs_t = acs.T
        e_full = jnp.exp(acs_full)
        last = acs_full[l - 1:l, :]
        dte = jnp.exp(last - acs_full)
        cd = jnp.exp(last)
        xdt = xs * dt_full
        xd = xdt * dte
        mask = _iota((l, l), 1) <= _iota((l, l), 0)
        lane = _iota((l, LANES), 1)
        prev_ref[0] = state_ref[...]
        for g in range(GROUPS):
            gsl = slice(g * gw, (g + 1) * gw)
            bg = bm_ref[:, g * D_STATE:(g + 1) * D_STATE].astype(BF16)
            cg = cm_ref[:, g * D_STATE:(g + 1) * D_STATE].astype(BF16)
            cb = _dot(cg, bg, NT)
            outs = []
            for q in range(pairs_per_group):
                pair = g * pairs_per_group + q
                h0, h1 = 2 * pair, 2 * pair + 1
                xpb = xdt[:, pair * LANES:(pair + 1) * LANES].astype(BF16)
                m0 = (cb * _decay_mats(acs, acs_t, h0, mask)).astype(BF16)
                m1 = (cb * _decay_mats(acs, acs_t, h1, mask)).astype(BF16)
                zero = jnp.zeros_like(xpb)
                rhs = jnp.concatenate([jnp.where(lane < HEAD_DIM, xpb, zero),
                                       jnp.where(lane >= HEAD_DIM, xpb, zero)], axis=0)
                outs.append(_dot(jnp.concatenate([m0, m1], axis=1), rhs))
            prev_g = state_ref[:, gsl]
            y_off = _dot(cg, prev_g.astype(BF16)) * e_full[:, gsl]
            y_ref[:, gsl] = jnp.concatenate(outs, axis=1) + y_off
            st = _dot(bg, xd[:, gsl].astype(BF16), TN)
            state_ref[:, gsl] = prev_g * cd[:, gsl] + st

        zv = z_ref[...]
        ya2 = (y_ref[...] + xs * dsk_ref[...]) * (zv * _sigmoid(zv))
        for g in range(GROUPS):
            gsl = slice(g * gw, (g + 1) * gw)
            blk = ya2[:, gsl]
            rg = lax.rsqrt(jnp.mean(blk * blk, axis=-1, keepdims=True) + EPS)
            ya3_ref[:, gsl] = (blk * rg * gn_ref[:, gsl]).astype(ya3_ref.dtype)

    row = lambda b, c: b * nc + c
    small = lambda n: pl.BlockSpec((1, n), lambda b, c: (0, 0))
    return _pallas(
        body, name="ssd_fwd", grid=(nb, nc),
        in_specs=[pl.BlockSpec((l, d_inner), lambda b, c: (row(b, c), 0)),
                  pl.BlockSpec((l, gn_cols), lambda b, c: (row(b, c), d_inner // gn_cols)),
                  pl.BlockSpec((l, gn_cols), lambda b, c: (row(b, c), d_inner // gn_cols + 1)),
                  pl.BlockSpec((l, d_inner), lambda b, c: (row(b, c), 0)),
                  pl.BlockSpec((l, HEAD_PAD), lambda b, c: (row(b, c), 0)),
                  small(HEAD_PAD), small(HEAD_PAD), small(d_inner), small(d_inner)],
        out_specs=[pl.BlockSpec((l, d_inner), lambda b, c: (row(b, c), 0)),
                   pl.BlockSpec((l, d_inner), lambda b, c: (row(b, c), 0)),
                   pl.BlockSpec((1, D_STATE, d_inner), lambda b, c: (row(b, c), 0, 0))],
        out_shape=[jax.ShapeDtypeStruct((nb * s, d_inner), F32),
                   jax.ShapeDtypeStruct((nb * s, d_inner), BF16),
                   jax.ShapeDtypeStruct((nb * nc, D_STATE, d_inner), F32)],
        scratch_shapes=[pltpu.VMEM((D_STATE, d_inner), F32)],
        compiler_params=_cparams(("parallel", "arbitrary"), 48),
    )(xbc, xbc, xbc, proj, dtraw, dt_bias, a_log, dskip_full, gnorm)


def _pool_fwd(proj, nb, s, off_u, d_pool):
    pg = d_pool // len(POOL_WINDOWS)

    def body(u_ref, o_ref):
        g = pl.program_id(1)
        uv = u_ref[...]
        row = _iota(uv.shape, 0)
        pos = (_iota((uv.shape[0], 1), 0) + 1).astype(F32)
        for gi, w in enumerate(POOL_WINDOWS):
            @pl.when(g == gi)
            def _():
                acc, span = uv, 1
                while span < w:
                    acc = acc + _shift_down(acc, span, row)
                    span *= 2
                o_ref[...] = (acc / jnp.minimum(pos, float(w)) - uv).astype(o_ref.dtype)

    return _pallas(
        body, name="pool_fwd", grid=(nb, len(POOL_WINDOWS)),
        in_specs=[pl.BlockSpec((s, pg), lambda b, g: (b, off_u // pg + g))],
        out_specs=pl.BlockSpec((s, pg), lambda b, g: (b, g)),
        out_shape=jax.ShapeDtypeStruct((nb * s, d_pool), BF16),
        compiler_params=_cparams(("parallel", "parallel"), 48),
    )(proj)


def _mix_fwd(pooled, mixw, proj, mix_b, scale, off_zp):
    t, d_pool = pooled.shape
    ng, pg = mixw.shape[0], mixw.shape[1]
    tm = 512

    def body(a_ref, w_ref, zp_ref, b_ref, s_ref, mixed_ref, yb_ref):
        mixed = _dot(a_ref[...], w_ref[0])
        mixed_ref[...] = mixed
        zp = zp_ref[...]
        yb_ref[...] = ((mixed + b_ref[...]) * s_ref[...] * (zp * _sigmoid(zp))).astype(yb_ref.dtype)

    blk = lambda off: pl.BlockSpec((tm, pg), lambda i, g: (i, off + g))
    vec = pl.BlockSpec((1, pg), lambda i, g: (0, g))
    return _pallas(
        body, name="mix_fwd", grid=(t // tm, ng),
        in_specs=[blk(0), pl.BlockSpec((1, pg, pg), lambda i, g: (g, 0, 0)), blk(off_zp // pg), vec, vec],
        out_specs=[blk(0), blk(0)],
        out_shape=[jax.ShapeDtypeStruct((t, d_pool), F32), jax.ShapeDtypeStruct((t, d_pool), BF16)],
        compiler_params=_cparams(("parallel", "parallel")),
    )(pooled, mixw, proj, mix_b, scale)


def _merge_fwd(ya3, wa, yb, wb, proj, off_ga, off_gb):
    t = ya3.shape[0]
    d = wa.shape[1]
    tm, tn = 256, 512

    def body(a_ref, wa_ref, b_ref, wb_ref, ga_ref, gb_ref, pa_ref, pb_ref, mg_ref):
        pa = _dot(a_ref[...], wa_ref[...])
        pb = _dot(b_ref[...], wb_ref[...])
        pa_ref[...] = pa
        pb_ref[...] = pb
        mg_ref[...] = (_sigmoid(ga_ref[...]) * pa + _sigmoid(gb_ref[...]) * pb).astype(mg_ref.dtype)

    o = pl.BlockSpec((tm, tn), lambda j, i: (i, j))
    return _pallas(
        body, name="merge_fwd", grid=(d // tn, t // tm),
        in_specs=[pl.BlockSpec((tm, ya3.shape[1]), lambda j, i: (i, 0)),
                  pl.BlockSpec((wa.shape[0], tn), lambda j, i: (0, j)),
                  pl.BlockSpec((tm, yb.shape[1]), lambda j, i: (i, 0)),
                  pl.BlockSpec((wb.shape[0], tn), lambda j, i: (0, j)),
                  pl.BlockSpec((tm, tn), lambda j, i: (i, off_ga // tn + j)),
                  pl.BlockSpec((tm, tn), lambda j, i: (i, off_gb // tn + j))],
        out_specs=[o, o, o],
        out_shape=[jax.ShapeDtypeStruct((t, d), F32), jax.ShapeDtypeStruct((t, d), F32),
                   jax.ShapeDtypeStruct((t, d), BF16)],
        compiler_params=_cparams(("parallel", "parallel"), 48),
    )(ya3, wa, yb, wb, proj, proj)


def _out_fwd(merged, wout, x, ple_g):
    t, d = x.shape
    tm = 256

    def body(m_ref, w_ref, x_ref, g_ref, x1_ref, h2_ref):
        x1 = x_ref[...] + _dot(m_ref[...], w_ref[...])
        x1_ref[...] = x1
        r = lax.rsqrt(jnp.mean(x1 * x1, axis=-1, keepdims=True) + EPS)
        h2_ref[...] = (x1 * r * g_ref[...]).astype(h2_ref.dtype)

    rowblk = pl.BlockSpec((tm, d), lambda i: (i, 0))
    return _pallas(
        body, name="out_fwd", grid=(t // tm,),
        in_specs=[rowblk, pl.BlockSpec((d, d), lambda i: (0, 0)), rowblk, pl.BlockSpec((1, d), lambda i: (0, 0))],
        out_specs=[rowblk, rowblk],
        out_shape=[jax.ShapeDtypeStruct((t, d), F32), jax.ShapeDtypeStruct((t, d), BF16)],
        compiler_params=_cparams(("parallel",), 48),
    )(merged, wout, x, ple_g)


def _final(h2, wg, p, wup, x1, target, final_g):
    t, d = x1.shape
    pd = p.shape[1]
    tm = 256

    def body(h2_ref, wg_ref, p_ref, wup_ref, x1_ref, tg_ref, g_ref,
             dx2_ref, dgl_ref, dpu_ref, loss_ref, dfg_ref):
        @pl.when(pl.program_id(0) == 0)
        def _():
            loss_ref[...] = jnp.zeros_like(loss_ref)
            dfg_ref[...] = jnp.zeros_like(dfg_ref)

        gate = _sigmoid(_dot(h2_ref[...], wg_ref[...]))
        pu = _dot(p_ref[...].astype(BF16), wup_ref[...])
        x2 = x1_ref[...] + gate * pu
        r = lax.rsqrt(jnp.mean(x2 * x2, axis=-1, keepdims=True) + EPS)
        xh = x2 * r
        gfin = g_ref[...]
        err = xh * gfin - tg_ref[...]
        loss_ref[...] += _sum_all(err * err) * (0.5 / d)
        dy = err * (1.0 / d)
        dfg_ref[...] += jnp.sum(dy * xh, axis=0, keepdims=True)
        dg = dy * gfin
        dx2 = r * (dg - xh * jnp.mean(dg * xh, axis=-1, keepdims=True))
        dx2_ref[...] = dx2
        dgl_ref[...] = (dx2 * pu * gate * (1.0 - gate)).astype(dgl_ref.dtype)
        dpu_ref[...] = (dx2 * gate).astype(dpu_ref.dtype)

    rowblk = pl.BlockSpec((tm, d), lambda i: (i, 0))
    full = lambda a, b: pl.BlockSpec((a, b), lambda i: (0, 0))
    return _pallas(
        body, name="final_fwd_bwd", grid=(t // tm,),
        in_specs=[rowblk, full(d, d), pl.BlockSpec((tm, pd), lambda i: (i, 0)), full(pd, d), rowblk, rowblk,
                  full(1, d)],
        out_specs=[rowblk, rowblk, rowblk, full(1, LANES), full(1, d)],
        out_shape=[jax.ShapeDtypeStruct((t, d), F32), jax.ShapeDtypeStruct((t, d), BF16),
                   jax.ShapeDtypeStruct((t, d), BF16), jax.ShapeDtypeStruct((1, LANES), F32),
                   jax.ShapeDtypeStruct((1, d), F32)],
        compiler_params=_cparams(("arbitrary",), 48),
    )(h2, wg, p, wup, x1, target, final_g)


def _ple_bwd(dgl, wg, dx2, x1, ple_g):
    t, d = x1.shape
    tm = 256

    def body(dgl_ref, wg_ref, dx2_ref, x1_ref, g_ref, dx1_ref, dx1b_ref, dg_ref):
        @pl.when(pl.program_id(0) == 0)
        def _():
            dg_ref[...] = jnp.zeros_like(dg_ref)

        dh2 = _dot(dgl_ref[...], wg_ref[...], NT)
        x1 = x1_ref[...]
        r = lax.rsqrt(jnp.mean(x1 * x1, axis=-1, keepdims=True) + EPS)
        xh = x1 * r
        dg_ref[...] += jnp.sum(dh2 * xh, axis=0, keepdims=True)
        dg = dh2 * g_ref[...]
        dx1 = dx2_ref[...] + r * (dg - xh * jnp.mean(dg * xh, axis=-1, keepdims=True))
        dx1_ref[...] = dx1
        dx1b_ref[...] = dx1.astype(dx1b_ref.dtype)

    rowblk = pl.BlockSpec((tm, d), lambda i: (i, 0))
    full = lambda a, b: pl.BlockSpec((a, b), lambda i: (0, 0))
    return _pallas(
        body, name="ple_bwd", grid=(t // tm,),
        in_specs=[rowblk, full(d, d), rowblk, rowblk, full(1, d)],
        out_specs=[rowblk, rowblk, full(1, d)],
        out_shape=[jax.ShapeDtypeStruct((t, d), F32), jax.ShapeDtypeStruct((t, d), BF16),
                   jax.ShapeDtypeStruct((1, d), F32)],
        compiler_params=_cparams(("arbitrary",), 48),
    )(dgl, wg, dx2, x1, ple_g)


def _merge_bwd(dx1b, wout, proj, pa, pb, n_main, off_ga, off_gb):
    t, d = pa.shape
    tm = 256

    def body(dx_ref, w_ref, ga_ref, gb_ref, pa_ref, pb_ref, dpa_ref, dpb_ref, dgate_ref):
        dm = _dot(dx_ref[...], w_ref[...], NT)
        sa = _sigmoid(ga_ref[...])
        sb = _sigmoid(gb_ref[...])
        dpa_ref[...] = (dm * sa).astype(dpa_ref.dtype)
        dpb_ref[...] = (dm * sb).astype(dpb_ref.dtype)
        dgate_ref[:, :d] = (dm * pa_ref[...] * sa * (1.0 - sa)).astype(dgate_ref.dtype)
        dgate_ref[:, d:] = (dm * pb_ref[...] * sb * (1.0 - sb)).astype(dgate_ref.dtype)

    assert off_gb == off_ga + d and off_ga % (2 * d) == 0
    rowblk = pl.BlockSpec((tm, d), lambda i: (i, 0))
    return _pallas(
        body, name="merge_bwd", grid=(t // tm,),
        in_specs=[rowblk, pl.BlockSpec((d, d), lambda i: (0, 0)),
                  pl.BlockSpec((tm, d), lambda i: (i, off_ga // d)),
                  pl.BlockSpec((tm, d), lambda i: (i, off_gb // d)), rowblk, rowblk],
        out_specs=[rowblk, rowblk, pl.BlockSpec((tm, 2 * d), lambda i: (i, off_ga // (2 * d)))],
        out_shape=[jax.ShapeDtypeStruct((t, d), BF16), jax.ShapeDtypeStruct((t, d), BF16),
                   jax.ShapeDtypeStruct((t, n_main), BF16)],
        compiler_params=_cparams(("parallel",), 48),
    )(dx1b, wout, proj, proj, pa, pb)


def _yb_bwd(dproj, dpb, wb, proj, mixed, mix_b, scale, off_zp):
    t, dp = mixed.shape
    tm = 256

    def body(_, dpb_ref, w_ref, zp_ref, mx_ref, b_ref, s_ref, dzp_ref, dmx_ref, dsc_ref, db_ref):
        @pl.when(pl.program_id(0) == 0)
        def _():
            dsc_ref[...] = jnp.zeros_like(dsc_ref)
            db_ref[...] = jnp.zeros_like(db_ref)

        dyb = _dot(dpb_ref[...], w_ref[...], NT)
        zp = zp_ref[...]
        sg = _sigmoid(zp)
        mb = mx_ref[...] + b_ref[...]
        yb0 = mb * s_ref[...]
        dyb0 = dyb * (zp * sg)
        dzp_ref[...] = (dyb * yb0 * (sg * (1.0 + zp * (1.0 - sg)))).astype(dzp_ref.dtype)
        dsc_ref[...] += jnp.sum(dyb0 * mb, axis=0, keepdims=True)
        dmx = dyb0 * s_ref[...]
        db_ref[...] += jnp.sum(dmx, axis=0, keepdims=True)
        dmx_ref[...] = dmx.astype(dmx_ref.dtype)

    rowblk = pl.BlockSpec((tm, dp), lambda i: (i, 0))
    vec = pl.BlockSpec((1, dp), lambda i: (0, 0))
    seg = pl.BlockSpec((tm, dp), lambda i: (i, off_zp // dp))
    return _pallas(
        body, name="yb_bwd", grid=(t // tm,),
        in_specs=[ANY, pl.BlockSpec((tm, wb.shape[1]), lambda i: (i, 0)),
                  pl.BlockSpec(wb.shape, lambda i: (0, 0)), seg, rowblk, vec, vec],
        out_specs=[seg, rowblk, vec, vec],
        out_shape=[jax.ShapeDtypeStruct(dproj.shape, dproj.dtype), jax.ShapeDtypeStruct((t, dp), BF16),
                   jax.ShapeDtypeStruct((1, dp), F32), jax.ShapeDtypeStruct((1, dp), F32)],
        input_output_aliases={0: 0},
        compiler_params=_cparams(("arbitrary",), 48),
    )(dproj, dpb, wb, proj, mixed, mix_b, scale)


def _dpooled(dmixed, mixw):
    t, dp = dmixed.shape
    ng, pg = mixw.shape[0], mixw.shape[1]
    tm = 512

    def body(a_ref, w_ref, o_ref):
        o_ref[...] = _dot(a_ref[...], w_ref[0], NT)

    blk = pl.BlockSpec((tm, pg), lambda i, g: (i, g))
    return _pallas(
        body, name="dpooled", grid=(t // tm, ng),
        in_specs=[blk, pl.BlockSpec((1, pg, pg), lambda i, g: (g, 0, 0))],
        out_specs=blk, out_shape=jax.ShapeDtypeStruct((t, dp), F32),
        compiler_params=_cparams(("parallel", "parallel")),
    )(dmixed, mixw)


def _dmixw(pooled, dmixed, ng):
    t, dp = pooled.shape
    pg = dp // ng
    tk = 512
    nk = t // tk

    def body(a_ref, b_ref, o_ref):
        @pl.when(pl.program_id(1) == 0)
        def _():
            o_ref[...] = jnp.zeros_like(o_ref)

        o_ref[0] += _dot(a_ref[...], b_ref[...], TN)

    blk = pl.BlockSpec((tk, pg), lambda g, k: (k, g))
    return _pallas(
        body, name="dmixw", grid=(ng, nk),
        in_specs=[blk, blk],
        out_specs=pl.BlockSpec((1, pg, pg), lambda g, k: (g, 0, 0)),
        out_shape=jax.ShapeDtypeStruct((ng, pg, pg), F32),
        compiler_params=_cparams(("parallel", "arbitrary")),
    )(pooled, dmixed)


def _pool_bwd(dproj, dpooled, nb, s, off_u):
    dp = dpooled.shape[1]
    pg = dp // len(POOL_WINDOWS)

    def body(_, d_ref, o_ref):
        g = pl.program_id(1)
        dv = d_ref[...]
        row = _iota(dv.shape, 0)
        pos = (_iota((dv.shape[0], 1), 0) + 1).astype(F32)
        for gi, w in enumerate(POOL_WINDOWS):
            @pl.when(g == gi)
            def _():
                acc, span = dv / jnp.minimum(pos, float(w)), 1
                while span < w:
                    acc = acc + _shift_up(acc, span, row)
                    span *= 2
                o_ref[...] = (acc - dv).astype(o_ref.dtype)

    return _pallas(
        body, name="pool_bwd", grid=(nb, len(POOL_WINDOWS)),
        in_specs=[ANY, pl.BlockSpec((s, pg), lambda b, g: (b, g))],
        out_specs=pl.BlockSpec((s, pg), lambda b, g: (b, off_u // pg + g)),
        out_shape=jax.ShapeDtypeStruct(dproj.shape, dproj.dtype),
        input_output_aliases={0: 0},
        compiler_params=_cparams(("parallel", "parallel"), 48),
    )(dproj, dpooled)


def _ssd_bwd(dproj, dya3, y, xbc, proj, dtraw, prev, dt_bias, a_log, dskip_full, gnorm, nb, s, d_inner):
    l = CHUNK
    nc = s // l
    n_heads = d_inner // HEAD_DIM
    gw = d_inner // GROUPS
    pairs_per_group = gw // LANES
    gn_cols = GROUPS * D_STATE
    n_conv = d_inner + 2 * gn_cols
    last_step = nb * nc - 1

    def body(_, dya3_ref, y_ref, xs_ref, bm_ref, cm_ref, z_ref, dtr_ref, prev_ref, bias_ref, alog_ref,
             dsk_ref, gn_ref,
             dz_ref, dxbc_ref, ddt_ref, dgn_ref, dskh_ref, dalog_ref, dbias_ref, g_ref, dskacc_ref):
        step = pl.program_id(0) * nc + pl.program_id(1)

        @pl.when(pl.program_id(1) == 0)
        def _():
            g_ref[...] = jnp.zeros_like(g_ref)

        @pl.when(step == 0)
        def _():
            dgn_ref[...] = jnp.zeros_like(dgn_ref)
            dalog_ref[...] = jnp.zeros_like(dalog_ref)
            dbias_ref[...] = jnp.zeros_like(dbias_ref)
            dskacc_ref[...] = jnp.zeros_like(dskacc_ref)

        xs = xs_ref[...]
        zv = z_ref[...]
        dsk = dsk_ref[...]
        sg = _sigmoid(zv)
        sz = zv * sg
        ya1 = y_ref[...] + xs * dsk
        ya2 = ya1 * sz
        parts = []
        for g in range(GROUPS):
            gsl = slice(g * gw, (g + 1) * gw)
            blk = ya2[:, gsl]
            rg = lax.rsqrt(jnp.mean(blk * blk, axis=-1, keepdims=True) + EPS)
            xh = blk * rg
            d3 = dya3_ref[:, gsl]
            dgn_ref[:, gsl] += jnp.sum(d3 * xh, axis=0, keepdims=True)
            dg = d3 * gn_ref[:, gsl]
            parts.append(rg * (dg - xh * jnp.mean(dg * xh, axis=-1, keepdims=True)))
        dya2 = jnp.concatenate(parts, axis=1)
        dy = dya2 * sz
        dz_ref[...] = (dya2 * ya1 * (sg * (1.0 + zv * (1.0 - sg)))).astype(dz_ref.dtype)
        dskacc_ref[...] += jnp.sum(dy * xs, axis=0, keepdims=True)

        dtraw_v = dtr_ref[...]
        bias = bias_ref[...]
        dt, apad, acs, dt_full, acs_full, tri, ex = _ssd_common(dtraw_v, bias, alog_ref[...], n_heads)
        acs_t = acs.T
        e_full = jnp.exp(acs_full)
        last = acs_full[l - 1:l, :]
        dte = jnp.exp(last - acs_full)
        cd = jnp.exp(last)
        xdt = xs * dt_full
        xd = xdt * dte
        dy_e = dy * e_full
        mask = _iota((l, l), 1) <= _iota((l, l), 0)
        lane = _iota((l, LANES), 1)
        hlane = _iota((l, HEAD_PAD), 1)
        hsub = _iota((HEAD_PAD, l), 0)
        dacs = jnp.zeros((l, HEAD_PAD), F32)
        dacs_t = jnp.zeros((HEAD_PAD, l), F32)
        dacs_full, dlast, dxdt_all, db_all, dc_all = [], [], [], [], []
        for g in range(GROUPS):
            gsl = slice(g * gw, (g + 1) * gw)
            bg = bm_ref[:, g * D_STATE:(g + 1) * D_STATE].astype(BF16)
            cg = cm_ref[:, g * D_STATE:(g + 1) * D_STATE].astype(BF16)
            prev_g = prev_ref[0, :, gsl]
            prev_b = prev_g.astype(BF16)
            gst = g_ref[:, gsl]
            gst_b = gst.astype(BF16)
            dye_b = dy_e[:, gsl].astype(BF16)
            dc = _dot(dye_b, prev_b, NT)
            dprev = _dot(cg, dye_b, TN)
            y_off = _dot(cg, prev_b) * e_full[:, gsl]
            xd_g = xd[:, gsl]
            db = _dot(xd_g.astype(BF16), gst_b, NT)
            dxd = _dot(bg, gst_b)
            wgt = dxd * xd_g
            dacs_full.append(dy[:, gsl] * y_off - wgt)
            dlast.append(jnp.sum(wgt, axis=0, keepdims=True)
                         + jnp.sum(prev_g * gst, axis=0, keepdims=True) * cd[:, gsl])
            g_ref[:, gsl] = dprev + cd[:, gsl] * gst
            dxdt_g = dxd * dte[:, gsl]
            cb = _dot(cg, bg, NT)
            dcb = jnp.zeros((l, l), F32)
            for q in range(pairs_per_group):
                pair = g * pairs_per_group + q
                psl = slice(pair * LANES, (pair + 1) * LANES)
                xpb = xdt[:, psl].astype(BF16)
                dyp = dy[:, psl].astype(BF16)
                zero = jnp.zeros_like(dyp)
                halves = (jnp.where(lane < HEAD_DIM, dyp, zero), jnp.where(lane >= HEAD_DIM, dyp, zero))
                acc = dxdt_g[:, q * LANES:(q + 1) * LANES]
                for k, dyh in enumerate(halves):
                    h = 2 * pair + k
                    dk = _decay_mats(acs, acs_t, h, mask)
                    mm = cb * dk
                    dm = _dot(dyh, xpb, NT)
                    dcb = dcb + dm * dk
                    dseg = dm * mm
                    dacs = dacs + jnp.where(hlane == h, jnp.sum(dseg, axis=1, keepdims=True), 0.0)
                    dacs_t = dacs_t + jnp.where(hsub == h, jnp.sum(dseg, axis=0, keepdims=True), 0.0)
                    acc = acc + _dot(mm.astype(BF16), dyh, TN)
                dxdt_all.append(acc)
            dcb_b = dcb.astype(BF16)
            dc_all.append(dc + _dot(dcb_b, bg))
            db_all.append(db + _dot(dcb_b, cg, TN))

        dxdt = jnp.concatenate(dxdt_all, axis=1)
        dxbc_ref[:, :d_inner] = dy * dsk + dxdt * dt_full
        dxbc_ref[:, d_inner:d_inner + gn_cols] = jnp.concatenate(db_all, axis=1)
        dxbc_ref[:, d_inner + gn_cols:] = jnp.concatenate(dc_all, axis=1)
        ddt = _exact_r(dxdt * xs, ex, NT)
        rowi = _iota((l, d_inner), 0)
        dacs_f = jnp.concatenate(dacs_full, axis=1) + jnp.where(rowi == l - 1, jnp.concatenate(dlast, axis=1), 0.0)
        dacs = dacs + _exact_r(dacs_f, ex, NT) - dacs_t.T
        da = _exact_l(tri, dacs, TN)
        ddt = ddt + da * apad
        dalog_ref[...] += jnp.sum(da * dt, axis=0, keepdims=True) * apad
        ddt_raw = ddt * _sigmoid(dtraw_v + bias)
        dbias_ref[...] += jnp.sum(ddt_raw, axis=0, keepdims=True)
        ddt_ref[...] = ddt_raw.astype(ddt_ref.dtype)

        @pl.when(step == last_step)
        def _():
            dskh_ref[...] = _exact_r(dskacc_ref[...], ex, NT)[0:1, :]

    row = lambda b, c: b * nc + (nc - 1 - c)
    small = lambda n: pl.BlockSpec((1, n), lambda b, c: (0, 0))
    wide = pl.BlockSpec((l, d_inner), lambda b, c: (row(b, c), 0))
    return _pallas(
        body, name="ssd_bwd", grid=(nb, nc),
        in_specs=[ANY, wide, wide, wide,
                  pl.BlockSpec((l, gn_cols), lambda b, c: (row(b, c), d_inner // gn_cols)),
                  pl.BlockSpec((l, gn_cols), lambda b, c: (row(b, c), d_inner // gn_cols + 1)),
                  wide, pl.BlockSpec((l, HEAD_PAD), lambda b, c: (row(b, c), 0)),
                  pl.BlockSpec((1, D_STATE, d_inner), lambda b, c: (row(b, c), 0, 0)),
                  small(HEAD_PAD), small(HEAD_PAD), small(d_inner), small(d_inner)],
        out_specs=[wide, pl.BlockSpec((l, n_conv), lambda b, c: (row(b, c), 0)),
                   pl.BlockSpec((l, HEAD_PAD), lambda b, c: (row(b, c), 0)),
                   small(d_inner), small(HEAD_PAD), small(HEAD_PAD), small(HEAD_PAD)],
        out_shape=[jax.ShapeDtypeStruct(dproj.shape, dproj.dtype),
                   jax.ShapeDtypeStruct((nb * s, n_conv), F32),
                   jax.ShapeDtypeStruct((nb * s, HEAD_PAD), BF16),
                   jax.ShapeDtypeStruct((1, d_inner), F32), jax.ShapeDtypeStruct((1, HEAD_PAD), F32),
                   jax.ShapeDtypeStruct((1, HEAD_PAD), F32), jax.ShapeDtypeStruct((1, HEAD_PAD), F32)],
        scratch_shapes=[pltpu.VMEM((D_STATE, d_inner), F32), pltpu.VMEM((8, d_inner), F32)],
        input_output_aliases={0: 0},
        compiler_params=_cparams(("arbitrary", "arbitrary"), 56),
    )(dproj, dya3, y, xbc, xbc, xbc, proj, dtraw, prev, dt_bias, a_log, dskip_full, gnorm)


def _conv_bwd(dproj, dxbc, proj, conv_w, conv_b, nb, s, off_xbc):
    n_conv = dxbc.shape[1]
    cw = 256

    def body(_, d_ref, x_ref, w_ref, b_ref, dx_ref, dw_ref, db_ref):
        @pl.when(pl.program_id(1) == 0)
        def _():
            dw_ref[...] = jnp.zeros_like(dw_ref)
            db_ref[...] = jnp.zeros_like(db_ref)

        xv = x_ref[...]
        row = _iota(xv.shape, 0)
        shifted = [xv] + [_shift_down(xv, j, row) for j in range(1, CONV_W)]
        acc = b_ref[...] + shifted[0] * w_ref[CONV_W - 1:CONV_W, :]
        for j in range(1, CONV_W):
            acc = acc + shifted[j] * w_ref[CONV_W - 1 - j:CONV_W - j, :]
        sg = _sigmoid(acc)
        dc = d_ref[...] * (sg * (1.0 + acc * (1.0 - sg)))
        db_ref[...] += jnp.sum(dc, axis=0, keepdims=True)
        dxv = dc * w_ref[CONV_W - 1:CONV_W, :]
        for j in range(CONV_W):
            dw_ref[CONV_W - 1 - j:CONV_W - j, :] += jnp.sum(dc * shifted[j], axis=0, keepdims=True)
            if j:
                dxv = dxv + _shift_up(dc, j, row) * w_ref[CONV_W - 1 - j:CONV_W - j, :]
        dx_ref[...] = dxv.astype(dx_ref.dtype)

    return _pallas(
        body, name="conv_bwd", grid=(n_conv // cw, nb),
        in_specs=[ANY, pl.BlockSpec((s, cw), lambda j, b: (b, j)),
                  pl.BlockSpec((s, cw), lambda j, b: (b, off_xbc // cw + j)),
                  pl.BlockSpec((CONV_W, cw), lambda j, b: (0, j)), pl.BlockSpec((1, cw), lambda j, b: (0, j))],
        out_specs=[pl.BlockSpec((s, cw), lambda j, b: (b, off_xbc // cw + j)),
                   pl.BlockSpec((CONV_W, cw), lambda j, b: (0, j)), pl.BlockSpec((1, cw), lambda j, b: (0, j))],
        out_shape=[jax.ShapeDtypeStruct(dproj.shape, dproj.dtype),
                   jax.ShapeDtypeStruct((CONV_W, n_conv), F32), jax.ShapeDtypeStruct((1, n_conv), F32)],
        input_output_aliases={0: 0},
        compiler_params=_cparams(("parallel", "arbitrary"), 48),
    )(dproj, dxbc, proj, conv_w, conv_b)


def _in_bwd(dproj, w_main, ddt, w_dt, x, norm_g, dx1):
    t, d = x.shape
    n_main = dproj.shape[1]
    tm, tk = min(1024, t), 1024
    nk = n_main // tk

    def body(dp_ref, w_ref, ddt_ref, wdt_ref, x_ref, g_ref, dx1_ref, dx_ref, dg_ref, acc_ref):
        i, kk = pl.program_id(0), pl.program_id(1)

        @pl.when(kk == 0)
        def _():
            acc_ref[...] = jnp.zeros_like(acc_ref)

        @pl.when((kk == 0) & (i == 0))
        def _():
            dg_ref[...] = jnp.zeros_like(dg_ref)

        acc_ref[...] += _dot(dp_ref[...], w_ref[...], NT)

        @pl.when(kk == nk - 1)
        def _():
            dh = acc_ref[...] + _dot(ddt_ref[...], wdt_ref[...], NT)
            xv = x_ref[...]
            r = lax.rsqrt(jnp.mean(xv * xv, axis=-1, keepdims=True) + EPS)
            xh = xv * r
            dg_ref[...] += jnp.sum(dh * xh, axis=0, keepdims=True)
            dg = dh * g_ref[...]
            dx_ref[...] = dx1_ref[...] + r * (dg - xh * jnp.mean(dg * xh, axis=-1, keepdims=True))

    rowblk = pl.BlockSpec((tm, d), lambda i, k: (i, 0))
    vec = pl.BlockSpec((1, d), lambda i, k: (0, 0))
    return _pallas(
        body, name="in_bwd", grid=(t // tm, nk),
        in_specs=[pl.BlockSpec((tm, tk), lambda i, k: (i, k)), pl.BlockSpec((d, tk), lambda i, k: (0, k)),
                  pl.BlockSpec((tm, HEAD_PAD), lambda i, k: (i, 0)), pl.BlockSpec((d, HEAD_PAD), lambda i, k: (0, 0)),
                  rowblk, vec, rowblk],
        out_specs=[rowblk, vec],
        out_shape=[jax.ShapeDtypeStruct((t, d), F32), jax.ShapeDtypeStruct((1, d), F32)],
        scratch_shapes=[pltpu.VMEM((tm, d), F32)],
        compiler_params=_cparams(("arbitrary", "arbitrary"), 56),
    )(dproj, w_main, ddt, w_dt, x, norm_g, dx1)


def _adamw_math(w, g, m, v):
    m = ADAM_B1 * m + (1.0 - ADAM_B1) * g
    v = ADAM_B2 * v + (1.0 - ADAM_B2) * (g * g)
    m_hat = m / (1.0 - ADAM_B1 ** ADAM_STEP)
    v_hat = v / (1.0 - ADAM_B2 ** ADAM_STEP)
    delta = -ADAM_LR * (m_hat / (jnp.sqrt(v_hat) + ADAM_EPS) + ADAM_WD * w)
    return delta, m, v


def _adamw(parts, w, m, v, name):
    n, rows, cols = parts.shape
    tr = _row_tile(rows, 128) if rows % 8 == 0 else rows

    def body(p_ref, w_ref, m_ref, v_ref, g_ref, d_ref, mo_ref, vo_ref):
        g = p_ref[0].astype(F32)
        for k in range(1, n):
            g = g + p_ref[k].astype(F32)
        d, mn, vn = _adamw_math(w_ref[...], g, m_ref[...], v_ref[...])
        g_ref[...] = g
        d_ref[...] = d
        mo_ref[...] = mn
        vo_ref[...] = vn

    blk = pl.BlockSpec((tr, cols), lambda i: (i, 0))
    shp = jax.ShapeDtypeStruct((rows, cols), F32)
    return _pallas(
        body, name=name, grid=(rows // tr,),
        in_specs=[pl.BlockSpec((n, tr, cols), lambda i: (0, i, 0)), blk, blk, blk],
        out_specs=[blk, blk, blk, blk], out_shape=[shp, shp, shp, shp],
        compiler_params=_cparams(("parallel",), 48),
    )(parts, w, m, v)


def _coords():
    return lax.axis_index("x"), lax.axis_index("y"), lax.axis_index("c")


def _all_gather_two_level(blocks, name):
    n = len(blocks)

    def body(*refs):
        x_refs, out_refs = refs[:n], refs[n:2 * n]
        send_sems, recv_sems, local_sems = refs[2 * n:]
        x, y, c = _coords()
        me, sibling = (x, y, c), (x, y, 1 - c)
        chips = [(1 - x, y), (x, 1 - y), (1 - x, 1 - y)]

        def slot(a, px, py, pc):
            return out_refs[a].at[4 * px + 2 * py + pc]

        def copy(a, k, block, to, src=None):
            return pltpu.make_async_remote_copy(
                src_ref=slot(a, *block) if src is None else src, dst_ref=slot(a, *block),
                send_sem=send_sems.at[a, k], recv_sem=recv_sems.at[a, k], device_id=to, device_id_type=MESH)

        local = [pltpu.make_async_copy(x_refs[a], slot(a, *me), local_sems.at[a]) for a in range(n)]
        first = []
        for a in range(n):
            first.append(copy(a, 0, me, sibling, src=x_refs[a]))
            first += [copy(a, 1 + j, me, (*chip, c), src=x_refs[a]) for j, chip in enumerate(chips)]
        for cp in local + first:
            cp.start()
        passed = []
        for j, chip in enumerate(chips):
            for a in range(n):
                copy(a, 1 + j, (*chip, c), me).wait_recv()
                passed.append(copy(a, 4 + j, (*chip, c), sibling))
                passed[-1].start()
        for a in range(n):
            copy(a, 0, sibling, me).wait_recv()
            for j, chip in enumerate(chips):
                copy(a, 4 + j, (*chip, 1 - c), me).wait_recv()
        for cp in first + passed:
            cp.wait_send()
        for cp in local:
            cp.wait()

    return _pallas(
        body, name=name, in_specs=[ANY] * n, out_specs=[ANY] * n,
        out_shape=[jax.ShapeDtypeStruct((N_DEV,) + b.shape, b.dtype) for b in blocks],
        scratch_shapes=[pltpu.SemaphoreType.DMA((n, 7)), pltpu.SemaphoreType.DMA((n, 7)),
                        pltpu.SemaphoreType.DMA((n,))],
    )(*blocks)


def _all_gather_direct(blk, name):
    rows, cols = blk.shape

    def body(x_ref, out_ref, send_sems, recv_sems, local_sem):
        x, y, c = _coords()
        me = 4 * x + 2 * y + c
        mine = pltpu.make_async_copy(x_ref, out_ref.at[me], local_sem)
        mine.start()
        copies = []
        for k in range(1, N_DEV):
            fx, fy, fc = (k >> 2) & 1, (k >> 1) & 1, k & 1
            peer = (x ^ fx, y ^ fy, c ^ fc)
            copies.append(pltpu.make_async_remote_copy(
                src_ref=x_ref, dst_ref=out_ref.at[me], send_sem=send_sems.at[k - 1], recv_sem=recv_sems.at[k - 1],
                device_id=peer, device_id_type=MESH))
        for cp in copies:
            cp.start()
        for k in range(1, N_DEV):
            fx, fy, fc = (k >> 2) & 1, (k >> 1) & 1, k & 1
            src = 4 * (x ^ fx) + 2 * (y ^ fy) + (c ^ fc)
            pltpu.make_async_remote_copy(
                src_ref=x_ref, dst_ref=out_ref.at[src], send_sem=send_sems.at[k - 1], recv_sem=recv_sems.at[k - 1],
                device_id=(x, y, c), device_id_type=MESH).wait_recv()
        for cp in copies:
            cp.wait_send()
        mine.wait()

    return _pallas(
        body, name=name, in_specs=[ANY], out_specs=ANY,
        out_shape=jax.ShapeDtypeStruct((N_DEV, rows, cols), blk.dtype),
        scratch_shapes=[pltpu.SemaphoreType.DMA((7,)), pltpu.SemaphoreType.DMA((7,)), pltpu.SemaphoreType.DMA],
    )(blk)


def _pair_exchange(grads):
    n = len(grads)

    def body(*refs):
        g_refs, out_refs = refs[:n], refs[n:2 * n]
        send_sems, recv_sems = refs[2 * n:]
        x, y, c = _coords()
        copies = []
        for a in range(n):
            for k in range(4):
                copies.append(pltpu.make_async_remote_copy(
                    src_ref=g_refs[a].at[2 * k + (1 - c)], dst_ref=out_refs[a].at[k],
                    send_sem=send_sems.at[a, k], recv_sem=recv_sems.at[a, k],
                    device_id=(x, y, 1 - c), device_id_type=MESH))
        for cp in copies:
            cp.start()
        for cp in copies:
            cp.wait()

    return _pallas(
        body, name="rs_pair_exchange", in_specs=[ANY] * n, out_specs=[ANY] * n,
        out_shape=[jax.ShapeDtypeStruct((4,) + g.shape[1:], g.dtype) for g in grads],
        scratch_shapes=[pltpu.SemaphoreType.DMA((n, 4)), pltpu.SemaphoreType.DMA((n, 4))],
    )(*grads)


def _pair_add(grads, got, core, out_dtype, name):
    _, rows, cols = grads.shape
    tr = _row_tile(rows, 256) if rows % 8 == 0 else rows

    def body(c_ref, a_ref, b_ref, o_ref):
        o_ref[...] = (a_ref[...] + b_ref[...]).astype(o_ref.dtype)

    grid_spec = pltpu.PrefetchScalarGridSpec(
        num_scalar_prefetch=1, grid=(4, rows // tr),
        in_specs=[pl.BlockSpec((1, tr, cols), lambda k, i, c_ref: (2 * k + c_ref[0], i, 0)),
                  pl.BlockSpec((1, tr, cols), lambda k, i, c_ref: (k, i, 0))],
        out_specs=pl.BlockSpec((1, tr, cols), lambda k, i, c_ref: (k, i, 0)))
    return _pallas(
        body, name=name, grid_spec=grid_spec,
        out_shape=jax.ShapeDtypeStruct((4, rows, cols), out_dtype),
        compiler_params=_cparams(("parallel", "parallel"), 48),
    )(core, grads, got)


def _chip_exchange(partials):
    n = len(partials)

    def body(*refs):
        p_refs, out_refs = refs[:n], refs[n:2 * n]
        send_sems, recv_sems, local_sems = refs[2 * n:]
        x, y, c = _coords()
        mychip = 2 * x + y
        local = [pltpu.make_async_copy(p_refs[a].at[mychip], out_refs[a].at[mychip], local_sems.at[a])
                 for a in range(n)]
        copies = []
        for a in range(n):
            for k in range(1, 4):
                fx, fy = (k >> 1) & 1, k & 1
                px, py = x ^ fx, y ^ fy
                copies.append(pltpu.make_async_remote_copy(
                    src_ref=p_refs[a].at[2 * px + py], dst_ref=out_refs[a].at[mychip],
                    send_sem=send_sems.at[a, k - 1], recv_sem=recv_sems.at[a, k - 1],
                    device_id=(px, py, c), device_id_type=MESH))
        for cp in local + copies:
            cp.start()
        for a in range(n):
            for k in range(1, 4):
                fx, fy = (k >> 1) & 1, k & 1
                src = 2 * (x ^ fx) + (y ^ fy)
                pltpu.make_async_remote_copy(
                    src_ref=p_refs[a].at[src], dst_ref=out_refs[a].at[src],
                    send_sem=send_sems.at[a, k - 1], recv_sem=recv_sems.at[a, k - 1],
                    device_id=(x, y, c), device_id_type=MESH).wait_recv()
        for cp in copies:
            cp.wait_send()
        for cp in local:
            cp.wait()

    return _pallas(
        body, name="rs_chip_exchange", in_specs=[ANY] * n, out_specs=[ANY] * n,
        out_shape=[jax.ShapeDtypeStruct(p.shape, p.dtype) for p in partials],
        scratch_shapes=[pltpu.SemaphoreType.DMA((n, 3)), pltpu.SemaphoreType.DMA((n, 3)),
                        pltpu.SemaphoreType.DMA((n,))],
    )(*partials)


def _pad_lanes(v, n=HEAD_PAD):
    return jnp.pad(v, ((0, 0), (0, n - v.shape[1])))


def kernel(x, p, norm_g, w_in, conv_w, conv_b, dt_bias, a_log, d_skip, gnorm_g, pool_mix_w, pool_mix_b, pool_scale, w_branch_a, w_branch_b, w_out, ple_norm_g, w_ple_gate, w_ple_up, final_g, loss_target, m_norm_g, m_w_in, m_conv_w, m_conv_b, m_dt_bias, m_a_log, m_d_skip, m_gnorm_g, m_pool_mix_w, m_pool_mix_b, m_pool_scale, m_w_branch_a, m_w_branch_b, m_w_out, m_ple_norm_g, m_w_ple_gate, m_w_ple_up, m_final_g, v_norm_g, v_w_in, v_conv_w, v_conv_b, v_dt_bias, v_a_log, v_d_skip, v_gnorm_g, v_pool_mix_w, v_pool_mix_b, v_pool_scale, v_w_branch_a, v_w_branch_b, v_w_out, v_ple_norm_g, v_w_ple_gate, v_w_ple_up, v_final_g):
    nb, s, d = x.shape
    t = nb * s
    n_heads = dt_bias.shape[1]
    d_inner = n_heads * HEAD_DIM
    n_conv = conv_b.shape[1]
    d_pool = pool_mix_b.shape[1]
    n_in = w_in.shape[2] * N_DEV
    ple_dim = p.shape[-1]
    o_xbc, o_dt = d_inner, d_inner + n_conv
    o_u = o_dt + n_heads
    o_zp = o_u + d_pool
    o_ga, o_gb = o_zp + d_pool, o_zp + d_pool + d
    n_main = n_in - n_heads
    off_ga, off_gb, off_xbc = d_inner, d_inner + d, d_inner + 2 * d
    off_u = off_xbc + n_conv
    off_zp = off_u + d_pool

    sharded = [w_in[0], conv_w[0], pool_mix_w[0], w_branch_a[0], w_branch_b[0], w_out[0], w_ple_gate[0], w_ple_up[0]]
    sharded_m = [m_w_in[0], m_conv_w[0], m_pool_mix_w[0], m_w_branch_a[0], m_w_branch_b[0], m_w_out[0],
                 m_w_ple_gate[0], m_w_ple_up[0]]
    sharded_v = [v_w_in[0], v_conv_w[0], v_pool_mix_w[0], v_w_branch_a[0], v_w_branch_b[0], v_w_out[0],
                 v_w_ple_gate[0], v_w_ple_up[0]]
    ng = pool_mix_w.shape[1]
    pg = d_pool // ng
    two_d = lambda w: w.reshape(-1, w.shape[-1])
    sharded, sharded_m, sharded_v = ([two_d(w) for w in ws] for ws in (sharded, sharded_m, sharded_v))

    send = [w if i == 1 else w.astype(BF16) for i, w in enumerate(sharded)]
    seg = _all_gather_two_level(send, "weights_all_gather")
    chunk = n_in // N_DEV
    segments = [((0, o_xbc), 0), ((o_ga, o_gb), off_ga), ((o_gb, n_in), off_gb), ((o_xbc, o_dt), off_xbc),
                ((o_u, o_zp), off_u), ((o_zp, o_ga), off_zp)]

    def gathered_cols(lo, hi):
        out = []
        while lo < hi:
            dev = lo // chunk
            a, b = lo - dev * chunk, min(hi - dev * chunk, chunk)
            out.append(seg[0][dev, :, a:b])
            lo = dev * chunk + b
        return out

    w_main = jnp.concatenate([pc for (lo, hi), _ in segments for pc in gathered_cols(lo, hi)], axis=1)
    w_dt = _pad_lanes(jnp.concatenate(gathered_cols(o_dt, o_u), axis=1))
    cw_full = seg[1].transpose(1, 0, 2).reshape(CONV_W, n_conv)
    mixw_full = seg[2].reshape(N_DEV, ng, pg // N_DEV, pg).transpose(1, 0, 2, 3).reshape(ng, pg, pg)
    wa_full = seg[3].reshape(d_inner, d)
    wb_full = seg[4].reshape(d_pool, d)
    wout_full = seg[5].reshape(d, d)
    wg_full = seg[6].reshape(d, d)
    wup_full = seg[7].transpose(1, 0, 2).reshape(ple_dim, d)

    x2 = x.reshape(t, d)
    p2 = p.reshape(t, ple_dim)
    tgt = loss_target.reshape(t, d)
    bias_p, alog_p = _pad_lanes(dt_bias), _pad_lanes(a_log)
    dskip_full = jnp.repeat(d_skip, HEAD_DIM, axis=1)

    h = _rms_fwd(x2, norm_g, "rms_in")
    proj = _matmul(h, w_main, "nn", F32, "in_proj", 1024, 512, d)
    dtraw = _matmul(h, w_dt, "nn", F32, "in_proj_dt", 1024, HEAD_PAD, d)
    xbc = _conv_fwd(proj, cw_full, conv_b, nb, s, off_xbc, n_conv)
    y, ya3, prev = _ssd_fwd(xbc, proj, dtraw, bias_p, alog_p, dskip_full, gnorm_g, nb, s, d_inner)
    pooled = _pool_fwd(proj, nb, s, off_u, d_pool)
    mixed, yb = _mix_fwd(pooled, mixw_full, proj, pool_mix_b, pool_scale, off_zp)
    pa, pb, merged = _merge_fwd(ya3, wa_full, yb, wb_full, proj, off_ga, off_gb)
    x1, h2 = _out_fwd(merged, wout_full, x2, ple_norm_g)
    dx2, dgl, dpu, loss_part, dfinal_g = _final(h2, wg_full, p2, wup_full, x1, tgt, final_g.reshape(1, d))

    g_wup = _matmul(p2, dpu, "tn", F32, "dw_ple_up", 256, 512, t)
    g_wg = _matmul(h2, dgl, "tn", F32, "dw_ple_gate", 512, 512, t)
    dx1, dx1b, d_ple_g = _ple_bwd(dgl, wg_full, dx2, x1, ple_norm_g)
    g_wout = _matmul(merged, dx1b, "tn", F32, "dw_out", 512, 512, t)
    dpa, dpb, dproj = _merge_bwd(dx1b, wout_full, proj, pa, pb, n_main, off_ga, off_gb)
    g_wa = _matmul(ya3, dpa, "tn", F32, "dw_branch_a", 512, 512, t)
    g_wb = _matmul(yb, dpb, "tn", F32, "dw_branch_b", 512, 512, t)
    dya3 = _matmul(dpa, wa_full, "nt", F32, "d_ya3", 512, 512, d)
    dproj, dmixed, d_scale, d_mixb = _yb_bwd(dproj, dpb, wb_full, proj, mixed, pool_mix_b, pool_scale, off_zp)
    g_mixw = _dmixw(pooled, dmixed, ng)
    dpool = _dpooled(dmixed, mixw_full)
    dproj = _pool_bwd(dproj, dpool, nb, s, off_u)
    dproj, dxbc, ddt, d_gn, d_dsk, d_alog, d_dtb = _ssd_bwd(
        dproj, dya3, y, xbc, proj, dtraw, prev, bias_p, alog_p, dskip_full, gnorm_g, nb, s, d_inner)
    dproj, g_cw, d_cb = _conv_bwd(dproj, dxbc, proj, cw_full, conv_b, nb, s, off_xbc)
    g_wmain = _matmul(h, dproj, "tn", F32, "dw_in", 512, 512, t)
    g_wdt = _matmul(h, ddt, "tn", F32, "dw_in_dt", 512, HEAD_PAD, t)
    grad_x, d_ng = _in_bwd(dproj, w_main, ddt, w_dt, x2, norm_g, dx1)

    def dest_cols(dev):
        lo, hi = dev * chunk, (dev + 1) * chunk
        srcs = [((olo, ohi), g_wmain, off) for (olo, ohi), off in segments] + [((o_dt, o_u), g_wdt, 0)]
        out = []
        for (olo, ohi), arr, off in sorted(srcs, key=lambda e: e[0][0]):
            a, b = max(lo, olo), min(hi, ohi)
            if a < b:
                out.append(arr[:, off + a - olo:off + b - olo])
        return jnp.concatenate(out, axis=1)

    by_dest = [
        jnp.stack([dest_cols(dev) for dev in range(N_DEV)], axis=0),
        g_cw.reshape(CONV_W, N_DEV, n_conv // N_DEV).transpose(1, 0, 2),
        g_mixw.reshape(ng, N_DEV, pg // N_DEV, pg).transpose(1, 0, 2, 3).reshape(N_DEV, ng * pg // N_DEV, pg),
        g_wa.reshape(N_DEV, d_inner // N_DEV, d), g_wb.reshape(N_DEV, d_pool // N_DEV, d),
        g_wout.reshape(N_DEV, d // N_DEV, d), g_wg.reshape(N_DEV, d // N_DEV, d),
        g_wup.reshape(ple_dim, N_DEV, d // N_DEV).transpose(1, 0, 2),
    ]
    core = lax.axis_index("c").astype(I32).reshape(1)
    got = _pair_exchange(by_dest)
    names = ["w_in", "conv_w", "pool_mix_w", "w_branch_a", "w_branch_b", "w_out", "w_ple_gate", "w_ple_up"]
    partials = [_pair_add(g, r, core, BF16, "rs_pair_add_" + nm) for g, r, nm in zip(by_dest, got, names)]
    parts = _chip_exchange(partials)
    sh_g, sh_d, sh_m, sh_v = [], [], [], []
    shapes = [w_in.shape, conv_w.shape, pool_mix_w.shape, w_branch_a.shape, w_branch_b.shape, w_out.shape,
              w_ple_gate.shape, w_ple_up.shape]
    for pt, w, m, v, nm, shp in zip(parts, sharded, sharded_m, sharded_v, names, shapes):
        res = _adamw(pt, w, m, v, "adamw_" + nm)
        for dst, r in zip((sh_g, sh_d, sh_m, sh_v), res):
            dst.append(r.reshape(shp))

    rep_w = [norm_g, conv_b, gnorm_g, pool_mix_b, pool_scale, ple_norm_g, final_g.reshape(1, d),
             bias_p, alog_p, _pad_lanes(d_skip)]
    rep_m = [m_norm_g, m_conv_b, m_gnorm_g, m_pool_mix_b, m_pool_scale, m_ple_norm_g, m_final_g.reshape(1, d),
             _pad_lanes(m_dt_bias), _pad_lanes(m_a_log), _pad_lanes(m_d_skip)]
    rep_v = [v_norm_g, v_conv_b, v_gnorm_g, v_pool_mix_b, v_pool_scale, v_ple_norm_g, v_final_g.reshape(1, d),
             _pad_lanes(v_dt_bias), _pad_lanes(v_a_log), _pad_lanes(v_d_skip)]
    rep_g = [d_ng, d_cb, d_gn, d_mixb, d_scale, d_ple_g, dfinal_g, d_dtb, d_alog, d_dsk]
    zero = jnp.zeros((1, LANES), F32)
    small = _all_gather_direct(jnp.concatenate(rep_g + [loss_part], axis=1), "small_all_gather")
    pk = lambda ws: jnp.concatenate(ws + [zero], axis=1)
    rg, rd, rm, rv = _adamw(small, pk(rep_w), pk(rep_m), pk(rep_v), "adamw_replicated")
    loss = rg[0, sum(int(w.size) for w in rep_w)]

    def unpack_rep(buf):
        out, pos_ = [], 0
        for w in rep_w:
            out.append(buf[:, pos_:pos_ + w.size])
            pos_ += w.size
        return out

    def arrange(sh, rep):
        rep = list(rep)
        rep[6] = rep[6].reshape(d)
        for i in (7, 8, 9):
            rep[i] = rep[i][:, :n_heads]
        return [rep[0], sh[0], sh[1], rep[1], rep[7], rep[8], rep[9], rep[2], sh[2], rep[3], rep[4],
                sh[3], sh[4], sh[5], rep[5], sh[6], sh[7], rep[6]]

    outs = [loss, grad_x.reshape(nb, s, d)]
    for sh, rep in ((sh_g, rg), (sh_d, rd), (sh_m, rm), (sh_v, rv)):
        outs += arrange(sh, unpack_rep(rep))
    return tuple(outs)
```

```python
import functools

import jax
import jax.numpy as jnp
from jax import lax
from jax.experimental import pallas as pl
from jax.experimental.pallas import tpu as pltpu

F32 = jnp.float32
BF16 = jnp.bfloat16
I32 = jnp.int32
MESH = pl.DeviceIdType.MESH
ANY = pl.BlockSpec(memory_space=pl.ANY)

EPS = 1e-6
CHUNK = 128
HEAD_DIM = 64
D_STATE = 128
GROUPS = 4
CONV_W = 4
POOL_WINDOWS = (2, 4, 8, 16)
LANES = 128
HEAD_PAD = 128
NEG = -1e30
N_DEV = 8

ADAM_LR = 0.001
ADAM_B1 = 0.9
ADAM_B2 = 0.999
ADAM_EPS = 1e-08
ADAM_WD = 0.01
ADAM_STEP = 10

NN = (((1,), (0,)), ((), ()))
NT = (((1,), (1,)), ((), ()))
TN = (((0,), (0,)), ((), ()))


def _pallas(body, **kw):
    return pl.pallas_call(body, **kw)


def _cparams(dims=None, vmem_mb=None):
    kw = {}
    if dims is not None:
        kw["dimension_semantics"] = dims
    if vmem_mb is not None:
        kw["vmem_limit_bytes"] = vmem_mb << 20
    return pltpu.CompilerParams(**kw)


def _dot(a, b, dims=NN):
    return lax.dot_general(a, b, dims, preferred_element_type=F32)


def _split3(v):
    hi = v.astype(BF16)
    r = v - hi.astype(F32)
    mid = r.astype(BF16)
    lo = (r - mid.astype(F32)).astype(BF16)
    return hi, mid, lo


def _exact_r(v, m, dims=NN):
    hi, mid, lo = _split3(v)
    return _dot(hi, m, dims) + _dot(mid, m, dims) + _dot(lo, m, dims)


def _exact_l(m, v, dims=NN):
    hi, mid, lo = _split3(v)
    return _dot(m, hi, dims) + _dot(m, mid, dims) + _dot(m, lo, dims)


def _row_tile(rows, target):
    best = None
    for cand in range(8, min(rows, target) + 1, 8):
        if rows % cand == 0:
            best = cand
    assert best is not None, rows
    return best


def _sum_all(v):
    return jnp.sum(jnp.sum(v, axis=1, keepdims=True), axis=0, keepdims=True)


def _iota(shape, dim):
    return lax.broadcasted_iota(I32, shape, dim)


def _sigmoid(x):
    return 0.5 * jnp.tanh(0.5 * x) + 0.5


def _softplus(x):
    return jnp.maximum(x, 0.0) + jnp.log(1.0 + jnp.exp(-jnp.abs(x)))


def _shift_down(v, j, row):
    return jnp.where(row >= j, pltpu.roll(v, j, 0), 0.0)


def _shift_up(v, j, row):
    n = v.shape[0]
    return jnp.where(row < n - j, pltpu.roll(v, n - j, 0), 0.0)


def _matmul(a, b, mode, out_dtype, name, tm, tn, tk):
    if mode == "nn":
        (m, k), n = a.shape, b.shape[1]
    elif mode == "nt":
        (m, k), n = a.shape, b.shape[0]
    else:
        (k, m), n = a.shape, b.shape[1]
    tm, tn, tk = min(tm, m), min(tn, n), min(tk, k)
    assert m % tm == 0 and n % tn == 0 and k % tk == 0, (name, m, n, k)
    nk = k // tk
    dims = {"nn": NN, "nt": NT, "tn": TN}[mode]

    def body_acc(a_ref, b_ref, o_ref, acc_ref):
        kk = pl.program_id(2)

        @pl.when(kk == 0)
        def _():
            acc_ref[...] = jnp.zeros_like(acc_ref)

        acc_ref[...] += _dot(a_ref[...].astype(BF16), b_ref[...].astype(BF16), dims)

        @pl.when(kk == nk - 1)
        def _():
            o_ref[...] = acc_ref[...].astype(o_ref.dtype)

    def body_one(a_ref, b_ref, o_ref):
        o_ref[...] = _dot(a_ref[...].astype(BF16), b_ref[...].astype(BF16), dims).astype(o_ref.dtype)

    body = body_one if nk == 1 else body_acc
    if mode == "tn":
        a_spec = pl.BlockSpec((tk, tm), lambda i, j, kk: (kk, i))
    else:
        a_spec = pl.BlockSpec((tm, tk), lambda i, j, kk: (i, kk))
    if mode == "nt":
        b_spec = pl.BlockSpec((tn, tk), lambda i, j, kk: (j, kk))
    else:
        b_spec = pl.BlockSpec((tk, tn), lambda i, j, kk: (kk, j))
    return _pallas(
        body, name=name, grid=(m // tm, n // tn, nk),
        in_specs=[a_spec, b_spec],
        out_specs=pl.BlockSpec((tm, tn), lambda i, j, kk: (i, j)),
        out_shape=jax.ShapeDtypeStruct((m, n), out_dtype),
        scratch_shapes=[] if nk == 1 else [pltpu.VMEM((tm, tn), F32)],
        compiler_params=_cparams(("parallel", "parallel", "arbitrary"), 56),
    )(a, b)


def _rms_fwd(x, g, name):
    t, d = x.shape
    tt = 512

    def body(x_ref, g_ref, h_ref):
        xv = x_ref[...]
        r = lax.rsqrt(jnp.mean(xv * xv, axis=-1, keepdims=True) + EPS)
        h_ref[...] = (xv * r * g_ref[...]).astype(h_ref.dtype)

    return _pallas(
        body, name=name, grid=(t // tt,),
        in_specs=[pl.BlockSpec((tt, d), lambda i: (i, 0)), pl.BlockSpec((1, d), lambda i: (0, 0))],
        out_specs=pl.BlockSpec((tt, d), lambda i: (i, 0)),
        out_shape=jax.ShapeDtypeStruct((t, d), BF16),
        compiler_params=_cparams(("parallel",)),
    )(x, g)


def _conv_fwd(proj, conv_w, conv_b, nb, s, off_xbc, n_conv):
    cw = 256

    def body(x_ref, w_ref, b_ref, o_ref):
        xv = x_ref[...]
        row = _iota(xv.shape, 0)
        acc = xv * w_ref[CONV_W - 1:CONV_W, :] + b_ref[...]
        for j in range(1, CONV_W):
            acc = acc + _shift_down(xv, j, row) * w_ref[CONV_W - 1 - j:CONV_W - j, :]
        o_ref[...] = acc * _sigmoid(acc)

    return _pallas(
        body, name="conv_fwd", grid=(nb, n_conv // cw),
        in_specs=[pl.BlockSpec((s, cw), lambda b, j: (b, off_xbc // cw + j)),
                  pl.BlockSpec((CONV_W, cw), lambda b, j: (0, j)),
                  pl.BlockSpec((1, cw), lambda b, j: (0, j))],
        out_specs=pl.BlockSpec((s, cw), lambda b, j: (b, j)),
        out_shape=jax.ShapeDtypeStruct((nb * s, n_conv), F32),
        compiler_params=_cparams(("parallel", "parallel"), 48),
    )(proj, conv_w, conv_b)


def _ssd_common(dtraw, bias, alog, n_heads):
    l = dtraw.shape[0]
    lane = _iota((l, HEAD_PAD), 1)
    lane1 = _iota((1, HEAD_PAD), 1)
    apad = jnp.where(lane1 < n_heads, -jnp.exp(alog), 0.0)
    dt = jnp.where(lane < n_heads, _softplus(dtraw + bias), 0.0)
    a = dt * apad
    tri = (_iota((l, l), 1) <= _iota((l, l), 0)).astype(BF16)
    acs = _exact_l(tri, a)
    width = n_heads * HEAD_DIM
    ex = (_iota((HEAD_PAD, width), 1) // HEAD_DIM == _iota((HEAD_PAD, width), 0)).astype(BF16)
    dt_full = _exact_r(dt, ex)
    acs_full = _exact_r(acs, ex)
    return dt, apad, acs, dt_full, acs_full, tri, ex


def _decay_mats(acs, acs_t, h, mask):
    col = acs[:, h:h + 1]
    row = acs_t[h:h + 1, :]
    return jnp.exp(jnp.where(mask, col - row, NEG))


def _ssd_fwd(xbc, proj, dtraw, dt_bias, a_log, dskip_full, gnorm, nb, s, d_inner):
    l = CHUNK
    nc = s // l
    n_heads = d_inner // HEAD_DIM
    gw = d_inner // GROUPS
    pairs_per_group = gw // LANES
    gn_cols = GROUPS * D_STATE

    def body(xs_ref, bm_ref, cm_ref, z_ref, dtr_ref, bias_ref, alog_ref, dsk_ref, gn_ref,
             y_ref, ya3_ref, prev_ref, state_ref):
        @pl.when(pl.program_id(1) == 0)
        def _():
            state_ref[...] = jnp.zeros_like(state_ref)

        xs = xs_ref[...]
        dt, apad, acs, dt_full, acs_full, tri, ex = _ssd_common(
            dtr_ref[...], bias_ref[...], alog_ref[...], n_heads)
        acs_t = acs.T
        e_full = jnp.exp(acs_full)
        last = acs_full[l - 1:l, :]
        dte = jnp.exp(last - acs_full)
        cd = jnp.exp(last)
        xdt = xs * dt_full
        xd = xdt * dte
        mask = _iota((l, l), 1) <= _iota((l, l), 0)
        lane = _iota((l, LANES), 1)
        prev_ref[0] = state_ref[...]
        for g in range(GROUPS):
            gsl = slice(g * gw, (g + 1) * gw)
            bg = bm_ref[:, g * D_STATE:(g + 1) * D_STATE].astype(BF16)
            cg = cm_ref[:, g * D_STATE:(g + 1) * D_STATE].astype(BF16)
            cb = _dot(cg, bg, NT)
            outs = []
            for q in range(pairs_per_group):
                pair = g * pairs_per_group + q
                h0, h1 = 2 * pair, 2 * pair + 1
                xpb = xdt[:, pair * LANES:(pair + 1) * LANES].astype(BF16)
                m0 = (cb * _decay_mats(acs, acs_t, h0, mask)).astype(BF16)
                m1 = (cb * _decay_mats(acs, acs_t, h1, mask)).astype(BF16)
                zero = jnp.zeros_like(xpb)
                rhs = jnp.concatenate([jnp.where(lane < HEAD_DIM, xpb, zero),
                                       jnp.where(lane >= HEAD_DIM, xpb, zero)], axis=0)
                outs.append(_dot(jnp.concatenate([m0, m1], axis=1), rhs))
            prev_g = state_ref[:, gsl]
            y_off = _dot(cg, prev_g.astype(BF16)) * e_full[:, gsl]
            y_ref[:, gsl] = jnp.concatenate(outs, axis=1) + y_off
            st = _dot(bg, xd[:, gsl].astype(BF16), TN)
            state_ref[:, gsl] = prev_g * cd[:, gsl] + st

        zv = z_ref[...]
        ya2 = (y_ref[...] + xs * dsk_ref[...]) * (zv * _sigmoid(zv))
        for g in range(GROUPS):
            gsl = slice(g * gw, (g + 1) * gw)
            blk = ya2[:, gsl]
            rg = lax.rsqrt(jnp.mean(blk * blk, axis=-1, keepdims=True) + EPS)
            ya3_ref[:, gsl] = (blk * rg * gn_ref[:, gsl]).astype(ya3_ref.dtype)

    row = lambda b, c: b * nc + c
    small = lambda n: pl.BlockSpec((1, n), lambda b, c: (0, 0))
    return _pallas(
        body, name="ssd_fwd", grid=(nb, nc),
        in_specs=[pl.BlockSpec((l, d_inner), lambda b, c: (row(b, c), 0)),
                  pl.BlockSpec((l, gn_cols), lambda b, c: (row(b, c), d_inner // gn_cols)),
                  pl.BlockSpec((l, gn_cols), lambda b, c: (row(b, c), d_inner // gn_cols + 1)),
                  pl.BlockSpec((l, d_inner), lambda b, c: (row(b, c), 0)),
                  pl.BlockSpec((l, HEAD_PAD), lambda b, c: (row(b, c), 0)),
                  small(HEAD_PAD), small(HEAD_PAD), small(d_inner), small(d_inner)],
        out_specs=[pl.BlockSpec((l, d_inner), lambda b, c: (row(b, c), 0)),
                   pl.BlockSpec((l, d_inner), lambda b, c: (row(b, c), 0)),
                   pl.BlockSpec((1, D_STATE, d_inner), lambda b, c: (row(b, c), 0, 0))],
        out_shape=[jax.ShapeDtypeStruct((nb * s, d_inner), F32),
                   jax.ShapeDtypeStruct((nb * s, d_inner), BF16),
                   jax.ShapeDtypeStruct((nb * nc, D_STATE, d_inner), F32)],
        scratch_shapes=[pltpu.VMEM((D_STATE, d_inner), F32)],
        compiler_params=_cparams(("parallel", "arbitrary"), 48),
    )(xbc, xbc, xbc, proj, dtraw, dt_bias, a_log, dskip_full, gnorm)


def _pool_fwd(proj, nb, s, off_u, d_pool):
    pg = d_pool // len(POOL_WINDOWS)

    def body(u_ref, o_ref):
        g = pl.program_id(1)
        uv = u_ref[...]
        row = _iota(uv.shape, 0)
        pos = (_iota((uv.shape[0], 1), 0) + 1).astype(F32)
        for gi, w in enumerate(POOL_WINDOWS):
            @pl.when(g == gi)
            def _():
                acc, span = uv, 1
                while span < w:
                    acc = acc + _shift_down(acc, span, row)
                    span *= 2
                o_ref[...] = (acc / jnp.minimum(pos, float(w)) - uv).astype(o_ref.dtype)

    return _pallas(
        body, name="pool_fwd", grid=(nb, len(POOL_WINDOWS)),
        in_specs=[pl.BlockSpec((s, pg), lambda b, g: (b, off_u // pg + g))],
        out_specs=pl.BlockSpec((s, pg), lambda b, g: (b, g)),
        out_shape=jax.ShapeDtypeStruct((nb * s, d_pool), BF16),
        compiler_params=_cparams(("parallel", "parallel"), 48),
    )(proj)


def _mix_fwd(pooled, mixw, proj, mix_b, scale, off_zp):
    t, d_pool = pooled.shape
    ng, pg = mixw.shape[0], mixw.shape[1]
    tm = 512

    def body(a_ref, w_ref, zp_ref, b_ref, s_ref, mixed_ref, yb_ref):
        mixed = _dot(a_ref[...], w_ref[0])
        mixed_ref[...] = mixed
        zp = zp_ref[...]
        yb_ref[...] = ((mixed + b_ref[...]) * s_ref[...] * (zp * _sigmoid(zp))).astype(yb_ref.dtype)

    blk = lambda off: pl.BlockSpec((tm, pg), lambda i, g: (i, off + g))
    vec = pl.BlockSpec((1, pg), lambda i, g: (0, g))
    return _pallas(
        body, name="mix_fwd", grid=(t // tm, ng),
        in_specs=[blk(0), pl.BlockSpec((1, pg, pg), lambda i, g: (g, 0, 0)), blk(off_zp // pg), vec, vec],
        out_specs=[blk(0), blk(0)],
        out_shape=[jax.ShapeDtypeStruct((t, d_pool), F32), jax.ShapeDtypeStruct((t, d_pool), BF16)],
        compiler_params=_cparams(("parallel", "parallel")),
    )(pooled, mixw, proj, mix_b, scale)


def _merge_fwd(ya3, wa, yb, wb, proj, off_ga, off_gb):
    t = ya3.shape[0]
    d = wa.shape[1]
    tm, tn = 256, 512

    def body(a_ref, wa_ref, b_ref, wb_ref, ga_ref, gb_ref, pa_ref, pb_ref, mg_ref):
        pa = _dot(a_ref[...], wa_ref[...])
        pb = _dot(b_ref[...], wb_ref[...])
        pa_ref[...] = pa
        pb_ref[...] = pb
        mg_ref[...] = (_sigmoid(ga_ref[...]) * pa + _sigmoid(gb_ref[...]) * pb).astype(mg_ref.dtype)

    o = pl.BlockSpec((tm, tn), lambda j, i: (i, j))
    return _pallas(
        body, name="merge_fwd", grid=(d // tn, t // tm),
        in_specs=[pl.BlockSpec((tm, ya3.shape[1]), lambda j, i: (i, 0)),
                  pl.BlockSpec((wa.shape[0], tn), lambda j, i: (0, j)),
                  pl.BlockSpec((tm, yb.shape[1]), lambda j, i: (i, 0)),
                  pl.BlockSpec((wb.shape[0], tn), lambda j, i: (0, j)),
                  pl.BlockSpec((tm, tn), lambda j, i: (i, off_ga // tn + j)),
                  pl.BlockSpec((tm, tn), lambda j, i: (i, off_gb // tn + j))],
        out_specs=[o, o, o],
        out_shape=[jax.ShapeDtypeStruct((t, d), F32), jax.ShapeDtypeStruct((t, d), F32),
                   jax.ShapeDtypeStruct((t, d), BF16)],
        compiler_params=_cparams(("parallel", "parallel"), 48),
    )(ya3, wa, yb, wb, proj, proj)


def _out_fwd(merged, wout, x, ple_g):
    t, d = x.shape
    tm = 256

    def body(m_ref, w_ref, x_ref, g_ref, x1_ref, h2_ref):
        x1 = x_ref[...] + _dot(m_ref[...], w_ref[...])
        x1_ref[...] = x1
        r = lax.rsqrt(jnp.mean(x1 * x1, axis=-1, keepdims=True) + EPS)
        h2_ref[...] = (x1 * r * g_ref[...]).astype(h2_ref.dtype)

    rowblk = pl.BlockSpec((tm, d), lambda i: (i, 0))
    return _pallas(
        body, name="out_fwd", grid=(t // tm,),
        in_specs=[rowblk, pl.BlockSpec((d, d), lambda i: (0, 0)), rowblk, pl.BlockSpec((1, d), lambda i: (0, 0))],
        out_specs=[rowblk, rowblk],
        out_shape=[jax.ShapeDtypeStruct((t, d), F32), jax.ShapeDtypeStruct((t, d), BF16)],
        compiler_params=_cparams(("parallel",), 48),
    )(merged, wout, x, ple_g)


def _final(h2, wg, p, wup, x1, target, final_g):
    t, d = x1.shape
    pd = p.shape[1]
    tm = 256

    def body(h2_ref, wg_ref, p_ref, wup_ref, x1_ref, tg_ref, g_ref,
             dx2_ref, dgl_ref, dpu_ref, loss_ref, dfg_ref):
        @pl.when(pl.program_id(0) == 0)
        def _():
            loss_ref[...] = jnp.zeros_like(loss_ref)
            dfg_ref[...] = jnp.zeros_like(dfg_ref)

        gate = _sigmoid(_dot(h2_ref[...], wg_ref[...]))
        pu = _dot(p_ref[...].astype(BF16), wup_ref[...])
        x2 = x1_ref[...] + gate * pu
        r = lax.rsqrt(jnp.mean(x2 * x2, axis=-1, keepdims=True) + EPS)
        xh = x2 * r
        gfin = g_ref[...]
        err = xh * gfin - tg_ref[...]
        loss_ref[...] += _sum_all(err * err) * (0.5 / d)
        dy = err * (1.0 / d)
        dfg_ref[...] += jnp.sum(dy * xh, axis=0, keepdims=True)
        dg = dy * gfin
        dx2 = r * (dg - xh * jnp.mean(dg * xh, axis=-1, keepdims=True))
        dx2_ref[...] = dx2
        dgl_ref[...] = (dx2 * pu * gate * (1.0 - gate)).astype(dgl_ref.dtype)
        dpu_ref[...] = (dx2 * gate).astype(dpu_ref.dtype)

    rowblk = pl.BlockSpec((tm, d), lambda i: (i, 0))
    full = lambda a, b: pl.BlockSpec((a, b), lambda i: (0, 0))
    return _pallas(
        body, name="final_fwd_bwd", grid=(t // tm,),
        in_specs=[rowblk, full(d, d), pl.BlockSpec((tm, pd), lambda i: (i, 0)), full(pd, d), rowblk, rowblk,
                  full(1, d)],
        out_specs=[rowblk, rowblk, rowblk, full(1, LANES), full(1, d)],
        out_shape=[jax.ShapeDtypeStruct((t, d), F32), jax.ShapeDtypeStruct((t, d), BF16),
                   jax.ShapeDtypeStruct((t, d), BF16), jax.ShapeDtypeStruct((1, LANES), F32),
                   jax.ShapeDtypeStruct((1, d), F32)],
        compiler_params=_cparams(("arbitrary",), 48),
    )(h2, wg, p, wup, x1, target, final_g)


def _ple_bwd(dgl, wg, dx2, x1, ple_g):
    t, d = x1.shape
    tm = 256

    def body(dgl_ref, wg_ref, dx2_ref, x1_ref, g_ref, dx1_ref, dx1b_ref, dg_ref):
        @pl.when(pl.program_id(0) == 0)
        def _():
            dg_ref[...] = jnp.zeros_like(dg_ref)

        dh2 = _dot(dgl_ref[...], wg_ref[...], NT)
        x1 = x1_ref[...]
        r = lax.rsqrt(jnp.mean(x1 * x1, axis=-1, keepdims=True) + EPS)
        xh = x1 * r
        dg_ref[...] += jnp.sum(dh2 * xh, axis=0, keepdims=True)
        dg = dh2 * g_ref[...]
        dx1 = dx2_ref[...] + r * (dg - xh * jnp.mean(dg * xh, axis=-1, keepdims=True))
        dx1_ref[...] = dx1
        dx1b_ref[...] = dx1.astype(dx1b_ref.dtype)

    rowblk = pl.BlockSpec((tm, d), lambda i: (i, 0))
    full = lambda a, b: pl.BlockSpec((a, b), lambda i: (0, 0))
    return _pallas(
        body, name="ple_bwd", grid=(t // tm,),
        in_specs=[rowblk, full(d, d), rowblk, rowblk, full(1, d)],
        out_specs=[rowblk, rowblk, full(1, d)],
        out_shape=[jax.ShapeDtypeStruct((t, d), F32), jax.ShapeDtypeStruct((t, d), BF16),
                   jax.ShapeDtypeStruct((1, d), F32)],
        compiler_params=_cparams(("arbitrary",), 48),
    )(dgl, wg, dx2, x1, ple_g)


def _merge_bwd(dx1b, wout, proj, pa, pb, n_main, off_ga, off_gb):
    t, d = pa.shape
    tm = 256

    def body(dx_ref, w_ref, ga_ref, gb_ref, pa_ref, pb_ref, dpa_ref, dpb_ref, dgate_ref):
        dm = _dot(dx_ref[...], w_ref[...], NT)
        sa = _sigmoid(ga_ref[...])
        sb = _sigmoid(gb_ref[...])
        dpa_ref[...] = (dm * sa).astype(dpa_ref.dtype)
        dpb_ref[...] = (dm * sb).astype(dpb_ref.dtype)
        dgate_ref[:, :d] = (dm * pa_ref[...] * sa * (1.0 - sa)).astype(dgate_ref.dtype)
        dgate_ref[:, d:] = (dm * pb_ref[...] * sb * (1.0 - sb)).astype(dgate_ref.dtype)

    assert off_gb == off_ga + d and off_ga % (2 * d) == 0
    rowblk = pl.BlockSpec((tm, d), lambda i: (i, 0))
    return _pallas(
        body, name="merge_bwd", grid=(t // tm,),
        in_specs=[rowblk, pl.BlockSpec((d, d), lambda i: (0, 0)),
                  pl.BlockSpec((tm, d), lambda i: (i, off_ga // d)),
                  pl.BlockSpec((tm, d), lambda i: (i, off_gb // d)), rowblk, rowblk],
        out_specs=[rowblk, rowblk, pl.BlockSpec((tm, 2 * d), lambda i: (i, off_ga // (2 * d)))],
        out_shape=[jax.ShapeDtypeStruct((t, d), BF16), jax.ShapeDtypeStruct((t, d), BF16),
                   jax.ShapeDtypeStruct((t, n_main), BF16)],
        compiler_params=_cparams(("parallel",), 48),
    )(dx1b, wout, proj, proj, pa, pb)


def _yb_bwd(dproj, dpb, wb, proj, mixed, mix_b, scale, off_zp):
    t, dp = mixed.shape
    tm = 256

    def body(_, dpb_ref, w_ref, zp_ref, mx_ref, b_ref, s_ref, dzp_ref, dmx_ref, dsc_ref, db_ref):
        @pl.when(pl.program_id(0) == 0)
        def _():
            dsc_ref[...] = jnp.zeros_like(dsc_ref)
            db_ref[...] = jnp.zeros_like(db_ref)

        dyb = _dot(dpb_ref[...], w_ref[...], NT)
        zp = zp_ref[...]
        sg = _sigmoid(zp)
        mb = mx_ref[...] + b_ref[...]
        yb0 = mb * s_ref[...]
        dyb0 = dyb * (zp * sg)
        dzp_ref[...] = (dyb * yb0 * (sg * (1.0 + zp * (1.0 - sg)))).astype(dzp_ref.dtype)
        dsc_ref[...] += jnp.sum(dyb0 * mb, axis=0, keepdims=True)
        dmx = dyb0 * s_ref[...]
        db_ref[...] += jnp.sum(dmx, axis=0, keepdims=True)
        dmx_ref[...] = dmx.astype(dmx_ref.dtype)

    rowblk = pl.BlockSpec((tm, dp), lambda i: (i, 0))
    vec = pl.BlockSpec((1, dp), lambda i: (0, 0))
    seg = pl.BlockSpec((tm, dp), lambda i: (i, off_zp // dp))
    return _pallas(
        body, name="yb_bwd", grid=(t // tm,),
        in_specs=[ANY, pl.BlockSpec((tm, wb.shape[1]), lambda i: (i, 0)),
                  pl.BlockSpec(wb.shape, lambda i: (0, 0)), seg, rowblk, vec, vec],
        out_specs=[seg, rowblk, vec, vec],
        out_shape=[jax.ShapeDtypeStruct(dproj.shape, dproj.dtype), jax.ShapeDtypeStruct((t, dp), BF16),
                   jax.ShapeDtypeStruct((1, dp), F32), jax.ShapeDtypeStruct((1, dp), F32)],
        input_output_aliases={0: 0},
        compiler_params=_cparams(("arbitrary",), 48),
    )(dproj, dpb, wb, proj, mixed, mix_b, scale)


def _dpooled(dmixed, mixw):
    t, dp = dmixed.shape
    ng, pg = mixw.shape[0], mixw.shape[1]
    tm = 512

    def body(a_ref, w_ref, o_ref):
        o_ref[...] = _dot(a_ref[...], w_ref[0], NT)

    blk = pl.BlockSpec((tm, pg), lambda i, g: (i, g))
    return _pallas(
        body, name="dpooled", grid=(t // tm, ng),
        in_specs=[blk, pl.BlockSpec((1, pg, pg), lambda i, g: (g, 0, 0))],
        out_specs=blk, out_shape=jax.ShapeDtypeStruct((t, dp), F32),
        compiler_params=_cparams(("parallel", "parallel")),
    )(dmixed, mixw)


def _dmixw(pooled, dmixed, ng):
    t, dp = pooled.shape
    pg = dp // ng
    tk = 512
    nk = t // tk

    def body(a_ref, b_ref, o_ref):
        @pl.when(pl.program_id(1) == 0)
        def _():
            o_ref[...] = jnp.zeros_like(o_ref)

        o_ref[0] += _dot(a_ref[...], b_ref[...], TN)

    blk = pl.BlockSpec((tk, pg), lambda g, k: (k, g))
    return _pallas(
        body, name="dmixw", grid=(ng, nk),
        in_specs=[blk, blk],
        out_specs=pl.BlockSpec((1, pg, pg), lambda g, k: (g, 0, 0)),
        out_shape=jax.ShapeDtypeStruct((ng, pg, pg), F32),
        compiler_params=_cparams(("parallel", "arbitrary")),
    )(pooled, dmixed)


def _pool_bwd(dproj, dpooled, nb, s, off_u):
    dp = dpooled.shape[1]
    pg = dp // len(POOL_WINDOWS)

    def body(_, d_ref, o_ref):
        g = pl.program_id(1)
        dv = d_ref[...]
        row = _iota(dv.shape, 0)
        pos = (_iota((dv.shape[0], 1), 0) + 1).astype(F32)
        for gi, w in enumerate(POOL_WINDOWS):
            @pl.when(g == gi)
            def _():
                acc, span = dv / jnp.minimum(pos, float(w)), 1
                while span < w:
                    acc = acc + _shift_up(acc, span, row)
                    span *= 2
                o_ref[...] = (acc - dv).astype(o_ref.dtype)

    return _pallas(
        body, name="pool_bwd", grid=(nb, len(POOL_WINDOWS)),
        in_specs=[ANY, pl.BlockSpec((s, pg), lambda b, g: (b, g))],
        out_specs=pl.BlockSpec((s, pg), lambda b, g: (b, off_u // pg + g)),
        out_shape=jax.ShapeDtypeStruct(dproj.shape, dproj.dtype),
        input_output_aliases={0: 0},
        compiler_params=_cparams(("parallel", "parallel"), 48),
    )(dproj, dpooled)


def _ssd_bwd(dproj, dya3, y, xbc, proj, dtraw, prev, dt_bias, a_log, dskip_full, gnorm, nb, s, d_inner, partials):
    l = CHUNK
    nc = s // l
    n_heads = d_inner // HEAD_DIM
    gw = d_inner // GROUPS
    pairs_per_group = gw // LANES
    gn_cols = GROUPS * D_STATE
    n_conv = d_inner + 2 * gn_cols
    last_step = nb * nc - 1

    xch = _ChipExchange(partials)
    n_in, n_out = 13, 7

    def body(*refs):
        (_, dya3_ref, y_ref, xs_ref, bm_ref, cm_ref, z_ref, dtr_ref, prev_ref, bias_ref, alog_ref,
         dsk_ref, gn_ref) = refs[:n_in]
        p_refs = refs[n_in:n_in + xch.n]
        outs = refs[n_in + xch.n:]
        dz_ref, dxbc_ref, ddt_ref, dgn_ref, dskh_ref, dalog_ref, dbias_ref = outs[:n_out]
        got_refs = outs[n_out:n_out + xch.n]
        g_ref, dskacc_ref = outs[n_out + xch.n:n_out + xch.n + 2]
        sems = outs[n_out + xch.n + 2:]
        step = pl.program_id(0) * nc + pl.program_id(1)

        @pl.when(pl.program_id(1) == 0)
        def _():
            g_ref[...] = jnp.zeros_like(g_ref)

        @pl.when(step == 0)
        def _():
            xch.start(p_refs, got_refs, sems)
            dgn_ref[...] = jnp.zeros_like(dgn_ref)
            dalog_ref[...] = jnp.zeros_like(dalog_ref)
            dbias_ref[...] = jnp.zeros_like(dbias_ref)
            dskacc_ref[...] = jnp.zeros_like(dskacc_ref)

        xs = xs_ref[...]
        zv = z_ref[...]
        dsk = dsk_ref[...]
        sg = _sigmoid(zv)
        sz = zv * sg
        ya1 = y_ref[...] + xs * dsk
        ya2 = ya1 * sz
        parts = []
        for g in range(GROUPS):
            gsl = slice(g * gw, (g + 1) * gw)
            blk = ya2[:, gsl]
            rg = lax.rsqrt(jnp.mean(blk * blk, axis=-1, keepdims=True) + EPS)
            xh = blk * rg
            d3 = dya3_ref[:, gsl]
            dgn_ref[:, gsl] += jnp.sum(d3 * xh, axis=0, keepdims=True)
            dg = d3 * gn_ref[:, gsl]
            parts.append(rg * (dg - xh * jnp.mean(dg * xh, axis=-1, keepdims=True)))
        dya2 = jnp.concatenate(parts, axis=1)
        dy = dya2 * sz
        dz_ref[...] = (dya2 * ya1 * (sg * (1.0 + zv * (1.0 - sg)))).astype(dz_ref.dtype)
        dskacc_ref[...] += jnp.sum(dy * xs, axis=0, keepdims=True)

        dtraw_v = dtr_ref[...]
        bias = bias_ref[...]
        dt, apad, acs, dt_full, acs_full, tri, ex = _ssd_common(dtraw_v, bias, alog_ref[...], n_heads)
        acs_t = acs.T
        e_full = jnp.exp(acs_full)
        last = acs_full[l - 1:l, :]
        dte = jnp.exp(last - acs_full)
        cd = jnp.exp(last)
        xdt = xs * dt_full
        xd = xdt * dte
        dy_e = dy * e_full
        mask = _iota((l, l), 1) <= _iota((l, l), 0)
        lane = _iota((l, LANES), 1)
        hlane = _iota((l, HEAD_PAD), 1)
        hsub = _iota((HEAD_PAD, l), 0)
        dacs = jnp.zeros((l, HEAD_PAD), F32)
        dacs_t = jnp.zeros((HEAD_PAD, l), F32)
        dacs_full, dlast, dxdt_all, db_all, dc_all = [], [], [], [], []
        for g in range(GROUPS):
            gsl = slice(g * gw, (g + 1) * gw)
            bg = bm_ref[:, g * D_STATE:(g + 1) * D_STATE].astype(BF16)
            cg = cm_ref[:, g * D_STATE:(g + 1) * D_STATE].astype(BF16)
            prev_g = prev_ref[0, :, gsl]
            prev_b = prev_g.astype(BF16)
            gst = g_ref[:, gsl]
            gst_b = gst.astype(BF16)
            dye_b = dy_e[:, gsl].astype(BF16)
            dc = _dot(dye_b, prev_b, NT)
            dprev = _dot(cg, dye_b, TN)
            y_off = _dot(cg, prev_b) * e_full[:, gsl]
            xd_g = xd[:, gsl]
            db = _dot(xd_g.astype(BF16), gst_b, NT)
            dxd = _dot(bg, gst_b)
            wgt = dxd * xd_g
            dacs_full.append(dy[:, gsl] * y_off - wgt)
            dlast.append(jnp.sum(wgt, axis=0, keepdims=True)
                         + jnp.sum(prev_g * gst, axis=0, keepdims=True) * cd[:, gsl])
            g_ref[:, gsl] = dprev + cd[:, gsl] * gst
            dxdt_g = dxd * dte[:, gsl]
            cb = _dot(cg, bg, NT)
            dcb = jnp.zeros((l, l), F32)
            for q in range(pairs_per_group):
                pair = g * pairs_per_group + q
                psl = slice(pair * LANES, (pair + 1) * LANES)
                xpb = xdt[:, psl].astype(BF16)
                dyp = dy[:, psl].astype(BF16)
                zero = jnp.zeros_like(dyp)
                halves = (jnp.where(lane < HEAD_DIM, dyp, zero), jnp.where(lane >= HEAD_DIM, dyp, zero))
                acc = dxdt_g[:, q * LANES:(q + 1) * LANES]
                for k, dyh in enumerate(halves):
                    h = 2 * pair + k
                    dk = _decay_mats(acs, acs_t, h, mask)
                    mm = cb * dk
                    dm = _dot(dyh, xpb, NT)
                    dcb = dcb + dm * dk
                    dseg = dm * mm
                    dacs = dacs + jnp.where(hlane == h, jnp.sum(dseg, axis=1, keepdims=True), 0.0)
                    dacs_t = dacs_t + jnp.where(hsub == h, jnp.sum(dseg, axis=0, keepdims=True), 0.0)
                    acc = acc + _dot(mm.astype(BF16), dyh, TN)
                dxdt_all.append(acc)
            dcb_b = dcb.astype(BF16)
            dc_all.append(dc + _dot(dcb_b, bg))
            db_all.append(db + _dot(dcb_b, cg, TN))

        dxdt = jnp.concatenate(dxdt_all, axis=1)
        dxbc_ref[:, :d_inner] = dy * dsk + dxdt * dt_full
        dxbc_ref[:, d_inner:d_inner + gn_cols] = jnp.concatenate(db_all, axis=1)
        dxbc_ref[:, d_inner + gn_cols:] = jnp.concatenate(dc_all, axis=1)
        ddt = _exact_r(dxdt * xs, ex, NT)
        rowi = _iota((l, d_inner), 0)
        dacs_f = jnp.concatenate(dacs_full, axis=1) + jnp.where(rowi == l - 1, jnp.concatenate(dlast, axis=1), 0.0)
        dacs = dacs + _exact_r(dacs_f, ex, NT) - dacs_t.T
        da = _exact_l(tri, dacs, TN)
        ddt = ddt + da * apad
        dalog_ref[...] += jnp.sum(da * dt, axis=0, keepdims=True) * apad
        ddt_raw = ddt * _sigmoid(dtraw_v + bias)
        dbias_ref[...] += jnp.sum(ddt_raw, axis=0, keepdims=True)
        ddt_ref[...] = ddt_raw.astype(ddt_ref.dtype)

        @pl.when(step == last_step)
        def _():
            dskh_ref[...] = _exact_r(dskacc_ref[...], ex, NT)[0:1, :]
            xch.wait(p_refs, got_refs, sems)

    row = lambda b, c: b * nc + (nc - 1 - c)
    small = lambda n: pl.BlockSpec((1, n), lambda b, c: (0, 0))
    wide = pl.BlockSpec((l, d_inner), lambda b, c: (row(b, c), 0))
    return _pallas(
        body, name="ssd_bwd", grid=(nb, nc),
        in_specs=[ANY, wide, wide, wide,
                  pl.BlockSpec((l, gn_cols), lambda b, c: (row(b, c), d_inner // gn_cols)),
                  pl.BlockSpec((l, gn_cols), lambda b, c: (row(b, c), d_inner // gn_cols + 1)),
                  wide, pl.BlockSpec((l, HEAD_PAD), lambda b, c: (row(b, c), 0)),
                  pl.BlockSpec((1, D_STATE, d_inner), lambda b, c: (row(b, c), 0, 0)),
                  small(HEAD_PAD), small(HEAD_PAD), small(d_inner), small(d_inner)] + xch.in_specs,
        out_specs=[wide, pl.BlockSpec((l, n_conv), lambda b, c: (row(b, c), 0)),
                   pl.BlockSpec((l, HEAD_PAD), lambda b, c: (row(b, c), 0)),
                   small(d_inner), small(HEAD_PAD), small(HEAD_PAD), small(HEAD_PAD)] + xch.out_specs,
        out_shape=[jax.ShapeDtypeStruct(dproj.shape, dproj.dtype),
                   jax.ShapeDtypeStruct((nb * s, n_conv), F32),
                   jax.ShapeDtypeStruct((nb * s, HEAD_PAD), BF16),
                   jax.ShapeDtypeStruct((1, d_inner), F32), jax.ShapeDtypeStruct((1, HEAD_PAD), F32),
                   jax.ShapeDtypeStruct((1, HEAD_PAD), F32), jax.ShapeDtypeStruct((1, HEAD_PAD), F32)]
        + xch.out_shape,
        scratch_shapes=[pltpu.VMEM((D_STATE, d_inner), F32), pltpu.VMEM((8, d_inner), F32)] + xch.scratch,
        input_output_aliases={0: 0},
        compiler_params=_cparams(("arbitrary", "arbitrary"), 56),
    )(dproj, dya3, y, xbc, xbc, xbc, proj, dtraw, prev, dt_bias, a_log, dskip_full, gnorm, *partials)


def _conv_bwd(dproj, dxbc, proj, conv_w, conv_b, nb, s, off_xbc):
    n_conv = dxbc.shape[1]
    cw = 256

    def body(_, d_ref, x_ref, w_ref, b_ref, dx_ref, dw_ref, db_ref):
        @pl.when(pl.program_id(1) == 0)
        def _():
            dw_ref[...] = jnp.zeros_like(dw_ref)
            db_ref[...] = jnp.zeros_like(db_ref)

        xv = x_ref[...]
        row = _iota(xv.shape, 0)
        shifted = [xv] + [_shift_down(xv, j, row) for j in range(1, CONV_W)]
        acc = b_ref[...] + shifted[0] * w_ref[CONV_W - 1:CONV_W, :]
        for j in range(1, CONV_W):
            acc = acc + shifted[j] * w_ref[CONV_W - 1 - j:CONV_W - j, :]
        sg = _sigmoid(acc)
        dc = d_ref[...] * (sg * (1.0 + acc * (1.0 - sg)))
        db_ref[...] += jnp.sum(dc, axis=0, keepdims=True)
        dxv = dc * w_ref[CONV_W - 1:CONV_W, :]
        for j in range(CONV_W):
            dw_ref[CONV_W - 1 - j:CONV_W - j, :] += jnp.sum(dc * shifted[j], axis=0, keepdims=True)
            if j:
                dxv = dxv + _shift_up(dc, j, row) * w_ref[CONV_W - 1 - j:CONV_W - j, :]
        dx_ref[...] = dxv.astype(dx_ref.dtype)

    return _pallas(
        body, name="conv_bwd", grid=(n_conv // cw, nb),
        in_specs=[ANY, pl.BlockSpec((s, cw), lambda j, b: (b, j)),
                  pl.BlockSpec((s, cw), lambda j, b: (b, off_xbc // cw + j)),
                  pl.BlockSpec((CONV_W, cw), lambda j, b: (0, j)), pl.BlockSpec((1, cw), lambda j, b: (0, j))],
        out_specs=[pl.BlockSpec((s, cw), lambda j, b: (b, off_xbc // cw + j)),
                   pl.BlockSpec((CONV_W, cw), lambda j, b: (0, j)), pl.BlockSpec((1, cw), lambda j, b: (0, j))],
        out_shape=[jax.ShapeDtypeStruct(dproj.shape, dproj.dtype),
                   jax.ShapeDtypeStruct((CONV_W, n_conv), F32), jax.ShapeDtypeStruct((1, n_conv), F32)],
        input_output_aliases={0: 0},
        compiler_params=_cparams(("parallel", "arbitrary"), 48),
    )(dproj, dxbc, proj, conv_w, conv_b)


def _in_bwd(dproj, w_main, ddt, w_dt, x, norm_g, dx1, partials):
    t, d = x.shape
    n_main = dproj.shape[1]
    tm, tk = min(1024, t), 1024
    nk = n_main // tk
    ni = t // tm
    xch = _ChipExchange(partials)
    n_in, n_out = 7, 2

    def body(*refs):
        dp_ref, w_ref, ddt_ref, wdt_ref, x_ref, g_ref, dx1_ref = refs[:n_in]
        p_refs = refs[n_in:n_in + xch.n]
        outs = refs[n_in + xch.n:]
        dx_ref, dg_ref = outs[:n_out]
        got_refs = outs[n_out:n_out + xch.n]
        acc_ref = outs[n_out + xch.n]
        sems = outs[n_out + xch.n + 1:]
        i, kk = pl.program_id(0), pl.program_id(1)

        @pl.when(kk == 0)
        def _():
            acc_ref[...] = jnp.zeros_like(acc_ref)

        @pl.when((kk == 0) & (i == 0))
        def _():
            xch.start(p_refs, got_refs, sems)
            dg_ref[...] = jnp.zeros_like(dg_ref)

        acc_ref[...] += _dot(dp_ref[...], w_ref[...], NT)

        @pl.when(kk == nk - 1)
        def _():
            dh = acc_ref[...] + _dot(ddt_ref[...], wdt_ref[...], NT)
            xv = x_ref[...]
            r = lax.rsqrt(jnp.mean(xv * xv, axis=-1, keepdims=True) + EPS)
            xh = xv * r
            dg_ref[...] += jnp.sum(dh * xh, axis=0, keepdims=True)
            dg = dh * g_ref[...]
            dx_ref[...] = dx1_ref[...] + r * (dg - xh * jnp.mean(dg * xh, axis=-1, keepdims=True))

        @pl.when((kk == nk - 1) & (i == ni - 1))
        def _():
            xch.wait(p_refs, got_refs, sems)

    rowblk = pl.BlockSpec((tm, d), lambda i, k: (i, 0))
    vec = pl.BlockSpec((1, d), lambda i, k: (0, 0))
    return _pallas(
        body, name="in_bwd", grid=(ni, nk),
        in_specs=[pl.BlockSpec((tm, tk), lambda i, k: (i, k)), pl.BlockSpec((d, tk), lambda i, k: (0, k)),
                  pl.BlockSpec((tm, HEAD_PAD), lambda i, k: (i, 0)), pl.BlockSpec((d, HEAD_PAD), lambda i, k: (0, 0)),
                  rowblk, vec, rowblk] + xch.in_specs,
        out_specs=[rowblk, vec] + xch.out_specs,
        out_shape=[jax.ShapeDtypeStruct((t, d), F32), jax.ShapeDtypeStruct((1, d), F32)] + xch.out_shape,
        scratch_shapes=[pltpu.VMEM((tm, d), F32)] + xch.scratch,
        compiler_params=_cparams(("arbitrary", "arbitrary"), 56),
    )(dproj, w_main, ddt, w_dt, x, norm_g, dx1, *partials)


def _adamw_math(w, g, m, v):
    m = ADAM_B1 * m + (1.0 - ADAM_B1) * g
    v = ADAM_B2 * v + (1.0 - ADAM_B2) * (g * g)
    m_hat = m / (1.0 - ADAM_B1 ** ADAM_STEP)
    v_hat = v / (1.0 - ADAM_B2 ** ADAM_STEP)
    delta = -ADAM_LR * (m_hat / (jnp.sqrt(v_hat) + ADAM_EPS) + ADAM_WD * w)
    return delta, m, v


def _adamw(parts, w, m, v, name):
    n, rows, cols = parts.shape
    tr = _row_tile(rows, 128) if rows % 8 == 0 else rows

    def body(p_ref, w_ref, m_ref, v_ref, g_ref, d_ref, mo_ref, vo_ref):
        g = p_ref[0].astype(F32)
        for k in range(1, n):
            g = g + p_ref[k].astype(F32)
        d, mn, vn = _adamw_math(w_ref[...], g, m_ref[...], v_ref[...])
        g_ref[...] = g
        d_ref[...] = d
        mo_ref[...] = mn
        vo_ref[...] = vn

    blk = pl.BlockSpec((tr, cols), lambda i: (i, 0))
    shp = jax.ShapeDtypeStruct((rows, cols), F32)
    return _pallas(
        body, name=name, grid=(rows // tr,),
        in_specs=[pl.BlockSpec((n, tr, cols), lambda i: (0, i, 0)), blk, blk, blk],
        out_specs=[blk, blk, blk, blk], out_shape=[shp, shp, shp, shp],
        compiler_params=_cparams(("parallel",), 48),
    )(parts, w, m, v)


def _coords():
    return lax.axis_index("x"), lax.axis_index("y"), lax.axis_index("c")


def _all_gather_two_level(blocks, name):
    n = len(blocks)

    def body(*refs):
        x_refs, out_refs = refs[:n], refs[n:2 * n]
        send_sems, recv_sems, local_sems = refs[2 * n:]
        x, y, c = _coords()
        me, sibling = (x, y, c), (x, y, 1 - c)
        chips = [(1 - x, y), (x, 1 - y), (1 - x, 1 - y)]

        def slot(a, px, py, pc):
            return out_refs[a].at[4 * px + 2 * py + pc]

        def copy(a, k, block, to, src=None):
            return pltpu.make_async_remote_copy(
                src_ref=slot(a, *block) if src is None else src, dst_ref=slot(a, *block),
                send_sem=send_sems.at[a, k], recv_sem=recv_sems.at[a, k], device_id=to, device_id_type=MESH)

        local = [pltpu.make_async_copy(x_refs[a], slot(a, *me), local_sems.at[a]) for a in range(n)]
        first = []
        for a in range(n):
            first.append(copy(a, 0, me, sibling, src=x_refs[a]))
            first += [copy(a, 1 + j, me, (*chip, c), src=x_refs[a]) for j, chip in enumerate(chips)]
        for cp in local + first:
            cp.start()
        passed = []
        for j, chip in enumerate(chips):
            for a in range(n):
                copy(a, 1 + j, (*chip, c), me).wait_recv()
                passed.append(copy(a, 4 + j, (*chip, c), sibling))
                passed[-1].start()
        for a in range(n):
            copy(a, 0, sibling, me).wait_recv()
            for j, chip in enumerate(chips):
                copy(a, 4 + j, (*chip, 1 - c), me).wait_recv()
        for cp in first + passed:
            cp.wait_send()
        for cp in local:
            cp.wait()

    return _pallas(
        body, name=name, in_specs=[ANY] * n, out_specs=[ANY] * n,
        out_shape=[jax.ShapeDtypeStruct((N_DEV,) + b.shape, b.dtype) for b in blocks],
        scratch_shapes=[pltpu.SemaphoreType.DMA((n, 7)), pltpu.SemaphoreType.DMA((n, 7)),
                        pltpu.SemaphoreType.DMA((n,))],
    )(*blocks)


def _all_gather_direct(blk, name):
    rows, cols = blk.shape

    def body(x_ref, out_ref, send_sems, recv_sems, local_sem):
        x, y, c = _coords()
        me = 4 * x + 2 * y + c
        mine = pltpu.make_async_copy(x_ref, out_ref.at[me], local_sem)
        mine.start()
        copies = []
        for k in range(1, N_DEV):
            fx, fy, fc = (k >> 2) & 1, (k >> 1) & 1, k & 1
            peer = (x ^ fx, y ^ fy, c ^ fc)
            copies.append(pltpu.make_async_remote_copy(
                src_ref=x_ref, dst_ref=out_ref.at[me], send_sem=send_sems.at[k - 1], recv_sem=recv_sems.at[k - 1],
                device_id=peer, device_id_type=MESH))
        for cp in copies:
            cp.start()
        for k in range(1, N_DEV):
            fx, fy, fc = (k >> 2) & 1, (k >> 1) & 1, k & 1
            src = 4 * (x ^ fx) + 2 * (y ^ fy) + (c ^ fc)
            pltpu.make_async_remote_copy(
                src_ref=x_ref, dst_ref=out_ref.at[src], send_sem=send_sems.at[k - 1], recv_sem=recv_sems.at[k - 1],
                device_id=(x, y, c), device_id_type=MESH).wait_recv()
        for cp in copies:
            cp.wait_send()
        mine.wait()

    return _pallas(
        body, name=name, in_specs=[ANY], out_specs=ANY,
        out_shape=jax.ShapeDtypeStruct((N_DEV, rows, cols), blk.dtype),
        scratch_shapes=[pltpu.SemaphoreType.DMA((7,)), pltpu.SemaphoreType.DMA((7,)), pltpu.SemaphoreType.DMA],
    )(blk)


def _pair_exchange(grads, name):
    n = len(grads)

    def body(*refs):
        g_refs, out_refs = refs[:n], refs[n:2 * n]
        send_sems, recv_sems = refs[2 * n:]
        x, y, c = _coords()
        copies = []
        for a in range(n):
            for k in range(4):
                copies.append(pltpu.make_async_remote_copy(
                    src_ref=g_refs[a].at[2 * k + (1 - c)], dst_ref=out_refs[a].at[k],
                    send_sem=send_sems.at[a, k], recv_sem=recv_sems.at[a, k],
                    device_id=(x, y, 1 - c), device_id_type=MESH))
        for cp in copies:
            cp.start()
        for cp in copies:
            cp.wait()

    return _pallas(
        body, name=name, in_specs=[ANY] * n, out_specs=[ANY] * n,
        out_shape=[jax.ShapeDtypeStruct((4,) + g.shape[1:], g.dtype) for g in grads],
        scratch_shapes=[pltpu.SemaphoreType.DMA((n, 4)), pltpu.SemaphoreType.DMA((n, 4))],
    )(*grads)


def _pair_add(grads, got, core, out_dtype, name):
    _, rows, cols = grads.shape
    tr = _row_tile(rows, 256) if rows % 8 == 0 else rows

    def body(c_ref, a_ref, b_ref, o_ref):
        o_ref[...] = (a_ref[...] + b_ref[...]).astype(o_ref.dtype)

    grid_spec = pltpu.PrefetchScalarGridSpec(
        num_scalar_prefetch=1, grid=(4, rows // tr),
        in_specs=[pl.BlockSpec((1, tr, cols), lambda k, i, c_ref: (2 * k + c_ref[0], i, 0)),
                  pl.BlockSpec((1, tr, cols), lambda k, i, c_ref: (k, i, 0))],
        out_specs=pl.BlockSpec((1, tr, cols), lambda k, i, c_ref: (k, i, 0)))
    return _pallas(
        body, name=name, grid_spec=grid_spec,
        out_shape=jax.ShapeDtypeStruct((4, rows, cols), out_dtype),
        compiler_params=_cparams(("parallel", "parallel"), 48),
    )(core, grads, got)


class _ChipExchange:
    def __init__(self, partials):
        self.n = len(partials)
        self.in_specs = [ANY] * self.n
        self.out_specs = [ANY] * self.n
        self.out_shape = [jax.ShapeDtypeStruct(p.shape, p.dtype) for p in partials]
        self.scratch = [pltpu.SemaphoreType.DMA((self.n, 3)), pltpu.SemaphoreType.DMA((self.n, 3)),
                        pltpu.SemaphoreType.DMA((self.n,))]

    def _copies(self, p_refs, out_refs, sems):
        send_sems, recv_sems, local_sems = sems
        x, y, c = _coords()
        mychip = 2 * x + y
        local = [pltpu.make_async_copy(p_refs[a].at[mychip], out_refs[a].at[mychip], local_sems.at[a])
                 for a in range(self.n)]
        sends, recvs = [], []
        for a in range(self.n):
            for k in range(1, 4):
                px, py = x ^ ((k >> 1) & 1), y ^ (k & 1)
                peer = 2 * px + py
                sems_k = dict(send_sem=send_sems.at[a, k - 1], recv_sem=recv_sems.at[a, k - 1])
                sends.append(pltpu.make_async_remote_copy(
                    src_ref=p_refs[a].at[peer], dst_ref=out_refs[a].at[mychip],
                    device_id=(px, py, c), device_id_type=MESH, **sems_k))
                recvs.append(pltpu.make_async_remote_copy(
                    src_ref=p_refs[a].at[peer], dst_ref=out_refs[a].at[peer],
                    device_id=(x, y, c), device_id_type=MESH, **sems_k))
        return local, sends, recvs

    def start(self, p_refs, out_refs, sems):
        local, sends, _ = self._copies(p_refs, out_refs, sems)
        for cp in local + sends:
            cp.start()

    def wait(self, p_refs, out_refs, sems):
        local, sends, recvs = self._copies(p_refs, out_refs, sems)
        for cp in recvs:
            cp.wait_recv()
        for cp in sends:
            cp.wait_send()
        for cp in local:
            cp.wait()


def _pad_lanes(v, n=HEAD_PAD):
    return jnp.pad(v, ((0, 0), (0, n - v.shape[1])))


def kernel(x, p, norm_g, w_in, conv_w, conv_b, dt_bias, a_log, d_skip, gnorm_g, pool_mix_w, pool_mix_b, pool_scale, w_branch_a, w_branch_b, w_out, ple_norm_g, w_ple_gate, w_ple_up, final_g, loss_target, m_norm_g, m_w_in, m_conv_w, m_conv_b, m_dt_bias, m_a_log, m_d_skip, m_gnorm_g, m_pool_mix_w, m_pool_mix_b, m_pool_scale, m_w_branch_a, m_w_branch_b, m_w_out, m_ple_norm_g, m_w_ple_gate, m_w_ple_up, m_final_g, v_norm_g, v_w_in, v_conv_w, v_conv_b, v_dt_bias, v_a_log, v_d_skip, v_gnorm_g, v_pool_mix_w, v_pool_mix_b, v_pool_scale, v_w_branch_a, v_w_branch_b, v_w_out, v_ple_norm_g, v_w_ple_gate, v_w_ple_up, v_final_g):
    nb, s, d = x.shape
    t = nb * s
    n_heads = dt_bias.shape[1]
    d_inner = n_heads * HEAD_DIM
    n_conv = conv_b.shape[1]
    d_pool = pool_mix_b.shape[1]
    n_in = w_in.shape[2] * N_DEV
    ple_dim = p.shape[-1]
    o_xbc, o_dt = d_inner, d_inner + n_conv
    o_u = o_dt + n_heads
    o_zp = o_u + d_pool
    o_ga, o_gb = o_zp + d_pool, o_zp + d_pool + d
    n_main = n_in - n_heads
    off_ga, off_gb, off_xbc = d_inner, d_inner + d, d_inner + 2 * d
    off_u = off_xbc + n_conv
    off_zp = off_u + d_pool

    sharded = [w_in[0], conv_w[0], pool_mix_w[0], w_branch_a[0], w_branch_b[0], w_out[0], w_ple_gate[0], w_ple_up[0]]
    sharded_m = [m_w_in[0], m_conv_w[0], m_pool_mix_w[0], m_w_branch_a[0], m_w_branch_b[0], m_w_out[0],
                 m_w_ple_gate[0], m_w_ple_up[0]]
    sharded_v = [v_w_in[0], v_conv_w[0], v_pool_mix_w[0], v_w_branch_a[0], v_w_branch_b[0], v_w_out[0],
                 v_w_ple_gate[0], v_w_ple_up[0]]
    ng = pool_mix_w.shape[1]
    pg = d_pool // ng
    two_d = lambda w: w.reshape(-1, w.shape[-1])
    sharded, sharded_m, sharded_v = ([two_d(w) for w in ws] for ws in (sharded, sharded_m, sharded_v))

    send = [w if i == 1 else w.astype(BF16) for i, w in enumerate(sharded)]
    seg = _all_gather_two_level(send, "weights_all_gather")
    chunk = n_in // N_DEV
    segments = [((0, o_xbc), 0), ((o_ga, o_gb), off_ga), ((o_gb, n_in), off_gb), ((o_xbc, o_dt), off_xbc),
                ((o_u, o_zp), off_u), ((o_zp, o_ga), off_zp)]

    def gathered_cols(lo, hi):
        out = []
        while lo < hi:
            dev = lo // chunk
            a, b = lo - dev * chunk, min(hi - dev * chunk, chunk)
            out.append(seg[0][dev, :, a:b])
            lo = dev * chunk + b
        return out

    w_main = jnp.concatenate([pc for (lo, hi), _ in segments for pc in gathered_cols(lo, hi)], axis=1)
    w_dt = _pad_lanes(jnp.concatenate(gathered_cols(o_dt, o_u), axis=1))
    cw_full = seg[1].transpose(1, 0, 2).reshape(CONV_W, n_conv)
    mixw_full = seg[2].reshape(N_DEV, ng, pg // N_DEV, pg).transpose(1, 0, 2, 3).reshape(ng, pg, pg)
    wa_full = seg[3].reshape(d_inner, d)
    wb_full = seg[4].reshape(d_pool, d)
    wout_full = seg[5].reshape(d, d)
    wg_full = seg[6].reshape(d, d)
    wup_full = seg[7].transpose(1, 0, 2).reshape(ple_dim, d)

    x2 = x.reshape(t, d)
    p2 = p.reshape(t, ple_dim)
    tgt = loss_target.reshape(t, d)
    bias_p, alog_p = _pad_lanes(dt_bias), _pad_lanes(a_log)
    dskip_full = jnp.repeat(d_skip, HEAD_DIM, axis=1)

    h = _rms_fwd(x2, norm_g, "rms_in")
    proj = _matmul(h, w_main, "nn", F32, "in_proj", 1024, 512, d)
    dtraw = _matmul(h, w_dt, "nn", F32, "in_proj_dt", 1024, HEAD_PAD, d)
    xbc = _conv_fwd(proj, cw_full, conv_b, nb, s, off_xbc, n_conv)
    y, ya3, prev = _ssd_fwd(xbc, proj, dtraw, bias_p, alog_p, dskip_full, gnorm_g, nb, s, d_inner)
    pooled = _pool_fwd(proj, nb, s, off_u, d_pool)
    mixed, yb = _mix_fwd(pooled, mixw_full, proj, pool_mix_b, pool_scale, off_zp)
    pa, pb, merged = _merge_fwd(ya3, wa_full, yb, wb_full, proj, off_ga, off_gb)
    x1, h2 = _out_fwd(merged, wout_full, x2, ple_norm_g)
    dx2, dgl, dpu, loss_part, dfinal_g = _final(h2, wg_full, p2, wup_full, x1, tgt, final_g.reshape(1, d))

    g_wup = _matmul(p2, dpu, "tn", F32, "dw_ple_up", 256, 512, t)
    g_wg = _matmul(h2, dgl, "tn", F32, "dw_ple_gate", 512, 512, t)
    dx1, dx1b, d_ple_g = _ple_bwd(dgl, wg_full, dx2, x1, ple_norm_g)
    g_wout = _matmul(merged, dx1b, "tn", F32, "dw_out", 512, 512, t)
    dpa, dpb, dproj = _merge_bwd(dx1b, wout_full, proj, pa, pb, n_main, off_ga, off_gb)
    g_wa = _matmul(ya3, dpa, "tn", F32, "dw_branch_a", 512, 512, t)
    g_wb = _matmul(yb, dpb, "tn", F32, "dw_branch_b", 512, 512, t)
    dya3 = _matmul(dpa, wa_full, "nt", F32, "d_ya3", 512, 512, d)
    dproj, dmixed, d_scale, d_mixb = _yb_bwd(dproj, dpb, wb_full, proj, mixed, pool_mix_b, pool_scale, off_zp)
    g_mixw = _dmixw(pooled, dmixed, ng)
    dpool = _dpooled(dmixed, mixw_full)
    dproj = _pool_bwd(dproj, dpool, nb, s, off_u)

    core = lax.axis_index("c").astype(I32).reshape(1)

    def pair_sums(by_dest, names):
        got = _pair_exchange(by_dest, "rs_pair_exchange_" + names[0])
        return [_pair_add(g, r, core, BF16, "rs_pair_add_" + nm) for g, r, nm in zip(by_dest, got, names)]

    names_a = ["pool_mix_w", "w_branch_a", "w_branch_b", "w_out", "w_ple_gate", "w_ple_up"]
    partials_a = pair_sums([
        g_mixw.reshape(ng, N_DEV, pg // N_DEV, pg).transpose(1, 0, 2, 3).reshape(N_DEV, ng * pg // N_DEV, pg),
        g_wa.reshape(N_DEV, d_inner // N_DEV, d), g_wb.reshape(N_DEV, d_pool // N_DEV, d),
        g_wout.reshape(N_DEV, d // N_DEV, d), g_wg.reshape(N_DEV, d // N_DEV, d),
        g_wup.reshape(ple_dim, N_DEV, d // N_DEV).transpose(1, 0, 2)], names_a)
    dproj, dxbc, ddt, d_gn, d_dsk, d_alog, d_dtb, *parts_a = _ssd_bwd(
        dproj, dya3, y, xbc, proj, dtraw, prev, bias_p, alog_p, dskip_full, gnorm_g, nb, s, d_inner, partials_a)
    dproj, g_cw, d_cb = _conv_bwd(dproj, dxbc, proj, cw_full, conv_b, nb, s, off_xbc)
    g_wmain = _matmul(h, dproj, "tn", F32, "dw_in", 512, 512, t)
    g_wdt = _matmul(h, ddt, "tn", F32, "dw_in_dt", 512, HEAD_PAD, t)

    def dest_cols(dev):
        lo, hi = dev * chunk, (dev + 1) * chunk
        srcs = [((olo, ohi), g_wmain, off) for (olo, ohi), off in segments] + [((o_dt, o_u), g_wdt, 0)]
        out = []
        for (olo, ohi), arr, off in sorted(srcs, key=lambda e: e[0][0]):
            a, b = max(lo, olo), min(hi, ohi)
            if a < b:
                out.append(arr[:, off + a - olo:off + b - olo])
        return jnp.concatenate(out, axis=1)

    names_b = ["w_in", "conv_w"]
    partials_b = pair_sums([jnp.stack([dest_cols(dev) for dev in range(N_DEV)], axis=0),
                            g_cw.reshape(CONV_W, N_DEV, n_conv // N_DEV).transpose(1, 0, 2)], names_b)
    grad_x, d_ng, *parts_b = _in_bwd(dproj, w_main, ddt, w_dt, x2, norm_g, dx1, partials_b)

    names = names_b + names_a
    parts = parts_b + parts_a
    sh_g, sh_d, sh_m, sh_v = [], [], [], []
    shapes = [w_in.shape, conv_w.shape, pool_mix_w.shape, w_branch_a.shape, w_branch_b.shape, w_out.shape,
              w_ple_gate.shape, w_ple_up.shape]
    for pt, w, m, v, nm, shp in zip(parts, sharded, sharded_m, sharded_v, names, shapes):
        res = _adamw(pt, w, m, v, "adamw_" + nm)
        for dst, r in zip((sh_g, sh_d, sh_m, sh_v), res):
            dst.append(r.reshape(shp))

    rep_w = [norm_g, conv_b, gnorm_g, pool_mix_b, pool_scale, ple_norm_g, final_g.reshape(1, d),
             bias_p, alog_p, _pad_lanes(d_skip)]
    rep_m = [m_norm_g, m_conv_b, m_gnorm_g, m_pool_mix_b, m_pool_scale, m_ple_norm_g, m_final_g.reshape(1, d),
             _pad_lanes(m_dt_bias), _pad_lanes(m_a_log), _pad_lanes(m_d_skip)]
    rep_v = [v_norm_g, v_conv_b, v_gnorm_g, v_pool_mix_b, v_pool_scale, v_ple_norm_g, v_final_g.reshape(1, d),
             _pad_lanes(v_dt_bias), _pad_lanes(v_a_log), _pad_lanes(v_d_skip)]
    rep_g = [d_ng, d_cb, d_gn, d_mixb, d_scale, d_ple_g, dfinal_g, d_dtb, d_alog, d_dsk]
    zero = jnp.zeros((1, LANES), F32)
    small = _all_gather_direct(jnp.concatenate(rep_g + [loss_part], axis=1), "small_all_gather")
    pk = lambda ws: jnp.concatenate(ws + [zero], axis=1)
    rg, rd, rm, rv = _adamw(small, pk(rep_w), pk(rep_m), pk(rep_v), "adamw_replicated")
    loss = rg[0, sum(int(w.size) for w in rep_w)]

    def unpack_rep(buf):
        out, pos_ = [], 0
        for w in rep_w:
            out.append(buf[:, pos_:pos_ + w.size])
            pos_ += w.size
        return out

    def arrange(sh, rep):
        rep = list(rep)
        rep[6] = rep[6].reshape(d)
        for i in (7, 8, 9):
            rep[i] = rep[i][:, :n_heads]
        return [rep[0], sh[0], sh[1], rep[1], rep[7], rep[8], rep[9], rep[2], sh[2], rep[3], rep[4],
                sh[3], sh[4], sh[5], rep[5], sh[6], sh[7], rep[6]]

    outs = [loss, grad_x.reshape(nb, s, d)]
    for sh, rep in ((sh_g, rg), (sh_d, rd), (sh_m, rm), (sh_v, rv)):
        outs += arrange(sh, unpack_rep(rep))
    return tuple(outs)
```

```python
import functools

import jax
import jax.numpy as jnp
from jax import lax
from jax.experimental import pallas as pl
from jax.experimental.pallas import tpu as pltpu

F32 = jnp.float32
BF16 = jnp.bfloat16
I32 = jnp.int32
MESH = pl.DeviceIdType.MESH
ANY = pl.BlockSpec(memory_space=pl.ANY)

EPS = 1e-6
CHUNK = 128
HEAD_DIM = 64
D_STATE = 128
GROUPS = 4
CONV_W = 4
POOL_WINDOWS = (2, 4, 8, 16)
ACT = jnp.bfloat16
LANES = 128
SUBLANES = 8
CONV_ROWS = 64
HEAD_PAD = 128
NEG = -1e30
N_DEV = 8

ADAM_LR = 0.001
ADAM_B1 = 0.9
ADAM_B2 = 0.999
ADAM_EPS = 1e-08
ADAM_WD = 0.01
ADAM_STEP = 10

NN = (((1,), (0,)), ((), ()))
NT = (((1,), (1,)), ((), ()))
TN = (((0,), (0,)), ((), ()))


def _pallas(body, **kw):
    return pl.pallas_call(body, **kw)


def _cparams(dims=None, vmem_mb=None):
    kw = {}
    if dims is not None:
        kw["dimension_semantics"] = dims
    if vmem_mb is not None:
        kw["vmem_limit_bytes"] = vmem_mb << 20
    return pltpu.CompilerParams(**kw)


def _dot(a, b, dims=NN):
    return lax.dot_general(a, b, dims, preferred_element_type=F32)


def _split3(v):
    hi = v.astype(BF16)
    r = v - hi.astype(F32)
    mid = r.astype(BF16)
    lo = (r - mid.astype(F32)).astype(BF16)
    return hi, mid, lo


def _exact_r(v, m, dims=NN):
    hi, mid, lo = _split3(v)
    return _dot(hi, m, dims) + _dot(mid, m, dims) + _dot(lo, m, dims)


def _exact_l(m, v, dims=NN):
    hi, mid, lo = _split3(v)
    return _dot(m, hi, dims) + _dot(m, mid, dims) + _dot(m, lo, dims)


def _row_tile(rows, target):
    best = None
    for cand in range(8, min(rows, target) + 1, 8):
        if rows % cand == 0:
            best = cand
    assert best is not None, rows
    return best


def _sum_all(v):
    return jnp.sum(jnp.sum(v, axis=1, keepdims=True), axis=0, keepdims=True)


def _iota(shape, dim):
    return lax.broadcasted_iota(I32, shape, dim)


def _sigmoid(x):
    return 0.5 * jnp.tanh(0.5 * x) + 0.5


def _softplus(x):
    return jnp.maximum(x, 0.0) + jnp.log(1.0 + jnp.exp(-jnp.abs(x)))


def _shift_down(v, j, row):
    return jnp.where(row >= j, pltpu.roll(v, j, 0), 0.0)


def _shift_up(v, j, row):
    n = v.shape[0]
    return jnp.where(row < n - j, pltpu.roll(v, n - j, 0), 0.0)


def _matmul(a, b, mode, out_dtype, name, tm, tn, tk):
    if mode == "nn":
        (m, k), n = a.shape, b.shape[1]
    elif mode == "nt":
        (m, k), n = a.shape, b.shape[0]
    else:
        (k, m), n = a.shape, b.shape[1]
    tm, tn, tk = min(tm, m), min(tn, n), min(tk, k)
    assert m % tm == 0 and n % tn == 0 and k % tk == 0, (name, m, n, k)
    nk = k // tk
    dims = {"nn": NN, "nt": NT, "tn": TN}[mode]

    def body_acc(a_ref, b_ref, o_ref, acc_ref):
        kk = pl.program_id(2)

        @pl.when(kk == 0)
        def _():
            acc_ref[...] = jnp.zeros_like(acc_ref)

        acc_ref[...] += _dot(a_ref[...].astype(BF16), b_ref[...].astype(BF16), dims)

        @pl.when(kk == nk - 1)
        def _():
            o_ref[...] = acc_ref[...].astype(o_ref.dtype)

    def body_one(a_ref, b_ref, o_ref):
        o_ref[...] = _dot(a_ref[...].astype(BF16), b_ref[...].astype(BF16), dims).astype(o_ref.dtype)

    body = body_one if nk == 1 else body_acc
    if mode == "tn":
        a_spec = pl.BlockSpec((tk, tm), lambda i, j, kk: (kk, i))
    else:
        a_spec = pl.BlockSpec((tm, tk), lambda i, j, kk: (i, kk))
    if mode == "nt":
        b_spec = pl.BlockSpec((tn, tk), lambda i, j, kk: (j, kk))
    else:
        b_spec = pl.BlockSpec((tk, tn), lambda i, j, kk: (kk, j))
    return _pallas(
        body, name=name, grid=(m // tm, n // tn, nk),
        in_specs=[a_spec, b_spec],
        out_specs=pl.BlockSpec((tm, tn), lambda i, j, kk: (i, j)),
        out_shape=jax.ShapeDtypeStruct((m, n), out_dtype),
        scratch_shapes=[] if nk == 1 else [pltpu.VMEM((tm, tn), F32)],
        compiler_params=_cparams(("parallel", "parallel", "arbitrary"), 56),
    )(a, b)


def _rms_fwd(x, g, name):
    t, d = x.shape
    tt = 512

    def body(x_ref, g_ref, h_ref):
        xv = x_ref[...]
        r = lax.rsqrt(jnp.mean(xv * xv, axis=-1, keepdims=True) + EPS)
        h_ref[...] = (xv * r * g_ref[...]).astype(h_ref.dtype)

    return _pallas(
        body, name=name, grid=(t // tt,),
        in_specs=[pl.BlockSpec((tt, d), lambda i: (i, 0)), pl.BlockSpec((1, d), lambda i: (0, 0))],
        out_specs=pl.BlockSpec((tt, d), lambda i: (i, 0)),
        out_shape=jax.ShapeDtypeStruct((t, d), BF16),
        compiler_params=_cparams(("parallel",)),
    )(x, g)


def _conv_fwd(proj, conv_w, conv_b, nb, s, off_xbc, n_conv):
    cw = 256

    rc = CONV_ROWS

    def body(x_ref, w_ref, b_ref, o_ref, pad_ref):
        pad_ref[0:SUBLANES, :] = jnp.zeros((SUBLANES, cw), F32)
        pad_ref[SUBLANES:, :] = x_ref[...].astype(F32)
        taps = [w_ref[k:k + 1, :] for k in range(CONV_W)]
        bias = b_ref[...]
        for r in range(s // rc):
            acc = bias
            for j in range(CONV_W):
                acc = acc + pad_ref[pl.ds(SUBLANES + r * rc - j, rc), :] * taps[CONV_W - 1 - j]
            o_ref[pl.ds(r * rc, rc), :] = (acc * _sigmoid(acc)).astype(o_ref.dtype)

    return _pallas(
        body, name="conv_fwd", grid=(nb, n_conv // cw),
        in_specs=[pl.BlockSpec((s, cw), lambda b, j: (b, off_xbc // cw + j)),
                  pl.BlockSpec((CONV_W, cw), lambda b, j: (0, j)),
                  pl.BlockSpec((1, cw), lambda b, j: (0, j))],
        out_specs=pl.BlockSpec((s, cw), lambda b, j: (b, j)),
        out_shape=jax.ShapeDtypeStruct((nb * s, n_conv), ACT),
        scratch_shapes=[pltpu.VMEM((s + SUBLANES, cw), F32)],
        compiler_params=_cparams(("parallel", "parallel"), 48),
    )(proj, conv_w, conv_b)


def _ssd_common(dtraw, bias, alog, n_heads):
    l = dtraw.shape[0]
    lane = _iota((l, HEAD_PAD), 1)
    lane1 = _iota((1, HEAD_PAD), 1)
    apad = jnp.where(lane1 < n_heads, -jnp.exp(alog), 0.0)
    dt = jnp.where(lane < n_heads, _softplus(dtraw + bias), 0.0)
    a = dt * apad
    tri = (_iota((l, l), 1) <= _iota((l, l), 0)).astype(BF16)
    acs = _exact_l(tri, a)
    width = n_heads * HEAD_DIM
    ex = (_iota((HEAD_PAD, width), 1) // HEAD_DIM == _iota((HEAD_PAD, width), 0)).astype(BF16)
    dt_full = _exact_r(dt, ex)
    acs_full = _exact_r(acs, ex)
    return dt, apad, acs, dt_full, acs_full, tri, ex


def _decay_mats(acs, acs_t, h, mask):
    col = acs[:, h:h + 1]
    row = acs_t[h:h + 1, :]
    return jnp.exp(jnp.where(mask, col - row, NEG))


def _ssd_fwd(xbc, proj, dtraw, dt_bias, a_log, dskip_full, gnorm, nb, s, d_inner):
    l = CHUNK
    nc = s // l
    n_heads = d_inner // HEAD_DIM
    gw = d_inner // GROUPS
    pairs_per_group = gw // LANES
    gn_cols = GROUPS * D_STATE

    def body(xs_ref, bm_ref, cm_ref, z_ref, dtr_ref, bias_ref, alog_ref, dsk_ref, gn_ref,
             y_ref, ya3_ref, prev_ref, state_ref):
        @pl.when(pl.program_id(1) == 0)
        def _():
            state_ref[...] = jnp.zeros_like(state_ref)

        xs = xs_ref[...].astype(F32)
        dt, apad, acs, dt_full, acs_full, tri, ex = _ssd_common(
            dtr_ref[...], bias_ref[...], alog_ref[...], n_heads)
        acs_t = acs.T
        e_full = jnp.exp(acs_full)
        last = acs_full[l - 1:l, :]
        dte = jnp.exp(last - acs_full)
        cd = jnp.exp(last)
        xdt = xs * dt_full
        xd = xdt * dte
        mask = _iota((l, l), 1) <= _iota((l, l), 0)
        lane = _iota((l, LANES), 1)
        prev_ref[0] = state_ref[...]
        for g in range(GROUPS):
            gsl = slice(g * gw, (g + 1) * gw)
            bg = bm_ref[:, g * D_STATE:(g + 1) * D_STATE].astype(BF16)
            cg = cm_ref[:, g * D_STATE:(g + 1) * D_STATE].astype(BF16)
            cb = _dot(cg, bg, NT)
            outs = []
            for q in range(pairs_per_group):
                pair = g * pairs_per_group + q
                h0, h1 = 2 * pair, 2 * pair + 1
                xpb = xdt[:, pair * LANES:(pair + 1) * LANES].astype(BF16)
                m0 = (cb * _decay_mats(acs, acs_t, h0, mask)).astype(BF16)
                m1 = (cb * _decay_mats(acs, acs_t, h1, mask)).astype(BF16)
                zero = jnp.zeros_like(xpb)
                rhs = jnp.concatenate([jnp.where(lane < HEAD_DIM, xpb, zero),
                                       jnp.where(lane >= HEAD_DIM, xpb, zero)], axis=0)
                outs.append(_dot(jnp.concatenate([m0, m1], axis=1), rhs))
            prev_g = state_ref[:, gsl]
            y_off = _dot(cg, prev_g.astype(BF16)) * e_full[:, gsl]
            y_g = jnp.concatenate(outs, axis=1) + y_off
            y_ref[:, gsl] = y_g.astype(y_ref.dtype)
            st = _dot(bg, xd[:, gsl].astype(BF16), TN)
            state_ref[:, gsl] = prev_g * cd[:, gsl] + st
            zv = z_ref[:, gsl].astype(F32)
            blk = (y_g + xs[:, gsl] * dsk_ref[:, gsl]) * (zv * _sigmoid(zv))
            rg = lax.rsqrt(jnp.mean(blk * blk, axis=-1, keepdims=True) + EPS)
            ya3_ref[:, gsl] = (blk * rg * gn_ref[:, gsl]).astype(ya3_ref.dtype)

    row = lambda b, c: b * nc + c
    small = lambda n: pl.BlockSpec((1, n), lambda b, c: (0, 0))
    return _pallas(
        body, name="ssd_fwd", grid=(nb, nc),
        in_specs=[pl.BlockSpec((l, d_inner), lambda b, c: (row(b, c), 0)),
                  pl.BlockSpec((l, gn_cols), lambda b, c: (row(b, c), d_inner // gn_cols)),
                  pl.BlockSpec((l, gn_cols), lambda b, c: (row(b, c), d_inner // gn_cols + 1)),
                  pl.BlockSpec((l, d_inner), lambda b, c: (row(b, c), 0)),
                  pl.BlockSpec((l, HEAD_PAD), lambda b, c: (row(b, c), 0)),
                  small(HEAD_PAD), small(HEAD_PAD), small(d_inner), small(d_inner)],
        out_specs=[pl.BlockSpec((l, d_inner), lambda b, c: (row(b, c), 0)),
                   pl.BlockSpec((l, d_inner), lambda b, c: (row(b, c), 0)),
                   pl.BlockSpec((1, D_STATE, d_inner), lambda b, c: (row(b, c), 0, 0))],
        out_shape=[jax.ShapeDtypeStruct((nb * s, d_inner), ACT),
                   jax.ShapeDtypeStruct((nb * s, d_inner), BF16),
                   jax.ShapeDtypeStruct((nb * nc, D_STATE, d_inner), F32)],
        scratch_shapes=[pltpu.VMEM((D_STATE, d_inner), F32)],
        compiler_params=_cparams(("parallel", "arbitrary"), 48),
    )(xbc, xbc, xbc, proj, dtraw, dt_bias, a_log, dskip_full, gnorm)


def _pool_fwd(proj, nb, s, off_u, d_pool):
    pg = d_pool // len(POOL_WINDOWS)

    def body(u_ref, o_ref):
        g = pl.program_id(1)
        uv = u_ref[...].astype(F32)
        row = _iota(uv.shape, 0)
        pos = (_iota((uv.shape[0], 1), 0) + 1).astype(F32)
        for gi, w in enumerate(POOL_WINDOWS):
            @pl.when(g == gi)
            def _():
                acc, span = uv, 1
                while span < w:
                    acc = acc + _shift_down(acc, span, row)
                    span *= 2
                o_ref[...] = (acc / jnp.minimum(pos, float(w)) - uv).astype(o_ref.dtype)

    return _pallas(
        body, name="pool_fwd", grid=(nb, len(POOL_WINDOWS)),
        in_specs=[pl.BlockSpec((s, pg), lambda b, g: (b, off_u // pg + g))],
        out_specs=pl.BlockSpec((s, pg), lambda b, g: (b, g)),
        out_shape=jax.ShapeDtypeStruct((nb * s, d_pool), BF16),
        compiler_params=_cparams(("parallel", "parallel"), 48),
    )(proj)


def _mix_fwd(pooled, mixw, proj, mix_b, scale, off_zp):
    t, d_pool = pooled.shape
    ng, pg = mixw.shape[0], mixw.shape[1]
    tm = min(1024, t)

    def body(a_ref, w_ref, zp_ref, b_ref, s_ref, mixed_ref, yb_ref):
        mixed = _dot(a_ref[...], w_ref[0])
        mixed_ref[...] = mixed.astype(mixed_ref.dtype)
        zp = zp_ref[...].astype(F32)
        yb_ref[...] = ((mixed + b_ref[...]) * s_ref[...] * (zp * _sigmoid(zp))).astype(yb_ref.dtype)

    blk = lambda off: pl.BlockSpec((tm, pg), lambda i, g: (i, off + g))
    vec = pl.BlockSpec((1, pg), lambda i, g: (0, g))
    return _pallas(
        body, name="mix_fwd", grid=(t // tm, ng),
        in_specs=[blk(0), pl.BlockSpec((1, pg, pg), lambda i, g: (g, 0, 0)), blk(off_zp // pg), vec, vec],
        out_specs=[blk(0), blk(0)],
        out_shape=[jax.ShapeDtypeStruct((t, d_pool), ACT), jax.ShapeDtypeStruct((t, d_pool), BF16)],
        compiler_params=_cparams(("parallel", "parallel")),
    )(pooled, mixw, proj, mix_b, scale)


def _merge_fwd(ya3, wa, yb, wb, proj, off_ga, off_gb):
    t = ya3.shape[0]
    d = wa.shape[1]
    tm, tn = 512, d

    def body(a_ref, wa_ref, b_ref, wb_ref, ga_ref, gb_ref, pa_ref, pb_ref, mg_ref):
        pa = _dot(a_ref[...], wa_ref[...])
        pb = _dot(b_ref[...], wb_ref[...])
        pa_ref[...] = pa.astype(pa_ref.dtype)
        pb_ref[...] = pb.astype(pb_ref.dtype)
        mg_ref[...] = (_sigmoid(ga_ref[...].astype(F32)) * pa
                       + _sigmoid(gb_ref[...].astype(F32)) * pb).astype(mg_ref.dtype)

    o = pl.BlockSpec((tm, tn), lambda j, i: (i, j))
    return _pallas(
        body, name="merge_fwd", grid=(d // tn, t // tm),
        in_specs=[pl.BlockSpec((tm, ya3.shape[1]), lambda j, i: (i, 0)),
                  pl.BlockSpec((wa.shape[0], tn), lambda j, i: (0, j)),
                  pl.BlockSpec((tm, yb.shape[1]), lambda j, i: (i, 0)),
                  pl.BlockSpec((wb.shape[0], tn), lambda j, i: (0, j)),
                  pl.BlockSpec((tm, tn), lambda j, i: (i, off_ga // tn + j)),
                  pl.BlockSpec((tm, tn), lambda j, i: (i, off_gb // tn + j))],
        out_specs=[o, o, o],
        out_shape=[jax.ShapeDtypeStruct((t, d), ACT), jax.ShapeDtypeStruct((t, d), ACT),
                   jax.ShapeDtypeStruct((t, d), BF16)],
        compiler_params=_cparams(("parallel", "parallel"), 56),
    )(ya3, wa, yb, wb, proj, proj)


def _out_fwd(merged, wout, x, ple_g):
    t, d = x.shape
    tm = 512

    def body(m_ref, w_ref, x_ref, g_ref, x1_ref, h2_ref):
        x1 = x_ref[...] + _dot(m_ref[...], w_ref[...])
        x1_ref[...] = x1
        r = lax.rsqrt(jnp.mean(x1 * x1, axis=-1, keepdims=True) + EPS)
        h2_ref[...] = (x1 * r * g_ref[...]).astype(h2_ref.dtype)

    rowblk = pl.BlockSpec((tm, d), lambda i: (i, 0))
    return _pallas(
        body, name="out_fwd", grid=(t // tm,),
        in_specs=[rowblk, pl.BlockSpec((d, d), lambda i: (0, 0)), rowblk, pl.BlockSpec((1, d), lambda i: (0, 0))],
        out_specs=[rowblk, rowblk],
        out_shape=[jax.ShapeDtypeStruct((t, d), F32), jax.ShapeDtypeStruct((t, d), BF16)],
        compiler_params=_cparams(("parallel",), 56),
    )(merged, wout, x, ple_g)


def _final(h2, wg, p, wup, x1, target, final_g):
    t, d = x1.shape
    pd = p.shape[1]
    tm = 256

    def body(h2_ref, wg_ref, p_ref, wup_ref, x1_ref, tg_ref, g_ref,
             dx2_ref, dgl_ref, dpu_ref, loss_ref, dfg_ref):
        @pl.when(pl.program_id(0) == 0)
        def _():
            loss_ref[...] = jnp.zeros_like(loss_ref)
            dfg_ref[...] = jnp.zeros_like(dfg_ref)

        gate = _sigmoid(_dot(h2_ref[...], wg_ref[...]))
        pu = _dot(p_ref[...].astype(BF16), wup_ref[...])
        x2 = x1_ref[...] + gate * pu
        r = lax.rsqrt(jnp.mean(x2 * x2, axis=-1, keepdims=True) + EPS)
        xh = x2 * r
        gfin = g_ref[...]
        err = xh * gfin - tg_ref[...]
        loss_ref[...] += _sum_all(err * err) * (0.5 / d)
        dy = err * (1.0 / d)
        dfg_ref[...] += jnp.sum(dy * xh, axis=0, keepdims=True)
        dg = dy * gfin
        dx2 = r * (dg - xh * jnp.mean(dg * xh, axis=-1, keepdims=True))
        dx2_ref[...] = dx2
        dgl_ref[...] = (dx2 * pu * gate * (1.0 - gate)).astype(dgl_ref.dtype)
        dpu_ref[...] = (dx2 * gate).astype(dpu_ref.dtype)

    rowblk = pl.BlockSpec((tm, d), lambda i: (i, 0))
    full = lambda a, b: pl.BlockSpec((a, b), lambda i: (0, 0))
    return _pallas(
        body, name="final_fwd_bwd", grid=(t // tm,),
        in_specs=[rowblk, full(d, d), pl.BlockSpec((tm, pd), lambda i: (i, 0)), full(pd, d), rowblk, rowblk,
                  full(1, d)],
        out_specs=[rowblk, rowblk, rowblk, full(1, LANES), full(1, d)],
        out_shape=[jax.ShapeDtypeStruct((t, d), F32), jax.ShapeDtypeStruct((t, d), BF16),
                   jax.ShapeDtypeStruct((t, d), BF16), jax.ShapeDtypeStruct((1, LANES), F32),
                   jax.ShapeDtypeStruct((1, d), F32)],
        compiler_params=_cparams(("arbitrary",), 56),
    )(h2, wg, p, wup, x1, target, final_g)


def _ple_bwd(dgl, wg, dx2, x1, ple_g):
    t, d = x1.shape
    tm = 512

    def body(dgl_ref, wg_ref, dx2_ref, x1_ref, g_ref, dx1_ref, dx1b_ref, dg_ref):
        @pl.when(pl.program_id(0) == 0)
        def _():
            dg_ref[...] = jnp.zeros_like(dg_ref)

        dh2 = _dot(dgl_ref[...], wg_ref[...], NT)
        x1 = x1_ref[...]
        r = lax.rsqrt(jnp.mean(x1 * x1, axis=-1, keepdims=True) + EPS)
        xh = x1 * r
        dg_ref[...] += jnp.sum(dh2 * xh, axis=0, keepdims=True)
        dg = dh2 * g_ref[...]
        dx1 = dx2_ref[...] + r * (dg - xh * jnp.mean(dg * xh, axis=-1, keepdims=True))
        dx1_ref[...] = dx1
        dx1b_ref[...] = dx1.astype(dx1b_ref.dtype)

    rowblk = pl.BlockSpec((tm, d), lambda i: (i, 0))
    full = lambda a, b: pl.BlockSpec((a, b), lambda i: (0, 0))
    return _pallas(
        body, name="ple_bwd", grid=(t // tm,),
        in_specs=[rowblk, full(d, d), rowblk, rowblk, full(1, d)],
        out_specs=[rowblk, rowblk, full(1, d)],
        out_shape=[jax.ShapeDtypeStruct((t, d), F32), jax.ShapeDtypeStruct((t, d), BF16),
                   jax.ShapeDtypeStruct((1, d), F32)],
        compiler_params=_cparams(("arbitrary",), 56),
    )(dgl, wg, dx2, x1, ple_g)


def _merge_bwd(dx1b, wout, proj, pa, pb, n_main, off_ga, off_gb):
    t, d = pa.shape
    tm = 512

    def body(dx_ref, w_ref, ga_ref, gb_ref, pa_ref, pb_ref, dpa_ref, dpb_ref, dgate_ref):
        dm = _dot(dx_ref[...], w_ref[...], NT)
        sa = _sigmoid(ga_ref[...].astype(F32))
        sb = _sigmoid(gb_ref[...].astype(F32))
        dpa_ref[...] = (dm * sa).astype(dpa_ref.dtype)
        dpb_ref[...] = (dm * sb).astype(dpb_ref.dtype)
        dgate_ref[:, :d] = (dm * pa_ref[...].astype(F32) * sa * (1.0 - sa)).astype(dgate_ref.dtype)
        dgate_ref[:, d:] = (dm * pb_ref[...].astype(F32) * sb * (1.0 - sb)).astype(dgate_ref.dtype)

    assert off_gb == off_ga + d and off_ga % (2 * d) == 0
    rowblk = pl.BlockSpec((tm, d), lambda i: (i, 0))
    return _pallas(
        body, name="merge_bwd", grid=(t // tm,),
        in_specs=[rowblk, pl.BlockSpec((d, d), lambda i: (0, 0)),
                  pl.BlockSpec((tm, d), lambda i: (i, off_ga // d)),
                  pl.BlockSpec((tm, d), lambda i: (i, off_gb // d)), rowblk, rowblk],
        out_specs=[rowblk, rowblk, pl.BlockSpec((tm, 2 * d), lambda i: (i, off_ga // (2 * d)))],
        out_shape=[jax.ShapeDtypeStruct((t, d), BF16), jax.ShapeDtypeStruct((t, d), BF16),
                   jax.ShapeDtypeStruct((t, n_main), BF16)],
        compiler_params=_cparams(("parallel",), 56),
    )(dx1b, wout, proj, proj, pa, pb)


def _yb_bwd(dproj, dpb, wb, proj, mixed, mix_b, scale, off_zp):
    t, dp = mixed.shape
    tm = 512

    def body(_, dpb_ref, w_ref, zp_ref, mx_ref, b_ref, s_ref, dzp_ref, dmx_ref, dsc_ref, db_ref):
        @pl.when(pl.program_id(0) == 0)
        def _():
            dsc_ref[...] = jnp.zeros_like(dsc_ref)
            db_ref[...] = jnp.zeros_like(db_ref)

        dyb = _dot(dpb_ref[...], w_ref[...], NT)
        zp = zp_ref[...].astype(F32)
        sg = _sigmoid(zp)
        mb = mx_ref[...].astype(F32) + b_ref[...]
        yb0 = mb * s_ref[...]
        dyb0 = dyb * (zp * sg)
        dzp_ref[...] = (dyb * yb0 * (sg * (1.0 + zp * (1.0 - sg)))).astype(dzp_ref.dtype)
        dsc_ref[...] += jnp.sum(dyb0 * mb, axis=0, keepdims=True)
        dmx = dyb0 * s_ref[...]
        db_ref[...] += jnp.sum(dmx, axis=0, keepdims=True)
        dmx_ref[...] = dmx.astype(dmx_ref.dtype)

    rowblk = pl.BlockSpec((tm, dp), lambda i: (i, 0))
    vec = pl.BlockSpec((1, dp), lambda i: (0, 0))
    seg = pl.BlockSpec((tm, dp), lambda i: (i, off_zp // dp))
    return _pallas(
        body, name="yb_bwd", grid=(t // tm,),
        in_specs=[ANY, pl.BlockSpec((tm, wb.shape[1]), lambda i: (i, 0)),
                  pl.BlockSpec(wb.shape, lambda i: (0, 0)), seg, rowblk, vec, vec],
        out_specs=[seg, rowblk, vec, vec],
        out_shape=[jax.ShapeDtypeStruct(dproj.shape, dproj.dtype), jax.ShapeDtypeStruct((t, dp), BF16),
                   jax.ShapeDtypeStruct((1, dp), F32), jax.ShapeDtypeStruct((1, dp), F32)],
        input_output_aliases={0: 0},
        compiler_params=_cparams(("arbitrary",), 56),
    )(dproj, dpb, wb, proj, mixed, mix_b, scale)


def _dpooled(dmixed, mixw):
    t, dp = dmixed.shape
    ng, pg = mixw.shape[0], mixw.shape[1]
    tm = min(1024, t)

    def body(a_ref, w_ref, o_ref):
        o_ref[...] = _dot(a_ref[...], w_ref[0], NT).astype(o_ref.dtype)

    blk = pl.BlockSpec((tm, pg), lambda i, g: (i, g))
    return _pallas(
        body, name="dpooled", grid=(t // tm, ng),
        in_specs=[blk, pl.BlockSpec((1, pg, pg), lambda i, g: (g, 0, 0))],
        out_specs=blk, out_shape=jax.ShapeDtypeStruct((t, dp), ACT),
        compiler_params=_cparams(("parallel", "parallel")),
    )(dmixed, mixw)


def _dmixw(pooled, dmixed, ng):
    t, dp = pooled.shape
    pg = dp // ng
    tk = min(2048, t)
    nk = t // tk

    def body(a_ref, b_ref, o_ref):
        @pl.when(pl.program_id(1) == 0)
        def _():
            o_ref[...] = jnp.zeros_like(o_ref)

        o_ref[0] += _dot(a_ref[...], b_ref[...], TN)

    blk = pl.BlockSpec((tk, pg), lambda g, k: (k, g))
    return _pallas(
        body, name="dmixw", grid=(ng, nk),
        in_specs=[blk, blk],
        out_specs=pl.BlockSpec((1, pg, pg), lambda g, k: (g, 0, 0)),
        out_shape=jax.ShapeDtypeStruct((ng, pg, pg), F32),
        compiler_params=_cparams(("parallel", "arbitrary")),
    )(pooled, dmixed)


def _pool_bwd(dproj, dpooled, nb, s, off_u):
    dp = dpooled.shape[1]
    pg = dp // len(POOL_WINDOWS)

    def body(_, d_ref, o_ref):
        g = pl.program_id(1)
        dv = d_ref[...].astype(F32)
        row = _iota(dv.shape, 0)
        pos = (_iota((dv.shape[0], 1), 0) + 1).astype(F32)
        for gi, w in enumerate(POOL_WINDOWS):
            @pl.when(g == gi)
            def _():
                acc, span = dv / jnp.minimum(pos, float(w)), 1
                while span < w:
                    acc = acc + _shift_up(acc, span, row)
                    span *= 2
                o_ref[...] = (acc - dv).astype(o_ref.dtype)

    return _pallas(
        body, name="pool_bwd", grid=(nb, len(POOL_WINDOWS)),
        in_specs=[ANY, pl.BlockSpec((s, pg), lambda b, g: (b, g))],
        out_specs=pl.BlockSpec((s, pg), lambda b, g: (b, off_u // pg + g)),
        out_shape=jax.ShapeDtypeStruct(dproj.shape, dproj.dtype),
        input_output_aliases={0: 0},
        compiler_params=_cparams(("parallel", "parallel"), 48),
    )(dproj, dpooled)


def _ssd_bwd(dproj, dya3, y, xbc, proj, dtraw, prev, dt_bias, a_log, dskip_full, gnorm, nb, s, d_inner, partials):
    l = CHUNK
    nc = s // l
    n_heads = d_inner // HEAD_DIM
    gw = d_inner // GROUPS
    pairs_per_group = gw // LANES
    gn_cols = GROUPS * D_STATE
    n_conv = d_inner + 2 * gn_cols
    last_step = nb * nc - 1

    xch = _ChipExchange(partials)
    n_in, n_out = 13, 7

    def body(*refs):
        (_, dya3_ref, y_ref, xs_ref, bm_ref, cm_ref, z_ref, dtr_ref, prev_ref, bias_ref, alog_ref,
         dsk_ref, gn_ref) = refs[:n_in]
        p_refs = refs[n_in:n_in + xch.n]
        outs = refs[n_in + xch.n:]
        dz_ref, dxbc_ref, ddt_ref, dgn_ref, dskh_ref, dalog_ref, dbias_ref = outs[:n_out]
        got_refs = outs[n_out:n_out + xch.n]
        g_ref, dskacc_ref = outs[n_out + xch.n:n_out + xch.n + 2]
        sems = outs[n_out + xch.n + 2:]
        step = pl.program_id(0) * nc + pl.program_id(1)

        @pl.when(pl.program_id(1) == 0)
        def _():
            g_ref[...] = jnp.zeros_like(g_ref)

        @pl.when(step == 0)
        def _():
            xch.start(p_refs, got_refs, sems)
            dgn_ref[...] = jnp.zeros_like(dgn_ref)
            dalog_ref[...] = jnp.zeros_like(dalog_ref)
            dbias_ref[...] = jnp.zeros_like(dbias_ref)
            dskacc_ref[...] = jnp.zeros_like(dskacc_ref)

        xs = xs_ref[...].astype(F32)
        zv = z_ref[...].astype(F32)
        dsk = dsk_ref[...]
        sg = _sigmoid(zv)
        sz = zv * sg
        ya1 = y_ref[...].astype(F32) + xs * dsk
        ya2 = ya1 * sz
        parts = []
        for g in range(GROUPS):
            gsl = slice(g * gw, (g + 1) * gw)
            blk = ya2[:, gsl]
            rg = lax.rsqrt(jnp.mean(blk * blk, axis=-1, keepdims=True) + EPS)
            xh = blk * rg
            d3 = dya3_ref[:, gsl].astype(F32)
            dgn_ref[:, gsl] += jnp.sum(d3 * xh, axis=0, keepdims=True)
            dg = d3 * gn_ref[:, gsl]
            parts.append(rg * (dg - xh * jnp.mean(dg * xh, axis=-1, keepdims=True)))
        dya2 = jnp.concatenate(parts, axis=1)
        dy = dya2 * sz
        dz_ref[...] = (dya2 * ya1 * (sg * (1.0 + zv * (1.0 - sg)))).astype(dz_ref.dtype)
        dskacc_ref[...] += jnp.sum(dy * xs, axis=0, keepdims=True)

        dtraw_v = dtr_ref[...]
        bias = bias_ref[...]
        dt, apad, acs, dt_full, acs_full, tri, ex = _ssd_common(dtraw_v, bias, alog_ref[...], n_heads)
        acs_t = acs.T
        e_full = jnp.exp(acs_full)
        last = acs_full[l - 1:l, :]
        dte = jnp.exp(last - acs_full)
        cd = jnp.exp(last)
        xdt = xs * dt_full
        xd = xdt * dte
        dy_e = dy * e_full
        mask = _iota((l, l), 1) <= _iota((l, l), 0)
        lane = _iota((l, LANES), 1)
        hlane = _iota((l, HEAD_PAD), 1)
        hsub = _iota((HEAD_PAD, l), 0)
        dacs = jnp.zeros((l, HEAD_PAD), F32)
        dacs_t = jnp.zeros((HEAD_PAD, l), F32)
        dacs_full, dlast, dxdt_all, db_all, dc_all = [], [], [], [], []
        for g in range(GROUPS):
            gsl = slice(g * gw, (g + 1) * gw)
            bg = bm_ref[:, g * D_STATE:(g + 1) * D_STATE].astype(BF16)
            cg = cm_ref[:, g * D_STATE:(g + 1) * D_STATE].astype(BF16)
            prev_g = prev_ref[0, :, gsl]
            prev_b = prev_g.astype(BF16)
            gst = g_ref[:, gsl]
            gst_b = gst.astype(BF16)
            dye_b = dy_e[:, gsl].astype(BF16)
            dc = _dot(dye_b, prev_b, NT)
            dprev = _dot(cg, dye_b, TN)
            y_off = _dot(cg, prev_b) * e_full[:, gsl]
            xd_g = xd[:, gsl]
            db = _dot(xd_g.astype(BF16), gst_b, NT)
            dxd = _dot(bg, gst_b)
            wgt = dxd * xd_g
            dacs_full.append(dy[:, gsl] * y_off - wgt)
            dlast.append(jnp.sum(wgt, axis=0, keepdims=True)
                         + jnp.sum(prev_g * gst, axis=0, keepdims=True) * cd[:, gsl])
            g_ref[:, gsl] = dprev + cd[:, gsl] * gst
            dxdt_g = dxd * dte[:, gsl]
            cb = _dot(cg, bg, NT)
            dcb = jnp.zeros((l, l), F32)
            for q in range(pairs_per_group):
                pair = g * pairs_per_group + q
                psl = slice(pair * LANES, (pair + 1) * LANES)
                xpb = xdt[:, psl].astype(BF16)
                dyp = dy[:, psl].astype(BF16)
                zero = jnp.zeros_like(dyp)
                halves = (jnp.where(lane < HEAD_DIM, dyp, zero), jnp.where(lane >= HEAD_DIM, dyp, zero))
                acc = dxdt_g[:, q * LANES:(q + 1) * LANES]
                for k, dyh in enumerate(halves):
                    h = 2 * pair + k
                    dk = _decay_mats(acs, acs_t, h, mask)
                    mm = cb * dk
                    dm = _dot(dyh, xpb, NT)
                    dcb = dcb + dm * dk
                    dseg = dm * mm
                    dacs = dacs + jnp.where(hlane == h, jnp.sum(dseg, axis=1, keepdims=True), 0.0)
                    dacs_t = dacs_t + jnp.where(hsub == h, jnp.sum(dseg, axis=0, keepdims=True), 0.0)
                    acc = acc + _dot(mm.astype(BF16), dyh, TN)
                dxdt_all.append(acc)
            dcb_b = dcb.astype(BF16)
            dc_all.append(dc + _dot(dcb_b, bg))
            db_all.append(db + _dot(dcb_b, cg, TN))

        dxdt = jnp.concatenate(dxdt_all, axis=1)
        dxbc_ref[:, :d_inner] = (dy * dsk + dxdt * dt_full).astype(dxbc_ref.dtype)
        dxbc_ref[:, d_inner:d_inner + gn_cols] = jnp.concatenate(db_all, axis=1).astype(dxbc_ref.dtype)
        dxbc_ref[:, d_inner + gn_cols:] = jnp.concatenate(dc_all, axis=1).astype(dxbc_ref.dtype)
        ddt = _exact_r(dxdt * xs, ex, NT)
        rowi = _iota((l, d_inner), 0)
        dacs_f = jnp.concatenate(dacs_full, axis=1) + jnp.where(rowi == l - 1, jnp.concatenate(dlast, axis=1), 0.0)
        dacs = dacs + _exact_r(dacs_f, ex, NT) - dacs_t.T
        da = _exact_l(tri, dacs, TN)
        ddt = ddt + da * apad
        dalog_ref[...] += jnp.sum(da * dt, axis=0, keepdims=True) * apad
        ddt_raw = ddt * _sigmoid(dtraw_v + bias)
        dbias_ref[...] += jnp.sum(ddt_raw, axis=0, keepdims=True)
        ddt_ref[...] = ddt_raw.astype(ddt_ref.dtype)

        @pl.when(step == last_step)
        def _():
            dskh_ref[...] = _exact_r(dskacc_ref[...], ex, NT)[0:1, :]
            xch.wait(p_refs, got_refs, sems)

    row = lambda b, c: b * nc + (nc - 1 - c)
    small = lambda n: pl.BlockSpec((1, n), lambda b, c: (0, 0))
    wide = pl.BlockSpec((l, d_inner), lambda b, c: (row(b, c), 0))
    return _pallas(
        body, name="ssd_bwd", grid=(nb, nc),
        in_specs=[ANY, wide, wide, wide,
                  pl.BlockSpec((l, gn_cols), lambda b, c: (row(b, c), d_inner // gn_cols)),
                  pl.BlockSpec((l, gn_cols), lambda b, c: (row(b, c), d_inner // gn_cols + 1)),
                  wide, pl.BlockSpec((l, HEAD_PAD), lambda b, c: (row(b, c), 0)),
                  pl.BlockSpec((1, D_STATE, d_inner), lambda b, c: (row(b, c), 0, 0)),
                  small(HEAD_PAD), small(HEAD_PAD), small(d_inner), small(d_inner)] + xch.in_specs,
        out_specs=[wide, pl.BlockSpec((l, n_conv), lambda b, c: (row(b, c), 0)),
                   pl.BlockSpec((l, HEAD_PAD), lambda b, c: (row(b, c), 0)),
                   small(d_inner), small(HEAD_PAD), small(HEAD_PAD), small(HEAD_PAD)] + xch.out_specs,
        out_shape=[jax.ShapeDtypeStruct(dproj.shape, dproj.dtype),
                   jax.ShapeDtypeStruct((nb * s, n_conv), ACT),
                   jax.ShapeDtypeStruct((nb * s, HEAD_PAD), BF16),
                   jax.ShapeDtypeStruct((1, d_inner), F32), jax.ShapeDtypeStruct((1, HEAD_PAD), F32),
                   jax.ShapeDtypeStruct((1, HEAD_PAD), F32), jax.ShapeDtypeStruct((1, HEAD_PAD), F32)]
        + xch.out_shape,
        scratch_shapes=[pltpu.VMEM((D_STATE, d_inner), F32), pltpu.VMEM((8, d_inner), F32)] + xch.scratch,
        input_output_aliases={0: 0},
        compiler_params=_cparams(("arbitrary", "arbitrary"), 56),
    )(dproj, dya3, y, xbc, xbc, xbc, proj, dtraw, prev, dt_bias, a_log, dskip_full, gnorm, *partials)


def _conv_bwd(dproj, dxbc, proj, conv_w, conv_b, nb, s, off_xbc):
    n_conv = dxbc.shape[1]
    cw = 256

    rc = CONV_ROWS

    def fold(v):
        return jnp.sum(v.reshape(rc // SUBLANES, SUBLANES, cw), axis=0)

    def body(_, d_ref, x_ref, w_ref, b_ref, dx_ref, dw_ref, db_ref, xpad_ref, dcpad_ref):
        @pl.when(pl.program_id(1) == 0)
        def _():
            dw_ref[...] = jnp.zeros_like(dw_ref)
            db_ref[...] = jnp.zeros_like(db_ref)

        xpad_ref[0:SUBLANES, :] = jnp.zeros((SUBLANES, cw), F32)
        xpad_ref[SUBLANES:, :] = x_ref[...].astype(F32)
        dcpad_ref[pl.ds(s, SUBLANES), :] = jnp.zeros((SUBLANES, cw), F32)
        taps = [w_ref[k:k + 1, :] for k in range(CONV_W)]
        bias = b_ref[...]
        db = jnp.zeros((SUBLANES, cw), F32)
        dw = [jnp.zeros((SUBLANES, cw), F32) for _ in range(CONV_W)]
        for r in range(s // rc):
            shifted = [xpad_ref[pl.ds(SUBLANES + r * rc - j, rc), :] for j in range(CONV_W)]
            acc = bias
            for j in range(CONV_W):
                acc = acc + shifted[j] * taps[CONV_W - 1 - j]
            sg = _sigmoid(acc)
            dc = d_ref[pl.ds(r * rc, rc), :].astype(F32) * (sg * (1.0 + acc * (1.0 - sg)))
            dcpad_ref[pl.ds(r * rc, rc), :] = dc
            db = db + fold(dc)
            for j in range(CONV_W):
                dw[CONV_W - 1 - j] = dw[CONV_W - 1 - j] + fold(dc * shifted[j])
        db_ref[...] += jnp.sum(db, axis=0, keepdims=True)
        for k in range(CONV_W):
            dw_ref[k:k + 1, :] += jnp.sum(dw[k], axis=0, keepdims=True)
        for r in range(s // rc):
            dxv = dcpad_ref[pl.ds(r * rc, rc), :] * taps[CONV_W - 1]
            for j in range(1, CONV_W):
                dxv = dxv + dcpad_ref[pl.ds(r * rc + j, rc), :] * taps[CONV_W - 1 - j]
            dx_ref[pl.ds(r * rc, rc), :] = dxv.astype(dx_ref.dtype)

    return _pallas(
        body, name="conv_bwd", grid=(n_conv // cw, nb),
        in_specs=[ANY, pl.BlockSpec((s, cw), lambda j, b: (b, j)),
                  pl.BlockSpec((s, cw), lambda j, b: (b, off_xbc // cw + j)),
                  pl.BlockSpec((CONV_W, cw), lambda j, b: (0, j)), pl.BlockSpec((1, cw), lambda j, b: (0, j))],
        out_specs=[pl.BlockSpec((s, cw), lambda j, b: (b, off_xbc // cw + j)),
                   pl.BlockSpec((CONV_W, cw), lambda j, b: (0, j)), pl.BlockSpec((1, cw), lambda j, b: (0, j))],
        out_shape=[jax.ShapeDtypeStruct(dproj.shape, dproj.dtype),
                   jax.ShapeDtypeStruct((CONV_W, n_conv), F32), jax.ShapeDtypeStruct((1, n_conv), F32)],
        scratch_shapes=[pltpu.VMEM((s + SUBLANES, cw), F32), pltpu.VMEM((s + SUBLANES, cw), F32)],
        input_output_aliases={0: 0},
        compiler_params=_cparams(("parallel", "arbitrary"), 48),
    )(dproj, dxbc, proj, conv_w, conv_b)


def _in_bwd(dproj, w_main, ddt, w_dt, x, norm_g, dx1, partials):
    t, d = x.shape
    n_main = dproj.shape[1]
    tm, tk = min(1024, t), 1024
    nk = n_main // tk
    ni = t // tm
    xch = _ChipExchange(partials)
    n_in, n_out = 7, 2

    def body(*refs):
        dp_ref, w_ref, ddt_ref, wdt_ref, x_ref, g_ref, dx1_ref = refs[:n_in]
        p_refs = refs[n_in:n_in + xch.n]
        outs = refs[n_in + xch.n:]
        dx_ref, dg_ref = outs[:n_out]
        got_refs = outs[n_out:n_out + xch.n]
        acc_ref = outs[n_out + xch.n]
        sems = outs[n_out + xch.n + 1:]
        i, kk = pl.program_id(0), pl.program_id(1)

        @pl.when(kk == 0)
        def _():
            acc_ref[...] = jnp.zeros_like(acc_ref)

        @pl.when((kk == 0) & (i == 0))
        def _():
            xch.start(p_refs, got_refs, sems)
            dg_ref[...] = jnp.zeros_like(dg_ref)

        acc_ref[...] += _dot(dp_ref[...], w_ref[...], NT)

        @pl.when(kk == nk - 1)
        def _():
            dh = acc_ref[...] + _dot(ddt_ref[...], wdt_ref[...], NT)
            xv = x_ref[...]
            r = lax.rsqrt(jnp.mean(xv * xv, axis=-1, keepdims=True) + EPS)
            xh = xv * r
            dg_ref[...] += jnp.sum(dh * xh, axis=0, keepdims=True)
            dg = dh * g_ref[...]
            dx_ref[...] = dx1_ref[...] + r * (dg - xh * jnp.mean(dg * xh, axis=-1, keepdims=True))

        @pl.when((kk == nk - 1) & (i == ni - 1))
        def _():
            xch.wait(p_refs, got_refs, sems)

    rowblk = pl.BlockSpec((tm, d), lambda i, k: (i, 0))
    vec = pl.BlockSpec((1, d), lambda i, k: (0, 0))
    return _pallas(
        body, name="in_bwd", grid=(ni, nk),
        in_specs=[pl.BlockSpec((tm, tk), lambda i, k: (i, k)), pl.BlockSpec((d, tk), lambda i, k: (0, k)),
                  pl.BlockSpec((tm, HEAD_PAD), lambda i, k: (i, 0)), pl.BlockSpec((d, HEAD_PAD), lambda i, k: (0, 0)),
                  rowblk, vec, rowblk] + xch.in_specs,
        out_specs=[rowblk, vec] + xch.out_specs,
        out_shape=[jax.ShapeDtypeStruct((t, d), F32), jax.ShapeDtypeStruct((1, d), F32)] + xch.out_shape,
        scratch_shapes=[pltpu.VMEM((tm, d), F32)] + xch.scratch,
        compiler_params=_cparams(("arbitrary", "arbitrary"), 56),
    )(dproj, w_main, ddt, w_dt, x, norm_g, dx1, *partials)


def _adamw_math(w, g, m, v):
    m = ADAM_B1 * m + (1.0 - ADAM_B1) * g
    v = ADAM_B2 * v + (1.0 - ADAM_B2) * (g * g)
    m_hat = m / (1.0 - ADAM_B1 ** ADAM_STEP)
    v_hat = v / (1.0 - ADAM_B2 ** ADAM_STEP)
    delta = -ADAM_LR * (m_hat / (jnp.sqrt(v_hat) + ADAM_EPS) + ADAM_WD * w)
    return delta, m, v


def _adamw(parts, w, m, v, name):
    n, rows, cols = parts.shape
    tr = _row_tile(rows, 128) if rows % 8 == 0 else rows

    def body(p_ref, w_ref, m_ref, v_ref, g_ref, d_ref, mo_ref, vo_ref):
        g = p_ref[0].astype(F32)
        for k in range(1, n):
            g = g + p_ref[k].astype(F32)
        d, mn, vn = _adamw_math(w_ref[...], g, m_ref[...], v_ref[...])
        g_ref[...] = g
        d_ref[...] = d
        mo_ref[...] = mn
        vo_ref[...] = vn

    blk = pl.BlockSpec((tr, cols), lambda i: (i, 0))
    shp = jax.ShapeDtypeStruct((rows, cols), F32)
    return _pallas(
        body, name=name, grid=(rows // tr,),
        in_specs=[pl.BlockSpec((n, tr, cols), lambda i: (0, i, 0)), blk, blk, blk],
        out_specs=[blk, blk, blk, blk], out_shape=[shp, shp, shp, shp],
        compiler_params=_cparams(("parallel",), 56),
    )(parts, w, m, v)


def _coords():
    return lax.axis_index("x"), lax.axis_index("y"), lax.axis_index("c")


def _all_gather_two_level(blocks, name):
    n = len(blocks)

    def body(*refs):
        x_refs, out_refs = refs[:n], refs[n:2 * n]
        send_sems, recv_sems, local_sems = refs[2 * n:]
        x, y, c = _coords()
        me, sibling = (x, y, c), (x, y, 1 - c)
        chips = [(1 - x, y), (x, 1 - y), (1 - x, 1 - y)]

        def slot(a, px, py, pc):
            return out_refs[a].at[4 * px + 2 * py + pc]

        def copy(a, k, block, to, src=None):
            return pltpu.make_async_remote_copy(
                src_ref=slot(a, *block) if src is None else src, dst_ref=slot(a, *block),
                send_sem=send_sems.at[a, k], recv_sem=recv_sems.at[a, k], device_id=to, device_id_type=MESH)

        local = [pltpu.make_async_copy(x_refs[a], slot(a, *me), local_sems.at[a]) for a in range(n)]
        first = []
        for a in range(n):
            first.append(copy(a, 0, me, sibling, src=x_refs[a]))
            first += [copy(a, 1 + j, me, (*chip, c), src=x_refs[a]) for j, chip in enumerate(chips)]
        for cp in local + first:
            cp.start()
        passed = []
        for j, chip in enumerate(chips):
            for a in range(n):
                copy(a, 1 + j, (*chip, c), me).wait_recv()
                passed.append(copy(a, 4 + j, (*chip, c), sibling))
                passed[-1].start()
        for a in range(n):
            copy(a, 0, sibling, me).wait_recv()
            for j, chip in enumerate(chips):
                copy(a, 4 + j, (*chip, 1 - c), me).wait_recv()
        for cp in first + passed:
            cp.wait_send()
        for cp in local:
            cp.wait()

    return _pallas(
        body, name=name, in_specs=[ANY] * n, out_specs=[ANY] * n,
        out_shape=[jax.ShapeDtypeStruct((N_DEV,) + b.shape, b.dtype) for b in blocks],
        scratch_shapes=[pltpu.SemaphoreType.DMA((n, 7)), pltpu.SemaphoreType.DMA((n, 7)),
                        pltpu.SemaphoreType.DMA((n,))],
    )(*blocks)


def _all_gather_direct(blk, name):
    rows, cols = blk.shape

    def body(x_ref, out_ref, send_sems, recv_sems, local_sem):
        x, y, c = _coords()
        me = 4 * x + 2 * y + c
        mine = pltpu.make_async_copy(x_ref, out_ref.at[me], local_sem)
        mine.start()
        copies = []
        for k in range(1, N_DEV):
            fx, fy, fc = (k >> 2) & 1, (k >> 1) & 1, k & 1
            peer = (x ^ fx, y ^ fy, c ^ fc)
            copies.append(pltpu.make_async_remote_copy(
                src_ref=x_ref, dst_ref=out_ref.at[me], send_sem=send_sems.at[k - 1], recv_sem=recv_sems.at[k - 1],
                device_id=peer, device_id_type=MESH))
        for cp in copies:
            cp.start()
        for k in range(1, N_DEV):
            fx, fy, fc = (k >> 2) & 1, (k >> 1) & 1, k & 1
            src = 4 * (x ^ fx) + 2 * (y ^ fy) + (c ^ fc)
            pltpu.make_async_remote_copy(
                src_ref=x_ref, dst_ref=out_ref.at[src], send_sem=send_sems.at[k - 1], recv_sem=recv_sems.at[k - 1],
                device_id=(x, y, c), device_id_type=MESH).wait_recv()
        for cp in copies:
            cp.wait_send()
        mine.wait()

    return _pallas(
        body, name=name, in_specs=[ANY], out_specs=ANY,
        out_shape=jax.ShapeDtypeStruct((N_DEV, rows, cols), blk.dtype),
        scratch_shapes=[pltpu.SemaphoreType.DMA((7,)), pltpu.SemaphoreType.DMA((7,)), pltpu.SemaphoreType.DMA],
    )(blk)


def _pair_exchange(grads, name):
    n = len(grads)

    def body(*refs):
        g_refs, out_refs = refs[:n], refs[n:2 * n]
        send_sems, recv_sems = refs[2 * n:]
        x, y, c = _coords()
        copies = []
        for a in range(n):
            for k in range(4):
                copies.append(pltpu.make_async_remote_copy(
                    src_ref=g_refs[a].at[2 * k + (1 - c)], dst_ref=out_refs[a].at[k],
                    send_sem=send_sems.at[a, k], recv_sem=recv_sems.at[a, k],
                    device_id=(x, y, 1 - c), device_id_type=MESH))
        for cp in copies:
            cp.start()
        for cp in copies:
            cp.wait()

    return _pallas(
        body, name=name, in_specs=[ANY] * n, out_specs=[ANY] * n,
        out_shape=[jax.ShapeDtypeStruct((4,) + g.shape[1:], g.dtype) for g in grads],
        scratch_shapes=[pltpu.SemaphoreType.DMA((n, 4)), pltpu.SemaphoreType.DMA((n, 4))],
    )(*grads)


def _pair_add(grads, got, core, out_dtype, name):
    _, rows, cols = grads.shape
    tr = _row_tile(rows, 256) if rows % 8 == 0 else rows

    def body(c_ref, a_ref, b_ref, o_ref):
        o_ref[...] = (a_ref[...] + b_ref[...]).astype(o_ref.dtype)

    grid_spec = pltpu.PrefetchScalarGridSpec(
        num_scalar_prefetch=1, grid=(4, rows // tr),
        in_specs=[pl.BlockSpec((1, tr, cols), lambda k, i, c_ref: (2 * k + c_ref[0], i, 0)),
                  pl.BlockSpec((1, tr, cols), lambda k, i, c_ref: (k, i, 0))],
        out_specs=pl.BlockSpec((1, tr, cols), lambda k, i, c_ref: (k, i, 0)))
    return _pallas(
        body, name=name, grid_spec=grid_spec,
        out_shape=jax.ShapeDtypeStruct((4, rows, cols), out_dtype),
        compiler_params=_cparams(("parallel", "parallel"), 48),
    )(core, grads, got)


class _ChipExchange:
    def __init__(self, partials):
        self.n = len(partials)
        self.in_specs = [ANY] * self.n
        self.out_specs = [ANY] * self.n
        self.out_shape = [jax.ShapeDtypeStruct(p.shape, p.dtype) for p in partials]
        self.scratch = [pltpu.SemaphoreType.DMA((self.n, 3)), pltpu.SemaphoreType.DMA((self.n, 3)),
                        pltpu.SemaphoreType.DMA((self.n,))]

    def _copies(self, p_refs, out_refs, sems, receiving):
        send_sems, recv_sems, local_sems = sems
        x, y, c = _coords()
        mychip = 2 * x + y
        local = [pltpu.make_async_copy(p_refs[a].at[mychip], out_refs[a].at[mychip], local_sems.at[a])
                 for a in range(self.n)]
        remote = []
        for a in range(self.n):
            for k in range(1, 4):
                px, py = x ^ ((k >> 1) & 1), y ^ (k & 1)
                peer = 2 * px + py
                remote.append(pltpu.make_async_remote_copy(
                    src_ref=p_refs[a].at[peer], dst_ref=out_refs[a].at[peer if receiving else mychip],
                    send_sem=send_sems.at[a, k - 1], recv_sem=recv_sems.at[a, k - 1],
                    device_id=(x, y, c) if receiving else (px, py, c), device_id_type=MESH))
        return local, remote

    def start(self, p_refs, out_refs, sems):
        local, sends = self._copies(p_refs, out_refs, sems, False)
        for cp in local + sends:
            cp.start()

    def wait(self, p_refs, out_refs, sems):
        local, recvs = self._copies(p_refs, out_refs, sems, True)
        for cp in recvs:
            cp.wait_recv()
        for cp in recvs:
            cp.wait_send()
        for cp in local:
            cp.wait()


def _pad_lanes(v, n=HEAD_PAD):
    return jnp.pad(v, ((0, 0), (0, n - v.shape[1])))


def kernel(x, p, norm_g, w_in, conv_w, conv_b, dt_bias, a_log, d_skip, gnorm_g, pool_mix_w, pool_mix_b, pool_scale, w_branch_a, w_branch_b, w_out, ple_norm_g, w_ple_gate, w_ple_up, final_g, loss_target, m_norm_g, m_w_in, m_conv_w, m_conv_b, m_dt_bias, m_a_log, m_d_skip, m_gnorm_g, m_pool_mix_w, m_pool_mix_b, m_pool_scale, m_w_branch_a, m_w_branch_b, m_w_out, m_ple_norm_g, m_w_ple_gate, m_w_ple_up, m_final_g, v_norm_g, v_w_in, v_conv_w, v_conv_b, v_dt_bias, v_a_log, v_d_skip, v_gnorm_g, v_pool_mix_w, v_pool_mix_b, v_pool_scale, v_w_branch_a, v_w_branch_b, v_w_out, v_ple_norm_g, v_w_ple_gate, v_w_ple_up, v_final_g):
    nb, s, d = x.shape
    t = nb * s
    n_heads = dt_bias.shape[1]
    d_inner = n_heads * HEAD_DIM
    n_conv = conv_b.shape[1]
    d_pool = pool_mix_b.shape[1]
    n_in = w_in.shape[2] * N_DEV
    ple_dim = p.shape[-1]
    o_xbc, o_dt = d_inner, d_inner + n_conv
    o_u = o_dt + n_heads
    o_zp = o_u + d_pool
    o_ga, o_gb = o_zp + d_pool, o_zp + d_pool + d
    n_main = n_in - n_heads
    off_ga, off_gb, off_xbc = d_inner, d_inner + d, d_inner + 2 * d
    off_u = off_xbc + n_conv
    off_zp = off_u + d_pool

    sharded = [w_in[0], conv_w[0], pool_mix_w[0], w_branch_a[0], w_branch_b[0], w_out[0], w_ple_gate[0], w_ple_up[0]]
    sharded_m = [m_w_in[0], m_conv_w[0], m_pool_mix_w[0], m_w_branch_a[0], m_w_branch_b[0], m_w_out[0],
                 m_w_ple_gate[0], m_w_ple_up[0]]
    sharded_v = [v_w_in[0], v_conv_w[0], v_pool_mix_w[0], v_w_branch_a[0], v_w_branch_b[0], v_w_out[0],
                 v_w_ple_gate[0], v_w_ple_up[0]]
    ng = pool_mix_w.shape[1]
    pg = d_pool // ng
    two_d = lambda w: w.reshape(-1, w.shape[-1])
    sharded, sharded_m, sharded_v = ([two_d(w) for w in ws] for ws in (sharded, sharded_m, sharded_v))

    send = [w if i == 1 else w.astype(BF16) for i, w in enumerate(sharded)]
    seg = _all_gather_two_level(send, "weights_all_gather")
    chunk = n_in // N_DEV
    segments = [((0, o_xbc), 0), ((o_ga, o_gb), off_ga), ((o_gb, n_in), off_gb), ((o_xbc, o_dt), off_xbc),
                ((o_u, o_zp), off_u), ((o_zp, o_ga), off_zp)]

    def gathered_cols(lo, hi):
        out = []
        while lo < hi:
            dev = lo // chunk
            a, b = lo - dev * chunk, min(hi - dev * chunk, chunk)
            out.append(seg[0][dev, :, a:b])
            lo = dev * chunk + b
        return out

    w_main = jnp.concatenate([pc for (lo, hi), _ in segments for pc in gathered_cols(lo, hi)], axis=1)
    w_dt = _pad_lanes(jnp.concatenate(gathered_cols(o_dt, o_u), axis=1))
    cw_full = seg[1].transpose(1, 0, 2).reshape(CONV_W, n_conv)
    mixw_full = seg[2].reshape(N_DEV, ng, pg // N_DEV, pg).transpose(1, 0, 2, 3).reshape(ng, pg, pg)
    wa_full = seg[3].reshape(d_inner, d)
    wb_full = seg[4].reshape(d_pool, d)
    wout_full = seg[5].reshape(d, d)
    wg_full = seg[6].reshape(d, d)
    wup_full = seg[7].transpose(1, 0, 2).reshape(ple_dim, d)

    x2 = x.reshape(t, d)
    p2 = p.reshape(t, ple_dim)
    tgt = loss_target.reshape(t, d)
    bias_p, alog_p = _pad_lanes(dt_bias), _pad_lanes(a_log)
    dskip_full = jnp.repeat(d_skip, HEAD_DIM, axis=1)

    h = _rms_fwd(x2, norm_g, "rms_in")
    proj = _matmul(h, w_main, "nn", ACT, "in_proj", 1024, 1024, d)
    dtraw = _matmul(h, w_dt, "nn", F32, "in_proj_dt", 1024, HEAD_PAD, d)
    xbc = _conv_fwd(proj, cw_full, conv_b, nb, s, off_xbc, n_conv)
    y, ya3, prev = _ssd_fwd(xbc, proj, dtraw, bias_p, alog_p, dskip_full, gnorm_g, nb, s, d_inner)
    pooled = _pool_fwd(proj, nb, s, off_u, d_pool)
    mixed, yb = _mix_fwd(pooled, mixw_full, proj, pool_mix_b, pool_scale, off_zp)
    pa, pb, merged = _merge_fwd(ya3, wa_full, yb, wb_full, proj, off_ga, off_gb)
    x1, h2 = _out_fwd(merged, wout_full, x2, ple_norm_g)
    dx2, dgl, dpu, loss_part, dfinal_g = _final(h2, wg_full, p2, wup_full, x1, tgt, final_g.reshape(1, d))

    g_wup = _matmul(p2, dpu, "tn", F32, "dw_ple_up", 256, 512, t)
    g_wg = _matmul(h2, dgl, "tn", F32, "dw_ple_gate", 512, 512, t)
    dx1, dx1b, d_ple_g = _ple_bwd(dgl, wg_full, dx2, x1, ple_norm_g)
    g_wout = _matmul(merged, dx1b, "tn", F32, "dw_out", 512, 512, t)
    dpa, dpb, dproj = _merge_bwd(dx1b, wout_full, proj, pa, pb, n_main, off_ga, off_gb)
    g_wa = _matmul(ya3, dpa, "tn", F32, "dw_branch_a", 512, 512, t)
    g_wb = _matmul(yb, dpb, "tn", F32, "dw_branch_b", 512, 512, t)
    dya3 = _matmul(dpa, wa_full, "nt", ACT, "d_ya3", 1024, 1024, d)
    dproj, dmixed, d_scale, d_mixb = _yb_bwd(dproj, dpb, wb_full, proj, mixed, pool_mix_b, pool_scale, off_zp)
    g_mixw = _dmixw(pooled, dmixed, ng)
    dpool = _dpooled(dmixed, mixw_full)
    dproj = _pool_bwd(dproj, dpool, nb, s, off_u)

    core = lax.axis_index("c").astype(I32).reshape(1)

    def pair_sums(by_dest, names):
        got = _pair_exchange(by_dest, "rs_pair_exchange_" + names[0])
        return [_pair_add(g, r, core, BF16, "rs_pair_add_" + nm) for g, r, nm in zip(by_dest, got, names)]

    names_a = ["pool_mix_w", "w_branch_a", "w_branch_b", "w_out", "w_ple_gate", "w_ple_up"]
    partials_a = pair_sums([
        g_mixw.reshape(ng, N_DEV, pg // N_DEV, pg).transpose(1, 0, 2, 3).reshape(N_DEV, ng * pg // N_DEV, pg),
        g_wa.reshape(N_DEV, d_inner // N_DEV, d), g_wb.reshape(N_DEV, d_pool // N_DEV, d),
        g_wout.reshape(N_DEV, d // N_DEV, d), g_wg.reshape(N_DEV, d // N_DEV, d),
        g_wup.reshape(ple_dim, N_DEV, d // N_DEV).transpose(1, 0, 2)], names_a)
    dproj, dxbc, ddt, d_gn, d_dsk, d_alog, d_dtb, *parts_a = _ssd_bwd(
        dproj, dya3, y, xbc, proj, dtraw, prev, bias_p, alog_p, dskip_full, gnorm_g, nb, s, d_inner, partials_a)
    dproj, g_cw, d_cb = _conv_bwd(dproj, dxbc, proj, cw_full, conv_b, nb, s, off_xbc)
    g_wmain = _matmul(h, dproj, "tn", F32, "dw_in", 512, 1024, t)
    g_wdt = _matmul(h, ddt, "tn", F32, "dw_in_dt", 512, HEAD_PAD, t)

    def dest_cols(dev):
        lo, hi = dev * chunk, (dev + 1) * chunk
        srcs = [((olo, ohi), g_wmain, off) for (olo, ohi), off in segments] + [((o_dt, o_u), g_wdt, 0)]
        out = []
        for (olo, ohi), arr, off in sorted(srcs, key=lambda e: e[0][0]):
            a, b = max(lo, olo), min(hi, ohi)
            if a < b:
                out.append(arr[:, off + a - olo:off + b - olo])
        return jnp.concatenate(out, axis=1)

    names_b = ["w_in", "conv_w"]
    partials_b = pair_sums([jnp.stack([dest_cols(dev) for dev in range(N_DEV)], axis=0),
                            g_cw.reshape(CONV_W, N_DEV, n_conv // N_DEV).transpose(1, 0, 2)], names_b)
    grad_x, d_ng, *parts_b = _in_bwd(dproj, w_main, ddt, w_dt, x2, norm_g, dx1, partials_b)

    names = names_b + names_a
    parts = parts_b + parts_a
    sh_g, sh_d, sh_m, sh_v = [], [], [], []
    shapes = [w_in.shape, conv_w.shape, pool_mix_w.shape, w_branch_a.shape, w_branch_b.shape, w_out.shape,
              w_ple_gate.shape, w_ple_up.shape]
    for pt, w, m, v, nm, shp in zip(parts, sharded, sharded_m, sharded_v, names, shapes):
        res = _adamw(pt, w, m, v, "adamw_" + nm)
        for dst, r in zip((sh_g, sh_d, sh_m, sh_v), res):
            dst.append(r.reshape(shp))

    rep_w = [norm_g, conv_b, gnorm_g, pool_mix_b, pool_scale, ple_norm_g, final_g.reshape(1, d),
             bias_p, alog_p, _pad_lanes(d_skip)]
    rep_m = [m_norm_g, m_conv_b, m_gnorm_g, m_pool_mix_b, m_pool_scale, m_ple_norm_g, m_final_g.reshape(1, d),
             _pad_lanes(m_dt_bias), _pad_lanes(m_a_log), _pad_lanes(m_d_skip)]
    rep_v = [v_norm_g, v_conv_b, v_gnorm_g, v_pool_mix_b, v_pool_scale, v_ple_norm_g, v_final_g.reshape(1, d),
             _pad_lanes(v_dt_bias), _pad_lanes(v_a_log), _pad_lanes(v_d_skip)]
    rep_g = [d_ng, d_cb, d_gn, d_mixb, d_scale, d_ple_g, dfinal_g, d_dtb, d_alog, d_dsk]
    zero = jnp.zeros((1, LANES), F32)
    small = _all_gather_direct(jnp.concatenate(rep_g + [loss_part], axis=1), "small_all_gather")
    pk = lambda ws: jnp.concatenate(ws + [zero], axis=1)
    rg, rd, rm, rv = _adamw(small, pk(rep_w), pk(rep_m), pk(rep_v), "adamw_replicated")
    loss = rg[0, sum(int(w.size) for w in rep_w)]

    def unpack_rep(buf):
        out, pos_ = [], 0
        for w in rep_w:
            out.append(buf[:, pos_:pos_ + w.size])
            pos_ += w.size
        return out

    def arrange(sh, rep):
        rep = list(rep)
        rep[6] = rep[6].reshape(d)
        for i in (7, 8, 9):
            rep[i] = rep[i][:, :n_heads]
        return [rep[0], sh[0], sh[1], rep[1], rep[7], rep[8], rep[9], rep[2], sh[2], rep[3], rep[4],
                sh[3], sh[4], sh[5], rep[5], sh[6], sh[7], rep[6]]

    outs = [loss, grad_x.reshape(nb, s, d)]
    for sh, rep in ((sh_g, rg), (sh_d, rd), (sh_m, rm), (sh_v, rv)):
        outs += arrange(sh, unpack_rep(rep))
    return tuple(outs)
```

```python
import functools

import jax
import jax.numpy as jnp
from jax import lax
from jax.experimental import pallas as pl
from jax.experimental.pallas import tpu as pltpu

F32 = jnp.float32
BF16 = jnp.bfloat16
I32 = jnp.int32
MESH = pl.DeviceIdType.MESH
ANY = pl.BlockSpec(memory_space=pl.ANY)

EPS = 1e-6
CHUNK = 128
HEAD_DIM = 64
D_STATE = 128
GROUPS = 4
CONV_W = 4
POOL_WINDOWS = (2, 4, 8, 16)
ACT = jnp.bfloat16
LANES = 128
SUBLANES = 8
CONV_ROWS = 64
HEAD_PAD = 128
NEG = -1e30
N_DEV = 8

ADAM_LR = 0.001
ADAM_B1 = 0.9
ADAM_B2 = 0.999
ADAM_EPS = 1e-08
ADAM_WD = 0.01
ADAM_STEP = 10

NN = (((1,), (0,)), ((), ()))
NT = (((1,), (1,)), ((), ()))
TN = (((0,), (0,)), ((), ()))


def _pallas(body, **kw):
    return pl.pallas_call(body, **kw)


def _cparams(dims=None, vmem_mb=None):
    kw = {}
    if dims is not None:
        kw["dimension_semantics"] = dims
    if vmem_mb is not None:
        kw["vmem_limit_bytes"] = vmem_mb << 20
    return pltpu.CompilerParams(**kw)


def _dot(a, b, dims=NN):
    return lax.dot_general(a, b, dims, preferred_element_type=F32)


def _split3(v):
    hi = v.astype(BF16)
    r = v - hi.astype(F32)
    mid = r.astype(BF16)
    lo = (r - mid.astype(F32)).astype(BF16)
    return hi, mid, lo


def _exact_r(v, m, dims=NN):
    hi, mid, lo = _split3(v)
    return _dot(hi, m, dims) + _dot(mid, m, dims) + _dot(lo, m, dims)


def _exact_l(m, v, dims=NN):
    hi, mid, lo = _split3(v)
    return _dot(m, hi, dims) + _dot(m, mid, dims) + _dot(m, lo, dims)


def _row_tile(rows, target):
    best = None
    for cand in range(8, min(rows, target) + 1, 8):
        if rows % cand == 0:
            best = cand
    assert best is not None, rows
    return best


def _sum_all(v):
    return jnp.sum(jnp.sum(v, axis=1, keepdims=True), axis=0, keepdims=True)


def _iota(shape, dim):
    return lax.broadcasted_iota(I32, shape, dim)


def _sigmoid(x):
    return 0.5 * jnp.tanh(0.5 * x) + 0.5


def _softplus(x):
    return jnp.maximum(x, 0.0) + jnp.log(1.0 + jnp.exp(-jnp.abs(x)))


def _shift_down(v, j, row):
    return jnp.where(row >= j, pltpu.roll(v, j, 0), 0.0)


def _shift_up(v, j, row):
    n = v.shape[0]
    return jnp.where(row < n - j, pltpu.roll(v, n - j, 0), 0.0)


def _matmul(a, b, mode, out_dtype, name, tm, tn, tk):
    if mode == "nn":
        (m, k), n = a.shape, b.shape[1]
    elif mode == "nt":
        (m, k), n = a.shape, b.shape[0]
    else:
        (k, m), n = a.shape, b.shape[1]
    tm, tn, tk = min(tm, m), min(tn, n), min(tk, k)
    assert m % tm == 0 and n % tn == 0 and k % tk == 0, (name, m, n, k)
    nk = k // tk
    dims = {"nn": NN, "nt": NT, "tn": TN}[mode]

    def body_acc(a_ref, b_ref, o_ref, acc_ref):
        kk = pl.program_id(2)

        @pl.when(kk == 0)
        def _():
            acc_ref[...] = jnp.zeros_like(acc_ref)

        acc_ref[...] += _dot(a_ref[...].astype(BF16), b_ref[...].astype(BF16), dims)

        @pl.when(kk == nk - 1)
        def _():
            o_ref[...] = acc_ref[...].astype(o_ref.dtype)

    def body_one(a_ref, b_ref, o_ref):
        o_ref[...] = _dot(a_ref[...].astype(BF16), b_ref[...].astype(BF16), dims).astype(o_ref.dtype)

    body = body_one if nk == 1 else body_acc
    if mode == "tn":
        a_spec = pl.BlockSpec((tk, tm), lambda i, j, kk: (kk, i))
    else:
        a_spec = pl.BlockSpec((tm, tk), lambda i, j, kk: (i, kk))
    if mode == "nt":
        b_spec = pl.BlockSpec((tn, tk), lambda i, j, kk: (j, kk))
    else:
        b_spec = pl.BlockSpec((tk, tn), lambda i, j, kk: (kk, j))
    return _pallas(
        body, name=name, grid=(m // tm, n // tn, nk),
        in_specs=[a_spec, b_spec],
        out_specs=pl.BlockSpec((tm, tn), lambda i, j, kk: (i, j)),
        out_shape=jax.ShapeDtypeStruct((m, n), out_dtype),
        scratch_shapes=[] if nk == 1 else [pltpu.VMEM((tm, tn), F32)],
        compiler_params=_cparams(("parallel", "parallel", "arbitrary"), 56),
    )(a, b)


def _in_proj(h, w_main, blocks):
    t, d = h.shape
    n = w_main.shape[1]
    tm, tn = min(1024, t), 1024
    ni, nj = t // tm, n // tn
    gather = _TwoLevelGather(blocks)

    def body(*refs):
        a_ref, b_ref = refs[:2]
        x_refs = refs[2:2 + gather.n]
        o_ref = refs[2 + gather.n]
        got_refs = refs[3 + gather.n:3 + 2 * gather.n]
        sems = refs[3 + 2 * gather.n:]
        i, j = pl.program_id(0), pl.program_id(1)

        @pl.when((i == 0) & (j == 0))
        def _():
            gather.start(x_refs, got_refs, sems)

        o_ref[...] = _dot(a_ref[...], b_ref[...]).astype(o_ref.dtype)

        @pl.when((i == ni - 1) & (j == nj - 1))
        def _():
            gather.finish(x_refs, got_refs, sems)

    return _pallas(
        body, name="in_proj", grid=(ni, nj),
        in_specs=[pl.BlockSpec((tm, d), lambda i, j: (i, 0)), pl.BlockSpec((d, tn), lambda i, j: (0, j))]
        + gather.in_specs,
        out_specs=[pl.BlockSpec((tm, tn), lambda i, j: (i, j))] + gather.out_specs,
        out_shape=[jax.ShapeDtypeStruct((t, n), ACT)] + gather.out_shape,
        scratch_shapes=gather.scratch,
        compiler_params=_cparams(("arbitrary", "arbitrary"), 56),
    )(h, w_main, *blocks)


def _rms_fwd(x, g, name):
    t, d = x.shape
    tt = 512

    def body(x_ref, g_ref, h_ref):
        xv = x_ref[...]
        r = lax.rsqrt(jnp.mean(xv * xv, axis=-1, keepdims=True) + EPS)
        h_ref[...] = (xv * r * g_ref[...]).astype(h_ref.dtype)

    return _pallas(
        body, name=name, grid=(t // tt,),
        in_specs=[pl.BlockSpec((tt, d), lambda i: (i, 0)), pl.BlockSpec((1, d), lambda i: (0, 0))],
        out_specs=pl.BlockSpec((tt, d), lambda i: (i, 0)),
        out_shape=jax.ShapeDtypeStruct((t, d), BF16),
        compiler_params=_cparams(("parallel",)),
    )(x, g)


def _conv_fwd(proj, conv_w, conv_b, nb, s, off_xbc, n_conv):
    cw = 256

    rc = CONV_ROWS

    def body(x_ref, w_ref, b_ref, o_ref, pad_ref):
        pad_ref[0:SUBLANES, :] = jnp.zeros((SUBLANES, cw), F32)
        pad_ref[SUBLANES:, :] = x_ref[...].astype(F32)
        taps = [w_ref[k:k + 1, :] for k in range(CONV_W)]
        bias = b_ref[...]
        for r in range(s // rc):
            acc = bias
            for j in range(CONV_W):
                acc = acc + pad_ref[pl.ds(SUBLANES + r * rc - j, rc), :] * taps[CONV_W - 1 - j]
            o_ref[pl.ds(r * rc, rc), :] = (acc * _sigmoid(acc)).astype(o_ref.dtype)

    return _pallas(
        body, name="conv_fwd", grid=(nb, n_conv // cw),
        in_specs=[pl.BlockSpec((s, cw), lambda b, j: (b, off_xbc // cw + j)),
                  pl.BlockSpec((CONV_W, cw), lambda b, j: (0, j)),
                  pl.BlockSpec((1, cw), lambda b, j: (0, j))],
        out_specs=pl.BlockSpec((s, cw), lambda b, j: (b, j)),
        out_shape=jax.ShapeDtypeStruct((nb * s, n_conv), ACT),
        scratch_shapes=[pltpu.VMEM((s + SUBLANES, cw), F32)],
        compiler_params=_cparams(("parallel", "parallel"), 48),
    )(proj, conv_w, conv_b)


def _ssd_common(dtraw, bias, alog, n_heads):
    l = dtraw.shape[0]
    lane = _iota((l, HEAD_PAD), 1)
    lane1 = _iota((1, HEAD_PAD), 1)
    apad = jnp.where(lane1 < n_heads, -jnp.exp(alog), 0.0)
    dt = jnp.where(lane < n_heads, _softplus(dtraw + bias), 0.0)
    a = dt * apad
    tri = (_iota((l, l), 1) <= _iota((l, l), 0)).astype(BF16)
    acs = _exact_l(tri, a)
    width = n_heads * HEAD_DIM
    ex = (_iota((HEAD_PAD, width), 1) // HEAD_DIM == _iota((HEAD_PAD, width), 0)).astype(BF16)
    dt_full = _exact_r(dt, ex)
    acs_full = _exact_r(acs, ex)
    return dt, apad, acs, dt_full, acs_full, tri, ex


def _decay_mats(acs, acs_t, h, mask):
    col = acs[:, h:h + 1]
    row = acs_t[h:h + 1, :]
    return jnp.exp(jnp.where(mask, col - row, NEG))


def _ssd_fwd(xbc, proj, dtraw, dt_bias, a_log, dskip_full, gnorm, nb, s, d_inner):
    l = CHUNK
    nc = s // l
    n_heads = d_inner // HEAD_DIM
    gw = d_inner // GROUPS
    pairs_per_group = gw // LANES
    gn_cols = GROUPS * D_STATE

    def body(xs_ref, bm_ref, cm_ref, z_ref, dtr_ref, bias_ref, alog_ref, dsk_ref, gn_ref,
             y_ref, ya3_ref, prev_ref, state_ref):
        @pl.when(pl.program_id(1) == 0)
        def _():
            state_ref[...] = jnp.zeros_like(state_ref)

        xs = xs_ref[...].astype(F32)
        dt, apad, acs, dt_full, acs_full, tri, ex = _ssd_common(
            dtr_ref[...], bias_ref[...], alog_ref[...], n_heads)
        acs_t = acs.T
        e_full = jnp.exp(acs_full)
        last = acs_full[l - 1:l, :]
        dte = jnp.exp(last - acs_full)
        cd = jnp.exp(last)
        xdt = xs * dt_full
        xd = xdt * dte
        mask = _iota((l, l), 1) <= _iota((l, l), 0)
        lane = _iota((l, LANES), 1)
        prev_ref[0] = state_ref[...]
        for g in range(GROUPS):
            gsl = slice(g * gw, (g + 1) * gw)
            bg = bm_ref[:, g * D_STATE:(g + 1) * D_STATE].astype(BF16)
            cg = cm_ref[:, g * D_STATE:(g + 1) * D_STATE].astype(BF16)
            cb = _dot(cg, bg, NT)
            outs = []
            for q in range(pairs_per_group):
                pair = g * pairs_per_group + q
                h0, h1 = 2 * pair, 2 * pair + 1
                xpb = xdt[:, pair * LANES:(pair + 1) * LANES].astype(BF16)
                m0 = (cb * _decay_mats(acs, acs_t, h0, mask)).astype(BF16)
                m1 = (cb * _decay_mats(acs, acs_t, h1, mask)).astype(BF16)
                zero = jnp.zeros_like(xpb)
                rhs = jnp.concatenate([jnp.where(lane < HEAD_DIM, xpb, zero),
                                       jnp.where(lane >= HEAD_DIM, xpb, zero)], axis=0)
                outs.append(_dot(jnp.concatenate([m0, m1], axis=1), rhs))
            prev_g = state_ref[:, gsl]
            y_off = _dot(cg, prev_g.astype(BF16)) * e_full[:, gsl]
            y_g = jnp.concatenate(outs, axis=1) + y_off
            y_ref[:, gsl] = y_g.astype(y_ref.dtype)
            st = _dot(bg, xd[:, gsl].astype(BF16), TN)
            state_ref[:, gsl] = prev_g * cd[:, gsl] + st
            zv = z_ref[:, gsl].astype(F32)
            blk = (y_g + xs[:, gsl] * dsk_ref[:, gsl]) * (zv * _sigmoid(zv))
            rg = lax.rsqrt(jnp.mean(blk * blk, axis=-1, keepdims=True) + EPS)
            ya3_ref[:, gsl] = (blk * rg * gn_ref[:, gsl]).astype(ya3_ref.dtype)

    row = lambda b, c: b * nc + c
    small = lambda n: pl.BlockSpec((1, n), lambda b, c: (0, 0))
    return _pallas(
        body, name="ssd_fwd", grid=(nb, nc),
        in_specs=[pl.BlockSpec((l, d_inner), lambda b, c: (row(b, c), 0)),
                  pl.BlockSpec((l, gn_cols), lambda b, c: (row(b, c), d_inner // gn_cols)),
                  pl.BlockSpec((l, gn_cols), lambda b, c: (row(b, c), d_inner // gn_cols + 1)),
                  pl.BlockSpec((l, d_inner), lambda b, c: (row(b, c), 0)),
                  pl.BlockSpec((l, HEAD_PAD), lambda b, c: (row(b, c), 0)),
                  small(HEAD_PAD), small(HEAD_PAD), small(d_inner), small(d_inner)],
        out_specs=[pl.BlockSpec((l, d_inner), lambda b, c: (row(b, c), 0)),
                   pl.BlockSpec((l, d_inner), lambda b, c: (row(b, c), 0)),
                   pl.BlockSpec((1, D_STATE, d_inner), lambda b, c: (row(b, c), 0, 0))],
        out_shape=[jax.ShapeDtypeStruct((nb * s, d_inner), ACT),
                   jax.ShapeDtypeStruct((nb * s, d_inner), BF16),
                   jax.ShapeDtypeStruct((nb * nc, D_STATE, d_inner), F32)],
        scratch_shapes=[pltpu.VMEM((D_STATE, d_inner), F32)],
        compiler_params=_cparams(("parallel", "arbitrary"), 48),
    )(xbc, xbc, xbc, proj, dtraw, dt_bias, a_log, dskip_full, gnorm)


def _pool_fwd(proj, nb, s, off_u, d_pool):
    pg = d_pool // len(POOL_WINDOWS)

    def body(u_ref, o_ref):
        g = pl.program_id(1)
        uv = u_ref[...].astype(F32)
        row = _iota(uv.shape, 0)
        pos = (_iota((uv.shape[0], 1), 0) + 1).astype(F32)
        for gi, w in enumerate(POOL_WINDOWS):
            @pl.when(g == gi)
            def _():
                acc, span = uv, 1
                while span < w:
                    acc = acc + _shift_down(acc, span, row)
                    span *= 2
                o_ref[...] = (acc / jnp.minimum(pos, float(w)) - uv).astype(o_ref.dtype)

    return _pallas(
        body, name="pool_fwd", grid=(nb, len(POOL_WINDOWS)),
        in_specs=[pl.BlockSpec((s, pg), lambda b, g: (b, off_u // pg + g))],
        out_specs=pl.BlockSpec((s, pg), lambda b, g: (b, g)),
        out_shape=jax.ShapeDtypeStruct((nb * s, d_pool), BF16),
        compiler_params=_cparams(("parallel", "parallel"), 48),
    )(proj)


def _mix_fwd(pooled, mixw, proj, mix_b, scale, off_zp):
    t, d_pool = pooled.shape
    ng, pg = mixw.shape[0], mixw.shape[1]
    tm = min(1024, t)

    def body(a_ref, w_ref, zp_ref, b_ref, s_ref, mixed_ref, yb_ref):
        mixed = _dot(a_ref[...], w_ref[0])
        mixed_ref[...] = mixed.astype(mixed_ref.dtype)
        zp = zp_ref[...].astype(F32)
        yb_ref[...] = ((mixed + b_ref[...]) * s_ref[...] * (zp * _sigmoid(zp))).astype(yb_ref.dtype)

    blk = lambda off: pl.BlockSpec((tm, pg), lambda i, g: (i, off + g))
    vec = pl.BlockSpec((1, pg), lambda i, g: (0, g))
    return _pallas(
        body, name="mix_fwd", grid=(t // tm, ng),
        in_specs=[blk(0), pl.BlockSpec((1, pg, pg), lambda i, g: (g, 0, 0)), blk(off_zp // pg), vec, vec],
        out_specs=[blk(0), blk(0)],
        out_shape=[jax.ShapeDtypeStruct((t, d_pool), ACT), jax.ShapeDtypeStruct((t, d_pool), BF16)],
        compiler_params=_cparams(("parallel", "parallel")),
    )(pooled, mixw, proj, mix_b, scale)


def _merge_fwd(ya3, wa, yb, wb, proj, off_ga, off_gb):
    t = ya3.shape[0]
    d = wa.shape[1]
    tm, tn = 512, d

    def body(a_ref, wa_ref, b_ref, wb_ref, ga_ref, gb_ref, pa_ref, pb_ref, mg_ref):
        pa = _dot(a_ref[...], wa_ref[...])
        pb = _dot(b_ref[...], wb_ref[...])
        pa_ref[...] = pa.astype(pa_ref.dtype)
        pb_ref[...] = pb.astype(pb_ref.dtype)
        mg_ref[...] = (_sigmoid(ga_ref[...].astype(F32)) * pa
                       + _sigmoid(gb_ref[...].astype(F32)) * pb).astype(mg_ref.dtype)

    o = pl.BlockSpec((tm, tn), lambda j, i: (i, j))
    return _pallas(
        body, name="merge_fwd", grid=(d // tn, t // tm),
        in_specs=[pl.BlockSpec((tm, ya3.shape[1]), lambda j, i: (i, 0)),
                  pl.BlockSpec((wa.shape[0], tn), lambda j, i: (0, j)),
                  pl.BlockSpec((tm, yb.shape[1]), lambda j, i: (i, 0)),
                  pl.BlockSpec((wb.shape[0], tn), lambda j, i: (0, j)),
                  pl.BlockSpec((tm, tn), lambda j, i: (i, off_ga // tn + j)),
                  pl.BlockSpec((tm, tn), lambda j, i: (i, off_gb // tn + j))],
        out_specs=[o, o, o],
        out_shape=[jax.ShapeDtypeStruct((t, d), ACT), jax.ShapeDtypeStruct((t, d), ACT),
                   jax.ShapeDtypeStruct((t, d), BF16)],
        compiler_params=_cparams(("parallel", "parallel"), 56),
    )(ya3, wa, yb, wb, proj, proj)


def _out_fwd(merged, wout, x, ple_g):
    t, d = x.shape
    tm = 512

    def body(m_ref, w_ref, x_ref, g_ref, x1_ref, h2_ref):
        x1 = x_ref[...] + _dot(m_ref[...], w_ref[...])
        x1_ref[...] = x1
        r = lax.rsqrt(jnp.mean(x1 * x1, axis=-1, keepdims=True) + EPS)
        h2_ref[...] = (x1 * r * g_ref[...]).astype(h2_ref.dtype)

    rowblk = pl.BlockSpec((tm, d), lambda i: (i, 0))
    return _pallas(
        body, name="out_fwd", grid=(t // tm,),
        in_specs=[rowblk, pl.BlockSpec((d, d), lambda i: (0, 0)), rowblk, pl.BlockSpec((1, d), lambda i: (0, 0))],
        out_specs=[rowblk, rowblk],
        out_shape=[jax.ShapeDtypeStruct((t, d), F32), jax.ShapeDtypeStruct((t, d), BF16)],
        compiler_params=_cparams(("parallel",), 56),
    )(merged, wout, x, ple_g)


def _final(h2, wg, p, wup, x1, target, final_g):
    t, d = x1.shape
    pd = p.shape[1]
    tm = 256

    def body(h2_ref, wg_ref, p_ref, wup_ref, x1_ref, tg_ref, g_ref,
             dx2_ref, dgl_ref, dpu_ref, loss_ref, dfg_ref):
        @pl.when(pl.program_id(0) == 0)
        def _():
            loss_ref[...] = jnp.zeros_like(loss_ref)
            dfg_ref[...] = jnp.zeros_like(dfg_ref)

        gate = _sigmoid(_dot(h2_ref[...], wg_ref[...]))
        pu = _dot(p_ref[...].astype(BF16), wup_ref[...])
        x2 = x1_ref[...] + gate * pu
        r = lax.rsqrt(jnp.mean(x2 * x2, axis=-1, keepdims=True) + EPS)
        xh = x2 * r
        gfin = g_ref[...]
        err = xh * gfin - tg_ref[...]
        loss_ref[...] += _sum_all(err * err) * (0.5 / d)
        dy = err * (1.0 / d)
        dfg_ref[...] += jnp.sum(dy * xh, axis=0, keepdims=True)
        dg = dy * gfin
        dx2 = r * (dg - xh * jnp.mean(dg * xh, axis=-1, keepdims=True))
        dx2_ref[...] = dx2
        dgl_ref[...] = (dx2 * pu * gate * (1.0 - gate)).astype(dgl_ref.dtype)
        dpu_ref[...] = (dx2 * gate).astype(dpu_ref.dtype)

    rowblk = pl.BlockSpec((tm, d), lambda i: (i, 0))
    full = lambda a, b: pl.BlockSpec((a, b), lambda i: (0, 0))
    return _pallas(
        body, name="final_fwd_bwd", grid=(t // tm,),
        in_specs=[rowblk, full(d, d), pl.BlockSpec((tm, pd), lambda i: (i, 0)), full(pd, d), rowblk, rowblk,
                  full(1, d)],
        out_specs=[rowblk, rowblk, rowblk, full(1, LANES), full(1, d)],
        out_shape=[jax.ShapeDtypeStruct((t, d), F32), jax.ShapeDtypeStruct((t, d), BF16),
                   jax.ShapeDtypeStruct((t, d), BF16), jax.ShapeDtypeStruct((1, LANES), F32),
                   jax.ShapeDtypeStruct((1, d), F32)],
        compiler_params=_cparams(("arbitrary",), 56),
    )(h2, wg, p, wup, x1, target, final_g)


def _ple_bwd(dgl, wg, dx2, x1, ple_g):
    t, d = x1.shape
    tm = 512

    def body(dgl_ref, wg_ref, dx2_ref, x1_ref, g_ref, dx1_ref, dx1b_ref, dg_ref):
        @pl.when(pl.program_id(0) == 0)
        def _():
            dg_ref[...] = jnp.zeros_like(dg_ref)

        dh2 = _dot(dgl_ref[...], wg_ref[...], NT)
        x1 = x1_ref[...]
        r = lax.rsqrt(jnp.mean(x1 * x1, axis=-1, keepdims=True) + EPS)
        xh = x1 * r
        dg_ref[...] += jnp.sum(dh2 * xh, axis=0, keepdims=True)
        dg = dh2 * g_ref[...]
        dx1 = dx2_ref[...] + r * (dg - xh * jnp.mean(dg * xh, axis=-1, keepdims=True))
        dx1_ref[...] = dx1
        dx1b_ref[...] = dx1.astype(dx1b_ref.dtype)

    rowblk = pl.BlockSpec((tm, d), lambda i: (i, 0))
    full = lambda a, b: pl.BlockSpec((a, b), lambda i: (0, 0))
    return _pallas(
        body, name="ple_bwd", grid=(t // tm,),
        in_specs=[rowblk, full(d, d), rowblk, rowblk, full(1, d)],
        out_specs=[rowblk, rowblk, full(1, d)],
        out_shape=[jax.ShapeDtypeStruct((t, d), F32), jax.ShapeDtypeStruct((t, d), BF16),
                   jax.ShapeDtypeStruct((1, d), F32)],
        compiler_params=_cparams(("arbitrary",), 56),
    )(dgl, wg, dx2, x1, ple_g)


def _merge_bwd(dx1b, wout, proj, pa, pb, n_main, off_ga, off_gb):
    t, d = pa.shape
    tm = 512

    def body(dx_ref, w_ref, ga_ref, gb_ref, pa_ref, pb_ref, dpa_ref, dpb_ref, dgate_ref):
        dm = _dot(dx_ref[...], w_ref[...], NT)
        sa = _sigmoid(ga_ref[...].astype(F32))
        sb = _sigmoid(gb_ref[...].astype(F32))
        dpa_ref[...] = (dm * sa).astype(dpa_ref.dtype)
        dpb_ref[...] = (dm * sb).astype(dpb_ref.dtype)
        dgate_ref[:, :d] = (dm * pa_ref[...].astype(F32) * sa * (1.0 - sa)).astype(dgate_ref.dtype)
        dgate_ref[:, d:] = (dm * pb_ref[...].astype(F32) * sb * (1.0 - sb)).astype(dgate_ref.dtype)

    assert off_gb == off_ga + d and off_ga % (2 * d) == 0
    rowblk = pl.BlockSpec((tm, d), lambda i: (i, 0))
    return _pallas(
        body, name="merge_bwd", grid=(t // tm,),
        in_specs=[rowblk, pl.BlockSpec((d, d), lambda i: (0, 0)),
                  pl.BlockSpec((tm, d), lambda i: (i, off_ga // d)),
                  pl.BlockSpec((tm, d), lambda i: (i, off_gb // d)), rowblk, rowblk],
        out_specs=[rowblk, rowblk, pl.BlockSpec((tm, 2 * d), lambda i: (i, off_ga // (2 * d)))],
        out_shape=[jax.ShapeDtypeStruct((t, d), BF16), jax.ShapeDtypeStruct((t, d), BF16),
                   jax.ShapeDtypeStruct((t, n_main), BF16)],
        compiler_params=_cparams(("parallel",), 56),
    )(dx1b, wout, proj, proj, pa, pb)


def _yb_bwd(dproj, dpb, wb, proj, mixed, mix_b, scale, off_zp):
    t, dp = mixed.shape
    tm = 512

    def body(_, dpb_ref, w_ref, zp_ref, mx_ref, b_ref, s_ref, dzp_ref, dmx_ref, dsc_ref, db_ref):
        @pl.when(pl.program_id(0) == 0)
        def _():
            dsc_ref[...] = jnp.zeros_like(dsc_ref)
            db_ref[...] = jnp.zeros_like(db_ref)

        dyb = _dot(dpb_ref[...], w_ref[...], NT)
        zp = zp_ref[...].astype(F32)
        sg = _sigmoid(zp)
        mb = mx_ref[...].astype(F32) + b_ref[...]
        yb0 = mb * s_ref[...]
        dyb0 = dyb * (zp * sg)
        dzp_ref[...] = (dyb * yb0 * (sg * (1.0 + zp * (1.0 - sg)))).astype(dzp_ref.dtype)
        dsc_ref[...] += jnp.sum(dyb0 * mb, axis=0, keepdims=True)
        dmx = dyb0 * s_ref[...]
        db_ref[...] += jnp.sum(dmx, axis=0, keepdims=True)
        dmx_ref[...] = dmx.astype(dmx_ref.dtype)

    rowblk = pl.BlockSpec((tm, dp), lambda i: (i, 0))
    vec = pl.BlockSpec((1, dp), lambda i: (0, 0))
    seg = pl.BlockSpec((tm, dp), lambda i: (i, off_zp // dp))
    return _pallas(
        body, name="yb_bwd", grid=(t // tm,),
        in_specs=[ANY, pl.BlockSpec((tm, wb.shape[1]), lambda i: (i, 0)),
                  pl.BlockSpec(wb.shape, lambda i: (0, 0)), seg, rowblk, vec, vec],
        out_specs=[seg, rowblk, vec, vec],
        out_shape=[jax.ShapeDtypeStruct(dproj.shape, dproj.dtype), jax.ShapeDtypeStruct((t, dp), BF16),
                   jax.ShapeDtypeStruct((1, dp), F32), jax.ShapeDtypeStruct((1, dp), F32)],
        input_output_aliases={0: 0},
        compiler_params=_cparams(("arbitrary",), 56),
    )(dproj, dpb, wb, proj, mixed, mix_b, scale)


def _dpooled(dmixed, mixw):
    t, dp = dmixed.shape
    ng, pg = mixw.shape[0], mixw.shape[1]
    tm = min(1024, t)

    def body(a_ref, w_ref, o_ref):
        o_ref[...] = _dot(a_ref[...], w_ref[0], NT).astype(o_ref.dtype)

    blk = pl.BlockSpec((tm, pg), lambda i, g: (i, g))
    return _pallas(
        body, name="dpooled", grid=(t // tm, ng),
        in_specs=[blk, pl.BlockSpec((1, pg, pg), lambda i, g: (g, 0, 0))],
        out_specs=blk, out_shape=jax.ShapeDtypeStruct((t, dp), ACT),
        compiler_params=_cparams(("parallel", "parallel")),
    )(dmixed, mixw)


def _dmixw(pooled, dmixed, ng):
    t, dp = pooled.shape
    pg = dp // ng
    tk = min(2048, t)
    nk = t // tk

    def body(a_ref, b_ref, o_ref):
        @pl.when(pl.program_id(1) == 0)
        def _():
            o_ref[...] = jnp.zeros_like(o_ref)

        o_ref[0] += _dot(a_ref[...], b_ref[...], TN)

    blk = pl.BlockSpec((tk, pg), lambda g, k: (k, g))
    return _pallas(
        body, name="dmixw", grid=(ng, nk),
        in_specs=[blk, blk],
        out_specs=pl.BlockSpec((1, pg, pg), lambda g, k: (g, 0, 0)),
        out_shape=jax.ShapeDtypeStruct((ng, pg, pg), F32),
        compiler_params=_cparams(("parallel", "arbitrary")),
    )(pooled, dmixed)


def _pool_bwd(dproj, dpooled, nb, s, off_u):
    dp = dpooled.shape[1]
    pg = dp // len(POOL_WINDOWS)

    def body(_, d_ref, o_ref):
        g = pl.program_id(1)
        dv = d_ref[...].astype(F32)
        row = _iota(dv.shape, 0)
        pos = (_iota((dv.shape[0], 1), 0) + 1).astype(F32)
        for gi, w in enumerate(POOL_WINDOWS):
            @pl.when(g == gi)
            def _():
                acc, span = dv / jnp.minimum(pos, float(w)), 1
                while span < w:
                    acc = acc + _shift_up(acc, span, row)
                    span *= 2
                o_ref[...] = (acc - dv).astype(o_ref.dtype)

    return _pallas(
        body, name="pool_bwd", grid=(nb, len(POOL_WINDOWS)),
        in_specs=[ANY, pl.BlockSpec((s, pg), lambda b, g: (b, g))],
        out_specs=pl.BlockSpec((s, pg), lambda b, g: (b, off_u // pg + g)),
        out_shape=jax.ShapeDtypeStruct(dproj.shape, dproj.dtype),
        input_output_aliases={0: 0},
        compiler_params=_cparams(("parallel", "parallel"), 48),
    )(dproj, dpooled)


def _ssd_bwd(dproj, dya3, y, xbc, proj, dtraw, prev, dt_bias, a_log, dskip_full, gnorm, nb, s, d_inner, partials):
    l = CHUNK
    nc = s // l
    n_heads = d_inner // HEAD_DIM
    gw = d_inner // GROUPS
    pairs_per_group = gw // LANES
    gn_cols = GROUPS * D_STATE
    n_conv = d_inner + 2 * gn_cols
    last_step = nb * nc - 1

    xch = _ChipExchange(partials)
    n_in, n_out = 13, 7

    def body(*refs):
        (_, dya3_ref, y_ref, xs_ref, bm_ref, cm_ref, z_ref, dtr_ref, prev_ref, bias_ref, alog_ref,
         dsk_ref, gn_ref) = refs[:n_in]
        p_refs = refs[n_in:n_in + xch.n]
        outs = refs[n_in + xch.n:]
        dz_ref, dxbc_ref, ddt_ref, dgn_ref, dskh_ref, dalog_ref, dbias_ref = outs[:n_out]
        got_refs = outs[n_out:n_out + xch.n]
        g_ref, dskacc_ref = outs[n_out + xch.n:n_out + xch.n + 2]
        sems = outs[n_out + xch.n + 2:]
        step = pl.program_id(0) * nc + pl.program_id(1)

        @pl.when(pl.program_id(1) == 0)
        def _():
            g_ref[...] = jnp.zeros_like(g_ref)

        @pl.when(step == 0)
        def _():
            xch.start(p_refs, got_refs, sems)
            dgn_ref[...] = jnp.zeros_like(dgn_ref)
            dalog_ref[...] = jnp.zeros_like(dalog_ref)
            dbias_ref[...] = jnp.zeros_like(dbias_ref)
            dskacc_ref[...] = jnp.zeros_like(dskacc_ref)

        xs = xs_ref[...].astype(F32)
        zv = z_ref[...].astype(F32)
        dsk = dsk_ref[...]
        sg = _sigmoid(zv)
        sz = zv * sg
        ya1 = y_ref[...].astype(F32) + xs * dsk
        ya2 = ya1 * sz
        parts = []
        for g in range(GROUPS):
            gsl = slice(g * gw, (g + 1) * gw)
            blk = ya2[:, gsl]
            rg = lax.rsqrt(jnp.mean(blk * blk, axis=-1, keepdims=True) + EPS)
            xh = blk * rg
            d3 = dya3_ref[:, gsl].astype(F32)
            dgn_ref[:, gsl] += jnp.sum(d3 * xh, axis=0, keepdims=True)
            dg = d3 * gn_ref[:, gsl]
            parts.append(rg * (dg - xh * jnp.mean(dg * xh, axis=-1, keepdims=True)))
        dya2 = jnp.concatenate(parts, axis=1)
        dy = dya2 * sz
        dz_ref[...] = (dya2 * ya1 * (sg * (1.0 + zv * (1.0 - sg)))).astype(dz_ref.dtype)
        dskacc_ref[...] += jnp.sum(dy * xs, axis=0, keepdims=True)

        dtraw_v = dtr_ref[...]
        bias = bias_ref[...]
        dt, apad, acs, dt_full, acs_full, tri, ex = _ssd_common(dtraw_v, bias, alog_ref[...], n_heads)
        acs_t = acs.T
        e_full = jnp.exp(acs_full)
        last = acs_full[l - 1:l, :]
        dte = jnp.exp(last - acs_full)
        cd = jnp.exp(last)
        xdt = xs * dt_full
        xd = xdt * dte
        dy_e = dy * e_full
        mask = _iota((l, l), 1) <= _iota((l, l), 0)
        lane = _iota((l, LANES), 1)
        hlane = _iota((l, HEAD_PAD), 1)
        hsub = _iota((HEAD_PAD, l), 0)
        dacs = jnp.zeros((l, HEAD_PAD), F32)
        dacs_t = jnp.zeros((HEAD_PAD, l), F32)
        dacs_full, dlast, dxdt_all, db_all, dc_all = [], [], [], [], []
        for g in range(GROUPS):
            gsl = slice(g * gw, (g + 1) * gw)
            bg = bm_ref[:, g * D_STATE:(g + 1) * D_STATE].astype(BF16)
            cg = cm_ref[:, g * D_STATE:(g + 1) * D_STATE].astype(BF16)
            prev_g = prev_ref[0, :, gsl]
            prev_b = prev_g.astype(BF16)
            gst = g_ref[:, gsl]
            gst_b = gst.astype(BF16)
            dye_b = dy_e[:, gsl].astype(BF16)
            dc = _dot(dye_b, prev_b, NT)
            dprev = _dot(cg, dye_b, TN)
            y_off = _dot(cg, prev_b) * e_full[:, gsl]
            xd_g = xd[:, gsl]
            db = _dot(xd_g.astype(BF16), gst_b, NT)
            dxd = _dot(bg, gst_b)
            wgt = dxd * xd_g
            dacs_full.append(dy[:, gsl] * y_off - wgt)
            dlast.append(jnp.sum(wgt, axis=0, keepdims=True)
                         + jnp.sum(prev_g * gst, axis=0, keepdims=True) * cd[:, gsl])
            g_ref[:, gsl] = dprev + cd[:, gsl] * gst
            dxdt_g = dxd * dte[:, gsl]
            cb = _dot(cg, bg, NT)
            dcb = jnp.zeros((l, l), F32)
            for q in range(pairs_per_group):
                pair = g * pairs_per_group + q
                psl = slice(pair * LANES, (pair + 1) * LANES)
                xpb = xdt[:, psl].astype(BF16)
                dyp = dy[:, psl].astype(BF16)
                zero = jnp.zeros_like(dyp)
                halves = (jnp.where(lane < HEAD_DIM, dyp, zero), jnp.where(lane >= HEAD_DIM, dyp, zero))
                acc = dxdt_g[:, q * LANES:(q + 1) * LANES]
                for k, dyh in enumerate(halves):
                    h = 2 * pair + k
                    dk = _decay_mats(acs, acs_t, h, mask)
                    mm = cb * dk
                    dm = _dot(dyh, xpb, NT)
                    dcb = dcb + dm * dk
                    dseg = dm * mm
                    dacs = dacs + jnp.where(hlane == h, jnp.sum(dseg, axis=1, keepdims=True), 0.0)
                    dacs_t = dacs_t + jnp.where(hsub == h, jnp.sum(dseg, axis=0, keepdims=True), 0.0)
                    acc = acc + _dot(mm.astype(BF16), dyh, TN)
                dxdt_all.append(acc)
            dcb_b = dcb.astype(BF16)
            dc_all.append(dc + _dot(dcb_b, bg))
            db_all.append(db + _dot(dcb_b, cg, TN))

        dxdt = jnp.concatenate(dxdt_all, axis=1)
        dxbc_ref[:, :d_inner] = (dy * dsk + dxdt * dt_full).astype(dxbc_ref.dtype)
        dxbc_ref[:, d_inner:d_inner + gn_cols] = jnp.concatenate(db_all, axis=1).astype(dxbc_ref.dtype)
        dxbc_ref[:, d_inner + gn_cols:] = jnp.concatenate(dc_all, axis=1).astype(dxbc_ref.dtype)
        ddt = _exact_r(dxdt * xs, ex, NT)
        rowi = _iota((l, d_inner), 0)
        dacs_f = jnp.concatenate(dacs_full, axis=1) + jnp.where(rowi == l - 1, jnp.concatenate(dlast, axis=1), 0.0)
        dacs = dacs + _exact_r(dacs_f, ex, NT) - dacs_t.T
        da = _exact_l(tri, dacs, TN)
        ddt = ddt + da * apad
        dalog_ref[...] += jnp.sum(da * dt, axis=0, keepdims=True) * apad
        ddt_raw = ddt * _sigmoid(dtraw_v + bias)
        dbias_ref[...] += jnp.sum(ddt_raw, axis=0, keepdims=True)
        ddt_ref[...] = ddt_raw.astype(ddt_ref.dtype)

        @pl.when(step == last_step)
        def _():
            dskh_ref[...] = _exact_r(dskacc_ref[...], ex, NT)[0:1, :]
            xch.wait(p_refs, got_refs, sems)

    row = lambda b, c: b * nc + (nc - 1 - c)
    small = lambda n: pl.BlockSpec((1, n), lambda b, c: (0, 0))
    wide = pl.BlockSpec((l, d_inner), lambda b, c: (row(b, c), 0))
    return _pallas(
        body, name="ssd_bwd", grid=(nb, nc),
        in_specs=[ANY, wide, wide, wide,
                  pl.BlockSpec((l, gn_cols), lambda b, c: (row(b, c), d_inner // gn_cols)),
                  pl.BlockSpec((l, gn_cols), lambda b, c: (row(b, c), d_inner // gn_cols + 1)),
                  wide, pl.BlockSpec((l, HEAD_PAD), lambda b, c: (row(b, c), 0)),
                  pl.BlockSpec((1, D_STATE, d_inner), lambda b, c: (row(b, c), 0, 0)),
                  small(HEAD_PAD), small(HEAD_PAD), small(d_inner), small(d_inner)] + xch.in_specs,
        out_specs=[wide, pl.BlockSpec((l, n_conv), lambda b, c: (row(b, c), 0)),
                   pl.BlockSpec((l, HEAD_PAD), lambda b, c: (row(b, c), 0)),
                   small(d_inner), small(HEAD_PAD), small(HEAD_PAD), small(HEAD_PAD)] + xch.out_specs,
        out_shape=[jax.ShapeDtypeStruct(dproj.shape, dproj.dtype),
                   jax.ShapeDtypeStruct((nb * s, n_conv), ACT),
                   jax.ShapeDtypeStruct((nb * s, HEAD_PAD), BF16),
                   jax.ShapeDtypeStruct((1, d_inner), F32), jax.ShapeDtypeStruct((1, HEAD_PAD), F32),
                   jax.ShapeDtypeStruct((1, HEAD_PAD), F32), jax.ShapeDtypeStruct((1, HEAD_PAD), F32)]
        + xch.out_shape,
        scratch_shapes=[pltpu.VMEM((D_STATE, d_inner), F32), pltpu.VMEM((8, d_inner), F32)] + xch.scratch,
        input_output_aliases={0: 0},
        compiler_params=_cparams(("arbitrary", "arbitrary"), 56),
    )(dproj, dya3, y, xbc, xbc, xbc, proj, dtraw, prev, dt_bias, a_log, dskip_full, gnorm, *partials)


def _conv_bwd(dproj, dxbc, proj, conv_w, conv_b, nb, s, off_xbc):
    n_conv = dxbc.shape[1]
    cw = 256

    rc = CONV_ROWS

    def fold(v):
        return jnp.sum(v.reshape(rc // SUBLANES, SUBLANES, cw), axis=0)

    def body(_, d_ref, x_ref, w_ref, b_ref, dx_ref, dw_ref, db_ref, xpad_ref, dcpad_ref):
        @pl.when(pl.program_id(1) == 0)
        def _():
            dw_ref[...] = jnp.zeros_like(dw_ref)
            db_ref[...] = jnp.zeros_like(db_ref)

        xpad_ref[0:SUBLANES, :] = jnp.zeros((SUBLANES, cw), F32)
        xpad_ref[SUBLANES:, :] = x_ref[...].astype(F32)
        dcpad_ref[pl.ds(s, SUBLANES), :] = jnp.zeros((SUBLANES, cw), F32)
        taps = [w_ref[k:k + 1, :] for k in range(CONV_W)]
        bias = b_ref[...]
        db = jnp.zeros((SUBLANES, cw), F32)
        dw = [jnp.zeros((SUBLANES, cw), F32) for _ in range(CONV_W)]
        for r in range(s // rc):
            shifted = [xpad_ref[pl.ds(SUBLANES + r * rc - j, rc), :] for j in range(CONV_W)]
            acc = bias
            for j in range(CONV_W):
                acc = acc + shifted[j] * taps[CONV_W - 1 - j]
            sg = _sigmoid(acc)
            dc = d_ref[pl.ds(r * rc, rc), :].astype(F32) * (sg * (1.0 + acc * (1.0 - sg)))
            dcpad_ref[pl.ds(r * rc, rc), :] = dc
            db = db + fold(dc)
            for j in range(CONV_W):
                dw[CONV_W - 1 - j] = dw[CONV_W - 1 - j] + fold(dc * shifted[j])
        db_ref[...] += jnp.sum(db, axis=0, keepdims=True)
        for k in range(CONV_W):
            dw_ref[k:k + 1, :] += jnp.sum(dw[k], axis=0, keepdims=True)
        for r in range(s // rc):
            dxv = dcpad_ref[pl.ds(r * rc, rc), :] * taps[CONV_W - 1]
            for j in range(1, CONV_W):
                dxv = dxv + dcpad_ref[pl.ds(r * rc + j, rc), :] * taps[CONV_W - 1 - j]
            dx_ref[pl.ds(r * rc, rc), :] = dxv.astype(dx_ref.dtype)

    return _pallas(
        body, name="conv_bwd", grid=(n_conv // cw, nb),
        in_specs=[ANY, pl.BlockSpec((s, cw), lambda j, b: (b, j)),
                  pl.BlockSpec((s, cw), lambda j, b: (b, off_xbc // cw + j)),
                  pl.BlockSpec((CONV_W, cw), lambda j, b: (0, j)), pl.BlockSpec((1, cw), lambda j, b: (0, j))],
        out_specs=[pl.BlockSpec((s, cw), lambda j, b: (b, off_xbc // cw + j)),
                   pl.BlockSpec((CONV_W, cw), lambda j, b: (0, j)), pl.BlockSpec((1, cw), lambda j, b: (0, j))],
        out_shape=[jax.ShapeDtypeStruct(dproj.shape, dproj.dtype),
                   jax.ShapeDtypeStruct((CONV_W, n_conv), F32), jax.ShapeDtypeStruct((1, n_conv), F32)],
        scratch_shapes=[pltpu.VMEM((s + SUBLANES, cw), F32), pltpu.VMEM((s + SUBLANES, cw), F32)],
        input_output_aliases={0: 0},
        compiler_params=_cparams(("parallel", "arbitrary"), 48),
    )(dproj, dxbc, proj, conv_w, conv_b)


def _in_bwd(dproj, w_main, ddt, w_dt, x, norm_g, dx1, partials):
    t, d = x.shape
    n_main = dproj.shape[1]
    tm, tk = min(1024, t), 1024
    nk = n_main // tk
    ni = t // tm
    xch = _ChipExchange(partials)
    n_in, n_out = 7, 2

    def body(*refs):
        dp_ref, w_ref, ddt_ref, wdt_ref, x_ref, g_ref, dx1_ref = refs[:n_in]
        p_refs = refs[n_in:n_in + xch.n]
        outs = refs[n_in + xch.n:]
        dx_ref, dg_ref = outs[:n_out]
        got_refs = outs[n_out:n_out + xch.n]
        acc_ref = outs[n_out + xch.n]
        sems = outs[n_out + xch.n + 1:]
        i, kk = pl.program_id(0), pl.program_id(1)

        @pl.when(kk == 0)
        def _():
            acc_ref[...] = jnp.zeros_like(acc_ref)

        @pl.when((kk == 0) & (i == 0))
        def _():
            xch.start(p_refs, got_refs, sems)
            dg_ref[...] = jnp.zeros_like(dg_ref)

        acc_ref[...] += _dot(dp_ref[...], w_ref[...], NT)

        @pl.when(kk == nk - 1)
        def _():
            dh = acc_ref[...] + _dot(ddt_ref[...], wdt_ref[...], NT)
            xv = x_ref[...]
            r = lax.rsqrt(jnp.mean(xv * xv, axis=-1, keepdims=True) + EPS)
            xh = xv * r
            dg_ref[...] += jnp.sum(dh * xh, axis=0, keepdims=True)
            dg = dh * g_ref[...]
            dx_ref[...] = dx1_ref[...] + r * (dg - xh * jnp.mean(dg * xh, axis=-1, keepdims=True))

        @pl.when((kk == nk - 1) & (i == ni - 1))
        def _():
            xch.wait(p_refs, got_refs, sems)

    rowblk = pl.BlockSpec((tm, d), lambda i, k: (i, 0))
    vec = pl.BlockSpec((1, d), lambda i, k: (0, 0))
    return _pallas(
        body, name="in_bwd", grid=(ni, nk),
        in_specs=[pl.BlockSpec((tm, tk), lambda i, k: (i, k)), pl.BlockSpec((d, tk), lambda i, k: (0, k)),
                  pl.BlockSpec((tm, HEAD_PAD), lambda i, k: (i, 0)), pl.BlockSpec((d, HEAD_PAD), lambda i, k: (0, 0)),
                  rowblk, vec, rowblk] + xch.in_specs,
        out_specs=[rowblk, vec] + xch.out_specs,
        out_shape=[jax.ShapeDtypeStruct((t, d), F32), jax.ShapeDtypeStruct((1, d), F32)] + xch.out_shape,
        scratch_shapes=[pltpu.VMEM((tm, d), F32)] + xch.scratch,
        compiler_params=_cparams(("arbitrary", "arbitrary"), 56),
    )(dproj, w_main, ddt, w_dt, x, norm_g, dx1, *partials)


def _adamw_math(w, g, m, v):
    m = ADAM_B1 * m + (1.0 - ADAM_B1) * g
    v = ADAM_B2 * v + (1.0 - ADAM_B2) * (g * g)
    m_hat = m / (1.0 - ADAM_B1 ** ADAM_STEP)
    v_hat = v / (1.0 - ADAM_B2 ** ADAM_STEP)
    delta = -ADAM_LR * (m_hat / (jnp.sqrt(v_hat) + ADAM_EPS) + ADAM_WD * w)
    return delta, m, v


def _adamw(parts, w, m, v, name):
    n, rows, cols = parts.shape
    tr = _row_tile(rows, 128) if rows % 8 == 0 else rows

    def body(p_ref, w_ref, m_ref, v_ref, g_ref, d_ref, mo_ref, vo_ref):
        g = p_ref[0].astype(F32)
        for k in range(1, n):
            g = g + p_ref[k].astype(F32)
        d, mn, vn = _adamw_math(w_ref[...], g, m_ref[...], v_ref[...])
        g_ref[...] = g
        d_ref[...] = d
        mo_ref[...] = mn
        vo_ref[...] = vn

    blk = pl.BlockSpec((tr, cols), lambda i: (i, 0))
    shp = jax.ShapeDtypeStruct((rows, cols), F32)
    return _pallas(
        body, name=name, grid=(rows // tr,),
        in_specs=[pl.BlockSpec((n, tr, cols), lambda i: (0, i, 0)), blk, blk, blk],
        out_specs=[blk, blk, blk, blk], out_shape=[shp, shp, shp, shp],
        compiler_params=_cparams(("parallel",), 56),
    )(parts, w, m, v)


def _coords():
    return lax.axis_index("x"), lax.axis_index("y"), lax.axis_index("c")


class _TwoLevelGather:
    def __init__(self, blocks):
        self.n = len(blocks)
        self.in_specs = [ANY] * self.n
        self.out_specs = [ANY] * self.n
        self.out_shape = [jax.ShapeDtypeStruct((N_DEV,) + b.shape, b.dtype) for b in blocks]
        self.scratch = [pltpu.SemaphoreType.DMA((self.n, 7)), pltpu.SemaphoreType.DMA((self.n, 7)),
                        pltpu.SemaphoreType.DMA((self.n,))]

    def _parts(self, x_refs, out_refs, sems):
        send_sems, recv_sems, local_sems = sems
        x, y, c = _coords()
        me, sibling = (x, y, c), (x, y, 1 - c)
        chips = [(1 - x, y), (x, 1 - y), (1 - x, 1 - y)]

        def slot(a, px, py, pc):
            return out_refs[a].at[4 * px + 2 * py + pc]

        def copy(a, k, block, to, src=None):
            return pltpu.make_async_remote_copy(
                src_ref=slot(a, *block) if src is None else src, dst_ref=slot(a, *block),
                send_sem=send_sems.at[a, k], recv_sem=recv_sems.at[a, k], device_id=to, device_id_type=MESH)

        local = [pltpu.make_async_copy(x_refs[a], slot(a, *me), local_sems.at[a]) for a in range(self.n)]
        first = []
        for a in range(self.n):
            first.append(copy(a, 0, me, sibling, src=x_refs[a]))
            first += [copy(a, 1 + j, me, (*chip, c), src=x_refs[a]) for j, chip in enumerate(chips)]
        return copy, local, first, chips, me, sibling, c

    def start(self, x_refs, out_refs, sems):
        _, local, first, *_ = self._parts(x_refs, out_refs, sems)
        for cp in local + first:
            cp.start()

    def finish(self, x_refs, out_refs, sems):
        copy, local, first, chips, me, sibling, c = self._parts(x_refs, out_refs, sems)
        passed = []
        for j, chip in enumerate(chips):
            for a in range(self.n):
                copy(a, 1 + j, (*chip, c), me).wait_recv()
                passed.append(copy(a, 4 + j, (*chip, c), sibling))
                passed[-1].start()
        for a in range(self.n):
            copy(a, 0, sibling, me).wait_recv()
            for j, chip in enumerate(chips):
                copy(a, 4 + j, (*chip, 1 - c), me).wait_recv()
        for cp in first + passed:
            cp.wait_send()
        for cp in local:
            cp.wait()


def _all_gather_routed(blk, name):
    rows, cols = blk.shape
    half = rows // 2
    assert half % 16 == 0

    def body(x_ref, out_ref, send_sems, recv_sems, local_sem):
        x, y, c = _coords()
        me, sib, xn, yn, dg = (x, y, c), (x, y, 1 - c), (1 - x, y, c), (x, 1 - y, c), (1 - x, 1 - y, c)
        other = lambda p: (p[0], p[1], 1 - c)
        top, bot = pl.ds(0, half), pl.ds(half, half)

        def slot(p, part=None):
            ref = out_ref.at[4 * p[0] + 2 * p[1] + p[2]]
            return ref if part is None else ref.at[part]

        def copy(k, block, to, part=None, src=None):
            return pltpu.make_async_remote_copy(
                src_ref=slot(block, part) if src is None else src, dst_ref=slot(block, part),
                send_sem=send_sems.at[k], recv_sem=recv_sems.at[k], device_id=to, device_id_type=MESH)

        mine = pltpu.make_async_copy(x_ref, slot(me), local_sem)
        own = [copy(1, me, xn, src=x_ref), copy(2, me, yn, src=x_ref), copy(0, me, sib, src=x_ref)]
        for cp in [mine] + own:
            cp.start()
        copy(1, xn, me).wait_recv()
        fwd = [copy(3, xn, yn, top), copy(5, xn, sib)]
        for cp in fwd:
            cp.start()
        copy(2, yn, me).wait_recv()
        fwd += [copy(4, yn, xn, bot), copy(6, yn, sib)]
        for cp in fwd[2:]:
            cp.start()
        copy(3, dg, me, top).wait_recv()
        fwd.append(copy(7, dg, sib, top))
        fwd[-1].start()
        copy(4, dg, me, bot).wait_recv()
        fwd.append(copy(8, dg, sib, bot))
        fwd[-1].start()
        copy(0, sib, me).wait_recv()
        copy(5, other(xn), me).wait_recv()
        copy(6, other(yn), me).wait_recv()
        copy(7, other(dg), me, top).wait_recv()
        copy(8, other(dg), me, bot).wait_recv()
        for cp in own + fwd:
            cp.wait_send()
        mine.wait()

    return _pallas(
        body, name=name, in_specs=[ANY], out_specs=ANY,
        out_shape=jax.ShapeDtypeStruct((N_DEV, rows, cols), blk.dtype),
        scratch_shapes=[pltpu.SemaphoreType.DMA((9,)), pltpu.SemaphoreType.DMA((9,)), pltpu.SemaphoreType.DMA],
    )(blk)


def _all_gather_direct(blk, name):
    rows, cols = blk.shape

    def body(x_ref, out_ref, send_sems, recv_sems, local_sem):
        x, y, c = _coords()
        me = 4 * x + 2 * y + c
        mine = pltpu.make_async_copy(x_ref, out_ref.at[me], local_sem)
        mine.start()
        copies = []
        for k in range(1, N_DEV):
            fx, fy, fc = (k >> 2) & 1, (k >> 1) & 1, k & 1
            peer = (x ^ fx, y ^ fy, c ^ fc)
            copies.append(pltpu.make_async_remote_copy(
                src_ref=x_ref, dst_ref=out_ref.at[me], send_sem=send_sems.at[k - 1], recv_sem=recv_sems.at[k - 1],
                device_id=peer, device_id_type=MESH))
        for cp in copies:
            cp.start()
        for k in range(1, N_DEV):
            fx, fy, fc = (k >> 2) & 1, (k >> 1) & 1, k & 1
            src = 4 * (x ^ fx) + 2 * (y ^ fy) + (c ^ fc)
            pltpu.make_async_remote_copy(
                src_ref=x_ref, dst_ref=out_ref.at[src], send_sem=send_sems.at[k - 1], recv_sem=recv_sems.at[k - 1],
                device_id=(x, y, c), device_id_type=MESH).wait_recv()
        for cp in copies:
            cp.wait_send()
        mine.wait()

    return _pallas(
        body, name=name, in_specs=[ANY], out_specs=ANY,
        out_shape=jax.ShapeDtypeStruct((N_DEV, rows, cols), blk.dtype),
        scratch_shapes=[pltpu.SemaphoreType.DMA((7,)), pltpu.SemaphoreType.DMA((7,)), pltpu.SemaphoreType.DMA],
    )(blk)


def _pair_exchange(grads, name):
    n = len(grads)

    def body(*refs):
        g_refs, out_refs = refs[:n], refs[n:2 * n]
        send_sems, recv_sems = refs[2 * n:]
        x, y, c = _coords()
        copies = []
        for a in range(n):
            for k in range(4):
                copies.append(pltpu.make_async_remote_copy(
                    src_ref=g_refs[a].at[2 * k + (1 - c)], dst_ref=out_refs[a].at[k],
                    send_sem=send_sems.at[a, k], recv_sem=recv_sems.at[a, k],
                    device_id=(x, y, 1 - c), device_id_type=MESH))
        for cp in copies:
            cp.start()
        for cp in copies:
            cp.wait()

    return _pallas(
        body, name=name, in_specs=[ANY] * n, out_specs=[ANY] * n,
        out_shape=[jax.ShapeDtypeStruct((4,) + g.shape[1:], g.dtype) for g in grads],
        scratch_shapes=[pltpu.SemaphoreType.DMA((n, 4)), pltpu.SemaphoreType.DMA((n, 4))],
    )(*grads)


def _pair_add(grads, got, core, out_dtype, name):
    _, rows, cols = grads.shape
    tr = _row_tile(rows, 256) if rows % 8 == 0 else rows

    def body(c_ref, a_ref, b_ref, o_ref):
        o_ref[...] = (a_ref[...] + b_ref[...]).astype(o_ref.dtype)

    grid_spec = pltpu.PrefetchScalarGridSpec(
        num_scalar_prefetch=1, grid=(4, rows // tr),
        in_specs=[pl.BlockSpec((1, tr, cols), lambda k, i, c_ref: (2 * k + c_ref[0], i, 0)),
                  pl.BlockSpec((1, tr, cols), lambda k, i, c_ref: (k, i, 0))],
        out_specs=pl.BlockSpec((1, tr, cols), lambda k, i, c_ref: (k, i, 0)))
    return _pallas(
        body, name=name, grid_spec=grid_spec,
        out_shape=jax.ShapeDtypeStruct((4, rows, cols), out_dtype),
        compiler_params=_cparams(("parallel", "parallel"), 48),
    )(core, grads, got)


class _ChipExchange:
    def __init__(self, partials):
        self.n = len(partials)
        self.in_specs = [ANY] * self.n
        self.out_specs = [ANY] * self.n
        self.out_shape = [jax.ShapeDtypeStruct(p.shape, p.dtype) for p in partials]
        self.scratch = [pltpu.SemaphoreType.DMA((self.n, 3)), pltpu.SemaphoreType.DMA((self.n, 3)),
                        pltpu.SemaphoreType.DMA((self.n,))]

    def _copies(self, p_refs, out_refs, sems, receiving):
        send_sems, recv_sems, local_sems = sems
        x, y, c = _coords()
        mychip = 2 * x + y
        local = [pltpu.make_async_copy(p_refs[a].at[mychip], out_refs[a].at[mychip], local_sems.at[a])
                 for a in range(self.n)]
        remote = []
        for a in range(self.n):
            for k in range(1, 4):
                px, py = x ^ ((k >> 1) & 1), y ^ (k & 1)
                peer = 2 * px + py
                remote.append(pltpu.make_async_remote_copy(
                    src_ref=p_refs[a].at[peer], dst_ref=out_refs[a].at[peer if receiving else mychip],
                    send_sem=send_sems.at[a, k - 1], recv_sem=recv_sems.at[a, k - 1],
                    device_id=(x, y, c) if receiving else (px, py, c), device_id_type=MESH))
        return local, remote

    def start(self, p_refs, out_refs, sems):
        local, sends = self._copies(p_refs, out_refs, sems, False)
        for cp in local + sends:
            cp.start()

    def wait(self, p_refs, out_refs, sems):
        local, recvs = self._copies(p_refs, out_refs, sems, True)
        for cp in recvs:
            cp.wait_recv()
        for cp in recvs:
            cp.wait_send()
        for cp in local:
            cp.wait()


def _pad_lanes(v, n=HEAD_PAD):
    return jnp.pad(v, ((0, 0), (0, n - v.shape[1])))


def kernel(x, p, norm_g, w_in, conv_w, conv_b, dt_bias, a_log, d_skip, gnorm_g, pool_mix_w, pool_mix_b, pool_scale, w_branch_a, w_branch_b, w_out, ple_norm_g, w_ple_gate, w_ple_up, final_g, loss_target, m_norm_g, m_w_in, m_conv_w, m_conv_b, m_dt_bias, m_a_log, m_d_skip, m_gnorm_g, m_pool_mix_w, m_pool_mix_b, m_pool_scale, m_w_branch_a, m_w_branch_b, m_w_out, m_ple_norm_g, m_w_ple_gate, m_w_ple_up, m_final_g, v_norm_g, v_w_in, v_conv_w, v_conv_b, v_dt_bias, v_a_log, v_d_skip, v_gnorm_g, v_pool_mix_w, v_pool_mix_b, v_pool_scale, v_w_branch_a, v_w_branch_b, v_w_out, v_ple_norm_g, v_w_ple_gate, v_w_ple_up, v_final_g):
    nb, s, d = x.shape
    t = nb * s
    n_heads = dt_bias.shape[1]
    d_inner = n_heads * HEAD_DIM
    n_conv = conv_b.shape[1]
    d_pool = pool_mix_b.shape[1]
    n_in = w_in.shape[2] * N_DEV
    ple_dim = p.shape[-1]
    o_xbc, o_dt = d_inner, d_inner + n_conv
    o_u = o_dt + n_heads
    o_zp = o_u + d_pool
    o_ga, o_gb = o_zp + d_pool, o_zp + d_pool + d
    n_main = n_in - n_heads
    off_ga, off_gb, off_xbc = d_inner, d_inner + d, d_inner + 2 * d
    off_u = off_xbc + n_conv
    off_zp = off_u + d_pool

    sharded = [w_in[0], conv_w[0], pool_mix_w[0], w_branch_a[0], w_branch_b[0], w_out[0], w_ple_gate[0], w_ple_up[0]]
    sharded_m = [m_w_in[0], m_conv_w[0], m_pool_mix_w[0], m_w_branch_a[0], m_w_branch_b[0], m_w_out[0],
                 m_w_ple_gate[0], m_w_ple_up[0]]
    sharded_v = [v_w_in[0], v_conv_w[0], v_pool_mix_w[0], v_w_branch_a[0], v_w_branch_b[0], v_w_out[0],
                 v_w_ple_gate[0], v_w_ple_up[0]]
    ng = pool_mix_w.shape[1]
    pg = d_pool // ng
    two_d = lambda w: w.reshape(-1, w.shape[-1])
    sharded, sharded_m, sharded_v = ([two_d(w) for w in ws] for ws in (sharded, sharded_m, sharded_v))

    send = [w if i == 1 else w.astype(BF16) for i, w in enumerate(sharded)]
    seg = [_all_gather_routed(send[0], "w_in_all_gather")]
    chunk = n_in // N_DEV
    segments = [((0, o_xbc), 0), ((o_ga, o_gb), off_ga), ((o_gb, n_in), off_gb), ((o_xbc, o_dt), off_xbc),
                ((o_u, o_zp), off_u), ((o_zp, o_ga), off_zp)]

    def gathered_cols(lo, hi):
        out = []
        while lo < hi:
            dev = lo // chunk
            a, b = lo - dev * chunk, min(hi - dev * chunk, chunk)
            out.append(seg[0][dev, :, a:b])
            lo = dev * chunk + b
        return out

    w_main = jnp.concatenate([pc for (lo, hi), _ in segments for pc in gathered_cols(lo, hi)], axis=1)
    w_dt = _pad_lanes(jnp.concatenate(gathered_cols(o_dt, o_u), axis=1))

    x2 = x.reshape(t, d)
    h = _rms_fwd(x2, norm_g, "rms_in")
    proj, *rest = _in_proj(h, w_main, send[1:])
    seg += rest
    cw_full = seg[1].transpose(1, 0, 2).reshape(CONV_W, n_conv)
    mixw_full = seg[2].reshape(N_DEV, ng, pg // N_DEV, pg).transpose(1, 0, 2, 3).reshape(ng, pg, pg)
    wa_full = seg[3].reshape(d_inner, d)
    wb_full = seg[4].reshape(d_pool, d)
    wout_full = seg[5].reshape(d, d)
    wg_full = seg[6].reshape(d, d)
    wup_full = seg[7].transpose(1, 0, 2).reshape(ple_dim, d)

    p2 = p.reshape(t, ple_dim)
    tgt = loss_target.reshape(t, d)
    bias_p, alog_p = _pad_lanes(dt_bias), _pad_lanes(a_log)
    dskip_full = jnp.repeat(d_skip, HEAD_DIM, axis=1)

    dtraw = _matmul(h, w_dt, "nn", F32, "in_proj_dt", 1024, HEAD_PAD, d)
    xbc = _conv_fwd(proj, cw_full, conv_b, nb, s, off_xbc, n_conv)
    y, ya3, prev = _ssd_fwd(xbc, proj, dtraw, bias_p, alog_p, dskip_full, gnorm_g, nb, s, d_inner)
    pooled = _pool_fwd(proj, nb, s, off_u, d_pool)
    mixed, yb = _mix_fwd(pooled, mixw_full, proj, pool_mix_b, pool_scale, off_zp)
    pa, pb, merged = _merge_fwd(ya3, wa_full, yb, wb_full, proj, off_ga, off_gb)
    x1, h2 = _out_fwd(merged, wout_full, x2, ple_norm_g)
    dx2, dgl, dpu, loss_part, dfinal_g = _final(h2, wg_full, p2, wup_full, x1, tgt, final_g.reshape(1, d))

    g_wup = _matmul(p2, dpu, "tn", F32, "dw_ple_up", 256, 512, t)
    g_wg = _matmul(h2, dgl, "tn", F32, "dw_ple_gate", 512, 512, t)
    dx1, dx1b, d_ple_g = _ple_bwd(dgl, wg_full, dx2, x1, ple_norm_g)
    g_wout = _matmul(merged, dx1b, "tn", F32, "dw_out", 512, 512, t)
    dpa, dpb, dproj = _merge_bwd(dx1b, wout_full, proj, pa, pb, n_main, off_ga, off_gb)
    g_wa = _matmul(ya3, dpa, "tn", F32, "dw_branch_a", 512, 512, t)
    g_wb = _matmul(yb, dpb, "tn", F32, "dw_branch_b", 512, 512, t)
    dya3 = _matmul(dpa, wa_full, "nt", ACT, "d_ya3", 1024, 1024, d)
    dproj, dmixed, d_scale, d_mixb = _yb_bwd(dproj, dpb, wb_full, proj, mixed, pool_mix_b, pool_scale, off_zp)
    g_mixw = _dmixw(pooled, dmixed, ng)
    dpool = _dpooled(dmixed, mixw_full)
    dproj = _pool_bwd(dproj, dpool, nb, s, off_u)

    core = lax.axis_index("c").astype(I32).reshape(1)

    def pair_sums(by_dest, names):
        got = _pair_exchange(by_dest, "rs_pair_exchange_" + names[0])
        return [_pair_add(g, r, core, BF16, "rs_pair_add_" + nm) for g, r, nm in zip(by_dest, got, names)]

    names_a = ["pool_mix_w", "w_branch_a", "w_branch_b", "w_out", "w_ple_gate", "w_ple_up"]
    partials_a = pair_sums([
        g_mixw.reshape(ng, N_DEV, pg // N_DEV, pg).transpose(1, 0, 2, 3).reshape(N_DEV, ng * pg // N_DEV, pg),
        g_wa.reshape(N_DEV, d_inner // N_DEV, d), g_wb.reshape(N_DEV, d_pool // N_DEV, d),
        g_wout.reshape(N_DEV, d // N_DEV, d), g_wg.reshape(N_DEV, d // N_DEV, d),
        g_wup.reshape(ple_dim, N_DEV, d // N_DEV).transpose(1, 0, 2)], names_a)
    dproj, dxbc, ddt, d_gn, d_dsk, d_alog, d_dtb, *parts_a = _ssd_bwd(
        dproj, dya3, y, xbc, proj, dtraw, prev, bias_p, alog_p, dskip_full, gnorm_g, nb, s, d_inner, partials_a)
    dproj, g_cw, d_cb = _conv_bwd(dproj, dxbc, proj, cw_full, conv_b, nb, s, off_xbc)
    g_wmain = _matmul(h, dproj, "tn", F32, "dw_in", 512, 1024, t)
    g_wdt = _matmul(h, ddt, "tn", F32, "dw_in_dt", 512, HEAD_PAD, t)

    def dest_cols(dev):
        lo, hi = dev * chunk, (dev + 1) * chunk
        srcs = [((olo, ohi), g_wmain, off) for (olo, ohi), off in segments] + [((o_dt, o_u), g_wdt, 0)]
        out = []
        for (olo, ohi), arr, off in sorted(srcs, key=lambda e: e[0][0]):
            a, b = max(lo, olo), min(hi, ohi)
            if a < b:
                out.append(arr[:, off + a - olo:off + b - olo])
        return jnp.concatenate(out, axis=1)

    names_b = ["w_in", "conv_w"]
    partials_b = pair_sums([jnp.stack([dest_cols(dev) for dev in range(N_DEV)], axis=0),
                            g_cw.reshape(CONV_W, N_DEV, n_conv // N_DEV).transpose(1, 0, 2)], names_b)
    grad_x, d_ng, *parts_b = _in_bwd(dproj, w_main, ddt, w_dt, x2, norm_g, dx1, partials_b)

    names = names_b + names_a
    parts = parts_b + parts_a
    sh_g, sh_d, sh_m, sh_v = [], [], [], []
    shapes = [w_in.shape, conv_w.shape, pool_mix_w.shape, w_branch_a.shape, w_branch_b.shape, w_out.shape,
              w_ple_gate.shape, w_ple_up.shape]
    for pt, w, m, v, nm, shp in zip(parts, sharded, sharded_m, sharded_v, names, shapes):
        res = _adamw(pt, w, m, v, "adamw_" + nm)
        for dst, r in zip((sh_g, sh_d, sh_m, sh_v), res):
            dst.append(r.reshape(shp))

    rep_w = [norm_g, conv_b, gnorm_g, pool_mix_b, pool_scale, ple_norm_g, final_g.reshape(1, d),
             bias_p, alog_p, _pad_lanes(d_skip)]
    rep_m = [m_norm_g, m_conv_b, m_gnorm_g, m_pool_mix_b, m_pool_scale, m_ple_norm_g, m_final_g.reshape(1, d),
             _pad_lanes(m_dt_bias), _pad_lanes(m_a_log), _pad_lanes(m_d_skip)]
    rep_v = [v_norm_g, v_conv_b, v_gnorm_g, v_pool_mix_b, v_pool_scale, v_ple_norm_g, v_final_g.reshape(1, d),
             _pad_lanes(v_dt_bias), _pad_lanes(v_a_log), _pad_lanes(v_d_skip)]
    rep_g = [d_ng, d_cb, d_gn, d_mixb, d_scale, d_ple_g, dfinal_g, d_dtb, d_alog, d_dsk]
    zero = jnp.zeros((1, LANES), F32)
    small = _all_gather_direct(jnp.concatenate(rep_g + [loss_part], axis=1), "small_all_gather")
    pk = lambda ws: jnp.concatenate(ws + [zero], axis=1)
    rg, rd, rm, rv = _adamw(small, pk(rep_w), pk(rep_m), pk(rep_v), "adamw_replicated")
    loss = rg[0, sum(int(w.size) for w in rep_w)]

    def unpack_rep(buf):
        out, pos_ = [], 0
        for w in rep_w:
            out.append(buf[:, pos_:pos_ + w.size])
            pos_ += w.size
        return out

    def arrange(sh, rep):
        rep = list(rep)
        rep[6] = rep[6].reshape(d)
        for i in (7, 8, 9):
            rep[i] = rep[i][:, :n_heads]
        return [rep[0], sh[0], sh[1], rep[1], rep[7], rep[8], rep[9], rep[2], sh[2], rep[3], rep[4],
                sh[3], sh[4], sh[5], rep[5], sh[6], sh[7], rep[6]]

    outs = [loss, grad_x.reshape(nb, s, d)]
    for sh, rep in ((sh_g, rg), (sh_d, rd), (sh_m, rm), (sh_v, rv)):
        outs += arrange(sh, unpack_rep(rep))
    return tuple(outs)
```

```python
import functools

import jax
import jax.numpy as jnp
from jax import lax
from jax.experimental import pallas as pl
from jax.experimental.pallas import tpu as pltpu

F32 = jnp.float32
BF16 = jnp.bfloat16
I32 = jnp.int32
MESH = pl.DeviceIdType.MESH
ANY = pl.BlockSpec(memory_space=pl.ANY)

EPS = 1e-6
CHUNK = 128
HEAD_DIM = 64
D_STATE = 128
GROUPS = 4
CONV_W = 4
POOL_WINDOWS = (2, 4, 8, 16)
ACT = jnp.bfloat16
LANES = 128
SUBLANES = 8
CONV_ROWS = 64
HEAD_PAD = 128
NEG = -1e30
N_DEV = 8

ADAM_LR = 0.001
ADAM_B1 = 0.9
ADAM_B2 = 0.999
ADAM_EPS = 1e-08
ADAM_WD = 0.01
ADAM_STEP = 10

NN = (((1,), (0,)), ((), ()))
NT = (((1,), (1,)), ((), ()))
TN = (((0,), (0,)), ((), ()))


def _pallas(body, **kw):
    return pl.pallas_call(body, **kw)


def _cparams(dims=None, vmem_mb=None):
    kw = {}
    if dims is not None:
        kw["dimension_semantics"] = dims
    if vmem_mb is not None:
        kw["vmem_limit_bytes"] = vmem_mb << 20
    return pltpu.CompilerParams(**kw)


def _dot(a, b, dims=NN):
    return lax.dot_general(a, b, dims, preferred_element_type=F32)


def _split3(v):
    hi = v.astype(BF16)
    r = v - hi.astype(F32)
    mid = r.astype(BF16)
    lo = (r - mid.astype(F32)).astype(BF16)
    return hi, mid, lo


def _exact_r(v, m, dims=NN):
    hi, mid, lo = _split3(v)
    return _dot(hi, m, dims) + _dot(mid, m, dims) + _dot(lo, m, dims)


def _exact_l(m, v, dims=NN):
    hi, mid, lo = _split3(v)
    return _dot(m, hi, dims) + _dot(m, mid, dims) + _dot(m, lo, dims)


def _row_tile(rows, target):
    best = None
    for cand in range(8, min(rows, target) + 1, 8):
        if rows % cand == 0:
            best = cand
    assert best is not None, rows
    return best


def _sum_all(v):
    return jnp.sum(jnp.sum(v, axis=1, keepdims=True), axis=0, keepdims=True)


def _iota(shape, dim):
    return lax.broadcasted_iota(I32, shape, dim)


def _sigmoid(x):
    return 0.5 * jnp.tanh(0.5 * x) + 0.5


def _softplus(x):
    return jnp.maximum(x, 0.0) + jnp.log(1.0 + jnp.exp(-jnp.abs(x)))


def _shift_down(v, j, row):
    return jnp.where(row >= j, pltpu.roll(v, j, 0), 0.0)


def _shift_up(v, j, row):
    n = v.shape[0]
    return jnp.where(row < n - j, pltpu.roll(v, n - j, 0), 0.0)


def _matmul(a, b, mode, out_dtype, name, tm, tn, tk):
    if mode == "nn":
        (m, k), n = a.shape, b.shape[1]
    elif mode == "nt":
        (m, k), n = a.shape, b.shape[0]
    else:
        (k, m), n = a.shape, b.shape[1]
    tm, tn, tk = min(tm, m), min(tn, n), min(tk, k)
    assert m % tm == 0 and n % tn == 0 and k % tk == 0, (name, m, n, k)
    nk = k // tk
    dims = {"nn": NN, "nt": NT, "tn": TN}[mode]

    def body_acc(a_ref, b_ref, o_ref, acc_ref):
        kk = pl.program_id(2)

        @pl.when(kk == 0)
        def _():
            acc_ref[...] = jnp.zeros_like(acc_ref)

        acc_ref[...] += _dot(a_ref[...].astype(BF16), b_ref[...].astype(BF16), dims)

        @pl.when(kk == nk - 1)
        def _():
            o_ref[...] = acc_ref[...].astype(o_ref.dtype)

    def body_one(a_ref, b_ref, o_ref):
        o_ref[...] = _dot(a_ref[...].astype(BF16), b_ref[...].astype(BF16), dims).astype(o_ref.dtype)

    body = body_one if nk == 1 else body_acc
    if mode == "tn":
        a_spec = pl.BlockSpec((tk, tm), lambda i, j, kk: (kk, i))
    else:
        a_spec = pl.BlockSpec((tm, tk), lambda i, j, kk: (i, kk))
    if mode == "nt":
        b_spec = pl.BlockSpec((tn, tk), lambda i, j, kk: (j, kk))
    else:
        b_spec = pl.BlockSpec((tk, tn), lambda i, j, kk: (kk, j))
    return _pallas(
        body, name=name, grid=(m // tm, n // tn, nk),
        in_specs=[a_spec, b_spec],
        out_specs=pl.BlockSpec((tm, tn), lambda i, j, kk: (i, j)),
        out_shape=jax.ShapeDtypeStruct((m, n), out_dtype),
        scratch_shapes=[] if nk == 1 else [pltpu.VMEM((tm, tn), F32)],
        compiler_params=_cparams(("parallel", "parallel", "arbitrary"), 56),
    )(a, b)


def _in_proj(h, w_main, blocks):
    t, d = h.shape
    n = w_main.shape[1]
    tm, tn = min(1024, t), 1024
    ni, nj = t // tm, n // tn
    gather = _TwoLevelGather(blocks)

    def body(*refs):
        a_ref, b_ref = refs[:2]
        x_refs = refs[2:2 + gather.n]
        o_ref = refs[2 + gather.n]
        got_refs = refs[3 + gather.n:3 + 2 * gather.n]
        sems = refs[3 + 2 * gather.n:]
        i, j = pl.program_id(0), pl.program_id(1)

        @pl.when((i == 0) & (j == 0))
        def _():
            gather.start(x_refs, got_refs, sems)

        o_ref[...] = _dot(a_ref[...], b_ref[...]).astype(o_ref.dtype)

        @pl.when((i == ni - 1) & (j == nj - 1))
        def _():
            gather.finish(x_refs, got_refs, sems)

    return _pallas(
        body, name="in_proj", grid=(ni, nj),
        in_specs=[pl.BlockSpec((tm, d), lambda i, j: (i, 0)), pl.BlockSpec((d, tn), lambda i, j: (0, j))]
        + gather.in_specs,
        out_specs=[pl.BlockSpec((tm, tn), lambda i, j: (i, j))] + gather.out_specs,
        out_shape=[jax.ShapeDtypeStruct((t, n), ACT)] + gather.out_shape,
        scratch_shapes=gather.scratch,
        compiler_params=_cparams(("arbitrary", "arbitrary"), 56),
    )(h, w_main, *blocks)


def _rms_fwd(x, g, name):
    t, d = x.shape
    tt = 512

    def body(x_ref, g_ref, h_ref):
        xv = x_ref[...]
        r = lax.rsqrt(jnp.mean(xv * xv, axis=-1, keepdims=True) + EPS)
        h_ref[...] = (xv * r * g_ref[...]).astype(h_ref.dtype)

    return _pallas(
        body, name=name, grid=(t // tt,),
        in_specs=[pl.BlockSpec((tt, d), lambda i: (i, 0)), pl.BlockSpec((1, d), lambda i: (0, 0))],
        out_specs=pl.BlockSpec((tt, d), lambda i: (i, 0)),
        out_shape=jax.ShapeDtypeStruct((t, d), BF16),
        compiler_params=_cparams(("parallel",)),
    )(x, g)


def _conv_fwd(proj, conv_w, conv_b, nb, s, off_xbc, n_conv):
    cw = 256
    rc = CONV_ROWS

    def body(x_ref, w_ref, b_ref, o_ref, pre_ref, pad_ref):
        pad_ref[0:SUBLANES, :] = jnp.zeros((SUBLANES, cw), F32)
        pad_ref[SUBLANES:, :] = x_ref[...].astype(F32)
        taps = [w_ref[k:k + 1, :] for k in range(CONV_W)]
        bias = b_ref[...]
        for r in range(s // rc):
            win = pad_ref[pl.ds(r * rc, rc + SUBLANES), :]
            acc = bias + win[SUBLANES:, :] * taps[CONV_W - 1]
            for j in range(1, CONV_W):
                acc = acc + pltpu.roll(win, j, 0)[SUBLANES:, :] * taps[CONV_W - 1 - j]
            pre_ref[pl.ds(r * rc, rc), :] = acc.astype(pre_ref.dtype)
            o_ref[pl.ds(r * rc, rc), :] = (acc * _sigmoid(acc)).astype(o_ref.dtype)

    out = pl.BlockSpec((s, cw), lambda b, j: (b, j))
    return _pallas(
        body, name="conv_fwd", grid=(nb, n_conv // cw),
        in_specs=[pl.BlockSpec((s, cw), lambda b, j: (b, off_xbc // cw + j)),
                  pl.BlockSpec((CONV_W, cw), lambda b, j: (0, j)),
                  pl.BlockSpec((1, cw), lambda b, j: (0, j))],
        out_specs=[out, out],
        out_shape=[jax.ShapeDtypeStruct((nb * s, n_conv), ACT), jax.ShapeDtypeStruct((nb * s, n_conv), ACT)],
        scratch_shapes=[pltpu.VMEM((s + SUBLANES, cw), F32)],
        compiler_params=_cparams(("parallel", "parallel"), 48),
    )(proj, conv_w, conv_b)


def _ssd_common(dtraw, bias, alog, n_heads):
    l = dtraw.shape[0]
    lane = _iota((l, HEAD_PAD), 1)
    lane1 = _iota((1, HEAD_PAD), 1)
    apad = jnp.where(lane1 < n_heads, -jnp.exp(alog), 0.0)
    dt = jnp.where(lane < n_heads, _softplus(dtraw + bias), 0.0)
    a = dt * apad
    tri = (_iota((l, l), 1) <= _iota((l, l), 0)).astype(BF16)
    acs = _exact_l(tri, a)
    width = n_heads * HEAD_DIM
    ex = (_iota((HEAD_PAD, width), 1) // HEAD_DIM == _iota((HEAD_PAD, width), 0)).astype(BF16)
    dt_full = _exact_r(dt, ex)
    acs_full = _exact_r(acs, ex)
    return dt, apad, acs, dt_full, acs_full, tri, ex


def _decay_mats(acs, acs_t, h, mask):
    col = acs[:, h:h + 1]
    row = acs_t[h:h + 1, :]
    return jnp.exp(jnp.where(mask, col - row, NEG))


def _ssd_fwd(xbc, proj, dtraw, dt_bias, a_log, dskip_full, gnorm, nb, s, d_inner):
    l = CHUNK
    nc = s // l
    n_heads = d_inner // HEAD_DIM
    gw = d_inner // GROUPS
    pairs_per_group = gw // LANES
    gn_cols = GROUPS * D_STATE

    def body(xs_ref, bm_ref, cm_ref, z_ref, dtr_ref, bias_ref, alog_ref, dsk_ref, gn_ref,
             y_ref, ya3_ref, prev_ref, state_ref):
        @pl.when(pl.program_id(1) == 0)
        def _():
            state_ref[...] = jnp.zeros_like(state_ref)

        xs = xs_ref[...].astype(F32)
        dt, apad, acs, dt_full, acs_full, tri, ex = _ssd_common(
            dtr_ref[...], bias_ref[...], alog_ref[...], n_heads)
        acs_t = acs.T
        e_full = jnp.exp(acs_full)
        last = acs_full[l - 1:l, :]
        dte = jnp.exp(last - acs_full)
        cd = jnp.exp(last)
        xdt = xs * dt_full
        xd = xdt * dte
        mask = _iota((l, l), 1) <= _iota((l, l), 0)
        lane = _iota((l, LANES), 1)
        prev_ref[0] = state_ref[...]
        for g in range(GROUPS):
            gsl = slice(g * gw, (g + 1) * gw)
            bg = bm_ref[:, g * D_STATE:(g + 1) * D_STATE].astype(BF16)
            cg = cm_ref[:, g * D_STATE:(g + 1) * D_STATE].astype(BF16)
            cb = _dot(cg, bg, NT)
            outs = []
            for q in range(pairs_per_group):
                pair = g * pairs_per_group + q
                h0, h1 = 2 * pair, 2 * pair + 1
                xpb = xdt[:, pair * LANES:(pair + 1) * LANES].astype(BF16)
                m0 = (cb * _decay_mats(acs, acs_t, h0, mask)).astype(BF16)
                m1 = (cb * _decay_mats(acs, acs_t, h1, mask)).astype(BF16)
                zero = jnp.zeros_like(xpb)
                rhs = jnp.concatenate([jnp.where(lane < HEAD_DIM, xpb, zero),
                                       jnp.where(lane >= HEAD_DIM, xpb, zero)], axis=0)
                outs.append(_dot(jnp.concatenate([m0, m1], axis=1), rhs))
            prev_g = state_ref[:, gsl]
            y_off = _dot(cg, prev_g.astype(BF16)) * e_full[:, gsl]
            y_g = jnp.concatenate(outs, axis=1) + y_off
            y_ref[:, gsl] = y_g.astype(y_ref.dtype)
            st = _dot(bg, xd[:, gsl].astype(BF16), TN)
            state_ref[:, gsl] = prev_g * cd[:, gsl] + st
            zv = z_ref[:, gsl].astype(F32)
            blk = (y_g + xs[:, gsl] * dsk_ref[:, gsl]) * (zv * _sigmoid(zv))
            rg = lax.rsqrt(jnp.mean(blk * blk, axis=-1, keepdims=True) + EPS)
            ya3_ref[:, gsl] = (blk * rg * gn_ref[:, gsl]).astype(ya3_ref.dtype)

    row = lambda b, c: b * nc + c
    small = lambda n: pl.BlockSpec((1, n), lambda b, c: (0, 0))
    return _pallas(
        body, name="ssd_fwd", grid=(nb, nc),
        in_specs=[pl.BlockSpec((l, d_inner), lambda b, c: (row(b, c), 0)),
                  pl.BlockSpec((l, gn_cols), lambda b, c: (row(b, c), d_inner // gn_cols)),
                  pl.BlockSpec((l, gn_cols), lambda b, c: (row(b, c), d_inner // gn_cols + 1)),
                  pl.BlockSpec((l, d_inner), lambda b, c: (row(b, c), 0)),
                  pl.BlockSpec((l, HEAD_PAD), lambda b, c: (row(b, c), 0)),
                  small(HEAD_PAD), small(HEAD_PAD), small(d_inner), small(d_inner)],
        out_specs=[pl.BlockSpec((l, d_inner), lambda b, c: (row(b, c), 0)),
                   pl.BlockSpec((l, d_inner), lambda b, c: (row(b, c), 0)),
                   pl.BlockSpec((1, D_STATE, d_inner), lambda b, c: (row(b, c), 0, 0))],
        out_shape=[jax.ShapeDtypeStruct((nb * s, d_inner), ACT),
                   jax.ShapeDtypeStruct((nb * s, d_inner), BF16),
                   jax.ShapeDtypeStruct((nb * nc, D_STATE, d_inner), F32)],
        scratch_shapes=[pltpu.VMEM((D_STATE, d_inner), F32)],
        compiler_params=_cparams(("parallel", "arbitrary"), 48),
    )(xbc, xbc, xbc, proj, dtraw, dt_bias, a_log, dskip_full, gnorm)


def _pool_fwd(proj, nb, s, off_u, d_pool):
    pg = d_pool // len(POOL_WINDOWS)

    def body(u_ref, o_ref):
        g = pl.program_id(1)
        uv = u_ref[...].astype(F32)
        row = _iota(uv.shape, 0)
        pos = (_iota((uv.shape[0], 1), 0) + 1).astype(F32)
        for gi, w in enumerate(POOL_WINDOWS):
            @pl.when(g == gi)
            def _():
                acc, span = uv, 1
                while span < w:
                    acc = acc + _shift_down(acc, span, row)
                    span *= 2
                o_ref[...] = (acc / jnp.minimum(pos, float(w)) - uv).astype(o_ref.dtype)

    return _pallas(
        body, name="pool_fwd", grid=(nb, len(POOL_WINDOWS)),
        in_specs=[pl.BlockSpec((s, pg), lambda b, g: (b, off_u // pg + g))],
        out_specs=pl.BlockSpec((s, pg), lambda b, g: (b, g)),
        out_shape=jax.ShapeDtypeStruct((nb * s, d_pool), BF16),
        compiler_params=_cparams(("parallel", "parallel"), 48),
    )(proj)


def _mix_fwd(pooled, mixw, proj, mix_b, scale, off_zp):
    t, d_pool = pooled.shape
    ng, pg = mixw.shape[0], mixw.shape[1]
    tm = min(1024, t)

    def body(a_ref, w_ref, zp_ref, b_ref, s_ref, mixed_ref, yb_ref):
        mixed = _dot(a_ref[...], w_ref[0])
        mixed_ref[...] = mixed.astype(mixed_ref.dtype)
        zp = zp_ref[...].astype(F32)
        yb_ref[...] = ((mixed + b_ref[...]) * s_ref[...] * (zp * _sigmoid(zp))).astype(yb_ref.dtype)

    blk = lambda off: pl.BlockSpec((tm, pg), lambda i, g: (i, off + g))
    vec = pl.BlockSpec((1, pg), lambda i, g: (0, g))
    return _pallas(
        body, name="mix_fwd", grid=(t // tm, ng),
        in_specs=[blk(0), pl.BlockSpec((1, pg, pg), lambda i, g: (g, 0, 0)), blk(off_zp // pg), vec, vec],
        out_specs=[blk(0), blk(0)],
        out_shape=[jax.ShapeDtypeStruct((t, d_pool), ACT), jax.ShapeDtypeStruct((t, d_pool), BF16)],
        compiler_params=_cparams(("parallel", "parallel")),
    )(pooled, mixw, proj, mix_b, scale)


def _merge_fwd(ya3, wa, yb, wb, proj, off_ga, off_gb):
    t = ya3.shape[0]
    d = wa.shape[1]
    tm, tn = 512, d

    def body(a_ref, wa_ref, b_ref, wb_ref, ga_ref, gb_ref, pa_ref, pb_ref, mg_ref):
        pa = _dot(a_ref[...], wa_ref[...])
        pb = _dot(b_ref[...], wb_ref[...])
        pa_ref[...] = pa.astype(pa_ref.dtype)
        pb_ref[...] = pb.astype(pb_ref.dtype)
        mg_ref[...] = (_sigmoid(ga_ref[...].astype(F32)) * pa
                       + _sigmoid(gb_ref[...].astype(F32)) * pb).astype(mg_ref.dtype)

    o = pl.BlockSpec((tm, tn), lambda j, i: (i, j))
    return _pallas(
        body, name="merge_fwd", grid=(d // tn, t // tm),
        in_specs=[pl.BlockSpec((tm, ya3.shape[1]), lambda j, i: (i, 0)),
                  pl.BlockSpec((wa.shape[0], tn), lambda j, i: (0, j)),
                  pl.BlockSpec((tm, yb.shape[1]), lambda j, i: (i, 0)),
                  pl.BlockSpec((wb.shape[0], tn), lambda j, i: (0, j)),
                  pl.BlockSpec((tm, tn), lambda j, i: (i, off_ga // tn + j)),
                  pl.BlockSpec((tm, tn), lambda j, i: (i, off_gb // tn + j))],
        out_specs=[o, o, o],
        out_shape=[jax.ShapeDtypeStruct((t, d), ACT), jax.ShapeDtypeStruct((t, d), ACT),
                   jax.ShapeDtypeStruct((t, d), BF16)],
        compiler_params=_cparams(("parallel", "parallel"), 56),
    )(ya3, wa, yb, wb, proj, proj)


def _out_fwd(merged, wout, x, ple_g):
    t, d = x.shape
    tm = 512

    def body(m_ref, w_ref, x_ref, g_ref, x1_ref, h2_ref):
        x1 = x_ref[...] + _dot(m_ref[...], w_ref[...])
        x1_ref[...] = x1
        r = lax.rsqrt(jnp.mean(x1 * x1, axis=-1, keepdims=True) + EPS)
        h2_ref[...] = (x1 * r * g_ref[...]).astype(h2_ref.dtype)

    rowblk = pl.BlockSpec((tm, d), lambda i: (i, 0))
    return _pallas(
        body, name="out_fwd", grid=(t // tm,),
        in_specs=[rowblk, pl.BlockSpec((d, d), lambda i: (0, 0)), rowblk, pl.BlockSpec((1, d), lambda i: (0, 0))],
        out_specs=[rowblk, rowblk],
        out_shape=[jax.ShapeDtypeStruct((t, d), F32), jax.ShapeDtypeStruct((t, d), BF16)],
        compiler_params=_cparams(("parallel",), 56),
    )(merged, wout, x, ple_g)


def _final(h2, wg, p, wup, x1, target, final_g):
    t, d = x1.shape
    pd = p.shape[1]
    tm = 256

    def body(h2_ref, wg_ref, p_ref, wup_ref, x1_ref, tg_ref, g_ref,
             dx2_ref, dgl_ref, dpu_ref, loss_ref, dfg_ref):
        @pl.when(pl.program_id(0) == 0)
        def _():
            loss_ref[...] = jnp.zeros_like(loss_ref)
            dfg_ref[...] = jnp.zeros_like(dfg_ref)

        gate = _sigmoid(_dot(h2_ref[...], wg_ref[...]))
        pu = _dot(p_ref[...].astype(BF16), wup_ref[...])
        x2 = x1_ref[...] + gate * pu
        r = lax.rsqrt(jnp.mean(x2 * x2, axis=-1, keepdims=True) + EPS)
        xh = x2 * r
        gfin = g_ref[...]
        err = xh * gfin - tg_ref[...]
        loss_ref[...] += _sum_all(err * err) * (0.5 / d)
        dy = err * (1.0 / d)
        dfg_ref[...] += jnp.sum(dy * xh, axis=0, keepdims=True)
        dg = dy * gfin
        dx2 = r * (dg - xh * jnp.mean(dg * xh, axis=-1, keepdims=True))
        dx2_ref[...] = dx2
        dgl_ref[...] = (dx2 * pu * gate * (1.0 - gate)).astype(dgl_ref.dtype)
        dpu_ref[...] = (dx2 * gate).astype(dpu_ref.dtype)

    rowblk = pl.BlockSpec((tm, d), lambda i: (i, 0))
    full = lambda a, b: pl.BlockSpec((a, b), lambda i: (0, 0))
    return _pallas(
        body, name="final_fwd_bwd", grid=(t // tm,),
        in_specs=[rowblk, full(d, d), pl.BlockSpec((tm, pd), lambda i: (i, 0)), full(pd, d), rowblk, rowblk,
                  full(1, d)],
        out_specs=[rowblk, rowblk, rowblk, full(1, LANES), full(1, d)],
        out_shape=[jax.ShapeDtypeStruct((t, d), F32), jax.ShapeDtypeStruct((t, d), BF16),
                   jax.ShapeDtypeStruct((t, d), BF16), jax.ShapeDtypeStruct((1, LANES), F32),
                   jax.ShapeDtypeStruct((1, d), F32)],
        compiler_params=_cparams(("arbitrary",), 56),
    )(h2, wg, p, wup, x1, target, final_g)


def _ple_bwd(dgl, wg, dx2, x1, ple_g):
    t, d = x1.shape
    tm = 512

    def body(dgl_ref, wg_ref, dx2_ref, x1_ref, g_ref, dx1_ref, dx1b_ref, dg_ref):
        @pl.when(pl.program_id(0) == 0)
        def _():
            dg_ref[...] = jnp.zeros_like(dg_ref)

        dh2 = _dot(dgl_ref[...], wg_ref[...], NT)
        x1 = x1_ref[...]
        r = lax.rsqrt(jnp.mean(x1 * x1, axis=-1, keepdims=True) + EPS)
        xh = x1 * r
        dg_ref[...] += jnp.sum(dh2 * xh, axis=0, keepdims=True)
        dg = dh2 * g_ref[...]
        dx1 = dx2_ref[...] + r * (dg - xh * jnp.mean(dg * xh, axis=-1, keepdims=True))
        dx1_ref[...] = dx1
        dx1b_ref[...] = dx1.astype(dx1b_ref.dtype)

    rowblk = pl.BlockSpec((tm, d), lambda i: (i, 0))
    full = lambda a, b: pl.BlockSpec((a, b), lambda i: (0, 0))
    return _pallas(
        body, name="ple_bwd", grid=(t // tm,),
        in_specs=[rowblk, full(d, d), rowblk, rowblk, full(1, d)],
        out_specs=[rowblk, rowblk, full(1, d)],
        out_shape=[jax.ShapeDtypeStruct((t, d), F32), jax.ShapeDtypeStruct((t, d), BF16),
                   jax.ShapeDtypeStruct((1, d), F32)],
        compiler_params=_cparams(("arbitrary",), 56),
    )(dgl, wg, dx2, x1, ple_g)


def _merge_bwd(dx1b, wout, proj, pa, pb, n_main, off_ga, off_gb):
    t, d = pa.shape
    tm = 512

    def body(dx_ref, w_ref, ga_ref, gb_ref, pa_ref, pb_ref, dpa_ref, dpb_ref, dgate_ref):
        dm = _dot(dx_ref[...], w_ref[...], NT)
        sa = _sigmoid(ga_ref[...].astype(F32))
        sb = _sigmoid(gb_ref[...].astype(F32))
        dpa_ref[...] = (dm * sa).astype(dpa_ref.dtype)
        dpb_ref[...] = (dm * sb).astype(dpb_ref.dtype)
        dgate_ref[:, :d] = (dm * pa_ref[...].astype(F32) * sa * (1.0 - sa)).astype(dgate_ref.dtype)
        dgate_ref[:, d:] = (dm * pb_ref[...].astype(F32) * sb * (1.0 - sb)).astype(dgate_ref.dtype)

    assert off_gb == off_ga + d and off_ga % (2 * d) == 0
    rowblk = pl.BlockSpec((tm, d), lambda i: (i, 0))
    return _pallas(
        body, name="merge_bwd", grid=(t // tm,),
        in_specs=[rowblk, pl.BlockSpec((d, d), lambda i: (0, 0)),
                  pl.BlockSpec((tm, d), lambda i: (i, off_ga // d)),
                  pl.BlockSpec((tm, d), lambda i: (i, off_gb // d)), rowblk, rowblk],
        out_specs=[rowblk, rowblk, pl.BlockSpec((tm, 2 * d), lambda i: (i, off_ga // (2 * d)))],
        out_shape=[jax.ShapeDtypeStruct((t, d), BF16), jax.ShapeDtypeStruct((t, d), BF16),
                   jax.ShapeDtypeStruct((t, n_main), BF16)],
        compiler_params=_cparams(("parallel",), 56),
    )(dx1b, wout, proj, proj, pa, pb)


def _yb_bwd(dproj, dpb, wb, proj, mixed, mix_b, scale, off_zp):
    t, dp = mixed.shape
    tm = 512

    def body(_, dpb_ref, w_ref, zp_ref, mx_ref, b_ref, s_ref, dzp_ref, dmx_ref, dsc_ref, db_ref):
        @pl.when(pl.program_id(0) == 0)
        def _():
            dsc_ref[...] = jnp.zeros_like(dsc_ref)
            db_ref[...] = jnp.zeros_like(db_ref)

        dyb = _dot(dpb_ref[...], w_ref[...], NT)
        zp = zp_ref[...].astype(F32)
        sg = _sigmoid(zp)
        mb = mx_ref[...].astype(F32) + b_ref[...]
        yb0 = mb * s_ref[...]
        dyb0 = dyb * (zp * sg)
        dzp_ref[...] = (dyb * yb0 * (sg * (1.0 + zp * (1.0 - sg)))).astype(dzp_ref.dtype)
        dsc_ref[...] += jnp.sum(dyb0 * mb, axis=0, keepdims=True)
        dmx = dyb0 * s_ref[...]
        db_ref[...] += jnp.sum(dmx, axis=0, keepdims=True)
        dmx_ref[...] = dmx.astype(dmx_ref.dtype)

    rowblk = pl.BlockSpec((tm, dp), lambda i: (i, 0))
    vec = pl.BlockSpec((1, dp), lambda i: (0, 0))
    seg = pl.BlockSpec((tm, dp), lambda i: (i, off_zp // dp))
    return _pallas(
        body, name="yb_bwd", grid=(t // tm,),
        in_specs=[ANY, pl.BlockSpec((tm, wb.shape[1]), lambda i: (i, 0)),
                  pl.BlockSpec(wb.shape, lambda i: (0, 0)), seg, rowblk, vec, vec],
        out_specs=[seg, rowblk, vec, vec],
        out_shape=[jax.ShapeDtypeStruct(dproj.shape, dproj.dtype), jax.ShapeDtypeStruct((t, dp), BF16),
                   jax.ShapeDtypeStruct((1, dp), F32), jax.ShapeDtypeStruct((1, dp), F32)],
        input_output_aliases={0: 0},
        compiler_params=_cparams(("arbitrary",), 56),
    )(dproj, dpb, wb, proj, mixed, mix_b, scale)


def _dpooled(dmixed, mixw):
    t, dp = dmixed.shape
    ng, pg = mixw.shape[0], mixw.shape[1]
    tm = min(1024, t)

    def body(a_ref, w_ref, o_ref):
        o_ref[...] = _dot(a_ref[...], w_ref[0], NT).astype(o_ref.dtype)

    blk = pl.BlockSpec((tm, pg), lambda i, g: (i, g))
    return _pallas(
        body, name="dpooled", grid=(t // tm, ng),
        in_specs=[blk, pl.BlockSpec((1, pg, pg), lambda i, g: (g, 0, 0))],
        out_specs=blk, out_shape=jax.ShapeDtypeStruct((t, dp), ACT),
        compiler_params=_cparams(("parallel", "parallel")),
    )(dmixed, mixw)


def _dmixw(pooled, dmixed, ng):
    t, dp = pooled.shape
    pg = dp // ng
    tk = min(2048, t)
    nk = t // tk

    def body(a_ref, b_ref, o_ref):
        @pl.when(pl.program_id(1) == 0)
        def _():
            o_ref[...] = jnp.zeros_like(o_ref)

        o_ref[0] += _dot(a_ref[...], b_ref[...], TN)

    blk = pl.BlockSpec((tk, pg), lambda g, k: (k, g))
    return _pallas(
        body, name="dmixw", grid=(ng, nk),
        in_specs=[blk, blk],
        out_specs=pl.BlockSpec((1, pg, pg), lambda g, k: (g, 0, 0)),
        out_shape=jax.ShapeDtypeStruct((ng, pg, pg), F32),
        compiler_params=_cparams(("parallel", "arbitrary")),
    )(pooled, dmixed)


def _pool_bwd(dproj, dpooled, nb, s, off_u):
    dp = dpooled.shape[1]
    pg = dp // len(POOL_WINDOWS)

    def body(_, d_ref, o_ref):
        g = pl.program_id(1)
        dv = d_ref[...].astype(F32)
        row = _iota(dv.shape, 0)
        pos = (_iota((dv.shape[0], 1), 0) + 1).astype(F32)
        for gi, w in enumerate(POOL_WINDOWS):
            @pl.when(g == gi)
            def _():
                acc, span = dv / jnp.minimum(pos, float(w)), 1
                while span < w:
                    acc = acc + _shift_up(acc, span, row)
                    span *= 2
                o_ref[...] = (acc - dv).astype(o_ref.dtype)

    return _pallas(
        body, name="pool_bwd", grid=(nb, len(POOL_WINDOWS)),
        in_specs=[ANY, pl.BlockSpec((s, pg), lambda b, g: (b, g))],
        out_specs=pl.BlockSpec((s, pg), lambda b, g: (b, off_u // pg + g)),
        out_shape=jax.ShapeDtypeStruct(dproj.shape, dproj.dtype),
        input_output_aliases={0: 0},
        compiler_params=_cparams(("parallel", "parallel"), 48),
    )(dproj, dpooled)


def _ssd_bwd(dproj, dya3, y, xbc, proj, dtraw, prev, dt_bias, a_log, dskip_full, gnorm, nb, s, d_inner, partials):
    l = CHUNK
    nc = s // l
    n_heads = d_inner // HEAD_DIM
    gw = d_inner // GROUPS
    pairs_per_group = gw // LANES
    gn_cols = GROUPS * D_STATE
    n_conv = d_inner + 2 * gn_cols
    last_step = nb * nc - 1

    xch = _ChipExchange(partials)
    n_in, n_out = 13, 7

    def body(*refs):
        (_, dya3_ref, y_ref, xs_ref, bm_ref, cm_ref, z_ref, dtr_ref, prev_ref, bias_ref, alog_ref,
         dsk_ref, gn_ref) = refs[:n_in]
        p_refs = refs[n_in:n_in + xch.n]
        outs = refs[n_in + xch.n:]
        dz_ref, dxbc_ref, ddt_ref, dgn_ref, dskh_ref, dalog_ref, dbias_ref = outs[:n_out]
        got_refs = outs[n_out:n_out + xch.n]
        g_ref, dskacc_ref = outs[n_out + xch.n:n_out + xch.n + 2]
        sems = outs[n_out + xch.n + 2:]
        step = pl.program_id(0) * nc + pl.program_id(1)

        @pl.when(pl.program_id(1) == 0)
        def _():
            g_ref[...] = jnp.zeros_like(g_ref)

        @pl.when(step == 0)
        def _():
            xch.start(p_refs, got_refs, sems)
            dgn_ref[...] = jnp.zeros_like(dgn_ref)
            dalog_ref[...] = jnp.zeros_like(dalog_ref)
            dbias_ref[...] = jnp.zeros_like(dbias_ref)
            dskacc_ref[...] = jnp.zeros_like(dskacc_ref)

        xs = xs_ref[...].astype(F32)
        zv = z_ref[...].astype(F32)
        dsk = dsk_ref[...]
        sg = _sigmoid(zv)
        sz = zv * sg
        ya1 = y_ref[...].astype(F32) + xs * dsk
        ya2 = ya1 * sz
        parts = []
        for g in range(GROUPS):
            gsl = slice(g * gw, (g + 1) * gw)
            blk = ya2[:, gsl]
            rg = lax.rsqrt(jnp.mean(blk * blk, axis=-1, keepdims=True) + EPS)
            xh = blk * rg
            d3 = dya3_ref[:, gsl].astype(F32)
            dgn_ref[:, gsl] += jnp.sum(d3 * xh, axis=0, keepdims=True)
            dg = d3 * gn_ref[:, gsl]
            parts.append(rg * (dg - xh * jnp.mean(dg * xh, axis=-1, keepdims=True)))
        dya2 = jnp.concatenate(parts, axis=1)
        dy = dya2 * sz
        dz_ref[...] = (dya2 * ya1 * (sg * (1.0 + zv * (1.0 - sg)))).astype(dz_ref.dtype)
        dskacc_ref[...] += jnp.sum(dy * xs, axis=0, keepdims=True)

        dtraw_v = dtr_ref[...]
        bias = bias_ref[...]
        dt, apad, acs, dt_full, acs_full, tri, ex = _ssd_common(dtraw_v, bias, alog_ref[...], n_heads)
        acs_t = acs.T
        e_full = jnp.exp(acs_full)
        last = acs_full[l - 1:l, :]
        dte = jnp.exp(last - acs_full)
        cd = jnp.exp(last)
        xdt = xs * dt_full
        xd = xdt * dte
        dy_e = dy * e_full
        mask = _iota((l, l), 1) <= _iota((l, l), 0)
        lane = _iota((l, LANES), 1)
        hlane = _iota((l, HEAD_PAD), 1)
        hsub = _iota((HEAD_PAD, l), 0)
        dacs = jnp.zeros((l, HEAD_PAD), F32)
        dacs_t = jnp.zeros((HEAD_PAD, l), F32)
        dacs_full, dlast, dxdt_all, db_all, dc_all = [], [], [], [], []
        for g in range(GROUPS):
            gsl = slice(g * gw, (g + 1) * gw)
            bg = bm_ref[:, g * D_STATE:(g + 1) * D_STATE].astype(BF16)
            cg = cm_ref[:, g * D_STATE:(g + 1) * D_STATE].astype(BF16)
            prev_g = prev_ref[0, :, gsl]
            prev_b = prev_g.astype(BF16)
            gst = g_ref[:, gsl]
            gst_b = gst.astype(BF16)
            dye_b = dy_e[:, gsl].astype(BF16)
            dc = _dot(dye_b, prev_b, NT)
            dprev = _dot(cg, dye_b, TN)
            y_off = _dot(cg, prev_b) * e_full[:, gsl]
            xd_g = xd[:, gsl]
            db = _dot(xd_g.astype(BF16), gst_b, NT)
            dxd = _dot(bg, gst_b)
            wgt = dxd * xd_g
            dacs_full.append(dy[:, gsl] * y_off - wgt)
            dlast.append(jnp.sum(wgt, axis=0, keepdims=True)
                         + jnp.sum(prev_g * gst, axis=0, keepdims=True) * cd[:, gsl])
            g_ref[:, gsl] = dprev + cd[:, gsl] * gst
            dxdt_g = dxd * dte[:, gsl]
            cb = _dot(cg, bg, NT)
            dcb = jnp.zeros((l, l), F32)
            for q in range(pairs_per_group):
                pair = g * pairs_per_group + q
                psl = slice(pair * LANES, (pair + 1) * LANES)
                xpb = xdt[:, psl].astype(BF16)
                dyp = dy[:, psl].astype(BF16)
                zero = jnp.zeros_like(dyp)
                halves = (jnp.where(lane < HEAD_DIM, dyp, zero), jnp.where(lane >= HEAD_DIM, dyp, zero))
                acc = dxdt_g[:, q * LANES:(q + 1) * LANES]
                for k, dyh in enumerate(halves):
                    h = 2 * pair + k
                    dk = _decay_mats(acs, acs_t, h, mask)
                    mm = cb * dk
                    dm = _dot(dyh, xpb, NT)
                    dcb = dcb + dm * dk
                    dseg = dm * mm
                    dacs = dacs + jnp.where(hlane == h, jnp.sum(dseg, axis=1, keepdims=True), 0.0)
                    dacs_t = dacs_t + jnp.where(hsub == h, jnp.sum(dseg, axis=0, keepdims=True), 0.0)
                    acc = acc + _dot(mm.astype(BF16), dyh, TN)
                dxdt_all.append(acc)
            dcb_b = dcb.astype(BF16)
            dc_all.append(dc + _dot(dcb_b, bg))
            db_all.append(db + _dot(dcb_b, cg, TN))

        dxdt = jnp.concatenate(dxdt_all, axis=1)
        dxbc_ref[:, :d_inner] = (dy * dsk + dxdt * dt_full).astype(dxbc_ref.dtype)
        dxbc_ref[:, d_inner:d_inner + gn_cols] = jnp.concatenate(db_all, axis=1).astype(dxbc_ref.dtype)
        dxbc_ref[:, d_inner + gn_cols:] = jnp.concatenate(dc_all, axis=1).astype(dxbc_ref.dtype)
        ddt = _exact_r(dxdt * xs, ex, NT)
        rowi = _iota((l, d_inner), 0)
        dacs_f = jnp.concatenate(dacs_full, axis=1) + jnp.where(rowi == l - 1, jnp.concatenate(dlast, axis=1), 0.0)
        dacs = dacs + _exact_r(dacs_f, ex, NT) - dacs_t.T
        da = _exact_l(tri, dacs, TN)
        ddt = ddt + da * apad
        dalog_ref[...] += jnp.sum(da * dt, axis=0, keepdims=True) * apad
        ddt_raw = ddt * _sigmoid(dtraw_v + bias)
        dbias_ref[...] += jnp.sum(ddt_raw, axis=0, keepdims=True)
        ddt_ref[...] = ddt_raw.astype(ddt_ref.dtype)

        @pl.when(step == last_step)
        def _():
            dskh_ref[...] = _exact_r(dskacc_ref[...], ex, NT)[0:1, :]
            xch.wait(p_refs, got_refs, sems)

    row = lambda b, c: b * nc + (nc - 1 - c)
    small = lambda n: pl.BlockSpec((1, n), lambda b, c: (0, 0))
    wide = pl.BlockSpec((l, d_inner), lambda b, c: (row(b, c), 0))
    return _pallas(
        body, name="ssd_bwd", grid=(nb, nc),
        in_specs=[ANY, wide, wide, wide,
                  pl.BlockSpec((l, gn_cols), lambda b, c: (row(b, c), d_inner // gn_cols)),
                  pl.BlockSpec((l, gn_cols), lambda b, c: (row(b, c), d_inner // gn_cols + 1)),
                  wide, pl.BlockSpec((l, HEAD_PAD), lambda b, c: (row(b, c), 0)),
                  pl.BlockSpec((1, D_STATE, d_inner), lambda b, c: (row(b, c), 0, 0)),
                  small(HEAD_PAD), small(HEAD_PAD), small(d_inner), small(d_inner)] + xch.in_specs,
        out_specs=[wide, pl.BlockSpec((l, n_conv), lambda b, c: (row(b, c), 0)),
                   pl.BlockSpec((l, HEAD_PAD), lambda b, c: (row(b, c), 0)),
                   small(d_inner), small(HEAD_PAD), small(HEAD_PAD), small(HEAD_PAD)] + xch.out_specs,
        out_shape=[jax.ShapeDtypeStruct(dproj.shape, dproj.dtype),
                   jax.ShapeDtypeStruct((nb * s, n_conv), ACT),
                   jax.ShapeDtypeStruct((nb * s, HEAD_PAD), BF16),
                   jax.ShapeDtypeStruct((1, d_inner), F32), jax.ShapeDtypeStruct((1, HEAD_PAD), F32),
                   jax.ShapeDtypeStruct((1, HEAD_PAD), F32), jax.ShapeDtypeStruct((1, HEAD_PAD), F32)]
        + xch.out_shape,
        scratch_shapes=[pltpu.VMEM((D_STATE, d_inner), F32), pltpu.VMEM((8, d_inner), F32)] + xch.scratch,
        input_output_aliases={0: 0},
        compiler_params=_cparams(("arbitrary", "arbitrary"), 56),
    )(dproj, dya3, y, xbc, xbc, xbc, proj, dtraw, prev, dt_bias, a_log, dskip_full, gnorm, *partials)


def _conv_bwd(dproj, dxbc, conv_pre, proj, conv_w, nb, s, off_xbc):
    n_conv = dxbc.shape[1]
    cw = 256

    rc = CONV_ROWS

    def fold(v):
        return jnp.sum(v.reshape(rc // SUBLANES, SUBLANES, cw), axis=0)

    def body(_, d_ref, pre_ref, x_ref, w_ref, dx_ref, dw_ref, db_ref, dcpad_ref):
        @pl.when(pl.program_id(1) == 0)
        def _():
            dw_ref[...] = jnp.zeros_like(dw_ref)
            db_ref[...] = jnp.zeros_like(db_ref)

        dcpad_ref[pl.ds(s, SUBLANES), :] = jnp.zeros((SUBLANES, cw), F32)
        taps = [w_ref[k:k + 1, :] for k in range(CONV_W)]
        db = jnp.zeros((SUBLANES, cw), F32)
        for r in range(s // rc):
            rows = pl.ds(r * rc, rc)
            acc = pre_ref[rows, :].astype(F32)
            sg = _sigmoid(acc)
            dc = d_ref[rows, :].astype(F32) * (sg * (1.0 + acc * (1.0 - sg)))
            dcpad_ref[rows, :] = dc
            db = db + fold(dc)
        db_ref[...] += jnp.sum(db, axis=0, keepdims=True)
        dw = [jnp.zeros((SUBLANES, cw), F32) for _ in range(CONV_W)]
        for r in range(s // rc):
            xv = x_ref[pl.ds(r * rc, rc), :].astype(F32)
            win = dcpad_ref[pl.ds(r * rc, rc + SUBLANES), :]
            dxv = jnp.zeros((rc, cw), F32)
            for j in range(CONV_W):
                up = win[:rc, :] if j == 0 else pltpu.roll(win, rc + SUBLANES - j, 0)[:rc, :]
                dxv = dxv + up * taps[CONV_W - 1 - j]
                dw[CONV_W - 1 - j] = dw[CONV_W - 1 - j] + fold(up * xv)
            dx_ref[pl.ds(r * rc, rc), :] = dxv.astype(dx_ref.dtype)
        for k in range(CONV_W):
            dw_ref[k:k + 1, :] += jnp.sum(dw[k], axis=0, keepdims=True)

    blk = pl.BlockSpec((s, cw), lambda j, b: (b, j))
    return _pallas(
        body, name="conv_bwd", grid=(n_conv // cw, nb),
        in_specs=[ANY, blk, blk, pl.BlockSpec((s, cw), lambda j, b: (b, off_xbc // cw + j)),
                  pl.BlockSpec((CONV_W, cw), lambda j, b: (0, j))],
        out_specs=[pl.BlockSpec((s, cw), lambda j, b: (b, off_xbc // cw + j)),
                   pl.BlockSpec((CONV_W, cw), lambda j, b: (0, j)), pl.BlockSpec((1, cw), lambda j, b: (0, j))],
        out_shape=[jax.ShapeDtypeStruct(dproj.shape, dproj.dtype),
                   jax.ShapeDtypeStruct((CONV_W, n_conv), F32), jax.ShapeDtypeStruct((1, n_conv), F32)],
        scratch_shapes=[pltpu.VMEM((s + SUBLANES, cw), F32)],
        input_output_aliases={0: 0},
        compiler_params=_cparams(("parallel", "arbitrary"), 48),
    )(dproj, dxbc, conv_pre, proj, conv_w)


def _in_bwd(dproj, w_main, ddt, w_dt, x, norm_g, dx1, partials):
    t, d = x.shape
    n_main = dproj.shape[1]
    tm, tk = min(1024, t), 1024
    nk = n_main // tk
    ni = t // tm
    xch = _ChipExchange(partials)
    n_in, n_out = 7, 2

    def body(*refs):
        dp_ref, w_ref, ddt_ref, wdt_ref, x_ref, g_ref, dx1_ref = refs[:n_in]
        p_refs = refs[n_in:n_in + xch.n]
        outs = refs[n_in + xch.n:]
        dx_ref, dg_ref = outs[:n_out]
        got_refs = outs[n_out:n_out + xch.n]
        acc_ref = outs[n_out + xch.n]
        sems = outs[n_out + xch.n + 1:]
        i, kk = pl.program_id(0), pl.program_id(1)

        @pl.when(kk == 0)
        def _():
            acc_ref[...] = jnp.zeros_like(acc_ref)

        @pl.when((kk == 0) & (i == 0))
        def _():
            xch.start(p_refs, got_refs, sems)
            dg_ref[...] = jnp.zeros_like(dg_ref)

        acc_ref[...] += _dot(dp_ref[...], w_ref[...], NT)

        @pl.when(kk == nk - 1)
        def _():
            dh = acc_ref[...] + _dot(ddt_ref[...], wdt_ref[...], NT)
            xv = x_ref[...]
            r = lax.rsqrt(jnp.mean(xv * xv, axis=-1, keepdims=True) + EPS)
            xh = xv * r
            dg_ref[...] += jnp.sum(dh * xh, axis=0, keepdims=True)
            dg = dh * g_ref[...]
            dx_ref[...] = dx1_ref[...] + r * (dg - xh * jnp.mean(dg * xh, axis=-1, keepdims=True))

        @pl.when((kk == nk - 1) & (i == ni - 1))
        def _():
            xch.wait(p_refs, got_refs, sems)

    rowblk = pl.BlockSpec((tm, d), lambda i, k: (i, 0))
    vec = pl.BlockSpec((1, d), lambda i, k: (0, 0))
    return _pallas(
        body, name="in_bwd", grid=(ni, nk),
        in_specs=[pl.BlockSpec((tm, tk), lambda i, k: (i, k)), pl.BlockSpec((d, tk), lambda i, k: (0, k)),
                  pl.BlockSpec((tm, HEAD_PAD), lambda i, k: (i, 0)), pl.BlockSpec((d, HEAD_PAD), lambda i, k: (0, 0)),
                  rowblk, vec, rowblk] + xch.in_specs,
        out_specs=[rowblk, vec] + xch.out_specs,
        out_shape=[jax.ShapeDtypeStruct((t, d), F32), jax.ShapeDtypeStruct((1, d), F32)] + xch.out_shape,
        scratch_shapes=[pltpu.VMEM((tm, d), F32)] + xch.scratch,
        compiler_params=_cparams(("arbitrary", "arbitrary"), 56),
    )(dproj, w_main, ddt, w_dt, x, norm_g, dx1, *partials)


def _adamw_math(w, g, m, v):
    m = ADAM_B1 * m + (1.0 - ADAM_B1) * g
    v = ADAM_B2 * v + (1.0 - ADAM_B2) * (g * g)
    m_hat = m / (1.0 - ADAM_B1 ** ADAM_STEP)
    v_hat = v / (1.0 - ADAM_B2 ** ADAM_STEP)
    delta = -ADAM_LR * (m_hat / (jnp.sqrt(v_hat) + ADAM_EPS) + ADAM_WD * w)
    return delta, m, v


def _adamw(parts, w, m, v, name):
    n, rows, cols = parts.shape
    tr = _row_tile(rows, 128) if rows % 8 == 0 else rows

    def body(p_ref, w_ref, m_ref, v_ref, g_ref, d_ref, mo_ref, vo_ref):
        g = p_ref[0].astype(F32)
        for k in range(1, n):
            g = g + p_ref[k].astype(F32)
        d, mn, vn = _adamw_math(w_ref[...], g, m_ref[...], v_ref[...])
        g_ref[...] = g
        d_ref[...] = d
        mo_ref[...] = mn
        vo_ref[...] = vn

    blk = pl.BlockSpec((tr, cols), lambda i: (i, 0))
    shp = jax.ShapeDtypeStruct((rows, cols), F32)
    return _pallas(
        body, name=name, grid=(rows // tr,),
        in_specs=[pl.BlockSpec((n, tr, cols), lambda i: (0, i, 0)), blk, blk, blk],
        out_specs=[blk, blk, blk, blk], out_shape=[shp, shp, shp, shp],
        compiler_params=_cparams(("parallel",), 56),
    )(parts, w, m, v)


def _coords():
    return lax.axis_index("x"), lax.axis_index("y"), lax.axis_index("c")


class _TwoLevelGather:
    def __init__(self, blocks):
        self.n = len(blocks)
        self.in_specs = [ANY] * self.n
        self.out_specs = [ANY] * self.n
        self.out_shape = [jax.ShapeDtypeStruct((N_DEV,) + b.shape, b.dtype) for b in blocks]
        self.scratch = [pltpu.SemaphoreType.DMA((self.n, 7)), pltpu.SemaphoreType.DMA((self.n, 7)),
                        pltpu.SemaphoreType.DMA((self.n,))]

    def _parts(self, x_refs, out_refs, sems):
        send_sems, recv_sems, local_sems = sems
        x, y, c = _coords()
        me, sibling = (x, y, c), (x, y, 1 - c)
        chips = [(1 - x, y), (x, 1 - y), (1 - x, 1 - y)]

        def slot(a, px, py, pc):
            return out_refs[a].at[4 * px + 2 * py + pc]

        def copy(a, k, block, to, src=None):
            return pltpu.make_async_remote_copy(
                src_ref=slot(a, *block) if src is None else src, dst_ref=slot(a, *block),
                send_sem=send_sems.at[a, k], recv_sem=recv_sems.at[a, k], device_id=to, device_id_type=MESH)

        local = [pltpu.make_async_copy(x_refs[a], slot(a, *me), local_sems.at[a]) for a in range(self.n)]
        first = []
        for a in range(self.n):
            first.append(copy(a, 0, me, sibling, src=x_refs[a]))
            first += [copy(a, 1 + j, me, (*chip, c), src=x_refs[a]) for j, chip in enumerate(chips)]
        return copy, local, first, chips, me, sibling, c

    def start(self, x_refs, out_refs, sems):
        _, local, first, *_ = self._parts(x_refs, out_refs, sems)
        for cp in local + first:
            cp.start()

    def finish(self, x_refs, out_refs, sems):
        copy, local, first, chips, me, sibling, c = self._parts(x_refs, out_refs, sems)
        passed = []
        for j, chip in enumerate(chips):
            for a in range(self.n):
                copy(a, 1 + j, (*chip, c), me).wait_recv()
                passed.append(copy(a, 4 + j, (*chip, c), sibling))
                passed[-1].start()
        for a in range(self.n):
            copy(a, 0, sibling, me).wait_recv()
            for j, chip in enumerate(chips):
                copy(a, 4 + j, (*chip, 1 - c), me).wait_recv()
        for cp in first + passed:
            cp.wait_send()
        for cp in local:
            cp.wait()


def _all_gather_routed(blk, name):
    rows, cols = blk.shape
    half = rows // 2
    assert half % 16 == 0

    def body(x_ref, out_ref, send_sems, recv_sems, local_sem):
        x, y, c = _coords()
        me, sib, xn, yn, dg = (x, y, c), (x, y, 1 - c), (1 - x, y, c), (x, 1 - y, c), (1 - x, 1 - y, c)
        other = lambda p: (p[0], p[1], 1 - c)
        top, bot = pl.ds(0, half), pl.ds(half, half)

        def slot(p, part=None):
            ref = out_ref.at[4 * p[0] + 2 * p[1] + p[2]]
            return ref if part is None else ref.at[part]

        def copy(k, block, to, part=None, src=None):
            return pltpu.make_async_remote_copy(
                src_ref=slot(block, part) if src is None else src, dst_ref=slot(block, part),
                send_sem=send_sems.at[k], recv_sem=recv_sems.at[k], device_id=to, device_id_type=MESH)

        mine = pltpu.make_async_copy(x_ref, slot(me), local_sem)
        own = [copy(1, me, xn, src=x_ref), copy(2, me, yn, src=x_ref), copy(0, me, sib, src=x_ref)]
        for cp in [mine] + own:
            cp.start()
        copy(1, xn, me).wait_recv()
        fwd = [copy(3, xn, yn, top), copy(5, xn, sib)]
        for cp in fwd:
            cp.start()
        copy(2, yn, me).wait_recv()
        fwd += [copy(4, yn, xn, bot), copy(6, yn, sib)]
        for cp in fwd[2:]:
            cp.start()
        copy(3, dg, me, top).wait_recv()
        fwd.append(copy(7, dg, sib, top))
        fwd[-1].start()
        copy(4, dg, me, bot).wait_recv()
        fwd.append(copy(8, dg, sib, bot))
        fwd[-1].start()
        copy(0, sib, me).wait_recv()
        copy(5, other(xn), me).wait_recv()
        copy(6, other(yn), me).wait_recv()
        copy(7, other(dg), me, top).wait_recv()
        copy(8, other(dg), me, bot).wait_recv()
        for cp in own + fwd:
            cp.wait_send()
        mine.wait()

    return _pallas(
        body, name=name, in_specs=[ANY], out_specs=ANY,
        out_shape=jax.ShapeDtypeStruct((N_DEV, rows, cols), blk.dtype),
        scratch_shapes=[pltpu.SemaphoreType.DMA((9,)), pltpu.SemaphoreType.DMA((9,)), pltpu.SemaphoreType.DMA],
    )(blk)


def _all_gather_direct(blk, name):
    rows, cols = blk.shape

    def body(x_ref, out_ref, send_sems, recv_sems, local_sem):
        x, y, c = _coords()
        me = 4 * x + 2 * y + c
        mine = pltpu.make_async_copy(x_ref, out_ref.at[me], local_sem)
        mine.start()
        copies = []
        for k in range(1, N_DEV):
            fx, fy, fc = (k >> 2) & 1, (k >> 1) & 1, k & 1
            peer = (x ^ fx, y ^ fy, c ^ fc)
            copies.append(pltpu.make_async_remote_copy(
                src_ref=x_ref, dst_ref=out_ref.at[me], send_sem=send_sems.at[k - 1], recv_sem=recv_sems.at[k - 1],
                device_id=peer, device_id_type=MESH))
        for cp in copies:
            cp.start()
        for k in range(1, N_DEV):
            fx, fy, fc = (k >> 2) & 1, (k >> 1) & 1, k & 1
            src = 4 * (x ^ fx) + 2 * (y ^ fy) + (c ^ fc)
            pltpu.make_async_remote_copy(
                src_ref=x_ref, dst_ref=out_ref.at[src], send_sem=send_sems.at[k - 1], recv_sem=recv_sems.at[k - 1],
                device_id=(x, y, c), device_id_type=MESH).wait_recv()
        for cp in copies:
            cp.wait_send()
        mine.wait()

    return _pallas(
        body, name=name, in_specs=[ANY], out_specs=ANY,
        out_shape=jax.ShapeDtypeStruct((N_DEV, rows, cols), blk.dtype),
        scratch_shapes=[pltpu.SemaphoreType.DMA((7,)), pltpu.SemaphoreType.DMA((7,)), pltpu.SemaphoreType.DMA],
    )(blk)


def _pair_exchange(grads, name):
    n = len(grads)

    def body(*refs):
        g_refs, out_refs = refs[:n], refs[n:2 * n]
        send_sems, recv_sems = refs[2 * n:]
        x, y, c = _coords()
        copies = []
        for a in range(n):
            for k in range(4):
                copies.append(pltpu.make_async_remote_copy(
                    src_ref=g_refs[a].at[2 * k + (1 - c)], dst_ref=out_refs[a].at[k],
                    send_sem=send_sems.at[a, k], recv_sem=recv_sems.at[a, k],
                    device_id=(x, y, 1 - c), device_id_type=MESH))
        for cp in copies:
            cp.start()
        for cp in copies:
            cp.wait()

    return _pallas(
        body, name=name, in_specs=[ANY] * n, out_specs=[ANY] * n,
        out_shape=[jax.ShapeDtypeStruct((4,) + g.shape[1:], g.dtype) for g in grads],
        scratch_shapes=[pltpu.SemaphoreType.DMA((n, 4)), pltpu.SemaphoreType.DMA((n, 4))],
    )(*grads)


def _pair_add(grads, got, core, out_dtype, name):
    _, rows, cols = grads.shape
    tr = _row_tile(rows, 256) if rows % 8 == 0 else rows

    def body(c_ref, a_ref, b_ref, o_ref):
        o_ref[...] = (a_ref[...] + b_ref[...]).astype(o_ref.dtype)

    grid_spec = pltpu.PrefetchScalarGridSpec(
        num_scalar_prefetch=1, grid=(4, rows // tr),
        in_specs=[pl.BlockSpec((1, tr, cols), lambda k, i, c_ref: (2 * k + c_ref[0], i, 0)),
                  pl.BlockSpec((1, tr, cols), lambda k, i, c_ref: (k, i, 0))],
        out_specs=pl.BlockSpec((1, tr, cols), lambda k, i, c_ref: (k, i, 0)))
    return _pallas(
        body, name=name, grid_spec=grid_spec,
        out_shape=jax.ShapeDtypeStruct((4, rows, cols), out_dtype),
        compiler_params=_cparams(("parallel", "parallel"), 48),
    )(core, grads, got)


class _ChipExchange:
    def __init__(self, partials):
        self.n = len(partials)
        self.in_specs = [ANY] * self.n
        self.out_specs = [ANY] * self.n
        self.out_shape = [jax.ShapeDtypeStruct(p.shape, p.dtype) for p in partials]
        self.scratch = [pltpu.SemaphoreType.DMA((self.n, 3)), pltpu.SemaphoreType.DMA((self.n, 3)),
                        pltpu.SemaphoreType.DMA((self.n,))]

    def _copies(self, p_refs, out_refs, sems, receiving):
        send_sems, recv_sems, local_sems = sems
        x, y, c = _coords()
        mychip = 2 * x + y
        local = [pltpu.make_async_copy(p_refs[a].at[mychip], out_refs[a].at[mychip], local_sems.at[a])
                 for a in range(self.n)]
        remote = []
        for a in range(self.n):
            for k in range(1, 4):
                px, py = x ^ ((k >> 1) & 1), y ^ (k & 1)
                peer = 2 * px + py
                remote.append(pltpu.make_async_remote_copy(
                    src_ref=p_refs[a].at[peer], dst_ref=out_refs[a].at[peer if receiving else mychip],
                    send_sem=send_sems.at[a, k - 1], recv_sem=recv_sems.at[a, k - 1],
                    device_id=(x, y, c) if receiving else (px, py, c), device_id_type=MESH))
        return local, remote

    def start(self, p_refs, out_refs, sems):
        local, sends = self._copies(p_refs, out_refs, sems, False)
        for cp in local + sends:
            cp.start()

    def wait(self, p_refs, out_refs, sems):
        local, recvs = self._copies(p_refs, out_refs, sems, True)
        for cp in recvs:
            cp.wait_recv()
        for cp in recvs:
            cp.wait_send()
        for cp in local:
            cp.wait()


def _pad_lanes(v, n=HEAD_PAD):
    return jnp.pad(v, ((0, 0), (0, n - v.shape[1])))


def kernel(x, p, norm_g, w_in, conv_w, conv_b, dt_bias, a_log, d_skip, gnorm_g, pool_mix_w, pool_mix_b, pool_scale, w_branch_a, w_branch_b, w_out, ple_norm_g, w_ple_gate, w_ple_up, final_g, loss_target, m_norm_g, m_w_in, m_conv_w, m_conv_b, m_dt_bias, m_a_log, m_d_skip, m_gnorm_g, m_pool_mix_w, m_pool_mix_b, m_pool_scale, m_w_branch_a, m_w_branch_b, m_w_out, m_ple_norm_g, m_w_ple_gate, m_w_ple_up, m_final_g, v_norm_g, v_w_in, v_conv_w, v_conv_b, v_dt_bias, v_a_log, v_d_skip, v_gnorm_g, v_pool_mix_w, v_pool_mix_b, v_pool_scale, v_w_branch_a, v_w_branch_b, v_w_out, v_ple_norm_g, v_w_ple_gate, v_w_ple_up, v_final_g):
    nb, s, d = x.shape
    t = nb * s
    n_heads = dt_bias.shape[1]
    d_inner = n_heads * HEAD_DIM
    n_conv = conv_b.shape[1]
    d_pool = pool_mix_b.shape[1]
    n_in = w_in.shape[2] * N_DEV
    ple_dim = p.shape[-1]
    o_xbc, o_dt = d_inner, d_inner + n_conv
    o_u = o_dt + n_heads
    o_zp = o_u + d_pool
    o_ga, o_gb = o_zp + d_pool, o_zp + d_pool + d
    n_main = n_in - n_heads
    off_ga, off_gb, off_xbc = d_inner, d_inner + d, d_inner + 2 * d
    off_u = off_xbc + n_conv
    off_zp = off_u + d_pool

    sharded = [w_in[0], conv_w[0], pool_mix_w[0], w_branch_a[0], w_branch_b[0], w_out[0], w_ple_gate[0], w_ple_up[0]]
    sharded_m = [m_w_in[0], m_conv_w[0], m_pool_mix_w[0], m_w_branch_a[0], m_w_branch_b[0], m_w_out[0],
                 m_w_ple_gate[0], m_w_ple_up[0]]
    sharded_v = [v_w_in[0], v_conv_w[0], v_pool_mix_w[0], v_w_branch_a[0], v_w_branch_b[0], v_w_out[0],
                 v_w_ple_gate[0], v_w_ple_up[0]]
    ng = pool_mix_w.shape[1]
    pg = d_pool // ng
    two_d = lambda w: w.reshape(-1, w.shape[-1])
    sharded, sharded_m, sharded_v = ([two_d(w) for w in ws] for ws in (sharded, sharded_m, sharded_v))

    send = [w if i == 1 else w.astype(BF16) for i, w in enumerate(sharded)]
    seg = [_all_gather_routed(send[0], "w_in_all_gather")]
    chunk = n_in // N_DEV
    segments = [((0, o_xbc), 0), ((o_ga, o_gb), off_ga), ((o_gb, n_in), off_gb), ((o_xbc, o_dt), off_xbc),
                ((o_u, o_zp), off_u), ((o_zp, o_ga), off_zp)]

    def gathered_cols(lo, hi):
        out = []
        while lo < hi:
            dev = lo // chunk
            a, b = lo - dev * chunk, min(hi - dev * chunk, chunk)
            out.append(seg[0][dev, :, a:b])
            lo = dev * chunk + b
        return out

    w_main = jnp.concatenate([pc for (lo, hi), _ in segments for pc in gathered_cols(lo, hi)], axis=1)
    w_dt = _pad_lanes(jnp.concatenate(gathered_cols(o_dt, o_u), axis=1))

    x2 = x.reshape(t, d)
    h = _rms_fwd(x2, norm_g, "rms_in")
    proj, *rest = _in_proj(h, w_main, send[1:])
    seg += rest
    cw_full = seg[1].transpose(1, 0, 2).reshape(CONV_W, n_conv)
    mixw_full = seg[2].reshape(N_DEV, ng, pg // N_DEV, pg).transpose(1, 0, 2, 3).reshape(ng, pg, pg)
    wa_full = seg[3].reshape(d_inner, d)
    wb_full = seg[4].reshape(d_pool, d)
    wout_full = seg[5].reshape(d, d)
    wg_full = seg[6].reshape(d, d)
    wup_full = seg[7].transpose(1, 0, 2).reshape(ple_dim, d)

    p2 = p.reshape(t, ple_dim)
    tgt = loss_target.reshape(t, d)
    bias_p, alog_p = _pad_lanes(dt_bias), _pad_lanes(a_log)
    dskip_full = jnp.repeat(d_skip, HEAD_DIM, axis=1)

    dtraw = _matmul(h, w_dt, "nn", F32, "in_proj_dt", 1024, HEAD_PAD, d)
    xbc, conv_pre = _conv_fwd(proj, cw_full, conv_b, nb, s, off_xbc, n_conv)
    y, ya3, prev = _ssd_fwd(xbc, proj, dtraw, bias_p, alog_p, dskip_full, gnorm_g, nb, s, d_inner)
    pooled = _pool_fwd(proj, nb, s, off_u, d_pool)
    mixed, yb = _mix_fwd(pooled, mixw_full, proj, pool_mix_b, pool_scale, off_zp)
    pa, pb, merged = _merge_fwd(ya3, wa_full, yb, wb_full, proj, off_ga, off_gb)
    x1, h2 = _out_fwd(merged, wout_full, x2, ple_norm_g)
    dx2, dgl, dpu, loss_part, dfinal_g = _final(h2, wg_full, p2, wup_full, x1, tgt, final_g.reshape(1, d))

    g_wup = _matmul(p2, dpu, "tn", F32, "dw_ple_up", 256, 512, t)
    g_wg = _matmul(h2, dgl, "tn", F32, "dw_ple_gate", 512, 512, t)
    dx1, dx1b, d_ple_g = _ple_bwd(dgl, wg_full, dx2, x1, ple_norm_g)
    g_wout = _matmul(merged, dx1b, "tn", F32, "dw_out", 512, 512, t)
    dpa, dpb, dproj = _merge_bwd(dx1b, wout_full, proj, pa, pb, n_main, off_ga, off_gb)
    g_wa = _matmul(ya3, dpa, "tn", F32, "dw_branch_a", 512, 512, t)
    g_wb = _matmul(yb, dpb, "tn", F32, "dw_branch_b", 512, 512, t)
    dya3 = _matmul(dpa, wa_full, "nt", ACT, "d_ya3", 1024, 1024, d)
    dproj, dmixed, d_scale, d_mixb = _yb_bwd(dproj, dpb, wb_full, proj, mixed, pool_mix_b, pool_scale, off_zp)
    g_mixw = _dmixw(pooled, dmixed, ng)
    dpool = _dpooled(dmixed, mixw_full)
    dproj = _pool_bwd(dproj, dpool, nb, s, off_u)

    core = lax.axis_index("c").astype(I32).reshape(1)

    def pair_sums(by_dest, names):
        got = _pair_exchange(by_dest, "rs_pair_exchange_" + names[0])
        return [_pair_add(g, r, core, BF16, "rs_pair_add_" + nm) for g, r, nm in zip(by_dest, got, names)]

    names_a = ["pool_mix_w", "w_branch_a", "w_branch_b", "w_out", "w_ple_gate", "w_ple_up"]
    partials_a = pair_sums([
        g_mixw.reshape(ng, N_DEV, pg // N_DEV, pg).transpose(1, 0, 2, 3).reshape(N_DEV, ng * pg // N_DEV, pg),
        g_wa.reshape(N_DEV, d_inner // N_DEV, d), g_wb.reshape(N_DEV, d_pool // N_DEV, d),
        g_wout.reshape(N_DEV, d // N_DEV, d), g_wg.reshape(N_DEV, d // N_DEV, d),
        g_wup.reshape(ple_dim, N_DEV, d // N_DEV).transpose(1, 0, 2)], names_a)
    dproj, dxbc, ddt, d_gn, d_dsk, d_alog, d_dtb, *parts_a = _ssd_bwd(
        dproj, dya3, y, xbc, proj, dtraw, prev, bias_p, alog_p, dskip_full, gnorm_g, nb, s, d_inner, partials_a)
    dproj, g_cw, d_cb = _conv_bwd(dproj, dxbc, conv_pre, proj, cw_full, nb, s, off_xbc)
    g_wmain = _matmul(h, dproj, "tn", F32, "dw_in", 512, 1024, t)
    g_wdt = _matmul(h, ddt, "tn", F32, "dw_in_dt", 512, HEAD_PAD, t)

    def dest_cols(dev):
        lo, hi = dev * chunk, (dev + 1) * chunk
        srcs = [((olo, ohi), g_wmain, off) for (olo, ohi), off in segments] + [((o_dt, o_u), g_wdt, 0)]
        out = []
        for (olo, ohi), arr, off in sorted(srcs, key=lambda e: e[0][0]):
            a, b = max(lo, olo), min(hi, ohi)
            if a < b:
                out.append(arr[:, off + a - olo:off + b - olo])
        return jnp.concatenate(out, axis=1)

    names_b = ["w_in", "conv_w"]
    partials_b = pair_sums([jnp.stack([dest_cols(dev) for dev in range(N_DEV)], axis=0),
                            g_cw.reshape(CONV_W, N_DEV, n_conv // N_DEV).transpose(1, 0, 2)], names_b)
    grad_x, d_ng, *parts_b = _in_bwd(dproj, w_main, ddt, w_dt, x2, norm_g, dx1, partials_b)

    names = names_b + names_a
    parts = parts_b + parts_a
    sh_g, sh_d, sh_m, sh_v = [], [], [], []
    shapes = [w_in.shape, conv_w.shape, pool_mix_w.shape, w_branch_a.shape, w_branch_b.shape, w_out.shape,
              w_ple_gate.shape, w_ple_up.shape]
    for pt, w, m, v, nm, shp in zip(parts, sharded, sharded_m, sharded_v, names, shapes):
        res = _adamw(pt, w, m, v, "adamw_" + nm)
        for dst, r in zip((sh_g, sh_d, sh_m, sh_v), res):
            dst.append(r.reshape(shp))

    rep_w = [norm_g, conv_b, gnorm_g, pool_mix_b, pool_scale, ple_norm_g, final_g.reshape(1, d),
             bias_p, alog_p, _pad_lanes(d_skip)]
    rep_m = [m_norm_g, m_conv_b, m_gnorm_g, m_pool_mix_b, m_pool_scale, m_ple_norm_g, m_final_g.reshape(1, d),
             _pad_lanes(m_dt_bias), _pad_lanes(m_a_log), _pad_lanes(m_d_skip)]
    rep_v = [v_norm_g, v_conv_b, v_gnorm_g, v_pool_mix_b, v_pool_scale, v_ple_norm_g, v_final_g.reshape(1, d),
             _pad_lanes(v_dt_bias), _pad_lanes(v_a_log), _pad_lanes(v_d_skip)]
    rep_g = [d_ng, d_cb, d_gn, d_mixb, d_scale, d_ple_g, dfinal_g, d_dtb, d_alog, d_dsk]
    zero = jnp.zeros((1, LANES), F32)
    small = _all_gather_direct(jnp.concatenate(rep_g + [loss_part], axis=1), "small_all_gather")
    pk = lambda ws: jnp.concatenate(ws + [zero], axis=1)
    rg, rd, rm, rv = _adamw(small, pk(rep_w), pk(rep_m), pk(rep_v), "adamw_replicated")
    loss = rg[0, sum(int(w.size) for w in rep_w)]

    def unpack_rep(buf):
        out, pos_ = [], 0
        for w in rep_w:
            out.append(buf[:, pos_:pos_ + w.size])
            pos_ += w.size
        return out

    def arrange(sh, rep):
        rep = list(rep)
        rep[6] = rep[6].reshape(d)
        for i in (7, 8, 9):
            rep[i] = rep[i][:, :n_heads]
        return [rep[0], sh[0], sh[1], rep[1], rep[7], rep[8], rep[9], rep[2], sh[2], rep[3], rep[4],
                sh[3], sh[4], sh[5], rep[5], sh[6], sh[7], rep[6]]

    outs = [loss, grad_x.reshape(nb, s, d)]
    for sh, rep in ((sh_g, rg), (sh_d, rd), (sh_m, rm), (sh_v, rv)):
        outs += arrange(sh, unpack_rep(rep))
    return tuple(outs)
```

```python
import functools

import jax
import jax.numpy as jnp
from jax import lax
from jax.experimental import pallas as pl
from jax.experimental.pallas import tpu as pltpu

F32 = jnp.float32
BF16 = jnp.bfloat16
I32 = jnp.int32
MESH = pl.DeviceIdType.MESH
ANY = pl.BlockSpec(memory_space=pl.ANY)

EPS = 1e-6
CHUNK = 128
HEAD_DIM = 64
D_STATE = 128
GROUPS = 4
CONV_W = 4
POOL_WINDOWS = (2, 4, 8, 16)
ACT = jnp.bfloat16
LANES = 128
SUBLANES = 8
CONV_ROWS = 64
HEAD_PAD = 128
NEG = -1e30
N_DEV = 8

ADAM_LR = 0.001
ADAM_B1 = 0.9
ADAM_B2 = 0.999
ADAM_EPS = 1e-08
ADAM_WD = 0.01
ADAM_STEP = 10

NN = (((1,), (0,)), ((), ()))
NT = (((1,), (1,)), ((), ()))
TN = (((0,), (0,)), ((), ()))


def _pallas(body, **kw):
    return pl.pallas_call(body, **kw)


def _cparams(dims=None, vmem_mb=None):
    kw = {}
    if dims is not None:
        kw["dimension_semantics"] = dims
    if vmem_mb is not None:
        kw["vmem_limit_bytes"] = vmem_mb << 20
    return pltpu.CompilerParams(**kw)


def _dot(a, b, dims=NN):
    return lax.dot_general(a, b, dims, preferred_element_type=F32)


def _split3(v):
    hi = v.astype(BF16)
    r = v - hi.astype(F32)
    mid = r.astype(BF16)
    lo = (r - mid.astype(F32)).astype(BF16)
    return hi, mid, lo


def _exact_r(v, m, dims=NN):
    hi, mid, lo = _split3(v)
    return _dot(hi, m, dims) + _dot(mid, m, dims) + _dot(lo, m, dims)


def _exact_l(m, v, dims=NN):
    hi, mid, lo = _split3(v)
    return _dot(m, hi, dims) + _dot(m, mid, dims) + _dot(m, lo, dims)


def _row_tile(rows, target):
    best = None
    for cand in range(8, min(rows, target) + 1, 8):
        if rows % cand == 0:
            best = cand
    assert best is not None, rows
    return best


def _sum_all(v):
    return jnp.sum(jnp.sum(v, axis=1, keepdims=True), axis=0, keepdims=True)


def _iota(shape, dim):
    return lax.broadcasted_iota(I32, shape, dim)


def _sigmoid(x):
    return 0.5 * jnp.tanh(0.5 * x) + 0.5


def _softplus(x):
    return jnp.maximum(x, 0.0) + jnp.log(1.0 + jnp.exp(-jnp.abs(x)))


def _shift_down(v, j, row):
    return jnp.where(row >= j, pltpu.roll(v, j, 0), 0.0)


def _shift_up(v, j, row):
    n = v.shape[0]
    return jnp.where(row < n - j, pltpu.roll(v, n - j, 0), 0.0)


def _matmul(a, b, mode, out_dtype, name, tm, tn, tk):
    if mode == "nn":
        (m, k), n = a.shape, b.shape[1]
    elif mode == "nt":
        (m, k), n = a.shape, b.shape[0]
    else:
        (k, m), n = a.shape, b.shape[1]
    tm, tn, tk = min(tm, m), min(tn, n), min(tk, k)
    assert m % tm == 0 and n % tn == 0 and k % tk == 0, (name, m, n, k)
    nk = k // tk
    dims = {"nn": NN, "nt": NT, "tn": TN}[mode]

    def body_acc(a_ref, b_ref, o_ref, acc_ref):
        kk = pl.program_id(2)

        @pl.when(kk == 0)
        def _():
            acc_ref[...] = jnp.zeros_like(acc_ref)

        acc_ref[...] += _dot(a_ref[...].astype(BF16), b_ref[...].astype(BF16), dims)

        @pl.when(kk == nk - 1)
        def _():
            o_ref[...] = acc_ref[...].astype(o_ref.dtype)

    def body_one(a_ref, b_ref, o_ref):
        o_ref[...] = _dot(a_ref[...].astype(BF16), b_ref[...].astype(BF16), dims).astype(o_ref.dtype)

    body = body_one if nk == 1 else body_acc
    if mode == "tn":
        a_spec = pl.BlockSpec((tk, tm), lambda i, j, kk: (kk, i))
    else:
        a_spec = pl.BlockSpec((tm, tk), lambda i, j, kk: (i, kk))
    if mode == "nt":
        b_spec = pl.BlockSpec((tn, tk), lambda i, j, kk: (j, kk))
    else:
        b_spec = pl.BlockSpec((tk, tn), lambda i, j, kk: (kk, j))
    return _pallas(
        body, name=name, grid=(m // tm, n // tn, nk),
        in_specs=[a_spec, b_spec],
        out_specs=pl.BlockSpec((tm, tn), lambda i, j, kk: (i, j)),
        out_shape=jax.ShapeDtypeStruct((m, n), out_dtype),
        scratch_shapes=[] if nk == 1 else [pltpu.VMEM((tm, tn), F32)],
        compiler_params=_cparams(("parallel", "parallel", "arbitrary"), 56),
    )(a, b)


def _in_proj(h, w_main, blocks):
    t, d = h.shape
    n = w_main.shape[1]
    tm, tn = min(1024, t), 1024
    ni, nj = t // tm, n // tn
    gather = _TwoLevelGather(blocks)

    def body(*refs):
        a_ref, b_ref = refs[:2]
        x_refs = refs[2:2 + gather.n]
        o_ref = refs[2 + gather.n]
        got_refs = refs[3 + gather.n:3 + 2 * gather.n]
        sems = refs[3 + 2 * gather.n:]
        i, j = pl.program_id(0), pl.program_id(1)

        @pl.when((i == 0) & (j == 0))
        def _():
            gather.start(x_refs, got_refs, sems)

        o_ref[...] = _dot(a_ref[...], b_ref[...]).astype(o_ref.dtype)

        @pl.when((i == ni - 1) & (j == nj - 1))
        def _():
            gather.finish(x_refs, got_refs, sems)

    return _pallas(
        body, name="in_proj", grid=(ni, nj),
        in_specs=[pl.BlockSpec((tm, d), lambda i, j: (i, 0)), pl.BlockSpec((d, tn), lambda i, j: (0, j))]
        + gather.in_specs,
        out_specs=[pl.BlockSpec((tm, tn), lambda i, j: (i, j))] + gather.out_specs,
        out_shape=[jax.ShapeDtypeStruct((t, n), ACT)] + gather.out_shape,
        scratch_shapes=gather.scratch,
        compiler_params=_cparams(("arbitrary", "arbitrary"), 56),
    )(h, w_main, *blocks)


def _rms_fwd(x, g, name):
    t, d = x.shape
    tt = 512

    def body(x_ref, g_ref, h_ref):
        xv = x_ref[...]
        r = lax.rsqrt(jnp.mean(xv * xv, axis=-1, keepdims=True) + EPS)
        h_ref[...] = (xv * r * g_ref[...]).astype(h_ref.dtype)

    return _pallas(
        body, name=name, grid=(t // tt,),
        in_specs=[pl.BlockSpec((tt, d), lambda i: (i, 0)), pl.BlockSpec((1, d), lambda i: (0, 0))],
        out_specs=pl.BlockSpec((tt, d), lambda i: (i, 0)),
        out_shape=jax.ShapeDtypeStruct((t, d), BF16),
        compiler_params=_cparams(("parallel",)),
    )(x, g)


def _conv_fwd(proj, conv_w, conv_b, nb, s, off_xbc, n_conv):
    cw = 256
    rc = CONV_ROWS

    def body(x_ref, w_ref, b_ref, o_ref, pre_ref, pad_ref):
        pad_ref[0:SUBLANES, :] = jnp.zeros((SUBLANES, cw), F32)
        pad_ref[SUBLANES:, :] = x_ref[...].astype(F32)
        taps = [w_ref[k:k + 1, :] for k in range(CONV_W)]
        bias = b_ref[...]
        for r in range(s // rc):
            win = pad_ref[pl.ds(r * rc, rc + SUBLANES), :]
            acc = bias + win[SUBLANES:, :] * taps[CONV_W - 1]
            for j in range(1, CONV_W):
                acc = acc + pltpu.roll(win, j, 0)[SUBLANES:, :] * taps[CONV_W - 1 - j]
            pre_ref[pl.ds(r * rc, rc), :] = acc.astype(pre_ref.dtype)
            o_ref[pl.ds(r * rc, rc), :] = (acc * _sigmoid(acc)).astype(o_ref.dtype)

    out = pl.BlockSpec((s, cw), lambda b, j: (b, j))
    return _pallas(
        body, name="conv_fwd", grid=(nb, n_conv // cw),
        in_specs=[pl.BlockSpec((s, cw), lambda b, j: (b, off_xbc // cw + j)),
                  pl.BlockSpec((CONV_W, cw), lambda b, j: (0, j)),
                  pl.BlockSpec((1, cw), lambda b, j: (0, j))],
        out_specs=[out, out],
        out_shape=[jax.ShapeDtypeStruct((nb * s, n_conv), ACT), jax.ShapeDtypeStruct((nb * s, n_conv), ACT)],
        scratch_shapes=[pltpu.VMEM((s + SUBLANES, cw), F32)],
        compiler_params=_cparams(("parallel", "parallel"), 48),
    )(proj, conv_w, conv_b)


def _ssd_common(dtraw, bias, alog, n_heads):
    l = dtraw.shape[0]
    lane = _iota((l, HEAD_PAD), 1)
    lane1 = _iota((1, HEAD_PAD), 1)
    apad = jnp.where(lane1 < n_heads, -jnp.exp(alog), 0.0)
    dt = jnp.where(lane < n_heads, _softplus(dtraw + bias), 0.0)
    a = dt * apad
    tri = (_iota((l, l), 1) <= _iota((l, l), 0)).astype(BF16)
    acs = _exact_l(tri, a)
    width = n_heads * HEAD_DIM
    ex = (_iota((HEAD_PAD, width), 1) // HEAD_DIM == _iota((HEAD_PAD, width), 0)).astype(BF16)
    dt_full = _exact_r(dt, ex)
    acs_full = _exact_r(acs, ex)
    return dt, apad, acs, dt_full, acs_full, tri, ex


def _decay_mats(acs, acs_t, h, mask):
    col = acs[:, h:h + 1]
    row = acs_t[h:h + 1, :]
    return jnp.exp(jnp.where(mask, col - row, NEG))


def _ssd_fwd(xbc, proj, dtraw, dt_bias, a_log, dskip_full, gnorm, nb, s, d_inner):
    l = CHUNK
    nc = s // l
    n_heads = d_inner // HEAD_DIM
    gw = d_inner // GROUPS
    pairs_per_group = gw // LANES
    gn_cols = GROUPS * D_STATE

    def body(xs_ref, bm_ref, cm_ref, z_ref, dtr_ref, bias_ref, alog_ref, dsk_ref, gn_ref,
             y_ref, ya3_ref, prev_ref, state_ref):
        @pl.when(pl.program_id(1) == 0)
        def _():
            state_ref[...] = jnp.zeros_like(state_ref)

        xs = xs_ref[...].astype(F32)
        dt, apad, acs, dt_full, acs_full, tri, ex = _ssd_common(
            dtr_ref[...], bias_ref[...], alog_ref[...], n_heads)
        acs_t = acs.T
        e_full = jnp.exp(acs_full)
        last = acs_full[l - 1:l, :]
        dte = jnp.exp(last - acs_full)
        cd = jnp.exp(last)
        xdt = xs * dt_full
        xd = xdt * dte
        mask = _iota((l, l), 1) <= _iota((l, l), 0)
        lane = _iota((l, LANES), 1)
        prev_ref[0] = state_ref[...]
        for g in range(GROUPS):
            gsl = slice(g * gw, (g + 1) * gw)
            bg = bm_ref[:, g * D_STATE:(g + 1) * D_STATE].astype(BF16)
            cg = cm_ref[:, g * D_STATE:(g + 1) * D_STATE].astype(BF16)
            cb = _dot(cg, bg, NT)
            outs = []
            for q in range(pairs_per_group):
                pair = g * pairs_per_group + q
                h0, h1 = 2 * pair, 2 * pair + 1
                xpb = xdt[:, pair * LANES:(pair + 1) * LANES].astype(BF16)
                m0 = (cb * _decay_mats(acs, acs_t, h0, mask)).astype(BF16)
                m1 = (cb * _decay_mats(acs, acs_t, h1, mask)).astype(BF16)
                zero = jnp.zeros_like(xpb)
                rhs = jnp.concatenate([jnp.where(lane < HEAD_DIM, xpb, zero),
                                       jnp.where(lane >= HEAD_DIM, xpb, zero)], axis=0)
                outs.append(_dot(jnp.concatenate([m0, m1], axis=1), rhs))
            prev_g = state_ref[:, gsl]
            y_off = _dot(cg, prev_g.astype(BF16)) * e_full[:, gsl]
            y_g = jnp.concatenate(outs, axis=1) + y_off
            y_ref[:, gsl] = y_g.astype(y_ref.dtype)
            st = _dot(bg, xd[:, gsl].astype(BF16), TN)
            state_ref[:, gsl] = prev_g * cd[:, gsl] + st
            zv = z_ref[:, gsl].astype(F32)
            blk = (y_g + xs[:, gsl] * dsk_ref[:, gsl]) * (zv * _sigmoid(zv))
            rg = lax.rsqrt(jnp.mean(blk * blk, axis=-1, keepdims=True) + EPS)
            ya3_ref[:, gsl] = (blk * rg * gn_ref[:, gsl]).astype(ya3_ref.dtype)

    row = lambda b, c: b * nc + c
    small = lambda n: pl.BlockSpec((1, n), lambda b, c: (0, 0))
    return _pallas(
        body, name="ssd_fwd", grid=(nb, nc),
        in_specs=[pl.BlockSpec((l, d_inner), lambda b, c: (row(b, c), 0)),
                  pl.BlockSpec((l, gn_cols), lambda b, c: (row(b, c), d_inner // gn_cols)),
                  pl.BlockSpec((l, gn_cols), lambda b, c: (row(b, c), d_inner // gn_cols + 1)),
                  pl.BlockSpec((l, d_inner), lambda b, c: (row(b, c), 0)),
                  pl.BlockSpec((l, HEAD_PAD), lambda b, c: (row(b, c), 0)),
                  small(HEAD_PAD), small(HEAD_PAD), small(d_inner), small(d_inner)],
        out_specs=[pl.BlockSpec((l, d_inner), lambda b, c: (row(b, c), 0)),
                   pl.BlockSpec((l, d_inner), lambda b, c: (row(b, c), 0)),
                   pl.BlockSpec((1, D_STATE, d_inner), lambda b, c: (row(b, c), 0, 0))],
        out_shape=[jax.ShapeDtypeStruct((nb * s, d_inner), ACT),
                   jax.ShapeDtypeStruct((nb * s, d_inner), BF16),
                   jax.ShapeDtypeStruct((nb * nc, D_STATE, d_inner), F32)],
        scratch_shapes=[pltpu.VMEM((D_STATE, d_inner), F32)],
        compiler_params=_cparams(("parallel", "arbitrary"), 48),
    )(xbc, xbc, xbc, proj, dtraw, dt_bias, a_log, dskip_full, gnorm)


def _pool_fwd(proj, nb, s, off_u, d_pool):
    pg = d_pool // len(POOL_WINDOWS)

    def body(u_ref, o_ref):
        g = pl.program_id(1)
        uv = u_ref[...].astype(F32)
        row = _iota(uv.shape, 0)
        pos = (_iota((uv.shape[0], 1), 0) + 1).astype(F32)
        for gi, w in enumerate(POOL_WINDOWS):
            @pl.when(g == gi)
            def _():
                acc, span = uv, 1
                while span < w:
                    acc = acc + _shift_down(acc, span, row)
                    span *= 2
                o_ref[...] = (acc / jnp.minimum(pos, float(w)) - uv).astype(o_ref.dtype)

    return _pallas(
        body, name="pool_fwd", grid=(nb, len(POOL_WINDOWS)),
        in_specs=[pl.BlockSpec((s, pg), lambda b, g: (b, off_u // pg + g))],
        out_specs=pl.BlockSpec((s, pg), lambda b, g: (b, g)),
        out_shape=jax.ShapeDtypeStruct((nb * s, d_pool), BF16),
        compiler_params=_cparams(("parallel", "parallel"), 48),
    )(proj)


def _mix_fwd(pooled, mixw, proj, mix_b, scale, off_zp):
    t, d_pool = pooled.shape
    ng, pg = mixw.shape[0], mixw.shape[1]
    tm = min(1024, t)

    def body(a_ref, w_ref, zp_ref, b_ref, s_ref, mixed_ref, yb_ref):
        mixed = _dot(a_ref[...], w_ref[0])
        mixed_ref[...] = mixed.astype(mixed_ref.dtype)
        zp = zp_ref[...].astype(F32)
        yb_ref[...] = ((mixed + b_ref[...]) * s_ref[...] * (zp * _sigmoid(zp))).astype(yb_ref.dtype)

    blk = lambda off: pl.BlockSpec((tm, pg), lambda i, g: (i, off + g))
    vec = pl.BlockSpec((1, pg), lambda i, g: (0, g))
    return _pallas(
        body, name="mix_fwd", grid=(t // tm, ng),
        in_specs=[blk(0), pl.BlockSpec((1, pg, pg), lambda i, g: (g, 0, 0)), blk(off_zp // pg), vec, vec],
        out_specs=[blk(0), blk(0)],
        out_shape=[jax.ShapeDtypeStruct((t, d_pool), ACT), jax.ShapeDtypeStruct((t, d_pool), BF16)],
        compiler_params=_cparams(("parallel", "parallel")),
    )(pooled, mixw, proj, mix_b, scale)


def _merge_fwd(ya3, wa, yb, wb, proj, off_ga, off_gb):
    t = ya3.shape[0]
    d = wa.shape[1]
    tm, tn = 512, d

    def body(a_ref, wa_ref, b_ref, wb_ref, ga_ref, gb_ref, pa_ref, pb_ref, mg_ref):
        pa = _dot(a_ref[...], wa_ref[...])
        pb = _dot(b_ref[...], wb_ref[...])
        pa_ref[...] = pa.astype(pa_ref.dtype)
        pb_ref[...] = pb.astype(pb_ref.dtype)
        mg_ref[...] = (_sigmoid(ga_ref[...].astype(F32)) * pa
                       + _sigmoid(gb_ref[...].astype(F32)) * pb).astype(mg_ref.dtype)

    o = pl.BlockSpec((tm, tn), lambda j, i: (i, j))
    return _pallas(
        body, name="merge_fwd", grid=(d // tn, t // tm),
        in_specs=[pl.BlockSpec((tm, ya3.shape[1]), lambda j, i: (i, 0)),
                  pl.BlockSpec((wa.shape[0], tn), lambda j, i: (0, j)),
                  pl.BlockSpec((tm, yb.shape[1]), lambda j, i: (i, 0)),
                  pl.BlockSpec((wb.shape[0], tn), lambda j, i: (0, j)),
                  pl.BlockSpec((tm, tn), lambda j, i: (i, off_ga // tn + j)),
                  pl.BlockSpec((tm, tn), lambda j, i: (i, off_gb // tn + j))],
        out_specs=[o, o, o],
        out_shape=[jax.ShapeDtypeStruct((t, d), ACT), jax.ShapeDtypeStruct((t, d), ACT),
                   jax.ShapeDtypeStruct((t, d), BF16)],
        compiler_params=_cparams(("parallel", "parallel"), 56),
    )(ya3, wa, yb, wb, proj, proj)


def _out_fwd(merged, wout, x, ple_g):
    t, d = x.shape
    tm = 512

    def body(m_ref, w_ref, x_ref, g_ref, x1_ref, h2_ref):
        x1 = x_ref[...] + _dot(m_ref[...], w_ref[...])
        x1_ref[...] = x1
        r = lax.rsqrt(jnp.mean(x1 * x1, axis=-1, keepdims=True) + EPS)
        h2_ref[...] = (x1 * r * g_ref[...]).astype(h2_ref.dtype)

    rowblk = pl.BlockSpec((tm, d), lambda i: (i, 0))
    return _pallas(
        body, name="out_fwd", grid=(t // tm,),
        in_specs=[rowblk, pl.BlockSpec((d, d), lambda i: (0, 0)), rowblk, pl.BlockSpec((1, d), lambda i: (0, 0))],
        out_specs=[rowblk, rowblk],
        out_shape=[jax.ShapeDtypeStruct((t, d), F32), jax.ShapeDtypeStruct((t, d), BF16)],
        compiler_params=_cparams(("parallel",), 56),
    )(merged, wout, x, ple_g)


def _final(h2, wg, p, wup, x1, target, final_g):
    t, d = x1.shape
    pd = p.shape[1]
    tm = 256

    def body(h2_ref, wg_ref, p_ref, wup_ref, x1_ref, tg_ref, g_ref,
             dx2_ref, dgl_ref, dpu_ref, loss_ref, dfg_ref):
        @pl.when(pl.program_id(0) == 0)
        def _():
            loss_ref[...] = jnp.zeros_like(loss_ref)
            dfg_ref[...] = jnp.zeros_like(dfg_ref)

        gate = _sigmoid(_dot(h2_ref[...], wg_ref[...]))
        pu = _dot(p_ref[...].astype(BF16), wup_ref[...])
        x2 = x1_ref[...] + gate * pu
        r = lax.rsqrt(jnp.mean(x2 * x2, axis=-1, keepdims=True) + EPS)
        xh = x2 * r
        gfin = g_ref[...]
        err = xh * gfin - tg_ref[...]
        loss_ref[...] += _sum_all(err * err) * (0.5 / d)
        dy = err * (1.0 / d)
        dfg_ref[...] += jnp.sum(dy * xh, axis=0, keepdims=True)
        dg = dy * gfin
        dx2 = r * (dg - xh * jnp.mean(dg * xh, axis=-1, keepdims=True))
        dx2_ref[...] = dx2
        dgl_ref[...] = (dx2 * pu * gate * (1.0 - gate)).astype(dgl_ref.dtype)
        dpu_ref[...] = (dx2 * gate).astype(dpu_ref.dtype)

    rowblk = pl.BlockSpec((tm, d), lambda i: (i, 0))
    full = lambda a, b: pl.BlockSpec((a, b), lambda i: (0, 0))
    return _pallas(
        body, name="final_fwd_bwd", grid=(t // tm,),
        in_specs=[rowblk, full(d, d), pl.BlockSpec((tm, pd), lambda i: (i, 0)), full(pd, d), rowblk, rowblk,
                  full(1, d)],
        out_specs=[rowblk, rowblk, rowblk, full(1, LANES), full(1, d)],
        out_shape=[jax.ShapeDtypeStruct((t, d), F32), jax.ShapeDtypeStruct((t, d), BF16),
                   jax.ShapeDtypeStruct((t, d), BF16), jax.ShapeDtypeStruct((1, LANES), F32),
                   jax.ShapeDtypeStruct((1, d), F32)],
        compiler_params=_cparams(("arbitrary",), 56),
    )(h2, wg, p, wup, x1, target, final_g)


def _ple_bwd(dgl, wg, dx2, x1, ple_g):
    t, d = x1.shape
    tm = 512

    def body(dgl_ref, wg_ref, dx2_ref, x1_ref, g_ref, dx1_ref, dx1b_ref, dg_ref):
        @pl.when(pl.program_id(0) == 0)
        def _():
            dg_ref[...] = jnp.zeros_like(dg_ref)

        dh2 = _dot(dgl_ref[...], wg_ref[...], NT)
        x1 = x1_ref[...]
        r = lax.rsqrt(jnp.mean(x1 * x1, axis=-1, keepdims=True) + EPS)
        xh = x1 * r
        dg_ref[...] += jnp.sum(dh2 * xh, axis=0, keepdims=True)
        dg = dh2 * g_ref[...]
        dx1 = dx2_ref[...] + r * (dg - xh * jnp.mean(dg * xh, axis=-1, keepdims=True))
        dx1_ref[...] = dx1
        dx1b_ref[...] = dx1.astype(dx1b_ref.dtype)

    rowblk = pl.BlockSpec((tm, d), lambda i: (i, 0))
    full = lambda a, b: pl.BlockSpec((a, b), lambda i: (0, 0))
    return _pallas(
        body, name="ple_bwd", grid=(t // tm,),
        in_specs=[rowblk, full(d, d), rowblk, rowblk, full(1, d)],
        out_specs=[rowblk, rowblk, full(1, d)],
        out_shape=[jax.ShapeDtypeStruct((t, d), F32), jax.ShapeDtypeStruct((t, d), BF16),
                   jax.ShapeDtypeStruct((1, d), F32)],
        compiler_params=_cparams(("arbitrary",), 56),
    )(dgl, wg, dx2, x1, ple_g)


def _merge_bwd(dx1b, wout, proj, pa, pb, n_main, off_ga, off_gb):
    t, d = pa.shape
    tm = 512

    def body(dx_ref, w_ref, ga_ref, gb_ref, pa_ref, pb_ref, dpa_ref, dpb_ref, dgate_ref):
        dm = _dot(dx_ref[...], w_ref[...], NT)
        sa = _sigmoid(ga_ref[...].astype(F32))
        sb = _sigmoid(gb_ref[...].astype(F32))
        dpa_ref[...] = (dm * sa).astype(dpa_ref.dtype)
        dpb_ref[...] = (dm * sb).astype(dpb_ref.dtype)
        dgate_ref[:, :d] = (dm * pa_ref[...].astype(F32) * sa * (1.0 - sa)).astype(dgate_ref.dtype)
        dgate_ref[:, d:] = (dm * pb_ref[...].astype(F32) * sb * (1.0 - sb)).astype(dgate_ref.dtype)

    assert off_gb == off_ga + d and off_ga % (2 * d) == 0
    rowblk = pl.BlockSpec((tm, d), lambda i: (i, 0))
    return _pallas(
        body, name="merge_bwd", grid=(t // tm,),
        in_specs=[rowblk, pl.BlockSpec((d, d), lambda i: (0, 0)),
                  pl.BlockSpec((tm, d), lambda i: (i, off_ga // d)),
                  pl.BlockSpec((tm, d), lambda i: (i, off_gb // d)), rowblk, rowblk],
        out_specs=[rowblk, rowblk, pl.BlockSpec((tm, 2 * d), lambda i: (i, off_ga // (2 * d)))],
        out_shape=[jax.ShapeDtypeStruct((t, d), BF16), jax.ShapeDtypeStruct((t, d), BF16),
                   jax.ShapeDtypeStruct((t, n_main), BF16)],
        compiler_params=_cparams(("parallel",), 56),
    )(dx1b, wout, proj, proj, pa, pb)


def _yb_bwd(dproj, dpb, wb, proj, mixed, mix_b, scale, off_zp):
    t, dp = mixed.shape
    tm = 512

    def body(_, dpb_ref, w_ref, zp_ref, mx_ref, b_ref, s_ref, dzp_ref, dmx_ref, dsc_ref, db_ref):
        @pl.when(pl.program_id(0) == 0)
        def _():
            dsc_ref[...] = jnp.zeros_like(dsc_ref)
            db_ref[...] = jnp.zeros_like(db_ref)

        dyb = _dot(dpb_ref[...], w_ref[...], NT)
        zp = zp_ref[...].astype(F32)
        sg = _sigmoid(zp)
        mb = mx_ref[...].astype(F32) + b_ref[...]
        yb0 = mb * s_ref[...]
        dyb0 = dyb * (zp * sg)
        dzp_ref[...] = (dyb * yb0 * (sg * (1.0 + zp * (1.0 - sg)))).astype(dzp_ref.dtype)
        dsc_ref[...] += jnp.sum(dyb0 * mb, axis=0, keepdims=True)
        dmx = dyb0 * s_ref[...]
        db_ref[...] += jnp.sum(dmx, axis=0, keepdims=True)
        dmx_ref[...] = dmx.astype(dmx_ref.dtype)

    rowblk = pl.BlockSpec((tm, dp), lambda i: (i, 0))
    vec = pl.BlockSpec((1, dp), lambda i: (0, 0))
    seg = pl.BlockSpec((tm, dp), lambda i: (i, off_zp // dp))
    return _pallas(
        body, name="yb_bwd", grid=(t // tm,),
        in_specs=[ANY, pl.BlockSpec((tm, wb.shape[1]), lambda i: (i, 0)),
                  pl.BlockSpec(wb.shape, lambda i: (0, 0)), seg, rowblk, vec, vec],
        out_specs=[seg, rowblk, vec, vec],
        out_shape=[jax.ShapeDtypeStruct(dproj.shape, dproj.dtype), jax.ShapeDtypeStruct((t, dp), BF16),
                   jax.ShapeDtypeStruct((1, dp), F32), jax.ShapeDtypeStruct((1, dp), F32)],
        input_output_aliases={0: 0},
        compiler_params=_cparams(("arbitrary",), 56),
    )(dproj, dpb, wb, proj, mixed, mix_b, scale)


def _dpooled(dmixed, mixw):
    t, dp = dmixed.shape
    ng, pg = mixw.shape[0], mixw.shape[1]
    tm = min(1024, t)

    def body(a_ref, w_ref, o_ref):
        o_ref[...] = _dot(a_ref[...], w_ref[0], NT).astype(o_ref.dtype)

    blk = pl.BlockSpec((tm, pg), lambda i, g: (i, g))
    return _pallas(
        body, name="dpooled", grid=(t // tm, ng),
        in_specs=[blk, pl.BlockSpec((1, pg, pg), lambda i, g: (g, 0, 0))],
        out_specs=blk, out_shape=jax.ShapeDtypeStruct((t, dp), ACT),
        compiler_params=_cparams(("parallel", "parallel")),
    )(dmixed, mixw)


def _dmixw(pooled, dmixed, ng):
    t, dp = pooled.shape
    pg = dp // ng
    tk = min(2048, t)
    nk = t // tk

    def body(a_ref, b_ref, o_ref):
        @pl.when(pl.program_id(1) == 0)
        def _():
            o_ref[...] = jnp.zeros_like(o_ref)

        o_ref[0] += _dot(a_ref[...], b_ref[...], TN)

    blk = pl.BlockSpec((tk, pg), lambda g, k: (k, g))
    return _pallas(
        body, name="dmixw", grid=(ng, nk),
        in_specs=[blk, blk],
        out_specs=pl.BlockSpec((1, pg, pg), lambda g, k: (g, 0, 0)),
        out_shape=jax.ShapeDtypeStruct((ng, pg, pg), F32),
        compiler_params=_cparams(("parallel", "arbitrary")),
    )(pooled, dmixed)


def _pool_bwd(dproj, dpooled, nb, s, off_u):
    dp = dpooled.shape[1]
    pg = dp // len(POOL_WINDOWS)

    def body(_, d_ref, o_ref):
        g = pl.program_id(1)
        dv = d_ref[...].astype(F32)
        row = _iota(dv.shape, 0)
        pos = (_iota((dv.shape[0], 1), 0) + 1).astype(F32)
        for gi, w in enumerate(POOL_WINDOWS):
            @pl.when(g == gi)
            def _():
                acc, span = dv / jnp.minimum(pos, float(w)), 1
                while span < w:
                    acc = acc + _shift_up(acc, span, row)
                    span *= 2
                o_ref[...] = (acc - dv).astype(o_ref.dtype)

    return _pallas(
        body, name="pool_bwd", grid=(nb, len(POOL_WINDOWS)),
        in_specs=[ANY, pl.BlockSpec((s, pg), lambda b, g: (b, g))],
        out_specs=pl.BlockSpec((s, pg), lambda b, g: (b, off_u // pg + g)),
        out_shape=jax.ShapeDtypeStruct(dproj.shape, dproj.dtype),
        input_output_aliases={0: 0},
        compiler_params=_cparams(("parallel", "parallel"), 48),
    )(dproj, dpooled)


def _ssd_bwd(dproj, dya3, y, xbc, proj, dtraw, prev, dt_bias, a_log, dskip_full, gnorm, nb, s, d_inner, partials):
    l = CHUNK
    nc = s // l
    n_heads = d_inner // HEAD_DIM
    gw = d_inner // GROUPS
    pairs_per_group = gw // LANES
    gn_cols = GROUPS * D_STATE
    n_conv = d_inner + 2 * gn_cols
    last_step = nb * nc - 1

    xch = _ChipExchange(partials)
    n_in, n_out = 13, 7

    def body(*refs):
        (_, dya3_ref, y_ref, xs_ref, bm_ref, cm_ref, z_ref, dtr_ref, prev_ref, bias_ref, alog_ref,
         dsk_ref, gn_ref) = refs[:n_in]
        p_refs = refs[n_in:n_in + xch.n]
        outs = refs[n_in + xch.n:]
        dz_ref, dxbc_ref, ddt_ref, dgn_ref, dskh_ref, dalog_ref, dbias_ref = outs[:n_out]
        got_refs = outs[n_out:n_out + xch.n]
        g_ref, dskacc_ref = outs[n_out + xch.n:n_out + xch.n + 2]
        sems = outs[n_out + xch.n + 2:]
        step = pl.program_id(0) * nc + pl.program_id(1)

        @pl.when(pl.program_id(1) == 0)
        def _():
            g_ref[...] = jnp.zeros_like(g_ref)

        @pl.when(step == 0)
        def _():
            xch.start(p_refs, got_refs, sems)
            dgn_ref[...] = jnp.zeros_like(dgn_ref)
            dalog_ref[...] = jnp.zeros_like(dalog_ref)
            dbias_ref[...] = jnp.zeros_like(dbias_ref)
            dskacc_ref[...] = jnp.zeros_like(dskacc_ref)

        xs = xs_ref[...].astype(F32)
        zv = z_ref[...].astype(F32)
        dsk = dsk_ref[...]
        sg = _sigmoid(zv)
        sz = zv * sg
        ya1 = y_ref[...].astype(F32) + xs * dsk
        ya2 = ya1 * sz
        parts = []
        for g in range(GROUPS):
            gsl = slice(g * gw, (g + 1) * gw)
            blk = ya2[:, gsl]
            rg = lax.rsqrt(jnp.mean(blk * blk, axis=-1, keepdims=True) + EPS)
            xh = blk * rg
            d3 = dya3_ref[:, gsl].astype(F32)
            dgn_ref[:, gsl] += jnp.sum(d3 * xh, axis=0, keepdims=True)
            dg = d3 * gn_ref[:, gsl]
            parts.append(rg * (dg - xh * jnp.mean(dg * xh, axis=-1, keepdims=True)))
        dya2 = jnp.concatenate(parts, axis=1)
        dy = dya2 * sz
        dz_ref[...] = (dya2 * ya1 * (sg * (1.0 + zv * (1.0 - sg)))).astype(dz_ref.dtype)
        dskacc_ref[...] += jnp.sum(dy * xs, axis=0, keepdims=True)

        dtraw_v = dtr_ref[...]
        bias = bias_ref[...]
        dt, apad, acs, dt_full, acs_full, tri, ex = _ssd_common(dtraw_v, bias, alog_ref[...], n_heads)
        acs_t = acs.T
        e_full = jnp.exp(acs_full)
        last = acs_full[l - 1:l, :]
        dte = jnp.exp(last - acs_full)
        cd = jnp.exp(last)
        xdt = xs * dt_full
        xd = xdt * dte
        dy_e = dy * e_full
        mask = _iota((l, l), 1) <= _iota((l, l), 0)
        lane = _iota((l, LANES), 1)
        hlane = _iota((l, HEAD_PAD), 1)
        hsub = _iota((HEAD_PAD, l), 0)
        dacs = jnp.zeros((l, HEAD_PAD), F32)
        dacs_t = jnp.zeros((HEAD_PAD, l), F32)
        dacs_full, dlast, dxdt_all, db_all, dc_all = [], [], [], [], []
        for g in range(GROUPS):
            gsl = slice(g * gw, (g + 1) * gw)
            bg = bm_ref[:, g * D_STATE:(g + 1) * D_STATE].astype(BF16)
            cg = cm_ref[:, g * D_STATE:(g + 1) * D_STATE].astype(BF16)
            prev_g = prev_ref[0, :, gsl]
            prev_b = prev_g.astype(BF16)
            gst = g_ref[:, gsl]
            gst_b = gst.astype(BF16)
            dye_b = dy_e[:, gsl].astype(BF16)
            dc = _dot(dye_b, prev_b, NT)
            dprev = _dot(cg, dye_b, TN)
            y_off = _dot(cg, prev_b) * e_full[:, gsl]
            xd_g = xd[:, gsl]
            db = _dot(xd_g.astype(BF16), gst_b, NT)
            dxd = _dot(bg, gst_b)
            wgt = dxd * xd_g
            dacs_full.append(dy[:, gsl] * y_off - wgt)
            dlast.append(jnp.sum(wgt, axis=0, keepdims=True)
                         + jnp.sum(prev_g * gst, axis=0, keepdims=True) * cd[:, gsl])
            g_ref[:, gsl] = dprev + cd[:, gsl] * gst
            dxdt_g = dxd * dte[:, gsl]
            cb = _dot(cg, bg, NT)
            dcb = jnp.zeros((l, l), F32)
            for q in range(pairs_per_group):
                pair = g * pairs_per_group + q
                psl = slice(pair * LANES, (pair + 1) * LANES)
                xpb = xdt[:, psl].astype(BF16)
                dyp = dy[:, psl].astype(BF16)
                zero = jnp.zeros_like(dyp)
                halves = (jnp.where(lane < HEAD_DIM, dyp, zero), jnp.where(lane >= HEAD_DIM, dyp, zero))
                acc = dxdt_g[:, q * LANES:(q + 1) * LANES]
                for k, dyh in enumerate(halves):
                    h = 2 * pair + k
                    dk = _decay_mats(acs, acs_t, h, mask)
                    mm = cb * dk
                    dm = _dot(dyh, xpb, NT)
                    dcb = dcb + dm * dk
                    dseg = dm * mm
                    dacs = dacs + jnp.where(hlane == h, jnp.sum(dseg, axis=1, keepdims=True), 0.0)
                    dacs_t = dacs_t + jnp.where(hsub == h, jnp.sum(dseg, axis=0, keepdims=True), 0.0)
                    acc = acc + _dot(mm.astype(BF16), dyh, TN)
                dxdt_all.append(acc)
            dcb_b = dcb.astype(BF16)
            dc_all.append(dc + _dot(dcb_b, bg))
            db_all.append(db + _dot(dcb_b, cg, TN))

        dxdt = jnp.concatenate(dxdt_all, axis=1)
        dxbc_ref[:, :d_inner] = (dy * dsk + dxdt * dt_full).astype(dxbc_ref.dtype)
        dxbc_ref[:, d_inner:d_inner + gn_cols] = jnp.concatenate(db_all, axis=1).astype(dxbc_ref.dtype)
        dxbc_ref[:, d_inner + gn_cols:] = jnp.concatenate(dc_all, axis=1).astype(dxbc_ref.dtype)
        ddt = _exact_r(dxdt * xs, ex, NT)
        rowi = _iota((l, d_inner), 0)
        dacs_f = jnp.concatenate(dacs_full, axis=1) + jnp.where(rowi == l - 1, jnp.concatenate(dlast, axis=1), 0.0)
        dacs = dacs + _exact_r(dacs_f, ex, NT) - dacs_t.T
        da = _exact_l(tri, dacs, TN)
        ddt = ddt + da * apad
        dalog_ref[...] += jnp.sum(da * dt, axis=0, keepdims=True) * apad
        ddt_raw = ddt * _sigmoid(dtraw_v + bias)
        dbias_ref[...] += jnp.sum(ddt_raw, axis=0, keepdims=True)
        ddt_ref[...] = ddt_raw.astype(ddt_ref.dtype)

        @pl.when(step == last_step)
        def _():
            dskh_ref[...] = _exact_r(dskacc_ref[...], ex, NT)[0:1, :]
            xch.wait(p_refs, got_refs, sems)

    row = lambda b, c: b * nc + (nc - 1 - c)
    small = lambda n: pl.BlockSpec((1, n), lambda b, c: (0, 0))
    wide = pl.BlockSpec((l, d_inner), lambda b, c: (row(b, c), 0))
    return _pallas(
        body, name="ssd_bwd", grid=(nb, nc),
        in_specs=[ANY, wide, wide, wide,
                  pl.BlockSpec((l, gn_cols), lambda b, c: (row(b, c), d_inner // gn_cols)),
                  pl.BlockSpec((l, gn_cols), lambda b, c: (row(b, c), d_inner // gn_cols + 1)),
                  wide, pl.BlockSpec((l, HEAD_PAD), lambda b, c: (row(b, c), 0)),
                  pl.BlockSpec((1, D_STATE, d_inner), lambda b, c: (row(b, c), 0, 0)),
                  small(HEAD_PAD), small(HEAD_PAD), small(d_inner), small(d_inner)] + xch.in_specs,
        out_specs=[wide, pl.BlockSpec((l, n_conv), lambda b, c: (row(b, c), 0)),
                   pl.BlockSpec((l, HEAD_PAD), lambda b, c: (row(b, c), 0)),
                   small(d_inner), small(HEAD_PAD), small(HEAD_PAD), small(HEAD_PAD)] + xch.out_specs,
        out_shape=[jax.ShapeDtypeStruct(dproj.shape, dproj.dtype),
                   jax.ShapeDtypeStruct((nb * s, n_conv), ACT),
                   jax.ShapeDtypeStruct((nb * s, HEAD_PAD), BF16),
                   jax.ShapeDtypeStruct((1, d_inner), F32), jax.ShapeDtypeStruct((1, HEAD_PAD), F32),
                   jax.ShapeDtypeStruct((1, HEAD_PAD), F32), jax.ShapeDtypeStruct((1, HEAD_PAD), F32)]
        + xch.out_shape,
        scratch_shapes=[pltpu.VMEM((D_STATE, d_inner), F32), pltpu.VMEM((8, d_inner), F32)] + xch.scratch,
        input_output_aliases={0: 0},
        compiler_params=_cparams(("arbitrary", "arbitrary"), 56),
    )(dproj, dya3, y, xbc, xbc, xbc, proj, dtraw, prev, dt_bias, a_log, dskip_full, gnorm, *partials)


def _conv_bwd(dproj, dxbc, conv_pre, proj, conv_w, nb, s, off_xbc):
    n_conv = dxbc.shape[1]
    cw = 256

    rc = CONV_ROWS

    def fold(v):
        return jnp.sum(v.reshape(rc // SUBLANES, SUBLANES, cw), axis=0)

    def body(_, d_ref, pre_ref, x_ref, w_ref, dx_ref, dw_ref, db_ref, dcpad_ref):
        @pl.when(pl.program_id(1) == 0)
        def _():
            dw_ref[...] = jnp.zeros_like(dw_ref)
            db_ref[...] = jnp.zeros_like(db_ref)

        dcpad_ref[pl.ds(s, SUBLANES), :] = jnp.zeros((SUBLANES, cw), F32)
        taps = [w_ref[k:k + 1, :] for k in range(CONV_W)]
        db = jnp.zeros((SUBLANES, cw), F32)
        for r in range(s // rc):
            rows = pl.ds(r * rc, rc)
            acc = pre_ref[rows, :].astype(F32)
            sg = _sigmoid(acc)
            dc = d_ref[rows, :].astype(F32) * (sg * (1.0 + acc * (1.0 - sg)))
            dcpad_ref[rows, :] = dc
            db = db + fold(dc)
        db_ref[...] += jnp.sum(db, axis=0, keepdims=True)
        dw = [jnp.zeros((SUBLANES, cw), F32) for _ in range(CONV_W)]
        for r in range(s // rc):
            xv = x_ref[pl.ds(r * rc, rc), :].astype(F32)
            win = dcpad_ref[pl.ds(r * rc, rc + SUBLANES), :]
            dxv = jnp.zeros((rc, cw), F32)
            for j in range(CONV_W):
                up = win[:rc, :] if j == 0 else pltpu.roll(win, rc + SUBLANES - j, 0)[:rc, :]
                dxv = dxv + up * taps[CONV_W - 1 - j]
                dw[CONV_W - 1 - j] = dw[CONV_W - 1 - j] + fold(up * xv)
            dx_ref[pl.ds(r * rc, rc), :] = dxv.astype(dx_ref.dtype)
        for k in range(CONV_W):
            dw_ref[k:k + 1, :] += jnp.sum(dw[k], axis=0, keepdims=True)

    blk = pl.BlockSpec((s, cw), lambda j, b: (b, j))
    return _pallas(
        body, name="conv_bwd", grid=(n_conv // cw, nb),
        in_specs=[ANY, blk, blk, pl.BlockSpec((s, cw), lambda j, b: (b, off_xbc // cw + j)),
                  pl.BlockSpec((CONV_W, cw), lambda j, b: (0, j))],
        out_specs=[pl.BlockSpec((s, cw), lambda j, b: (b, off_xbc // cw + j)),
                   pl.BlockSpec((CONV_W, cw), lambda j, b: (0, j)), pl.BlockSpec((1, cw), lambda j, b: (0, j))],
        out_shape=[jax.ShapeDtypeStruct(dproj.shape, dproj.dtype),
                   jax.ShapeDtypeStruct((CONV_W, n_conv), F32), jax.ShapeDtypeStruct((1, n_conv), F32)],
        scratch_shapes=[pltpu.VMEM((s + SUBLANES, cw), F32)],
        input_output_aliases={0: 0},
        compiler_params=_cparams(("parallel", "arbitrary"), 48),
    )(dproj, dxbc, conv_pre, proj, conv_w)


def _in_bwd(dproj, w_main, ddt, w_dt, x, norm_g, dx1, partials):
    t, d = x.shape
    n_main = dproj.shape[1]
    tm, tk = min(1024, t), 1024
    nk = n_main // tk
    ni = t // tm
    xch = _ChipExchange(partials)
    n_in, n_out = 7, 2

    def body(*refs):
        dp_ref, w_ref, ddt_ref, wdt_ref, x_ref, g_ref, dx1_ref = refs[:n_in]
        p_refs = refs[n_in:n_in + xch.n]
        outs = refs[n_in + xch.n:]
        dx_ref, dg_ref = outs[:n_out]
        got_refs = outs[n_out:n_out + xch.n]
        acc_ref = outs[n_out + xch.n]
        sems = outs[n_out + xch.n + 1:]
        i, kk = pl.program_id(0), pl.program_id(1)

        @pl.when(kk == 0)
        def _():
            acc_ref[...] = jnp.zeros_like(acc_ref)

        @pl.when((kk == 0) & (i == 0))
        def _():
            xch.start(p_refs, got_refs, sems)
            dg_ref[...] = jnp.zeros_like(dg_ref)

        acc_ref[...] += _dot(dp_ref[...], w_ref[...], NT)

        @pl.when(kk == nk - 1)
        def _():
            dh = acc_ref[...] + _dot(ddt_ref[...], wdt_ref[...], NT)
            xv = x_ref[...]
            r = lax.rsqrt(jnp.mean(xv * xv, axis=-1, keepdims=True) + EPS)
            xh = xv * r
            dg_ref[...] += jnp.sum(dh * xh, axis=0, keepdims=True)
            dg = dh * g_ref[...]
            dx_ref[...] = dx1_ref[...] + r * (dg - xh * jnp.mean(dg * xh, axis=-1, keepdims=True))

        @pl.when((kk == nk - 1) & (i == ni - 1))
        def _():
            xch.wait(p_refs, got_refs, sems)

    rowblk = pl.BlockSpec((tm, d), lambda i, k: (i, 0))
    vec = pl.BlockSpec((1, d), lambda i, k: (0, 0))
    return _pallas(
        body, name="in_bwd", grid=(ni, nk),
        in_specs=[pl.BlockSpec((tm, tk), lambda i, k: (i, k)), pl.BlockSpec((d, tk), lambda i, k: (0, k)),
                  pl.BlockSpec((tm, HEAD_PAD), lambda i, k: (i, 0)), pl.BlockSpec((d, HEAD_PAD), lambda i, k: (0, 0)),
                  rowblk, vec, rowblk] + xch.in_specs,
        out_specs=[rowblk, vec] + xch.out_specs,
        out_shape=[jax.ShapeDtypeStruct((t, d), F32), jax.ShapeDtypeStruct((1, d), F32)] + xch.out_shape,
        scratch_shapes=[pltpu.VMEM((tm, d), F32)] + xch.scratch,
        compiler_params=_cparams(("arbitrary", "arbitrary"), 56),
    )(dproj, w_main, ddt, w_dt, x, norm_g, dx1, *partials)


def _adamw_math(w, g, m, v):
    m = ADAM_B1 * m + (1.0 - ADAM_B1) * g
    v = ADAM_B2 * v + (1.0 - ADAM_B2) * (g * g)
    m_hat = m / (1.0 - ADAM_B1 ** ADAM_STEP)
    v_hat = v / (1.0 - ADAM_B2 ** ADAM_STEP)
    delta = -ADAM_LR * (m_hat / (jnp.sqrt(v_hat) + ADAM_EPS) + ADAM_WD * w)
    return delta, m, v


def _adamw(parts, w, m, v, name):
    n, rows, cols = parts.shape
    tr = _row_tile(rows, 128) if rows % 8 == 0 else rows

    def body(p_ref, w_ref, m_ref, v_ref, g_ref, d_ref, mo_ref, vo_ref):
        g = p_ref[0].astype(F32)
        for k in range(1, n):
            g = g + p_ref[k].astype(F32)
        d, mn, vn = _adamw_math(w_ref[...], g, m_ref[...], v_ref[...])
        g_ref[...] = g
        d_ref[...] = d
        mo_ref[...] = mn
        vo_ref[...] = vn

    blk = pl.BlockSpec((tr, cols), lambda i: (i, 0))
    shp = jax.ShapeDtypeStruct((rows, cols), F32)
    return _pallas(
        body, name=name, grid=(rows // tr,),
        in_specs=[pl.BlockSpec((n, tr, cols), lambda i: (0, i, 0)), blk, blk, blk],
        out_specs=[blk, blk, blk, blk], out_shape=[shp, shp, shp, shp],
        compiler_params=_cparams(("parallel",), 56),
    )(parts, w, m, v)


def _coords():
    return lax.axis_index("x"), lax.axis_index("y"), lax.axis_index("c")


class _TwoLevelGather:
    def __init__(self, blocks):
        self.n = len(blocks)
        self.in_specs = [ANY] * self.n
        self.out_specs = [ANY] * self.n
        self.out_shape = [jax.ShapeDtypeStruct((N_DEV,) + b.shape, b.dtype) for b in blocks]
        self.scratch = [pltpu.SemaphoreType.DMA((self.n, 7)), pltpu.SemaphoreType.DMA((self.n, 7)),
                        pltpu.SemaphoreType.DMA((self.n,))]

    def _parts(self, x_refs, out_refs, sems):
        send_sems, recv_sems, local_sems = sems
        x, y, c = _coords()
        me, sibling = (x, y, c), (x, y, 1 - c)
        chips = [(1 - x, y), (x, 1 - y), (1 - x, 1 - y)]

        def slot(a, px, py, pc):
            return out_refs[a].at[4 * px + 2 * py + pc]

        def copy(a, k, block, to, src=None):
            return pltpu.make_async_remote_copy(
                src_ref=slot(a, *block) if src is None else src, dst_ref=slot(a, *block),
                send_sem=send_sems.at[a, k], recv_sem=recv_sems.at[a, k], device_id=to, device_id_type=MESH)

        local = [pltpu.make_async_copy(x_refs[a], slot(a, *me), local_sems.at[a]) for a in range(self.n)]
        first = []
        for a in range(self.n):
            first.append(copy(a, 0, me, sibling, src=x_refs[a]))
            first += [copy(a, 1 + j, me, (*chip, c), src=x_refs[a]) for j, chip in enumerate(chips)]
        return copy, local, first, chips, me, sibling, c

    def start(self, x_refs, out_refs, sems):
        _, local, first, *_ = self._parts(x_refs, out_refs, sems)
        for cp in local + first:
            cp.start()

    def finish(self, x_refs, out_refs, sems):
        copy, local, first, chips, me, sibling, c = self._parts(x_refs, out_refs, sems)
        passed = []
        for j, chip in enumerate(chips):
            for a in range(self.n):
                copy(a, 1 + j, (*chip, c), me).wait_recv()
                passed.append(copy(a, 4 + j, (*chip, c), sibling))
                passed[-1].start()
        for a in range(self.n):
            copy(a, 0, sibling, me).wait_recv()
            for j, chip in enumerate(chips):
                copy(a, 4 + j, (*chip, 1 - c), me).wait_recv()
        for cp in first + passed:
            cp.wait_send()
        for cp in local:
            cp.wait()


def _all_gather_routed(blk, name):
    rows, cols = blk.shape
    half = rows // 2
    assert half % 16 == 0

    def body(x_ref, out_ref, send_sems, recv_sems, local_sem):
        x, y, c = _coords()
        me, sib, xn, yn, dg = (x, y, c), (x, y, 1 - c), (1 - x, y, c), (x, 1 - y, c), (1 - x, 1 - y, c)
        other = lambda p: (p[0], p[1], 1 - c)
        top, bot = pl.ds(0, half), pl.ds(half, half)

        def slot(p, part=None):
            ref = out_ref.at[4 * p[0] + 2 * p[1] + p[2]]
            return ref if part is None else ref.at[part]

        def copy(k, block, to, part=None, src=None):
            return pltpu.make_async_remote_copy(
                src_ref=slot(block, part) if src is None else src, dst_ref=slot(block, part),
                send_sem=send_sems.at[k], recv_sem=recv_sems.at[k], device_id=to, device_id_type=MESH)

        mine = pltpu.make_async_copy(x_ref, slot(me), local_sem)
        own = [copy(1, me, xn, src=x_ref), copy(2, me, yn, src=x_ref), copy(0, me, sib, src=x_ref)]
        for cp in [mine] + own:
            cp.start()
        copy(1, xn, me).wait_recv()
        fwd = [copy(3, xn, yn, top), copy(5, xn, sib)]
        for cp in fwd:
            cp.start()
        copy(2, yn, me).wait_recv()
        fwd += [copy(4, yn, xn, bot), copy(6, yn, sib)]
        for cp in fwd[2:]:
            cp.start()
        copy(3, dg, me, top).wait_recv()
        fwd.append(copy(7, dg, sib, top))
        fwd[-1].start()
        copy(4, dg, me, bot).wait_recv()
        fwd.append(copy(8, dg, sib, bot))
        fwd[-1].start()
        copy(0, sib, me).wait_recv()
        copy(5, other(xn), me).wait_recv()
        copy(6, other(yn), me).wait_recv()
        copy(7, other(dg), me, top).wait_recv()
        copy(8, other(dg), me, bot).wait_recv()
        for cp in own + fwd:
            cp.wait_send()
        mine.wait()

    return _pallas(
        body, name=name, in_specs=[ANY], out_specs=ANY,
        out_shape=jax.ShapeDtypeStruct((N_DEV, rows, cols), blk.dtype),
        scratch_shapes=[pltpu.SemaphoreType.DMA((9,)), pltpu.SemaphoreType.DMA((9,)), pltpu.SemaphoreType.DMA],
    )(blk)


def _all_gather_direct(blk, name):
    rows, cols = blk.shape

    def body(x_ref, out_ref, send_sems, recv_sems, local_sem):
        x, y, c = _coords()
        me = 4 * x + 2 * y + c
        mine = pltpu.make_async_copy(x_ref, out_ref.at[me], local_sem)
        mine.start()
        copies = []
        for k in range(1, N_DEV):
            fx, fy, fc = (k >> 2) & 1, (k >> 1) & 1, k & 1
            peer = (x ^ fx, y ^ fy, c ^ fc)
            copies.append(pltpu.make_async_remote_copy(
                src_ref=x_ref, dst_ref=out_ref.at[me], send_sem=send_sems.at[k - 1], recv_sem=recv_sems.at[k - 1],
                device_id=peer, device_id_type=MESH))
        for cp in copies:
            cp.start()
        for k in range(1, N_DEV):
            fx, fy, fc = (k >> 2) & 1, (k >> 1) & 1, k & 1
            src = 4 * (x ^ fx) + 2 * (y ^ fy) + (c ^ fc)
            pltpu.make_async_remote_copy(
                src_ref=x_ref, dst_ref=out_ref.at[src], send_sem=send_sems.at[k - 1], recv_sem=recv_sems.at[k - 1],
                device_id=(x, y, c), device_id_type=MESH).wait_recv()
        for cp in copies:
            cp.wait_send()
        mine.wait()

    return _pallas(
        body, name=name, in_specs=[ANY], out_specs=ANY,
        out_shape=jax.ShapeDtypeStruct((N_DEV, rows, cols), blk.dtype),
        scratch_shapes=[pltpu.SemaphoreType.DMA((7,)), pltpu.SemaphoreType.DMA((7,)), pltpu.SemaphoreType.DMA],
    )(blk)


def _pair_exchange(grads, name):
    n = len(grads)

    def body(*refs):
        g_refs, out_refs = refs[:n], refs[n:2 * n]
        send_sems, recv_sems = refs[2 * n:]
        x, y, c = _coords()
        copies = []
        for a in range(n):
            for k in range(4):
                copies.append(pltpu.make_async_remote_copy(
                    src_ref=g_refs[a].at[2 * k + (1 - c)], dst_ref=out_refs[a].at[k],
                    send_sem=send_sems.at[a, k], recv_sem=recv_sems.at[a, k],
                    device_id=(x, y, 1 - c), device_id_type=MESH))
        for cp in copies:
            cp.start()
        for cp in copies:
            cp.wait()

    return _pallas(
        body, name=name, in_specs=[ANY] * n, out_specs=[ANY] * n,
        out_shape=[jax.ShapeDtypeStruct((4,) + g.shape[1:], g.dtype) for g in grads],
        scratch_shapes=[pltpu.SemaphoreType.DMA((n, 4)), pltpu.SemaphoreType.DMA((n, 4))],
    )(*grads)


def _pair_add(grads, got, core, out_dtype, name):
    _, rows, cols = grads.shape
    tr = _row_tile(rows, 256) if rows % 8 == 0 else rows

    def body(c_ref, a_ref, b_ref, o_ref):
        o_ref[...] = (a_ref[...].astype(F32) + b_ref[...].astype(F32)).astype(o_ref.dtype)

    grid_spec = pltpu.PrefetchScalarGridSpec(
        num_scalar_prefetch=1, grid=(4, rows // tr),
        in_specs=[pl.BlockSpec((1, tr, cols), lambda k, i, c_ref: (2 * k + c_ref[0], i, 0)),
                  pl.BlockSpec((1, tr, cols), lambda k, i, c_ref: (k, i, 0))],
        out_specs=pl.BlockSpec((1, tr, cols), lambda k, i, c_ref: (k, i, 0)))
    return _pallas(
        body, name=name, grid_spec=grid_spec,
        out_shape=jax.ShapeDtypeStruct((4, rows, cols), out_dtype),
        compiler_params=_cparams(("parallel", "parallel"), 48),
    )(core, grads, got)


class _ChipExchange:
    def __init__(self, partials):
        self.n = len(partials)
        self.in_specs = [ANY] * self.n
        self.out_specs = [ANY] * self.n
        self.out_shape = [jax.ShapeDtypeStruct(p.shape, p.dtype) for p in partials]
        self.scratch = [pltpu.SemaphoreType.DMA((self.n, 3)), pltpu.SemaphoreType.DMA((self.n, 3)),
                        pltpu.SemaphoreType.DMA((self.n,))]

    def _copies(self, p_refs, out_refs, sems, receiving):
        send_sems, recv_sems, local_sems = sems
        x, y, c = _coords()
        mychip = 2 * x + y
        local = [pltpu.make_async_copy(p_refs[a].at[mychip], out_refs[a].at[mychip], local_sems.at[a])
                 for a in range(self.n)]
        remote = []
        for a in range(self.n):
            for k in range(1, 4):
                px, py = x ^ ((k >> 1) & 1), y ^ (k & 1)
                peer = 2 * px + py
                remote.append(pltpu.make_async_remote_copy(
                    src_ref=p_refs[a].at[peer], dst_ref=out_refs[a].at[peer if receiving else mychip],
                    send_sem=send_sems.at[a, k - 1], recv_sem=recv_sems.at[a, k - 1],
                    device_id=(x, y, c) if receiving else (px, py, c), device_id_type=MESH))
        return local, remote

    def start(self, p_refs, out_refs, sems):
        local, sends = self._copies(p_refs, out_refs, sems, False)
        for cp in local + sends:
            cp.start()

    def wait(self, p_refs, out_refs, sems):
        local, recvs = self._copies(p_refs, out_refs, sems, True)
        for cp in recvs:
            cp.wait_recv()
        for cp in recvs:
            cp.wait_send()
        for cp in local:
            cp.wait()


def _pad_lanes(v, n=HEAD_PAD):
    return jnp.pad(v, ((0, 0), (0, n - v.shape[1])))


def kernel(x, p, norm_g, w_in, conv_w, conv_b, dt_bias, a_log, d_skip, gnorm_g, pool_mix_w, pool_mix_b, pool_scale, w_branch_a, w_branch_b, w_out, ple_norm_g, w_ple_gate, w_ple_up, final_g, loss_target, m_norm_g, m_w_in, m_conv_w, m_conv_b, m_dt_bias, m_a_log, m_d_skip, m_gnorm_g, m_pool_mix_w, m_pool_mix_b, m_pool_scale, m_w_branch_a, m_w_branch_b, m_w_out, m_ple_norm_g, m_w_ple_gate, m_w_ple_up, m_final_g, v_norm_g, v_w_in, v_conv_w, v_conv_b, v_dt_bias, v_a_log, v_d_skip, v_gnorm_g, v_pool_mix_w, v_pool_mix_b, v_pool_scale, v_w_branch_a, v_w_branch_b, v_w_out, v_ple_norm_g, v_w_ple_gate, v_w_ple_up, v_final_g):
    nb, s, d = x.shape
    t = nb * s
    n_heads = dt_bias.shape[1]
    d_inner = n_heads * HEAD_DIM
    n_conv = conv_b.shape[1]
    d_pool = pool_mix_b.shape[1]
    n_in = w_in.shape[2] * N_DEV
    ple_dim = p.shape[-1]
    o_xbc, o_dt = d_inner, d_inner + n_conv
    o_u = o_dt + n_heads
    o_zp = o_u + d_pool
    o_ga, o_gb = o_zp + d_pool, o_zp + d_pool + d
    n_main = n_in - n_heads
    off_ga, off_gb, off_xbc = d_inner, d_inner + d, d_inner + 2 * d
    off_u = off_xbc + n_conv
    off_zp = off_u + d_pool

    sharded = [w_in[0], conv_w[0], pool_mix_w[0], w_branch_a[0], w_branch_b[0], w_out[0], w_ple_gate[0], w_ple_up[0]]
    sharded_m = [m_w_in[0], m_conv_w[0], m_pool_mix_w[0], m_w_branch_a[0], m_w_branch_b[0], m_w_out[0],
                 m_w_ple_gate[0], m_w_ple_up[0]]
    sharded_v = [v_w_in[0], v_conv_w[0], v_pool_mix_w[0], v_w_branch_a[0], v_w_branch_b[0], v_w_out[0],
                 v_w_ple_gate[0], v_w_ple_up[0]]
    ng = pool_mix_w.shape[1]
    pg = d_pool // ng
    two_d = lambda w: w.reshape(-1, w.shape[-1])
    sharded, sharded_m, sharded_v = ([two_d(w) for w in ws] for ws in (sharded, sharded_m, sharded_v))

    send = [w if i == 1 else w.astype(BF16) for i, w in enumerate(sharded)]
    seg = [_all_gather_routed(send[0], "w_in_all_gather")]
    chunk = n_in // N_DEV
    segments = [((0, o_xbc), 0), ((o_ga, o_gb), off_ga), ((o_gb, n_in), off_gb), ((o_xbc, o_dt), off_xbc),
                ((o_u, o_zp), off_u), ((o_zp, o_ga), off_zp)]

    def gathered_cols(lo, hi):
        out = []
        while lo < hi:
            dev = lo // chunk
            a, b = lo - dev * chunk, min(hi - dev * chunk, chunk)
            out.append(seg[0][dev, :, a:b])
            lo = dev * chunk + b
        return out

    w_main = jnp.concatenate([pc for (lo, hi), _ in segments for pc in gathered_cols(lo, hi)], axis=1)
    w_dt = _pad_lanes(jnp.concatenate(gathered_cols(o_dt, o_u), axis=1))

    x2 = x.reshape(t, d)
    h = _rms_fwd(x2, norm_g, "rms_in")
    proj, *rest = _in_proj(h, w_main, send[1:])
    seg += rest
    cw_full = seg[1].transpose(1, 0, 2).reshape(CONV_W, n_conv)
    mixw_full = seg[2].reshape(N_DEV, ng, pg // N_DEV, pg).transpose(1, 0, 2, 3).reshape(ng, pg, pg)
    wa_full = seg[3].reshape(d_inner, d)
    wb_full = seg[4].reshape(d_pool, d)
    wout_full = seg[5].reshape(d, d)
    wg_full = seg[6].reshape(d, d)
    wup_full = seg[7].transpose(1, 0, 2).reshape(ple_dim, d)

    p2 = p.reshape(t, ple_dim)
    tgt = loss_target.reshape(t, d)
    bias_p, alog_p = _pad_lanes(dt_bias), _pad_lanes(a_log)
    dskip_full = jnp.repeat(d_skip, HEAD_DIM, axis=1)

    dtraw = _matmul(h, w_dt, "nn", F32, "in_proj_dt", 1024, HEAD_PAD, d)
    xbc, conv_pre = _conv_fwd(proj, cw_full, conv_b, nb, s, off_xbc, n_conv)
    y, ya3, prev = _ssd_fwd(xbc, proj, dtraw, bias_p, alog_p, dskip_full, gnorm_g, nb, s, d_inner)
    pooled = _pool_fwd(proj, nb, s, off_u, d_pool)
    mixed, yb = _mix_fwd(pooled, mixw_full, proj, pool_mix_b, pool_scale, off_zp)
    pa, pb, merged = _merge_fwd(ya3, wa_full, yb, wb_full, proj, off_ga, off_gb)
    x1, h2 = _out_fwd(merged, wout_full, x2, ple_norm_g)
    dx2, dgl, dpu, loss_part, dfinal_g = _final(h2, wg_full, p2, wup_full, x1, tgt, final_g.reshape(1, d))

    g_wup = _matmul(p2, dpu, "tn", BF16, "dw_ple_up", 256, 512, t)
    g_wg = _matmul(h2, dgl, "tn", BF16, "dw_ple_gate", 512, 512, t)
    dx1, dx1b, d_ple_g = _ple_bwd(dgl, wg_full, dx2, x1, ple_norm_g)
    g_wout = _matmul(merged, dx1b, "tn", BF16, "dw_out", 512, 512, t)
    dpa, dpb, dproj = _merge_bwd(dx1b, wout_full, proj, pa, pb, n_main, off_ga, off_gb)
    g_wa = _matmul(ya3, dpa, "tn", BF16, "dw_branch_a", 512, 512, t)
    g_wb = _matmul(yb, dpb, "tn", BF16, "dw_branch_b", 512, 512, t)
    dya3 = _matmul(dpa, wa_full, "nt", ACT, "d_ya3", 1024, 1024, d)
    dproj, dmixed, d_scale, d_mixb = _yb_bwd(dproj, dpb, wb_full, proj, mixed, pool_mix_b, pool_scale, off_zp)
    g_mixw = _dmixw(pooled, dmixed, ng)
    dpool = _dpooled(dmixed, mixw_full)
    dproj = _pool_bwd(dproj, dpool, nb, s, off_u)

    core = lax.axis_index("c").astype(I32).reshape(1)

    def pair_sums(by_dest, names):
        by_dest = [g.astype(BF16) for g in by_dest]
        got = _pair_exchange(by_dest, "rs_pair_exchange_" + names[0])
        return [_pair_add(g, r, core, BF16, "rs_pair_add_" + nm) for g, r, nm in zip(by_dest, got, names)]

    names_a = ["pool_mix_w", "w_branch_a", "w_branch_b", "w_out", "w_ple_gate", "w_ple_up"]
    partials_a = pair_sums([
        g_mixw.reshape(ng, N_DEV, pg // N_DEV, pg).transpose(1, 0, 2, 3).reshape(N_DEV, ng * pg // N_DEV, pg),
        g_wa.reshape(N_DEV, d_inner // N_DEV, d), g_wb.reshape(N_DEV, d_pool // N_DEV, d),
        g_wout.reshape(N_DEV, d // N_DEV, d), g_wg.reshape(N_DEV, d // N_DEV, d),
        g_wup.reshape(ple_dim, N_DEV, d // N_DEV).transpose(1, 0, 2)], names_a)
    dproj, dxbc, ddt, d_gn, d_dsk, d_alog, d_dtb, *parts_a = _ssd_bwd(
        dproj, dya3, y, xbc, proj, dtraw, prev, bias_p, alog_p, dskip_full, gnorm_g, nb, s, d_inner, partials_a)
    dproj, g_cw, d_cb = _conv_bwd(dproj, dxbc, conv_pre, proj, cw_full, nb, s, off_xbc)
    g_wmain = _matmul(h, dproj, "tn", BF16, "dw_in", 512, 1024, t)
    g_wdt = _matmul(h, ddt, "tn", BF16, "dw_in_dt", 512, HEAD_PAD, t)

    def dest_cols(dev):
        lo, hi = dev * chunk, (dev + 1) * chunk
        srcs = [((olo, ohi), g_wmain, off) for (olo, ohi), off in segments] + [((o_dt, o_u), g_wdt, 0)]
        out = []
        for (olo, ohi), arr, off in sorted(srcs, key=lambda e: e[0][0]):
            a, b = max(lo, olo), min(hi, ohi)
            if a < b:
                out.append(arr[:, off + a - olo:off + b - olo])
        return jnp.concatenate(out, axis=1)

    names_b = ["w_in", "conv_w"]
    partials_b = pair_sums([jnp.stack([dest_cols(dev) for dev in range(N_DEV)], axis=0),
                            g_cw.reshape(CONV_W, N_DEV, n_conv // N_DEV).transpose(1, 0, 2)], names_b)
    grad_x, d_ng, *parts_b = _in_bwd(dproj, w_main, ddt, w_dt, x2, norm_g, dx1, partials_b)

    names = names_b + names_a
    parts = parts_b + parts_a
    sh_g, sh_d, sh_m, sh_v = [], [], [], []
    shapes = [w_in.shape, conv_w.shape, pool_mix_w.shape, w_branch_a.shape, w_branch_b.shape, w_out.shape,
              w_ple_gate.shape, w_ple_up.shape]
    for pt, w, m, v, nm, shp in zip(parts, sharded, sharded_m, sharded_v, names, shapes):
        res = _adamw(pt, w, m, v, "adamw_" + nm)
        for dst, r in zip((sh_g, sh_d, sh_m, sh_v), res):
            dst.append(r.reshape(shp))

    rep_w = [norm_g, conv_b, gnorm_g, pool_mix_b, pool_scale, ple_norm_g, final_g.reshape(1, d),
             bias_p, alog_p, _pad_lanes(d_skip)]
    rep_m = [m_norm_g, m_conv_b, m_gnorm_g, m_pool_mix_b, m_pool_scale, m_ple_norm_g, m_final_g.reshape(1, d),
             _pad_lanes(m_dt_bias), _pad_lanes(m_a_log), _pad_lanes(m_d_skip)]
    rep_v = [v_norm_g, v_conv_b, v_gnorm_g, v_pool_mix_b, v_pool_scale, v_ple_norm_g, v_final_g.reshape(1, d),
             _pad_lanes(v_dt_bias), _pad_lanes(v_a_log), _pad_lanes(v_d_skip)]
    rep_g = [d_ng, d_cb, d_gn, d_mixb, d_scale, d_ple_g, dfinal_g, d_dtb, d_alog, d_dsk]
    zero = jnp.zeros((1, LANES), F32)
    small = _all_gather_direct(jnp.concatenate(rep_g + [loss_part], axis=1), "small_all_gather")
    pk = lambda ws: jnp.concatenate(ws + [zero], axis=1)
    rg, rd, rm, rv = _adamw(small, pk(rep_w), pk(rep_m), pk(rep_v), "adamw_replicated")
    loss = rg[0, sum(int(w.size) for w in rep_w)]

    def unpack_rep(buf):
        out, pos_ = [], 0
        for w in rep_w:
            out.append(buf[:, pos_:pos_ + w.size])
            pos_ += w.size
        return out

    def arrange(sh, rep):
        rep = list(rep)
        rep[6] = rep[6].reshape(d)
        for i in (7, 8, 9):
            rep[i] = rep[i][:, :n_heads]
        return [rep[0], sh[0], sh[1], rep[1], rep[7], rep[8], rep[9], rep[2], sh[2], rep[3], rep[4],
                sh[3], sh[4], sh[5], rep[5], sh[6], sh[7], rep[6]]

    outs = [loss, grad_x.reshape(nb, s, d)]
    for sh, rep in ((sh_g, rg), (sh_d, rd), (sh_m, rm), (sh_v, rv)):
        outs += arrange(sh, unpack_rep(rep))
    return tuple(outs)
```

```python
import functools

import jax
import jax.numpy as jnp
from jax import lax
from jax.experimental import pallas as pl
from jax.experimental.pallas import tpu as pltpu

F32 = jnp.float32
BF16 = jnp.bfloat16
I32 = jnp.int32
MESH = pl.DeviceIdType.MESH
ANY = pl.BlockSpec(memory_space=pl.ANY)

EPS = 1e-6
CHUNK = 128
HEAD_DIM = 64
D_STATE = 128
GROUPS = 4
CONV_W = 4
POOL_WINDOWS = (2, 4, 8, 16)
ACT = jnp.bfloat16
LANES = 128
SUBLANES = 8
CONV_ROWS = 64
HEAD_PAD = 128
NEG = -1e30
N_DEV = 8

ADAM_LR = 0.001
ADAM_B1 = 0.9
ADAM_B2 = 0.999
ADAM_EPS = 1e-08
ADAM_WD = 0.01
ADAM_STEP = 10

NN = (((1,), (0,)), ((), ()))
NT = (((1,), (1,)), ((), ()))
TN = (((0,), (0,)), ((), ()))


def _pallas(body, **kw):
    return pl.pallas_call(body, **kw)


def _cparams(dims=None, vmem_mb=None):
    kw = {}
    if dims is not None:
        kw["dimension_semantics"] = dims
    if vmem_mb is not None:
        kw["vmem_limit_bytes"] = vmem_mb << 20
    return pltpu.CompilerParams(**kw)


def _dot(a, b, dims=NN):
    return lax.dot_general(a, b, dims, preferred_element_type=F32)


def _split3(v):
    hi = v.astype(BF16)
    r = v - hi.astype(F32)
    mid = r.astype(BF16)
    lo = (r - mid.astype(F32)).astype(BF16)
    return hi, mid, lo


def _exact_r(v, m, dims=NN):
    hi, mid, lo = _split3(v)
    return _dot(hi, m, dims) + _dot(mid, m, dims) + _dot(lo, m, dims)


def _exact_l(m, v, dims=NN):
    hi, mid, lo = _split3(v)
    return _dot(m, hi, dims) + _dot(m, mid, dims) + _dot(m, lo, dims)


def _row_tile(rows, target):
    best = None
    for cand in range(8, min(rows, target) + 1, 8):
        if rows % cand == 0:
            best = cand
    assert best is not None, rows
    return best


def _sum_all(v):
    return jnp.sum(jnp.sum(v, axis=1, keepdims=True), axis=0, keepdims=True)


def _iota(shape, dim):
    return lax.broadcasted_iota(I32, shape, dim)


def _sigmoid(x):
    return 0.5 * jnp.tanh(0.5 * x) + 0.5


def _softplus(x):
    return jnp.maximum(x, 0.0) + jnp.log(1.0 + jnp.exp(-jnp.abs(x)))


def _shift_down(v, j, row):
    return jnp.where(row >= j, pltpu.roll(v, j, 0), 0.0)


def _shift_up(v, j, row):
    n = v.shape[0]
    return jnp.where(row < n - j, pltpu.roll(v, n - j, 0), 0.0)


def _matmul(a, b, mode, out_dtype, name, tm, tn, tk):
    if mode == "nn":
        (m, k), n = a.shape, b.shape[1]
    elif mode == "nt":
        (m, k), n = a.shape, b.shape[0]
    else:
        (k, m), n = a.shape, b.shape[1]
    tm, tn, tk = min(tm, m), min(tn, n), min(tk, k)
    assert m % tm == 0 and n % tn == 0 and k % tk == 0, (name, m, n, k)
    nk = k // tk
    dims = {"nn": NN, "nt": NT, "tn": TN}[mode]

    def body_acc(a_ref, b_ref, o_ref, acc_ref):
        kk = pl.program_id(2)

        @pl.when(kk == 0)
        def _():
            acc_ref[...] = jnp.zeros_like(acc_ref)

        acc_ref[...] += _dot(a_ref[...].astype(BF16), b_ref[...].astype(BF16), dims)

        @pl.when(kk == nk - 1)
        def _():
            o_ref[...] = acc_ref[...].astype(o_ref.dtype)

    def body_one(a_ref, b_ref, o_ref):
        o_ref[...] = _dot(a_ref[...].astype(BF16), b_ref[...].astype(BF16), dims).astype(o_ref.dtype)

    body = body_one if nk == 1 else body_acc
    if mode == "tn":
        a_spec = pl.BlockSpec((tk, tm), lambda i, j, kk: (kk, i))
    else:
        a_spec = pl.BlockSpec((tm, tk), lambda i, j, kk: (i, kk))
    if mode == "nt":
        b_spec = pl.BlockSpec((tn, tk), lambda i, j, kk: (j, kk))
    else:
        b_spec = pl.BlockSpec((tk, tn), lambda i, j, kk: (kk, j))
    return _pallas(
        body, name=name, grid=(m // tm, n // tn, nk),
        in_specs=[a_spec, b_spec],
        out_specs=pl.BlockSpec((tm, tn), lambda i, j, kk: (i, j)),
        out_shape=jax.ShapeDtypeStruct((m, n), out_dtype),
        scratch_shapes=[] if nk == 1 else [pltpu.VMEM((tm, tn), F32)],
        compiler_params=_cparams(("parallel", "parallel", "arbitrary"), 56),
    )(a, b)


def _in_proj(h, w_main, blocks):
    t, d = h.shape
    n = w_main.shape[1]
    tm, tn = min(1024, t), 1024
    ni, nj = t // tm, n // tn
    gather = _TwoLevelGather(blocks)

    def body(*refs):
        a_ref, b_ref = refs[:2]
        x_refs = refs[2:2 + gather.n]
        o_ref = refs[2 + gather.n]
        got_refs = refs[3 + gather.n:3 + 2 * gather.n]
        sems = refs[3 + 2 * gather.n:]
        i, j = pl.program_id(0), pl.program_id(1)

        @pl.when((i == 0) & (j == 0))
        def _():
            gather.start(x_refs, got_refs, sems)

        o_ref[...] = _dot(a_ref[...], b_ref[...]).astype(o_ref.dtype)

        @pl.when((i == ni - 1) & (j == nj - 1))
        def _():
            gather.finish(x_refs, got_refs, sems)

    return _pallas(
        body, name="in_proj", grid=(ni, nj),
        in_specs=[pl.BlockSpec((tm, d), lambda i, j: (i, 0)), pl.BlockSpec((d, tn), lambda i, j: (0, j))]
        + gather.in_specs,
        out_specs=[pl.BlockSpec((tm, tn), lambda i, j: (i, j))] + gather.out_specs,
        out_shape=[jax.ShapeDtypeStruct((t, n), ACT)] + gather.out_shape,
        scratch_shapes=gather.scratch,
        compiler_params=_cparams(("arbitrary", "arbitrary"), 56),
    )(h, w_main, *blocks)


def _rms_fwd(x, g, name):
    t, d = x.shape
    tt = 512

    def body(x_ref, g_ref, h_ref):
        xv = x_ref[...]
        r = lax.rsqrt(jnp.mean(xv * xv, axis=-1, keepdims=True) + EPS)
        h_ref[...] = (xv * r * g_ref[...]).astype(h_ref.dtype)

    return _pallas(
        body, name=name, grid=(t // tt,),
        in_specs=[pl.BlockSpec((tt, d), lambda i: (i, 0)), pl.BlockSpec((1, d), lambda i: (0, 0))],
        out_specs=pl.BlockSpec((tt, d), lambda i: (i, 0)),
        out_shape=jax.ShapeDtypeStruct((t, d), BF16),
        compiler_params=_cparams(("parallel",)),
    )(x, g)


def _conv_fwd(proj, conv_w, conv_b, nb, s, off_xbc, n_conv):
    cw = 256
    rc = CONV_ROWS

    def body(x_ref, w_ref, b_ref, o_ref, pre_ref, pad_ref):
        pad_ref[0:SUBLANES, :] = jnp.zeros((SUBLANES, cw), F32)
        pad_ref[SUBLANES:, :] = x_ref[...].astype(F32)
        taps = [w_ref[k:k + 1, :] for k in range(CONV_W)]
        bias = b_ref[...]
        for r in range(s // rc):
            win = pad_ref[pl.ds(r * rc, rc + SUBLANES), :]
            acc = bias + win[SUBLANES:, :] * taps[CONV_W - 1]
            for j in range(1, CONV_W):
                acc = acc + pltpu.roll(win, j, 0)[SUBLANES:, :] * taps[CONV_W - 1 - j]
            pre_ref[pl.ds(r * rc, rc), :] = acc.astype(pre_ref.dtype)
            o_ref[pl.ds(r * rc, rc), :] = (acc * _sigmoid(acc)).astype(o_ref.dtype)

    out = pl.BlockSpec((s, cw), lambda b, j: (b, j))
    return _pallas(
        body, name="conv_fwd", grid=(nb, n_conv // cw),
        in_specs=[pl.BlockSpec((s, cw), lambda b, j: (b, off_xbc // cw + j)),
                  pl.BlockSpec((CONV_W, cw), lambda b, j: (0, j)),
                  pl.BlockSpec((1, cw), lambda b, j: (0, j))],
        out_specs=[out, out],
        out_shape=[jax.ShapeDtypeStruct((nb * s, n_conv), ACT), jax.ShapeDtypeStruct((nb * s, n_conv), ACT)],
        scratch_shapes=[pltpu.VMEM((s + SUBLANES, cw), F32)],
        compiler_params=_cparams(("parallel", "parallel"), 48),
    )(proj, conv_w, conv_b)


def _ssd_common(dtraw, bias, alog, n_heads):
    l = dtraw.shape[0]
    lane = _iota((l, HEAD_PAD), 1)
    lane1 = _iota((1, HEAD_PAD), 1)
    apad = jnp.where(lane1 < n_heads, -jnp.exp(alog), 0.0)
    dt = jnp.where(lane < n_heads, _softplus(dtraw + bias), 0.0)
    a = dt * apad
    tri = (_iota((l, l), 1) <= _iota((l, l), 0)).astype(BF16)
    acs = _exact_l(tri, a)
    width = n_heads * HEAD_DIM
    ex = (_iota((HEAD_PAD, width), 1) // HEAD_DIM == _iota((HEAD_PAD, width), 0)).astype(BF16)
    dt_full = _exact_r(dt, ex)
    acs_full = _exact_r(acs, ex)
    return dt, apad, acs, dt_full, acs_full, tri, ex


def _decay_mats(acs, acs_t, h, mask):
    col = acs[:, h:h + 1]
    row = acs_t[h:h + 1, :]
    return jnp.exp(jnp.where(mask, col - row, NEG))


def _ssd_fwd(xbc, proj, dtraw, dt_bias, a_log, dskip_full, gnorm, nb, s, d_inner):
    l = CHUNK
    nc = s // l
    n_heads = d_inner // HEAD_DIM
    gw = d_inner // GROUPS
    pairs_per_group = gw // LANES
    gn_cols = GROUPS * D_STATE

    def body(xs_ref, bm_ref, cm_ref, z_ref, dtr_ref, bias_ref, alog_ref, dsk_ref, gn_ref,
             y_ref, ya3_ref, prev_ref, state_ref):
        @pl.when(pl.program_id(1) == 0)
        def _():
            state_ref[...] = jnp.zeros_like(state_ref)

        xs = xs_ref[...].astype(F32)
        dt, apad, acs, dt_full, acs_full, tri, ex = _ssd_common(
            dtr_ref[...], bias_ref[...], alog_ref[...], n_heads)
        acs_t = acs.T
        e_full = jnp.exp(acs_full)
        last = acs_full[l - 1:l, :]
        dte = jnp.exp(last - acs_full)
        cd = jnp.exp(last)
        xdt = xs * dt_full
        xd = xdt * dte
        mask = _iota((l, l), 1) <= _iota((l, l), 0)
        lane = _iota((l, LANES), 1)
        prev_ref[0] = state_ref[...]
        for g in range(GROUPS):
            gsl = slice(g * gw, (g + 1) * gw)
            bg = bm_ref[:, g * D_STATE:(g + 1) * D_STATE].astype(BF16)
            cg = cm_ref[:, g * D_STATE:(g + 1) * D_STATE].astype(BF16)
            cb = _dot(cg, bg, NT)
            outs = []
            for q in range(pairs_per_group):
                pair = g * pairs_per_group + q
                h0, h1 = 2 * pair, 2 * pair + 1
                xpb = xdt[:, pair * LANES:(pair + 1) * LANES].astype(BF16)
                m0 = (cb * _decay_mats(acs, acs_t, h0, mask)).astype(BF16)
                m1 = (cb * _decay_mats(acs, acs_t, h1, mask)).astype(BF16)
                zero = jnp.zeros_like(xpb)
                rhs = jnp.concatenate([jnp.where(lane < HEAD_DIM, xpb, zero),
                                       jnp.where(lane >= HEAD_DIM, xpb, zero)], axis=0)
                outs.append(_dot(jnp.concatenate([m0, m1], axis=1), rhs))
            prev_g = state_ref[:, gsl]
            y_off = _dot(cg, prev_g.astype(BF16)) * e_full[:, gsl]
            y_g = jnp.concatenate(outs, axis=1) + y_off
            y_ref[:, gsl] = y_g.astype(y_ref.dtype)
            st = _dot(bg, xd[:, gsl].astype(BF16), TN)
            state_ref[:, gsl] = prev_g * cd[:, gsl] + st
            zv = z_ref[:, gsl].astype(F32)
            blk = (y_g + xs[:, gsl] * dsk_ref[:, gsl]) * (zv * _sigmoid(zv))
            rg = lax.rsqrt(jnp.mean(blk * blk, axis=-1, keepdims=True) + EPS)
            ya3_ref[:, gsl] = (blk * rg * gn_ref[:, gsl]).astype(ya3_ref.dtype)

    row = lambda b, c: b * nc + c
    small = lambda n: pl.BlockSpec((1, n), lambda b, c: (0, 0))
    return _pallas(
        body, name="ssd_fwd", grid=(nb, nc),
        in_specs=[pl.BlockSpec((l, d_inner), lambda b, c: (row(b, c), 0)),
                  pl.BlockSpec((l, gn_cols), lambda b, c: (row(b, c), d_inner // gn_cols)),
                  pl.BlockSpec((l, gn_cols), lambda b, c: (row(b, c), d_inner // gn_cols + 1)),
                  pl.BlockSpec((l, d_inner), lambda b, c: (row(b, c), 0)),
                  pl.BlockSpec((l, HEAD_PAD), lambda b, c: (row(b, c), 0)),
                  small(HEAD_PAD), small(HEAD_PAD), small(d_inner), small(d_inner)],
        out_specs=[pl.BlockSpec((l, d_inner), lambda b, c: (row(b, c), 0)),
                   pl.BlockSpec((l, d_inner), lambda b, c: (row(b, c), 0)),
                   pl.BlockSpec((1, D_STATE, d_inner), lambda b, c: (row(b, c), 0, 0))],
        out_shape=[jax.ShapeDtypeStruct((nb * s, d_inner), ACT),
                   jax.ShapeDtypeStruct((nb * s, d_inner), BF16),
                   jax.ShapeDtypeStruct((nb * nc, D_STATE, d_inner), F32)],
        scratch_shapes=[pltpu.VMEM((D_STATE, d_inner), F32)],
        compiler_params=_cparams(("parallel", "arbitrary"), 48),
    )(xbc, xbc, xbc, proj, dtraw, dt_bias, a_log, dskip_full, gnorm)


def _pool_fwd(proj, nb, s, off_u, d_pool):
    pg = d_pool // len(POOL_WINDOWS)

    def body(u_ref, o_ref):
        g = pl.program_id(1)
        uv = u_ref[...].astype(F32)
        row = _iota(uv.shape, 0)
        pos = (_iota((uv.shape[0], 1), 0) + 1).astype(F32)
        for gi, w in enumerate(POOL_WINDOWS):
            @pl.when(g == gi)
            def _():
                acc, span = uv, 1
                while span < w:
                    acc = acc + _shift_down(acc, span, row)
                    span *= 2
                o_ref[...] = (acc / jnp.minimum(pos, float(w)) - uv).astype(o_ref.dtype)

    return _pallas(
        body, name="pool_fwd", grid=(nb, len(POOL_WINDOWS)),
        in_specs=[pl.BlockSpec((s, pg), lambda b, g: (b, off_u // pg + g))],
        out_specs=pl.BlockSpec((s, pg), lambda b, g: (b, g)),
        out_shape=jax.ShapeDtypeStruct((nb * s, d_pool), BF16),
        compiler_params=_cparams(("parallel", "parallel"), 48),
    )(proj)


def _mix_fwd(pooled, mixw, proj, mix_b, scale, off_zp):
    t, d_pool = pooled.shape
    ng, pg = mixw.shape[0], mixw.shape[1]
    tm = min(1024, t)

    def body(a_ref, w_ref, zp_ref, b_ref, s_ref, mixed_ref, yb_ref):
        mixed = _dot(a_ref[...], w_ref[0])
        mixed_ref[...] = mixed.astype(mixed_ref.dtype)
        zp = zp_ref[...].astype(F32)
        yb_ref[...] = ((mixed + b_ref[...]) * s_ref[...] * (zp * _sigmoid(zp))).astype(yb_ref.dtype)

    blk = lambda off: pl.BlockSpec((tm, pg), lambda i, g: (i, off + g))
    vec = pl.BlockSpec((1, pg), lambda i, g: (0, g))
    return _pallas(
        body, name="mix_fwd", grid=(t // tm, ng),
        in_specs=[blk(0), pl.BlockSpec((1, pg, pg), lambda i, g: (g, 0, 0)), blk(off_zp // pg), vec, vec],
        out_specs=[blk(0), blk(0)],
        out_shape=[jax.ShapeDtypeStruct((t, d_pool), ACT), jax.ShapeDtypeStruct((t, d_pool), BF16)],
        compiler_params=_cparams(("parallel", "parallel")),
    )(pooled, mixw, proj, mix_b, scale)


def _merge_fwd(ya3, wa, yb, wb, proj, off_ga, off_gb):
    t = ya3.shape[0]
    d = wa.shape[1]
    tm, tn = 512, d

    def body(a_ref, wa_ref, b_ref, wb_ref, ga_ref, gb_ref, pa_ref, pb_ref, mg_ref):
        pa = _dot(a_ref[...], wa_ref[...])
        pb = _dot(b_ref[...], wb_ref[...])
        pa_ref[...] = pa.astype(pa_ref.dtype)
        pb_ref[...] = pb.astype(pb_ref.dtype)
        mg_ref[...] = (_sigmoid(ga_ref[...].astype(F32)) * pa
                       + _sigmoid(gb_ref[...].astype(F32)) * pb).astype(mg_ref.dtype)

    o = pl.BlockSpec((tm, tn), lambda j, i: (i, j))
    return _pallas(
        body, name="merge_fwd", grid=(d // tn, t // tm),
        in_specs=[pl.BlockSpec((tm, ya3.shape[1]), lambda j, i: (i, 0)),
                  pl.BlockSpec((wa.shape[0], tn), lambda j, i: (0, j)),
                  pl.BlockSpec((tm, yb.shape[1]), lambda j, i: (i, 0)),
                  pl.BlockSpec((wb.shape[0], tn), lambda j, i: (0, j)),
                  pl.BlockSpec((tm, tn), lambda j, i: (i, off_ga // tn + j)),
                  pl.BlockSpec((tm, tn), lambda j, i: (i, off_gb // tn + j))],
        out_specs=[o, o, o],
        out_shape=[jax.ShapeDtypeStruct((t, d), ACT), jax.ShapeDtypeStruct((t, d), ACT),
                   jax.ShapeDtypeStruct((t, d), BF16)],
        compiler_params=_cparams(("parallel", "parallel"), 56),
    )(ya3, wa, yb, wb, proj, proj)


def _out_fwd(merged, wout, x, ple_g):
    t, d = x.shape
    tm = 512

    def body(m_ref, w_ref, x_ref, g_ref, x1_ref, h2_ref):
        x1 = x_ref[...] + _dot(m_ref[...], w_ref[...])
        x1_ref[...] = x1
        r = lax.rsqrt(jnp.mean(x1 * x1, axis=-1, keepdims=True) + EPS)
        h2_ref[...] = (x1 * r * g_ref[...]).astype(h2_ref.dtype)

    rowblk = pl.BlockSpec((tm, d), lambda i: (i, 0))
    return _pallas(
        body, name="out_fwd", grid=(t // tm,),
        in_specs=[rowblk, pl.BlockSpec((d, d), lambda i: (0, 0)), rowblk, pl.BlockSpec((1, d), lambda i: (0, 0))],
        out_specs=[rowblk, rowblk],
        out_shape=[jax.ShapeDtypeStruct((t, d), F32), jax.ShapeDtypeStruct((t, d), BF16)],
        compiler_params=_cparams(("parallel",), 56),
    )(merged, wout, x, ple_g)


def _final(h2, wg, p, wup, x1, target, final_g):
    t, d = x1.shape
    pd = p.shape[1]
    tm = 256

    def body(h2_ref, wg_ref, p_ref, wup_ref, x1_ref, tg_ref, g_ref,
             dx2_ref, dgl_ref, dpu_ref, loss_ref, dfg_ref):
        @pl.when(pl.program_id(0) == 0)
        def _():
            loss_ref[...] = jnp.zeros_like(loss_ref)
            dfg_ref[...] = jnp.zeros_like(dfg_ref)

        gate = _sigmoid(_dot(h2_ref[...], wg_ref[...]))
        pu = _dot(p_ref[...].astype(BF16), wup_ref[...])
        x2 = x1_ref[...] + gate * pu
        r = lax.rsqrt(jnp.mean(x2 * x2, axis=-1, keepdims=True) + EPS)
        xh = x2 * r
        gfin = g_ref[...]
        err = xh * gfin - tg_ref[...]
        loss_ref[...] += _sum_all(err * err) * (0.5 / d)
        dy = err * (1.0 / d)
        dfg_ref[...] += jnp.sum(dy * xh, axis=0, keepdims=True)
        dg = dy * gfin
        dx2 = r * (dg - xh * jnp.mean(dg * xh, axis=-1, keepdims=True))
        dx2_ref[...] = dx2
        dgl_ref[...] = (dx2 * pu * gate * (1.0 - gate)).astype(dgl_ref.dtype)
        dpu_ref[...] = (dx2 * gate).astype(dpu_ref.dtype)

    rowblk = pl.BlockSpec((tm, d), lambda i: (i, 0))
    full = lambda a, b: pl.BlockSpec((a, b), lambda i: (0, 0))
    return _pallas(
        body, name="final_fwd_bwd", grid=(t // tm,),
        in_specs=[rowblk, full(d, d), pl.BlockSpec((tm, pd), lambda i: (i, 0)), full(pd, d), rowblk, rowblk,
                  full(1, d)],
        out_specs=[rowblk, rowblk, rowblk, full(1, LANES), full(1, d)],
        out_shape=[jax.ShapeDtypeStruct((t, d), F32), jax.ShapeDtypeStruct((t, d), BF16),
                   jax.ShapeDtypeStruct((t, d), BF16), jax.ShapeDtypeStruct((1, LANES), F32),
                   jax.ShapeDtypeStruct((1, d), F32)],
        compiler_params=_cparams(("arbitrary",), 56),
    )(h2, wg, p, wup, x1, target, final_g)


def _ple_bwd(dgl, wg, dx2, x1, ple_g):
    t, d = x1.shape
    tm = 512

    def body(dgl_ref, wg_ref, dx2_ref, x1_ref, g_ref, dx1_ref, dx1b_ref, dg_ref):
        @pl.when(pl.program_id(0) == 0)
        def _():
            dg_ref[...] = jnp.zeros_like(dg_ref)

        dh2 = _dot(dgl_ref[...], wg_ref[...], NT)
        x1 = x1_ref[...]
        r = lax.rsqrt(jnp.mean(x1 * x1, axis=-1, keepdims=True) + EPS)
        xh = x1 * r
        dg_ref[...] += jnp.sum(dh2 * xh, axis=0, keepdims=True)
        dg = dh2 * g_ref[...]
        dx1 = dx2_ref[...] + r * (dg - xh * jnp.mean(dg * xh, axis=-1, keepdims=True))
        dx1_ref[...] = dx1
        dx1b_ref[...] = dx1.astype(dx1b_ref.dtype)

    rowblk = pl.BlockSpec((tm, d), lambda i: (i, 0))
    full = lambda a, b: pl.BlockSpec((a, b), lambda i: (0, 0))
    return _pallas(
        body, name="ple_bwd", grid=(t // tm,),
        in_specs=[rowblk, full(d, d), rowblk, rowblk, full(1, d)],
        out_specs=[rowblk, rowblk, full(1, d)],
        out_shape=[jax.ShapeDtypeStruct((t, d), F32), jax.ShapeDtypeStruct((t, d), BF16),
                   jax.ShapeDtypeStruct((1, d), F32)],
        compiler_params=_cparams(("arbitrary",), 56),
    )(dgl, wg, dx2, x1, ple_g)


def _merge_bwd(dx1b, wout, proj, pa, pb, n_main, off_ga, off_gb):
    t, d = pa.shape
    tm = 512

    def body(dx_ref, w_ref, ga_ref, gb_ref, pa_ref, pb_ref, dpa_ref, dpb_ref, dgate_ref):
        dm = _dot(dx_ref[...], w_ref[...], NT)
        sa = _sigmoid(ga_ref[...].astype(F32))
        sb = _sigmoid(gb_ref[...].astype(F32))
        dpa_ref[...] = (dm * sa).astype(dpa_ref.dtype)
        dpb_ref[...] = (dm * sb).astype(dpb_ref.dtype)
        dgate_ref[:, :d] = (dm * pa_ref[...].astype(F32) * sa * (1.0 - sa)).astype(dgate_ref.dtype)
        dgate_ref[:, d:] = (dm * pb_ref[...].astype(F32) * sb * (1.0 - sb)).astype(dgate_ref.dtype)

    assert off_gb == off_ga + d and off_ga % (2 * d) == 0
    rowblk = pl.BlockSpec((tm, d), lambda i: (i, 0))
    return _pallas(
        body, name="merge_bwd", grid=(t // tm,),
        in_specs=[rowblk, pl.BlockSpec((d, d), lambda i: (0, 0)),
                  pl.BlockSpec((tm, d), lambda i: (i, off_ga // d)),
                  pl.BlockSpec((tm, d), lambda i: (i, off_gb // d)), rowblk, rowblk],
        out_specs=[rowblk, rowblk, pl.BlockSpec((tm, 2 * d), lambda i: (i, off_ga // (2 * d)))],
        out_shape=[jax.ShapeDtypeStruct((t, d), BF16), jax.ShapeDtypeStruct((t, d), BF16),
                   jax.ShapeDtypeStruct((t, n_main), BF16)],
        compiler_params=_cparams(("parallel",), 56),
    )(dx1b, wout, proj, proj, pa, pb)


def _yb_bwd(dproj, dpb, wb, proj, mixed, mix_b, scale, off_zp):
    t, dp = mixed.shape
    tm = 512

    def body(_, dpb_ref, w_ref, zp_ref, mx_ref, b_ref, s_ref, dzp_ref, dmx_ref, dsc_ref, db_ref):
        @pl.when(pl.program_id(0) == 0)
        def _():
            dsc_ref[...] = jnp.zeros_like(dsc_ref)
            db_ref[...] = jnp.zeros_like(db_ref)

        dyb = _dot(dpb_ref[...], w_ref[...], NT)
        zp = zp_ref[...].astype(F32)
        sg = _sigmoid(zp)
        mb = mx_ref[...].astype(F32) + b_ref[...]
        yb0 = mb * s_ref[...]
        dyb0 = dyb * (zp * sg)
        dzp_ref[...] = (dyb * yb0 * (sg * (1.0 + zp * (1.0 - sg)))).astype(dzp_ref.dtype)
        dsc_ref[...] += jnp.sum(dyb0 * mb, axis=0, keepdims=True)
        dmx = dyb0 * s_ref[...]
        db_ref[...] += jnp.sum(dmx, axis=0, keepdims=True)
        dmx_ref[...] = dmx.astype(dmx_ref.dtype)

    rowblk = pl.BlockSpec((tm, dp), lambda i: (i, 0))
    vec = pl.BlockSpec((1, dp), lambda i: (0, 0))
    seg = pl.BlockSpec((tm, dp), lambda i: (i, off_zp // dp))
    return _pallas(
        body, name="yb_bwd", grid=(t // tm,),
        in_specs=[ANY, pl.BlockSpec((tm, wb.shape[1]), lambda i: (i, 0)),
                  pl.BlockSpec(wb.shape, lambda i: (0, 0)), seg, rowblk, vec, vec],
        out_specs=[seg, rowblk, vec, vec],
        out_shape=[jax.ShapeDtypeStruct(dproj.shape, dproj.dtype), jax.ShapeDtypeStruct((t, dp), BF16),
                   jax.ShapeDtypeStruct((1, dp), F32), jax.ShapeDtypeStruct((1, dp), F32)],
        input_output_aliases={0: 0},
        compiler_params=_cparams(("arbitrary",), 56),
    )(dproj, dpb, wb, proj, mixed, mix_b, scale)


def _dpooled(dmixed, mixw):
    t, dp = dmixed.shape
    ng, pg = mixw.shape[0], mixw.shape[1]
    tm = min(1024, t)

    def body(a_ref, w_ref, o_ref):
        o_ref[...] = _dot(a_ref[...], w_ref[0], NT).astype(o_ref.dtype)

    blk = pl.BlockSpec((tm, pg), lambda i, g: (i, g))
    return _pallas(
        body, name="dpooled", grid=(t // tm, ng),
        in_specs=[blk, pl.BlockSpec((1, pg, pg), lambda i, g: (g, 0, 0))],
        out_specs=blk, out_shape=jax.ShapeDtypeStruct((t, dp), ACT),
        compiler_params=_cparams(("parallel", "parallel")),
    )(dmixed, mixw)


def _dmixw(pooled, dmixed, ng):
    t, dp = pooled.shape
    pg = dp // ng
    tk = min(2048, t)
    nk = t // tk

    def body(a_ref, b_ref, o_ref):
        @pl.when(pl.program_id(1) == 0)
        def _():
            o_ref[...] = jnp.zeros_like(o_ref)

        o_ref[0] += _dot(a_ref[...], b_ref[...], TN)

    blk = pl.BlockSpec((tk, pg), lambda g, k: (k, g))
    return _pallas(
        body, name="dmixw", grid=(ng, nk),
        in_specs=[blk, blk],
        out_specs=pl.BlockSpec((1, pg, pg), lambda g, k: (g, 0, 0)),
        out_shape=jax.ShapeDtypeStruct((ng, pg, pg), F32),
        compiler_params=_cparams(("parallel", "arbitrary")),
    )(pooled, dmixed)


def _pool_bwd(dproj, dpooled, nb, s, off_u):
    dp = dpooled.shape[1]
    pg = dp // len(POOL_WINDOWS)

    def body(_, d_ref, o_ref):
        g = pl.program_id(1)
        dv = d_ref[...].astype(F32)
        row = _iota(dv.shape, 0)
        pos = (_iota((dv.shape[0], 1), 0) + 1).astype(F32)
        for gi, w in enumerate(POOL_WINDOWS):
            @pl.when(g == gi)
            def _():
                acc, span = dv / jnp.minimum(pos, float(w)), 1
                while span < w:
                    acc = acc + _shift_up(acc, span, row)
                    span *= 2
                o_ref[...] = (acc - dv).astype(o_ref.dtype)

    return _pallas(
        body, name="pool_bwd", grid=(nb, len(POOL_WINDOWS)),
        in_specs=[ANY, pl.BlockSpec((s, pg), lambda b, g: (b, g))],
        out_specs=pl.BlockSpec((s, pg), lambda b, g: (b, off_u // pg + g)),
        out_shape=jax.ShapeDtypeStruct(dproj.shape, dproj.dtype),
        input_output_aliases={0: 0},
        compiler_params=_cparams(("parallel", "parallel"), 48),
    )(dproj, dpooled)


def _gate_bwd(dproj, dpa, wa, xbc, y, proj, dskip_full, gnorm, d_inner):
    t = dpa.shape[0]
    n_heads = d_inner // HEAD_DIM
    gw = d_inner // GROUPS
    tm = 256
    ni = t // tm

    def body(_, dpa_ref, wa_ref, xs_ref, y_ref, z_ref, dsk_ref, gn_ref,
             dz_ref, dy_ref, dgn_ref, dskh_ref, dskacc_ref):
        i = pl.program_id(0)

        @pl.when(i == 0)
        def _():
            dgn_ref[...] = jnp.zeros_like(dgn_ref)
            dskacc_ref[...] = jnp.zeros_like(dskacc_ref)

        dya3 = _dot(dpa_ref[...], wa_ref[...], NT)
        for g in range(GROUPS):
            gsl = slice(g * gw, (g + 1) * gw)
            xs = xs_ref[:, gsl].astype(F32)
            zv = z_ref[:, gsl].astype(F32)
            sg = _sigmoid(zv)
            sz = zv * sg
            ya1 = y_ref[:, gsl].astype(F32) + xs * dsk_ref[:, gsl]
            blk = ya1 * sz
            rg = lax.rsqrt(jnp.mean(blk * blk, axis=-1, keepdims=True) + EPS)
            xh = blk * rg
            d3 = dya3[:, gsl]
            dgn_ref[:, gsl] += jnp.sum(d3 * xh, axis=0, keepdims=True)
            dg = d3 * gn_ref[:, gsl]
            dya2 = rg * (dg - xh * jnp.mean(dg * xh, axis=-1, keepdims=True))
            dy = dya2 * sz
            dy_ref[:, gsl] = dy.astype(dy_ref.dtype)
            dz_ref[:, gsl] = (dya2 * ya1 * (sg * (1.0 + zv * (1.0 - sg)))).astype(dz_ref.dtype)
            dskacc_ref[:, gsl] += jnp.sum(dy * xs, axis=0, keepdims=True)

        @pl.when(i == ni - 1)
        def _():
            ex = (_iota((HEAD_PAD, d_inner), 1) // HEAD_DIM == _iota((HEAD_PAD, d_inner), 0)).astype(BF16)
            dskh_ref[...] = _exact_r(dskacc_ref[...], ex, NT)[0:1, :]

    wide = pl.BlockSpec((tm, d_inner), lambda i: (i, 0))
    vec = lambda n: pl.BlockSpec((1, n), lambda i: (0, 0))
    return _pallas(
        body, name="gate_bwd", grid=(ni,),
        in_specs=[ANY, pl.BlockSpec((tm, dpa.shape[1]), lambda i: (i, 0)),
                  pl.BlockSpec(wa.shape, lambda i: (0, 0), pipeline_mode=pl.Buffered(1)),
                  wide, wide, wide, vec(d_inner), vec(d_inner)],
        out_specs=[wide, wide, vec(d_inner), vec(HEAD_PAD)],
        out_shape=[jax.ShapeDtypeStruct(dproj.shape, dproj.dtype), jax.ShapeDtypeStruct((t, d_inner), BF16),
                   jax.ShapeDtypeStruct((1, d_inner), F32), jax.ShapeDtypeStruct((1, HEAD_PAD), F32)],
        scratch_shapes=[pltpu.VMEM((8, d_inner), F32)],
        input_output_aliases={0: 0},
        compiler_params=_cparams(("arbitrary",), 56),
    )(dproj, dpa, wa, xbc, y, proj, dskip_full, gnorm)


def _ssd_bwd(dy_all, xbc, dtraw, prev, dt_bias, a_log, dskip_full, nb, s, d_inner, partials):
    l = CHUNK
    nc = s // l
    n_heads = d_inner // HEAD_DIM
    gw = d_inner // GROUPS
    pairs_per_group = gw // LANES
    gn_cols = GROUPS * D_STATE
    n_conv = d_inner + 2 * gn_cols
    last_step = nb * nc - 1

    xch = _ChipExchange(partials)
    n_in, n_out = 9, 4

    def body(*refs):
        dy_ref, xs_ref, bm_ref, cm_ref, dtr_ref, prev_ref, bias_ref, alog_ref, dsk_ref = refs[:n_in]
        p_refs = refs[n_in:n_in + xch.n]
        outs = refs[n_in + xch.n:]
        dxbc_ref, ddt_ref, dalog_ref, dbias_ref = outs[:n_out]
        got_refs = outs[n_out:n_out + xch.n]
        g_ref = outs[n_out + xch.n]
        sems = outs[n_out + xch.n + 1:]
        step = pl.program_id(0) * nc + pl.program_id(1)

        @pl.when(pl.program_id(1) == 0)
        def _():
            g_ref[...] = jnp.zeros_like(g_ref)

        @pl.when(step == 0)
        def _():
            xch.start(p_refs, got_refs, sems)
            dalog_ref[...] = jnp.zeros_like(dalog_ref)
            dbias_ref[...] = jnp.zeros_like(dbias_ref)

        xs = xs_ref[...].astype(F32)
        dsk = dsk_ref[...]
        dy = dy_ref[...].astype(F32)

        dtraw_v = dtr_ref[...]
        bias = bias_ref[...]
        dt, apad, acs, dt_full, acs_full, tri, ex = _ssd_common(dtraw_v, bias, alog_ref[...], n_heads)
        acs_t = acs.T
        e_full = jnp.exp(acs_full)
        last = acs_full[l - 1:l, :]
        dte = jnp.exp(last - acs_full)
        cd = jnp.exp(last)
        xdt = xs * dt_full
        xd = xdt * dte
        dy_e = dy * e_full
        mask = _iota((l, l), 1) <= _iota((l, l), 0)
        lane = _iota((l, LANES), 1)
        hlane = _iota((l, HEAD_PAD), 1)
        hsub = _iota((HEAD_PAD, l), 0)
        dacs = jnp.zeros((l, HEAD_PAD), F32)
        dacs_t = jnp.zeros((HEAD_PAD, l), F32)
        dacs_full, dlast, dxdt_all, db_all, dc_all = [], [], [], [], []
        for g in range(GROUPS):
            gsl = slice(g * gw, (g + 1) * gw)
            bg = bm_ref[:, g * D_STATE:(g + 1) * D_STATE].astype(BF16)
            cg = cm_ref[:, g * D_STATE:(g + 1) * D_STATE].astype(BF16)
            prev_g = prev_ref[0, :, gsl]
            prev_b = prev_g.astype(BF16)
            gst = g_ref[:, gsl]
            gst_b = gst.astype(BF16)
            dye_b = dy_e[:, gsl].astype(BF16)
            dc = _dot(dye_b, prev_b, NT)
            dprev = _dot(cg, dye_b, TN)
            y_off = _dot(cg, prev_b) * e_full[:, gsl]
            xd_g = xd[:, gsl]
            db = _dot(xd_g.astype(BF16), gst_b, NT)
            dxd = _dot(bg, gst_b)
            wgt = dxd * xd_g
            dacs_full.append(dy[:, gsl] * y_off - wgt)
            dlast.append(jnp.sum(wgt, axis=0, keepdims=True)
                         + jnp.sum(prev_g * gst, axis=0, keepdims=True) * cd[:, gsl])
            g_ref[:, gsl] = dprev + cd[:, gsl] * gst
            dxdt_g = dxd * dte[:, gsl]
            cb = _dot(cg, bg, NT)
            dcb = jnp.zeros((l, l), F32)
            for q in range(pairs_per_group):
                pair = g * pairs_per_group + q
                psl = slice(pair * LANES, (pair + 1) * LANES)
                xpb = xdt[:, psl].astype(BF16)
                dyp = dy[:, psl].astype(BF16)
                zero = jnp.zeros_like(dyp)
                halves = (jnp.where(lane < HEAD_DIM, dyp, zero), jnp.where(lane >= HEAD_DIM, dyp, zero))
                acc = dxdt_g[:, q * LANES:(q + 1) * LANES]
                for k, dyh in enumerate(halves):
                    h = 2 * pair + k
                    dk = _decay_mats(acs, acs_t, h, mask)
                    mm = cb * dk
                    dm = _dot(dyh, xpb, NT)
                    dcb = dcb + dm * dk
                    dseg = dm * mm
                    dacs = dacs + jnp.where(hlane == h, jnp.sum(dseg, axis=1, keepdims=True), 0.0)
                    dacs_t = dacs_t + jnp.where(hsub == h, jnp.sum(dseg, axis=0, keepdims=True), 0.0)
                    acc = acc + _dot(mm.astype(BF16), dyh, TN)
                dxdt_all.append(acc)
            dcb_b = dcb.astype(BF16)
            dc_all.append(dc + _dot(dcb_b, bg))
            db_all.append(db + _dot(dcb_b, cg, TN))

        dxdt = jnp.concatenate(dxdt_all, axis=1)
        dxbc_ref[:, :d_inner] = (dy * dsk + dxdt * dt_full).astype(dxbc_ref.dtype)
        dxbc_ref[:, d_inner:d_inner + gn_cols] = jnp.concatenate(db_all, axis=1).astype(dxbc_ref.dtype)
        dxbc_ref[:, d_inner + gn_cols:] = jnp.concatenate(dc_all, axis=1).astype(dxbc_ref.dtype)
        ddt = _exact_r(dxdt * xs, ex, NT)
        rowi = _iota((l, d_inner), 0)
        dacs_f = jnp.concatenate(dacs_full, axis=1) + jnp.where(rowi == l - 1, jnp.concatenate(dlast, axis=1), 0.0)
        dacs = dacs + _exact_r(dacs_f, ex, NT) - dacs_t.T
        da = _exact_l(tri, dacs, TN)
        ddt = ddt + da * apad
        dalog_ref[...] += jnp.sum(da * dt, axis=0, keepdims=True) * apad
        ddt_raw = ddt * _sigmoid(dtraw_v + bias)
        dbias_ref[...] += jnp.sum(ddt_raw, axis=0, keepdims=True)
        ddt_ref[...] = ddt_raw.astype(ddt_ref.dtype)

        @pl.when(step == last_step)
        def _():
            xch.wait(p_refs, got_refs, sems)

    row = lambda b, c: b * nc + (nc - 1 - c)
    small = lambda n: pl.BlockSpec((1, n), lambda b, c: (0, 0))
    wide = pl.BlockSpec((l, d_inner), lambda b, c: (row(b, c), 0))
    return _pallas(
        body, name="ssd_bwd", grid=(nb, nc),
        in_specs=[wide, wide,
                  pl.BlockSpec((l, gn_cols), lambda b, c: (row(b, c), d_inner // gn_cols)),
                  pl.BlockSpec((l, gn_cols), lambda b, c: (row(b, c), d_inner // gn_cols + 1)),
                  pl.BlockSpec((l, HEAD_PAD), lambda b, c: (row(b, c), 0)),
                  pl.BlockSpec((1, D_STATE, d_inner), lambda b, c: (row(b, c), 0, 0)),
                  small(HEAD_PAD), small(HEAD_PAD), small(d_inner)] + xch.in_specs,
        out_specs=[pl.BlockSpec((l, n_conv), lambda b, c: (row(b, c), 0)),
                   pl.BlockSpec((l, HEAD_PAD), lambda b, c: (row(b, c), 0)),
                   small(HEAD_PAD), small(HEAD_PAD)] + xch.out_specs,
        out_shape=[jax.ShapeDtypeStruct((nb * s, n_conv), ACT),
                   jax.ShapeDtypeStruct((nb * s, HEAD_PAD), BF16),
                   jax.ShapeDtypeStruct((1, HEAD_PAD), F32), jax.ShapeDtypeStruct((1, HEAD_PAD), F32)]
        + xch.out_shape,
        scratch_shapes=[pltpu.VMEM((D_STATE, d_inner), F32)] + xch.scratch,
        compiler_params=_cparams(("arbitrary", "arbitrary"), 56),
    )(dy_all, xbc, xbc, xbc, dtraw, prev, dt_bias, a_log, dskip_full, *partials)


def _conv_bwd(dproj, dxbc, conv_pre, proj, conv_w, nb, s, off_xbc):
    n_conv = dxbc.shape[1]
    cw = 256

    rc = CONV_ROWS

    def fold(v):
        return jnp.sum(v.reshape(rc // SUBLANES, SUBLANES, cw), axis=0)

    def body(_, d_ref, pre_ref, x_ref, w_ref, dx_ref, dw_ref, db_ref, dcpad_ref):
        @pl.when(pl.program_id(1) == 0)
        def _():
            dw_ref[...] = jnp.zeros_like(dw_ref)
            db_ref[...] = jnp.zeros_like(db_ref)

        dcpad_ref[pl.ds(s, SUBLANES), :] = jnp.zeros((SUBLANES, cw), F32)
        taps = [w_ref[k:k + 1, :] for k in range(CONV_W)]
        db = jnp.zeros((SUBLANES, cw), F32)
        for r in range(s // rc):
            rows = pl.ds(r * rc, rc)
            acc = pre_ref[rows, :].astype(F32)
            sg = _sigmoid(acc)
            dc = d_ref[rows, :].astype(F32) * (sg * (1.0 + acc * (1.0 - sg)))
            dcpad_ref[rows, :] = dc
            db = db + fold(dc)
        db_ref[...] += jnp.sum(db, axis=0, keepdims=True)
        dw = [jnp.zeros((SUBLANES, cw), F32) for _ in range(CONV_W)]
        for r in range(s // rc):
            xv = x_ref[pl.ds(r * rc, rc), :].astype(F32)
            win = dcpad_ref[pl.ds(r * rc, rc + SUBLANES), :]
            dxv = jnp.zeros((rc, cw), F32)
            for j in range(CONV_W):
                up = win[:rc, :] if j == 0 else pltpu.roll(win, rc + SUBLANES - j, 0)[:rc, :]
                dxv = dxv + up * taps[CONV_W - 1 - j]
                dw[CONV_W - 1 - j] = dw[CONV_W - 1 - j] + fold(up * xv)
            dx_ref[pl.ds(r * rc, rc), :] = dxv.astype(dx_ref.dtype)
        for k in range(CONV_W):
            dw_ref[k:k + 1, :] += jnp.sum(dw[k], axis=0, keepdims=True)

    blk = pl.BlockSpec((s, cw), lambda j, b: (b, j))
    return _pallas(
        body, name="conv_bwd", grid=(n_conv // cw, nb),
        in_specs=[ANY, blk, blk, pl.BlockSpec((s, cw), lambda j, b: (b, off_xbc // cw + j)),
                  pl.BlockSpec((CONV_W, cw), lambda j, b: (0, j))],
        out_specs=[pl.BlockSpec((s, cw), lambda j, b: (b, off_xbc // cw + j)),
                   pl.BlockSpec((CONV_W, cw), lambda j, b: (0, j)), pl.BlockSpec((1, cw), lambda j, b: (0, j))],
        out_shape=[jax.ShapeDtypeStruct(dproj.shape, dproj.dtype),
                   jax.ShapeDtypeStruct((CONV_W, n_conv), F32), jax.ShapeDtypeStruct((1, n_conv), F32)],
        scratch_shapes=[pltpu.VMEM((s + SUBLANES, cw), F32)],
        input_output_aliases={0: 0},
        compiler_params=_cparams(("parallel", "arbitrary"), 48),
    )(dproj, dxbc, conv_pre, proj, conv_w)


def _in_bwd(dproj, w_main, ddt, w_dt, x, norm_g, dx1, partials):
    t, d = x.shape
    n_main = dproj.shape[1]
    tm = 256
    ni = t // tm
    xch = _ChipExchange(partials)
    n_in, n_out = 7, 2

    def body(*refs):
        dp_ref, w_ref, ddt_ref, wdt_ref, x_ref, g_ref, dx1_ref = refs[:n_in]
        p_refs = refs[n_in:n_in + xch.n]
        outs = refs[n_in + xch.n:]
        dx_ref, dg_ref = outs[:n_out]
        got_refs = outs[n_out:n_out + xch.n]
        sems = outs[n_out + xch.n:]
        i = pl.program_id(0)

        @pl.when(i == 0)
        def _():
            xch.start(p_refs, got_refs, sems)
            dg_ref[...] = jnp.zeros_like(dg_ref)

        dh = _dot(dp_ref[...], w_ref[...], NT) + _dot(ddt_ref[...], wdt_ref[...], NT)
        xv = x_ref[...]
        r = lax.rsqrt(jnp.mean(xv * xv, axis=-1, keepdims=True) + EPS)
        xh = xv * r
        dg_ref[...] += jnp.sum(dh * xh, axis=0, keepdims=True)
        dg = dh * g_ref[...]
        dx_ref[...] = dx1_ref[...] + r * (dg - xh * jnp.mean(dg * xh, axis=-1, keepdims=True))

        @pl.when(i == ni - 1)
        def _():
            xch.wait(p_refs, got_refs, sems)

    rowblk = pl.BlockSpec((tm, d), lambda i: (i, 0))
    vec = pl.BlockSpec((1, d), lambda i: (0, 0))
    resident = lambda shape: pl.BlockSpec(shape, lambda i: (0, 0), pipeline_mode=pl.Buffered(1))
    return _pallas(
        body, name="in_bwd", grid=(ni,),
        in_specs=[pl.BlockSpec((tm, n_main), lambda i: (i, 0)), resident((d, n_main)),
                  pl.BlockSpec((tm, HEAD_PAD), lambda i: (i, 0)), resident((d, HEAD_PAD)),
                  rowblk, vec, rowblk] + xch.in_specs,
        out_specs=[rowblk, vec] + xch.out_specs,
        out_shape=[jax.ShapeDtypeStruct((t, d), F32), jax.ShapeDtypeStruct((1, d), F32)] + xch.out_shape,
        scratch_shapes=xch.scratch,
        compiler_params=_cparams(("arbitrary",), 56),
    )(dproj, w_main, ddt, w_dt, x, norm_g, dx1, *partials)


def _adamw_math(w, g, m, v):
    m = ADAM_B1 * m + (1.0 - ADAM_B1) * g
    v = ADAM_B2 * v + (1.0 - ADAM_B2) * (g * g)
    m_hat = m / (1.0 - ADAM_B1 ** ADAM_STEP)
    v_hat = v / (1.0 - ADAM_B2 ** ADAM_STEP)
    delta = -ADAM_LR * (m_hat / (jnp.sqrt(v_hat) + ADAM_EPS) + ADAM_WD * w)
    return delta, m, v


def _adamw(parts, w, m, v, name):
    n, rows, cols = parts.shape
    tr = _row_tile(rows, 128) if rows % 8 == 0 else rows

    def body(p_ref, w_ref, m_ref, v_ref, g_ref, d_ref, mo_ref, vo_ref):
        g = p_ref[0].astype(F32)
        for k in range(1, n):
            g = g + p_ref[k].astype(F32)
        d, mn, vn = _adamw_math(w_ref[...], g, m_ref[...], v_ref[...])
        g_ref[...] = g
        d_ref[...] = d
        mo_ref[...] = mn
        vo_ref[...] = vn

    blk = pl.BlockSpec((tr, cols), lambda i: (i, 0))
    shp = jax.ShapeDtypeStruct((rows, cols), F32)
    return _pallas(
        body, name=name, grid=(rows // tr,),
        in_specs=[pl.BlockSpec((n, tr, cols), lambda i: (0, i, 0)), blk, blk, blk],
        out_specs=[blk, blk, blk, blk], out_shape=[shp, shp, shp, shp],
        compiler_params=_cparams(("parallel",), 56),
    )(parts, w, m, v)


def _coords():
    return lax.axis_index("x"), lax.axis_index("y"), lax.axis_index("c")


class _TwoLevelGather:
    def __init__(self, blocks):
        self.n = len(blocks)
        self.in_specs = [ANY] * self.n
        self.out_specs = [ANY] * self.n
        self.out_shape = [jax.ShapeDtypeStruct((N_DEV,) + b.shape, b.dtype) for b in blocks]
        self.scratch = [pltpu.SemaphoreType.DMA((self.n, 7)), pltpu.SemaphoreType.DMA((self.n, 7)),
                        pltpu.SemaphoreType.DMA((self.n,))]

    def _parts(self, x_refs, out_refs, sems):
        send_sems, recv_sems, local_sems = sems
        x, y, c = _coords()
        me, sibling = (x, y, c), (x, y, 1 - c)
        chips = [(1 - x, y), (x, 1 - y), (1 - x, 1 - y)]

        def slot(a, px, py, pc):
            return out_refs[a].at[4 * px + 2 * py + pc]

        def copy(a, k, block, to, src=None):
            return pltpu.make_async_remote_copy(
                src_ref=slot(a, *block) if src is None else src, dst_ref=slot(a, *block),
                send_sem=send_sems.at[a, k], recv_sem=recv_sems.at[a, k], device_id=to, device_id_type=MESH)

        local = [pltpu.make_async_copy(x_refs[a], slot(a, *me), local_sems.at[a]) for a in range(self.n)]
        first = []
        for a in range(self.n):
            first.append(copy(a, 0, me, sibling, src=x_refs[a]))
            first += [copy(a, 1 + j, me, (*chip, c), src=x_refs[a]) for j, chip in enumerate(chips)]
        return copy, local, first, chips, me, sibling, c

    def start(self, x_refs, out_refs, sems):
        _, local, first, *_ = self._parts(x_refs, out_refs, sems)
        for cp in local + first:
            cp.start()

    def finish(self, x_refs, out_refs, sems):
        copy, local, first, chips, me, sibling, c = self._parts(x_refs, out_refs, sems)
        passed = []
        for j, chip in enumerate(chips):
            for a in range(self.n):
                copy(a, 1 + j, (*chip, c), me).wait_recv()
                passed.append(copy(a, 4 + j, (*chip, c), sibling))
                passed[-1].start()
        for a in range(self.n):
            copy(a, 0, sibling, me).wait_recv()
            for j, chip in enumerate(chips):
                copy(a, 4 + j, (*chip, 1 - c), me).wait_recv()
        for cp in first + passed:
            cp.wait_send()
        for cp in local:
            cp.wait()


def _all_gather_routed(blk, name):
    rows, cols = blk.shape
    half = rows // 2
    assert half % 16 == 0

    def body(x_ref, out_ref, send_sems, recv_sems, local_sem):
        x, y, c = _coords()
        me, sib, xn, yn, dg = (x, y, c), (x, y, 1 - c), (1 - x, y, c), (x, 1 - y, c), (1 - x, 1 - y, c)
        other = lambda p: (p[0], p[1], 1 - c)
        top, bot = pl.ds(0, half), pl.ds(half, half)

        def slot(p, part=None):
            ref = out_ref.at[4 * p[0] + 2 * p[1] + p[2]]
            return ref if part is None else ref.at[part]

        def copy(k, block, to, part=None, src=None):
            return pltpu.make_async_remote_copy(
                src_ref=slot(block, part) if src is None else src, dst_ref=slot(block, part),
                send_sem=send_sems.at[k], recv_sem=recv_sems.at[k], device_id=to, device_id_type=MESH)

        mine = pltpu.make_async_copy(x_ref, slot(me), local_sem)
        own = [copy(1, me, xn, src=x_ref), copy(2, me, yn, src=x_ref), copy(0, me, sib, src=x_ref)]
        for cp in [mine] + own:
            cp.start()
        copy(1, xn, me).wait_recv()
        fwd = [copy(3, xn, yn, top), copy(5, xn, sib)]
        for cp in fwd:
            cp.start()
        copy(2, yn, me).wait_recv()
        fwd += [copy(4, yn, xn, bot), copy(6, yn, sib)]
        for cp in fwd[2:]:
            cp.start()
        copy(3, dg, me, top).wait_recv()
        fwd.append(copy(7, dg, sib, top))
        fwd[-1].start()
        copy(4, dg, me, bot).wait_recv()
        fwd.append(copy(8, dg, sib, bot))
        fwd[-1].start()
        copy(0, sib, me).wait_recv()
        copy(5, other(xn), me).wait_recv()
        copy(6, other(yn), me).wait_recv()
        copy(7, other(dg), me, top).wait_recv()
        copy(8, other(dg), me, bot).wait_recv()
        for cp in own + fwd:
            cp.wait_send()
        mine.wait()

    return _pallas(
        body, name=name, in_specs=[ANY], out_specs=ANY,
        out_shape=jax.ShapeDtypeStruct((N_DEV, rows, cols), blk.dtype),
        scratch_shapes=[pltpu.SemaphoreType.DMA((9,)), pltpu.SemaphoreType.DMA((9,)), pltpu.SemaphoreType.DMA],
    )(blk)


def _all_gather_direct(blk, name):
    rows, cols = blk.shape

    def body(x_ref, out_ref, send_sems, recv_sems, local_sem):
        x, y, c = _coords()
        me = 4 * x + 2 * y + c
        mine = pltpu.make_async_copy(x_ref, out_ref.at[me], local_sem)
        mine.start()
        copies = []
        for k in range(1, N_DEV):
            fx, fy, fc = (k >> 2) & 1, (k >> 1) & 1, k & 1
            peer = (x ^ fx, y ^ fy, c ^ fc)
            copies.append(pltpu.make_async_remote_copy(
                src_ref=x_ref, dst_ref=out_ref.at[me], send_sem=send_sems.at[k - 1], recv_sem=recv_sems.at[k - 1],
                device_id=peer, device_id_type=MESH))
        for cp in copies:
            cp.start()
        for k in range(1, N_DEV):
            fx, fy, fc = (k >> 2) & 1, (k >> 1) & 1, k & 1
            src = 4 * (x ^ fx) + 2 * (y ^ fy) + (c ^ fc)
            pltpu.make_async_remote_copy(
                src_ref=x_ref, dst_ref=out_ref.at[src], send_sem=send_sems.at[k - 1], recv_sem=recv_sems.at[k - 1],
                device_id=(x, y, c), device_id_type=MESH).wait_recv()
        for cp in copies:
            cp.wait_send()
        mine.wait()

    return _pallas(
        body, name=name, in_specs=[ANY], out_specs=ANY,
        out_shape=jax.ShapeDtypeStruct((N_DEV, rows, cols), blk.dtype),
        scratch_shapes=[pltpu.SemaphoreType.DMA((7,)), pltpu.SemaphoreType.DMA((7,)), pltpu.SemaphoreType.DMA],
    )(blk)


def _pair_exchange(grads, name):
    n = len(grads)

    def body(*refs):
        g_refs, out_refs = refs[:n], refs[n:2 * n]
        send_sems, recv_sems = refs[2 * n:]
        x, y, c = _coords()
        copies = []
        for a in range(n):
            for k in range(4):
                copies.append(pltpu.make_async_remote_copy(
                    src_ref=g_refs[a].at[2 * k + (1 - c)], dst_ref=out_refs[a].at[k],
                    send_sem=send_sems.at[a, k], recv_sem=recv_sems.at[a, k],
                    device_id=(x, y, 1 - c), device_id_type=MESH))
        for cp in copies:
            cp.start()
        for cp in copies:
            cp.wait()

    return _pallas(
        body, name=name, in_specs=[ANY] * n, out_specs=[ANY] * n,
        out_shape=[jax.ShapeDtypeStruct((4,) + g.shape[1:], g.dtype) for g in grads],
        scratch_shapes=[pltpu.SemaphoreType.DMA((n, 4)), pltpu.SemaphoreType.DMA((n, 4))],
    )(*grads)


def _pair_add(grads, got, core, out_dtype, name):
    _, rows, cols = grads.shape
    tr = _row_tile(rows, 256) if rows % 8 == 0 else rows

    def body(c_ref, a_ref, b_ref, o_ref):
        o_ref[...] = (a_ref[...].astype(F32) + b_ref[...].astype(F32)).astype(o_ref.dtype)

    grid_spec = pltpu.PrefetchScalarGridSpec(
        num_scalar_prefetch=1, grid=(4, rows // tr),
        in_specs=[pl.BlockSpec((1, tr, cols), lambda k, i, c_ref: (2 * k + c_ref[0], i, 0)),
                  pl.BlockSpec((1, tr, cols), lambda k, i, c_ref: (k, i, 0))],
        out_specs=pl.BlockSpec((1, tr, cols), lambda k, i, c_ref: (k, i, 0)))
    return _pallas(
        body, name=name, grid_spec=grid_spec,
        out_shape=jax.ShapeDtypeStruct((4, rows, cols), out_dtype),
        compiler_params=_cparams(("parallel", "parallel"), 48),
    )(core, grads, got)


class _ChipExchange:
    def __init__(self, partials):
        self.n = len(partials)
        self.in_specs = [ANY] * self.n
        self.out_specs = [ANY] * self.n
        self.out_shape = [jax.ShapeDtypeStruct(p.shape, p.dtype) for p in partials]
        self.scratch = [pltpu.SemaphoreType.DMA((self.n, 3)), pltpu.SemaphoreType.DMA((self.n, 3)),
                        pltpu.SemaphoreType.DMA((self.n,))]

    def _copies(self, p_refs, out_refs, sems, receiving):
        send_sems, recv_sems, local_sems = sems
        x, y, c = _coords()
        mychip = 2 * x + y
        local = [pltpu.make_async_copy(p_refs[a].at[mychip], out_refs[a].at[mychip], local_sems.at[a])
                 for a in range(self.n)]
        remote = []
        for a in range(self.n):
            for k in range(1, 4):
                px, py = x ^ ((k >> 1) & 1), y ^ (k & 1)
                peer = 2 * px + py
                remote.append(pltpu.make_async_remote_copy(
                    src_ref=p_refs[a].at[peer], dst_ref=out_refs[a].at[peer if receiving else mychip],
                    send_sem=send_sems.at[a, k - 1], recv_sem=recv_sems.at[a, k - 1],
                    device_id=(x, y, c) if receiving else (px, py, c), device_id_type=MESH))
        return local, remote

    def start(self, p_refs, out_refs, sems):
        local, sends = self._copies(p_refs, out_refs, sems, False)
        for cp in local + sends:
            cp.start()

    def wait(self, p_refs, out_refs, sems):
        local, recvs = self._copies(p_refs, out_refs, sems, True)
        for cp in recvs:
            cp.wait_recv()
        for cp in recvs:
            cp.wait_send()
        for cp in local:
            cp.wait()


def _pad_lanes(v, n=HEAD_PAD):
    return jnp.pad(v, ((0, 0), (0, n - v.shape[1])))


def kernel(x, p, norm_g, w_in, conv_w, conv_b, dt_bias, a_log, d_skip, gnorm_g, pool_mix_w, pool_mix_b, pool_scale, w_branch_a, w_branch_b, w_out, ple_norm_g, w_ple_gate, w_ple_up, final_g, loss_target, m_norm_g, m_w_in, m_conv_w, m_conv_b, m_dt_bias, m_a_log, m_d_skip, m_gnorm_g, m_pool_mix_w, m_pool_mix_b, m_pool_scale, m_w_branch_a, m_w_branch_b, m_w_out, m_ple_norm_g, m_w_ple_gate, m_w_ple_up, m_final_g, v_norm_g, v_w_in, v_conv_w, v_conv_b, v_dt_bias, v_a_log, v_d_skip, v_gnorm_g, v_pool_mix_w, v_pool_mix_b, v_pool_scale, v_w_branch_a, v_w_branch_b, v_w_out, v_ple_norm_g, v_w_ple_gate, v_w_ple_up, v_final_g):
    nb, s, d = x.shape
    t = nb * s
    n_heads = dt_bias.shape[1]
    d_inner = n_heads * HEAD_DIM
    n_conv = conv_b.shape[1]
    d_pool = pool_mix_b.shape[1]
    n_in = w_in.shape[2] * N_DEV
    ple_dim = p.shape[-1]
    o_xbc, o_dt = d_inner, d_inner + n_conv
    o_u = o_dt + n_heads
    o_zp = o_u + d_pool
    o_ga, o_gb = o_zp + d_pool, o_zp + d_pool + d
    n_main = n_in - n_heads
    off_ga, off_gb, off_xbc = d_inner, d_inner + d, d_inner + 2 * d
    off_u = off_xbc + n_conv
    off_zp = off_u + d_pool

    sharded = [w_in[0], conv_w[0], pool_mix_w[0], w_branch_a[0], w_branch_b[0], w_out[0], w_ple_gate[0], w_ple_up[0]]
    sharded_m = [m_w_in[0], m_conv_w[0], m_pool_mix_w[0], m_w_branch_a[0], m_w_branch_b[0], m_w_out[0],
                 m_w_ple_gate[0], m_w_ple_up[0]]
    sharded_v = [v_w_in[0], v_conv_w[0], v_pool_mix_w[0], v_w_branch_a[0], v_w_branch_b[0], v_w_out[0],
                 v_w_ple_gate[0], v_w_ple_up[0]]
    ng = pool_mix_w.shape[1]
    pg = d_pool // ng
    two_d = lambda w: w.reshape(-1, w.shape[-1])
    sharded, sharded_m, sharded_v = ([two_d(w) for w in ws] for ws in (sharded, sharded_m, sharded_v))

    send = [w if i == 1 else w.astype(BF16) for i, w in enumerate(sharded)]
    seg = [_all_gather_routed(send[0], "w_in_all_gather")]
    chunk = n_in // N_DEV
    segments = [((0, o_xbc), 0), ((o_ga, o_gb), off_ga), ((o_gb, n_in), off_gb), ((o_xbc, o_dt), off_xbc),
                ((o_u, o_zp), off_u), ((o_zp, o_ga), off_zp)]

    def gathered_cols(lo, hi):
        out = []
        while lo < hi:
            dev = lo // chunk
            a, b = lo - dev * chunk, min(hi - dev * chunk, chunk)
            out.append(seg[0][dev, :, a:b])
            lo = dev * chunk + b
        return out

    w_main = jnp.concatenate([pc for (lo, hi), _ in segments for pc in gathered_cols(lo, hi)], axis=1)
    w_dt = _pad_lanes(jnp.concatenate(gathered_cols(o_dt, o_u), axis=1))

    x2 = x.reshape(t, d)
    h = _rms_fwd(x2, norm_g, "rms_in")
    proj, *rest = _in_proj(h, w_main, send[1:])
    seg += rest
    cw_full = seg[1].transpose(1, 0, 2).reshape(CONV_W, n_conv)
    mixw_full = seg[2].reshape(N_DEV, ng, pg // N_DEV, pg).transpose(1, 0, 2, 3).reshape(ng, pg, pg)
    wa_full = seg[3].reshape(d_inner, d)
    wb_full = seg[4].reshape(d_pool, d)
    wout_full = seg[5].reshape(d, d)
    wg_full = seg[6].reshape(d, d)
    wup_full = seg[7].transpose(1, 0, 2).reshape(ple_dim, d)

    p2 = p.reshape(t, ple_dim)
    tgt = loss_target.reshape(t, d)
    bias_p, alog_p = _pad_lanes(dt_bias), _pad_lanes(a_log)
    dskip_full = jnp.repeat(d_skip, HEAD_DIM, axis=1)

    dtraw = _matmul(h, w_dt, "nn", F32, "in_proj_dt", 1024, HEAD_PAD, d)
    xbc, conv_pre = _conv_fwd(proj, cw_full, conv_b, nb, s, off_xbc, n_conv)
    y, ya3, prev = _ssd_fwd(xbc, proj, dtraw, bias_p, alog_p, dskip_full, gnorm_g, nb, s, d_inner)
    pooled = _pool_fwd(proj, nb, s, off_u, d_pool)
    mixed, yb = _mix_fwd(pooled, mixw_full, proj, pool_mix_b, pool_scale, off_zp)
    pa, pb, merged = _merge_fwd(ya3, wa_full, yb, wb_full, proj, off_ga, off_gb)
    x1, h2 = _out_fwd(merged, wout_full, x2, ple_norm_g)
    dx2, dgl, dpu, loss_part, dfinal_g = _final(h2, wg_full, p2, wup_full, x1, tgt, final_g.reshape(1, d))

    g_wup = _matmul(p2, dpu, "tn", BF16, "dw_ple_up", 256, 512, t)
    g_wg = _matmul(h2, dgl, "tn", BF16, "dw_ple_gate", 512, 512, t)
    dx1, dx1b, d_ple_g = _ple_bwd(dgl, wg_full, dx2, x1, ple_norm_g)
    g_wout = _matmul(merged, dx1b, "tn", BF16, "dw_out", 512, 512, t)
    dpa, dpb, dproj = _merge_bwd(dx1b, wout_full, proj, pa, pb, n_main, off_ga, off_gb)
    g_wa = _matmul(ya3, dpa, "tn", BF16, "dw_branch_a", 512, 512, t)
    g_wb = _matmul(yb, dpb, "tn", BF16, "dw_branch_b", 512, 512, t)
    dproj, dmixed, d_scale, d_mixb = _yb_bwd(dproj, dpb, wb_full, proj, mixed, pool_mix_b, pool_scale, off_zp)
    g_mixw = _dmixw(pooled, dmixed, ng)
    dpool = _dpooled(dmixed, mixw_full)
    dproj = _pool_bwd(dproj, dpool, nb, s, off_u)

    core = lax.axis_index("c").astype(I32).reshape(1)

    def pair_sums(by_dest, names):
        by_dest = [g.astype(BF16) for g in by_dest]
        got = _pair_exchange(by_dest, "rs_pair_exchange_" + names[0])
        return [_pair_add(g, r, core, BF16, "rs_pair_add_" + nm) for g, r, nm in zip(by_dest, got, names)]

    names_a = ["pool_mix_w", "w_branch_a", "w_branch_b", "w_out", "w_ple_gate", "w_ple_up"]
    partials_a = pair_sums([
        g_mixw.reshape(ng, N_DEV, pg // N_DEV, pg).transpose(1, 0, 2, 3).reshape(N_DEV, ng * pg // N_DEV, pg),
        g_wa.reshape(N_DEV, d_inner // N_DEV, d), g_wb.reshape(N_DEV, d_pool // N_DEV, d),
        g_wout.reshape(N_DEV, d // N_DEV, d), g_wg.reshape(N_DEV, d // N_DEV, d),
        g_wup.reshape(ple_dim, N_DEV, d // N_DEV).transpose(1, 0, 2)], names_a)
    dproj, dy_ssd, d_gn, d_dsk = _gate_bwd(dproj, dpa, wa_full, xbc, y, proj, dskip_full, gnorm_g, d_inner)
    dxbc, ddt, d_alog, d_dtb, *parts_a = _ssd_bwd(
        dy_ssd, xbc, dtraw, prev, bias_p, alog_p, dskip_full, nb, s, d_inner, partials_a)
    dproj, g_cw, d_cb = _conv_bwd(dproj, dxbc, conv_pre, proj, cw_full, nb, s, off_xbc)
    g_wmain = _matmul(h, dproj, "tn", BF16, "dw_in", 512, 1024, t)
    g_wdt = _matmul(h, ddt, "tn", BF16, "dw_in_dt", 512, HEAD_PAD, t)

    def dest_cols(dev):
        lo, hi = dev * chunk, (dev + 1) * chunk
        srcs = [((olo, ohi), g_wmain, off) for (olo, ohi), off in segments] + [((o_dt, o_u), g_wdt, 0)]
        out = []
        for (olo, ohi), arr, off in sorted(srcs, key=lambda e: e[0][0]):
            a, b = max(lo, olo), min(hi, ohi)
            if a < b:
                out.append(arr[:, off + a - olo:off + b - olo])
        return jnp.concatenate(out, axis=1)

    names_b = ["w_in", "conv_w"]
    partials_b = pair_sums([jnp.stack([dest_cols(dev) for dev in range(N_DEV)], axis=0),
                            g_cw.reshape(CONV_W, N_DEV, n_conv // N_DEV).transpose(1, 0, 2)], names_b)
    grad_x, d_ng, *parts_b = _in_bwd(dproj, w_main, ddt, w_dt, x2, norm_g, dx1, partials_b)

    names = names_b + names_a
    parts = parts_b + parts_a
    sh_g, sh_d, sh_m, sh_v = [], [], [], []
    shapes = [w_in.shape, conv_w.shape, pool_mix_w.shape, w_branch_a.shape, w_branch_b.shape, w_out.shape,
              w_ple_gate.shape, w_ple_up.shape]
    for pt, w, m, v, nm, shp in zip(parts, sharded, sharded_m, sharded_v, names, shapes):
        res = _adamw(pt, w, m, v, "adamw_" + nm)
        for dst, r in zip((sh_g, sh_d, sh_m, sh_v), res):
            dst.append(r.reshape(shp))

    rep_w = [norm_g, conv_b, gnorm_g, pool_mix_b, pool_scale, ple_norm_g, final_g.reshape(1, d),
             bias_p, alog_p, _pad_lanes(d_skip)]
    rep_m = [m_norm_g, m_conv_b, m_gnorm_g, m_pool_mix_b, m_pool_scale, m_ple_norm_g, m_final_g.reshape(1, d),
             _pad_lanes(m_dt_bias), _pad_lanes(m_a_log), _pad_lanes(m_d_skip)]
    rep_v = [v_norm_g, v_conv_b, v_gnorm_g, v_pool_mix_b, v_pool_scale, v_ple_norm_g, v_final_g.reshape(1, d),
             _pad_lanes(v_dt_bias), _pad_lanes(v_a_log), _pad_lanes(v_d_skip)]
    rep_g = [d_ng, d_cb, d_gn, d_mixb, d_scale, d_ple_g, dfinal_g, d_dtb, d_alog, d_dsk]
    zero = jnp.zeros((1, LANES), F32)
    small = _all_gather_direct(jnp.concatenate(rep_g + [loss_part], axis=1), "small_all_gather")
    pk = lambda ws: jnp.concatenate(ws + [zero], axis=1)
    rg, rd, rm, rv = _adamw(small, pk(rep_w), pk(rep_m), pk(rep_v), "adamw_replicated")
    loss = rg[0, sum(int(w.size) for w in rep_w)]

    def unpack_rep(buf):
        out, pos_ = [], 0
        for w in rep_w:
            out.append(buf[:, pos_:pos_ + w.size])
            pos_ += w.size
        return out

    def arrange(sh, rep):
        rep = list(rep)
        rep[6] = rep[6].reshape(d)
        for i in (7, 8, 9):
            rep[i] = rep[i][:, :n_heads]
        return [rep[0], sh[0], sh[1], rep[1], rep[7], rep[8], rep[9], rep[2], sh[2], rep[3], rep[4],
                sh[3], sh[4], sh[5], rep[5], sh[6], sh[7], rep[6]]

    outs = [loss, grad_x.reshape(nb, s, d)]
    for sh, rep in ((sh_g, rg), (sh_d, rd), (sh_m, rm), (sh_v, rv)):
        outs += arrange(sh, unpack_rep(rep))
    return tuple(outs)
```

```python
import functools

import jax
import jax.numpy as jnp
from jax import lax
from jax.experimental import pallas as pl
from jax.experimental.pallas import tpu as pltpu

F32 = jnp.float32
BF16 = jnp.bfloat16
I32 = jnp.int32
MESH = pl.DeviceIdType.MESH
ANY = pl.BlockSpec(memory_space=pl.ANY)

EPS = 1e-6
CHUNK = 128
HEAD_DIM = 64
D_STATE = 128
GROUPS = 4
CONV_W = 4
POOL_WINDOWS = (2, 4, 8, 16)
ACT = jnp.bfloat16
LANES = 128
SUBLANES = 8
CONV_ROWS = 64
HEAD_PAD = 128
NEG = -1e30
N_DEV = 8

ADAM_LR = 0.001
ADAM_B1 = 0.9
ADAM_B2 = 0.999
ADAM_EPS = 1e-08
ADAM_WD = 0.01
ADAM_STEP = 10

NN = (((1,), (0,)), ((), ()))
NT = (((1,), (1,)), ((), ()))
TN = (((0,), (0,)), ((), ()))


def _pallas(body, **kw):
    return pl.pallas_call(body, **kw)


def _cparams(dims=None, vmem_mb=None):
    kw = {}
    if dims is not None:
        kw["dimension_semantics"] = dims
    if vmem_mb is not None:
        kw["vmem_limit_bytes"] = vmem_mb << 20
    return pltpu.CompilerParams(**kw)


def _dot(a, b, dims=NN):
    return lax.dot_general(a, b, dims, preferred_element_type=F32)


def _split3(v):
    hi = v.astype(BF16)
    r = v - hi.astype(F32)
    mid = r.astype(BF16)
    lo = (r - mid.astype(F32)).astype(BF16)
    return hi, mid, lo


def _exact_r(v, m, dims=NN):
    hi, mid, lo = _split3(v)
    return _dot(hi, m, dims) + _dot(mid, m, dims) + _dot(lo, m, dims)


def _exact_l(m, v, dims=NN):
    hi, mid, lo = _split3(v)
    return _dot(m, hi, dims) + _dot(m, mid, dims) + _dot(m, lo, dims)


def _row_tile(rows, target):
    best = None
    for cand in range(8, min(rows, target) + 1, 8):
        if rows % cand == 0:
            best = cand
    assert best is not None, rows
    return best


def _sum_all(v):
    return jnp.sum(jnp.sum(v, axis=1, keepdims=True), axis=0, keepdims=True)


def _iota(shape, dim):
    return lax.broadcasted_iota(I32, shape, dim)


def _sigmoid(x):
    return 0.5 * jnp.tanh(0.5 * x) + 0.5


def _softplus(x):
    return jnp.maximum(x, 0.0) + jnp.log(1.0 + jnp.exp(-jnp.abs(x)))


def _shift_down(v, j, row):
    return jnp.where(row >= j, pltpu.roll(v, j, 0), 0.0)


def _shift_up(v, j, row):
    n = v.shape[0]
    return jnp.where(row < n - j, pltpu.roll(v, n - j, 0), 0.0)


def _matmul(a, b, mode, out_dtype, name, tm, tn, tk):
    if mode == "nn":
        (m, k), n = a.shape, b.shape[1]
    elif mode == "nt":
        (m, k), n = a.shape, b.shape[0]
    else:
        (k, m), n = a.shape, b.shape[1]
    tm, tn, tk = min(tm, m), min(tn, n), min(tk, k)
    assert m % tm == 0 and n % tn == 0 and k % tk == 0, (name, m, n, k)
    nk = k // tk
    dims = {"nn": NN, "nt": NT, "tn": TN}[mode]

    def body_acc(a_ref, b_ref, o_ref, acc_ref):
        kk = pl.program_id(2)

        @pl.when(kk == 0)
        def _():
            acc_ref[...] = jnp.zeros_like(acc_ref)

        acc_ref[...] += _dot(a_ref[...].astype(BF16), b_ref[...].astype(BF16), dims)

        @pl.when(kk == nk - 1)
        def _():
            o_ref[...] = acc_ref[...].astype(o_ref.dtype)

    def body_one(a_ref, b_ref, o_ref):
        o_ref[...] = _dot(a_ref[...].astype(BF16), b_ref[...].astype(BF16), dims).astype(o_ref.dtype)

    body = body_one if nk == 1 else body_acc
    if mode == "tn":
        a_spec = pl.BlockSpec((tk, tm), lambda i, j, kk: (kk, i))
    else:
        a_spec = pl.BlockSpec((tm, tk), lambda i, j, kk: (i, kk))
    if mode == "nt":
        b_spec = pl.BlockSpec((tn, tk), lambda i, j, kk: (j, kk))
    else:
        b_spec = pl.BlockSpec((tk, tn), lambda i, j, kk: (kk, j))
    return _pallas(
        body, name=name, grid=(m // tm, n // tn, nk),
        in_specs=[a_spec, b_spec],
        out_specs=pl.BlockSpec((tm, tn), lambda i, j, kk: (i, j)),
        out_shape=jax.ShapeDtypeStruct((m, n), out_dtype),
        scratch_shapes=[] if nk == 1 else [pltpu.VMEM((tm, tn), F32)],
        compiler_params=_cparams(("parallel", "parallel", "arbitrary"), 56),
    )(a, b)


def _in_proj(h, w_main, blocks):
    t, d = h.shape
    n = w_main.shape[1]
    tm, tn = min(1024, t), 1024
    ni, nj = t // tm, n // tn
    gather = _TwoLevelGather(blocks)

    def body(*refs):
        a_ref, b_ref = refs[:2]
        x_refs = refs[2:2 + gather.n]
        o_ref = refs[2 + gather.n]
        got_refs = refs[3 + gather.n:3 + 2 * gather.n]
        sems = refs[3 + 2 * gather.n:]
        i, j = pl.program_id(0), pl.program_id(1)

        @pl.when((i == 0) & (j == 0))
        def _():
            gather.start(x_refs, got_refs, sems)

        o_ref[...] = _dot(a_ref[...], b_ref[...]).astype(o_ref.dtype)

        @pl.when((i == ni - 1) & (j == nj - 1))
        def _():
            gather.finish(x_refs, got_refs, sems)

    return _pallas(
        body, name="in_proj", grid=(ni, nj),
        in_specs=[pl.BlockSpec((tm, d), lambda i, j: (i, 0)), pl.BlockSpec((d, tn), lambda i, j: (0, j))]
        + gather.in_specs,
        out_specs=[pl.BlockSpec((tm, tn), lambda i, j: (i, j))] + gather.out_specs,
        out_shape=[jax.ShapeDtypeStruct((t, n), ACT)] + gather.out_shape,
        scratch_shapes=gather.scratch,
        compiler_params=_cparams(("arbitrary", "arbitrary"), 56),
    )(h, w_main, *blocks)


def _rms_fwd(x, g, name):
    t, d = x.shape
    tt = 512

    def body(x_ref, g_ref, h_ref):
        xv = x_ref[...]
        r = lax.rsqrt(jnp.mean(xv * xv, axis=-1, keepdims=True) + EPS)
        h_ref[...] = (xv * r * g_ref[...]).astype(h_ref.dtype)

    return _pallas(
        body, name=name, grid=(t // tt,),
        in_specs=[pl.BlockSpec((tt, d), lambda i: (i, 0)), pl.BlockSpec((1, d), lambda i: (0, 0))],
        out_specs=pl.BlockSpec((tt, d), lambda i: (i, 0)),
        out_shape=jax.ShapeDtypeStruct((t, d), BF16),
        compiler_params=_cparams(("parallel",)),
    )(x, g)


def _conv_fwd(proj, conv_w, conv_b, nb, s, off_xbc, n_conv):
    cw = 256
    rc = CONV_ROWS

    def body(x_ref, w_ref, b_ref, o_ref, pre_ref, pad_ref):
        pad_ref[0:SUBLANES, :] = jnp.zeros((SUBLANES, cw), F32)
        pad_ref[SUBLANES:, :] = x_ref[...].astype(F32)
        taps = [w_ref[k:k + 1, :] for k in range(CONV_W)]
        bias = b_ref[...]
        for r in range(s // rc):
            win = pad_ref[pl.ds(r * rc, rc + SUBLANES), :]
            acc = bias + win[SUBLANES:, :] * taps[CONV_W - 1]
            for j in range(1, CONV_W):
                acc = acc + pltpu.roll(win, j, 0)[SUBLANES:, :] * taps[CONV_W - 1 - j]
            pre_ref[pl.ds(r * rc, rc), :] = acc.astype(pre_ref.dtype)
            o_ref[pl.ds(r * rc, rc), :] = (acc * _sigmoid(acc)).astype(o_ref.dtype)

    out = pl.BlockSpec((s, cw), lambda b, j: (b, j))
    return _pallas(
        body, name="conv_fwd", grid=(nb, n_conv // cw),
        in_specs=[pl.BlockSpec((s, cw), lambda b, j: (b, off_xbc // cw + j)),
                  pl.BlockSpec((CONV_W, cw), lambda b, j: (0, j)),
                  pl.BlockSpec((1, cw), lambda b, j: (0, j))],
        out_specs=[out, out],
        out_shape=[jax.ShapeDtypeStruct((nb * s, n_conv), ACT), jax.ShapeDtypeStruct((nb * s, n_conv), ACT)],
        scratch_shapes=[pltpu.VMEM((s + SUBLANES, cw), F32)],
        compiler_params=_cparams(("parallel", "parallel"), 48),
    )(proj, conv_w, conv_b)


def _ssd_common(dtraw, bias, alog, n_heads):
    l = dtraw.shape[0]
    lane = _iota((l, HEAD_PAD), 1)
    lane1 = _iota((1, HEAD_PAD), 1)
    apad = jnp.where(lane1 < n_heads, -jnp.exp(alog), 0.0)
    dt = jnp.where(lane < n_heads, _softplus(dtraw + bias), 0.0)
    a = dt * apad
    tri = (_iota((l, l), 1) <= _iota((l, l), 0)).astype(BF16)
    acs = _exact_l(tri, a)
    width = n_heads * HEAD_DIM
    ex = (_iota((HEAD_PAD, width), 1) // HEAD_DIM == _iota((HEAD_PAD, width), 0)).astype(BF16)
    dt_full = _exact_r(dt, ex)
    acs_full = _exact_r(acs, ex)
    return dt, apad, acs, dt_full, acs_full, tri, ex


def _decay_mats(acs, acs_t, h, mask):
    col = acs[:, h:h + 1]
    row = acs_t[h:h + 1, :]
    return jnp.exp(jnp.where(mask, col - row, NEG))


def _ssd_fwd(xbc, proj, dtraw, dt_bias, a_log, dskip_full, gnorm, nb, s, d_inner):
    l = CHUNK
    nc = s // l
    n_heads = d_inner // HEAD_DIM
    gw = d_inner // GROUPS
    pairs_per_group = gw // LANES
    gn_cols = GROUPS * D_STATE

    def body(xs_ref, bm_ref, cm_ref, z_ref, dtr_ref, bias_ref, alog_ref, dsk_ref, gn_ref,
             y_ref, ya3_ref, prev_ref, state_ref):
        @pl.when(pl.program_id(1) == 0)
        def _():
            state_ref[...] = jnp.zeros_like(state_ref)

        xs = xs_ref[...].astype(F32)
        dt, apad, acs, dt_full, acs_full, tri, ex = _ssd_common(
            dtr_ref[...], bias_ref[...], alog_ref[...], n_heads)
        acs_t = acs.T
        e_full = jnp.exp(acs_full)
        last = acs_full[l - 1:l, :]
        dte = jnp.exp(last - acs_full)
        cd = jnp.exp(last)
        xdt = xs * dt_full
        xd = xdt * dte
        mask = _iota((l, l), 1) <= _iota((l, l), 0)
        lane = _iota((l, LANES), 1)
        prev_ref[0] = state_ref[...]
        for g in range(GROUPS):
            gsl = slice(g * gw, (g + 1) * gw)
            bg = bm_ref[:, g * D_STATE:(g + 1) * D_STATE].astype(BF16)
            cg = cm_ref[:, g * D_STATE:(g + 1) * D_STATE].astype(BF16)
            cb = _dot(cg, bg, NT)
            outs = []
            for q in range(pairs_per_group):
                pair = g * pairs_per_group + q
                h0, h1 = 2 * pair, 2 * pair + 1
                xpb = xdt[:, pair * LANES:(pair + 1) * LANES].astype(BF16)
                m0 = (cb * _decay_mats(acs, acs_t, h0, mask)).astype(BF16)
                m1 = (cb * _decay_mats(acs, acs_t, h1, mask)).astype(BF16)
                zero = jnp.zeros_like(xpb)
                rhs = jnp.concatenate([jnp.where(lane < HEAD_DIM, xpb, zero),
                                       jnp.where(lane >= HEAD_DIM, xpb, zero)], axis=0)
                outs.append(_dot(jnp.concatenate([m0, m1], axis=1), rhs))
            prev_g = state_ref[:, gsl]
            y_off = _dot(cg, prev_g.astype(BF16)) * e_full[:, gsl]
            y_g = jnp.concatenate(outs, axis=1) + y_off
            y_ref[:, gsl] = y_g.astype(y_ref.dtype)
            st = _dot(bg, xd[:, gsl].astype(BF16), TN)
            state_ref[:, gsl] = prev_g * cd[:, gsl] + st
            zv = z_ref[:, gsl].astype(F32)
            blk = (y_g + xs[:, gsl] * dsk_ref[:, gsl]) * (zv * _sigmoid(zv))
            rg = lax.rsqrt(jnp.mean(blk * blk, axis=-1, keepdims=True) + EPS)
            ya3_ref[:, gsl] = (blk * rg * gn_ref[:, gsl]).astype(ya3_ref.dtype)

    row = lambda b, c: b * nc + c
    small = lambda n: pl.BlockSpec((1, n), lambda b, c: (0, 0))
    return _pallas(
        body, name="ssd_fwd", grid=(nb, nc),
        in_specs=[pl.BlockSpec((l, d_inner), lambda b, c: (row(b, c), 0)),
                  pl.BlockSpec((l, gn_cols), lambda b, c: (row(b, c), d_inner // gn_cols)),
                  pl.BlockSpec((l, gn_cols), lambda b, c: (row(b, c), d_inner // gn_cols + 1)),
                  pl.BlockSpec((l, d_inner), lambda b, c: (row(b, c), 0)),
                  pl.BlockSpec((l, HEAD_PAD), lambda b, c: (row(b, c), 0)),
                  small(HEAD_PAD), small(HEAD_PAD), small(d_inner), small(d_inner)],
        out_specs=[pl.BlockSpec((l, d_inner), lambda b, c: (row(b, c), 0)),
                   pl.BlockSpec((l, d_inner), lambda b, c: (row(b, c), 0)),
                   pl.BlockSpec((1, D_STATE, d_inner), lambda b, c: (row(b, c), 0, 0))],
        out_shape=[jax.ShapeDtypeStruct((nb * s, d_inner), ACT),
                   jax.ShapeDtypeStruct((nb * s, d_inner), BF16),
                   jax.ShapeDtypeStruct((nb * nc, D_STATE, d_inner), F32)],
        scratch_shapes=[pltpu.VMEM((D_STATE, d_inner), F32)],
        compiler_params=_cparams(("parallel", "arbitrary"), 48),
    )(xbc, xbc, xbc, proj, dtraw, dt_bias, a_log, dskip_full, gnorm)


def _pool_fwd(proj, nb, s, off_u, d_pool):
    pg = d_pool // len(POOL_WINDOWS)

    def body(u_ref, o_ref):
        g = pl.program_id(1)
        uv = u_ref[...].astype(F32)
        row = _iota(uv.shape, 0)
        pos = (_iota((uv.shape[0], 1), 0) + 1).astype(F32)
        for gi, w in enumerate(POOL_WINDOWS):
            @pl.when(g == gi)
            def _():
                acc, span = uv, 1
                while span < w:
                    acc = acc + _shift_down(acc, span, row)
                    span *= 2
                o_ref[...] = (acc / jnp.minimum(pos, float(w)) - uv).astype(o_ref.dtype)

    return _pallas(
        body, name="pool_fwd", grid=(nb, len(POOL_WINDOWS)),
        in_specs=[pl.BlockSpec((s, pg), lambda b, g: (b, off_u // pg + g))],
        out_specs=pl.BlockSpec((s, pg), lambda b, g: (b, g)),
        out_shape=jax.ShapeDtypeStruct((nb * s, d_pool), BF16),
        compiler_params=_cparams(("parallel", "parallel"), 48),
    )(proj)


def _mix_fwd(pooled, mixw, proj, mix_b, scale, off_zp):
    t, d_pool = pooled.shape
    ng, pg = mixw.shape[0], mixw.shape[1]
    tm = min(1024, t)

    def body(a_ref, w_ref, zp_ref, b_ref, s_ref, mixed_ref, yb_ref):
        mixed = _dot(a_ref[...], w_ref[0])
        mixed_ref[...] = mixed.astype(mixed_ref.dtype)
        zp = zp_ref[...].astype(F32)
        yb_ref[...] = ((mixed + b_ref[...]) * s_ref[...] * (zp * _sigmoid(zp))).astype(yb_ref.dtype)

    blk = lambda off: pl.BlockSpec((tm, pg), lambda i, g: (i, off + g))
    vec = pl.BlockSpec((1, pg), lambda i, g: (0, g))
    return _pallas(
        body, name="mix_fwd", grid=(t // tm, ng),
        in_specs=[blk(0), pl.BlockSpec((1, pg, pg), lambda i, g: (g, 0, 0)), blk(off_zp // pg), vec, vec],
        out_specs=[blk(0), blk(0)],
        out_shape=[jax.ShapeDtypeStruct((t, d_pool), ACT), jax.ShapeDtypeStruct((t, d_pool), BF16)],
        compiler_params=_cparams(("parallel", "parallel")),
    )(pooled, mixw, proj, mix_b, scale)


def _merge_to_grads(ya3, wa, yb, wb, proj, off_ga, off_gb, wout, x, ple_g, wg, p, wup, target, final_g, n_main):
    t, d = x.shape
    pd = p.shape[1]
    tm = 256
    assert off_gb == off_ga + d and off_ga % (2 * d) == 0

    def body(a_ref, wa_ref, b_ref, wb_ref, ga_ref, gb_ref, wout_ref, x_ref, pg_ref, wg_ref, p_ref, wup_ref,
             tg_ref, fg_ref,
             mg_ref, h2_ref, dx1_ref, dx1b_ref, dgl_ref, dpu_ref, dpa_ref, dpb_ref, dgate_ref,
             loss_ref, dfg_ref, dpg_ref):
        @pl.when(pl.program_id(0) == 0)
        def _():
            loss_ref[...] = jnp.zeros_like(loss_ref)
            dfg_ref[...] = jnp.zeros_like(dfg_ref)
            dpg_ref[...] = jnp.zeros_like(dpg_ref)

        pa = _dot(a_ref[...], wa_ref[...])
        pb = _dot(b_ref[...], wb_ref[...])
        sa = _sigmoid(ga_ref[...].astype(F32))
        sb = _sigmoid(gb_ref[...].astype(F32))
        merged = (sa * pa + sb * pb).astype(BF16)
        mg_ref[...] = merged
        x1 = x_ref[...] + _dot(merged, wout_ref[...])
        r1 = lax.rsqrt(jnp.mean(x1 * x1, axis=-1, keepdims=True) + EPS)
        xh1 = x1 * r1
        pg = pg_ref[...]
        h2 = (xh1 * pg).astype(BF16)
        h2_ref[...] = h2
        gate = _sigmoid(_dot(h2, wg_ref[...]))
        pu = _dot(p_ref[...].astype(BF16), wup_ref[...])
        x2 = x1 + gate * pu
        r2 = lax.rsqrt(jnp.mean(x2 * x2, axis=-1, keepdims=True) + EPS)
        xh2 = x2 * r2
        gfin = fg_ref[...]
        err = xh2 * gfin - tg_ref[...]
        loss_ref[...] += _sum_all(err * err) * (0.5 / d)
        dy = err * (1.0 / d)
        dfg_ref[...] += jnp.sum(dy * xh2, axis=0, keepdims=True)
        dg = dy * gfin
        dx2 = r2 * (dg - xh2 * jnp.mean(dg * xh2, axis=-1, keepdims=True))
        dgl = (dx2 * pu * gate * (1.0 - gate)).astype(BF16)
        dgl_ref[...] = dgl
        dpu_ref[...] = (dx2 * gate).astype(dpu_ref.dtype)
        dh2 = _dot(dgl, wg_ref[...], NT)
        dpg_ref[...] += jnp.sum(dh2 * xh1, axis=0, keepdims=True)
        dg = dh2 * pg
        dx1 = dx2 + r1 * (dg - xh1 * jnp.mean(dg * xh1, axis=-1, keepdims=True))
        dx1_ref[...] = dx1
        dx1b = dx1.astype(BF16)
        dx1b_ref[...] = dx1b
        dm = _dot(dx1b, wout_ref[...], NT)
        dpa_ref[...] = (dm * sa).astype(dpa_ref.dtype)
        dpb_ref[...] = (dm * sb).astype(dpb_ref.dtype)
        dgate_ref[:, :d] = (dm * pa * sa * (1.0 - sa)).astype(dgate_ref.dtype)
        dgate_ref[:, d:] = (dm * pb * sb * (1.0 - sb)).astype(dgate_ref.dtype)

    rows = lambda n: pl.BlockSpec((tm, n), lambda i: (i, 0))
    resident = lambda a: pl.BlockSpec(a.shape, lambda i: (0, 0), pipeline_mode=pl.Buffered(1))
    vec = lambda n: pl.BlockSpec((1, n), lambda i: (0, 0))
    bf = lambda: jax.ShapeDtypeStruct((t, d), BF16)
    return _pallas(
        body, name="merge_to_grads", grid=(t // tm,),
        in_specs=[rows(ya3.shape[1]), resident(wa), rows(yb.shape[1]), resident(wb),
                  pl.BlockSpec((tm, d), lambda i: (i, off_ga // d)), pl.BlockSpec((tm, d), lambda i: (i, off_gb // d)),
                  resident(wout), rows(d), vec(d), resident(wg), rows(pd), resident(wup), rows(d), vec(d)],
        out_specs=[rows(d)] * 8 + [pl.BlockSpec((tm, 2 * d), lambda i: (i, off_ga // (2 * d))),
                                   vec(LANES), vec(d), vec(d)],
        out_shape=[bf(), bf(), jax.ShapeDtypeStruct((t, d), F32), bf(), bf(), bf(), bf(), bf(),
                   jax.ShapeDtypeStruct((t, n_main), BF16), jax.ShapeDtypeStruct((1, LANES), F32),
                   jax.ShapeDtypeStruct((1, d), F32), jax.ShapeDtypeStruct((1, d), F32)],
        compiler_params=_cparams(("arbitrary",), 60),
    )(ya3, wa, yb, wb, proj, proj, wout, x, ple_g, wg, p, wup, target, final_g)


def _yb_bwd(dproj, dpb, wb, proj, mixed, mix_b, scale, off_zp):
    t, dp = mixed.shape
    tm = 512

    def body(_, dpb_ref, w_ref, zp_ref, mx_ref, b_ref, s_ref, dzp_ref, dmx_ref, dsc_ref, db_ref):
        @pl.when(pl.program_id(0) == 0)
        def _():
            dsc_ref[...] = jnp.zeros_like(dsc_ref)
            db_ref[...] = jnp.zeros_like(db_ref)

        dyb = _dot(dpb_ref[...], w_ref[...], NT)
        zp = zp_ref[...].astype(F32)
        sg = _sigmoid(zp)
        mb = mx_ref[...].astype(F32) + b_ref[...]
        yb0 = mb * s_ref[...]
        dyb0 = dyb * (zp * sg)
        dzp_ref[...] = (dyb * yb0 * (sg * (1.0 + zp * (1.0 - sg)))).astype(dzp_ref.dtype)
        dsc_ref[...] += jnp.sum(dyb0 * mb, axis=0, keepdims=True)
        dmx = dyb0 * s_ref[...]
        db_ref[...] += jnp.sum(dmx, axis=0, keepdims=True)
        dmx_ref[...] = dmx.astype(dmx_ref.dtype)

    rowblk = pl.BlockSpec((tm, dp), lambda i: (i, 0))
    vec = pl.BlockSpec((1, dp), lambda i: (0, 0))
    seg = pl.BlockSpec((tm, dp), lambda i: (i, off_zp // dp))
    return _pallas(
        body, name="yb_bwd", grid=(t // tm,),
        in_specs=[ANY, pl.BlockSpec((tm, wb.shape[1]), lambda i: (i, 0)),
                  pl.BlockSpec(wb.shape, lambda i: (0, 0)), seg, rowblk, vec, vec],
        out_specs=[seg, rowblk, vec, vec],
        out_shape=[jax.ShapeDtypeStruct(dproj.shape, dproj.dtype), jax.ShapeDtypeStruct((t, dp), BF16),
                   jax.ShapeDtypeStruct((1, dp), F32), jax.ShapeDtypeStruct((1, dp), F32)],
        input_output_aliases={0: 0},
        compiler_params=_cparams(("arbitrary",), 56),
    )(dproj, dpb, wb, proj, mixed, mix_b, scale)


def _dpooled(dmixed, mixw):
    t, dp = dmixed.shape
    ng, pg = mixw.shape[0], mixw.shape[1]
    tm = min(1024, t)

    def body(a_ref, w_ref, o_ref):
        o_ref[...] = _dot(a_ref[...], w_ref[0], NT).astype(o_ref.dtype)

    blk = pl.BlockSpec((tm, pg), lambda i, g: (i, g))
    return _pallas(
        body, name="dpooled", grid=(t // tm, ng),
        in_specs=[blk, pl.BlockSpec((1, pg, pg), lambda i, g: (g, 0, 0))],
        out_specs=blk, out_shape=jax.ShapeDtypeStruct((t, dp), ACT),
        compiler_params=_cparams(("parallel", "parallel")),
    )(dmixed, mixw)


def _dmixw(pooled, dmixed, ng):
    t, dp = pooled.shape
    pg = dp // ng
    tk = min(2048, t)
    nk = t // tk

    def body(a_ref, b_ref, o_ref):
        @pl.when(pl.program_id(1) == 0)
        def _():
            o_ref[...] = jnp.zeros_like(o_ref)

        o_ref[0] += _dot(a_ref[...], b_ref[...], TN)

    blk = pl.BlockSpec((tk, pg), lambda g, k: (k, g))
    return _pallas(
        body, name="dmixw", grid=(ng, nk),
        in_specs=[blk, blk],
        out_specs=pl.BlockSpec((1, pg, pg), lambda g, k: (g, 0, 0)),
        out_shape=jax.ShapeDtypeStruct((ng, pg, pg), F32),
        compiler_params=_cparams(("parallel", "arbitrary")),
    )(pooled, dmixed)


def _pool_bwd(dproj, dpooled, nb, s, off_u):
    dp = dpooled.shape[1]
    pg = dp // len(POOL_WINDOWS)

    def body(_, d_ref, o_ref):
        g = pl.program_id(1)
        dv = d_ref[...].astype(F32)
        row = _iota(dv.shape, 0)
        pos = (_iota((dv.shape[0], 1), 0) + 1).astype(F32)
        for gi, w in enumerate(POOL_WINDOWS):
            @pl.when(g == gi)
            def _():
                acc, span = dv / jnp.minimum(pos, float(w)), 1
                while span < w:
                    acc = acc + _shift_up(acc, span, row)
                    span *= 2
                o_ref[...] = (acc - dv).astype(o_ref.dtype)

    return _pallas(
        body, name="pool_bwd", grid=(nb, len(POOL_WINDOWS)),
        in_specs=[ANY, pl.BlockSpec((s, pg), lambda b, g: (b, g))],
        out_specs=pl.BlockSpec((s, pg), lambda b, g: (b, off_u // pg + g)),
        out_shape=jax.ShapeDtypeStruct(dproj.shape, dproj.dtype),
        input_output_aliases={0: 0},
        compiler_params=_cparams(("parallel", "parallel"), 48),
    )(dproj, dpooled)


def _ssd_bwd(dproj, dya3, y, xbc, proj, dtraw, prev, dt_bias, a_log, dskip_full, gnorm, nb, s, d_inner, partials):
    l = CHUNK
    nc = s // l
    n_heads = d_inner // HEAD_DIM
    gw = d_inner // GROUPS
    pairs_per_group = gw // LANES
    gn_cols = GROUPS * D_STATE
    n_conv = d_inner + 2 * gn_cols
    last_step = nb * nc - 1

    xch = _ChipExchange(partials)
    n_in, n_out = 13, 7

    def body(*refs):
        (_, dya3_ref, y_ref, xs_ref, bm_ref, cm_ref, z_ref, dtr_ref, prev_ref, bias_ref, alog_ref,
         dsk_ref, gn_ref) = refs[:n_in]
        p_refs = refs[n_in:n_in + xch.n]
        outs = refs[n_in + xch.n:]
        dz_ref, dxbc_ref, ddt_ref, dgn_ref, dskh_ref, dalog_ref, dbias_ref = outs[:n_out]
        got_refs = outs[n_out:n_out + xch.n]
        g_ref, dskacc_ref = outs[n_out + xch.n:n_out + xch.n + 2]
        sems = outs[n_out + xch.n + 2:]
        step = pl.program_id(0) * nc + pl.program_id(1)

        @pl.when(pl.program_id(1) == 0)
        def _():
            g_ref[...] = jnp.zeros_like(g_ref)

        @pl.when(step == 0)
        def _():
            xch.start(p_refs, got_refs, sems)
            dgn_ref[...] = jnp.zeros_like(dgn_ref)
            dalog_ref[...] = jnp.zeros_like(dalog_ref)
            dbias_ref[...] = jnp.zeros_like(dbias_ref)
            dskacc_ref[...] = jnp.zeros_like(dskacc_ref)

        xs = xs_ref[...].astype(F32)
        zv = z_ref[...].astype(F32)
        dsk = dsk_ref[...]
        sg = _sigmoid(zv)
        sz = zv * sg
        ya1 = y_ref[...].astype(F32) + xs * dsk
        ya2 = ya1 * sz
        parts = []
        for g in range(GROUPS):
            gsl = slice(g * gw, (g + 1) * gw)
            blk = ya2[:, gsl]
            rg = lax.rsqrt(jnp.mean(blk * blk, axis=-1, keepdims=True) + EPS)
            xh = blk * rg
            d3 = dya3_ref[:, gsl].astype(F32)
            dgn_ref[:, gsl] += jnp.sum(d3 * xh, axis=0, keepdims=True)
            dg = d3 * gn_ref[:, gsl]
            parts.append(rg * (dg - xh * jnp.mean(dg * xh, axis=-1, keepdims=True)))
        dya2 = jnp.concatenate(parts, axis=1)
        dy = dya2 * sz
        dz_ref[...] = (dya2 * ya1 * (sg * (1.0 + zv * (1.0 - sg)))).astype(dz_ref.dtype)
        dskacc_ref[...] += jnp.sum(dy * xs, axis=0, keepdims=True)

        dtraw_v = dtr_ref[...]
        bias = bias_ref[...]
        dt, apad, acs, dt_full, acs_full, tri, ex = _ssd_common(dtraw_v, bias, alog_ref[...], n_heads)
        acs_t = acs.T
        e_full = jnp.exp(acs_full)
        last = acs_full[l - 1:l, :]
        dte = jnp.exp(last - acs_full)
        cd = jnp.exp(last)
        xdt = xs * dt_full
        xd = xdt * dte
        dy_e = dy * e_full
        mask = _iota((l, l), 1) <= _iota((l, l), 0)
        lane = _iota((l, LANES), 1)
        hlane = _iota((l, HEAD_PAD), 1)
        hsub = _iota((HEAD_PAD, l), 0)
        dacs = jnp.zeros((l, HEAD_PAD), F32)
        dacs_t = jnp.zeros((HEAD_PAD, l), F32)
        dacs_full, dlast, dxdt_all, db_all, dc_all = [], [], [], [], []
        for g in range(GROUPS):
            gsl = slice(g * gw, (g + 1) * gw)
            bg = bm_ref[:, g * D_STATE:(g + 1) * D_STATE].astype(BF16)
            cg = cm_ref[:, g * D_STATE:(g + 1) * D_STATE].astype(BF16)
            prev_g = prev_ref[0, :, gsl]
            prev_b = prev_g.astype(BF16)
            gst = g_ref[:, gsl]
            gst_b = gst.astype(BF16)
            dye_b = dy_e[:, gsl].astype(BF16)
            dc = _dot(dye_b, prev_b, NT)
            dprev = _dot(cg, dye_b, TN)
            y_off = _dot(cg, prev_b) * e_full[:, gsl]
            xd_g = xd[:, gsl]
            db = _dot(xd_g.astype(BF16), gst_b, NT)
            dxd = _dot(bg, gst_b)
            wgt = dxd * xd_g
            dacs_full.append(dy[:, gsl] * y_off - wgt)
            dlast.append(jnp.sum(wgt, axis=0, keepdims=True)
                         + jnp.sum(prev_g * gst, axis=0, keepdims=True) * cd[:, gsl])
            g_ref[:, gsl] = dprev + cd[:, gsl] * gst
            dxdt_g = dxd * dte[:, gsl]
            cb = _dot(cg, bg, NT)
            dcb = jnp.zeros((l, l), F32)
            for q in range(pairs_per_group):
                pair = g * pairs_per_group + q
                psl = slice(pair * LANES, (pair + 1) * LANES)
                xpb = xdt[:, psl].astype(BF16)
                dyp = dy[:, psl].astype(BF16)
                zero = jnp.zeros_like(dyp)
                halves = (jnp.where(lane < HEAD_DIM, dyp, zero), jnp.where(lane >= HEAD_DIM, dyp, zero))
                acc = dxdt_g[:, q * LANES:(q + 1) * LANES]
                for k, dyh in enumerate(halves):
                    h = 2 * pair + k
                    dk = _decay_mats(acs, acs_t, h, mask)
                    mm = cb * dk
                    dm = _dot(dyh, xpb, NT)
                    dcb = dcb + dm * dk
                    dseg = dm * mm
                    dacs = dacs + jnp.where(hlane == h, jnp.sum(dseg, axis=1, keepdims=True), 0.0)
                    dacs_t = dacs_t + jnp.where(hsub == h, jnp.sum(dseg, axis=0, keepdims=True), 0.0)
                    acc = acc + _dot(mm.astype(BF16), dyh, TN)
                dxdt_all.append(acc)
            dcb_b = dcb.astype(BF16)
            dc_all.append(dc + _dot(dcb_b, bg))
            db_all.append(db + _dot(dcb_b, cg, TN))

        dxdt = jnp.concatenate(dxdt_all, axis=1)
        dxbc_ref[:, :d_inner] = (dy * dsk + dxdt * dt_full).astype(dxbc_ref.dtype)
        dxbc_ref[:, d_inner:d_inner + gn_cols] = jnp.concatenate(db_all, axis=1).astype(dxbc_ref.dtype)
        dxbc_ref[:, d_inner + gn_cols:] = jnp.concatenate(dc_all, axis=1).astype(dxbc_ref.dtype)
        ddt = _exact_r(dxdt * xs, ex, NT)
        rowi = _iota((l, d_inner), 0)
        dacs_f = jnp.concatenate(dacs_full, axis=1) + jnp.where(rowi == l - 1, jnp.concatenate(dlast, axis=1), 0.0)
        dacs = dacs + _exact_r(dacs_f, ex, NT) - dacs_t.T
        da = _exact_l(tri, dacs, TN)
        ddt = ddt + da * apad
        dalog_ref[...] += jnp.sum(da * dt, axis=0, keepdims=True) * apad
        ddt_raw = ddt * _sigmoid(dtraw_v + bias)
        dbias_ref[...] += jnp.sum(ddt_raw, axis=0, keepdims=True)
        ddt_ref[...] = ddt_raw.astype(ddt_ref.dtype)

        @pl.when(step == last_step)
        def _():
            dskh_ref[...] = _exact_r(dskacc_ref[...], ex, NT)[0:1, :]
            xch.wait(p_refs, got_refs, sems)

    row = lambda b, c: b * nc + (nc - 1 - c)
    small = lambda n: pl.BlockSpec((1, n), lambda b, c: (0, 0))
    wide = pl.BlockSpec((l, d_inner), lambda b, c: (row(b, c), 0))
    return _pallas(
        body, name="ssd_bwd", grid=(nb, nc),
        in_specs=[ANY, wide, wide, wide,
                  pl.BlockSpec((l, gn_cols), lambda b, c: (row(b, c), d_inner // gn_cols)),
                  pl.BlockSpec((l, gn_cols), lambda b, c: (row(b, c), d_inner // gn_cols + 1)),
                  wide, pl.BlockSpec((l, HEAD_PAD), lambda b, c: (row(b, c), 0)),
                  pl.BlockSpec((1, D_STATE, d_inner), lambda b, c: (row(b, c), 0, 0)),
                  small(HEAD_PAD), small(HEAD_PAD), small(d_inner), small(d_inner)] + xch.in_specs,
        out_specs=[wide, pl.BlockSpec((l, n_conv), lambda b, c: (row(b, c), 0)),
                   pl.BlockSpec((l, HEAD_PAD), lambda b, c: (row(b, c), 0)),
                   small(d_inner), small(HEAD_PAD), small(HEAD_PAD), small(HEAD_PAD)] + xch.out_specs,
        out_shape=[jax.ShapeDtypeStruct(dproj.shape, dproj.dtype),
                   jax.ShapeDtypeStruct((nb * s, n_conv), ACT),
                   jax.ShapeDtypeStruct((nb * s, HEAD_PAD), BF16),
                   jax.ShapeDtypeStruct((1, d_inner), F32), jax.ShapeDtypeStruct((1, HEAD_PAD), F32),
                   jax.ShapeDtypeStruct((1, HEAD_PAD), F32), jax.ShapeDtypeStruct((1, HEAD_PAD), F32)]
        + xch.out_shape,
        scratch_shapes=[pltpu.VMEM((D_STATE, d_inner), F32), pltpu.VMEM((8, d_inner), F32)] + xch.scratch,
        input_output_aliases={0: 0},
        compiler_params=_cparams(("arbitrary", "arbitrary"), 56),
    )(dproj, dya3, y, xbc, xbc, xbc, proj, dtraw, prev, dt_bias, a_log, dskip_full, gnorm, *partials)


def _conv_bwd(dproj, dxbc, conv_pre, proj, conv_w, nb, s, off_xbc):
    n_conv = dxbc.shape[1]
    cw = 256

    rc = CONV_ROWS

    def fold(v):
        return jnp.sum(v.reshape(rc // SUBLANES, SUBLANES, cw), axis=0)

    def body(_, d_ref, pre_ref, x_ref, w_ref, dx_ref, dw_ref, db_ref, dcpad_ref):
        @pl.when(pl.program_id(1) == 0)
        def _():
            dw_ref[...] = jnp.zeros_like(dw_ref)
            db_ref[...] = jnp.zeros_like(db_ref)

        dcpad_ref[pl.ds(s, SUBLANES), :] = jnp.zeros((SUBLANES, cw), F32)
        taps = [w_ref[k:k + 1, :] for k in range(CONV_W)]
        db = jnp.zeros((SUBLANES, cw), F32)
        for r in range(s // rc):
            rows = pl.ds(r * rc, rc)
            acc = pre_ref[rows, :].astype(F32)
            sg = _sigmoid(acc)
            dc = d_ref[rows, :].astype(F32) * (sg * (1.0 + acc * (1.0 - sg)))
            dcpad_ref[rows, :] = dc
            db = db + fold(dc)
        db_ref[...] += jnp.sum(db, axis=0, keepdims=True)
        dw = [jnp.zeros((SUBLANES, cw), F32) for _ in range(CONV_W)]
        for r in range(s // rc):
            xv = x_ref[pl.ds(r * rc, rc), :].astype(F32)
            win = dcpad_ref[pl.ds(r * rc, rc + SUBLANES), :]
            dxv = jnp.zeros((rc, cw), F32)
            for j in range(CONV_W):
                up = win[:rc, :] if j == 0 else pltpu.roll(win, rc + SUBLANES - j, 0)[:rc, :]
                dxv = dxv + up * taps[CONV_W - 1 - j]
                dw[CONV_W - 1 - j] = dw[CONV_W - 1 - j] + fold(up * xv)
            dx_ref[pl.ds(r * rc, rc), :] = dxv.astype(dx_ref.dtype)
        for k in range(CONV_W):
            dw_ref[k:k + 1, :] += jnp.sum(dw[k], axis=0, keepdims=True)

    blk = pl.BlockSpec((s, cw), lambda j, b: (b, j))
    return _pallas(
        body, name="conv_bwd", grid=(n_conv // cw, nb),
        in_specs=[ANY, blk, blk, pl.BlockSpec((s, cw), lambda j, b: (b, off_xbc // cw + j)),
                  pl.BlockSpec((CONV_W, cw), lambda j, b: (0, j))],
        out_specs=[pl.BlockSpec((s, cw), lambda j, b: (b, off_xbc // cw + j)),
                   pl.BlockSpec((CONV_W, cw), lambda j, b: (0, j)), pl.BlockSpec((1, cw), lambda j, b: (0, j))],
        out_shape=[jax.ShapeDtypeStruct(dproj.shape, dproj.dtype),
                   jax.ShapeDtypeStruct((CONV_W, n_conv), F32), jax.ShapeDtypeStruct((1, n_conv), F32)],
        scratch_shapes=[pltpu.VMEM((s + SUBLANES, cw), F32)],
        input_output_aliases={0: 0},
        compiler_params=_cparams(("parallel", "arbitrary"), 48),
    )(dproj, dxbc, conv_pre, proj, conv_w)


def _in_bwd(dproj, w_main, ddt, w_dt, x, norm_g, dx1, partials):
    t, d = x.shape
    n_main = dproj.shape[1]
    tm = 256
    ni = t // tm
    xch = _ChipExchange(partials)
    n_in, n_out = 7, 2

    def body(*refs):
        dp_ref, w_ref, ddt_ref, wdt_ref, x_ref, g_ref, dx1_ref = refs[:n_in]
        p_refs = refs[n_in:n_in + xch.n]
        outs = refs[n_in + xch.n:]
        dx_ref, dg_ref = outs[:n_out]
        got_refs = outs[n_out:n_out + xch.n]
        sems = outs[n_out + xch.n:]
        i = pl.program_id(0)

        @pl.when(i == 0)
        def _():
            xch.start(p_refs, got_refs, sems)
            dg_ref[...] = jnp.zeros_like(dg_ref)

        dh = _dot(dp_ref[...], w_ref[...], NT) + _dot(ddt_ref[...], wdt_ref[...], NT)
        xv = x_ref[...]
        r = lax.rsqrt(jnp.mean(xv * xv, axis=-1, keepdims=True) + EPS)
        xh = xv * r
        dg_ref[...] += jnp.sum(dh * xh, axis=0, keepdims=True)
        dg = dh * g_ref[...]
        dx_ref[...] = dx1_ref[...] + r * (dg - xh * jnp.mean(dg * xh, axis=-1, keepdims=True))

        @pl.when(i == ni - 1)
        def _():
            xch.wait(p_refs, got_refs, sems)

    rowblk = pl.BlockSpec((tm, d), lambda i: (i, 0))
    vec = pl.BlockSpec((1, d), lambda i: (0, 0))
    resident = lambda shape: pl.BlockSpec(shape, lambda i: (0, 0), pipeline_mode=pl.Buffered(1))
    return _pallas(
        body, name="in_bwd", grid=(ni,),
        in_specs=[pl.BlockSpec((tm, n_main), lambda i: (i, 0)), resident((d, n_main)),
                  pl.BlockSpec((tm, HEAD_PAD), lambda i: (i, 0)), resident((d, HEAD_PAD)),
                  rowblk, vec, rowblk] + xch.in_specs,
        out_specs=[rowblk, vec] + xch.out_specs,
        out_shape=[jax.ShapeDtypeStruct((t, d), F32), jax.ShapeDtypeStruct((1, d), F32)] + xch.out_shape,
        scratch_shapes=xch.scratch,
        compiler_params=_cparams(("arbitrary",), 56),
    )(dproj, w_main, ddt, w_dt, x, norm_g, dx1, *partials)


def _adamw_math(w, g, m, v):
    m = ADAM_B1 * m + (1.0 - ADAM_B1) * g
    v = ADAM_B2 * v + (1.0 - ADAM_B2) * (g * g)
    m_hat = m / (1.0 - ADAM_B1 ** ADAM_STEP)
    v_hat = v / (1.0 - ADAM_B2 ** ADAM_STEP)
    delta = -ADAM_LR * (m_hat / (jnp.sqrt(v_hat) + ADAM_EPS) + ADAM_WD * w)
    return delta, m, v


def _adamw(parts, w, m, v, name):
    n, rows, cols = parts.shape
    tr = _row_tile(rows, 128) if rows % 8 == 0 else rows

    def body(p_ref, w_ref, m_ref, v_ref, g_ref, d_ref, mo_ref, vo_ref):
        g = p_ref[0].astype(F32)
        for k in range(1, n):
            g = g + p_ref[k].astype(F32)
        d, mn, vn = _adamw_math(w_ref[...], g, m_ref[...], v_ref[...])
        g_ref[...] = g
        d_ref[...] = d
        mo_ref[...] = mn
        vo_ref[...] = vn

    blk = pl.BlockSpec((tr, cols), lambda i: (i, 0))
    shp = jax.ShapeDtypeStruct((rows, cols), F32)
    return _pallas(
        body, name=name, grid=(rows // tr,),
        in_specs=[pl.BlockSpec((n, tr, cols), lambda i: (0, i, 0)), blk, blk, blk],
        out_specs=[blk, blk, blk, blk], out_shape=[shp, shp, shp, shp],
        compiler_params=_cparams(("parallel",), 56),
    )(parts, w, m, v)


def _coords():
    return lax.axis_index("x"), lax.axis_index("y"), lax.axis_index("c")


class _TwoLevelGather:
    def __init__(self, blocks):
        self.n = len(blocks)
        self.in_specs = [ANY] * self.n
        self.out_specs = [ANY] * self.n
        self.out_shape = [jax.ShapeDtypeStruct((N_DEV,) + b.shape, b.dtype) for b in blocks]
        self.scratch = [pltpu.SemaphoreType.DMA((self.n, 7)), pltpu.SemaphoreType.DMA((self.n, 7)),
                        pltpu.SemaphoreType.DMA((self.n,))]

    def _parts(self, x_refs, out_refs, sems):
        send_sems, recv_sems, local_sems = sems
        x, y, c = _coords()
        me, sibling = (x, y, c), (x, y, 1 - c)
        chips = [(1 - x, y), (x, 1 - y), (1 - x, 1 - y)]

        def slot(a, px, py, pc):
            return out_refs[a].at[4 * px + 2 * py + pc]

        def copy(a, k, block, to, src=None):
            return pltpu.make_async_remote_copy(
                src_ref=slot(a, *block) if src is None else src, dst_ref=slot(a, *block),
                send_sem=send_sems.at[a, k], recv_sem=recv_sems.at[a, k], device_id=to, device_id_type=MESH)

        local = [pltpu.make_async_copy(x_refs[a], slot(a, *me), local_sems.at[a]) for a in range(self.n)]
        first = []
        for a in range(self.n):
            first.append(copy(a, 0, me, sibling, src=x_refs[a]))
            first += [copy(a, 1 + j, me, (*chip, c), src=x_refs[a]) for j, chip in enumerate(chips)]
        return copy, local, first, chips, me, sibling, c

    def start(self, x_refs, out_refs, sems):
        _, local, first, *_ = self._parts(x_refs, out_refs, sems)
        for cp in local + first:
            cp.start()

    def finish(self, x_refs, out_refs, sems):
        copy, local, first, chips, me, sibling, c = self._parts(x_refs, out_refs, sems)
        passed = []
        for j, chip in enumerate(chips):
            for a in range(self.n):
                copy(a, 1 + j, (*chip, c), me).wait_recv()
                passed.append(copy(a, 4 + j, (*chip, c), sibling))
                passed[-1].start()
        for a in range(self.n):
            copy(a, 0, sibling, me).wait_recv()
            for j, chip in enumerate(chips):
                copy(a, 4 + j, (*chip, 1 - c), me).wait_recv()
        for cp in first + passed:
            cp.wait_send()
        for cp in local:
            cp.wait()


def _all_gather_routed(blk, name):
    rows, cols = blk.shape
    half = rows // 2
    assert half % 16 == 0

    def body(x_ref, out_ref, send_sems, recv_sems, local_sem):
        x, y, c = _coords()
        me, sib, xn, yn, dg = (x, y, c), (x, y, 1 - c), (1 - x, y, c), (x, 1 - y, c), (1 - x, 1 - y, c)
        other = lambda p: (p[0], p[1], 1 - c)
        top, bot = pl.ds(0, half), pl.ds(half, half)

        def slot(p, part=None):
            ref = out_ref.at[4 * p[0] + 2 * p[1] + p[2]]
            return ref if part is None else ref.at[part]

        def copy(k, block, to, part=None, src=None):
            return pltpu.make_async_remote_copy(
                src_ref=slot(block, part) if src is None else src, dst_ref=slot(block, part),
                send_sem=send_sems.at[k], recv_sem=recv_sems.at[k], device_id=to, device_id_type=MESH)

        mine = pltpu.make_async_copy(x_ref, slot(me), local_sem)
        own = [copy(1, me, xn, src=x_ref), copy(2, me, yn, src=x_ref), copy(0, me, sib, src=x_ref)]
        for cp in [mine] + own:
            cp.start()
        copy(1, xn, me).wait_recv()
        fwd = [copy(3, xn, yn, top), copy(5, xn, sib)]
        for cp in fwd:
            cp.start()
        copy(2, yn, me).wait_recv()
        fwd += [copy(4, yn, xn, bot), copy(6, yn, sib)]
        for cp in fwd[2:]:
            cp.start()
        copy(3, dg, me, top).wait_recv()
        fwd.append(copy(7, dg, sib, top))
        fwd[-1].start()
        copy(4, dg, me, bot).wait_recv()
        fwd.append(copy(8, dg, sib, bot))
        fwd[-1].start()
        copy(0, sib, me).wait_recv()
        copy(5, other(xn), me).wait_recv()
        copy(6, other(yn), me).wait_recv()
        copy(7, other(dg), me, top).wait_recv()
        copy(8, other(dg), me, bot).wait_recv()
        for cp in own + fwd:
            cp.wait_send()
        mine.wait()

    return _pallas(
        body, name=name, in_specs=[ANY], out_specs=ANY,
        out_shape=jax.ShapeDtypeStruct((N_DEV, rows, cols), blk.dtype),
        scratch_shapes=[pltpu.SemaphoreType.DMA((9,)), pltpu.SemaphoreType.DMA((9,)), pltpu.SemaphoreType.DMA],
    )(blk)


def _all_gather_direct(blk, name):
    rows, cols = blk.shape

    def body(x_ref, out_ref, send_sems, recv_sems, local_sem):
        x, y, c = _coords()
        me = 4 * x + 2 * y + c
        mine = pltpu.make_async_copy(x_ref, out_ref.at[me], local_sem)
        mine.start()
        copies = []
        for k in range(1, N_DEV):
            fx, fy, fc = (k >> 2) & 1, (k >> 1) & 1, k & 1
            peer = (x ^ fx, y ^ fy, c ^ fc)
            copies.append(pltpu.make_async_remote_copy(
                src_ref=x_ref, dst_ref=out_ref.at[me], send_sem=send_sems.at[k - 1], recv_sem=recv_sems.at[k - 1],
                device_id=peer, device_id_type=MESH))
        for cp in copies:
            cp.start()
        for k in range(1, N_DEV):
            fx, fy, fc = (k >> 2) & 1, (k >> 1) & 1, k & 1
            src = 4 * (x ^ fx) + 2 * (y ^ fy) + (c ^ fc)
            pltpu.make_async_remote_copy(
                src_ref=x_ref, dst_ref=out_ref.at[src], send_sem=send_sems.at[k - 1], recv_sem=recv_sems.at[k - 1],
                device_id=(x, y, c), device_id_type=MESH).wait_recv()
        for cp in copies:
            cp.wait_send()
        mine.wait()

    return _pallas(
        body, name=name, in_specs=[ANY], out_specs=ANY,
        out_shape=jax.ShapeDtypeStruct((N_DEV, rows, cols), blk.dtype),
        scratch_shapes=[pltpu.SemaphoreType.DMA((7,)), pltpu.SemaphoreType.DMA((7,)), pltpu.SemaphoreType.DMA],
    )(blk)


def _pair_exchange(grads, name):
    n = len(grads)

    def body(*refs):
        g_refs, out_refs = refs[:n], refs[n:2 * n]
        send_sems, recv_sems = refs[2 * n:]
        x, y, c = _coords()
        copies = []
        for a in range(n):
            for k in range(4):
                copies.append(pltpu.make_async_remote_copy(
                    src_ref=g_refs[a].at[2 * k + (1 - c)], dst_ref=out_refs[a].at[k],
                    send_sem=send_sems.at[a, k], recv_sem=recv_sems.at[a, k],
                    device_id=(x, y, 1 - c), device_id_type=MESH))
        for cp in copies:
            cp.start()
        for cp in copies:
            cp.wait()

    return _pallas(
        body, name=name, in_specs=[ANY] * n, out_specs=[ANY] * n,
        out_shape=[jax.ShapeDtypeStruct((4,) + g.shape[1:], g.dtype) for g in grads],
        scratch_shapes=[pltpu.SemaphoreType.DMA((n, 4)), pltpu.SemaphoreType.DMA((n, 4))],
    )(*grads)


def _pair_add(grads, got, core, out_dtype, name):
    _, rows, cols = grads.shape
    tr = _row_tile(rows, 256) if rows % 8 == 0 else rows

    def body(c_ref, a_ref, b_ref, o_ref):
        o_ref[...] = (a_ref[...].astype(F32) + b_ref[...].astype(F32)).astype(o_ref.dtype)

    grid_spec = pltpu.PrefetchScalarGridSpec(
        num_scalar_prefetch=1, grid=(4, rows // tr),
        in_specs=[pl.BlockSpec((1, tr, cols), lambda k, i, c_ref: (2 * k + c_ref[0], i, 0)),
                  pl.BlockSpec((1, tr, cols), lambda k, i, c_ref: (k, i, 0))],
        out_specs=pl.BlockSpec((1, tr, cols), lambda k, i, c_ref: (k, i, 0)))
    return _pallas(
        body, name=name, grid_spec=grid_spec,
        out_shape=jax.ShapeDtypeStruct((4, rows, cols), out_dtype),
        compiler_params=_cparams(("parallel", "parallel"), 48),
    )(core, grads, got)


class _ChipExchange:
    def __init__(self, partials):
        self.n = len(partials)
        self.in_specs = [ANY] * self.n
        self.out_specs = [ANY] * self.n
        self.out_shape = [jax.ShapeDtypeStruct(p.shape, p.dtype) for p in partials]
        self.scratch = [pltpu.SemaphoreType.DMA((self.n, 3)), pltpu.SemaphoreType.DMA((self.n, 3)),
                        pltpu.SemaphoreType.DMA((self.n,))]

    def _copies(self, p_refs, out_refs, sems, receiving):
        send_sems, recv_sems, local_sems = sems
        x, y, c = _coords()
        mychip = 2 * x + y
        local = [pltpu.make_async_copy(p_refs[a].at[mychip], out_refs[a].at[mychip], local_sems.at[a])
                 for a in range(self.n)]
        remote = []
        for a in range(self.n):
            for k in range(1, 4):
                px, py = x ^ ((k >> 1) & 1), y ^ (k & 1)
                peer = 2 * px + py
                remote.append(pltpu.make_async_remote_copy(
                    src_ref=p_refs[a].at[peer], dst_ref=out_refs[a].at[peer if receiving else mychip],
                    send_sem=send_sems.at[a, k - 1], recv_sem=recv_sems.at[a, k - 1],
                    device_id=(x, y, c) if receiving else (px, py, c), device_id_type=MESH))
        return local, remote

    def start(self, p_refs, out_refs, sems):
        local, sends = self._copies(p_refs, out_refs, sems, False)
        for cp in local + sends:
            cp.start()

    def wait(self, p_refs, out_refs, sems):
        local, recvs = self._copies(p_refs, out_refs, sems, True)
        for cp in recvs:
            cp.wait_recv()
        for cp in recvs:
            cp.wait_send()
        for cp in local:
            cp.wait()


def _pad_lanes(v, n=HEAD_PAD):
    return jnp.pad(v, ((0, 0), (0, n - v.shape[1])))


def kernel(x, p, norm_g, w_in, conv_w, conv_b, dt_bias, a_log, d_skip, gnorm_g, pool_mix_w, pool_mix_b, pool_scale, w_branch_a, w_branch_b, w_out, ple_norm_g, w_ple_gate, w_ple_up, final_g, loss_target, m_norm_g, m_w_in, m_conv_w, m_conv_b, m_dt_bias, m_a_log, m_d_skip, m_gnorm_g, m_pool_mix_w, m_pool_mix_b, m_pool_scale, m_w_branch_a, m_w_branch_b, m_w_out, m_ple_norm_g, m_w_ple_gate, m_w_ple_up, m_final_g, v_norm_g, v_w_in, v_conv_w, v_conv_b, v_dt_bias, v_a_log, v_d_skip, v_gnorm_g, v_pool_mix_w, v_pool_mix_b, v_pool_scale, v_w_branch_a, v_w_branch_b, v_w_out, v_ple_norm_g, v_w_ple_gate, v_w_ple_up, v_final_g):
    nb, s, d = x.shape
    t = nb * s
    n_heads = dt_bias.shape[1]
    d_inner = n_heads * HEAD_DIM
    n_conv = conv_b.shape[1]
    d_pool = pool_mix_b.shape[1]
    n_in = w_in.shape[2] * N_DEV
    ple_dim = p.shape[-1]
    o_xbc, o_dt = d_inner, d_inner + n_conv
    o_u = o_dt + n_heads
    o_zp = o_u + d_pool
    o_ga, o_gb = o_zp + d_pool, o_zp + d_pool + d
    n_main = n_in - n_heads
    off_ga, off_gb, off_xbc = d_inner, d_inner + d, d_inner + 2 * d
    off_u = off_xbc + n_conv
    off_zp = off_u + d_pool

    sharded = [w_in[0], conv_w[0], pool_mix_w[0], w_branch_a[0], w_branch_b[0], w_out[0], w_ple_gate[0], w_ple_up[0]]
    sharded_m = [m_w_in[0], m_conv_w[0], m_pool_mix_w[0], m_w_branch_a[0], m_w_branch_b[0], m_w_out[0],
                 m_w_ple_gate[0], m_w_ple_up[0]]
    sharded_v = [v_w_in[0], v_conv_w[0], v_pool_mix_w[0], v_w_branch_a[0], v_w_branch_b[0], v_w_out[0],
                 v_w_ple_gate[0], v_w_ple_up[0]]
    ng = pool_mix_w.shape[1]
    pg = d_pool // ng
    two_d = lambda w: w.reshape(-1, w.shape[-1])
    sharded, sharded_m, sharded_v = ([two_d(w) for w in ws] for ws in (sharded, sharded_m, sharded_v))

    send = [w if i == 1 else w.astype(BF16) for i, w in enumerate(sharded)]
    seg = [_all_gather_routed(send[0], "w_in_all_gather")]
    chunk = n_in // N_DEV
    segments = [((0, o_xbc), 0), ((o_ga, o_gb), off_ga), ((o_gb, n_in), off_gb), ((o_xbc, o_dt), off_xbc),
                ((o_u, o_zp), off_u), ((o_zp, o_ga), off_zp)]

    def gathered_cols(lo, hi):
        out = []
        while lo < hi:
            dev = lo // chunk
            a, b = lo - dev * chunk, min(hi - dev * chunk, chunk)
            out.append(seg[0][dev, :, a:b])
            lo = dev * chunk + b
        return out

    w_main = jnp.concatenate([pc for (lo, hi), _ in segments for pc in gathered_cols(lo, hi)], axis=1)
    w_dt = _pad_lanes(jnp.concatenate(gathered_cols(o_dt, o_u), axis=1))

    x2 = x.reshape(t, d)
    h = _rms_fwd(x2, norm_g, "rms_in")
    proj, *rest = _in_proj(h, w_main, send[1:])
    seg += rest
    cw_full = seg[1].transpose(1, 0, 2).reshape(CONV_W, n_conv)
    mixw_full = seg[2].reshape(N_DEV, ng, pg // N_DEV, pg).transpose(1, 0, 2, 3).reshape(ng, pg, pg)
    wa_full = seg[3].reshape(d_inner, d)
    wb_full = seg[4].reshape(d_pool, d)
    wout_full = seg[5].reshape(d, d)
    wg_full = seg[6].reshape(d, d)
    wup_full = seg[7].transpose(1, 0, 2).reshape(ple_dim, d)

    p2 = p.reshape(t, ple_dim)
    tgt = loss_target.reshape(t, d)
    bias_p, alog_p = _pad_lanes(dt_bias), _pad_lanes(a_log)
    dskip_full = jnp.repeat(d_skip, HEAD_DIM, axis=1)

    dtraw = _matmul(h, w_dt, "nn", F32, "in_proj_dt", 1024, HEAD_PAD, d)
    xbc, conv_pre = _conv_fwd(proj, cw_full, conv_b, nb, s, off_xbc, n_conv)
    y, ya3, prev = _ssd_fwd(xbc, proj, dtraw, bias_p, alog_p, dskip_full, gnorm_g, nb, s, d_inner)
    pooled = _pool_fwd(proj, nb, s, off_u, d_pool)
    mixed, yb = _mix_fwd(pooled, mixw_full, proj, pool_mix_b, pool_scale, off_zp)
    (merged, h2, dx1, dx1b, dgl, dpu, dpa, dpb, dproj, loss_part, dfinal_g, d_ple_g) = _merge_to_grads(
        ya3, wa_full, yb, wb_full, proj, off_ga, off_gb, wout_full, x2, ple_norm_g, wg_full, p2, wup_full, tgt,
        final_g.reshape(1, d), n_main)

    g_wup = _matmul(p2, dpu, "tn", BF16, "dw_ple_up", 256, 512, t)
    g_wg = _matmul(h2, dgl, "tn", BF16, "dw_ple_gate", 512, 512, t)
    g_wout = _matmul(merged, dx1b, "tn", BF16, "dw_out", 512, 512, t)
    g_wa = _matmul(ya3, dpa, "tn", BF16, "dw_branch_a", 512, 512, t)
    g_wb = _matmul(yb, dpb, "tn", BF16, "dw_branch_b", 512, 512, t)
    dya3 = _matmul(dpa, wa_full, "nt", ACT, "d_ya3", 1024, 1024, d)
    dproj, dmixed, d_scale, d_mixb = _yb_bwd(dproj, dpb, wb_full, proj, mixed, pool_mix_b, pool_scale, off_zp)
    g_mixw = _dmixw(pooled, dmixed, ng)
    dpool = _dpooled(dmixed, mixw_full)
    dproj = _pool_bwd(dproj, dpool, nb, s, off_u)

    core = lax.axis_index("c").astype(I32).reshape(1)

    def pair_sums(by_dest, names):
        by_dest = [g.astype(BF16) for g in by_dest]
        got = _pair_exchange(by_dest, "rs_pair_exchange_" + names[0])
        return [_pair_add(g, r, core, BF16, "rs_pair_add_" + nm) for g, r, nm in zip(by_dest, got, names)]

    names_a = ["pool_mix_w", "w_branch_a", "w_branch_b", "w_out", "w_ple_gate", "w_ple_up"]
    partials_a = pair_sums([
        g_mixw.reshape(ng, N_DEV, pg // N_DEV, pg).transpose(1, 0, 2, 3).reshape(N_DEV, ng * pg // N_DEV, pg),
        g_wa.reshape(N_DEV, d_inner // N_DEV, d), g_wb.reshape(N_DEV, d_pool // N_DEV, d),
        g_wout.reshape(N_DEV, d // N_DEV, d), g_wg.reshape(N_DEV, d // N_DEV, d),
        g_wup.reshape(ple_dim, N_DEV, d // N_DEV).transpose(1, 0, 2)], names_a)
    dproj, dxbc, ddt, d_gn, d_dsk, d_alog, d_dtb, *parts_a = _ssd_bwd(
        dproj, dya3, y, xbc, proj, dtraw, prev, bias_p, alog_p, dskip_full, gnorm_g, nb, s, d_inner, partials_a)
    dproj, g_cw, d_cb = _conv_bwd(dproj, dxbc, conv_pre, proj, cw_full, nb, s, off_xbc)
    g_wmain = _matmul(h, dproj, "tn", BF16, "dw_in", 512, 1024, t)
    g_wdt = _matmul(h, ddt, "tn", BF16, "dw_in_dt", 512, HEAD_PAD, t)

    def dest_cols(dev):
        lo, hi = dev * chunk, (dev + 1) * chunk
        srcs = [((olo, ohi), g_wmain, off) for (olo, ohi), off in segments] + [((o_dt, o_u), g_wdt, 0)]
        out = []
        for (olo, ohi), arr, off in sorted(srcs, key=lambda e: e[0][0]):
            a, b = max(lo, olo), min(hi, ohi)
            if a < b:
                out.append(arr[:, off + a - olo:off + b - olo])
        return jnp.concatenate(out, axis=1)

    names_b = ["w_in", "conv_w"]
    partials_b = pair_sums([jnp.stack([dest_cols(dev) for dev in range(N_DEV)], axis=0),
                            g_cw.reshape(CONV_W, N_DEV, n_conv // N_DEV).transpose(1, 0, 2)], names_b)
    grad_x, d_ng, *parts_b = _in_bwd(dproj, w_main, ddt, w_dt, x2, norm_g, dx1, partials_b)

    names = names_b + names_a
    parts = parts_b + parts_a
    sh_g, sh_d, sh_m, sh_v = [], [], [], []
    shapes = [w_in.shape, conv_w.shape, pool_mix_w.shape, w_branch_a.shape, w_branch_b.shape, w_out.shape,
              w_ple_gate.shape, w_ple_up.shape]
    for pt, w, m, v, nm, shp in zip(parts, sharded, sharded_m, sharded_v, names, shapes):
        res = _adamw(pt, w, m, v, "adamw_" + nm)
        for dst, r in zip((sh_g, sh_d, sh_m, sh_v), res):
            dst.append(r.reshape(shp))

    rep_w = [norm_g, conv_b, gnorm_g, pool_mix_b, pool_scale, ple_norm_g, final_g.reshape(1, d),
             bias_p, alog_p, _pad_lanes(d_skip)]
    rep_m = [m_norm_g, m_conv_b, m_gnorm_g, m_pool_mix_b, m_pool_scale, m_ple_norm_g, m_final_g.reshape(1, d),
             _pad_lanes(m_dt_bias), _pad_lanes(m_a_log), _pad_lanes(m_d_skip)]
    rep_v = [v_norm_g, v_conv_b, v_gnorm_g, v_pool_mix_b, v_pool_scale, v_ple_norm_g, v_final_g.reshape(1, d),
             _pad_lanes(v_dt_bias), _pad_lanes(v_a_log), _pad_lanes(v_d_skip)]
    rep_g = [d_ng, d_cb, d_gn, d_mixb, d_scale, d_ple_g, dfinal_g, d_dtb, d_alog, d_dsk]
    zero = jnp.zeros((1, LANES), F32)
    small = _all_gather_direct(jnp.concatenate(rep_g + [loss_part], axis=1), "small_all_gather")
    pk = lambda ws: jnp.concatenate(ws + [zero], axis=1)
    rg, rd, rm, rv = _adamw(small, pk(rep_w), pk(rep_m), pk(rep_v), "adamw_replicated")
    loss = rg[0, sum(int(w.size) for w in rep_w)]

    def unpack_rep(buf):
        out, pos_ = [], 0
        for w in rep_w:
            out.append(buf[:, pos_:pos_ + w.size])
            pos_ += w.size
        return out

    def arrange(sh, rep):
        rep = list(rep)
        rep[6] = rep[6].reshape(d)
        for i in (7, 8, 9):
            rep[i] = rep[i][:, :n_heads]
        return [rep[0], sh[0], sh[1], rep[1], rep[7], rep[8], rep[9], rep[2], sh[2], rep[3], rep[4],
                sh[3], sh[4], sh[5], rep[5], sh[6], sh[7], rep[6]]

    outs = [loss, grad_x.reshape(nb, s, d)]
    for sh, rep in ((sh_g, rg), (sh_d, rd), (sh_m, rm), (sh_v, rv)):
        outs += arrange(sh, unpack_rep(rep))
    return tuple(outs)
```

```python
import functools

import jax
import jax.numpy as jnp
from jax import lax
from jax.experimental import pallas as pl
from jax.experimental.pallas import tpu as pltpu

F32 = jnp.float32
BF16 = jnp.bfloat16
I32 = jnp.int32
MESH = pl.DeviceIdType.MESH
ANY = pl.BlockSpec(memory_space=pl.ANY)

EPS = 1e-6
CHUNK = 128
HEAD_DIM = 64
D_STATE = 128
GROUPS = 4
CONV_W = 4
POOL_WINDOWS = (2, 4, 8, 16)
ACT = jnp.bfloat16
LANES = 128
SUBLANES = 8
CONV_ROWS = 64
HEAD_PAD = 128
NEG = -1e30
N_DEV = 8

ADAM_LR = 0.001
ADAM_B1 = 0.9
ADAM_B2 = 0.999
ADAM_EPS = 1e-08
ADAM_WD = 0.01
ADAM_STEP = 10

NN = (((1,), (0,)), ((), ()))
NT = (((1,), (1,)), ((), ()))
TN = (((0,), (0,)), ((), ()))


def _pallas(body, **kw):
    return pl.pallas_call(body, **kw)


def _cparams(dims=None, vmem_mb=None):
    kw = {}
    if dims is not None:
        kw["dimension_semantics"] = dims
    if vmem_mb is not None:
        kw["vmem_limit_bytes"] = vmem_mb << 20
    return pltpu.CompilerParams(**kw)


def _dot(a, b, dims=NN):
    return lax.dot_general(a, b, dims, preferred_element_type=F32)


def _split3(v):
    hi = v.astype(BF16)
    r = v - hi.astype(F32)
    mid = r.astype(BF16)
    lo = (r - mid.astype(F32)).astype(BF16)
    return hi, mid, lo


def _exact_r(v, m, dims=NN):
    hi, mid, lo = _split3(v)
    return _dot(hi, m, dims) + _dot(mid, m, dims) + _dot(lo, m, dims)


def _exact_l(m, v, dims=NN):
    hi, mid, lo = _split3(v)
    return _dot(m, hi, dims) + _dot(m, mid, dims) + _dot(m, lo, dims)


def _row_tile(rows, target):
    best = None
    for cand in range(8, min(rows, target) + 1, 8):
        if rows % cand == 0:
            best = cand
    assert best is not None, rows
    return best


def _sum_all(v):
    return jnp.sum(jnp.sum(v, axis=1, keepdims=True), axis=0, keepdims=True)


def _iota(shape, dim):
    return lax.broadcasted_iota(I32, shape, dim)


def _sigmoid(x):
    return 0.5 * jnp.tanh(0.5 * x) + 0.5


def _softplus(x):
    return jnp.maximum(x, 0.0) + jnp.log(1.0 + jnp.exp(-jnp.abs(x)))


def _shift_down(v, j, row):
    return jnp.where(row >= j, pltpu.roll(v, j, 0), 0.0)


def _shift_up(v, j, row):
    n = v.shape[0]
    return jnp.where(row < n - j, pltpu.roll(v, n - j, 0), 0.0)


def _matmul(a, b, mode, out_dtype, name, tm, tn, tk):
    if mode == "nn":
        (m, k), n = a.shape, b.shape[1]
    elif mode == "nt":
        (m, k), n = a.shape, b.shape[0]
    else:
        (k, m), n = a.shape, b.shape[1]
    tm, tn, tk = min(tm, m), min(tn, n), min(tk, k)
    assert m % tm == 0 and n % tn == 0 and k % tk == 0, (name, m, n, k)
    nk = k // tk
    dims = {"nn": NN, "nt": NT, "tn": TN}[mode]

    def body_acc(a_ref, b_ref, o_ref, acc_ref):
        kk = pl.program_id(2)

        @pl.when(kk == 0)
        def _():
            acc_ref[...] = jnp.zeros_like(acc_ref)

        acc_ref[...] += _dot(a_ref[...].astype(BF16), b_ref[...].astype(BF16), dims)

        @pl.when(kk == nk - 1)
        def _():
            o_ref[...] = acc_ref[...].astype(o_ref.dtype)

    def body_one(a_ref, b_ref, o_ref):
        o_ref[...] = _dot(a_ref[...].astype(BF16), b_ref[...].astype(BF16), dims).astype(o_ref.dtype)

    body = body_one if nk == 1 else body_acc
    if mode == "tn":
        a_spec = pl.BlockSpec((tk, tm), lambda i, j, kk: (kk, i))
    else:
        a_spec = pl.BlockSpec((tm, tk), lambda i, j, kk: (i, kk))
    if mode == "nt":
        b_spec = pl.BlockSpec((tn, tk), lambda i, j, kk: (j, kk))
    else:
        b_spec = pl.BlockSpec((tk, tn), lambda i, j, kk: (kk, j))
    return _pallas(
        body, name=name, grid=(m // tm, n // tn, nk),
        in_specs=[a_spec, b_spec],
        out_specs=pl.BlockSpec((tm, tn), lambda i, j, kk: (i, j)),
        out_shape=jax.ShapeDtypeStruct((m, n), out_dtype),
        scratch_shapes=[] if nk == 1 else [pltpu.VMEM((tm, tn), F32)],
        compiler_params=_cparams(("parallel", "parallel", "arbitrary"), 56),
    )(a, b)


def _in_proj(h, w_main, blocks):
    t, d = h.shape
    n = w_main.shape[1]
    tm, tn = min(1024, t), 1024
    ni, nj = t // tm, n // tn
    gather = _TwoLevelGather(blocks)

    def body(*refs):
        a_ref, b_ref = refs[:2]
        x_refs = refs[2:2 + gather.n]
        o_ref = refs[2 + gather.n]
        got_refs = refs[3 + gather.n:3 + 2 * gather.n]
        sems = refs[3 + 2 * gather.n:]
        i, j = pl.program_id(0), pl.program_id(1)

        @pl.when((i == 0) & (j == 0))
        def _():
            gather.start(x_refs, got_refs, sems)

        o_ref[...] = _dot(a_ref[...], b_ref[...]).astype(o_ref.dtype)

        @pl.when((i == ni - 1) & (j == nj - 1))
        def _():
            gather.finish(x_refs, got_refs, sems)

    return _pallas(
        body, name="in_proj", grid=(ni, nj),
        in_specs=[pl.BlockSpec((tm, d), lambda i, j: (i, 0)), pl.BlockSpec((d, tn), lambda i, j: (0, j))]
        + gather.in_specs,
        out_specs=[pl.BlockSpec((tm, tn), lambda i, j: (i, j))] + gather.out_specs,
        out_shape=[jax.ShapeDtypeStruct((t, n), ACT)] + gather.out_shape,
        scratch_shapes=gather.scratch,
        compiler_params=_cparams(("arbitrary", "arbitrary"), 56),
    )(h, w_main, *blocks)


def _rms_fwd(x, g, name):
    t, d = x.shape
    tt = 512

    def body(x_ref, g_ref, h_ref):
        xv = x_ref[...]
        r = lax.rsqrt(jnp.mean(xv * xv, axis=-1, keepdims=True) + EPS)
        h_ref[...] = (xv * r * g_ref[...]).astype(h_ref.dtype)

    return _pallas(
        body, name=name, grid=(t // tt,),
        in_specs=[pl.BlockSpec((tt, d), lambda i: (i, 0)), pl.BlockSpec((1, d), lambda i: (0, 0))],
        out_specs=pl.BlockSpec((tt, d), lambda i: (i, 0)),
        out_shape=jax.ShapeDtypeStruct((t, d), BF16),
        compiler_params=_cparams(("parallel",)),
    )(x, g)


def _conv_fwd(proj, conv_w, conv_b, nb, s, off_xbc, n_conv):
    cw = 256
    rc = CONV_ROWS

    def body(x_ref, w_ref, b_ref, o_ref, pre_ref, pad_ref):
        pad_ref[0:SUBLANES, :] = jnp.zeros((SUBLANES, cw), F32)
        pad_ref[SUBLANES:, :] = x_ref[...].astype(F32)
        taps = [w_ref[k:k + 1, :] for k in range(CONV_W)]
        bias = b_ref[...]
        for r in range(s // rc):
            win = pad_ref[pl.ds(r * rc, rc + SUBLANES), :]
            acc = bias + win[SUBLANES:, :] * taps[CONV_W - 1]
            for j in range(1, CONV_W):
                acc = acc + pltpu.roll(win, j, 0)[SUBLANES:, :] * taps[CONV_W - 1 - j]
            pre_ref[pl.ds(r * rc, rc), :] = acc.astype(pre_ref.dtype)
            o_ref[pl.ds(r * rc, rc), :] = (acc * _sigmoid(acc)).astype(o_ref.dtype)

    out = pl.BlockSpec((s, cw), lambda b, j: (b, j))
    return _pallas(
        body, name="conv_fwd", grid=(nb, n_conv // cw),
        in_specs=[pl.BlockSpec((s, cw), lambda b, j: (b, off_xbc // cw + j)),
                  pl.BlockSpec((CONV_W, cw), lambda b, j: (0, j)),
                  pl.BlockSpec((1, cw), lambda b, j: (0, j))],
        out_specs=[out, out],
        out_shape=[jax.ShapeDtypeStruct((nb * s, n_conv), ACT), jax.ShapeDtypeStruct((nb * s, n_conv), ACT)],
        scratch_shapes=[pltpu.VMEM((s + SUBLANES, cw), F32)],
        compiler_params=_cparams(("parallel", "parallel"), 48),
    )(proj, conv_w, conv_b)


def _ssd_common(dtraw, bias, alog, n_heads):
    l = dtraw.shape[0]
    lane = _iota((l, HEAD_PAD), 1)
    lane1 = _iota((1, HEAD_PAD), 1)
    apad = jnp.where(lane1 < n_heads, -jnp.exp(alog), 0.0)
    dt = jnp.where(lane < n_heads, _softplus(dtraw + bias), 0.0)
    a = dt * apad
    tri = (_iota((l, l), 1) <= _iota((l, l), 0)).astype(BF16)
    acs = _exact_l(tri, a)
    width = n_heads * HEAD_DIM
    ex = (_iota((HEAD_PAD, width), 1) // HEAD_DIM == _iota((HEAD_PAD, width), 0)).astype(BF16)
    dt_full = _exact_r(dt, ex)
    acs_full = _exact_r(acs, ex)
    return dt, apad, acs, dt_full, acs_full, tri, ex


def _decay_mats(acs, acs_t, h, mask):
    col = acs[:, h:h + 1]
    row = acs_t[h:h + 1, :]
    return jnp.exp(jnp.where(mask, col - row, NEG))


def _ssd_fwd(xbc, proj, dtraw, dt_bias, a_log, dskip_full, gnorm, nb, s, d_inner):
    l = CHUNK
    nc = s // l
    n_heads = d_inner // HEAD_DIM
    gw = d_inner // GROUPS
    pairs_per_group = gw // LANES
    gn_cols = GROUPS * D_STATE

    def body(xs_ref, bm_ref, cm_ref, z_ref, dtr_ref, bias_ref, alog_ref, dsk_ref, gn_ref,
             y_ref, ya3_ref, prev_ref, state_ref):
        @pl.when(pl.program_id(1) == 0)
        def _():
            state_ref[...] = jnp.zeros_like(state_ref)

        xs = xs_ref[...].astype(F32)
        dt, apad, acs, dt_full, acs_full, tri, ex = _ssd_common(
            dtr_ref[...], bias_ref[...], alog_ref[...], n_heads)
        acs_t = acs.T
        e_full = jnp.exp(acs_full)
        last = acs_full[l - 1:l, :]
        dte = jnp.exp(last - acs_full)
        cd = jnp.exp(last)
        xdt = xs * dt_full
        xd = xdt * dte
        mask = _iota((l, l), 1) <= _iota((l, l), 0)
        lane = _iota((l, LANES), 1)
        prev_ref[0] = state_ref[...]
        for g in range(GROUPS):
            gsl = slice(g * gw, (g + 1) * gw)
            bg = bm_ref[:, g * D_STATE:(g + 1) * D_STATE].astype(BF16)
            cg = cm_ref[:, g * D_STATE:(g + 1) * D_STATE].astype(BF16)
            cb = _dot(cg, bg, NT)
            outs = []
            for q in range(pairs_per_group):
                pair = g * pairs_per_group + q
                h0, h1 = 2 * pair, 2 * pair + 1
                xpb = xdt[:, pair * LANES:(pair + 1) * LANES].astype(BF16)
                m0 = (cb * _decay_mats(acs, acs_t, h0, mask)).astype(BF16)
                m1 = (cb * _decay_mats(acs, acs_t, h1, mask)).astype(BF16)
                zero = jnp.zeros_like(xpb)
                rhs = jnp.concatenate([jnp.where(lane < HEAD_DIM, xpb, zero),
                                       jnp.where(lane >= HEAD_DIM, xpb, zero)], axis=0)
                outs.append(_dot(jnp.concatenate([m0, m1], axis=1), rhs))
            prev_g = state_ref[:, gsl]
            y_off = _dot(cg, prev_g.astype(BF16)) * e_full[:, gsl]
            y_g = jnp.concatenate(outs, axis=1) + y_off
            y_ref[:, gsl] = y_g.astype(y_ref.dtype)
            st = _dot(bg, xd[:, gsl].astype(BF16), TN)
            state_ref[:, gsl] = prev_g * cd[:, gsl] + st
            zv = z_ref[:, gsl].astype(F32)
            blk = (y_g + xs[:, gsl] * dsk_ref[:, gsl]) * (zv * _sigmoid(zv))
            rg = lax.rsqrt(jnp.mean(blk * blk, axis=-1, keepdims=True) + EPS)
            ya3_ref[:, gsl] = (blk * rg * gn_ref[:, gsl]).astype(ya3_ref.dtype)

    row = lambda b, c: b * nc + c
    small = lambda n: pl.BlockSpec((1, n), lambda b, c: (0, 0))
    return _pallas(
        body, name="ssd_fwd", grid=(nb, nc),
        in_specs=[pl.BlockSpec((l, d_inner), lambda b, c: (row(b, c), 0)),
                  pl.BlockSpec((l, gn_cols), lambda b, c: (row(b, c), d_inner // gn_cols)),
                  pl.BlockSpec((l, gn_cols), lambda b, c: (row(b, c), d_inner // gn_cols + 1)),
                  pl.BlockSpec((l, d_inner), lambda b, c: (row(b, c), 0)),
                  pl.BlockSpec((l, HEAD_PAD), lambda b, c: (row(b, c), 0)),
                  small(HEAD_PAD), small(HEAD_PAD), small(d_inner), small(d_inner)],
        out_specs=[pl.BlockSpec((l, d_inner), lambda b, c: (row(b, c), 0)),
                   pl.BlockSpec((l, d_inner), lambda b, c: (row(b, c), 0)),
                   pl.BlockSpec((1, D_STATE, d_inner), lambda b, c: (row(b, c), 0, 0))],
        out_shape=[jax.ShapeDtypeStruct((nb * s, d_inner), ACT),
                   jax.ShapeDtypeStruct((nb * s, d_inner), BF16),
                   jax.ShapeDtypeStruct((nb * nc, D_STATE, d_inner), F32)],
        scratch_shapes=[pltpu.VMEM((D_STATE, d_inner), F32)],
        compiler_params=_cparams(("parallel", "arbitrary"), 48),
    )(xbc, xbc, xbc, proj, dtraw, dt_bias, a_log, dskip_full, gnorm)


def _pool_fwd(proj, nb, s, off_u, d_pool):
    pg = d_pool // len(POOL_WINDOWS)

    def body(u_ref, o_ref):
        g = pl.program_id(1)
        uv = u_ref[...].astype(F32)
        row = _iota(uv.shape, 0)
        pos = (_iota((uv.shape[0], 1), 0) + 1).astype(F32)
        for gi, w in enumerate(POOL_WINDOWS):
            @pl.when(g == gi)
            def _():
                acc, span = uv, 1
                while span < w:
                    acc = acc + _shift_down(acc, span, row)
                    span *= 2
                o_ref[...] = (acc / jnp.minimum(pos, float(w)) - uv).astype(o_ref.dtype)

    return _pallas(
        body, name="pool_fwd", grid=(nb, len(POOL_WINDOWS)),
        in_specs=[pl.BlockSpec((s, pg), lambda b, g: (b, off_u // pg + g))],
        out_specs=pl.BlockSpec((s, pg), lambda b, g: (b, g)),
        out_shape=jax.ShapeDtypeStruct((nb * s, d_pool), BF16),
        compiler_params=_cparams(("parallel", "parallel"), 48),
    )(proj)


def _mix_fwd(pooled, mixw, proj, mix_b, scale, off_zp):
    t, d_pool = pooled.shape
    ng, pg = mixw.shape[0], mixw.shape[1]
    tm = min(1024, t)

    def body(a_ref, w_ref, zp_ref, b_ref, s_ref, mixed_ref, yb_ref):
        mixed = _dot(a_ref[...], w_ref[0])
        mixed_ref[...] = mixed.astype(mixed_ref.dtype)
        zp = zp_ref[...].astype(F32)
        yb_ref[...] = ((mixed + b_ref[...]) * s_ref[...] * (zp * _sigmoid(zp))).astype(yb_ref.dtype)

    blk = lambda off: pl.BlockSpec((tm, pg), lambda i, g: (i, off + g))
    vec = pl.BlockSpec((1, pg), lambda i, g: (0, g))
    return _pallas(
        body, name="mix_fwd", grid=(t // tm, ng),
        in_specs=[blk(0), pl.BlockSpec((1, pg, pg), lambda i, g: (g, 0, 0)), blk(off_zp // pg), vec, vec],
        out_specs=[blk(0), blk(0)],
        out_shape=[jax.ShapeDtypeStruct((t, d_pool), ACT), jax.ShapeDtypeStruct((t, d_pool), BF16)],
        compiler_params=_cparams(("parallel", "parallel")),
    )(pooled, mixw, proj, mix_b, scale)


def _merge_to_grads(ya3, wa, yb, wb, proj, off_ga, off_gb, wout, x, ple_g, wg, p, wup, target, final_g, n_main):
    t, d = x.shape
    pd = p.shape[1]
    tm = 256
    assert off_gb == off_ga + d and off_ga % (2 * d) == 0

    def body(a_ref, wa_ref, b_ref, wb_ref, ga_ref, gb_ref, wout_ref, x_ref, pg_ref, wg_ref, p_ref, wup_ref,
             tg_ref, fg_ref,
             mg_ref, h2_ref, dx1_ref, dx1b_ref, dgl_ref, dpu_ref, dpa_ref, dpb_ref, dgate_ref,
             loss_ref, dfg_ref, dpg_ref):
        @pl.when(pl.program_id(0) == 0)
        def _():
            loss_ref[...] = jnp.zeros_like(loss_ref)
            dfg_ref[...] = jnp.zeros_like(dfg_ref)
            dpg_ref[...] = jnp.zeros_like(dpg_ref)

        pa = _dot(a_ref[...], wa_ref[...])
        pb = _dot(b_ref[...], wb_ref[...])
        sa = _sigmoid(ga_ref[...].astype(F32))
        sb = _sigmoid(gb_ref[...].astype(F32))
        merged = (sa * pa + sb * pb).astype(BF16)
        mg_ref[...] = merged
        x1 = x_ref[...] + _dot(merged, wout_ref[...])
        r1 = lax.rsqrt(jnp.mean(x1 * x1, axis=-1, keepdims=True) + EPS)
        xh1 = x1 * r1
        pg = pg_ref[...]
        h2 = (xh1 * pg).astype(BF16)
        h2_ref[...] = h2
        gate = _sigmoid(_dot(h2, wg_ref[...]))
        pu = _dot(p_ref[...].astype(BF16), wup_ref[...])
        x2 = x1 + gate * pu
        r2 = lax.rsqrt(jnp.mean(x2 * x2, axis=-1, keepdims=True) + EPS)
        xh2 = x2 * r2
        gfin = fg_ref[...]
        err = xh2 * gfin - tg_ref[...]
        loss_ref[...] += _sum_all(err * err) * (0.5 / d)
        dy = err * (1.0 / d)
        dfg_ref[...] += jnp.sum(dy * xh2, axis=0, keepdims=True)
        dg = dy * gfin
        dx2 = r2 * (dg - xh2 * jnp.mean(dg * xh2, axis=-1, keepdims=True))
        dgl = (dx2 * pu * gate * (1.0 - gate)).astype(BF16)
        dgl_ref[...] = dgl
        dpu_ref[...] = (dx2 * gate).astype(dpu_ref.dtype)
        dh2 = _dot(dgl, wg_ref[...], NT)
        dpg_ref[...] += jnp.sum(dh2 * xh1, axis=0, keepdims=True)
        dg = dh2 * pg
        dx1 = dx2 + r1 * (dg - xh1 * jnp.mean(dg * xh1, axis=-1, keepdims=True))
        dx1_ref[...] = dx1
        dx1b = dx1.astype(BF16)
        dx1b_ref[...] = dx1b
        dm = _dot(dx1b, wout_ref[...], NT)
        dpa_ref[...] = (dm * sa).astype(dpa_ref.dtype)
        dpb_ref[...] = (dm * sb).astype(dpb_ref.dtype)
        dgate_ref[:, :d] = (dm * pa * sa * (1.0 - sa)).astype(dgate_ref.dtype)
        dgate_ref[:, d:] = (dm * pb * sb * (1.0 - sb)).astype(dgate_ref.dtype)

    rows = lambda n: pl.BlockSpec((tm, n), lambda i: (i, 0))
    resident = lambda a: pl.BlockSpec(a.shape, lambda i: (0, 0), pipeline_mode=pl.Buffered(1))
    vec = lambda n: pl.BlockSpec((1, n), lambda i: (0, 0))
    bf = lambda: jax.ShapeDtypeStruct((t, d), BF16)
    return _pallas(
        body, name="merge_to_grads", grid=(t // tm,),
        in_specs=[rows(ya3.shape[1]), resident(wa), rows(yb.shape[1]), resident(wb),
                  pl.BlockSpec((tm, d), lambda i: (i, off_ga // d)), pl.BlockSpec((tm, d), lambda i: (i, off_gb // d)),
                  resident(wout), rows(d), vec(d), resident(wg), rows(pd), resident(wup), rows(d), vec(d)],
        out_specs=[rows(d)] * 8 + [pl.BlockSpec((tm, 2 * d), lambda i: (i, off_ga // (2 * d))),
                                   vec(LANES), vec(d), vec(d)],
        out_shape=[bf(), bf(), jax.ShapeDtypeStruct((t, d), F32), bf(), bf(), bf(), bf(), bf(),
                   jax.ShapeDtypeStruct((t, n_main), BF16), jax.ShapeDtypeStruct((1, LANES), F32),
                   jax.ShapeDtypeStruct((1, d), F32), jax.ShapeDtypeStruct((1, d), F32)],
        compiler_params=_cparams(("arbitrary",), 60),
    )(ya3, wa, yb, wb, proj, proj, wout, x, ple_g, wg, p, wup, target, final_g)


def _yb_bwd(dproj, dpb, wb, proj, mixed, mix_b, scale, off_zp):
    t, dp = mixed.shape
    tm = 512

    def body(_, dpb_ref, w_ref, zp_ref, mx_ref, b_ref, s_ref, dzp_ref, dmx_ref, dsc_ref, db_ref):
        @pl.when(pl.program_id(0) == 0)
        def _():
            dsc_ref[...] = jnp.zeros_like(dsc_ref)
            db_ref[...] = jnp.zeros_like(db_ref)

        dyb = _dot(dpb_ref[...], w_ref[...], NT)
        zp = zp_ref[...].astype(F32)
        sg = _sigmoid(zp)
        mb = mx_ref[...].astype(F32) + b_ref[...]
        yb0 = mb * s_ref[...]
        dyb0 = dyb * (zp * sg)
        dzp_ref[...] = (dyb * yb0 * (sg * (1.0 + zp * (1.0 - sg)))).astype(dzp_ref.dtype)
        dsc_ref[...] += jnp.sum(dyb0 * mb, axis=0, keepdims=True)
        dmx = dyb0 * s_ref[...]
        db_ref[...] += jnp.sum(dmx, axis=0, keepdims=True)
        dmx_ref[...] = dmx.astype(dmx_ref.dtype)

    rowblk = pl.BlockSpec((tm, dp), lambda i: (i, 0))
    vec = pl.BlockSpec((1, dp), lambda i: (0, 0))
    seg = pl.BlockSpec((tm, dp), lambda i: (i, off_zp // dp))
    return _pallas(
        body, name="yb_bwd", grid=(t // tm,),
        in_specs=[ANY, pl.BlockSpec((tm, wb.shape[1]), lambda i: (i, 0)),
                  pl.BlockSpec(wb.shape, lambda i: (0, 0)), seg, rowblk, vec, vec],
        out_specs=[seg, rowblk, vec, vec],
        out_shape=[jax.ShapeDtypeStruct(dproj.shape, dproj.dtype), jax.ShapeDtypeStruct((t, dp), BF16),
                   jax.ShapeDtypeStruct((1, dp), F32), jax.ShapeDtypeStruct((1, dp), F32)],
        input_output_aliases={0: 0},
        compiler_params=_cparams(("arbitrary",), 56),
    )(dproj, dpb, wb, proj, mixed, mix_b, scale)


def _dpooled(dmixed, mixw):
    t, dp = dmixed.shape
    ng, pg = mixw.shape[0], mixw.shape[1]
    tm = min(1024, t)

    def body(a_ref, w_ref, o_ref):
        o_ref[...] = _dot(a_ref[...], w_ref[0], NT).astype(o_ref.dtype)

    blk = pl.BlockSpec((tm, pg), lambda i, g: (i, g))
    return _pallas(
        body, name="dpooled", grid=(t // tm, ng),
        in_specs=[blk, pl.BlockSpec((1, pg, pg), lambda i, g: (g, 0, 0))],
        out_specs=blk, out_shape=jax.ShapeDtypeStruct((t, dp), ACT),
        compiler_params=_cparams(("parallel", "parallel")),
    )(dmixed, mixw)


def _dmixw(pooled, dmixed, ng):
    t, dp = pooled.shape
    pg = dp // ng
    tk = min(2048, t)
    nk = t // tk

    def body(a_ref, b_ref, o_ref):
        @pl.when(pl.program_id(1) == 0)
        def _():
            o_ref[...] = jnp.zeros_like(o_ref)

        o_ref[0] += _dot(a_ref[...], b_ref[...], TN)

    blk = pl.BlockSpec((tk, pg), lambda g, k: (k, g))
    return _pallas(
        body, name="dmixw", grid=(ng, nk),
        in_specs=[blk, blk],
        out_specs=pl.BlockSpec((1, pg, pg), lambda g, k: (g, 0, 0)),
        out_shape=jax.ShapeDtypeStruct((ng, pg, pg), F32),
        compiler_params=_cparams(("parallel", "arbitrary")),
    )(pooled, dmixed)


def _pool_bwd(dproj, dpooled, nb, s, off_u):
    dp = dpooled.shape[1]
    pg = dp // len(POOL_WINDOWS)

    def body(_, d_ref, o_ref):
        g = pl.program_id(1)
        dv = d_ref[...].astype(F32)
        row = _iota(dv.shape, 0)
        pos = (_iota((dv.shape[0], 1), 0) + 1).astype(F32)
        for gi, w in enumerate(POOL_WINDOWS):
            @pl.when(g == gi)
            def _():
                acc, span = dv / jnp.minimum(pos, float(w)), 1
                while span < w:
                    acc = acc + _shift_up(acc, span, row)
                    span *= 2
                o_ref[...] = (acc - dv).astype(o_ref.dtype)

    return _pallas(
        body, name="pool_bwd", grid=(nb, len(POOL_WINDOWS)),
        in_specs=[ANY, pl.BlockSpec((s, pg), lambda b, g: (b, g))],
        out_specs=pl.BlockSpec((s, pg), lambda b, g: (b, off_u // pg + g)),
        out_shape=jax.ShapeDtypeStruct(dproj.shape, dproj.dtype),
        input_output_aliases={0: 0},
        compiler_params=_cparams(("parallel", "parallel"), 48),
    )(dproj, dpooled)


def _ssd_bwd(dproj, dya3, y, xbc, proj, dtraw, prev, dt_bias, a_log, dskip_full, gnorm, nb, s, d_inner, partials):
    l = CHUNK
    nc = s // l
    n_heads = d_inner // HEAD_DIM
    gw = d_inner // GROUPS
    pairs_per_group = gw // LANES
    gn_cols = GROUPS * D_STATE
    n_conv = d_inner + 2 * gn_cols
    last_step = nb * nc - 1

    xch = _DirectExchange(partials)
    n_in, n_out = 13, 7

    def body(*refs):
        (_, dya3_ref, y_ref, xs_ref, bm_ref, cm_ref, z_ref, dtr_ref, prev_ref, bias_ref, alog_ref,
         dsk_ref, gn_ref) = refs[:n_in]
        p_refs = refs[n_in:n_in + xch.n]
        outs = refs[n_in + xch.n:]
        dz_ref, dxbc_ref, ddt_ref, dgn_ref, dskh_ref, dalog_ref, dbias_ref = outs[:n_out]
        got_refs = outs[n_out:n_out + xch.n]
        g_ref, dskacc_ref = outs[n_out + xch.n:n_out + xch.n + 2]
        sems = outs[n_out + xch.n + 2:]
        step = pl.program_id(0) * nc + pl.program_id(1)

        @pl.when(pl.program_id(1) == 0)
        def _():
            g_ref[...] = jnp.zeros_like(g_ref)

        @pl.when(step == 0)
        def _():
            xch.start(p_refs, got_refs, sems)
            dgn_ref[...] = jnp.zeros_like(dgn_ref)
            dalog_ref[...] = jnp.zeros_like(dalog_ref)
            dbias_ref[...] = jnp.zeros_like(dbias_ref)
            dskacc_ref[...] = jnp.zeros_like(dskacc_ref)

        xs = xs_ref[...].astype(F32)
        zv = z_ref[...].astype(F32)
        dsk = dsk_ref[...]
        sg = _sigmoid(zv)
        sz = zv * sg
        ya1 = y_ref[...].astype(F32) + xs * dsk
        ya2 = ya1 * sz
        parts = []
        for g in range(GROUPS):
            gsl = slice(g * gw, (g + 1) * gw)
            blk = ya2[:, gsl]
            rg = lax.rsqrt(jnp.mean(blk * blk, axis=-1, keepdims=True) + EPS)
            xh = blk * rg
            d3 = dya3_ref[:, gsl].astype(F32)
            dgn_ref[:, gsl] += jnp.sum(d3 * xh, axis=0, keepdims=True)
            dg = d3 * gn_ref[:, gsl]
            parts.append(rg * (dg - xh * jnp.mean(dg * xh, axis=-1, keepdims=True)))
        dya2 = jnp.concatenate(parts, axis=1)
        dy = dya2 * sz
        dz_ref[...] = (dya2 * ya1 * (sg * (1.0 + zv * (1.0 - sg)))).astype(dz_ref.dtype)
        dskacc_ref[...] += jnp.sum(dy * xs, axis=0, keepdims=True)

        dtraw_v = dtr_ref[...]
        bias = bias_ref[...]
        dt, apad, acs, dt_full, acs_full, tri, ex = _ssd_common(dtraw_v, bias, alog_ref[...], n_heads)
        acs_t = acs.T
        e_full = jnp.exp(acs_full)
        last = acs_full[l - 1:l, :]
        dte = jnp.exp(last - acs_full)
        cd = jnp.exp(last)
        xdt = xs * dt_full
        xd = xdt * dte
        dy_e = dy * e_full
        mask = _iota((l, l), 1) <= _iota((l, l), 0)
        lane = _iota((l, LANES), 1)
        hlane = _iota((l, HEAD_PAD), 1)
        hsub = _iota((HEAD_PAD, l), 0)
        dacs = jnp.zeros((l, HEAD_PAD), F32)
        dacs_t = jnp.zeros((HEAD_PAD, l), F32)
        dacs_full, dlast, dxdt_all, db_all, dc_all = [], [], [], [], []
        for g in range(GROUPS):
            gsl = slice(g * gw, (g + 1) * gw)
            bg = bm_ref[:, g * D_STATE:(g + 1) * D_STATE].astype(BF16)
            cg = cm_ref[:, g * D_STATE:(g + 1) * D_STATE].astype(BF16)
            prev_g = prev_ref[0, :, gsl]
            prev_b = prev_g.astype(BF16)
            gst = g_ref[:, gsl]
            gst_b = gst.astype(BF16)
            dye_b = dy_e[:, gsl].astype(BF16)
            dc = _dot(dye_b, prev_b, NT)
            dprev = _dot(cg, dye_b, TN)
            y_off = _dot(cg, prev_b) * e_full[:, gsl]
            xd_g = xd[:, gsl]
            db = _dot(xd_g.astype(BF16), gst_b, NT)
            dxd = _dot(bg, gst_b)
            wgt = dxd * xd_g
            dacs_full.append(dy[:, gsl] * y_off - wgt)
            dlast.append(jnp.sum(wgt, axis=0, keepdims=True)
                         + jnp.sum(prev_g * gst, axis=0, keepdims=True) * cd[:, gsl])
            g_ref[:, gsl] = dprev + cd[:, gsl] * gst
            dxdt_g = dxd * dte[:, gsl]
            cb = _dot(cg, bg, NT)
            dcb = jnp.zeros((l, l), F32)
            for q in range(pairs_per_group):
                pair = g * pairs_per_group + q
                psl = slice(pair * LANES, (pair + 1) * LANES)
                xpb = xdt[:, psl].astype(BF16)
                dyp = dy[:, psl].astype(BF16)
                zero = jnp.zeros_like(dyp)
                halves = (jnp.where(lane < HEAD_DIM, dyp, zero), jnp.where(lane >= HEAD_DIM, dyp, zero))
                acc = dxdt_g[:, q * LANES:(q + 1) * LANES]
                for k, dyh in enumerate(halves):
                    h = 2 * pair + k
                    dk = _decay_mats(acs, acs_t, h, mask)
                    mm = cb * dk
                    dm = _dot(dyh, xpb, NT)
                    dcb = dcb + dm * dk
                    dseg = dm * mm
                    dacs = dacs + jnp.where(hlane == h, jnp.sum(dseg, axis=1, keepdims=True), 0.0)
                    dacs_t = dacs_t + jnp.where(hsub == h, jnp.sum(dseg, axis=0, keepdims=True), 0.0)
                    acc = acc + _dot(mm.astype(BF16), dyh, TN)
                dxdt_all.append(acc)
            dcb_b = dcb.astype(BF16)
            dc_all.append(dc + _dot(dcb_b, bg))
            db_all.append(db + _dot(dcb_b, cg, TN))

        dxdt = jnp.concatenate(dxdt_all, axis=1)
        dxbc_ref[:, :d_inner] = (dy * dsk + dxdt * dt_full).astype(dxbc_ref.dtype)
        dxbc_ref[:, d_inner:d_inner + gn_cols] = jnp.concatenate(db_all, axis=1).astype(dxbc_ref.dtype)
        dxbc_ref[:, d_inner + gn_cols:] = jnp.concatenate(dc_all, axis=1).astype(dxbc_ref.dtype)
        ddt = _exact_r(dxdt * xs, ex, NT)
        rowi = _iota((l, d_inner), 0)
        dacs_f = jnp.concatenate(dacs_full, axis=1) + jnp.where(rowi == l - 1, jnp.concatenate(dlast, axis=1), 0.0)
        dacs = dacs + _exact_r(dacs_f, ex, NT) - dacs_t.T
        da = _exact_l(tri, dacs, TN)
        ddt = ddt + da * apad
        dalog_ref[...] += jnp.sum(da * dt, axis=0, keepdims=True) * apad
        ddt_raw = ddt * _sigmoid(dtraw_v + bias)
        dbias_ref[...] += jnp.sum(ddt_raw, axis=0, keepdims=True)
        ddt_ref[...] = ddt_raw.astype(ddt_ref.dtype)

        @pl.when(step == last_step)
        def _():
            dskh_ref[...] = _exact_r(dskacc_ref[...], ex, NT)[0:1, :]
            xch.wait(p_refs, got_refs, sems)

    row = lambda b, c: b * nc + (nc - 1 - c)
    small = lambda n: pl.BlockSpec((1, n), lambda b, c: (0, 0))
    wide = pl.BlockSpec((l, d_inner), lambda b, c: (row(b, c), 0))
    return _pallas(
        body, name="ssd_bwd", grid=(nb, nc),
        in_specs=[ANY, wide, wide, wide,
                  pl.BlockSpec((l, gn_cols), lambda b, c: (row(b, c), d_inner // gn_cols)),
                  pl.BlockSpec((l, gn_cols), lambda b, c: (row(b, c), d_inner // gn_cols + 1)),
                  wide, pl.BlockSpec((l, HEAD_PAD), lambda b, c: (row(b, c), 0)),
                  pl.BlockSpec((1, D_STATE, d_inner), lambda b, c: (row(b, c), 0, 0)),
                  small(HEAD_PAD), small(HEAD_PAD), small(d_inner), small(d_inner)] + xch.in_specs,
        out_specs=[wide, pl.BlockSpec((l, n_conv), lambda b, c: (row(b, c), 0)),
                   pl.BlockSpec((l, HEAD_PAD), lambda b, c: (row(b, c), 0)),
                   small(d_inner), small(HEAD_PAD), small(HEAD_PAD), small(HEAD_PAD)] + xch.out_specs,
        out_shape=[jax.ShapeDtypeStruct(dproj.shape, dproj.dtype),
                   jax.ShapeDtypeStruct((nb * s, n_conv), ACT),
                   jax.ShapeDtypeStruct((nb * s, HEAD_PAD), BF16),
                   jax.ShapeDtypeStruct((1, d_inner), F32), jax.ShapeDtypeStruct((1, HEAD_PAD), F32),
                   jax.ShapeDtypeStruct((1, HEAD_PAD), F32), jax.ShapeDtypeStruct((1, HEAD_PAD), F32)]
        + xch.out_shape,
        scratch_shapes=[pltpu.VMEM((D_STATE, d_inner), F32), pltpu.VMEM((8, d_inner), F32)] + xch.scratch,
        input_output_aliases={0: 0},
        compiler_params=_cparams(("arbitrary", "arbitrary"), 56),
    )(dproj, dya3, y, xbc, xbc, xbc, proj, dtraw, prev, dt_bias, a_log, dskip_full, gnorm, *partials)


def _conv_bwd(dproj, dxbc, conv_pre, proj, conv_w, nb, s, off_xbc):
    n_conv = dxbc.shape[1]
    cw = 256

    rc = CONV_ROWS

    def fold(v):
        return jnp.sum(v.reshape(rc // SUBLANES, SUBLANES, cw), axis=0)

    def body(_, d_ref, pre_ref, x_ref, w_ref, dx_ref, dw_ref, db_ref, dcpad_ref):
        @pl.when(pl.program_id(1) == 0)
        def _():
            dw_ref[...] = jnp.zeros_like(dw_ref)
            db_ref[...] = jnp.zeros_like(db_ref)

        dcpad_ref[pl.ds(s, SUBLANES), :] = jnp.zeros((SUBLANES, cw), F32)
        taps = [w_ref[k:k + 1, :] for k in range(CONV_W)]
        db = jnp.zeros((SUBLANES, cw), F32)
        for r in range(s // rc):
            rows = pl.ds(r * rc, rc)
            acc = pre_ref[rows, :].astype(F32)
            sg = _sigmoid(acc)
            dc = d_ref[rows, :].astype(F32) * (sg * (1.0 + acc * (1.0 - sg)))
            dcpad_ref[rows, :] = dc
            db = db + fold(dc)
        db_ref[...] += jnp.sum(db, axis=0, keepdims=True)
        dw = [jnp.zeros((SUBLANES, cw), F32) for _ in range(CONV_W)]
        for r in range(s // rc):
            xv = x_ref[pl.ds(r * rc, rc), :].astype(F32)
            win = dcpad_ref[pl.ds(r * rc, rc + SUBLANES), :]
            dxv = jnp.zeros((rc, cw), F32)
            for j in range(CONV_W):
                up = win[:rc, :] if j == 0 else pltpu.roll(win, rc + SUBLANES - j, 0)[:rc, :]
                dxv = dxv + up * taps[CONV_W - 1 - j]
                dw[CONV_W - 1 - j] = dw[CONV_W - 1 - j] + fold(up * xv)
            dx_ref[pl.ds(r * rc, rc), :] = dxv.astype(dx_ref.dtype)
        for k in range(CONV_W):
            dw_ref[k:k + 1, :] += jnp.sum(dw[k], axis=0, keepdims=True)

    blk = pl.BlockSpec((s, cw), lambda j, b: (b, j))
    return _pallas(
        body, name="conv_bwd", grid=(n_conv // cw, nb),
        in_specs=[ANY, blk, blk, pl.BlockSpec((s, cw), lambda j, b: (b, off_xbc // cw + j)),
                  pl.BlockSpec((CONV_W, cw), lambda j, b: (0, j))],
        out_specs=[pl.BlockSpec((s, cw), lambda j, b: (b, off_xbc // cw + j)),
                   pl.BlockSpec((CONV_W, cw), lambda j, b: (0, j)), pl.BlockSpec((1, cw), lambda j, b: (0, j))],
        out_shape=[jax.ShapeDtypeStruct(dproj.shape, dproj.dtype),
                   jax.ShapeDtypeStruct((CONV_W, n_conv), F32), jax.ShapeDtypeStruct((1, n_conv), F32)],
        scratch_shapes=[pltpu.VMEM((s + SUBLANES, cw), F32)],
        input_output_aliases={0: 0},
        compiler_params=_cparams(("parallel", "arbitrary"), 48),
    )(dproj, dxbc, conv_pre, proj, conv_w)


def _in_bwd(dproj, w_main, ddt, w_dt, x, norm_g, dx1, partials):
    t, d = x.shape
    n_main = dproj.shape[1]
    tm = 256
    ni = t // tm
    xch = _ChipExchange(partials)
    n_in, n_out = 7, 2

    def body(*refs):
        dp_ref, w_ref, ddt_ref, wdt_ref, x_ref, g_ref, dx1_ref = refs[:n_in]
        p_refs = refs[n_in:n_in + xch.n]
        outs = refs[n_in + xch.n:]
        dx_ref, dg_ref = outs[:n_out]
        got_refs = outs[n_out:n_out + xch.n]
        sems = outs[n_out + xch.n:]
        i = pl.program_id(0)

        @pl.when(i == 0)
        def _():
            xch.start(p_refs, got_refs, sems)
            dg_ref[...] = jnp.zeros_like(dg_ref)

        dh = _dot(dp_ref[...], w_ref[...], NT) + _dot(ddt_ref[...], wdt_ref[...], NT)
        xv = x_ref[...]
        r = lax.rsqrt(jnp.mean(xv * xv, axis=-1, keepdims=True) + EPS)
        xh = xv * r
        dg_ref[...] += jnp.sum(dh * xh, axis=0, keepdims=True)
        dg = dh * g_ref[...]
        dx_ref[...] = dx1_ref[...] + r * (dg - xh * jnp.mean(dg * xh, axis=-1, keepdims=True))

        @pl.when(i == ni - 1)
        def _():
            xch.wait(p_refs, got_refs, sems)

    rowblk = pl.BlockSpec((tm, d), lambda i: (i, 0))
    vec = pl.BlockSpec((1, d), lambda i: (0, 0))
    resident = lambda shape: pl.BlockSpec(shape, lambda i: (0, 0), pipeline_mode=pl.Buffered(1))
    return _pallas(
        body, name="in_bwd", grid=(ni,),
        in_specs=[pl.BlockSpec((tm, n_main), lambda i: (i, 0)), resident((d, n_main)),
                  pl.BlockSpec((tm, HEAD_PAD), lambda i: (i, 0)), resident((d, HEAD_PAD)),
                  rowblk, vec, rowblk] + xch.in_specs,
        out_specs=[rowblk, vec] + xch.out_specs,
        out_shape=[jax.ShapeDtypeStruct((t, d), F32), jax.ShapeDtypeStruct((1, d), F32)] + xch.out_shape,
        scratch_shapes=xch.scratch,
        compiler_params=_cparams(("arbitrary",), 56),
    )(dproj, w_main, ddt, w_dt, x, norm_g, dx1, *partials)


def _adamw_math(w, g, m, v):
    m = ADAM_B1 * m + (1.0 - ADAM_B1) * g
    v = ADAM_B2 * v + (1.0 - ADAM_B2) * (g * g)
    m_hat = m / (1.0 - ADAM_B1 ** ADAM_STEP)
    v_hat = v / (1.0 - ADAM_B2 ** ADAM_STEP)
    delta = -ADAM_LR * (m_hat / (jnp.sqrt(v_hat) + ADAM_EPS) + ADAM_WD * w)
    return delta, m, v


def _adamw(parts, w, m, v, name):
    n, rows, cols = parts.shape
    tr = _row_tile(rows, 128) if rows % 8 == 0 else rows

    def body(p_ref, w_ref, m_ref, v_ref, g_ref, d_ref, mo_ref, vo_ref):
        g = p_ref[0].astype(F32)
        for k in range(1, n):
            g = g + p_ref[k].astype(F32)
        d, mn, vn = _adamw_math(w_ref[...], g, m_ref[...], v_ref[...])
        g_ref[...] = g
        d_ref[...] = d
        mo_ref[...] = mn
        vo_ref[...] = vn

    blk = pl.BlockSpec((tr, cols), lambda i: (i, 0))
    shp = jax.ShapeDtypeStruct((rows, cols), F32)
    return _pallas(
        body, name=name, grid=(rows // tr,),
        in_specs=[pl.BlockSpec((n, tr, cols), lambda i: (0, i, 0)), blk, blk, blk],
        out_specs=[blk, blk, blk, blk], out_shape=[shp, shp, shp, shp],
        compiler_params=_cparams(("parallel",), 56),
    )(parts, w, m, v)


def _coords():
    return lax.axis_index("x"), lax.axis_index("y"), lax.axis_index("c")


class _TwoLevelGather:
    def __init__(self, blocks):
        self.n = len(blocks)
        self.in_specs = [ANY] * self.n
        self.out_specs = [ANY] * self.n
        self.out_shape = [jax.ShapeDtypeStruct((N_DEV,) + b.shape, b.dtype) for b in blocks]
        self.scratch = [pltpu.SemaphoreType.DMA((self.n, 7)), pltpu.SemaphoreType.DMA((self.n, 7)),
                        pltpu.SemaphoreType.DMA((self.n,))]

    def _parts(self, x_refs, out_refs, sems):
        send_sems, recv_sems, local_sems = sems
        x, y, c = _coords()
        me, sibling = (x, y, c), (x, y, 1 - c)
        chips = [(1 - x, y), (x, 1 - y), (1 - x, 1 - y)]

        def slot(a, px, py, pc):
            return out_refs[a].at[4 * px + 2 * py + pc]

        def copy(a, k, block, to, src=None):
            return pltpu.make_async_remote_copy(
                src_ref=slot(a, *block) if src is None else src, dst_ref=slot(a, *block),
                send_sem=send_sems.at[a, k], recv_sem=recv_sems.at[a, k], device_id=to, device_id_type=MESH)

        local = [pltpu.make_async_copy(x_refs[a], slot(a, *me), local_sems.at[a]) for a in range(self.n)]
        first = []
        for a in range(self.n):
            first.append(copy(a, 0, me, sibling, src=x_refs[a]))
            first += [copy(a, 1 + j, me, (*chip, c), src=x_refs[a]) for j, chip in enumerate(chips)]
        return copy, local, first, chips, me, sibling, c

    def start(self, x_refs, out_refs, sems):
        _, local, first, *_ = self._parts(x_refs, out_refs, sems)
        for cp in local + first:
            cp.start()

    def finish(self, x_refs, out_refs, sems):
        copy, local, first, chips, me, sibling, c = self._parts(x_refs, out_refs, sems)
        passed = []
        for j, chip in enumerate(chips):
            for a in range(self.n):
                copy(a, 1 + j, (*chip, c), me).wait_recv()
                passed.append(copy(a, 4 + j, (*chip, c), sibling))
                passed[-1].start()
        for a in range(self.n):
            copy(a, 0, sibling, me).wait_recv()
            for j, chip in enumerate(chips):
                copy(a, 4 + j, (*chip, 1 - c), me).wait_recv()
        for cp in first + passed:
            cp.wait_send()
        for cp in local:
            cp.wait()


def _all_gather_routed(blk, name):
    rows, cols = blk.shape
    half = rows // 2
    assert half % 16 == 0

    def body(x_ref, out_ref, send_sems, recv_sems, local_sem):
        x, y, c = _coords()
        me, sib, xn, yn, dg = (x, y, c), (x, y, 1 - c), (1 - x, y, c), (x, 1 - y, c), (1 - x, 1 - y, c)
        other = lambda p: (p[0], p[1], 1 - c)
        top, bot = pl.ds(0, half), pl.ds(half, half)

        def slot(p, part=None):
            ref = out_ref.at[4 * p[0] + 2 * p[1] + p[2]]
            return ref if part is None else ref.at[part]

        def copy(k, block, to, part=None, src=None):
            return pltpu.make_async_remote_copy(
                src_ref=slot(block, part) if src is None else src, dst_ref=slot(block, part),
                send_sem=send_sems.at[k], recv_sem=recv_sems.at[k], device_id=to, device_id_type=MESH)

        mine = pltpu.make_async_copy(x_ref, slot(me), local_sem)
        own = [copy(1, me, xn, src=x_ref), copy(2, me, yn, src=x_ref), copy(0, me, sib, src=x_ref)]
        for cp in [mine] + own:
            cp.start()
        copy(1, xn, me).wait_recv()
        fwd = [copy(3, xn, yn, top), copy(5, xn, sib)]
        for cp in fwd:
            cp.start()
        copy(2, yn, me).wait_recv()
        fwd += [copy(4, yn, xn, bot), copy(6, yn, sib)]
        for cp in fwd[2:]:
            cp.start()
        copy(3, dg, me, top).wait_recv()
        fwd.append(copy(7, dg, sib, top))
        fwd[-1].start()
        copy(4, dg, me, bot).wait_recv()
        fwd.append(copy(8, dg, sib, bot))
        fwd[-1].start()
        copy(0, sib, me).wait_recv()
        copy(5, other(xn), me).wait_recv()
        copy(6, other(yn), me).wait_recv()
        copy(7, other(dg), me, top).wait_recv()
        copy(8, other(dg), me, bot).wait_recv()
        for cp in own + fwd:
            cp.wait_send()
        mine.wait()

    return _pallas(
        body, name=name, in_specs=[ANY], out_specs=ANY,
        out_shape=jax.ShapeDtypeStruct((N_DEV, rows, cols), blk.dtype),
        scratch_shapes=[pltpu.SemaphoreType.DMA((9,)), pltpu.SemaphoreType.DMA((9,)), pltpu.SemaphoreType.DMA],
    )(blk)


def _all_gather_direct(blk, name):
    rows, cols = blk.shape

    def body(x_ref, out_ref, send_sems, recv_sems, local_sem):
        x, y, c = _coords()
        me = 4 * x + 2 * y + c
        mine = pltpu.make_async_copy(x_ref, out_ref.at[me], local_sem)
        mine.start()
        copies = []
        for k in range(1, N_DEV):
            fx, fy, fc = (k >> 2) & 1, (k >> 1) & 1, k & 1
            peer = (x ^ fx, y ^ fy, c ^ fc)
            copies.append(pltpu.make_async_remote_copy(
                src_ref=x_ref, dst_ref=out_ref.at[me], send_sem=send_sems.at[k - 1], recv_sem=recv_sems.at[k - 1],
                device_id=peer, device_id_type=MESH))
        for cp in copies:
            cp.start()
        for k in range(1, N_DEV):
            fx, fy, fc = (k >> 2) & 1, (k >> 1) & 1, k & 1
            src = 4 * (x ^ fx) + 2 * (y ^ fy) + (c ^ fc)
            pltpu.make_async_remote_copy(
                src_ref=x_ref, dst_ref=out_ref.at[src], send_sem=send_sems.at[k - 1], recv_sem=recv_sems.at[k - 1],
                device_id=(x, y, c), device_id_type=MESH).wait_recv()
        for cp in copies:
            cp.wait_send()
        mine.wait()

    return _pallas(
        body, name=name, in_specs=[ANY], out_specs=ANY,
        out_shape=jax.ShapeDtypeStruct((N_DEV, rows, cols), blk.dtype),
        scratch_shapes=[pltpu.SemaphoreType.DMA((7,)), pltpu.SemaphoreType.DMA((7,)), pltpu.SemaphoreType.DMA],
    )(blk)


def _pair_exchange(grads, name):
    n = len(grads)

    def body(*refs):
        g_refs, out_refs = refs[:n], refs[n:2 * n]
        send_sems, recv_sems = refs[2 * n:]
        x, y, c = _coords()
        copies = []
        for a in range(n):
            for k in range(4):
                copies.append(pltpu.make_async_remote_copy(
                    src_ref=g_refs[a].at[2 * k + (1 - c)], dst_ref=out_refs[a].at[k],
                    send_sem=send_sems.at[a, k], recv_sem=recv_sems.at[a, k],
                    device_id=(x, y, 1 - c), device_id_type=MESH))
        for cp in copies:
            cp.start()
        for cp in copies:
            cp.wait()

    return _pallas(
        body, name=name, in_specs=[ANY] * n, out_specs=[ANY] * n,
        out_shape=[jax.ShapeDtypeStruct((4,) + g.shape[1:], g.dtype) for g in grads],
        scratch_shapes=[pltpu.SemaphoreType.DMA((n, 4)), pltpu.SemaphoreType.DMA((n, 4))],
    )(*grads)


def _pair_add(grads, got, core, out_dtype, name):
    _, rows, cols = grads.shape
    tr = _row_tile(rows, 256) if rows % 8 == 0 else rows

    def body(c_ref, a_ref, b_ref, o_ref):
        o_ref[...] = (a_ref[...].astype(F32) + b_ref[...].astype(F32)).astype(o_ref.dtype)

    grid_spec = pltpu.PrefetchScalarGridSpec(
        num_scalar_prefetch=1, grid=(4, rows // tr),
        in_specs=[pl.BlockSpec((1, tr, cols), lambda k, i, c_ref: (2 * k + c_ref[0], i, 0)),
                  pl.BlockSpec((1, tr, cols), lambda k, i, c_ref: (k, i, 0))],
        out_specs=pl.BlockSpec((1, tr, cols), lambda k, i, c_ref: (k, i, 0)))
    return _pallas(
        body, name=name, grid_spec=grid_spec,
        out_shape=jax.ShapeDtypeStruct((4, rows, cols), out_dtype),
        compiler_params=_cparams(("parallel", "parallel"), 48),
    )(core, grads, got)


class _ChipExchange:
    def __init__(self, partials):
        self.n = len(partials)
        self.in_specs = [ANY] * self.n
        self.out_specs = [ANY] * self.n
        self.out_shape = [jax.ShapeDtypeStruct(p.shape, p.dtype) for p in partials]
        self.scratch = [pltpu.SemaphoreType.DMA((self.n, 3)), pltpu.SemaphoreType.DMA((self.n, 3)),
                        pltpu.SemaphoreType.DMA((self.n,))]

    def _copies(self, p_refs, out_refs, sems, receiving):
        send_sems, recv_sems, local_sems = sems
        x, y, c = _coords()
        mychip = 2 * x + y
        local = [pltpu.make_async_copy(p_refs[a].at[mychip], out_refs[a].at[mychip], local_sems.at[a])
                 for a in range(self.n)]
        remote = []
        for a in range(self.n):
            for k in range(1, 4):
                px, py = x ^ ((k >> 1) & 1), y ^ (k & 1)
                peer = 2 * px + py
                remote.append(pltpu.make_async_remote_copy(
                    src_ref=p_refs[a].at[peer], dst_ref=out_refs[a].at[peer if receiving else mychip],
                    send_sem=send_sems.at[a, k - 1], recv_sem=recv_sems.at[a, k - 1],
                    device_id=(x, y, c) if receiving else (px, py, c), device_id_type=MESH))
        return local, remote

    def start(self, p_refs, out_refs, sems):
        local, sends = self._copies(p_refs, out_refs, sems, False)
        for cp in local + sends:
            cp.start()

    def wait(self, p_refs, out_refs, sems):
        local, recvs = self._copies(p_refs, out_refs, sems, True)
        for cp in recvs:
            cp.wait_recv()
        for cp in recvs:
            cp.wait_send()
        for cp in local:
            cp.wait()


class _DirectExchange:
    def __init__(self, partials):
        self.n = len(partials)
        self.in_specs = [ANY] * self.n
        self.out_specs = [ANY] * self.n
        self.out_shape = [jax.ShapeDtypeStruct(p.shape, p.dtype) for p in partials]
        self.scratch = [pltpu.SemaphoreType.DMA((self.n, N_DEV - 1)), pltpu.SemaphoreType.DMA((self.n, N_DEV - 1)),
                        pltpu.SemaphoreType.DMA((self.n,))]

    def _copies(self, p_refs, out_refs, sems, receiving):
        send_sems, recv_sems, local_sems = sems
        x, y, c = _coords()
        me = 4 * x + 2 * y + c
        local = [pltpu.make_async_copy(p_refs[a].at[me], out_refs[a].at[me], local_sems.at[a])
                 for a in range(self.n)]
        remote = []
        for a in range(self.n):
            for k in range(1, N_DEV):
                px, py, pc = x ^ ((k >> 2) & 1), y ^ ((k >> 1) & 1), c ^ (k & 1)
                peer = 4 * px + 2 * py + pc
                remote.append(pltpu.make_async_remote_copy(
                    src_ref=p_refs[a].at[peer], dst_ref=out_refs[a].at[peer if receiving else me],
                    send_sem=send_sems.at[a, k - 1], recv_sem=recv_sems.at[a, k - 1],
                    device_id=(x, y, c) if receiving else (px, py, pc), device_id_type=MESH))
        return local, remote

    def start(self, p_refs, out_refs, sems):
        local, sends = self._copies(p_refs, out_refs, sems, False)
        for cp in local + sends:
            cp.start()

    def wait(self, p_refs, out_refs, sems):
        local, recvs = self._copies(p_refs, out_refs, sems, True)
        for cp in recvs:
            cp.wait_recv()
        for cp in recvs:
            cp.wait_send()
        for cp in local:
            cp.wait()


def _pad_lanes(v, n=HEAD_PAD):
    return jnp.pad(v, ((0, 0), (0, n - v.shape[1])))


def kernel(x, p, norm_g, w_in, conv_w, conv_b, dt_bias, a_log, d_skip, gnorm_g, pool_mix_w, pool_mix_b, pool_scale, w_branch_a, w_branch_b, w_out, ple_norm_g, w_ple_gate, w_ple_up, final_g, loss_target, m_norm_g, m_w_in, m_conv_w, m_conv_b, m_dt_bias, m_a_log, m_d_skip, m_gnorm_g, m_pool_mix_w, m_pool_mix_b, m_pool_scale, m_w_branch_a, m_w_branch_b, m_w_out, m_ple_norm_g, m_w_ple_gate, m_w_ple_up, m_final_g, v_norm_g, v_w_in, v_conv_w, v_conv_b, v_dt_bias, v_a_log, v_d_skip, v_gnorm_g, v_pool_mix_w, v_pool_mix_b, v_pool_scale, v_w_branch_a, v_w_branch_b, v_w_out, v_ple_norm_g, v_w_ple_gate, v_w_ple_up, v_final_g):
    nb, s, d = x.shape
    t = nb * s
    n_heads = dt_bias.shape[1]
    d_inner = n_heads * HEAD_DIM
    n_conv = conv_b.shape[1]
    d_pool = pool_mix_b.shape[1]
    n_in = w_in.shape[2] * N_DEV
    ple_dim = p.shape[-1]
    o_xbc, o_dt = d_inner, d_inner + n_conv
    o_u = o_dt + n_heads
    o_zp = o_u + d_pool
    o_ga, o_gb = o_zp + d_pool, o_zp + d_pool + d
    n_main = n_in - n_heads
    off_ga, off_gb, off_xbc = d_inner, d_inner + d, d_inner + 2 * d
    off_u = off_xbc + n_conv
    off_zp = off_u + d_pool

    sharded = [w_in[0], conv_w[0], pool_mix_w[0], w_branch_a[0], w_branch_b[0], w_out[0], w_ple_gate[0], w_ple_up[0]]
    sharded_m = [m_w_in[0], m_conv_w[0], m_pool_mix_w[0], m_w_branch_a[0], m_w_branch_b[0], m_w_out[0],
                 m_w_ple_gate[0], m_w_ple_up[0]]
    sharded_v = [v_w_in[0], v_conv_w[0], v_pool_mix_w[0], v_w_branch_a[0], v_w_branch_b[0], v_w_out[0],
                 v_w_ple_gate[0], v_w_ple_up[0]]
    ng = pool_mix_w.shape[1]
    pg = d_pool // ng
    two_d = lambda w: w.reshape(-1, w.shape[-1])
    sharded, sharded_m, sharded_v = ([two_d(w) for w in ws] for ws in (sharded, sharded_m, sharded_v))

    send = [w if i == 1 else w.astype(BF16) for i, w in enumerate(sharded)]
    seg = [_all_gather_routed(send[0], "w_in_all_gather")]
    chunk = n_in // N_DEV
    segments = [((0, o_xbc), 0), ((o_ga, o_gb), off_ga), ((o_gb, n_in), off_gb), ((o_xbc, o_dt), off_xbc),
                ((o_u, o_zp), off_u), ((o_zp, o_ga), off_zp)]

    def gathered_cols(lo, hi):
        out = []
        while lo < hi:
            dev = lo // chunk
            a, b = lo - dev * chunk, min(hi - dev * chunk, chunk)
            out.append(seg[0][dev, :, a:b])
            lo = dev * chunk + b
        return out

    w_main = jnp.concatenate([pc for (lo, hi), _ in segments for pc in gathered_cols(lo, hi)], axis=1)
    w_dt = _pad_lanes(jnp.concatenate(gathered_cols(o_dt, o_u), axis=1))

    x2 = x.reshape(t, d)
    h = _rms_fwd(x2, norm_g, "rms_in")
    proj, *rest = _in_proj(h, w_main, send[1:])
    seg += rest
    cw_full = seg[1].transpose(1, 0, 2).reshape(CONV_W, n_conv)
    mixw_full = seg[2].reshape(N_DEV, ng, pg // N_DEV, pg).transpose(1, 0, 2, 3).reshape(ng, pg, pg)
    wa_full = seg[3].reshape(d_inner, d)
    wb_full = seg[4].reshape(d_pool, d)
    wout_full = seg[5].reshape(d, d)
    wg_full = seg[6].reshape(d, d)
    wup_full = seg[7].transpose(1, 0, 2).reshape(ple_dim, d)

    p2 = p.reshape(t, ple_dim)
    tgt = loss_target.reshape(t, d)
    bias_p, alog_p = _pad_lanes(dt_bias), _pad_lanes(a_log)
    dskip_full = jnp.repeat(d_skip, HEAD_DIM, axis=1)

    dtraw = _matmul(h, w_dt, "nn", F32, "in_proj_dt", 1024, HEAD_PAD, d)
    xbc, conv_pre = _conv_fwd(proj, cw_full, conv_b, nb, s, off_xbc, n_conv)
    y, ya3, prev = _ssd_fwd(xbc, proj, dtraw, bias_p, alog_p, dskip_full, gnorm_g, nb, s, d_inner)
    pooled = _pool_fwd(proj, nb, s, off_u, d_pool)
    mixed, yb = _mix_fwd(pooled, mixw_full, proj, pool_mix_b, pool_scale, off_zp)
    (merged, h2, dx1, dx1b, dgl, dpu, dpa, dpb, dproj, loss_part, dfinal_g, d_ple_g) = _merge_to_grads(
        ya3, wa_full, yb, wb_full, proj, off_ga, off_gb, wout_full, x2, ple_norm_g, wg_full, p2, wup_full, tgt,
        final_g.reshape(1, d), n_main)

    g_wup = _matmul(p2, dpu, "tn", BF16, "dw_ple_up", 256, 512, t)
    g_wg = _matmul(h2, dgl, "tn", BF16, "dw_ple_gate", 512, 512, t)
    g_wout = _matmul(merged, dx1b, "tn", BF16, "dw_out", 512, 512, t)
    g_wa = _matmul(ya3, dpa, "tn", BF16, "dw_branch_a", 512, 512, t)
    g_wb = _matmul(yb, dpb, "tn", BF16, "dw_branch_b", 512, 512, t)
    dya3 = _matmul(dpa, wa_full, "nt", ACT, "d_ya3", 1024, 1024, d)
    dproj, dmixed, d_scale, d_mixb = _yb_bwd(dproj, dpb, wb_full, proj, mixed, pool_mix_b, pool_scale, off_zp)
    g_mixw = _dmixw(pooled, dmixed, ng)
    dpool = _dpooled(dmixed, mixw_full)
    dproj = _pool_bwd(dproj, dpool, nb, s, off_u)

    core = lax.axis_index("c").astype(I32).reshape(1)

    def pair_sums(by_dest, names):
        by_dest = [g.astype(BF16) for g in by_dest]
        got = _pair_exchange(by_dest, "rs_pair_exchange_" + names[0])
        return [_pair_add(g, r, core, BF16, "rs_pair_add_" + nm) for g, r, nm in zip(by_dest, got, names)]

    names_a = ["pool_mix_w", "w_branch_a", "w_branch_b", "w_out", "w_ple_gate", "w_ple_up"]
    partials_a = [
        g_mixw.reshape(ng, N_DEV, pg // N_DEV, pg).transpose(1, 0, 2, 3).reshape(N_DEV, ng * pg // N_DEV, pg)
        .astype(BF16),
        g_wa.reshape(N_DEV, d_inner // N_DEV, d), g_wb.reshape(N_DEV, d_pool // N_DEV, d),
        g_wout.reshape(N_DEV, d // N_DEV, d), g_wg.reshape(N_DEV, d // N_DEV, d),
        g_wup.reshape(ple_dim, N_DEV, d // N_DEV).transpose(1, 0, 2)]
    dproj, dxbc, ddt, d_gn, d_dsk, d_alog, d_dtb, *parts_a = _ssd_bwd(
        dproj, dya3, y, xbc, proj, dtraw, prev, bias_p, alog_p, dskip_full, gnorm_g, nb, s, d_inner, partials_a)
    dproj, g_cw, d_cb = _conv_bwd(dproj, dxbc, conv_pre, proj, cw_full, nb, s, off_xbc)
    g_wmain = _matmul(h, dproj, "tn", BF16, "dw_in", 512, 1024, t)
    g_wdt = _matmul(h, ddt, "tn", BF16, "dw_in_dt", 512, HEAD_PAD, t)

    def dest_cols(dev):
        lo, hi = dev * chunk, (dev + 1) * chunk
        srcs = [((olo, ohi), g_wmain, off) for (olo, ohi), off in segments] + [((o_dt, o_u), g_wdt, 0)]
        out = []
        for (olo, ohi), arr, off in sorted(srcs, key=lambda e: e[0][0]):
            a, b = max(lo, olo), min(hi, ohi)
            if a < b:
                out.append(arr[:, off + a - olo:off + b - olo])
        return jnp.concatenate(out, axis=1)

    names_b = ["w_in", "conv_w"]
    partials_b = pair_sums([jnp.stack([dest_cols(dev) for dev in range(N_DEV)], axis=0),
                            g_cw.reshape(CONV_W, N_DEV, n_conv // N_DEV).transpose(1, 0, 2)], names_b)
    grad_x, d_ng, *parts_b = _in_bwd(dproj, w_main, ddt, w_dt, x2, norm_g, dx1, partials_b)

    names = names_b + names_a
    parts = parts_b + parts_a
    sh_g, sh_d, sh_m, sh_v = [], [], [], []
    shapes = [w_in.shape, conv_w.shape, pool_mix_w.shape, w_branch_a.shape, w_branch_b.shape, w_out.shape,
              w_ple_gate.shape, w_ple_up.shape]
    for pt, w, m, v, nm, shp in zip(parts, sharded, sharded_m, sharded_v, names, shapes):
        res = _adamw(pt, w, m, v, "adamw_" + nm)
        for dst, r in zip((sh_g, sh_d, sh_m, sh_v), res):
            dst.append(r.reshape(shp))

    rep_w = [norm_g, conv_b, gnorm_g, pool_mix_b, pool_scale, ple_norm_g, final_g.reshape(1, d),
             bias_p, alog_p, _pad_lanes(d_skip)]
    rep_m = [m_norm_g, m_conv_b, m_gnorm_g, m_pool_mix_b, m_pool_scale, m_ple_norm_g, m_final_g.reshape(1, d),
             _pad_lanes(m_dt_bias), _pad_lanes(m_a_log), _pad_lanes(m_d_skip)]
    rep_v = [v_norm_g, v_conv_b, v_gnorm_g, v_pool_mix_b, v_pool_scale, v_ple_norm_g, v_final_g.reshape(1, d),
             _pad_lanes(v_dt_bias), _pad_lanes(v_a_log), _pad_lanes(v_d_skip)]
    rep_g = [d_ng, d_cb, d_gn, d_mixb, d_scale, d_ple_g, dfinal_g, d_dtb, d_alog, d_dsk]
    zero = jnp.zeros((1, LANES), F32)
    small = _all_gather_direct(jnp.concatenate(rep_g + [loss_part], axis=1), "small_all_gather")
    pk = lambda ws: jnp.concatenate(ws + [zero], axis=1)
    rg, rd, rm, rv = _adamw(small, pk(rep_w), pk(rep_m), pk(rep_v), "adamw_replicated")
    loss = rg[0, sum(int(w.size) for w in rep_w)]

    def unpack_rep(buf):
        out, pos_ = [], 0
        for w in rep_w:
            out.append(buf[:, pos_:pos_ + w.size])
            pos_ += w.size
        return out

    def arrange(sh, rep):
        rep = list(rep)
        rep[6] = rep[6].reshape(d)
        for i in (7, 8, 9):
            rep[i] = rep[i][:, :n_heads]
        return [rep[0], sh[0], sh[1], rep[1], rep[7], rep[8], rep[9], rep[2], sh[2], rep[3], rep[4],
                sh[3], sh[4], sh[5], rep[5], sh[6], sh[7], rep[6]]

    outs = [loss, grad_x.reshape(nb, s, d)]
    for sh, rep in ((sh_g, rg), (sh_d, rd), (sh_m, rm), (sh_v, rv)):
        outs += arrange(sh, unpack_rep(rep))
    return tuple(outs)
```

```python
import jax
import jax.numpy as jnp
from jax import lax
from jax.experimental import pallas as pl
from jax.experimental.pallas import tpu as pltpu

F32 = jnp.float32
BF16 = jnp.bfloat16
I32 = jnp.int32
MESH = pl.DeviceIdType.MESH
ANY = pl.BlockSpec(memory_space=pl.ANY)

EPS = 1e-6
CHUNK = 128
HEAD_DIM = 64
D_STATE = 128
GROUPS = 4
CONV_W = 4
POOL_WINDOWS = (2, 4, 8, 16)
ACT = jnp.bfloat16
LANES = 128
SUBLANES = 8
CONV_ROWS = 64
HEAD_PAD = 128
NEG = -1e30
N_DEV = 8

ADAM_LR = 0.001
ADAM_B1 = 0.9
ADAM_B2 = 0.999
ADAM_EPS = 1e-08
ADAM_WD = 0.01
ADAM_STEP = 10

NN = (((1,), (0,)), ((), ()))
NT = (((1,), (1,)), ((), ()))
TN = (((0,), (0,)), ((), ()))


def _pallas(body, **kw):
    return pl.pallas_call(body, **kw)


def _cparams(dims=None, vmem_mb=None):
    kw = {}
    if dims is not None:
        kw["dimension_semantics"] = dims
    if vmem_mb is not None:
        kw["vmem_limit_bytes"] = vmem_mb << 20
    return pltpu.CompilerParams(**kw)


def _dot(a, b, dims=NN):
    return lax.dot_general(a, b, dims, preferred_element_type=F32)


def _split3(v):
    hi = v.astype(BF16)
    r = v - hi.astype(F32)
    mid = r.astype(BF16)
    lo = (r - mid.astype(F32)).astype(BF16)
    return hi, mid, lo


def _exact_r(v, m, dims=NN):
    hi, mid, lo = _split3(v)
    return _dot(hi, m, dims) + _dot(mid, m, dims) + _dot(lo, m, dims)


def _exact_l(m, v, dims=NN):
    hi, mid, lo = _split3(v)
    return _dot(m, hi, dims) + _dot(m, mid, dims) + _dot(m, lo, dims)


def _row_tile(rows, target):
    best = None
    for cand in range(8, min(rows, target) + 1, 8):
        if rows % cand == 0:
            best = cand
    assert best is not None, rows
    return best


def _sum_all(v):
    return jnp.sum(jnp.sum(v, axis=1, keepdims=True), axis=0, keepdims=True)


def _iota(shape, dim):
    return lax.broadcasted_iota(I32, shape, dim)


def _sigmoid(x):
    return 0.5 * jnp.tanh(0.5 * x) + 0.5


def _softplus(x):
    return jnp.maximum(x, 0.0) + jnp.log(1.0 + jnp.exp(-jnp.abs(x)))


def _shift_down(v, j, row):
    return jnp.where(row >= j, pltpu.roll(v, j, 0), 0.0)


def _shift_up(v, j, row):
    n = v.shape[0]
    return jnp.where(row < n - j, pltpu.roll(v, n - j, 0), 0.0)


def _matmul(a, b, mode, out_dtype, name, tm, tn, tk):
    if mode == "nn":
        (m, k), n = a.shape, b.shape[1]
    elif mode == "nt":
        (m, k), n = a.shape, b.shape[0]
    else:
        (k, m), n = a.shape, b.shape[1]
    tm, tn, tk = min(tm, m), min(tn, n), min(tk, k)
    assert m % tm == 0 and n % tn == 0 and k % tk == 0, (name, m, n, k)
    nk = k // tk
    dims = {"nn": NN, "nt": NT, "tn": TN}[mode]

    def body_acc(a_ref, b_ref, o_ref, acc_ref):
        kk = pl.program_id(2)

        @pl.when(kk == 0)
        def _():
            acc_ref[...] = jnp.zeros_like(acc_ref)

        acc_ref[...] += _dot(a_ref[...].astype(BF16), b_ref[...].astype(BF16), dims)

        @pl.when(kk == nk - 1)
        def _():
            o_ref[...] = acc_ref[...].astype(o_ref.dtype)

    def body_one(a_ref, b_ref, o_ref):
        o_ref[...] = _dot(a_ref[...].astype(BF16), b_ref[...].astype(BF16), dims).astype(o_ref.dtype)

    body = body_one if nk == 1 else body_acc
    if mode == "tn":
        a_spec = pl.BlockSpec((tk, tm), lambda i, j, kk: (kk, i))
    else:
        a_spec = pl.BlockSpec((tm, tk), lambda i, j, kk: (i, kk))
    if mode == "nt":
        b_spec = pl.BlockSpec((tn, tk), lambda i, j, kk: (j, kk))
    else:
        b_spec = pl.BlockSpec((tk, tn), lambda i, j, kk: (kk, j))
    return _pallas(
        body, name=name, grid=(m // tm, n // tn, nk),
        in_specs=[a_spec, b_spec],
        out_specs=pl.BlockSpec((tm, tn), lambda i, j, kk: (i, j)),
        out_shape=jax.ShapeDtypeStruct((m, n), out_dtype),
        scratch_shapes=[] if nk == 1 else [pltpu.VMEM((tm, tn), F32)],
        compiler_params=_cparams(("parallel", "parallel", "arbitrary"), 56),
    )(a, b)


def _in_proj(h, w_main, blocks):
    t, d = h.shape
    n = w_main.shape[1]
    tm, tn = min(1024, t), 1024
    ni, nj = t // tm, n // tn
    gather = _TwoLevelGather(blocks)

    def body(*refs):
        a_ref, b_ref = refs[:2]
        x_refs = refs[2:2 + gather.n]
        o_ref = refs[2 + gather.n]
        got_refs = refs[3 + gather.n:3 + 2 * gather.n]
        sems = refs[3 + 2 * gather.n:]
        i, j = pl.program_id(0), pl.program_id(1)

        @pl.when((i == 0) & (j == 0))
        def _():
            gather.start(x_refs, got_refs, sems)

        o_ref[...] = _dot(a_ref[...], b_ref[...]).astype(o_ref.dtype)

        @pl.when((i == ni - 1) & (j == nj - 1))
        def _():
            gather.finish(x_refs, got_refs, sems)

    return _pallas(
        body, name="in_proj", grid=(ni, nj),
        in_specs=[pl.BlockSpec((tm, d), lambda i, j: (i, 0)), pl.BlockSpec((d, tn), lambda i, j: (0, j))]
        + gather.in_specs,
        out_specs=[pl.BlockSpec((tm, tn), lambda i, j: (i, j))] + gather.out_specs,
        out_shape=[jax.ShapeDtypeStruct((t, n), ACT)] + gather.out_shape,
        scratch_shapes=gather.scratch,
        compiler_params=_cparams(("arbitrary", "arbitrary"), 56),
    )(h, w_main, *blocks)


def _rms_fwd(x, g, name):
    t, d = x.shape
    tt = 512

    def body(x_ref, g_ref, h_ref):
        xv = x_ref[...]
        r = lax.rsqrt(jnp.mean(xv * xv, axis=-1, keepdims=True) + EPS)
        h_ref[...] = (xv * r * g_ref[...]).astype(h_ref.dtype)

    return _pallas(
        body, name=name, grid=(t // tt,),
        in_specs=[pl.BlockSpec((tt, d), lambda i: (i, 0)), pl.BlockSpec((1, d), lambda i: (0, 0))],
        out_specs=pl.BlockSpec((tt, d), lambda i: (i, 0)),
        out_shape=jax.ShapeDtypeStruct((t, d), BF16),
        compiler_params=_cparams(("parallel",)),
    )(x, g)


def _conv_fwd(proj, conv_w, conv_b, nb, s, off_xbc, n_conv):
    cw = 256
    rc = CONV_ROWS

    def body(x_ref, w_ref, b_ref, o_ref, pre_ref, pad_ref):
        pad_ref[0:SUBLANES, :] = jnp.zeros((SUBLANES, cw), F32)
        pad_ref[SUBLANES:, :] = x_ref[...].astype(F32)
        taps = [w_ref[k:k + 1, :] for k in range(CONV_W)]
        bias = b_ref[...]
        for r in range(s // rc):
            win = pad_ref[pl.ds(r * rc, rc + SUBLANES), :]
            acc = bias + win[SUBLANES:, :] * taps[CONV_W - 1]
            for j in range(1, CONV_W):
                acc = acc + pltpu.roll(win, j, 0)[SUBLANES:, :] * taps[CONV_W - 1 - j]
            pre_ref[pl.ds(r * rc, rc), :] = acc.astype(pre_ref.dtype)
            o_ref[pl.ds(r * rc, rc), :] = (acc * _sigmoid(acc)).astype(o_ref.dtype)

    out = pl.BlockSpec((s, cw), lambda b, j: (b, j))
    return _pallas(
        body, name="conv_fwd", grid=(nb, n_conv // cw),
        in_specs=[pl.BlockSpec((s, cw), lambda b, j: (b, off_xbc // cw + j)),
                  pl.BlockSpec((CONV_W, cw), lambda b, j: (0, j)),
                  pl.BlockSpec((1, cw), lambda b, j: (0, j))],
        out_specs=[out, out],
        out_shape=[jax.ShapeDtypeStruct((nb * s, n_conv), ACT), jax.ShapeDtypeStruct((nb * s, n_conv), ACT)],
        scratch_shapes=[pltpu.VMEM((s + SUBLANES, cw), F32)],
        compiler_params=_cparams(("parallel", "parallel"), 48),
    )(proj, conv_w, conv_b)


def _ssd_common(dtraw, bias, alog, n_heads):
    l = dtraw.shape[0]
    lane = _iota((l, HEAD_PAD), 1)
    lane1 = _iota((1, HEAD_PAD), 1)
    apad = jnp.where(lane1 < n_heads, -jnp.exp(alog), 0.0)
    dt = jnp.where(lane < n_heads, _softplus(dtraw + bias), 0.0)
    a = dt * apad
    tri = (_iota((l, l), 1) <= _iota((l, l), 0)).astype(BF16)
    acs = _exact_l(tri, a)
    width = n_heads * HEAD_DIM
    ex = (_iota((HEAD_PAD, width), 1) // HEAD_DIM == _iota((HEAD_PAD, width), 0)).astype(BF16)
    dt_full = _exact_r(dt, ex)
    acs_full = _exact_r(acs, ex)
    return dt, apad, acs, dt_full, acs_full, tri, ex


def _decay_mats(acs, acs_t, h, mask):
    col = acs[:, h:h + 1]
    row = acs_t[h:h + 1, :]
    return jnp.exp(jnp.where(mask, col - row, NEG))


def _ssd_fwd(xbc, proj, dtraw, dt_bias, a_log, dskip_full, gnorm, nb, s, d_inner):
    l = CHUNK
    nc = s // l
    n_heads = d_inner // HEAD_DIM
    gw = d_inner // GROUPS
    pairs_per_group = gw // LANES
    gn_cols = GROUPS * D_STATE

    def body(xs_ref, bm_ref, cm_ref, z_ref, dtr_ref, bias_ref, alog_ref, dsk_ref, gn_ref,
             y_ref, ya3_ref, prev_ref, state_ref):
        @pl.when(pl.program_id(1) == 0)
        def _():
            state_ref[...] = jnp.zeros_like(state_ref)

        xs = xs_ref[...].astype(F32)
        dt, apad, acs, dt_full, acs_full, tri, ex = _ssd_common(
            dtr_ref[...], bias_ref[...], alog_ref[...], n_heads)
        acs_t = acs.T
        e_full = jnp.exp(acs_full)
        last = acs_full[l - 1:l, :]
        dte = jnp.exp(last - acs_full)
        cd = jnp.exp(last)
        xdt = xs * dt_full
        xd = xdt * dte
        mask = _iota((l, l), 1) <= _iota((l, l), 0)
        lane = _iota((l, LANES), 1)
        prev_ref[0] = state_ref[...]
        for g in range(GROUPS):
            gsl = slice(g * gw, (g + 1) * gw)
            bg = bm_ref[:, g * D_STATE:(g + 1) * D_STATE].astype(BF16)
            cg = cm_ref[:, g * D_STATE:(g + 1) * D_STATE].astype(BF16)
            cb = _dot(cg, bg, NT)
            outs = []
            for q in range(pairs_per_group):
                pair = g * pairs_per_group + q
                h0, h1 = 2 * pair, 2 * pair + 1
                xpb = xdt[:, pair * LANES:(pair + 1) * LANES].astype(BF16)
                m0 = (cb * _decay_mats(acs, acs_t, h0, mask)).astype(BF16)
                m1 = (cb * _decay_mats(acs, acs_t, h1, mask)).astype(BF16)
                zero = jnp.zeros_like(xpb)
                rhs = jnp.concatenate([jnp.where(lane < HEAD_DIM, xpb, zero),
                                       jnp.where(lane >= HEAD_DIM, xpb, zero)], axis=0)
                outs.append(_dot(jnp.concatenate([m0, m1], axis=1), rhs))
            prev_g = state_ref[:, gsl]
            y_off = _dot(cg, prev_g.astype(BF16)) * e_full[:, gsl]
            y_g = jnp.concatenate(outs, axis=1) + y_off
            y_ref[:, gsl] = y_g.astype(y_ref.dtype)
            st = _dot(bg, xd[:, gsl].astype(BF16), TN)
            state_ref[:, gsl] = prev_g * cd[:, gsl] + st
            zv = z_ref[:, gsl].astype(F32)
            blk = (y_g + xs[:, gsl] * dsk_ref[:, gsl]) * (zv * _sigmoid(zv))
            rg = lax.rsqrt(jnp.mean(blk * blk, axis=-1, keepdims=True) + EPS)
            ya3_ref[:, gsl] = (blk * rg * gn_ref[:, gsl]).astype(ya3_ref.dtype)

    row = lambda b, c: b * nc + c
    small = lambda n: pl.BlockSpec((1, n), lambda b, c: (0, 0))
    return _pallas(
        body, name="ssd_fwd", grid=(nb, nc),
        in_specs=[pl.BlockSpec((l, d_inner), lambda b, c: (row(b, c), 0)),
                  pl.BlockSpec((l, gn_cols), lambda b, c: (row(b, c), d_inner // gn_cols)),
                  pl.BlockSpec((l, gn_cols), lambda b, c: (row(b, c), d_inner // gn_cols + 1)),
                  pl.BlockSpec((l, d_inner), lambda b, c: (row(b, c), 0)),
                  pl.BlockSpec((l, HEAD_PAD), lambda b, c: (row(b, c), 0)),
                  small(HEAD_PAD), small(HEAD_PAD), small(d_inner), small(d_inner)],
        out_specs=[pl.BlockSpec((l, d_inner), lambda b, c: (row(b, c), 0)),
                   pl.BlockSpec((l, d_inner), lambda b, c: (row(b, c), 0)),
                   pl.BlockSpec((1, D_STATE, d_inner), lambda b, c: (row(b, c), 0, 0))],
        out_shape=[jax.ShapeDtypeStruct((nb * s, d_inner), ACT),
                   jax.ShapeDtypeStruct((nb * s, d_inner), BF16),
                   jax.ShapeDtypeStruct((nb * nc, D_STATE, d_inner), F32)],
        scratch_shapes=[pltpu.VMEM((D_STATE, d_inner), F32)],
        compiler_params=_cparams(("parallel", "arbitrary"), 48),
    )(xbc, xbc, xbc, proj, dtraw, dt_bias, a_log, dskip_full, gnorm)


def _pool_fwd(proj, mixw, mix_b, scale, nb, s, off_u, off_zp, d_pool):
    pg = d_pool // len(POOL_WINDOWS)

    def body(u_ref, zp_ref, w_ref, b_ref, s_ref, pooled_ref, mixed_ref, yb_ref):
        g = pl.program_id(1)
        uv = u_ref[...].astype(F32)
        row = _iota(uv.shape, 0)
        pos = (_iota((uv.shape[0], 1), 0) + 1).astype(F32)
        for gi, w in enumerate(POOL_WINDOWS):
            @pl.when(g == gi)
            def _():
                acc, span = uv, 1
                while span < w:
                    acc = acc + _shift_down(acc, span, row)
                    span *= 2
                pooled_ref[...] = (acc / jnp.minimum(pos, float(w)) - uv).astype(pooled_ref.dtype)

        mixed = _dot(pooled_ref[...], w_ref[0])
        mixed_ref[...] = mixed.astype(mixed_ref.dtype)
        zp = zp_ref[...].astype(F32)
        yb_ref[...] = ((mixed + b_ref[...]) * s_ref[...] * (zp * _sigmoid(zp))).astype(yb_ref.dtype)

    seg = lambda off: pl.BlockSpec((s, pg), lambda b, g: (b, off // pg + g))
    vec = pl.BlockSpec((1, pg), lambda b, g: (0, g))
    return _pallas(
        body, name="pool_fwd", grid=(nb, len(POOL_WINDOWS)),
        in_specs=[seg(off_u), seg(off_zp), pl.BlockSpec((1, pg, pg), lambda b, g: (g, 0, 0)), vec, vec],
        out_specs=[seg(0), seg(0), seg(0)],
        out_shape=[jax.ShapeDtypeStruct((nb * s, d_pool), BF16), jax.ShapeDtypeStruct((nb * s, d_pool), ACT),
                   jax.ShapeDtypeStruct((nb * s, d_pool), BF16)],
        compiler_params=_cparams(("parallel", "parallel"), 48),
    )(proj, proj, mixw, mix_b, scale)


def _merge_to_grads(ya3, wa, yb, wb, proj, off_ga, off_gb, wout, x, ple_g, wg, p, wup, target, final_g, n_main):
    t, d = x.shape
    pd = p.shape[1]
    tm = 256
    assert off_gb == off_ga + d and off_ga % (2 * d) == 0

    def body(a_ref, wa_ref, b_ref, wb_ref, ga_ref, gb_ref, wout_ref, x_ref, pg_ref, wg_ref, p_ref, wup_ref,
             tg_ref, fg_ref,
             mg_ref, h2_ref, dx1_ref, dx1b_ref, dgl_ref, dpu_ref, dpa_ref, dpb_ref, dgate_ref,
             loss_ref, dfg_ref, dpg_ref):
        @pl.when(pl.program_id(0) == 0)
        def _():
            loss_ref[...] = jnp.zeros_like(loss_ref)
            dfg_ref[...] = jnp.zeros_like(dfg_ref)
            dpg_ref[...] = jnp.zeros_like(dpg_ref)

        pa = _dot(a_ref[...], wa_ref[...])
        pb = _dot(b_ref[...], wb_ref[...])
        sa = _sigmoid(ga_ref[...].astype(F32))
        sb = _sigmoid(gb_ref[...].astype(F32))
        merged = (sa * pa + sb * pb).astype(BF16)
        mg_ref[...] = merged
        x1 = x_ref[...] + _dot(merged, wout_ref[...])
        r1 = lax.rsqrt(jnp.mean(x1 * x1, axis=-1, keepdims=True) + EPS)
        xh1 = x1 * r1
        pg = pg_ref[...]
        h2 = (xh1 * pg).astype(BF16)
        h2_ref[...] = h2
        gate = _sigmoid(_dot(h2, wg_ref[...]))
        pu = _dot(p_ref[...].astype(BF16), wup_ref[...])
        x2 = x1 + gate * pu
        r2 = lax.rsqrt(jnp.mean(x2 * x2, axis=-1, keepdims=True) + EPS)
        xh2 = x2 * r2
        gfin = fg_ref[...]
        err = xh2 * gfin - tg_ref[...]
        loss_ref[...] += _sum_all(err * err) * (0.5 / d)
        dy = err * (1.0 / d)
        dfg_ref[...] += jnp.sum(dy * xh2, axis=0, keepdims=True)
        dg = dy * gfin
        dx2 = r2 * (dg - xh2 * jnp.mean(dg * xh2, axis=-1, keepdims=True))
        dgl = (dx2 * pu * gate * (1.0 - gate)).astype(BF16)
        dgl_ref[...] = dgl
        dpu_ref[...] = (dx2 * gate).astype(dpu_ref.dtype)
        dh2 = _dot(dgl, wg_ref[...], NT)
        dpg_ref[...] += jnp.sum(dh2 * xh1, axis=0, keepdims=True)
        dg = dh2 * pg
        dx1 = dx2 + r1 * (dg - xh1 * jnp.mean(dg * xh1, axis=-1, keepdims=True))
        dx1_ref[...] = dx1
        dx1b = dx1.astype(BF16)
        dx1b_ref[...] = dx1b
        dm = _dot(dx1b, wout_ref[...], NT)
        dpa_ref[...] = (dm * sa).astype(dpa_ref.dtype)
        dpb_ref[...] = (dm * sb).astype(dpb_ref.dtype)
        dgate_ref[:, :d] = (dm * pa * sa * (1.0 - sa)).astype(dgate_ref.dtype)
        dgate_ref[:, d:] = (dm * pb * sb * (1.0 - sb)).astype(dgate_ref.dtype)

    rows = lambda n: pl.BlockSpec((tm, n), lambda i: (i, 0))
    resident = lambda a: pl.BlockSpec(a.shape, lambda i: (0, 0), pipeline_mode=pl.Buffered(1))
    vec = lambda n: pl.BlockSpec((1, n), lambda i: (0, 0))
    bf = lambda: jax.ShapeDtypeStruct((t, d), BF16)
    return _pallas(
        body, name="merge_to_grads", grid=(t // tm,),
        in_specs=[rows(ya3.shape[1]), resident(wa), rows(yb.shape[1]), resident(wb),
                  pl.BlockSpec((tm, d), lambda i: (i, off_ga // d)), pl.BlockSpec((tm, d), lambda i: (i, off_gb // d)),
                  resident(wout), rows(d), vec(d), resident(wg), rows(pd), resident(wup), rows(d), vec(d)],
        out_specs=[rows(d)] * 8 + [pl.BlockSpec((tm, 2 * d), lambda i: (i, off_ga // (2 * d))),
                                   vec(LANES), vec(d), vec(d)],
        out_shape=[bf(), bf(), jax.ShapeDtypeStruct((t, d), F32), bf(), bf(), bf(), bf(), bf(),
                   jax.ShapeDtypeStruct((t, n_main), BF16), jax.ShapeDtypeStruct((1, LANES), F32),
                   jax.ShapeDtypeStruct((1, d), F32), jax.ShapeDtypeStruct((1, d), F32)],
        compiler_params=_cparams(("arbitrary",), 60),
    )(ya3, wa, yb, wb, proj, proj, wout, x, ple_g, wg, p, wup, target, final_g)


def _yb_bwd(dproj, dpb, wb, proj, mixed, mix_b, scale, off_zp):
    t, dp = mixed.shape
    tm = 512

    def body(_, dpb_ref, w_ref, zp_ref, mx_ref, b_ref, s_ref, dzp_ref, dmx_ref, dsc_ref, db_ref):
        @pl.when(pl.program_id(0) == 0)
        def _():
            dsc_ref[...] = jnp.zeros_like(dsc_ref)
            db_ref[...] = jnp.zeros_like(db_ref)

        dyb = _dot(dpb_ref[...], w_ref[...], NT)
        zp = zp_ref[...].astype(F32)
        sg = _sigmoid(zp)
        mb = mx_ref[...].astype(F32) + b_ref[...]
        yb0 = mb * s_ref[...]
        dyb0 = dyb * (zp * sg)
        dzp_ref[...] = (dyb * yb0 * (sg * (1.0 + zp * (1.0 - sg)))).astype(dzp_ref.dtype)
        dsc_ref[...] += jnp.sum(dyb0 * mb, axis=0, keepdims=True)
        dmx = dyb0 * s_ref[...]
        db_ref[...] += jnp.sum(dmx, axis=0, keepdims=True)
        dmx_ref[...] = dmx.astype(dmx_ref.dtype)

    rowblk = pl.BlockSpec((tm, dp), lambda i: (i, 0))
    vec = pl.BlockSpec((1, dp), lambda i: (0, 0))
    seg = pl.BlockSpec((tm, dp), lambda i: (i, off_zp // dp))
    return _pallas(
        body, name="yb_bwd", grid=(t // tm,),
        in_specs=[ANY, pl.BlockSpec((tm, wb.shape[1]), lambda i: (i, 0)),
                  pl.BlockSpec(wb.shape, lambda i: (0, 0)), seg, rowblk, vec, vec],
        out_specs=[seg, rowblk, vec, vec],
        out_shape=[jax.ShapeDtypeStruct(dproj.shape, dproj.dtype), jax.ShapeDtypeStruct((t, dp), BF16),
                   jax.ShapeDtypeStruct((1, dp), F32), jax.ShapeDtypeStruct((1, dp), F32)],
        input_output_aliases={0: 0},
        compiler_params=_cparams(("arbitrary",), 56),
    )(dproj, dpb, wb, proj, mixed, mix_b, scale)


def _dmixw(pooled, dmixed, ng):
    t, dp = pooled.shape
    pg = dp // ng
    tk = min(2048, t)
    nk = t // tk

    def body(a_ref, b_ref, o_ref):
        @pl.when(pl.program_id(1) == 0)
        def _():
            o_ref[...] = jnp.zeros_like(o_ref)

        o_ref[0] += _dot(a_ref[...], b_ref[...], TN)

    blk = pl.BlockSpec((tk, pg), lambda g, k: (k, g))
    return _pallas(
        body, name="dmixw", grid=(ng, nk),
        in_specs=[blk, blk],
        out_specs=pl.BlockSpec((1, pg, pg), lambda g, k: (g, 0, 0)),
        out_shape=jax.ShapeDtypeStruct((ng, pg, pg), F32),
        compiler_params=_cparams(("parallel", "arbitrary")),
    )(pooled, dmixed)


def _pool_bwd(dproj, dmixed, mixw, nb, s, off_u):
    dp = dmixed.shape[1]
    pg = dp // len(POOL_WINDOWS)

    def body(_, d_ref, w_ref, o_ref):
        g = pl.program_id(1)
        dv = _dot(d_ref[...], w_ref[0], NT)
        row = _iota(dv.shape, 0)
        pos = (_iota((dv.shape[0], 1), 0) + 1).astype(F32)
        for gi, w in enumerate(POOL_WINDOWS):
            @pl.when(g == gi)
            def _():
                acc, span = dv / jnp.minimum(pos, float(w)), 1
                while span < w:
                    acc = acc + _shift_up(acc, span, row)
                    span *= 2
                o_ref[...] = (acc - dv).astype(o_ref.dtype)

    return _pallas(
        body, name="pool_bwd", grid=(nb, len(POOL_WINDOWS)),
        in_specs=[ANY, pl.BlockSpec((s, pg), lambda b, g: (b, g)), pl.BlockSpec((1, pg, pg), lambda b, g: (g, 0, 0))],
        out_specs=pl.BlockSpec((s, pg), lambda b, g: (b, off_u // pg + g)),
        out_shape=jax.ShapeDtypeStruct(dproj.shape, dproj.dtype),
        input_output_aliases={0: 0},
        compiler_params=_cparams(("parallel", "parallel"), 48),
    )(dproj, dmixed, mixw)


def _ssd_bwd(dproj, dya3, y, xbc, proj, dtraw, prev, dt_bias, a_log, dskip_full, gnorm, nb, s, d_inner, partials):
    l = CHUNK
    nc = s // l
    n_heads = d_inner // HEAD_DIM
    gw = d_inner // GROUPS
    pairs_per_group = gw // LANES
    gn_cols = GROUPS * D_STATE
    n_conv = d_inner + 2 * gn_cols
    last_step = nb * nc - 1

    xch = _DirectExchange(partials)
    n_in, n_out = 13, 7

    def body(*refs):
        (_, dya3_ref, y_ref, xs_ref, bm_ref, cm_ref, z_ref, dtr_ref, prev_ref, bias_ref, alog_ref,
         dsk_ref, gn_ref) = refs[:n_in]
        p_refs = refs[n_in:n_in + xch.n]
        outs = refs[n_in + xch.n:]
        dz_ref, dxbc_ref, ddt_ref, dgn_ref, dskh_ref, dalog_ref, dbias_ref = outs[:n_out]
        got_refs = outs[n_out:n_out + xch.n]
        g_ref, dskacc_ref = outs[n_out + xch.n:n_out + xch.n + 2]
        sems = outs[n_out + xch.n + 2:]
        step = pl.program_id(0) * nc + pl.program_id(1)

        @pl.when(pl.program_id(1) == 0)
        def _():
            g_ref[...] = jnp.zeros_like(g_ref)

        @pl.when(step == 0)
        def _():
            xch.start(p_refs, got_refs, sems)
            dgn_ref[...] = jnp.zeros_like(dgn_ref)
            dalog_ref[...] = jnp.zeros_like(dalog_ref)
            dbias_ref[...] = jnp.zeros_like(dbias_ref)
            dskacc_ref[...] = jnp.zeros_like(dskacc_ref)

        xs = xs_ref[...].astype(F32)
        zv = z_ref[...].astype(F32)
        dsk = dsk_ref[...]
        sg = _sigmoid(zv)
        sz = zv * sg
        ya1 = y_ref[...].astype(F32) + xs * dsk
        ya2 = ya1 * sz
        parts = []
        for g in range(GROUPS):
            gsl = slice(g * gw, (g + 1) * gw)
            blk = ya2[:, gsl]
            rg = lax.rsqrt(jnp.mean(blk * blk, axis=-1, keepdims=True) + EPS)
            xh = blk * rg
            d3 = dya3_ref[:, gsl].astype(F32)
            dgn_ref[:, gsl] += jnp.sum(d3 * xh, axis=0, keepdims=True)
            dg = d3 * gn_ref[:, gsl]
            parts.append(rg * (dg - xh * jnp.mean(dg * xh, axis=-1, keepdims=True)))
        dya2 = jnp.concatenate(parts, axis=1)
        dy = dya2 * sz
        dz_ref[...] = (dya2 * ya1 * (sg * (1.0 + zv * (1.0 - sg)))).astype(dz_ref.dtype)
        dskacc_ref[...] += jnp.sum(dy * xs, axis=0, keepdims=True)

        dtraw_v = dtr_ref[...]
        bias = bias_ref[...]
        dt, apad, acs, dt_full, acs_full, tri, ex = _ssd_common(dtraw_v, bias, alog_ref[...], n_heads)
        acs_t = acs.T
        e_full = jnp.exp(acs_full)
        last = acs_full[l - 1:l, :]
        dte = jnp.exp(last - acs_full)
        cd = jnp.exp(last)
        xdt = xs * dt_full
        xd = xdt * dte
        dy_e = dy * e_full
        mask = _iota((l, l), 1) <= _iota((l, l), 0)
        lane = _iota((l, LANES), 1)
        hlane = _iota((l, HEAD_PAD), 1)
        hsub = _iota((HEAD_PAD, l), 0)
        dacs = jnp.zeros((l, HEAD_PAD), F32)
        dacs_t = jnp.zeros((HEAD_PAD, l), F32)
        dacs_full, dlast, dxdt_all, db_all, dc_all = [], [], [], [], []
        for g in range(GROUPS):
            gsl = slice(g * gw, (g + 1) * gw)
            bg = bm_ref[:, g * D_STATE:(g + 1) * D_STATE].astype(BF16)
            cg = cm_ref[:, g * D_STATE:(g + 1) * D_STATE].astype(BF16)
            prev_g = prev_ref[0, :, gsl]
            prev_b = prev_g.astype(BF16)
            gst = g_ref[:, gsl]
            gst_b = gst.astype(BF16)
            dye_b = dy_e[:, gsl].astype(BF16)
            dc = _dot(dye_b, prev_b, NT)
            dprev = _dot(cg, dye_b, TN)
            y_off = _dot(cg, prev_b) * e_full[:, gsl]
            xd_g = xd[:, gsl]
            db = _dot(xd_g.astype(BF16), gst_b, NT)
            dxd = _dot(bg, gst_b)
            wgt = dxd * xd_g
            dacs_full.append(dy[:, gsl] * y_off - wgt)
            dlast.append(jnp.sum(wgt, axis=0, keepdims=True)
                         + jnp.sum(prev_g * gst, axis=0, keepdims=True) * cd[:, gsl])
            g_ref[:, gsl] = dprev + cd[:, gsl] * gst
            dxdt_g = dxd * dte[:, gsl]
            cb = _dot(cg, bg, NT)
            dcb = jnp.zeros((l, l), F32)
            for q in range(pairs_per_group):
                pair = g * pairs_per_group + q
                psl = slice(pair * LANES, (pair + 1) * LANES)
                xpb = xdt[:, psl].astype(BF16)
                dyp = dy[:, psl].astype(BF16)
                zero = jnp.zeros_like(dyp)
                halves = (jnp.where(lane < HEAD_DIM, dyp, zero), jnp.where(lane >= HEAD_DIM, dyp, zero))
                acc = dxdt_g[:, q * LANES:(q + 1) * LANES]
                for k, dyh in enumerate(halves):
                    h = 2 * pair + k
                    dk = _decay_mats(acs, acs_t, h, mask)
                    mm = cb * dk
                    dm = _dot(dyh, xpb, NT)
                    dcb = dcb + dm * dk
                    dseg = dm * mm
                    dacs = dacs + jnp.where(hlane == h, jnp.sum(dseg, axis=1, keepdims=True), 0.0)
                    dacs_t = dacs_t + jnp.where(hsub == h, jnp.sum(dseg, axis=0, keepdims=True), 0.0)
                    acc = acc + _dot(mm.astype(BF16), dyh, TN)
                dxdt_all.append(acc)
            dcb_b = dcb.astype(BF16)
            dc_all.append(dc + _dot(dcb_b, bg))
            db_all.append(db + _dot(dcb_b, cg, TN))

        dxdt = jnp.concatenate(dxdt_all, axis=1)
        dxbc_ref[:, :d_inner] = (dy * dsk + dxdt * dt_full).astype(dxbc_ref.dtype)
        dxbc_ref[:, d_inner:d_inner + gn_cols] = jnp.concatenate(db_all, axis=1).astype(dxbc_ref.dtype)
        dxbc_ref[:, d_inner + gn_cols:] = jnp.concatenate(dc_all, axis=1).astype(dxbc_ref.dtype)
        ddt = _exact_r(dxdt * xs, ex, NT)
        rowi = _iota((l, d_inner), 0)
        dacs_f = jnp.concatenate(dacs_full, axis=1) + jnp.where(rowi == l - 1, jnp.concatenate(dlast, axis=1), 0.0)
        dacs = dacs + _exact_r(dacs_f, ex, NT) - dacs_t.T
        da = _exact_l(tri, dacs, TN)
        ddt = ddt + da * apad
        dalog_ref[...] += jnp.sum(da * dt, axis=0, keepdims=True) * apad
        ddt_raw = ddt * _sigmoid(dtraw_v + bias)
        dbias_ref[...] += jnp.sum(ddt_raw, axis=0, keepdims=True)
        ddt_ref[...] = ddt_raw.astype(ddt_ref.dtype)

        @pl.when(step == last_step)
        def _():
            dskh_ref[...] = _exact_r(dskacc_ref[...], ex, NT)[0:1, :]
            xch.wait(p_refs, got_refs, sems)

    row = lambda b, c: b * nc + (nc - 1 - c)
    small = lambda n: pl.BlockSpec((1, n), lambda b, c: (0, 0))
    wide = pl.BlockSpec((l, d_inner), lambda b, c: (row(b, c), 0))
    return _pallas(
        body, name="ssd_bwd", grid=(nb, nc),
        in_specs=[ANY, wide, wide, wide,
                  pl.BlockSpec((l, gn_cols), lambda b, c: (row(b, c), d_inner // gn_cols)),
                  pl.BlockSpec((l, gn_cols), lambda b, c: (row(b, c), d_inner // gn_cols + 1)),
                  wide, pl.BlockSpec((l, HEAD_PAD), lambda b, c: (row(b, c), 0)),
                  pl.BlockSpec((1, D_STATE, d_inner), lambda b, c: (row(b, c), 0, 0)),
                  small(HEAD_PAD), small(HEAD_PAD), small(d_inner), small(d_inner)] + xch.in_specs,
        out_specs=[wide, pl.BlockSpec((l, n_conv), lambda b, c: (row(b, c), 0)),
                   pl.BlockSpec((l, HEAD_PAD), lambda b, c: (row(b, c), 0)),
                   small(d_inner), small(HEAD_PAD), small(HEAD_PAD), small(HEAD_PAD)] + xch.out_specs,
        out_shape=[jax.ShapeDtypeStruct(dproj.shape, dproj.dtype),
                   jax.ShapeDtypeStruct((nb * s, n_conv), ACT),
                   jax.ShapeDtypeStruct((nb * s, HEAD_PAD), BF16),
                   jax.ShapeDtypeStruct((1, d_inner), F32), jax.ShapeDtypeStruct((1, HEAD_PAD), F32),
                   jax.ShapeDtypeStruct((1, HEAD_PAD), F32), jax.ShapeDtypeStruct((1, HEAD_PAD), F32)]
        + xch.out_shape,
        scratch_shapes=[pltpu.VMEM((D_STATE, d_inner), F32), pltpu.VMEM((8, d_inner), F32)] + xch.scratch,
        input_output_aliases={0: 0},
        compiler_params=_cparams(("arbitrary", "arbitrary"), 56),
    )(dproj, dya3, y, xbc, xbc, xbc, proj, dtraw, prev, dt_bias, a_log, dskip_full, gnorm, *partials)


def _conv_bwd(dproj, dxbc, conv_pre, proj, conv_w, nb, s, off_xbc):
    n_conv = dxbc.shape[1]
    cw = 256

    rc = CONV_ROWS

    def fold(v):
        return jnp.sum(v.reshape(rc // SUBLANES, SUBLANES, cw), axis=0)

    def body(_, d_ref, pre_ref, x_ref, w_ref, dx_ref, dw_ref, db_ref, dcpad_ref):
        @pl.when(pl.program_id(1) == 0)
        def _():
            dw_ref[...] = jnp.zeros_like(dw_ref)
            db_ref[...] = jnp.zeros_like(db_ref)

        dcpad_ref[pl.ds(s, SUBLANES), :] = jnp.zeros((SUBLANES, cw), F32)
        taps = [w_ref[k:k + 1, :] for k in range(CONV_W)]
        db = jnp.zeros((SUBLANES, cw), F32)
        for r in range(s // rc):
            rows = pl.ds(r * rc, rc)
            acc = pre_ref[rows, :].astype(F32)
            sg = _sigmoid(acc)
            dc = d_ref[rows, :].astype(F32) * (sg * (1.0 + acc * (1.0 - sg)))
            dcpad_ref[rows, :] = dc
            db = db + fold(dc)
        db_ref[...] += jnp.sum(db, axis=0, keepdims=True)
        dw = [jnp.zeros((SUBLANES, cw), F32) for _ in range(CONV_W)]
        for r in range(s // rc):
            xv = x_ref[pl.ds(r * rc, rc), :].astype(F32)
            win = dcpad_ref[pl.ds(r * rc, rc + SUBLANES), :]
            dxv = jnp.zeros((rc, cw), F32)
            for j in range(CONV_W):
                up = win[:rc, :] if j == 0 else pltpu.roll(win, rc + SUBLANES - j, 0)[:rc, :]
                dxv = dxv + up * taps[CONV_W - 1 - j]
                dw[CONV_W - 1 - j] = dw[CONV_W - 1 - j] + fold(up * xv)
            dx_ref[pl.ds(r * rc, rc), :] = dxv.astype(dx_ref.dtype)
        for k in range(CONV_W):
            dw_ref[k:k + 1, :] += jnp.sum(dw[k], axis=0, keepdims=True)

    blk = pl.BlockSpec((s, cw), lambda j, b: (b, j))
    return _pallas(
        body, name="conv_bwd", grid=(n_conv // cw, nb),
        in_specs=[ANY, blk, blk, pl.BlockSpec((s, cw), lambda j, b: (b, off_xbc // cw + j)),
                  pl.BlockSpec((CONV_W, cw), lambda j, b: (0, j))],
        out_specs=[pl.BlockSpec((s, cw), lambda j, b: (b, off_xbc // cw + j)),
                   pl.BlockSpec((CONV_W, cw), lambda j, b: (0, j)), pl.BlockSpec((1, cw), lambda j, b: (0, j))],
        out_shape=[jax.ShapeDtypeStruct(dproj.shape, dproj.dtype),
                   jax.ShapeDtypeStruct((CONV_W, n_conv), F32), jax.ShapeDtypeStruct((1, n_conv), F32)],
        scratch_shapes=[pltpu.VMEM((s + SUBLANES, cw), F32)],
        input_output_aliases={0: 0},
        compiler_params=_cparams(("parallel", "arbitrary"), 48),
    )(dproj, dxbc, conv_pre, proj, conv_w)


def _in_bwd(dproj, w_main, ddt, w_dt, x, norm_g, dx1, partials):
    t, d = x.shape
    n_main = dproj.shape[1]
    tm = 256
    ni = t // tm
    xch = _ChipExchange(partials)
    n_in, n_out = 7, 2

    def body(*refs):
        dp_ref, w_ref, ddt_ref, wdt_ref, x_ref, g_ref, dx1_ref = refs[:n_in]
        p_refs = refs[n_in:n_in + xch.n]
        outs = refs[n_in + xch.n:]
        dx_ref, dg_ref = outs[:n_out]
        got_refs = outs[n_out:n_out + xch.n]
        sems = outs[n_out + xch.n:]
        i = pl.program_id(0)

        @pl.when(i == 0)
        def _():
            xch.start(p_refs, got_refs, sems)
            dg_ref[...] = jnp.zeros_like(dg_ref)

        dh = _dot(dp_ref[...], w_ref[...], NT) + _dot(ddt_ref[...], wdt_ref[...], NT)
        xv = x_ref[...]
        r = lax.rsqrt(jnp.mean(xv * xv, axis=-1, keepdims=True) + EPS)
        xh = xv * r
        dg_ref[...] += jnp.sum(dh * xh, axis=0, keepdims=True)
        dg = dh * g_ref[...]
        dx_ref[...] = dx1_ref[...] + r * (dg - xh * jnp.mean(dg * xh, axis=-1, keepdims=True))

        @pl.when(i == ni - 1)
        def _():
            xch.wait(p_refs, got_refs, sems)

    rowblk = pl.BlockSpec((tm, d), lambda i: (i, 0))
    vec = pl.BlockSpec((1, d), lambda i: (0, 0))
    resident = lambda shape: pl.BlockSpec(shape, lambda i: (0, 0), pipeline_mode=pl.Buffered(1))
    return _pallas(
        body, name="in_bwd", grid=(ni,),
        in_specs=[pl.BlockSpec((tm, n_main), lambda i: (i, 0)), resident((d, n_main)),
                  pl.BlockSpec((tm, HEAD_PAD), lambda i: (i, 0)), resident((d, HEAD_PAD)),
                  rowblk, vec, rowblk] + xch.in_specs,
        out_specs=[rowblk, vec] + xch.out_specs,
        out_shape=[jax.ShapeDtypeStruct((t, d), F32), jax.ShapeDtypeStruct((1, d), F32)] + xch.out_shape,
        scratch_shapes=xch.scratch,
        compiler_params=_cparams(("arbitrary",), 56),
    )(dproj, w_main, ddt, w_dt, x, norm_g, dx1, *partials)


def _adamw_math(w, g, m, v):
    m = ADAM_B1 * m + (1.0 - ADAM_B1) * g
    v = ADAM_B2 * v + (1.0 - ADAM_B2) * (g * g)
    m_hat = m / (1.0 - ADAM_B1 ** ADAM_STEP)
    v_hat = v / (1.0 - ADAM_B2 ** ADAM_STEP)
    delta = -ADAM_LR * (m_hat / (jnp.sqrt(v_hat) + ADAM_EPS) + ADAM_WD * w)
    return delta, m, v


def _adamw(parts, w, m, v, name):
    n, rows, cols = parts.shape
    tr = _row_tile(rows, 128) if rows % 8 == 0 else rows

    def body(p_ref, w_ref, m_ref, v_ref, g_ref, d_ref, mo_ref, vo_ref):
        g = p_ref[0].astype(F32)
        for k in range(1, n):
            g = g + p_ref[k].astype(F32)
        d, mn, vn = _adamw_math(w_ref[...], g, m_ref[...], v_ref[...])
        g_ref[...] = g
        d_ref[...] = d
        mo_ref[...] = mn
        vo_ref[...] = vn

    blk = pl.BlockSpec((tr, cols), lambda i: (i, 0))
    shp = jax.ShapeDtypeStruct((rows, cols), F32)
    return _pallas(
        body, name=name, grid=(rows // tr,),
        in_specs=[pl.BlockSpec((n, tr, cols), lambda i: (0, i, 0)), blk, blk, blk],
        out_specs=[blk, blk, blk, blk], out_shape=[shp, shp, shp, shp],
        compiler_params=_cparams(("parallel",), 56),
    )(parts, w, m, v)


def _coords():
    return lax.axis_index("x"), lax.axis_index("y"), lax.axis_index("c")


class _TwoLevelGather:
    def __init__(self, blocks):
        self.n = len(blocks)
        self.in_specs = [ANY] * self.n
        self.out_specs = [ANY] * self.n
        self.out_shape = [jax.ShapeDtypeStruct((N_DEV,) + b.shape, b.dtype) for b in blocks]
        self.scratch = [pltpu.SemaphoreType.DMA((self.n, 7)), pltpu.SemaphoreType.DMA((self.n, 7)),
                        pltpu.SemaphoreType.DMA((self.n,))]

    def _parts(self, x_refs, out_refs, sems):
        send_sems, recv_sems, local_sems = sems
        x, y, c = _coords()
        me, sibling = (x, y, c), (x, y, 1 - c)
        chips = [(1 - x, y), (x, 1 - y), (1 - x, 1 - y)]

        def slot(a, px, py, pc):
            return out_refs[a].at[4 * px + 2 * py + pc]

        def copy(a, k, block, to, src=None):
            return pltpu.make_async_remote_copy(
                src_ref=slot(a, *block) if src is None else src, dst_ref=slot(a, *block),
                send_sem=send_sems.at[a, k], recv_sem=recv_sems.at[a, k], device_id=to, device_id_type=MESH)

        local = [pltpu.make_async_copy(x_refs[a], slot(a, *me), local_sems.at[a]) for a in range(self.n)]
        first = []
        for a in range(self.n):
            first.append(copy(a, 0, me, sibling, src=x_refs[a]))
            first += [copy(a, 1 + j, me, (*chip, c), src=x_refs[a]) for j, chip in enumerate(chips)]
        return copy, local, first, chips, me, sibling, c

    def start(self, x_refs, out_refs, sems):
        _, local, first, *_ = self._parts(x_refs, out_refs, sems)
        for cp in local + first:
            cp.start()

    def finish(self, x_refs, out_refs, sems):
        copy, local, first, chips, me, sibling, c = self._parts(x_refs, out_refs, sems)
        passed = []
        for j, chip in enumerate(chips):
            for a in range(self.n):
                copy(a, 1 + j, (*chip, c), me).wait_recv()
                passed.append(copy(a, 4 + j, (*chip, c), sibling))
                passed[-1].start()
        for a in range(self.n):
            copy(a, 0, sibling, me).wait_recv()
            for j, chip in enumerate(chips):
                copy(a, 4 + j, (*chip, 1 - c), me).wait_recv()
        for cp in first + passed:
            cp.wait_send()
        for cp in local:
            cp.wait()


def _all_gather_routed(blk, name):
    rows, cols = blk.shape
    half = rows // 2
    assert half % 16 == 0

    def body(x_ref, out_ref, send_sems, recv_sems, local_sem):
        x, y, c = _coords()
        me, sib, xn, yn, dg = (x, y, c), (x, y, 1 - c), (1 - x, y, c), (x, 1 - y, c), (1 - x, 1 - y, c)
        other = lambda p: (p[0], p[1], 1 - c)
        top, bot = pl.ds(0, half), pl.ds(half, half)

        def slot(p, part=None):
            ref = out_ref.at[4 * p[0] + 2 * p[1] + p[2]]
            return ref if part is None else ref.at[part]

        def copy(k, block, to, part=None, src=None):
            return pltpu.make_async_remote_copy(
                src_ref=slot(block, part) if src is None else src, dst_ref=slot(block, part),
                send_sem=send_sems.at[k], recv_sem=recv_sems.at[k], device_id=to, device_id_type=MESH)

        mine = pltpu.make_async_copy(x_ref, slot(me), local_sem)
        own = [copy(1, me, xn, src=x_ref), copy(2, me, yn, src=x_ref), copy(0, me, sib, src=x_ref)]
        for cp in [mine] + own:
            cp.start()
        copy(1, xn, me).wait_recv()
        fwd = [copy(3, xn, yn, top), copy(5, xn, sib)]
        for cp in fwd:
            cp.start()
        copy(2, yn, me).wait_recv()
        fwd += [copy(4, yn, xn, bot), copy(6, yn, sib)]
        for cp in fwd[2:]:
            cp.start()
        copy(3, dg, me, top).wait_recv()
        fwd.append(copy(7, dg, sib, top))
        fwd[-1].start()
        copy(4, dg, me, bot).wait_recv()
        fwd.append(copy(8, dg, sib, bot))
        fwd[-1].start()
        copy(0, sib, me).wait_recv()
        copy(5, other(xn), me).wait_recv()
        copy(6, other(yn), me).wait_recv()
        copy(7, other(dg), me, top).wait_recv()
        copy(8, other(dg), me, bot).wait_recv()
        for cp in own + fwd:
            cp.wait_send()
        mine.wait()

    return _pallas(
        body, name=name, in_specs=[ANY], out_specs=ANY,
        out_shape=jax.ShapeDtypeStruct((N_DEV, rows, cols), blk.dtype),
        scratch_shapes=[pltpu.SemaphoreType.DMA((9,)), pltpu.SemaphoreType.DMA((9,)), pltpu.SemaphoreType.DMA],
    )(blk)


def _all_gather_direct(blk, name):
    rows, cols = blk.shape

    def body(x_ref, out_ref, send_sems, recv_sems, local_sem):
        x, y, c = _coords()
        me = 4 * x + 2 * y + c
        mine = pltpu.make_async_copy(x_ref, out_ref.at[me], local_sem)
        mine.start()
        copies = []
        for k in range(1, N_DEV):
            fx, fy, fc = (k >> 2) & 1, (k >> 1) & 1, k & 1
            peer = (x ^ fx, y ^ fy, c ^ fc)
            copies.append(pltpu.make_async_remote_copy(
                src_ref=x_ref, dst_ref=out_ref.at[me], send_sem=send_sems.at[k - 1], recv_sem=recv_sems.at[k - 1],
                device_id=peer, device_id_type=MESH))
        for cp in copies:
            cp.start()
        for k in range(1, N_DEV):
            fx, fy, fc = (k >> 2) & 1, (k >> 1) & 1, k & 1
            src = 4 * (x ^ fx) + 2 * (y ^ fy) + (c ^ fc)
            pltpu.make_async_remote_copy(
                src_ref=x_ref, dst_ref=out_ref.at[src], send_sem=send_sems.at[k - 1], recv_sem=recv_sems.at[k - 1],
                device_id=(x, y, c), device_id_type=MESH).wait_recv()
        for cp in copies:
            cp.wait_send()
        mine.wait()

    return _pallas(
        body, name=name, in_specs=[ANY], out_specs=ANY,
        out_shape=jax.ShapeDtypeStruct((N_DEV, rows, cols), blk.dtype),
        scratch_shapes=[pltpu.SemaphoreType.DMA((7,)), pltpu.SemaphoreType.DMA((7,)), pltpu.SemaphoreType.DMA],
    )(blk)


def _pair_exchange(grads, name):
    n = len(grads)

    def body(*refs):
        g_refs, out_refs = refs[:n], refs[n:2 * n]
        send_sems, recv_sems = refs[2 * n:]
        x, y, c = _coords()
        copies = []
        for a in range(n):
            for k in range(4):
                copies.append(pltpu.make_async_remote_copy(
                    src_ref=g_refs[a].at[2 * k + (1 - c)], dst_ref=out_refs[a].at[k],
                    send_sem=send_sems.at[a, k], recv_sem=recv_sems.at[a, k],
                    device_id=(x, y, 1 - c), device_id_type=MESH))
        for cp in copies:
            cp.start()
        for cp in copies:
            cp.wait()

    return _pallas(
        body, name=name, in_specs=[ANY] * n, out_specs=[ANY] * n,
        out_shape=[jax.ShapeDtypeStruct((4,) + g.shape[1:], g.dtype) for g in grads],
        scratch_shapes=[pltpu.SemaphoreType.DMA((n, 4)), pltpu.SemaphoreType.DMA((n, 4))],
    )(*grads)


def _pair_add(grads, got, core, out_dtype, name):
    _, rows, cols = grads.shape
    tr = _row_tile(rows, 256) if rows % 8 == 0 else rows

    def body(c_ref, a_ref, b_ref, o_ref):
        o_ref[...] = (a_ref[...].astype(F32) + b_ref[...].astype(F32)).astype(o_ref.dtype)

    grid_spec = pltpu.PrefetchScalarGridSpec(
        num_scalar_prefetch=1, grid=(4, rows // tr),
        in_specs=[pl.BlockSpec((1, tr, cols), lambda k, i, c_ref: (2 * k + c_ref[0], i, 0)),
                  pl.BlockSpec((1, tr, cols), lambda k, i, c_ref: (k, i, 0))],
        out_specs=pl.BlockSpec((1, tr, cols), lambda k, i, c_ref: (k, i, 0)))
    return _pallas(
        body, name=name, grid_spec=grid_spec,
        out_shape=jax.ShapeDtypeStruct((4, rows, cols), out_dtype),
        compiler_params=_cparams(("parallel", "parallel"), 48),
    )(core, grads, got)


class _ChipExchange:
    def __init__(self, partials):
        self.n = len(partials)
        self.in_specs = [ANY] * self.n
        self.out_specs = [ANY] * self.n
        self.out_shape = [jax.ShapeDtypeStruct(p.shape, p.dtype) for p in partials]
        self.scratch = [pltpu.SemaphoreType.DMA((self.n, 3)), pltpu.SemaphoreType.DMA((self.n, 3)),
                        pltpu.SemaphoreType.DMA((self.n,))]

    def _copies(self, p_refs, out_refs, sems, receiving):
        send_sems, recv_sems, local_sems = sems
        x, y, c = _coords()
        mychip = 2 * x + y
        local = [pltpu.make_async_copy(p_refs[a].at[mychip], out_refs[a].at[mychip], local_sems.at[a])
                 for a in range(self.n)]
        remote = []
        for a in range(self.n):
            for k in range(1, 4):
                px, py = x ^ ((k >> 1) & 1), y ^ (k & 1)
                peer = 2 * px + py
                remote.append(pltpu.make_async_remote_copy(
                    src_ref=p_refs[a].at[peer], dst_ref=out_refs[a].at[peer if receiving else mychip],
                    send_sem=send_sems.at[a, k - 1], recv_sem=recv_sems.at[a, k - 1],
                    device_id=(x, y, c) if receiving else (px, py, c), device_id_type=MESH))
        return local, remote

    def start(self, p_refs, out_refs, sems):
        local, sends = self._copies(p_refs, out_refs, sems, False)
        for cp in local + sends:
            cp.start()

    def wait(self, p_refs, out_refs, sems):
        local, recvs = self._copies(p_refs, out_refs, sems, True)
        for cp in recvs:
            cp.wait_recv()
        for cp in recvs:
            cp.wait_send()
        for cp in local:
            cp.wait()


class _DirectExchange:
    def __init__(self, partials):
        self.n = len(partials)
        self.in_specs = [ANY] * self.n
        self.out_specs = [ANY] * self.n
        self.out_shape = [jax.ShapeDtypeStruct(p.shape, p.dtype) for p in partials]
        self.scratch = [pltpu.SemaphoreType.DMA((self.n, N_DEV - 1)), pltpu.SemaphoreType.DMA((self.n, N_DEV - 1)),
                        pltpu.SemaphoreType.DMA((self.n,))]

    def _copies(self, p_refs, out_refs, sems, receiving):
        send_sems, recv_sems, local_sems = sems
        x, y, c = _coords()
        me = 4 * x + 2 * y + c
        local = [pltpu.make_async_copy(p_refs[a].at[me], out_refs[a].at[me], local_sems.at[a])
                 for a in range(self.n)]
        remote = []
        for a in range(self.n):
            for k in range(1, N_DEV):
                px, py, pc = x ^ ((k >> 2) & 1), y ^ ((k >> 1) & 1), c ^ (k & 1)
                peer = 4 * px + 2 * py + pc
                remote.append(pltpu.make_async_remote_copy(
                    src_ref=p_refs[a].at[peer], dst_ref=out_refs[a].at[peer if receiving else me],
                    send_sem=send_sems.at[a, k - 1], recv_sem=recv_sems.at[a, k - 1],
                    device_id=(x, y, c) if receiving else (px, py, pc), device_id_type=MESH))
        return local, remote

    def start(self, p_refs, out_refs, sems):
        local, sends = self._copies(p_refs, out_refs, sems, False)
        for cp in local + sends:
            cp.start()

    def wait(self, p_refs, out_refs, sems):
        local, recvs = self._copies(p_refs, out_refs, sems, True)
        for cp in recvs:
            cp.wait_recv()
        for cp in recvs:
            cp.wait_send()
        for cp in local:
            cp.wait()


def _pad_lanes(v, n=HEAD_PAD):
    return jnp.pad(v, ((0, 0), (0, n - v.shape[1])))


def kernel(x, p, norm_g, w_in, conv_w, conv_b, dt_bias, a_log, d_skip, gnorm_g, pool_mix_w, pool_mix_b, pool_scale, w_branch_a, w_branch_b, w_out, ple_norm_g, w_ple_gate, w_ple_up, final_g, loss_target, m_norm_g, m_w_in, m_conv_w, m_conv_b, m_dt_bias, m_a_log, m_d_skip, m_gnorm_g, m_pool_mix_w, m_pool_mix_b, m_pool_scale, m_w_branch_a, m_w_branch_b, m_w_out, m_ple_norm_g, m_w_ple_gate, m_w_ple_up, m_final_g, v_norm_g, v_w_in, v_conv_w, v_conv_b, v_dt_bias, v_a_log, v_d_skip, v_gnorm_g, v_pool_mix_w, v_pool_mix_b, v_pool_scale, v_w_branch_a, v_w_branch_b, v_w_out, v_ple_norm_g, v_w_ple_gate, v_w_ple_up, v_final_g):
    nb, s, d = x.shape
    t = nb * s
    n_heads = dt_bias.shape[1]
    d_inner = n_heads * HEAD_DIM
    n_conv = conv_b.shape[1]
    d_pool = pool_mix_b.shape[1]
    n_in = w_in.shape[2] * N_DEV
    ple_dim = p.shape[-1]
    o_xbc, o_dt = d_inner, d_inner + n_conv
    o_u = o_dt + n_heads
    o_zp = o_u + d_pool
    o_ga, o_gb = o_zp + d_pool, o_zp + d_pool + d
    n_main = n_in - n_heads
    off_ga, off_gb, off_xbc = d_inner, d_inner + d, d_inner + 2 * d
    off_u = off_xbc + n_conv
    off_zp = off_u + d_pool

    sharded = [w_in[0], conv_w[0], pool_mix_w[0], w_branch_a[0], w_branch_b[0], w_out[0], w_ple_gate[0], w_ple_up[0]]
    sharded_m = [m_w_in[0], m_conv_w[0], m_pool_mix_w[0], m_w_branch_a[0], m_w_branch_b[0], m_w_out[0],
                 m_w_ple_gate[0], m_w_ple_up[0]]
    sharded_v = [v_w_in[0], v_conv_w[0], v_pool_mix_w[0], v_w_branch_a[0], v_w_branch_b[0], v_w_out[0],
                 v_w_ple_gate[0], v_w_ple_up[0]]
    ng = pool_mix_w.shape[1]
    pg = d_pool // ng
    two_d = lambda w: w.reshape(-1, w.shape[-1])
    sharded, sharded_m, sharded_v = ([two_d(w) for w in ws] for ws in (sharded, sharded_m, sharded_v))

    send = [w if i == 1 else w.astype(BF16) for i, w in enumerate(sharded)]
    seg = [_all_gather_routed(send[0], "w_in_all_gather")]
    chunk = n_in // N_DEV
    segments = [((0, o_xbc), 0), ((o_ga, o_gb), off_ga), ((o_gb, n_in), off_gb), ((o_xbc, o_dt), off_xbc),
                ((o_u, o_zp), off_u), ((o_zp, o_ga), off_zp)]

    def gathered_cols(lo, hi):
        out = []
        while lo < hi:
            dev = lo // chunk
            a, b = lo - dev * chunk, min(hi - dev * chunk, chunk)
            out.append(seg[0][dev, :, a:b])
            lo = dev * chunk + b
        return out

    w_main = jnp.concatenate([pc for (lo, hi), _ in segments for pc in gathered_cols(lo, hi)], axis=1)
    w_dt = _pad_lanes(jnp.concatenate(gathered_cols(o_dt, o_u), axis=1))

    x2 = x.reshape(t, d)
    h = _rms_fwd(x2, norm_g, "rms_in")
    proj, *rest = _in_proj(h, w_main, send[1:])
    seg += rest
    cw_full = seg[1].transpose(1, 0, 2).reshape(CONV_W, n_conv)
    mixw_full = seg[2].reshape(N_DEV, ng, pg // N_DEV, pg).transpose(1, 0, 2, 3).reshape(ng, pg, pg)
    wa_full = seg[3].reshape(d_inner, d)
    wb_full = seg[4].reshape(d_pool, d)
    wout_full = seg[5].reshape(d, d)
    wg_full = seg[6].reshape(d, d)
    wup_full = seg[7].transpose(1, 0, 2).reshape(ple_dim, d)

    p2 = p.reshape(t, ple_dim)
    tgt = loss_target.reshape(t, d)
    bias_p, alog_p = _pad_lanes(dt_bias), _pad_lanes(a_log)
    dskip_full = jnp.repeat(d_skip, HEAD_DIM, axis=1)

    dtraw = _matmul(h, w_dt, "nn", F32, "in_proj_dt", 1024, HEAD_PAD, d)
    xbc, conv_pre = _conv_fwd(proj, cw_full, conv_b, nb, s, off_xbc, n_conv)
    y, ya3, prev = _ssd_fwd(xbc, proj, dtraw, bias_p, alog_p, dskip_full, gnorm_g, nb, s, d_inner)
    pooled, mixed, yb = _pool_fwd(proj, mixw_full, pool_mix_b, pool_scale, nb, s, off_u, off_zp, d_pool)
    (merged, h2, dx1, dx1b, dgl, dpu, dpa, dpb, dproj, loss_part, dfinal_g, d_ple_g) = _merge_to_grads(
        ya3, wa_full, yb, wb_full, proj, off_ga, off_gb, wout_full, x2, ple_norm_g, wg_full, p2, wup_full, tgt,
        final_g.reshape(1, d), n_main)

    g_wup = _matmul(p2, dpu, "tn", BF16, "dw_ple_up", 256, 512, t)
    g_wg = _matmul(h2, dgl, "tn", BF16, "dw_ple_gate", 512, 512, t)
    g_wout = _matmul(merged, dx1b, "tn", BF16, "dw_out", 512, 512, t)
    g_wa = _matmul(ya3, dpa, "tn", BF16, "dw_branch_a", 512, 512, t)
    g_wb = _matmul(yb, dpb, "tn", BF16, "dw_branch_b", 512, 512, t)
    dya3 = _matmul(dpa, wa_full, "nt", ACT, "d_ya3", 1024, 1024, d)
    dproj, dmixed, d_scale, d_mixb = _yb_bwd(dproj, dpb, wb_full, proj, mixed, pool_mix_b, pool_scale, off_zp)
    g_mixw = _dmixw(pooled, dmixed, ng)
    dproj = _pool_bwd(dproj, dmixed, mixw_full, nb, s, off_u)

    core = lax.axis_index("c").astype(I32).reshape(1)

    def pair_sums(by_dest, names):
        by_dest = [g.astype(BF16) for g in by_dest]
        got = _pair_exchange(by_dest, "rs_pair_exchange_" + names[0])
        return [_pair_add(g, r, core, BF16, "rs_pair_add_" + nm) for g, r, nm in zip(by_dest, got, names)]

    names_a = ["pool_mix_w", "w_branch_a", "w_branch_b", "w_out", "w_ple_gate", "w_ple_up"]
    partials_a = [
        g_mixw.reshape(ng, N_DEV, pg // N_DEV, pg).transpose(1, 0, 2, 3).reshape(N_DEV, ng * pg // N_DEV, pg)
        .astype(BF16),
        g_wa.reshape(N_DEV, d_inner // N_DEV, d), g_wb.reshape(N_DEV, d_pool // N_DEV, d),
        g_wout.reshape(N_DEV, d // N_DEV, d), g_wg.reshape(N_DEV, d // N_DEV, d),
        g_wup.reshape(ple_dim, N_DEV, d // N_DEV).transpose(1, 0, 2)]
    dproj, dxbc, ddt, d_gn, d_dsk, d_alog, d_dtb, *parts_a = _ssd_bwd(
        dproj, dya3, y, xbc, proj, dtraw, prev, bias_p, alog_p, dskip_full, gnorm_g, nb, s, d_inner, partials_a)
    dproj, g_cw, d_cb = _conv_bwd(dproj, dxbc, conv_pre, proj, cw_full, nb, s, off_xbc)
    g_wmain = _matmul(h, dproj, "tn", BF16, "dw_in", 512, 1024, t)
    g_wdt = _matmul(h, ddt, "tn", BF16, "dw_in_dt", 512, HEAD_PAD, t)

    def dest_cols(dev):
        lo, hi = dev * chunk, (dev + 1) * chunk
        srcs = [((olo, ohi), g_wmain, off) for (olo, ohi), off in segments] + [((o_dt, o_u), g_wdt, 0)]
        out = []
        for (olo, ohi), arr, off in sorted(srcs, key=lambda e: e[0][0]):
            a, b = max(lo, olo), min(hi, ohi)
            if a < b:
                out.append(arr[:, off + a - olo:off + b - olo])
        return jnp.concatenate(out, axis=1)

    names_b = ["w_in", "conv_w"]
    partials_b = pair_sums([jnp.stack([dest_cols(dev) for dev in range(N_DEV)], axis=0),
                            g_cw.reshape(CONV_W, N_DEV, n_conv // N_DEV).transpose(1, 0, 2)], names_b)
    grad_x, d_ng, *parts_b = _in_bwd(dproj, w_main, ddt, w_dt, x2, norm_g, dx1, partials_b)

    names = names_b + names_a
    parts = parts_b + parts_a
    sh_g, sh_d, sh_m, sh_v = [], [], [], []
    shapes = [w_in.shape, conv_w.shape, pool_mix_w.shape, w_branch_a.shape, w_branch_b.shape, w_out.shape,
              w_ple_gate.shape, w_ple_up.shape]
    for pt, w, m, v, nm, shp in zip(parts, sharded, sharded_m, sharded_v, names, shapes):
        res = _adamw(pt, w, m, v, "adamw_" + nm)
        for dst, r in zip((sh_g, sh_d, sh_m, sh_v), res):
            dst.append(r.reshape(shp))

    rep_w = [norm_g, conv_b, gnorm_g, pool_mix_b, pool_scale, ple_norm_g, final_g.reshape(1, d),
             bias_p, alog_p, _pad_lanes(d_skip)]
    rep_m = [m_norm_g, m_conv_b, m_gnorm_g, m_pool_mix_b, m_pool_scale, m_ple_norm_g, m_final_g.reshape(1, d),
             _pad_lanes(m_dt_bias), _pad_lanes(m_a_log), _pad_lanes(m_d_skip)]
    rep_v = [v_norm_g, v_conv_b, v_gnorm_g, v_pool_mix_b, v_pool_scale, v_ple_norm_g, v_final_g.reshape(1, d),
             _pad_lanes(v_dt_bias), _pad_lanes(v_a_log), _pad_lanes(v_d_skip)]
    rep_g = [d_ng, d_cb, d_gn, d_mixb, d_scale, d_ple_g, dfinal_g, d_dtb, d_alog, d_dsk]
    zero = jnp.zeros((1, LANES), F32)
    small = _all_gather_direct(jnp.concatenate(rep_g + [loss_part], axis=1), "small_all_gather")
    pk = lambda ws: jnp.concatenate(ws + [zero], axis=1)
    rg, rd, rm, rv = _adamw(small, pk(rep_w), pk(rep_m), pk(rep_v), "adamw_replicated")
    loss = rg[0, sum(int(w.size) for w in rep_w)]

    def unpack_rep(buf):
        out, pos_ = [], 0
        for w in rep_w:
            out.append(buf[:, pos_:pos_ + w.size])
            pos_ += w.size
        return out

    def arrange(sh, rep):
        rep = list(rep)
        rep[6] = rep[6].reshape(d)
        for i in (7, 8, 9):
            rep[i] = rep[i][:, :n_heads]
        return [rep[0], sh[0], sh[1], rep[1], rep[7], rep[8], rep[9], rep[2], sh[2], rep[3], rep[4],
                sh[3], sh[4], sh[5], rep[5], sh[6], sh[7], rep[6]]

    outs = [loss, grad_x.reshape(nb, s, d)]
    for sh, rep in ((sh_g, rg), (sh_d, rd), (sh_m, rm), (sh_v, rv)):
        outs += arrange(sh, unpack_rep(rep))
    return tuple(outs)
```

```python
import jax
import jax.numpy as jnp
from jax import lax
from jax.experimental import pallas as pl
from jax.experimental.pallas import tpu as pltpu

F32 = jnp.float32
BF16 = jnp.bfloat16
I32 = jnp.int32
MESH = pl.DeviceIdType.MESH
ANY = pl.BlockSpec(memory_space=pl.ANY)

EPS = 1e-6
CHUNK = 128
HEAD_DIM = 64
D_STATE = 128
GROUPS = 4
CONV_W = 4
POOL_WINDOWS = (2, 4, 8, 16)
ACT = jnp.bfloat16
LANES = 128
SUBLANES = 8
CONV_ROWS = 64
HEAD_PAD = 128
NEG = -1e30
N_DEV = 8

ADAM_LR = 0.001
ADAM_B1 = 0.9
ADAM_B2 = 0.999
ADAM_EPS = 1e-08
ADAM_WD = 0.01
ADAM_STEP = 10

NN = (((1,), (0,)), ((), ()))
NT = (((1,), (1,)), ((), ()))
TN = (((0,), (0,)), ((), ()))


def _pallas(body, **kw):
    return pl.pallas_call(body, **kw)


def _cparams(dims=None, vmem_mb=None):
    kw = {}
    if dims is not None:
        kw["dimension_semantics"] = dims
    if vmem_mb is not None:
        kw["vmem_limit_bytes"] = vmem_mb << 20
    return pltpu.CompilerParams(**kw)


def _dot(a, b, dims=NN):
    return lax.dot_general(a, b, dims, preferred_element_type=F32)


def _split3(v):
    hi = v.astype(BF16)
    r = v - hi.astype(F32)
    mid = r.astype(BF16)
    lo = (r - mid.astype(F32)).astype(BF16)
    return hi, mid, lo


def _exact_r(v, m, dims=NN):
    hi, mid, lo = _split3(v)
    return _dot(hi, m, dims) + _dot(mid, m, dims) + _dot(lo, m, dims)


def _exact_l(m, v, dims=NN):
    hi, mid, lo = _split3(v)
    return _dot(m, hi, dims) + _dot(m, mid, dims) + _dot(m, lo, dims)


def _row_tile(rows, target):
    best = None
    for cand in range(8, min(rows, target) + 1, 8):
        if rows % cand == 0:
            best = cand
    assert best is not None, rows
    return best


def _sum_all(v):
    return jnp.sum(jnp.sum(v, axis=1, keepdims=True), axis=0, keepdims=True)


def _iota(shape, dim):
    return lax.broadcasted_iota(I32, shape, dim)


def _sigmoid(x):
    return 0.5 * jnp.tanh(0.5 * x) + 0.5


def _softplus(x):
    return jnp.maximum(x, 0.0) + jnp.log(1.0 + jnp.exp(-jnp.abs(x)))


def _shift_down(v, j, row):
    return jnp.where(row >= j, pltpu.roll(v, j, 0), 0.0)


def _shift_up(v, j, row):
    n = v.shape[0]
    return jnp.where(row < n - j, pltpu.roll(v, n - j, 0), 0.0)


def _matmul(a, b, mode, out_dtype, name, tm, tn, tk):
    if mode == "nn":
        (m, k), n = a.shape, b.shape[1]
    elif mode == "nt":
        (m, k), n = a.shape, b.shape[0]
    else:
        (k, m), n = a.shape, b.shape[1]
    tm, tn, tk = min(tm, m), min(tn, n), min(tk, k)
    assert m % tm == 0 and n % tn == 0 and k % tk == 0, (name, m, n, k)
    nk = k // tk
    dims = {"nn": NN, "nt": NT, "tn": TN}[mode]

    def body_acc(a_ref, b_ref, o_ref, acc_ref):
        kk = pl.program_id(2)

        @pl.when(kk == 0)
        def _():
            acc_ref[...] = jnp.zeros_like(acc_ref)

        acc_ref[...] += _dot(a_ref[...].astype(BF16), b_ref[...].astype(BF16), dims)

        @pl.when(kk == nk - 1)
        def _():
            o_ref[...] = acc_ref[...].astype(o_ref.dtype)

    def body_one(a_ref, b_ref, o_ref):
        o_ref[...] = _dot(a_ref[...].astype(BF16), b_ref[...].astype(BF16), dims).astype(o_ref.dtype)

    body = body_one if nk == 1 else body_acc
    if mode == "tn":
        a_spec = pl.BlockSpec((tk, tm), lambda i, j, kk: (kk, i))
    else:
        a_spec = pl.BlockSpec((tm, tk), lambda i, j, kk: (i, kk))
    if mode == "nt":
        b_spec = pl.BlockSpec((tn, tk), lambda i, j, kk: (j, kk))
    else:
        b_spec = pl.BlockSpec((tk, tn), lambda i, j, kk: (kk, j))
    return _pallas(
        body, name=name, grid=(m // tm, n // tn, nk),
        in_specs=[a_spec, b_spec],
        out_specs=pl.BlockSpec((tm, tn), lambda i, j, kk: (i, j)),
        out_shape=jax.ShapeDtypeStruct((m, n), out_dtype),
        scratch_shapes=[] if nk == 1 else [pltpu.VMEM((tm, tn), F32)],
        compiler_params=_cparams(("parallel", "parallel", "arbitrary"), 56),
    )(a, b)


def _in_proj(h, w_main, blocks):
    t, d = h.shape
    n = w_main.shape[1]
    tm, tn = min(1024, t), 1024
    ni, nj = t // tm, n // tn
    gather = _TwoLevelGather(blocks)

    def body(*refs):
        a_ref, b_ref = refs[:2]
        x_refs = refs[2:2 + gather.n]
        o_ref = refs[2 + gather.n]
        got_refs = refs[3 + gather.n:3 + 2 * gather.n]
        sems = refs[3 + 2 * gather.n:]
        i, j = pl.program_id(0), pl.program_id(1)

        @pl.when((i == 0) & (j == 0))
        def _():
            gather.start(x_refs, got_refs, sems)

        o_ref[...] = _dot(a_ref[...], b_ref[...]).astype(o_ref.dtype)

        @pl.when((i == ni - 1) & (j == nj - 1))
        def _():
            gather.finish(x_refs, got_refs, sems)

    return _pallas(
        body, name="in_proj", grid=(ni, nj),
        in_specs=[pl.BlockSpec((tm, d), lambda i, j: (i, 0)), pl.BlockSpec((d, tn), lambda i, j: (0, j))]
        + gather.in_specs,
        out_specs=[pl.BlockSpec((tm, tn), lambda i, j: (i, j))] + gather.out_specs,
        out_shape=[jax.ShapeDtypeStruct((t, n), ACT)] + gather.out_shape,
        scratch_shapes=gather.scratch,
        compiler_params=_cparams(("arbitrary", "arbitrary"), 56),
    )(h, w_main, *blocks)


def _conv_fwd(proj, conv_w, conv_b, nb, s, off_xbc, n_conv):
    cw = 256
    rc = CONV_ROWS

    def body(x_ref, w_ref, b_ref, o_ref, pre_ref, pad_ref):
        pad_ref[0:SUBLANES, :] = jnp.zeros((SUBLANES, cw), F32)
        pad_ref[SUBLANES:, :] = x_ref[...].astype(F32)
        taps = [w_ref[k:k + 1, :] for k in range(CONV_W)]
        bias = b_ref[...]
        for r in range(s // rc):
            win = pad_ref[pl.ds(r * rc, rc + SUBLANES), :]
            acc = bias + win[SUBLANES:, :] * taps[CONV_W - 1]
            for j in range(1, CONV_W):
                acc = acc + pltpu.roll(win, j, 0)[SUBLANES:, :] * taps[CONV_W - 1 - j]
            pre_ref[pl.ds(r * rc, rc), :] = acc.astype(pre_ref.dtype)
            o_ref[pl.ds(r * rc, rc), :] = (acc * _sigmoid(acc)).astype(o_ref.dtype)

    out = pl.BlockSpec((s, cw), lambda b, j: (b, j))
    return _pallas(
        body, name="conv_fwd", grid=(nb, n_conv // cw),
        in_specs=[pl.BlockSpec((s, cw), lambda b, j: (b, off_xbc // cw + j)),
                  pl.BlockSpec((CONV_W, cw), lambda b, j: (0, j)),
                  pl.BlockSpec((1, cw), lambda b, j: (0, j))],
        out_specs=[out, out],
        out_shape=[jax.ShapeDtypeStruct((nb * s, n_conv), ACT), jax.ShapeDtypeStruct((nb * s, n_conv), ACT)],
        scratch_shapes=[pltpu.VMEM((s + SUBLANES, cw), F32)],
        compiler_params=_cparams(("parallel", "parallel"), 48),
    )(proj, conv_w, conv_b)


def _ssd_common(dtraw, bias, alog, n_heads):
    l = dtraw.shape[0]
    lane = _iota((l, HEAD_PAD), 1)
    lane1 = _iota((1, HEAD_PAD), 1)
    apad = jnp.where(lane1 < n_heads, -jnp.exp(alog), 0.0)
    dt = jnp.where(lane < n_heads, _softplus(dtraw + bias), 0.0)
    a = dt * apad
    tri = (_iota((l, l), 1) <= _iota((l, l), 0)).astype(BF16)
    acs = _exact_l(tri, a)
    width = n_heads * HEAD_DIM
    ex = (_iota((HEAD_PAD, width), 1) // HEAD_DIM == _iota((HEAD_PAD, width), 0)).astype(BF16)
    dt_full = _exact_r(dt, ex)
    acs_full = _exact_r(acs, ex)
    return dt, apad, acs, dt_full, acs_full, tri, ex


def _decay_mats(acs, acs_t, h, mask):
    col = acs[:, h:h + 1]
    row = acs_t[h:h + 1, :]
    return jnp.exp(jnp.where(mask, col - row, NEG))


def _ssd_fwd(xbc, proj, dtraw, dt_bias, a_log, dskip_full, gnorm, nb, s, d_inner):
    l = CHUNK
    nc = s // l
    n_heads = d_inner // HEAD_DIM
    gw = d_inner // GROUPS
    pairs_per_group = gw // LANES
    gn_cols = GROUPS * D_STATE

    def body(xs_ref, bm_ref, cm_ref, z_ref, dtr_ref, bias_ref, alog_ref, dsk_ref, gn_ref,
             y_ref, ya3_ref, prev_ref, state_ref):
        @pl.when(pl.program_id(1) == 0)
        def _():
            state_ref[...] = jnp.zeros_like(state_ref)

        xs = xs_ref[...].astype(F32)
        dt, apad, acs, dt_full, acs_full, tri, ex = _ssd_common(
            dtr_ref[...], bias_ref[...], alog_ref[...], n_heads)
        acs_t = acs.T
        e_full = jnp.exp(acs_full)
        last = acs_full[l - 1:l, :]
        dte = jnp.exp(last - acs_full)
        cd = jnp.exp(last)
        xdt = xs * dt_full
        xd = xdt * dte
        mask = _iota((l, l), 1) <= _iota((l, l), 0)
        lane = _iota((l, LANES), 1)
        prev_ref[0] = state_ref[...]
        for g in range(GROUPS):
            gsl = slice(g * gw, (g + 1) * gw)
            bg = bm_ref[:, g * D_STATE:(g + 1) * D_STATE].astype(BF16)
            cg = cm_ref[:, g * D_STATE:(g + 1) * D_STATE].astype(BF16)
            cb = _dot(cg, bg, NT)
            outs = []
            for q in range(pairs_per_group):
                pair = g * pairs_per_group + q
                h0, h1 = 2 * pair, 2 * pair + 1
                xpb = xdt[:, pair * LANES:(pair + 1) * LANES].astype(BF16)
                m0 = (cb * _decay_mats(acs, acs_t, h0, mask)).astype(BF16)
                m1 = (cb * _decay_mats(acs, acs_t, h1, mask)).astype(BF16)
                zero = jnp.zeros_like(xpb)
                rhs = jnp.concatenate([jnp.where(lane < HEAD_DIM, xpb, zero),
                                       jnp.where(lane >= HEAD_DIM, xpb, zero)], axis=0)
                outs.append(_dot(jnp.concatenate([m0, m1], axis=1), rhs))
            prev_g = state_ref[:, gsl]
            y_off = _dot(cg, prev_g.astype(BF16)) * e_full[:, gsl]
            y_g = jnp.concatenate(outs, axis=1) + y_off
            y_ref[:, gsl] = y_g.astype(y_ref.dtype)
            st = _dot(bg, xd[:, gsl].astype(BF16), TN)
            state_ref[:, gsl] = prev_g * cd[:, gsl] + st
            zv = z_ref[:, gsl].astype(F32)
            blk = (y_g + xs[:, gsl] * dsk_ref[:, gsl]) * (zv * _sigmoid(zv))
            rg = lax.rsqrt(jnp.mean(blk * blk, axis=-1, keepdims=True) + EPS)
            ya3_ref[:, gsl] = (blk * rg * gn_ref[:, gsl]).astype(ya3_ref.dtype)

    row = lambda b, c: b * nc + c
    small = lambda n: pl.BlockSpec((1, n), lambda b, c: (0, 0))
    return _pallas(
        body, name="ssd_fwd", grid=(nb, nc),
        in_specs=[pl.BlockSpec((l, d_inner), lambda b, c: (row(b, c), 0)),
                  pl.BlockSpec((l, gn_cols), lambda b, c: (row(b, c), d_inner // gn_cols)),
                  pl.BlockSpec((l, gn_cols), lambda b, c: (row(b, c), d_inner // gn_cols + 1)),
                  pl.BlockSpec((l, d_inner), lambda b, c: (row(b, c), 0)),
                  pl.BlockSpec((l, HEAD_PAD), lambda b, c: (row(b, c), 0)),
                  small(HEAD_PAD), small(HEAD_PAD), small(d_inner), small(d_inner)],
        out_specs=[pl.BlockSpec((l, d_inner), lambda b, c: (row(b, c), 0)),
                   pl.BlockSpec((l, d_inner), lambda b, c: (row(b, c), 0)),
                   pl.BlockSpec((1, D_STATE, d_inner), lambda b, c: (row(b, c), 0, 0))],
        out_shape=[jax.ShapeDtypeStruct((nb * s, d_inner), ACT),
                   jax.ShapeDtypeStruct((nb * s, d_inner), BF16),
                   jax.ShapeDtypeStruct((nb * nc, D_STATE, d_inner), F32)],
        scratch_shapes=[pltpu.VMEM((D_STATE, d_inner), F32)],
        compiler_params=_cparams(("parallel", "arbitrary"), 48),
    )(xbc, xbc, xbc, proj, dtraw, dt_bias, a_log, dskip_full, gnorm)


def _pool_fwd(proj, mixw, mix_b, scale, nb, s, off_u, off_zp, d_pool):
    pg = d_pool // len(POOL_WINDOWS)

    def body(u_ref, zp_ref, w_ref, b_ref, s_ref, pooled_ref, mixed_ref, yb_ref):
        g = pl.program_id(1)
        uv = u_ref[...].astype(F32)
        row = _iota(uv.shape, 0)
        pos = (_iota((uv.shape[0], 1), 0) + 1).astype(F32)
        for gi, w in enumerate(POOL_WINDOWS):
            @pl.when(g == gi)
            def _():
                acc, span = uv, 1
                while span < w:
                    acc = acc + _shift_down(acc, span, row)
                    span *= 2
                pooled_ref[...] = (acc / jnp.minimum(pos, float(w)) - uv).astype(pooled_ref.dtype)

        mixed = _dot(pooled_ref[...], w_ref[0])
        mixed_ref[...] = mixed.astype(mixed_ref.dtype)
        zp = zp_ref[...].astype(F32)
        yb_ref[...] = ((mixed + b_ref[...]) * s_ref[...] * (zp * _sigmoid(zp))).astype(yb_ref.dtype)

    seg = lambda off: pl.BlockSpec((s, pg), lambda b, g: (b, off // pg + g))
    vec = pl.BlockSpec((1, pg), lambda b, g: (0, g))
    return _pallas(
        body, name="pool_fwd", grid=(nb, len(POOL_WINDOWS)),
        in_specs=[seg(off_u), seg(off_zp), pl.BlockSpec((1, pg, pg), lambda b, g: (g, 0, 0)), vec, vec],
        out_specs=[seg(0), seg(0), seg(0)],
        out_shape=[jax.ShapeDtypeStruct((nb * s, d_pool), BF16), jax.ShapeDtypeStruct((nb * s, d_pool), ACT),
                   jax.ShapeDtypeStruct((nb * s, d_pool), BF16)],
        compiler_params=_cparams(("parallel", "parallel"), 48),
    )(proj, proj, mixw, mix_b, scale)


def _merge_to_grads(ya3, wa, yb, wb, proj, off_ga, off_gb, wout, x, ple_g, wg, p, wup, target, final_g, n_main):
    t, d = x.shape
    pd = p.shape[1]
    tm = 256
    assert off_gb == off_ga + d and off_ga % (2 * d) == 0

    def body(a_ref, wa_ref, b_ref, wb_ref, ga_ref, gb_ref, wout_ref, x_ref, pg_ref, wg_ref, p_ref, wup_ref,
             tg_ref, fg_ref,
             mg_ref, h2_ref, dx1_ref, dx1b_ref, dgl_ref, dpu_ref, dpa_ref, dpb_ref, dgate_ref,
             loss_ref, dfg_ref, dpg_ref):
        @pl.when(pl.program_id(0) == 0)
        def _():
            loss_ref[...] = jnp.zeros_like(loss_ref)
            dfg_ref[...] = jnp.zeros_like(dfg_ref)
            dpg_ref[...] = jnp.zeros_like(dpg_ref)

        pa = _dot(a_ref[...], wa_ref[...])
        pb = _dot(b_ref[...], wb_ref[...])
        sa = _sigmoid(ga_ref[...].astype(F32))
        sb = _sigmoid(gb_ref[...].astype(F32))
        merged = (sa * pa + sb * pb).astype(BF16)
        mg_ref[...] = merged
        x1 = x_ref[...] + _dot(merged, wout_ref[...])
        r1 = lax.rsqrt(jnp.mean(x1 * x1, axis=-1, keepdims=True) + EPS)
        xh1 = x1 * r1
        pg = pg_ref[...]
        h2 = (xh1 * pg).astype(BF16)
        h2_ref[...] = h2
        gate = _sigmoid(_dot(h2, wg_ref[...]))
        pu = _dot(p_ref[...].astype(BF16), wup_ref[...])
        x2 = x1 + gate * pu
        r2 = lax.rsqrt(jnp.mean(x2 * x2, axis=-1, keepdims=True) + EPS)
        xh2 = x2 * r2
        gfin = fg_ref[...]
        err = xh2 * gfin - tg_ref[...]
        loss_ref[...] += _sum_all(err * err) * (0.5 / d)
        dy = err * (1.0 / d)
        dfg_ref[...] += jnp.sum(dy * xh2, axis=0, keepdims=True)
        dg = dy * gfin
        dx2 = r2 * (dg - xh2 * jnp.mean(dg * xh2, axis=-1, keepdims=True))
        dgl = (dx2 * pu * gate * (1.0 - gate)).astype(BF16)
        dgl_ref[...] = dgl
        dpu_ref[...] = (dx2 * gate).astype(dpu_ref.dtype)
        dh2 = _dot(dgl, wg_ref[...], NT)
        dpg_ref[...] += jnp.sum(dh2 * xh1, axis=0, keepdims=True)
        dg = dh2 * pg
        dx1 = dx2 + r1 * (dg - xh1 * jnp.mean(dg * xh1, axis=-1, keepdims=True))
        dx1_ref[...] = dx1
        dx1b = dx1.astype(BF16)
        dx1b_ref[...] = dx1b
        dm = _dot(dx1b, wout_ref[...], NT)
        dpa_ref[...] = (dm * sa).astype(dpa_ref.dtype)
        dpb_ref[...] = (dm * sb).astype(dpb_ref.dtype)
        dgate_ref[:, :d] = (dm * pa * sa * (1.0 - sa)).astype(dgate_ref.dtype)
        dgate_ref[:, d:] = (dm * pb * sb * (1.0 - sb)).astype(dgate_ref.dtype)

    rows = lambda n: pl.BlockSpec((tm, n), lambda i: (i, 0))
    resident = lambda a: pl.BlockSpec(a.shape, lambda i: (0, 0), pipeline_mode=pl.Buffered(1))
    vec = lambda n: pl.BlockSpec((1, n), lambda i: (0, 0))
    bf = lambda: jax.ShapeDtypeStruct((t, d), BF16)
    return _pallas(
        body, name="merge_to_grads", grid=(t // tm,),
        in_specs=[rows(ya3.shape[1]), resident(wa), rows(yb.shape[1]), resident(wb),
                  pl.BlockSpec((tm, d), lambda i: (i, off_ga // d)), pl.BlockSpec((tm, d), lambda i: (i, off_gb // d)),
                  resident(wout), rows(d), vec(d), resident(wg), rows(pd), resident(wup), rows(d), vec(d)],
        out_specs=[rows(d)] * 8 + [pl.BlockSpec((tm, 2 * d), lambda i: (i, off_ga // (2 * d))),
                                   vec(LANES), vec(d), vec(d)],
        out_shape=[bf(), bf(), jax.ShapeDtypeStruct((t, d), F32), bf(), bf(), bf(), bf(), bf(),
                   jax.ShapeDtypeStruct((t, n_main), BF16), jax.ShapeDtypeStruct((1, LANES), F32),
                   jax.ShapeDtypeStruct((1, d), F32), jax.ShapeDtypeStruct((1, d), F32)],
        compiler_params=_cparams(("arbitrary",), 60),
    )(ya3, wa, yb, wb, proj, proj, wout, x, ple_g, wg, p, wup, target, final_g)


def _yb_bwd(dproj, dpb, wb, proj, mixed, mix_b, scale, off_zp):
    t, dp = mixed.shape
    tm = 512

    def body(_, dpb_ref, w_ref, zp_ref, mx_ref, b_ref, s_ref, dzp_ref, dmx_ref, dsc_ref, db_ref):
        @pl.when(pl.program_id(0) == 0)
        def _():
            dsc_ref[...] = jnp.zeros_like(dsc_ref)
            db_ref[...] = jnp.zeros_like(db_ref)

        dyb = _dot(dpb_ref[...], w_ref[...], NT)
        zp = zp_ref[...].astype(F32)
        sg = _sigmoid(zp)
        mb = mx_ref[...].astype(F32) + b_ref[...]
        yb0 = mb * s_ref[...]
        dyb0 = dyb * (zp * sg)
        dzp_ref[...] = (dyb * yb0 * (sg * (1.0 + zp * (1.0 - sg)))).astype(dzp_ref.dtype)
        dsc_ref[...] += jnp.sum(dyb0 * mb, axis=0, keepdims=True)
        dmx = dyb0 * s_ref[...]
        db_ref[...] += jnp.sum(dmx, axis=0, keepdims=True)
        dmx_ref[...] = dmx.astype(dmx_ref.dtype)

    rowblk = pl.BlockSpec((tm, dp), lambda i: (i, 0))
    vec = pl.BlockSpec((1, dp), lambda i: (0, 0))
    seg = pl.BlockSpec((tm, dp), lambda i: (i, off_zp // dp))
    return _pallas(
        body, name="yb_bwd", grid=(t // tm,),
        in_specs=[ANY, pl.BlockSpec((tm, wb.shape[1]), lambda i: (i, 0)),
                  pl.BlockSpec(wb.shape, lambda i: (0, 0)), seg, rowblk, vec, vec],
        out_specs=[seg, rowblk, vec, vec],
        out_shape=[jax.ShapeDtypeStruct(dproj.shape, dproj.dtype), jax.ShapeDtypeStruct((t, dp), BF16),
                   jax.ShapeDtypeStruct((1, dp), F32), jax.ShapeDtypeStruct((1, dp), F32)],
        input_output_aliases={0: 0},
        compiler_params=_cparams(("arbitrary",), 56),
    )(dproj, dpb, wb, proj, mixed, mix_b, scale)


def _dmixw(pooled, dmixed, ng):
    t, dp = pooled.shape
    pg = dp // ng
    tk = min(2048, t)
    nk = t // tk

    def body(a_ref, b_ref, o_ref):
        @pl.when(pl.program_id(1) == 0)
        def _():
            o_ref[...] = jnp.zeros_like(o_ref)

        o_ref[0] += _dot(a_ref[...], b_ref[...], TN)

    blk = pl.BlockSpec((tk, pg), lambda g, k: (k, g))
    return _pallas(
        body, name="dmixw", grid=(ng, nk),
        in_specs=[blk, blk],
        out_specs=pl.BlockSpec((1, pg, pg), lambda g, k: (g, 0, 0)),
        out_shape=jax.ShapeDtypeStruct((ng, pg, pg), F32),
        compiler_params=_cparams(("parallel", "arbitrary")),
    )(pooled, dmixed)


def _pool_bwd(dproj, dmixed, mixw, nb, s, off_u):
    dp = dmixed.shape[1]
    pg = dp // len(POOL_WINDOWS)

    def body(_, d_ref, w_ref, o_ref):
        g = pl.program_id(1)
        dv = _dot(d_ref[...], w_ref[0], NT)
        row = _iota(dv.shape, 0)
        pos = (_iota((dv.shape[0], 1), 0) + 1).astype(F32)
        for gi, w in enumerate(POOL_WINDOWS):
            @pl.when(g == gi)
            def _():
                acc, span = dv / jnp.minimum(pos, float(w)), 1
                while span < w:
                    acc = acc + _shift_up(acc, span, row)
                    span *= 2
                o_ref[...] = (acc - dv).astype(o_ref.dtype)

    return _pallas(
        body, name="pool_bwd", grid=(nb, len(POOL_WINDOWS)),
        in_specs=[ANY, pl.BlockSpec((s, pg), lambda b, g: (b, g)), pl.BlockSpec((1, pg, pg), lambda b, g: (g, 0, 0))],
        out_specs=pl.BlockSpec((s, pg), lambda b, g: (b, off_u // pg + g)),
        out_shape=jax.ShapeDtypeStruct(dproj.shape, dproj.dtype),
        input_output_aliases={0: 0},
        compiler_params=_cparams(("parallel", "parallel"), 48),
    )(dproj, dmixed, mixw)


def _ssd_bwd(dproj, dya3, y, xbc, proj, dtraw, prev, dt_bias, a_log, dskip_full, gnorm, nb, s, d_inner, partials):
    l = CHUNK
    nc = s // l
    n_heads = d_inner // HEAD_DIM
    gw = d_inner // GROUPS
    pairs_per_group = gw // LANES
    gn_cols = GROUPS * D_STATE
    n_conv = d_inner + 2 * gn_cols
    last_step = nb * nc - 1

    xch = _DirectExchange(partials)
    n_in, n_out = 13, 7

    def body(*refs):
        (_, dya3_ref, y_ref, xs_ref, bm_ref, cm_ref, z_ref, dtr_ref, prev_ref, bias_ref, alog_ref,
         dsk_ref, gn_ref) = refs[:n_in]
        p_refs = refs[n_in:n_in + xch.n]
        outs = refs[n_in + xch.n:]
        dz_ref, dxbc_ref, ddt_ref, dgn_ref, dskh_ref, dalog_ref, dbias_ref = outs[:n_out]
        got_refs = outs[n_out:n_out + xch.n]
        g_ref, dskacc_ref = outs[n_out + xch.n:n_out + xch.n + 2]
        sems = outs[n_out + xch.n + 2:]
        step = pl.program_id(0) * nc + pl.program_id(1)

        @pl.when(pl.program_id(1) == 0)
        def _():
            g_ref[...] = jnp.zeros_like(g_ref)

        @pl.when(step == 0)
        def _():
            xch.start(p_refs, got_refs, sems)
            dgn_ref[...] = jnp.zeros_like(dgn_ref)
            dalog_ref[...] = jnp.zeros_like(dalog_ref)
            dbias_ref[...] = jnp.zeros_like(dbias_ref)
            dskacc_ref[...] = jnp.zeros_like(dskacc_ref)

        xs = xs_ref[...].astype(F32)
        zv = z_ref[...].astype(F32)
        dsk = dsk_ref[...]
        sg = _sigmoid(zv)
        sz = zv * sg
        ya1 = y_ref[...].astype(F32) + xs * dsk
        ya2 = ya1 * sz
        parts = []
        for g in range(GROUPS):
            gsl = slice(g * gw, (g + 1) * gw)
            blk = ya2[:, gsl]
            rg = lax.rsqrt(jnp.mean(blk * blk, axis=-1, keepdims=True) + EPS)
            xh = blk * rg
            d3 = dya3_ref[:, gsl].astype(F32)
            dgn_ref[:, gsl] += jnp.sum(d3 * xh, axis=0, keepdims=True)
            dg = d3 * gn_ref[:, gsl]
            parts.append(rg * (dg - xh * jnp.mean(dg * xh, axis=-1, keepdims=True)))
        dya2 = jnp.concatenate(parts, axis=1)
        dy = dya2 * sz
        dz_ref[...] = (dya2 * ya1 * (sg * (1.0 + zv * (1.0 - sg)))).astype(dz_ref.dtype)
        dskacc_ref[...] += jnp.sum(dy * xs, axis=0, keepdims=True)

        dtraw_v = dtr_ref[...]
        bias = bias_ref[...]
        dt, apad, acs, dt_full, acs_full, tri, ex = _ssd_common(dtraw_v, bias, alog_ref[...], n_heads)
        acs_t = acs.T
        e_full = jnp.exp(acs_full)
        last = acs_full[l - 1:l, :]
        dte = jnp.exp(last - acs_full)
        cd = jnp.exp(last)
        xdt = xs * dt_full
        xd = xdt * dte
        dy_e = dy * e_full
        mask = _iota((l, l), 1) <= _iota((l, l), 0)
        lane = _iota((l, LANES), 1)
        hlane = _iota((l, HEAD_PAD), 1)
        hsub = _iota((HEAD_PAD, l), 0)
        dacs = jnp.zeros((l, HEAD_PAD), F32)
        dacs_t = jnp.zeros((HEAD_PAD, l), F32)
        dacs_full, dlast, dxdt_all, db_all, dc_all = [], [], [], [], []
        for g in range(GROUPS):
            gsl = slice(g * gw, (g + 1) * gw)
            bg = bm_ref[:, g * D_STATE:(g + 1) * D_STATE].astype(BF16)
            cg = cm_ref[:, g * D_STATE:(g + 1) * D_STATE].astype(BF16)
            prev_g = prev_ref[0, :, gsl]
            prev_b = prev_g.astype(BF16)
            gst = g_ref[:, gsl]
            gst_b = gst.astype(BF16)
            dye_b = dy_e[:, gsl].astype(BF16)
            dc = _dot(dye_b, prev_b, NT)
            dprev = _dot(cg, dye_b, TN)
            y_off = _dot(cg, prev_b) * e_full[:, gsl]
            xd_g = xd[:, gsl]
            db = _dot(xd_g.astype(BF16), gst_b, NT)
            dxd = _dot(bg, gst_b)
            wgt = dxd * xd_g
            dacs_full.append(dy[:, gsl] * y_off - wgt)
            dlast.append(jnp.sum(wgt, axis=0, keepdims=True)
                         + jnp.sum(prev_g * gst, axis=0, keepdims=True) * cd[:, gsl])
            g_ref[:, gsl] = dprev + cd[:, gsl] * gst
            dxdt_g = dxd * dte[:, gsl]
            cb = _dot(cg, bg, NT)
            dcb = jnp.zeros((l, l), F32)
            for q in range(pairs_per_group):
                pair = g * pairs_per_group + q
                psl = slice(pair * LANES, (pair + 1) * LANES)
                xpb = xdt[:, psl].astype(BF16)
                dyp = dy[:, psl].astype(BF16)
                zero = jnp.zeros_like(dyp)
                halves = (jnp.where(lane < HEAD_DIM, dyp, zero), jnp.where(lane >= HEAD_DIM, dyp, zero))
                acc = dxdt_g[:, q * LANES:(q + 1) * LANES]
                for k, dyh in enumerate(halves):
                    h = 2 * pair + k
                    dk = _decay_mats(acs, acs_t, h, mask)
                    mm = cb * dk
                    dm = _dot(dyh, xpb, NT)
                    dcb = dcb + dm * dk
                    dseg = dm * mm
                    dacs = dacs + jnp.where(hlane == h, jnp.sum(dseg, axis=1, keepdims=True), 0.0)
                    dacs_t = dacs_t + jnp.where(hsub == h, jnp.sum(dseg, axis=0, keepdims=True), 0.0)
                    acc = acc + _dot(mm.astype(BF16), dyh, TN)
                dxdt_all.append(acc)
            dcb_b = dcb.astype(BF16)
            dc_all.append(dc + _dot(dcb_b, bg))
            db_all.append(db + _dot(dcb_b, cg, TN))

        dxdt = jnp.concatenate(dxdt_all, axis=1)
        dxbc_ref[:, :d_inner] = (dy * dsk + dxdt * dt_full).astype(dxbc_ref.dtype)
        dxbc_ref[:, d_inner:d_inner + gn_cols] = jnp.concatenate(db_all, axis=1).astype(dxbc_ref.dtype)
        dxbc_ref[:, d_inner + gn_cols:] = jnp.concatenate(dc_all, axis=1).astype(dxbc_ref.dtype)
        ddt = _exact_r(dxdt * xs, ex, NT)
        rowi = _iota((l, d_inner), 0)
        dacs_f = jnp.concatenate(dacs_full, axis=1) + jnp.where(rowi == l - 1, jnp.concatenate(dlast, axis=1), 0.0)
        dacs = dacs + _exact_r(dacs_f, ex, NT) - dacs_t.T
        da = _exact_l(tri, dacs, TN)
        ddt = ddt + da * apad
        dalog_ref[...] += jnp.sum(da * dt, axis=0, keepdims=True) * apad
        ddt_raw = ddt * _sigmoid(dtraw_v + bias)
        dbias_ref[...] += jnp.sum(ddt_raw, axis=0, keepdims=True)
        ddt_ref[...] = ddt_raw.astype(ddt_ref.dtype)

        @pl.when(step == last_step)
        def _():
            dskh_ref[...] = _exact_r(dskacc_ref[...], ex, NT)[0:1, :]
            xch.wait(p_refs, got_refs, sems)

    row = lambda b, c: b * nc + (nc - 1 - c)
    small = lambda n: pl.BlockSpec((1, n), lambda b, c: (0, 0))
    wide = pl.BlockSpec((l, d_inner), lambda b, c: (row(b, c), 0))
    return _pallas(
        body, name="ssd_bwd", grid=(nb, nc),
        in_specs=[ANY, wide, wide, wide,
                  pl.BlockSpec((l, gn_cols), lambda b, c: (row(b, c), d_inner // gn_cols)),
                  pl.BlockSpec((l, gn_cols), lambda b, c: (row(b, c), d_inner // gn_cols + 1)),
                  wide, pl.BlockSpec((l, HEAD_PAD), lambda b, c: (row(b, c), 0)),
                  pl.BlockSpec((1, D_STATE, d_inner), lambda b, c: (row(b, c), 0, 0)),
                  small(HEAD_PAD), small(HEAD_PAD), small(d_inner), small(d_inner)] + xch.in_specs,
        out_specs=[wide, pl.BlockSpec((l, n_conv), lambda b, c: (row(b, c), 0)),
                   pl.BlockSpec((l, HEAD_PAD), lambda b, c: (row(b, c), 0)),
                   small(d_inner), small(HEAD_PAD), small(HEAD_PAD), small(HEAD_PAD)] + xch.out_specs,
        out_shape=[jax.ShapeDtypeStruct(dproj.shape, dproj.dtype),
                   jax.ShapeDtypeStruct((nb * s, n_conv), ACT),
                   jax.ShapeDtypeStruct((nb * s, HEAD_PAD), BF16),
                   jax.ShapeDtypeStruct((1, d_inner), F32), jax.ShapeDtypeStruct((1, HEAD_PAD), F32),
                   jax.ShapeDtypeStruct((1, HEAD_PAD), F32), jax.ShapeDtypeStruct((1, HEAD_PAD), F32)]
        + xch.out_shape,
        scratch_shapes=[pltpu.VMEM((D_STATE, d_inner), F32), pltpu.VMEM((8, d_inner), F32)] + xch.scratch,
        input_output_aliases={0: 0},
        compiler_params=_cparams(("arbitrary", "arbitrary"), 56),
    )(dproj, dya3, y, xbc, xbc, xbc, proj, dtraw, prev, dt_bias, a_log, dskip_full, gnorm, *partials)


def _conv_bwd(dproj, dxbc, conv_pre, proj, conv_w, nb, s, off_xbc):
    n_conv = dxbc.shape[1]
    cw = 256

    rc = CONV_ROWS

    def fold(v):
        return jnp.sum(v.reshape(rc // SUBLANES, SUBLANES, cw), axis=0)

    def body(_, d_ref, pre_ref, x_ref, w_ref, dx_ref, dw_ref, db_ref, dcpad_ref):
        @pl.when(pl.program_id(1) == 0)
        def _():
            dw_ref[...] = jnp.zeros_like(dw_ref)
            db_ref[...] = jnp.zeros_like(db_ref)

        dcpad_ref[pl.ds(s, SUBLANES), :] = jnp.zeros((SUBLANES, cw), F32)
        taps = [w_ref[k:k + 1, :] for k in range(CONV_W)]
        db = jnp.zeros((SUBLANES, cw), F32)
        for r in range(s // rc):
            rows = pl.ds(r * rc, rc)
            acc = pre_ref[rows, :].astype(F32)
            sg = _sigmoid(acc)
            dc = d_ref[rows, :].astype(F32) * (sg * (1.0 + acc * (1.0 - sg)))
            dcpad_ref[rows, :] = dc
            db = db + fold(dc)
        db_ref[...] += jnp.sum(db, axis=0, keepdims=True)
        dw = [jnp.zeros((SUBLANES, cw), F32) for _ in range(CONV_W)]
        for r in range(s // rc):
            xv = x_ref[pl.ds(r * rc, rc), :].astype(F32)
            win = dcpad_ref[pl.ds(r * rc, rc + SUBLANES), :]
            dxv = jnp.zeros((rc, cw), F32)
            for j in range(CONV_W):
                up = win[:rc, :] if j == 0 else pltpu.roll(win, rc + SUBLANES - j, 0)[:rc, :]
                dxv = dxv + up * taps[CONV_W - 1 - j]
                dw[CONV_W - 1 - j] = dw[CONV_W - 1 - j] + fold(up * xv)
            dx_ref[pl.ds(r * rc, rc), :] = dxv.astype(dx_ref.dtype)
        for k in range(CONV_W):
            dw_ref[k:k + 1, :] += jnp.sum(dw[k], axis=0, keepdims=True)

    blk = pl.BlockSpec((s, cw), lambda j, b: (b, j))
    return _pallas(
        body, name="conv_bwd", grid=(n_conv // cw, nb),
        in_specs=[ANY, blk, blk, pl.BlockSpec((s, cw), lambda j, b: (b, off_xbc // cw + j)),
                  pl.BlockSpec((CONV_W, cw), lambda j, b: (0, j))],
        out_specs=[pl.BlockSpec((s, cw), lambda j, b: (b, off_xbc // cw + j)),
                   pl.BlockSpec((CONV_W, cw), lambda j, b: (0, j)), pl.BlockSpec((1, cw), lambda j, b: (0, j))],
        out_shape=[jax.ShapeDtypeStruct(dproj.shape, dproj.dtype),
                   jax.ShapeDtypeStruct((CONV_W, n_conv), F32), jax.ShapeDtypeStruct((1, n_conv), F32)],
        scratch_shapes=[pltpu.VMEM((s + SUBLANES, cw), F32)],
        input_output_aliases={0: 0},
        compiler_params=_cparams(("parallel", "arbitrary"), 48),
    )(dproj, dxbc, conv_pre, proj, conv_w)


def _in_bwd(dproj, w_main, ddt, w_dt, x, norm_g, dx1, partials):
    t, d = x.shape
    n_main = dproj.shape[1]
    tm = 256
    ni = t // tm
    xch = _ChipExchange(partials)
    n_in, n_out = 7, 2

    def body(*refs):
        dp_ref, w_ref, ddt_ref, wdt_ref, x_ref, g_ref, dx1_ref = refs[:n_in]
        p_refs = refs[n_in:n_in + xch.n]
        outs = refs[n_in + xch.n:]
        dx_ref, dg_ref = outs[:n_out]
        got_refs = outs[n_out:n_out + xch.n]
        sems = outs[n_out + xch.n:]
        i = pl.program_id(0)

        @pl.when(i == 0)
        def _():
            xch.start(p_refs, got_refs, sems)
            dg_ref[...] = jnp.zeros_like(dg_ref)

        dh = _dot(dp_ref[...], w_ref[...], NT) + _dot(ddt_ref[...], wdt_ref[...], NT)
        xv = x_ref[...]
        r = lax.rsqrt(jnp.mean(xv * xv, axis=-1, keepdims=True) + EPS)
        xh = xv * r
        dg_ref[...] += jnp.sum(dh * xh, axis=0, keepdims=True)
        dg = dh * g_ref[...]
        dx_ref[...] = dx1_ref[...] + r * (dg - xh * jnp.mean(dg * xh, axis=-1, keepdims=True))

        @pl.when(i == ni - 1)
        def _():
            xch.wait(p_refs, got_refs, sems)

    rowblk = pl.BlockSpec((tm, d), lambda i: (i, 0))
    vec = pl.BlockSpec((1, d), lambda i: (0, 0))
    resident = lambda shape: pl.BlockSpec(shape, lambda i: (0, 0), pipeline_mode=pl.Buffered(1))
    return _pallas(
        body, name="in_bwd", grid=(ni,),
        in_specs=[pl.BlockSpec((tm, n_main), lambda i: (i, 0)), resident((d, n_main)),
                  pl.BlockSpec((tm, HEAD_PAD), lambda i: (i, 0)), resident((d, HEAD_PAD)),
                  rowblk, vec, rowblk] + xch.in_specs,
        out_specs=[rowblk, vec] + xch.out_specs,
        out_shape=[jax.ShapeDtypeStruct((t, d), F32), jax.ShapeDtypeStruct((1, d), F32)] + xch.out_shape,
        scratch_shapes=xch.scratch,
        compiler_params=_cparams(("arbitrary",), 56),
    )(dproj, w_main, ddt, w_dt, x, norm_g, dx1, *partials)


def _adamw_math(w, g, m, v):
    m = ADAM_B1 * m + (1.0 - ADAM_B1) * g
    v = ADAM_B2 * v + (1.0 - ADAM_B2) * (g * g)
    m_hat = m / (1.0 - ADAM_B1 ** ADAM_STEP)
    v_hat = v / (1.0 - ADAM_B2 ** ADAM_STEP)
    delta = -ADAM_LR * (m_hat / (jnp.sqrt(v_hat) + ADAM_EPS) + ADAM_WD * w)
    return delta, m, v


def _adamw(parts, w, m, v, name):
    n, rows, cols = parts.shape
    tr = _row_tile(rows, 128) if rows % 8 == 0 else rows

    def body(p_ref, w_ref, m_ref, v_ref, g_ref, d_ref, mo_ref, vo_ref):
        g = p_ref[0].astype(F32)
        for k in range(1, n):
            g = g + p_ref[k].astype(F32)
        d, mn, vn = _adamw_math(w_ref[...], g, m_ref[...], v_ref[...])
        g_ref[...] = g
        d_ref[...] = d
        mo_ref[...] = mn
        vo_ref[...] = vn

    blk = pl.BlockSpec((tr, cols), lambda i: (i, 0))
    shp = jax.ShapeDtypeStruct((rows, cols), F32)
    return _pallas(
        body, name=name, grid=(rows // tr,),
        in_specs=[pl.BlockSpec((n, tr, cols), lambda i: (0, i, 0)), blk, blk, blk],
        out_specs=[blk, blk, blk, blk], out_shape=[shp, shp, shp, shp],
        compiler_params=_cparams(("parallel",), 56),
    )(parts, w, m, v)


def _coords():
    return lax.axis_index("x"), lax.axis_index("y"), lax.axis_index("c")


class _TwoLevelGather:
    def __init__(self, blocks):
        self.n = len(blocks)
        self.in_specs = [ANY] * self.n
        self.out_specs = [ANY] * self.n
        self.out_shape = [jax.ShapeDtypeStruct((N_DEV,) + b.shape, b.dtype) for b in blocks]
        self.scratch = [pltpu.SemaphoreType.DMA((self.n, 7)), pltpu.SemaphoreType.DMA((self.n, 7)),
                        pltpu.SemaphoreType.DMA((self.n,))]

    def _parts(self, x_refs, out_refs, sems):
        send_sems, recv_sems, local_sems = sems
        x, y, c = _coords()
        me, sibling = (x, y, c), (x, y, 1 - c)
        chips = [(1 - x, y), (x, 1 - y), (1 - x, 1 - y)]

        def slot(a, px, py, pc):
            return out_refs[a].at[4 * px + 2 * py + pc]

        def copy(a, k, block, to, src=None):
            return pltpu.make_async_remote_copy(
                src_ref=slot(a, *block) if src is None else src, dst_ref=slot(a, *block),
                send_sem=send_sems.at[a, k], recv_sem=recv_sems.at[a, k], device_id=to, device_id_type=MESH)

        local = [pltpu.make_async_copy(x_refs[a], slot(a, *me), local_sems.at[a]) for a in range(self.n)]
        first = []
        for a in range(self.n):
            first.append(copy(a, 0, me, sibling, src=x_refs[a]))
            first += [copy(a, 1 + j, me, (*chip, c), src=x_refs[a]) for j, chip in enumerate(chips)]
        return copy, local, first, chips, me, sibling, c

    def start(self, x_refs, out_refs, sems):
        _, local, first, *_ = self._parts(x_refs, out_refs, sems)
        for cp in local + first:
            cp.start()

    def finish(self, x_refs, out_refs, sems):
        copy, local, first, chips, me, sibling, c = self._parts(x_refs, out_refs, sems)
        passed = []
        for j, chip in enumerate(chips):
            for a in range(self.n):
                copy(a, 1 + j, (*chip, c), me).wait_recv()
                passed.append(copy(a, 4 + j, (*chip, c), sibling))
                passed[-1].start()
        for a in range(self.n):
            copy(a, 0, sibling, me).wait_recv()
            for j, chip in enumerate(chips):
                copy(a, 4 + j, (*chip, 1 - c), me).wait_recv()
        for cp in first + passed:
            cp.wait_send()
        for cp in local:
            cp.wait()


def _rms_in_and_gather(xin, g, blk):
    rows, cols = blk.shape
    half = rows // 2
    assert half % 16 == 0
    t, d = xin.shape
    tt = 512
    ni = t // tt

    def exchange(x_ref, out_ref, send_sems, recv_sems, local_sem, begin):
        x, y, c = _coords()
        me, sib, xn, yn, dg = (x, y, c), (x, y, 1 - c), (1 - x, y, c), (x, 1 - y, c), (1 - x, 1 - y, c)
        other = lambda p: (p[0], p[1], 1 - c)
        top, bot = pl.ds(0, half), pl.ds(half, half)

        def slot(p, part=None):
            ref = out_ref.at[4 * p[0] + 2 * p[1] + p[2]]
            return ref if part is None else ref.at[part]

        def copy(k, block, to, part=None, src=None):
            return pltpu.make_async_remote_copy(
                src_ref=slot(block, part) if src is None else src, dst_ref=slot(block, part),
                send_sem=send_sems.at[k], recv_sem=recv_sems.at[k], device_id=to, device_id_type=MESH)

        mine = pltpu.make_async_copy(x_ref, slot(me), local_sem)
        own = [copy(1, me, xn, src=x_ref), copy(2, me, yn, src=x_ref), copy(0, me, sib, src=x_ref)]
        if begin:
            for cp in [mine] + own:
                cp.start()
            return
        copy(1, xn, me).wait_recv()
        fwd = [copy(3, xn, yn, top), copy(5, xn, sib)]
        for cp in fwd:
            cp.start()
        copy(2, yn, me).wait_recv()
        fwd += [copy(4, yn, xn, bot), copy(6, yn, sib)]
        for cp in fwd[2:]:
            cp.start()
        copy(3, dg, me, top).wait_recv()
        fwd.append(copy(7, dg, sib, top))
        fwd[-1].start()
        copy(4, dg, me, bot).wait_recv()
        fwd.append(copy(8, dg, sib, bot))
        fwd[-1].start()
        copy(0, sib, me).wait_recv()
        copy(5, other(xn), me).wait_recv()
        copy(6, other(yn), me).wait_recv()
        copy(7, other(dg), me, top).wait_recv()
        copy(8, other(dg), me, bot).wait_recv()
        for cp in own + fwd:
            cp.wait_send()
        mine.wait()

    def body(w_ref, x_ref, g_ref, out_ref, h_ref, send_sems, recv_sems, local_sem):
        i = pl.program_id(0)

        @pl.when(i == 0)
        def _():
            exchange(w_ref, out_ref, send_sems, recv_sems, local_sem, True)

        xv = x_ref[...]
        r = lax.rsqrt(jnp.mean(xv * xv, axis=-1, keepdims=True) + EPS)
        h_ref[...] = (xv * r * g_ref[...]).astype(h_ref.dtype)

        @pl.when(i == ni - 1)
        def _():
            exchange(w_ref, out_ref, send_sems, recv_sems, local_sem, False)

    return _pallas(
        body, name="rms_in_w_in_all_gather", grid=(ni,),
        in_specs=[ANY, pl.BlockSpec((tt, d), lambda i: (i, 0)), pl.BlockSpec((1, d), lambda i: (0, 0))],
        out_specs=[ANY, pl.BlockSpec((tt, d), lambda i: (i, 0))],
        out_shape=[jax.ShapeDtypeStruct((N_DEV, rows, cols), blk.dtype), jax.ShapeDtypeStruct((t, d), BF16)],
        scratch_shapes=[pltpu.SemaphoreType.DMA((9,)), pltpu.SemaphoreType.DMA((9,)), pltpu.SemaphoreType.DMA],
        compiler_params=_cparams(("arbitrary",)),
    )(blk, xin, g)


def _all_gather_direct(blk, name):
    rows, cols = blk.shape

    def body(x_ref, out_ref, send_sems, recv_sems, local_sem):
        x, y, c = _coords()
        me = 4 * x + 2 * y + c
        mine = pltpu.make_async_copy(x_ref, out_ref.at[me], local_sem)
        mine.start()
        copies = []
        for k in range(1, N_DEV):
            fx, fy, fc = (k >> 2) & 1, (k >> 1) & 1, k & 1
            peer = (x ^ fx, y ^ fy, c ^ fc)
            copies.append(pltpu.make_async_remote_copy(
                src_ref=x_ref, dst_ref=out_ref.at[me], send_sem=send_sems.at[k - 1], recv_sem=recv_sems.at[k - 1],
                device_id=peer, device_id_type=MESH))
        for cp in copies:
            cp.start()
        for k in range(1, N_DEV):
            fx, fy, fc = (k >> 2) & 1, (k >> 1) & 1, k & 1
            src = 4 * (x ^ fx) + 2 * (y ^ fy) + (c ^ fc)
            pltpu.make_async_remote_copy(
                src_ref=x_ref, dst_ref=out_ref.at[src], send_sem=send_sems.at[k - 1], recv_sem=recv_sems.at[k - 1],
                device_id=(x, y, c), device_id_type=MESH).wait_recv()
        for cp in copies:
            cp.wait_send()
        mine.wait()

    return _pallas(
        body, name=name, in_specs=[ANY], out_specs=ANY,
        out_shape=jax.ShapeDtypeStruct((N_DEV, rows, cols), blk.dtype),
        scratch_shapes=[pltpu.SemaphoreType.DMA((7,)), pltpu.SemaphoreType.DMA((7,)), pltpu.SemaphoreType.DMA],
    )(blk)


def _pair_exchange(grads, name):
    n = len(grads)

    def body(*refs):
        g_refs, out_refs = refs[:n], refs[n:2 * n]
        send_sems, recv_sems = refs[2 * n:]
        x, y, c = _coords()
        copies = []
        for a in range(n):
            for k in range(4):
                copies.append(pltpu.make_async_remote_copy(
                    src_ref=g_refs[a].at[2 * k + (1 - c)], dst_ref=out_refs[a].at[k],
                    send_sem=send_sems.at[a, k], recv_sem=recv_sems.at[a, k],
                    device_id=(x, y, 1 - c), device_id_type=MESH))
        for cp in copies:
            cp.start()
        for cp in copies:
            cp.wait()

    return _pallas(
        body, name=name, in_specs=[ANY] * n, out_specs=[ANY] * n,
        out_shape=[jax.ShapeDtypeStruct((4,) + g.shape[1:], g.dtype) for g in grads],
        scratch_shapes=[pltpu.SemaphoreType.DMA((n, 4)), pltpu.SemaphoreType.DMA((n, 4))],
    )(*grads)


def _pair_add(grads, got, core, out_dtype, name):
    _, rows, cols = grads.shape
    tr = _row_tile(rows, 256) if rows % 8 == 0 else rows

    def body(c_ref, a_ref, b_ref, o_ref):
        o_ref[...] = (a_ref[...].astype(F32) + b_ref[...].astype(F32)).astype(o_ref.dtype)

    grid_spec = pltpu.PrefetchScalarGridSpec(
        num_scalar_prefetch=1, grid=(4, rows // tr),
        in_specs=[pl.BlockSpec((1, tr, cols), lambda k, i, c_ref: (2 * k + c_ref[0], i, 0)),
                  pl.BlockSpec((1, tr, cols), lambda k, i, c_ref: (k, i, 0))],
        out_specs=pl.BlockSpec((1, tr, cols), lambda k, i, c_ref: (k, i, 0)))
    return _pallas(
        body, name=name, grid_spec=grid_spec,
        out_shape=jax.ShapeDtypeStruct((4, rows, cols), out_dtype),
        compiler_params=_cparams(("parallel", "parallel"), 48),
    )(core, grads, got)


class _ChipExchange:
    def __init__(self, partials):
        self.n = len(partials)
        self.in_specs = [ANY] * self.n
        self.out_specs = [ANY] * self.n
        self.out_shape = [jax.ShapeDtypeStruct(p.shape, p.dtype) for p in partials]
        self.scratch = [pltpu.SemaphoreType.DMA((self.n, 3)), pltpu.SemaphoreType.DMA((self.n, 3)),
                        pltpu.SemaphoreType.DMA((self.n,))]

    def _copies(self, p_refs, out_refs, sems, receiving):
        send_sems, recv_sems, local_sems = sems
        x, y, c = _coords()
        mychip = 2 * x + y
        local = [pltpu.make_async_copy(p_refs[a].at[mychip], out_refs[a].at[mychip], local_sems.at[a])
                 for a in range(self.n)]
        remote = []
        for a in range(self.n):
            for k in range(1, 4):
                px, py = x ^ ((k >> 1) & 1), y ^ (k & 1)
                peer = 2 * px + py
                remote.append(pltpu.make_async_remote_copy(
                    src_ref=p_refs[a].at[peer], dst_ref=out_refs[a].at[peer if receiving else mychip],
                    send_sem=send_sems.at[a, k - 1], recv_sem=recv_sems.at[a, k - 1],
                    device_id=(x, y, c) if receiving else (px, py, c), device_id_type=MESH))
        return local, remote

    def start(self, p_refs, out_refs, sems):
        local, sends = self._copies(p_refs, out_refs, sems, False)
        for cp in local + sends:
            cp.start()

    def wait(self, p_refs, out_refs, sems):
        local, recvs = self._copies(p_refs, out_refs, sems, True)
        for cp in recvs:
            cp.wait_recv()
        for cp in recvs:
            cp.wait_send()
        for cp in local:
            cp.wait()


class _DirectExchange:
    def __init__(self, partials):
        self.n = len(partials)
        self.in_specs = [ANY] * self.n
        self.out_specs = [ANY] * self.n
        self.out_shape = [jax.ShapeDtypeStruct(p.shape, p.dtype) for p in partials]
        self.scratch = [pltpu.SemaphoreType.DMA((self.n, N_DEV - 1)), pltpu.SemaphoreType.DMA((self.n, N_DEV - 1)),
                        pltpu.SemaphoreType.DMA((self.n,))]

    def _copies(self, p_refs, out_refs, sems, receiving):
        send_sems, recv_sems, local_sems = sems
        x, y, c = _coords()
        me = 4 * x + 2 * y + c
        local = [pltpu.make_async_copy(p_refs[a].at[me], out_refs[a].at[me], local_sems.at[a])
                 for a in range(self.n)]
        remote = []
        for a in range(self.n):
            for k in range(1, N_DEV):
                px, py, pc = x ^ ((k >> 2) & 1), y ^ ((k >> 1) & 1), c ^ (k & 1)
                peer = 4 * px + 2 * py + pc
                remote.append(pltpu.make_async_remote_copy(
                    src_ref=p_refs[a].at[peer], dst_ref=out_refs[a].at[peer if receiving else me],
                    send_sem=send_sems.at[a, k - 1], recv_sem=recv_sems.at[a, k - 1],
                    device_id=(x, y, c) if receiving else (px, py, pc), device_id_type=MESH))
        return local, remote

    def start(self, p_refs, out_refs, sems):
        local, sends = self._copies(p_refs, out_refs, sems, False)
        for cp in local + sends:
            cp.start()

    def wait(self, p_refs, out_refs, sems):
        local, recvs = self._copies(p_refs, out_refs, sems, True)
        for cp in recvs:
            cp.wait_recv()
        for cp in recvs:
            cp.wait_send()
        for cp in local:
            cp.wait()


def _pad_lanes(v, n=HEAD_PAD):
    return jnp.pad(v, ((0, 0), (0, n - v.shape[1])))


def kernel(x, p, norm_g, w_in, conv_w, conv_b, dt_bias, a_log, d_skip, gnorm_g, pool_mix_w, pool_mix_b, pool_scale, w_branch_a, w_branch_b, w_out, ple_norm_g, w_ple_gate, w_ple_up, final_g, loss_target, m_norm_g, m_w_in, m_conv_w, m_conv_b, m_dt_bias, m_a_log, m_d_skip, m_gnorm_g, m_pool_mix_w, m_pool_mix_b, m_pool_scale, m_w_branch_a, m_w_branch_b, m_w_out, m_ple_norm_g, m_w_ple_gate, m_w_ple_up, m_final_g, v_norm_g, v_w_in, v_conv_w, v_conv_b, v_dt_bias, v_a_log, v_d_skip, v_gnorm_g, v_pool_mix_w, v_pool_mix_b, v_pool_scale, v_w_branch_a, v_w_branch_b, v_w_out, v_ple_norm_g, v_w_ple_gate, v_w_ple_up, v_final_g):
    nb, s, d = x.shape
    t = nb * s
    n_heads = dt_bias.shape[1]
    d_inner = n_heads * HEAD_DIM
    n_conv = conv_b.shape[1]
    d_pool = pool_mix_b.shape[1]
    n_in = w_in.shape[2] * N_DEV
    ple_dim = p.shape[-1]
    o_xbc, o_dt = d_inner, d_inner + n_conv
    o_u = o_dt + n_heads
    o_zp = o_u + d_pool
    o_ga, o_gb = o_zp + d_pool, o_zp + d_pool + d
    n_main = n_in - n_heads
    off_ga, off_gb, off_xbc = d_inner, d_inner + d, d_inner + 2 * d
    off_u = off_xbc + n_conv
    off_zp = off_u + d_pool

    sharded = [w_in[0], conv_w[0], pool_mix_w[0], w_branch_a[0], w_branch_b[0], w_out[0], w_ple_gate[0], w_ple_up[0]]
    sharded_m = [m_w_in[0], m_conv_w[0], m_pool_mix_w[0], m_w_branch_a[0], m_w_branch_b[0], m_w_out[0],
                 m_w_ple_gate[0], m_w_ple_up[0]]
    sharded_v = [v_w_in[0], v_conv_w[0], v_pool_mix_w[0], v_w_branch_a[0], v_w_branch_b[0], v_w_out[0],
                 v_w_ple_gate[0], v_w_ple_up[0]]
    ng = pool_mix_w.shape[1]
    pg = d_pool // ng
    two_d = lambda w: w.reshape(-1, w.shape[-1])
    sharded, sharded_m, sharded_v = ([two_d(w) for w in ws] for ws in (sharded, sharded_m, sharded_v))

    send = [w if i == 1 else w.astype(BF16) for i, w in enumerate(sharded)]
    x2 = x.reshape(t, d)
    w_in_all, h = _rms_in_and_gather(x2, norm_g, send[0])
    seg = [w_in_all]
    chunk = n_in // N_DEV
    segments = [((0, o_xbc), 0), ((o_ga, o_gb), off_ga), ((o_gb, n_in), off_gb), ((o_xbc, o_dt), off_xbc),
                ((o_u, o_zp), off_u), ((o_zp, o_ga), off_zp)]

    def gathered_cols(lo, hi):
        out = []
        while lo < hi:
            dev = lo // chunk
            a, b = lo - dev * chunk, min(hi - dev * chunk, chunk)
            out.append(seg[0][dev, :, a:b])
            lo = dev * chunk + b
        return out

    w_main = jnp.concatenate([pc for (lo, hi), _ in segments for pc in gathered_cols(lo, hi)], axis=1)
    w_dt = _pad_lanes(jnp.concatenate(gathered_cols(o_dt, o_u), axis=1))

    proj, *rest = _in_proj(h, w_main, send[1:])
    seg += rest
    cw_full = seg[1].transpose(1, 0, 2).reshape(CONV_W, n_conv)
    mixw_full = seg[2].reshape(N_DEV, ng, pg // N_DEV, pg).transpose(1, 0, 2, 3).reshape(ng, pg, pg)
    wa_full = seg[3].reshape(d_inner, d)
    wb_full = seg[4].reshape(d_pool, d)
    wout_full = seg[5].reshape(d, d)
    wg_full = seg[6].reshape(d, d)
    wup_full = seg[7].transpose(1, 0, 2).reshape(ple_dim, d)

    p2 = p.reshape(t, ple_dim)
    tgt = loss_target.reshape(t, d)
    bias_p, alog_p = _pad_lanes(dt_bias), _pad_lanes(a_log)
    dskip_full = jnp.repeat(d_skip, HEAD_DIM, axis=1)

    dtraw = _matmul(h, w_dt, "nn", F32, "in_proj_dt", 1024, HEAD_PAD, d)
    xbc, conv_pre = _conv_fwd(proj, cw_full, conv_b, nb, s, off_xbc, n_conv)
    y, ya3, prev = _ssd_fwd(xbc, proj, dtraw, bias_p, alog_p, dskip_full, gnorm_g, nb, s, d_inner)
    pooled, mixed, yb = _pool_fwd(proj, mixw_full, pool_mix_b, pool_scale, nb, s, off_u, off_zp, d_pool)
    (merged, h2, dx1, dx1b, dgl, dpu, dpa, dpb, dproj, loss_part, dfinal_g, d_ple_g) = _merge_to_grads(
        ya3, wa_full, yb, wb_full, proj, off_ga, off_gb, wout_full, x2, ple_norm_g, wg_full, p2, wup_full, tgt,
        final_g.reshape(1, d), n_main)

    g_wup = _matmul(p2, dpu, "tn", BF16, "dw_ple_up", 256, 512, t)
    g_wg = _matmul(h2, dgl, "tn", BF16, "dw_ple_gate", 512, 512, t)
    g_wout = _matmul(merged, dx1b, "tn", BF16, "dw_out", 512, 512, t)
    g_wa = _matmul(ya3, dpa, "tn", BF16, "dw_branch_a", 512, 512, t)
    g_wb = _matmul(yb, dpb, "tn", BF16, "dw_branch_b", 512, 512, t)
    dya3 = _matmul(dpa, wa_full, "nt", ACT, "d_ya3", 1024, 1024, d)
    dproj, dmixed, d_scale, d_mixb = _yb_bwd(dproj, dpb, wb_full, proj, mixed, pool_mix_b, pool_scale, off_zp)
    g_mixw = _dmixw(pooled, dmixed, ng)
    dproj = _pool_bwd(dproj, dmixed, mixw_full, nb, s, off_u)

    core = lax.axis_index("c").astype(I32).reshape(1)

    def pair_sums(by_dest, names):
        by_dest = [g.astype(BF16) for g in by_dest]
        got = _pair_exchange(by_dest, "rs_pair_exchange_" + names[0])
        return [_pair_add(g, r, core, BF16, "rs_pair_add_" + nm) for g, r, nm in zip(by_dest, got, names)]

    names_a = ["pool_mix_w", "w_branch_a", "w_branch_b", "w_out", "w_ple_gate", "w_ple_up"]
    partials_a = [
        g_mixw.reshape(ng, N_DEV, pg // N_DEV, pg).transpose(1, 0, 2, 3).reshape(N_DEV, ng * pg // N_DEV, pg)
        .astype(BF16),
        g_wa.reshape(N_DEV, d_inner // N_DEV, d), g_wb.reshape(N_DEV, d_pool // N_DEV, d),
        g_wout.reshape(N_DEV, d // N_DEV, d), g_wg.reshape(N_DEV, d // N_DEV, d),
        g_wup.reshape(ple_dim, N_DEV, d // N_DEV).transpose(1, 0, 2)]
    dproj, dxbc, ddt, d_gn, d_dsk, d_alog, d_dtb, *parts_a = _ssd_bwd(
        dproj, dya3, y, xbc, proj, dtraw, prev, bias_p, alog_p, dskip_full, gnorm_g, nb, s, d_inner, partials_a)
    dproj, g_cw, d_cb = _conv_bwd(dproj, dxbc, conv_pre, proj, cw_full, nb, s, off_xbc)
    g_wmain = _matmul(h, dproj, "tn", BF16, "dw_in", 512, 1024, t)
    g_wdt = _matmul(h, ddt, "tn", BF16, "dw_in_dt", 512, HEAD_PAD, t)

    def dest_cols(dev):
        lo, hi = dev * chunk, (dev + 1) * chunk
        srcs = [((olo, ohi), g_wmain, off) for (olo, ohi), off in segments] + [((o_dt, o_u), g_wdt, 0)]
        out = []
        for (olo, ohi), arr, off in sorted(srcs, key=lambda e: e[0][0]):
            a, b = max(lo, olo), min(hi, ohi)
            if a < b:
                out.append(arr[:, off + a - olo:off + b - olo])
        return jnp.concatenate(out, axis=1)

    names_b = ["w_in", "conv_w"]
    partials_b = pair_sums([jnp.stack([dest_cols(dev) for dev in range(N_DEV)], axis=0),
                            g_cw.reshape(CONV_W, N_DEV, n_conv // N_DEV).transpose(1, 0, 2)], names_b)
    grad_x, d_ng, *parts_b = _in_bwd(dproj, w_main, ddt, w_dt, x2, norm_g, dx1, partials_b)

    names = names_b + names_a
    parts = parts_b + parts_a
    sh_g, sh_d, sh_m, sh_v = [], [], [], []
    shapes = [w_in.shape, conv_w.shape, pool_mix_w.shape, w_branch_a.shape, w_branch_b.shape, w_out.shape,
              w_ple_gate.shape, w_ple_up.shape]
    for pt, w, m, v, nm, shp in zip(parts, sharded, sharded_m, sharded_v, names, shapes):
        res = _adamw(pt, w, m, v, "adamw_" + nm)
        for dst, r in zip((sh_g, sh_d, sh_m, sh_v), res):
            dst.append(r.reshape(shp))

    rep_w = [norm_g, conv_b, gnorm_g, pool_mix_b, pool_scale, ple_norm_g, final_g.reshape(1, d),
             bias_p, alog_p, _pad_lanes(d_skip)]
    rep_m = [m_norm_g, m_conv_b, m_gnorm_g, m_pool_mix_b, m_pool_scale, m_ple_norm_g, m_final_g.reshape(1, d),
             _pad_lanes(m_dt_bias), _pad_lanes(m_a_log), _pad_lanes(m_d_skip)]
    rep_v = [v_norm_g, v_conv_b, v_gnorm_g, v_pool_mix_b, v_pool_scale, v_ple_norm_g, v_final_g.reshape(1, d),
             _pad_lanes(v_dt_bias), _pad_lanes(v_a_log), _pad_lanes(v_d_skip)]
    rep_g = [d_ng, d_cb, d_gn, d_mixb, d_scale, d_ple_g, dfinal_g, d_dtb, d_alog, d_dsk]
    zero = jnp.zeros((1, LANES), F32)
    small = _all_gather_direct(jnp.concatenate(rep_g + [loss_part], axis=1), "small_all_gather")
    pk = lambda ws: jnp.concatenate(ws + [zero], axis=1)
    rg, rd, rm, rv = _adamw(small, pk(rep_w), pk(rep_m), pk(rep_v), "adamw_replicated")
    loss = rg[0, sum(int(w.size) for w in rep_w)]

    def unpack_rep(buf):
        out, pos_ = [], 0
        for w in rep_w:
            out.append(buf[:, pos_:pos_ + w.size])
            pos_ += w.size
        return out

    def arrange(sh, rep):
        rep = list(rep)
        rep[6] = rep[6].reshape(d)
        for i in (7, 8, 9):
            rep[i] = rep[i][:, :n_heads]
        return [rep[0], sh[0], sh[1], rep[1], rep[7], rep[8], rep[9], rep[2], sh[2], rep[3], rep[4],
                sh[3], sh[4], sh[5], rep[5], sh[6], sh[7], rep[6]]

    outs = [loss, grad_x.reshape(nb, s, d)]
    for sh, rep in ((sh_g, rg), (sh_d, rd), (sh_m, rm), (sh_v, rv)):
        outs += arrange(sh, unpack_rep(rep))
    return tuple(outs)
```

```python
import jax
import jax.numpy as jnp
from jax import lax
from jax.experimental import pallas as pl
from jax.experimental.pallas import tpu as pltpu

F32 = jnp.float32
BF16 = jnp.bfloat16
I32 = jnp.int32
MESH = pl.DeviceIdType.MESH
ANY = pl.BlockSpec(memory_space=pl.ANY)

EPS = 1e-6
CHUNK = 128
HEAD_DIM = 64
D_STATE = 128
GROUPS = 4
CONV_W = 4
POOL_WINDOWS = (2, 4, 8, 16)
ACT = jnp.bfloat16
LANES = 128
SUBLANES = 8
CONV_ROWS = 64
HEAD_PAD = 128
NEG = -1e30
N_DEV = 8

ADAM_LR = 0.001
ADAM_B1 = 0.9
ADAM_B2 = 0.999
ADAM_EPS = 1e-08
ADAM_WD = 0.01
ADAM_STEP = 10

NN = (((1,), (0,)), ((), ()))
NT = (((1,), (1,)), ((), ()))
TN = (((0,), (0,)), ((), ()))


def _pallas(body, **kw):
    return pl.pallas_call(body, **kw)


def _cparams(dims=None, vmem_mb=None):
    kw = {}
    if dims is not None:
        kw["dimension_semantics"] = dims
    if vmem_mb is not None:
        kw["vmem_limit_bytes"] = vmem_mb << 20
    return pltpu.CompilerParams(**kw)


def _dot(a, b, dims=NN):
    return lax.dot_general(a, b, dims, preferred_element_type=F32)


def _split3(v):
    hi = v.astype(BF16)
    r = v - hi.astype(F32)
    mid = r.astype(BF16)
    lo = (r - mid.astype(F32)).astype(BF16)
    return hi, mid, lo


def _exact_r(v, m, dims=NN):
    hi, mid, lo = _split3(v)
    return _dot(hi, m, dims) + _dot(mid, m, dims) + _dot(lo, m, dims)


def _exact_l(m, v, dims=NN):
    hi, mid, lo = _split3(v)
    return _dot(m, hi, dims) + _dot(m, mid, dims) + _dot(m, lo, dims)


def _row_tile(rows, target):
    best = None
    for cand in range(8, min(rows, target) + 1, 8):
        if rows % cand == 0:
            best = cand
    assert best is not None, rows
    return best


def _sum_all(v):
    return jnp.sum(jnp.sum(v, axis=1, keepdims=True), axis=0, keepdims=True)


def _iota(shape, dim):
    return lax.broadcasted_iota(I32, shape, dim)


def _sigmoid(x):
    return 0.5 * jnp.tanh(0.5 * x) + 0.5


def _softplus(x):
    return jnp.maximum(x, 0.0) + jnp.log(1.0 + jnp.exp(-jnp.abs(x)))


def _shift_down(v, j, row):
    return jnp.where(row >= j, pltpu.roll(v, j, 0), 0.0)


def _shift_up(v, j, row):
    n = v.shape[0]
    return jnp.where(row < n - j, pltpu.roll(v, n - j, 0), 0.0)


def _matmul(a, b, mode, out_dtype, name, tm, tn, tk):
    if mode == "nn":
        (m, k), n = a.shape, b.shape[1]
    elif mode == "nt":
        (m, k), n = a.shape, b.shape[0]
    else:
        (k, m), n = a.shape, b.shape[1]
    tm, tn, tk = min(tm, m), min(tn, n), min(tk, k)
    assert m % tm == 0 and n % tn == 0 and k % tk == 0, (name, m, n, k)
    nk = k // tk
    dims = {"nn": NN, "nt": NT, "tn": TN}[mode]

    def body_acc(a_ref, b_ref, o_ref, acc_ref):
        kk = pl.program_id(2)

        @pl.when(kk == 0)
        def _():
            acc_ref[...] = jnp.zeros_like(acc_ref)

        acc_ref[...] += _dot(a_ref[...].astype(BF16), b_ref[...].astype(BF16), dims)

        @pl.when(kk == nk - 1)
        def _():
            o_ref[...] = acc_ref[...].astype(o_ref.dtype)

    def body_one(a_ref, b_ref, o_ref):
        o_ref[...] = _dot(a_ref[...].astype(BF16), b_ref[...].astype(BF16), dims).astype(o_ref.dtype)

    body = body_one if nk == 1 else body_acc
    if mode == "tn":
        a_spec = pl.BlockSpec((tk, tm), lambda i, j, kk: (kk, i))
    else:
        a_spec = pl.BlockSpec((tm, tk), lambda i, j, kk: (i, kk))
    if mode == "nt":
        b_spec = pl.BlockSpec((tn, tk), lambda i, j, kk: (j, kk))
    else:
        b_spec = pl.BlockSpec((tk, tn), lambda i, j, kk: (kk, j))
    return _pallas(
        body, name=name, grid=(m // tm, n // tn, nk),
        in_specs=[a_spec, b_spec],
        out_specs=pl.BlockSpec((tm, tn), lambda i, j, kk: (i, j)),
        out_shape=jax.ShapeDtypeStruct((m, n), out_dtype),
        scratch_shapes=[] if nk == 1 else [pltpu.VMEM((tm, tn), F32)],
        compiler_params=_cparams(("parallel", "parallel", "arbitrary"), 56),
    )(a, b)


def _in_proj(h, w_main, w_dt, blocks):
    t, d = h.shape
    n = w_main.shape[1]
    tm, tn = min(1024, t), 1024
    ni, nj = t // tm, n // tn
    gather = _TwoLevelGather(blocks)

    def body(*refs):
        a_ref, b_ref, wdt_ref = refs[:3]
        x_refs = refs[3:3 + gather.n]
        o_ref, dt_ref = refs[3 + gather.n:5 + gather.n]
        got_refs = refs[5 + gather.n:5 + 2 * gather.n]
        sems = refs[5 + 2 * gather.n:]
        i, j = pl.program_id(0), pl.program_id(1)

        @pl.when((i == 0) & (j == 0))
        def _():
            gather.start(x_refs, got_refs, sems)

        o_ref[...] = _dot(a_ref[...], b_ref[...]).astype(o_ref.dtype)

        @pl.when(j == 0)
        def _():
            dt_ref[...] = _dot(a_ref[...], wdt_ref[...])

        @pl.when((i == ni - 1) & (j == nj - 1))
        def _():
            gather.finish(x_refs, got_refs, sems)

    return _pallas(
        body, name="in_proj", grid=(ni, nj),
        in_specs=[pl.BlockSpec((tm, d), lambda i, j: (i, 0)), pl.BlockSpec((d, tn), lambda i, j: (0, j)),
                  pl.BlockSpec((d, HEAD_PAD), lambda i, j: (0, 0))] + gather.in_specs,
        out_specs=[pl.BlockSpec((tm, tn), lambda i, j: (i, j)), pl.BlockSpec((tm, HEAD_PAD), lambda i, j: (i, 0))]
        + gather.out_specs,
        out_shape=[jax.ShapeDtypeStruct((t, n), ACT), jax.ShapeDtypeStruct((t, HEAD_PAD), F32)] + gather.out_shape,
        scratch_shapes=gather.scratch,
        compiler_params=_cparams(("arbitrary", "arbitrary"), 56),
    )(h, w_main, w_dt, *blocks)


def _dw_in(h, dproj, ddt):
    t, d = h.shape
    n = dproj.shape[1]
    tm, tn = 512, 1024

    def body(a_ref, b_ref, ddt_ref, o_ref, odt_ref):
        a = a_ref[...]
        o_ref[...] = _dot(a, b_ref[...], TN).astype(o_ref.dtype)

        @pl.when(pl.program_id(1) == 0)
        def _():
            odt_ref[...] = _dot(a, ddt_ref[...], TN).astype(odt_ref.dtype)

    return _pallas(
        body, name="dw_in", grid=(d // tm, n // tn),
        in_specs=[pl.BlockSpec((t, tm), lambda i, j: (0, i)), pl.BlockSpec((t, tn), lambda i, j: (0, j)),
                  pl.BlockSpec((t, HEAD_PAD), lambda i, j: (0, 0))],
        out_specs=[pl.BlockSpec((tm, tn), lambda i, j: (i, j)), pl.BlockSpec((tm, HEAD_PAD), lambda i, j: (i, 0))],
        out_shape=[jax.ShapeDtypeStruct((d, n), BF16), jax.ShapeDtypeStruct((d, HEAD_PAD), BF16)],
        compiler_params=_cparams(("parallel", "arbitrary"), 56),
    )(h, dproj, ddt)


def _conv_fwd(proj, conv_w, conv_b, nb, s, off_xbc, n_conv):
    cw = 256
    rc = CONV_ROWS

    def body(x_ref, w_ref, b_ref, o_ref, pre_ref, pad_ref):
        pad_ref[0:SUBLANES, :] = jnp.zeros((SUBLANES, cw), F32)
        pad_ref[SUBLANES:, :] = x_ref[...].astype(F32)
        taps = [w_ref[k:k + 1, :] for k in range(CONV_W)]
        bias = b_ref[...]
        for r in range(s // rc):
            win = pad_ref[pl.ds(r * rc, rc + SUBLANES), :]
            acc = bias + win[SUBLANES:, :] * taps[CONV_W - 1]
            for j in range(1, CONV_W):
                acc = acc + pltpu.roll(win, j, 0)[SUBLANES:, :] * taps[CONV_W - 1 - j]
            pre_ref[pl.ds(r * rc, rc), :] = acc.astype(pre_ref.dtype)
            o_ref[pl.ds(r * rc, rc), :] = (acc * _sigmoid(acc)).astype(o_ref.dtype)

    out = pl.BlockSpec((s, cw), lambda b, j: (b, j))
    return _pallas(
        body, name="conv_fwd", grid=(nb, n_conv // cw),
        in_specs=[pl.BlockSpec((s, cw), lambda b, j: (b, off_xbc // cw + j)),
                  pl.BlockSpec((CONV_W, cw), lambda b, j: (0, j)),
                  pl.BlockSpec((1, cw), lambda b, j: (0, j))],
        out_specs=[out, out],
        out_shape=[jax.ShapeDtypeStruct((nb * s, n_conv), ACT), jax.ShapeDtypeStruct((nb * s, n_conv), ACT)],
        scratch_shapes=[pltpu.VMEM((s + SUBLANES, cw), F32)],
        compiler_params=_cparams(("parallel", "parallel"), 48),
    )(proj, conv_w, conv_b)


def _ssd_common(dtraw, bias, alog, n_heads):
    l = dtraw.shape[0]
    lane = _iota((l, HEAD_PAD), 1)
    lane1 = _iota((1, HEAD_PAD), 1)
    apad = jnp.where(lane1 < n_heads, -jnp.exp(alog), 0.0)
    dt = jnp.where(lane < n_heads, _softplus(dtraw + bias), 0.0)
    a = dt * apad
    tri = (_iota((l, l), 1) <= _iota((l, l), 0)).astype(BF16)
    acs = _exact_l(tri, a)
    width = n_heads * HEAD_DIM
    ex = (_iota((HEAD_PAD, width), 1) // HEAD_DIM == _iota((HEAD_PAD, width), 0)).astype(BF16)
    dt_full = _exact_r(dt, ex)
    acs_full = _exact_r(acs, ex)
    return dt, apad, acs, dt_full, acs_full, tri, ex


def _decay_mats(acs, acs_t, h, mask):
    col = acs[:, h:h + 1]
    row = acs_t[h:h + 1, :]
    return jnp.exp(jnp.where(mask, col - row, NEG))


def _ssd_fwd(xbc, proj, dtraw, dt_bias, a_log, dskip_full, gnorm, nb, s, d_inner):
    l = CHUNK
    nc = s // l
    n_heads = d_inner // HEAD_DIM
    gw = d_inner // GROUPS
    pairs_per_group = gw // LANES
    gn_cols = GROUPS * D_STATE

    def body(xs_ref, bm_ref, cm_ref, z_ref, dtr_ref, bias_ref, alog_ref, dsk_ref, gn_ref,
             y_ref, ya3_ref, prev_ref, state_ref):
        @pl.when(pl.program_id(1) == 0)
        def _():
            state_ref[...] = jnp.zeros_like(state_ref)

        xs = xs_ref[...].astype(F32)
        dt, apad, acs, dt_full, acs_full, tri, ex = _ssd_common(
            dtr_ref[...], bias_ref[...], alog_ref[...], n_heads)
        acs_t = acs.T
        e_full = jnp.exp(acs_full)
        last = acs_full[l - 1:l, :]
        dte = jnp.exp(last - acs_full)
        cd = jnp.exp(last)
        xdt = xs * dt_full
        xd = xdt * dte
        mask = _iota((l, l), 1) <= _iota((l, l), 0)
        lane = _iota((l, LANES), 1)
        prev_ref[0] = state_ref[...]
        for g in range(GROUPS):
            gsl = slice(g * gw, (g + 1) * gw)
            bg = bm_ref[:, g * D_STATE:(g + 1) * D_STATE].astype(BF16)
            cg = cm_ref[:, g * D_STATE:(g + 1) * D_STATE].astype(BF16)
            cb = _dot(cg, bg, NT)
            outs = []
            for q in range(pairs_per_group):
                pair = g * pairs_per_group + q
                h0, h1 = 2 * pair, 2 * pair + 1
                xpb = xdt[:, pair * LANES:(pair + 1) * LANES].astype(BF16)
                m0 = (cb * _decay_mats(acs, acs_t, h0, mask)).astype(BF16)
                m1 = (cb * _decay_mats(acs, acs_t, h1, mask)).astype(BF16)
                zero = jnp.zeros_like(xpb)
                rhs = jnp.concatenate([jnp.where(lane < HEAD_DIM, xpb, zero),
                                       jnp.where(lane >= HEAD_DIM, xpb, zero)], axis=0)
                outs.append(_dot(jnp.concatenate([m0, m1], axis=1), rhs))
            prev_g = state_ref[:, gsl]
            y_off = _dot(cg, prev_g.astype(BF16)) * e_full[:, gsl]
            y_g = jnp.concatenate(outs, axis=1) + y_off
            y_ref[:, gsl] = y_g.astype(y_ref.dtype)
            st = _dot(bg, xd[:, gsl].astype(BF16), TN)
            state_ref[:, gsl] = prev_g * cd[:, gsl] + st
            zv = z_ref[:, gsl].astype(F32)
            blk = (y_g + xs[:, gsl] * dsk_ref[:, gsl]) * (zv * _sigmoid(zv))
            rg = lax.rsqrt(jnp.mean(blk * blk, axis=-1, keepdims=True) + EPS)
            ya3_ref[:, gsl] = (blk * rg * gn_ref[:, gsl]).astype(ya3_ref.dtype)

    row = lambda b, c: b * nc + c
    small = lambda n: pl.BlockSpec((1, n), lambda b, c: (0, 0))
    return _pallas(
        body, name="ssd_fwd", grid=(nb, nc),
        in_specs=[pl.BlockSpec((l, d_inner), lambda b, c: (row(b, c), 0)),
                  pl.BlockSpec((l, gn_cols), lambda b, c: (row(b, c), d_inner // gn_cols)),
                  pl.BlockSpec((l, gn_cols), lambda b, c: (row(b, c), d_inner // gn_cols + 1)),
                  pl.BlockSpec((l, d_inner), lambda b, c: (row(b, c), 0)),
                  pl.BlockSpec((l, HEAD_PAD), lambda b, c: (row(b, c), 0)),
                  small(HEAD_PAD), small(HEAD_PAD), small(d_inner), small(d_inner)],
        out_specs=[pl.BlockSpec((l, d_inner), lambda b, c: (row(b, c), 0)),
                   pl.BlockSpec((l, d_inner), lambda b, c: (row(b, c), 0)),
                   pl.BlockSpec((1, D_STATE, d_inner), lambda b, c: (row(b, c), 0, 0))],
        out_shape=[jax.ShapeDtypeStruct((nb * s, d_inner), ACT),
                   jax.ShapeDtypeStruct((nb * s, d_inner), BF16),
                   jax.ShapeDtypeStruct((nb * nc, D_STATE, d_inner), F32)],
        scratch_shapes=[pltpu.VMEM((D_STATE, d_inner), F32)],
        compiler_params=_cparams(("parallel", "arbitrary"), 48),
    )(xbc, xbc, xbc, proj, dtraw, dt_bias, a_log, dskip_full, gnorm)


def _pool_fwd(proj, mixw, mix_b, scale, nb, s, off_u, off_zp, d_pool):
    pg = d_pool // len(POOL_WINDOWS)

    def body(u_ref, zp_ref, w_ref, b_ref, s_ref, pooled_ref, mixed_ref, yb_ref):
        g = pl.program_id(1)
        uv = u_ref[...].astype(F32)
        row = _iota(uv.shape, 0)
        pos = (_iota((uv.shape[0], 1), 0) + 1).astype(F32)
        for gi, w in enumerate(POOL_WINDOWS):
            @pl.when(g == gi)
            def _():
                acc, span = uv, 1
                while span < w:
                    acc = acc + _shift_down(acc, span, row)
                    span *= 2
                pooled_ref[...] = (acc / jnp.minimum(pos, float(w)) - uv).astype(pooled_ref.dtype)

        mixed = _dot(pooled_ref[...], w_ref[0])
        mixed_ref[...] = mixed.astype(mixed_ref.dtype)
        zp = zp_ref[...].astype(F32)
        yb_ref[...] = ((mixed + b_ref[...]) * s_ref[...] * (zp * _sigmoid(zp))).astype(yb_ref.dtype)

    seg = lambda off: pl.BlockSpec((s, pg), lambda b, g: (b, off // pg + g))
    vec = pl.BlockSpec((1, pg), lambda b, g: (0, g))
    return _pallas(
        body, name="pool_fwd", grid=(nb, len(POOL_WINDOWS)),
        in_specs=[seg(off_u), seg(off_zp), pl.BlockSpec((1, pg, pg), lambda b, g: (g, 0, 0)), vec, vec],
        out_specs=[seg(0), seg(0), seg(0)],
        out_shape=[jax.ShapeDtypeStruct((nb * s, d_pool), BF16), jax.ShapeDtypeStruct((nb * s, d_pool), ACT),
                   jax.ShapeDtypeStruct((nb * s, d_pool), BF16)],
        compiler_params=_cparams(("parallel", "parallel"), 48),
    )(proj, proj, mixw, mix_b, scale)


def _merge_to_grads(ya3, wa, yb, wb, proj, off_ga, off_gb, wout, x, ple_g, wg, p, wup, target, final_g, n_main):
    t, d = x.shape
    pd = p.shape[1]
    tm = 256
    assert off_gb == off_ga + d and off_ga % (2 * d) == 0

    def body(a_ref, wa_ref, b_ref, wb_ref, ga_ref, gb_ref, wout_ref, x_ref, pg_ref, wg_ref, p_ref, wup_ref,
             tg_ref, fg_ref,
             mg_ref, h2_ref, dx1_ref, dx1b_ref, dgl_ref, dpu_ref, dpa_ref, dpb_ref, dgate_ref,
             loss_ref, dfg_ref, dpg_ref):
        @pl.when(pl.program_id(0) == 0)
        def _():
            loss_ref[...] = jnp.zeros_like(loss_ref)
            dfg_ref[...] = jnp.zeros_like(dfg_ref)
            dpg_ref[...] = jnp.zeros_like(dpg_ref)

        pa = _dot(a_ref[...], wa_ref[...])
        pb = _dot(b_ref[...], wb_ref[...])
        sa = _sigmoid(ga_ref[...].astype(F32))
        sb = _sigmoid(gb_ref[...].astype(F32))
        merged = (sa * pa + sb * pb).astype(BF16)
        mg_ref[...] = merged
        x1 = x_ref[...] + _dot(merged, wout_ref[...])
        r1 = lax.rsqrt(jnp.mean(x1 * x1, axis=-1, keepdims=True) + EPS)
        xh1 = x1 * r1
        pg = pg_ref[...]
        h2 = (xh1 * pg).astype(BF16)
        h2_ref[...] = h2
        gate = _sigmoid(_dot(h2, wg_ref[...]))
        pu = _dot(p_ref[...].astype(BF16), wup_ref[...])
        x2 = x1 + gate * pu
        r2 = lax.rsqrt(jnp.mean(x2 * x2, axis=-1, keepdims=True) + EPS)
        xh2 = x2 * r2
        gfin = fg_ref[...]
        err = xh2 * gfin - tg_ref[...]
        loss_ref[...] += _sum_all(err * err) * (0.5 / d)
        dy = err * (1.0 / d)
        dfg_ref[...] += jnp.sum(dy * xh2, axis=0, keepdims=True)
        dg = dy * gfin
        dx2 = r2 * (dg - xh2 * jnp.mean(dg * xh2, axis=-1, keepdims=True))
        dgl = (dx2 * pu * gate * (1.0 - gate)).astype(BF16)
        dgl_ref[...] = dgl
        dpu_ref[...] = (dx2 * gate).astype(dpu_ref.dtype)
        dh2 = _dot(dgl, wg_ref[...], NT)
        dpg_ref[...] += jnp.sum(dh2 * xh1, axis=0, keepdims=True)
        dg = dh2 * pg
        dx1 = dx2 + r1 * (dg - xh1 * jnp.mean(dg * xh1, axis=-1, keepdims=True))
        dx1_ref[...] = dx1
        dx1b = dx1.astype(BF16)
        dx1b_ref[...] = dx1b
        dm = _dot(dx1b, wout_ref[...], NT)
        dpa_ref[...] = (dm * sa).astype(dpa_ref.dtype)
        dpb_ref[...] = (dm * sb).astype(dpb_ref.dtype)
        dgate_ref[:, :d] = (dm * pa * sa * (1.0 - sa)).astype(dgate_ref.dtype)
        dgate_ref[:, d:] = (dm * pb * sb * (1.0 - sb)).astype(dgate_ref.dtype)

    rows = lambda n: pl.BlockSpec((tm, n), lambda i: (i, 0))
    resident = lambda a: pl.BlockSpec(a.shape, lambda i: (0, 0), pipeline_mode=pl.Buffered(1))
    vec = lambda n: pl.BlockSpec((1, n), lambda i: (0, 0))
    bf = lambda: jax.ShapeDtypeStruct((t, d), BF16)
    return _pallas(
        body, name="merge_to_grads", grid=(t // tm,),
        in_specs=[rows(ya3.shape[1]), resident(wa), rows(yb.shape[1]), resident(wb),
                  pl.BlockSpec((tm, d), lambda i: (i, off_ga // d)), pl.BlockSpec((tm, d), lambda i: (i, off_gb // d)),
                  resident(wout), rows(d), vec(d), resident(wg), rows(pd), resident(wup), rows(d), vec(d)],
        out_specs=[rows(d)] * 8 + [pl.BlockSpec((tm, 2 * d), lambda i: (i, off_ga // (2 * d))),
                                   vec(LANES), vec(d), vec(d)],
        out_shape=[bf(), bf(), jax.ShapeDtypeStruct((t, d), F32), bf(), bf(), bf(), bf(), bf(),
                   jax.ShapeDtypeStruct((t, n_main), BF16), jax.ShapeDtypeStruct((1, LANES), F32),
                   jax.ShapeDtypeStruct((1, d), F32), jax.ShapeDtypeStruct((1, d), F32)],
        compiler_params=_cparams(("arbitrary",), 60),
    )(ya3, wa, yb, wb, proj, proj, wout, x, ple_g, wg, p, wup, target, final_g)


def _yb_bwd(dproj, dpb, wb, proj, mixed, mix_b, scale, off_zp):
    t, dp = mixed.shape
    tm = 512

    def body(_, dpb_ref, w_ref, zp_ref, mx_ref, b_ref, s_ref, dzp_ref, dmx_ref, dsc_ref, db_ref):
        @pl.when(pl.program_id(0) == 0)
        def _():
            dsc_ref[...] = jnp.zeros_like(dsc_ref)
            db_ref[...] = jnp.zeros_like(db_ref)

        dyb = _dot(dpb_ref[...], w_ref[...], NT)
        zp = zp_ref[...].astype(F32)
        sg = _sigmoid(zp)
        mb = mx_ref[...].astype(F32) + b_ref[...]
        yb0 = mb * s_ref[...]
        dyb0 = dyb * (zp * sg)
        dzp_ref[...] = (dyb * yb0 * (sg * (1.0 + zp * (1.0 - sg)))).astype(dzp_ref.dtype)
        dsc_ref[...] += jnp.sum(dyb0 * mb, axis=0, keepdims=True)
        dmx = dyb0 * s_ref[...]
        db_ref[...] += jnp.sum(dmx, axis=0, keepdims=True)
        dmx_ref[...] = dmx.astype(dmx_ref.dtype)

    rowblk = pl.BlockSpec((tm, dp), lambda i: (i, 0))
    vec = pl.BlockSpec((1, dp), lambda i: (0, 0))
    seg = pl.BlockSpec((tm, dp), lambda i: (i, off_zp // dp))
    return _pallas(
        body, name="yb_bwd", grid=(t // tm,),
        in_specs=[ANY, pl.BlockSpec((tm, wb.shape[1]), lambda i: (i, 0)),
                  pl.BlockSpec(wb.shape, lambda i: (0, 0)), seg, rowblk, vec, vec],
        out_specs=[seg, rowblk, vec, vec],
        out_shape=[jax.ShapeDtypeStruct(dproj.shape, dproj.dtype), jax.ShapeDtypeStruct((t, dp), BF16),
                   jax.ShapeDtypeStruct((1, dp), F32), jax.ShapeDtypeStruct((1, dp), F32)],
        input_output_aliases={0: 0},
        compiler_params=_cparams(("arbitrary",), 56),
    )(dproj, dpb, wb, proj, mixed, mix_b, scale)


def _dmixw(pooled, dmixed, ng):
    t, dp = pooled.shape
    pg = dp // ng
    tk = min(2048, t)
    nk = t // tk

    def body(a_ref, b_ref, o_ref):
        @pl.when(pl.program_id(1) == 0)
        def _():
            o_ref[...] = jnp.zeros_like(o_ref)

        o_ref[0] += _dot(a_ref[...], b_ref[...], TN)

    blk = pl.BlockSpec((tk, pg), lambda g, k: (k, g))
    return _pallas(
        body, name="dmixw", grid=(ng, nk),
        in_specs=[blk, blk],
        out_specs=pl.BlockSpec((1, pg, pg), lambda g, k: (g, 0, 0)),
        out_shape=jax.ShapeDtypeStruct((ng, pg, pg), F32),
        compiler_params=_cparams(("parallel", "arbitrary")),
    )(pooled, dmixed)


def _pool_bwd(dproj, dmixed, mixw, nb, s, off_u):
    dp = dmixed.shape[1]
    pg = dp // len(POOL_WINDOWS)

    def body(_, d_ref, w_ref, o_ref):
        g = pl.program_id(1)
        dv = _dot(d_ref[...], w_ref[0], NT)
        row = _iota(dv.shape, 0)
        pos = (_iota((dv.shape[0], 1), 0) + 1).astype(F32)
        for gi, w in enumerate(POOL_WINDOWS):
            @pl.when(g == gi)
            def _():
                acc, span = dv / jnp.minimum(pos, float(w)), 1
                while span < w:
                    acc = acc + _shift_up(acc, span, row)
                    span *= 2
                o_ref[...] = (acc - dv).astype(o_ref.dtype)

    return _pallas(
        body, name="pool_bwd", grid=(nb, len(POOL_WINDOWS)),
        in_specs=[ANY, pl.BlockSpec((s, pg), lambda b, g: (b, g)), pl.BlockSpec((1, pg, pg), lambda b, g: (g, 0, 0))],
        out_specs=pl.BlockSpec((s, pg), lambda b, g: (b, off_u // pg + g)),
        out_shape=jax.ShapeDtypeStruct(dproj.shape, dproj.dtype),
        input_output_aliases={0: 0},
        compiler_params=_cparams(("parallel", "parallel"), 48),
    )(dproj, dmixed, mixw)


def _ssd_bwd(dproj, dya3, y, xbc, proj, dtraw, prev, dt_bias, a_log, dskip_full, gnorm, nb, s, d_inner, partials):
    l = CHUNK
    nc = s // l
    n_heads = d_inner // HEAD_DIM
    gw = d_inner // GROUPS
    pairs_per_group = gw // LANES
    gn_cols = GROUPS * D_STATE
    n_conv = d_inner + 2 * gn_cols
    last_step = nb * nc - 1

    xch = _DirectExchange(partials)
    n_in, n_out = 13, 7

    def body(*refs):
        (_, dya3_ref, y_ref, xs_ref, bm_ref, cm_ref, z_ref, dtr_ref, prev_ref, bias_ref, alog_ref,
         dsk_ref, gn_ref) = refs[:n_in]
        p_refs = refs[n_in:n_in + xch.n]
        outs = refs[n_in + xch.n:]
        dz_ref, dxbc_ref, ddt_ref, dgn_ref, dskh_ref, dalog_ref, dbias_ref = outs[:n_out]
        got_refs = outs[n_out:n_out + xch.n]
        g_ref, dskacc_ref = outs[n_out + xch.n:n_out + xch.n + 2]
        sems = outs[n_out + xch.n + 2:]
        step = pl.program_id(0) * nc + pl.program_id(1)

        @pl.when(pl.program_id(1) == 0)
        def _():
            g_ref[...] = jnp.zeros_like(g_ref)

        @pl.when(step == 0)
        def _():
            xch.start(p_refs, got_refs, sems)
            dgn_ref[...] = jnp.zeros_like(dgn_ref)
            dalog_ref[...] = jnp.zeros_like(dalog_ref)
            dbias_ref[...] = jnp.zeros_like(dbias_ref)
            dskacc_ref[...] = jnp.zeros_like(dskacc_ref)

        xs = xs_ref[...].astype(F32)
        zv = z_ref[...].astype(F32)
        dsk = dsk_ref[...]
        sg = _sigmoid(zv)
        sz = zv * sg
        ya1 = y_ref[...].astype(F32) + xs * dsk
        ya2 = ya1 * sz
        parts = []
        for g in range(GROUPS):
            gsl = slice(g * gw, (g + 1) * gw)
            blk = ya2[:, gsl]
            rg = lax.rsqrt(jnp.mean(blk * blk, axis=-1, keepdims=True) + EPS)
            xh = blk * rg
            d3 = dya3_ref[:, gsl].astype(F32)
            dgn_ref[:, gsl] += jnp.sum(d3 * xh, axis=0, keepdims=True)
            dg = d3 * gn_ref[:, gsl]
            parts.append(rg * (dg - xh * jnp.mean(dg * xh, axis=-1, keepdims=True)))
        dya2 = jnp.concatenate(parts, axis=1)
        dy = dya2 * sz
        dz_ref[...] = (dya2 * ya1 * (sg * (1.0 + zv * (1.0 - sg)))).astype(dz_ref.dtype)
        dskacc_ref[...] += jnp.sum(dy * xs, axis=0, keepdims=True)

        dtraw_v = dtr_ref[...]
        bias = bias_ref[...]
        dt, apad, acs, dt_full, acs_full, tri, ex = _ssd_common(dtraw_v, bias, alog_ref[...], n_heads)
        acs_t = acs.T
        e_full = jnp.exp(acs_full)
        last = acs_full[l - 1:l, :]
        dte = jnp.exp(last - acs_full)
        cd = jnp.exp(last)
        xdt = xs * dt_full
        xd = xdt * dte
        dy_e = dy * e_full
        mask = _iota((l, l), 1) <= _iota((l, l), 0)
        lane = _iota((l, LANES), 1)
        hlane = _iota((l, HEAD_PAD), 1)
        hsub = _iota((HEAD_PAD, l), 0)
        dacs = jnp.zeros((l, HEAD_PAD), F32)
        dacs_t = jnp.zeros((HEAD_PAD, l), F32)
        dacs_full, dlast, dxdt_all, db_all, dc_all = [], [], [], [], []
        for g in range(GROUPS):
            gsl = slice(g * gw, (g + 1) * gw)
            bg = bm_ref[:, g * D_STATE:(g + 1) * D_STATE].astype(BF16)
            cg = cm_ref[:, g * D_STATE:(g + 1) * D_STATE].astype(BF16)
            prev_g = prev_ref[0, :, gsl]
            prev_b = prev_g.astype(BF16)
            gst = g_ref[:, gsl]
            gst_b = gst.astype(BF16)
            dye_b = dy_e[:, gsl].astype(BF16)
            dc = _dot(dye_b, prev_b, NT)
            dprev = _dot(cg, dye_b, TN)
            y_off = _dot(cg, prev_b) * e_full[:, gsl]
            xd_g = xd[:, gsl]
            db = _dot(xd_g.astype(BF16), gst_b, NT)
            dxd = _dot(bg, gst_b)
            wgt = dxd * xd_g
            dacs_full.append(dy[:, gsl] * y_off - wgt)
            dlast.append(jnp.sum(wgt, axis=0, keepdims=True)
                         + jnp.sum(prev_g * gst, axis=0, keepdims=True) * cd[:, gsl])
            g_ref[:, gsl] = dprev + cd[:, gsl] * gst
            dxdt_g = dxd * dte[:, gsl]
            cb = _dot(cg, bg, NT)
            dcb = jnp.zeros((l, l), F32)
            for q in range(pairs_per_group):
                pair = g * pairs_per_group + q
                psl = slice(pair * LANES, (pair + 1) * LANES)
                xpb = xdt[:, psl].astype(BF16)
                dyp = dy[:, psl].astype(BF16)
                zero = jnp.zeros_like(dyp)
                halves = (jnp.where(lane < HEAD_DIM, dyp, zero), jnp.where(lane >= HEAD_DIM, dyp, zero))
                acc = dxdt_g[:, q * LANES:(q + 1) * LANES]
                for k, dyh in enumerate(halves):
                    h = 2 * pair + k
                    dk = _decay_mats(acs, acs_t, h, mask)
                    mm = cb * dk
                    dm = _dot(dyh, xpb, NT)
                    dcb = dcb + dm * dk
                    dseg = dm * mm
                    dacs = dacs + jnp.where(hlane == h, jnp.sum(dseg, axis=1, keepdims=True), 0.0)
                    dacs_t = dacs_t + jnp.where(hsub == h, jnp.sum(dseg, axis=0, keepdims=True), 0.0)
                    acc = acc + _dot(mm.astype(BF16), dyh, TN)
                dxdt_all.append(acc)
            dcb_b = dcb.astype(BF16)
            dc_all.append(dc + _dot(dcb_b, bg))
            db_all.append(db + _dot(dcb_b, cg, TN))

        dxdt = jnp.concatenate(dxdt_all, axis=1)
        dxbc_ref[:, :d_inner] = (dy * dsk + dxdt * dt_full).astype(dxbc_ref.dtype)
        dxbc_ref[:, d_inner:d_inner + gn_cols] = jnp.concatenate(db_all, axis=1).astype(dxbc_ref.dtype)
        dxbc_ref[:, d_inner + gn_cols:] = jnp.concatenate(dc_all, axis=1).astype(dxbc_ref.dtype)
        ddt = _exact_r(dxdt * xs, ex, NT)
        rowi = _iota((l, d_inner), 0)
        dacs_f = jnp.concatenate(dacs_full, axis=1) + jnp.where(rowi == l - 1, jnp.concatenate(dlast, axis=1), 0.0)
        dacs = dacs + _exact_r(dacs_f, ex, NT) - dacs_t.T
        da = _exact_l(tri, dacs, TN)
        ddt = ddt + da * apad
        dalog_ref[...] += jnp.sum(da * dt, axis=0, keepdims=True) * apad
        ddt_raw = ddt * _sigmoid(dtraw_v + bias)
        dbias_ref[...] += jnp.sum(ddt_raw, axis=0, keepdims=True)
        ddt_ref[...] = ddt_raw.astype(ddt_ref.dtype)

        @pl.when(step == last_step)
        def _():
            dskh_ref[...] = _exact_r(dskacc_ref[...], ex, NT)[0:1, :]
            xch.wait(p_refs, got_refs, sems)

    row = lambda b, c: b * nc + (nc - 1 - c)
    small = lambda n: pl.BlockSpec((1, n), lambda b, c: (0, 0))
    wide = pl.BlockSpec((l, d_inner), lambda b, c: (row(b, c), 0))
    return _pallas(
        body, name="ssd_bwd", grid=(nb, nc),
        in_specs=[ANY, wide, wide, wide,
                  pl.BlockSpec((l, gn_cols), lambda b, c: (row(b, c), d_inner // gn_cols)),
                  pl.BlockSpec((l, gn_cols), lambda b, c: (row(b, c), d_inner // gn_cols + 1)),
                  wide, pl.BlockSpec((l, HEAD_PAD), lambda b, c: (row(b, c), 0)),
                  pl.BlockSpec((1, D_STATE, d_inner), lambda b, c: (row(b, c), 0, 0)),
                  small(HEAD_PAD), small(HEAD_PAD), small(d_inner), small(d_inner)] + xch.in_specs,
        out_specs=[wide, pl.BlockSpec((l, n_conv), lambda b, c: (row(b, c), 0)),
                   pl.BlockSpec((l, HEAD_PAD), lambda b, c: (row(b, c), 0)),
                   small(d_inner), small(HEAD_PAD), small(HEAD_PAD), small(HEAD_PAD)] + xch.out_specs,
        out_shape=[jax.ShapeDtypeStruct(dproj.shape, dproj.dtype),
                   jax.ShapeDtypeStruct((nb * s, n_conv), ACT),
                   jax.ShapeDtypeStruct((nb * s, HEAD_PAD), BF16),
                   jax.ShapeDtypeStruct((1, d_inner), F32), jax.ShapeDtypeStruct((1, HEAD_PAD), F32),
                   jax.ShapeDtypeStruct((1, HEAD_PAD), F32), jax.ShapeDtypeStruct((1, HEAD_PAD), F32)]
        + xch.out_shape,
        scratch_shapes=[pltpu.VMEM((D_STATE, d_inner), F32), pltpu.VMEM((8, d_inner), F32)] + xch.scratch,
        input_output_aliases={0: 0},
        compiler_params=_cparams(("arbitrary", "arbitrary"), 56),
    )(dproj, dya3, y, xbc, xbc, xbc, proj, dtraw, prev, dt_bias, a_log, dskip_full, gnorm, *partials)


def _conv_bwd(dproj, dxbc, conv_pre, proj, conv_w, nb, s, off_xbc):
    n_conv = dxbc.shape[1]
    cw = 256

    rc = CONV_ROWS

    def fold(v):
        return jnp.sum(v.reshape(rc // SUBLANES, SUBLANES, cw), axis=0)

    def body(_, d_ref, pre_ref, x_ref, w_ref, dx_ref, dw_ref, db_ref, dcpad_ref):
        @pl.when(pl.program_id(1) == 0)
        def _():
            dw_ref[...] = jnp.zeros_like(dw_ref)
            db_ref[...] = jnp.zeros_like(db_ref)

        dcpad_ref[pl.ds(s, SUBLANES), :] = jnp.zeros((SUBLANES, cw), F32)
        taps = [w_ref[k:k + 1, :] for k in range(CONV_W)]
        db = jnp.zeros((SUBLANES, cw), F32)
        for r in range(s // rc):
            rows = pl.ds(r * rc, rc)
            acc = pre_ref[rows, :].astype(F32)
            sg = _sigmoid(acc)
            dc = d_ref[rows, :].astype(F32) * (sg * (1.0 + acc * (1.0 - sg)))
            dcpad_ref[rows, :] = dc
            db = db + fold(dc)
        db_ref[...] += jnp.sum(db, axis=0, keepdims=True)
        dw = [jnp.zeros((SUBLANES, cw), F32) for _ in range(CONV_W)]
        for r in range(s // rc):
            xv = x_ref[pl.ds(r * rc, rc), :].astype(F32)
            win = dcpad_ref[pl.ds(r * rc, rc + SUBLANES), :]
            dxv = jnp.zeros((rc, cw), F32)
            for j in range(CONV_W):
                up = win[:rc, :] if j == 0 else pltpu.roll(win, rc + SUBLANES - j, 0)[:rc, :]
                dxv = dxv + up * taps[CONV_W - 1 - j]
                dw[CONV_W - 1 - j] = dw[CONV_W - 1 - j] + fold(up * xv)
            dx_ref[pl.ds(r * rc, rc), :] = dxv.astype(dx_ref.dtype)
        for k in range(CONV_W):
            dw_ref[k:k + 1, :] += jnp.sum(dw[k], axis=0, keepdims=True)

    blk = pl.BlockSpec((s, cw), lambda j, b: (b, j))
    return _pallas(
        body, name="conv_bwd", grid=(n_conv // cw, nb),
        in_specs=[ANY, blk, blk, pl.BlockSpec((s, cw), lambda j, b: (b, off_xbc // cw + j)),
                  pl.BlockSpec((CONV_W, cw), lambda j, b: (0, j))],
        out_specs=[pl.BlockSpec((s, cw), lambda j, b: (b, off_xbc // cw + j)),
                   pl.BlockSpec((CONV_W, cw), lambda j, b: (0, j)), pl.BlockSpec((1, cw), lambda j, b: (0, j))],
        out_shape=[jax.ShapeDtypeStruct(dproj.shape, dproj.dtype),
                   jax.ShapeDtypeStruct((CONV_W, n_conv), F32), jax.ShapeDtypeStruct((1, n_conv), F32)],
        scratch_shapes=[pltpu.VMEM((s + SUBLANES, cw), F32)],
        input_output_aliases={0: 0},
        compiler_params=_cparams(("parallel", "arbitrary"), 48),
    )(dproj, dxbc, conv_pre, proj, conv_w)


def _in_bwd(dproj, w_main, ddt, w_dt, x, norm_g, dx1, partials):
    t, d = x.shape
    n_main = dproj.shape[1]
    tm = 256
    ni = t // tm
    xch = _ChipExchange(partials)
    n_in, n_out = 7, 2

    def body(*refs):
        dp_ref, w_ref, ddt_ref, wdt_ref, x_ref, g_ref, dx1_ref = refs[:n_in]
        p_refs = refs[n_in:n_in + xch.n]
        outs = refs[n_in + xch.n:]
        dx_ref, dg_ref = outs[:n_out]
        got_refs = outs[n_out:n_out + xch.n]
        sems = outs[n_out + xch.n:]
        i = pl.program_id(0)

        @pl.when(i == 0)
        def _():
            xch.start(p_refs, got_refs, sems)
            dg_ref[...] = jnp.zeros_like(dg_ref)

        dh = _dot(dp_ref[...], w_ref[...], NT) + _dot(ddt_ref[...], wdt_ref[...], NT)
        xv = x_ref[...]
        r = lax.rsqrt(jnp.mean(xv * xv, axis=-1, keepdims=True) + EPS)
        xh = xv * r
        dg_ref[...] += jnp.sum(dh * xh, axis=0, keepdims=True)
        dg = dh * g_ref[...]
        dx_ref[...] = dx1_ref[...] + r * (dg - xh * jnp.mean(dg * xh, axis=-1, keepdims=True))

        @pl.when(i == ni - 1)
        def _():
            xch.wait(p_refs, got_refs, sems)

    rowblk = pl.BlockSpec((tm, d), lambda i: (i, 0))
    vec = pl.BlockSpec((1, d), lambda i: (0, 0))
    resident = lambda shape: pl.BlockSpec(shape, lambda i: (0, 0), pipeline_mode=pl.Buffered(1))
    return _pallas(
        body, name="in_bwd", grid=(ni,),
        in_specs=[pl.BlockSpec((tm, n_main), lambda i: (i, 0)), resident((d, n_main)),
                  pl.BlockSpec((tm, HEAD_PAD), lambda i: (i, 0)), resident((d, HEAD_PAD)),
                  rowblk, vec, rowblk] + xch.in_specs,
        out_specs=[rowblk, vec] + xch.out_specs,
        out_shape=[jax.ShapeDtypeStruct((t, d), F32), jax.ShapeDtypeStruct((1, d), F32)] + xch.out_shape,
        scratch_shapes=xch.scratch,
        compiler_params=_cparams(("arbitrary",), 56),
    )(dproj, w_main, ddt, w_dt, x, norm_g, dx1, *partials)


def _adamw_math(w, g, m, v):
    m = ADAM_B1 * m + (1.0 - ADAM_B1) * g
    v = ADAM_B2 * v + (1.0 - ADAM_B2) * (g * g)
    m_hat = m / (1.0 - ADAM_B1 ** ADAM_STEP)
    v_hat = v / (1.0 - ADAM_B2 ** ADAM_STEP)
    delta = -ADAM_LR * (m_hat / (jnp.sqrt(v_hat) + ADAM_EPS) + ADAM_WD * w)
    return delta, m, v


def _adamw(parts, w, m, v, name):
    n, rows, cols = parts.shape
    tr = _row_tile(rows, 128) if rows % 8 == 0 else rows

    def body(p_ref, w_ref, m_ref, v_ref, g_ref, d_ref, mo_ref, vo_ref):
        g = p_ref[0].astype(F32)
        for k in range(1, n):
            g = g + p_ref[k].astype(F32)
        d, mn, vn = _adamw_math(w_ref[...], g, m_ref[...], v_ref[...])
        g_ref[...] = g
        d_ref[...] = d
        mo_ref[...] = mn
        vo_ref[...] = vn

    blk = pl.BlockSpec((tr, cols), lambda i: (i, 0))
    shp = jax.ShapeDtypeStruct((rows, cols), F32)
    return _pallas(
        body, name=name, grid=(rows // tr,),
        in_specs=[pl.BlockSpec((n, tr, cols), lambda i: (0, i, 0)), blk, blk, blk],
        out_specs=[blk, blk, blk, blk], out_shape=[shp, shp, shp, shp],
        compiler_params=_cparams(("parallel",), 56),
    )(parts, w, m, v)


def _coords():
    return lax.axis_index("x"), lax.axis_index("y"), lax.axis_index("c")


class _TwoLevelGather:
    def __init__(self, blocks):
        self.n = len(blocks)
        self.in_specs = [ANY] * self.n
        self.out_specs = [ANY] * self.n
        self.out_shape = [jax.ShapeDtypeStruct((N_DEV,) + b.shape, b.dtype) for b in blocks]
        self.scratch = [pltpu.SemaphoreType.DMA((self.n, 7)), pltpu.SemaphoreType.DMA((self.n, 7)),
                        pltpu.SemaphoreType.DMA((self.n,))]

    def _parts(self, x_refs, out_refs, sems):
        send_sems, recv_sems, local_sems = sems
        x, y, c = _coords()
        me, sibling = (x, y, c), (x, y, 1 - c)
        chips = [(1 - x, y), (x, 1 - y), (1 - x, 1 - y)]

        def slot(a, px, py, pc):
            return out_refs[a].at[4 * px + 2 * py + pc]

        def copy(a, k, block, to, src=None):
            return pltpu.make_async_remote_copy(
                src_ref=slot(a, *block) if src is None else src, dst_ref=slot(a, *block),
                send_sem=send_sems.at[a, k], recv_sem=recv_sems.at[a, k], device_id=to, device_id_type=MESH)

        local = [pltpu.make_async_copy(x_refs[a], slot(a, *me), local_sems.at[a]) for a in range(self.n)]
        first = []
        for a in range(self.n):
            first.append(copy(a, 0, me, sibling, src=x_refs[a]))
            first += [copy(a, 1 + j, me, (*chip, c), src=x_refs[a]) for j, chip in enumerate(chips)]
        return copy, local, first, chips, me, sibling, c

    def start(self, x_refs, out_refs, sems):
        _, local, first, *_ = self._parts(x_refs, out_refs, sems)
        for cp in local + first:
            cp.start()

    def finish(self, x_refs, out_refs, sems):
        copy, local, first, chips, me, sibling, c = self._parts(x_refs, out_refs, sems)
        passed = []
        for j, chip in enumerate(chips):
            for a in range(self.n):
                copy(a, 1 + j, (*chip, c), me).wait_recv()
                passed.append(copy(a, 4 + j, (*chip, c), sibling))
                passed[-1].start()
        for a in range(self.n):
            copy(a, 0, sibling, me).wait_recv()
            for j, chip in enumerate(chips):
                copy(a, 4 + j, (*chip, 1 - c), me).wait_recv()
        for cp in first + passed:
            cp.wait_send()
        for cp in local:
            cp.wait()


def _rms_in_and_gather(xin, g, blk):
    rows, cols = blk.shape
    half = rows // 2
    assert half % 16 == 0
    t, d = xin.shape
    tt = 512
    ni = t // tt

    def exchange(x_ref, out_ref, send_sems, recv_sems, local_sem, begin):
        x, y, c = _coords()
        me, sib, xn, yn, dg = (x, y, c), (x, y, 1 - c), (1 - x, y, c), (x, 1 - y, c), (1 - x, 1 - y, c)
        other = lambda p: (p[0], p[1], 1 - c)
        top, bot = pl.ds(0, half), pl.ds(half, half)

        def slot(p, part=None):
            ref = out_ref.at[4 * p[0] + 2 * p[1] + p[2]]
            return ref if part is None else ref.at[part]

        def copy(k, block, to, part=None, src=None):
            return pltpu.make_async_remote_copy(
                src_ref=slot(block, part) if src is None else src, dst_ref=slot(block, part),
                send_sem=send_sems.at[k], recv_sem=recv_sems.at[k], device_id=to, device_id_type=MESH)

        mine = pltpu.make_async_copy(x_ref, slot(me), local_sem)
        own = [copy(1, me, xn, src=x_ref), copy(2, me, yn, src=x_ref), copy(0, me, sib, src=x_ref)]
        if begin:
            for cp in [mine] + own:
                cp.start()
            return
        copy(1, xn, me).wait_recv()
        fwd = [copy(3, xn, yn, top), copy(5, xn, sib)]
        for cp in fwd:
            cp.start()
        copy(2, yn, me).wait_recv()
        fwd += [copy(4, yn, xn, bot), copy(6, yn, sib)]
        for cp in fwd[2:]:
            cp.start()
        copy(3, dg, me, top).wait_recv()
        fwd.append(copy(7, dg, sib, top))
        fwd[-1].start()
        copy(4, dg, me, bot).wait_recv()
        fwd.append(copy(8, dg, sib, bot))
        fwd[-1].start()
        copy(0, sib, me).wait_recv()
        copy(5, other(xn), me).wait_recv()
        copy(6, other(yn), me).wait_recv()
        copy(7, other(dg), me, top).wait_recv()
        copy(8, other(dg), me, bot).wait_recv()
        for cp in own + fwd:
            cp.wait_send()
        mine.wait()

    def body(w_ref, x_ref, g_ref, out_ref, h_ref, send_sems, recv_sems, local_sem):
        i = pl.program_id(0)

        @pl.when(i == 0)
        def _():
            exchange(w_ref, out_ref, send_sems, recv_sems, local_sem, True)

        xv = x_ref[...]
        r = lax.rsqrt(jnp.mean(xv * xv, axis=-1, keepdims=True) + EPS)
        h_ref[...] = (xv * r * g_ref[...]).astype(h_ref.dtype)

        @pl.when(i == ni - 1)
        def _():
            exchange(w_ref, out_ref, send_sems, recv_sems, local_sem, False)

    return _pallas(
        body, name="rms_in_w_in_all_gather", grid=(ni,),
        in_specs=[ANY, pl.BlockSpec((tt, d), lambda i: (i, 0)), pl.BlockSpec((1, d), lambda i: (0, 0))],
        out_specs=[ANY, pl.BlockSpec((tt, d), lambda i: (i, 0))],
        out_shape=[jax.ShapeDtypeStruct((N_DEV, rows, cols), blk.dtype), jax.ShapeDtypeStruct((t, d), BF16)],
        scratch_shapes=[pltpu.SemaphoreType.DMA((9,)), pltpu.SemaphoreType.DMA((9,)), pltpu.SemaphoreType.DMA],
        compiler_params=_cparams(("arbitrary",)),
    )(blk, xin, g)


def _all_gather_direct(blk, name):
    rows, cols = blk.shape

    def body(x_ref, out_ref, send_sems, recv_sems, local_sem):
        x, y, c = _coords()
        me = 4 * x + 2 * y + c
        mine = pltpu.make_async_copy(x_ref, out_ref.at[me], local_sem)
        mine.start()
        copies = []
        for k in range(1, N_DEV):
            fx, fy, fc = (k >> 2) & 1, (k >> 1) & 1, k & 1
            peer = (x ^ fx, y ^ fy, c ^ fc)
            copies.append(pltpu.make_async_remote_copy(
                src_ref=x_ref, dst_ref=out_ref.at[me], send_sem=send_sems.at[k - 1], recv_sem=recv_sems.at[k - 1],
                device_id=peer, device_id_type=MESH))
        for cp in copies:
            cp.start()
        for k in range(1, N_DEV):
            fx, fy, fc = (k >> 2) & 1, (k >> 1) & 1, k & 1
            src = 4 * (x ^ fx) + 2 * (y ^ fy) + (c ^ fc)
            pltpu.make_async_remote_copy(
                src_ref=x_ref, dst_ref=out_ref.at[src], send_sem=send_sems.at[k - 1], recv_sem=recv_sems.at[k - 1],
                device_id=(x, y, c), device_id_type=MESH).wait_recv()
        for cp in copies:
            cp.wait_send()
        mine.wait()

    return _pallas(
        body, name=name, in_specs=[ANY], out_specs=ANY,
        out_shape=jax.ShapeDtypeStruct((N_DEV, rows, cols), blk.dtype),
        scratch_shapes=[pltpu.SemaphoreType.DMA((7,)), pltpu.SemaphoreType.DMA((7,)), pltpu.SemaphoreType.DMA],
    )(blk)


def _pair_exchange(grads, name):
    n = len(grads)

    def body(*refs):
        g_refs, out_refs = refs[:n], refs[n:2 * n]
        send_sems, recv_sems = refs[2 * n:]
        x, y, c = _coords()
        copies = []
        for a in range(n):
            for k in range(4):
                copies.append(pltpu.make_async_remote_copy(
                    src_ref=g_refs[a].at[2 * k + (1 - c)], dst_ref=out_refs[a].at[k],
                    send_sem=send_sems.at[a, k], recv_sem=recv_sems.at[a, k],
                    device_id=(x, y, 1 - c), device_id_type=MESH))
        for cp in copies:
            cp.start()
        for cp in copies:
            cp.wait()

    return _pallas(
        body, name=name, in_specs=[ANY] * n, out_specs=[ANY] * n,
        out_shape=[jax.ShapeDtypeStruct((4,) + g.shape[1:], g.dtype) for g in grads],
        scratch_shapes=[pltpu.SemaphoreType.DMA((n, 4)), pltpu.SemaphoreType.DMA((n, 4))],
    )(*grads)


def _pair_add(grads, got, core, out_dtype, name):
    _, rows, cols = grads.shape
    tr = _row_tile(rows, 256) if rows % 8 == 0 else rows

    def body(c_ref, a_ref, b_ref, o_ref):
        o_ref[...] = (a_ref[...].astype(F32) + b_ref[...].astype(F32)).astype(o_ref.dtype)

    grid_spec = pltpu.PrefetchScalarGridSpec(
        num_scalar_prefetch=1, grid=(4, rows // tr),
        in_specs=[pl.BlockSpec((1, tr, cols), lambda k, i, c_ref: (2 * k + c_ref[0], i, 0)),
                  pl.BlockSpec((1, tr, cols), lambda k, i, c_ref: (k, i, 0))],
        out_specs=pl.BlockSpec((1, tr, cols), lambda k, i, c_ref: (k, i, 0)))
    return _pallas(
        body, name=name, grid_spec=grid_spec,
        out_shape=jax.ShapeDtypeStruct((4, rows, cols), out_dtype),
        compiler_params=_cparams(("parallel", "parallel"), 48),
    )(core, grads, got)


class _ChipExchange:
    def __init__(self, partials):
        self.n = len(partials)
        self.in_specs = [ANY] * self.n
        self.out_specs = [ANY] * self.n
        self.out_shape = [jax.ShapeDtypeStruct(p.shape, p.dtype) for p in partials]
        self.scratch = [pltpu.SemaphoreType.DMA((self.n, 3)), pltpu.SemaphoreType.DMA((self.n, 3)),
                        pltpu.SemaphoreType.DMA((self.n,))]

    def _copies(self, p_refs, out_refs, sems, receiving):
        send_sems, recv_sems, local_sems = sems
        x, y, c = _coords()
        mychip = 2 * x + y
        local = [pltpu.make_async_copy(p_refs[a].at[mychip], out_refs[a].at[mychip], local_sems.at[a])
                 for a in range(self.n)]
        remote = []
        for a in range(self.n):
            for k in range(1, 4):
                px, py = x ^ ((k >> 1) & 1), y ^ (k & 1)
                peer = 2 * px + py
                remote.append(pltpu.make_async_remote_copy(
                    src_ref=p_refs[a].at[peer], dst_ref=out_refs[a].at[peer if receiving else mychip],
                    send_sem=send_sems.at[a, k - 1], recv_sem=recv_sems.at[a, k - 1],
                    device_id=(x, y, c) if receiving else (px, py, c), device_id_type=MESH))
        return local, remote

    def start(self, p_refs, out_refs, sems):
        local, sends = self._copies(p_refs, out_refs, sems, False)
        for cp in local + sends:
            cp.start()

    def wait(self, p_refs, out_refs, sems):
        local, recvs = self._copies(p_refs, out_refs, sems, True)
        for cp in recvs:
            cp.wait_recv()
        for cp in recvs:
            cp.wait_send()
        for cp in local:
            cp.wait()


class _DirectExchange:
    def __init__(self, partials):
        self.n = len(partials)
        self.in_specs = [ANY] * self.n
        self.out_specs = [ANY] * self.n
        self.out_shape = [jax.ShapeDtypeStruct(p.shape, p.dtype) for p in partials]
        self.scratch = [pltpu.SemaphoreType.DMA((self.n, N_DEV - 1)), pltpu.SemaphoreType.DMA((self.n, N_DEV - 1)),
                        pltpu.SemaphoreType.DMA((self.n,))]

    def _copies(self, p_refs, out_refs, sems, receiving):
        send_sems, recv_sems, local_sems = sems
        x, y, c = _coords()
        me = 4 * x + 2 * y + c
        local = [pltpu.make_async_copy(p_refs[a].at[me], out_refs[a].at[me], local_sems.at[a])
                 for a in range(self.n)]
        remote = []
        for a in range(self.n):
            for k in range(1, N_DEV):
                px, py, pc = x ^ ((k >> 2) & 1), y ^ ((k >> 1) & 1), c ^ (k & 1)
                peer = 4 * px + 2 * py + pc
                remote.append(pltpu.make_async_remote_copy(
                    src_ref=p_refs[a].at[peer], dst_ref=out_refs[a].at[peer if receiving else me],
                    send_sem=send_sems.at[a, k - 1], recv_sem=recv_sems.at[a, k - 1],
                    device_id=(x, y, c) if receiving else (px, py, pc), device_id_type=MESH))
        return local, remote

    def start(self, p_refs, out_refs, sems):
        local, sends = self._copies(p_refs, out_refs, sems, False)
        for cp in local + sends:
            cp.start()

    def wait(self, p_refs, out_refs, sems):
        local, recvs = self._copies(p_refs, out_refs, sems, True)
        for cp in recvs:
            cp.wait_recv()
        for cp in recvs:
            cp.wait_send()
        for cp in local:
            cp.wait()


def _pad_lanes(v, n=HEAD_PAD):
    return jnp.pad(v, ((0, 0), (0, n - v.shape[1])))


def kernel(x, p, norm_g, w_in, conv_w, conv_b, dt_bias, a_log, d_skip, gnorm_g, pool_mix_w, pool_mix_b, pool_scale, w_branch_a, w_branch_b, w_out, ple_norm_g, w_ple_gate, w_ple_up, final_g, loss_target, m_norm_g, m_w_in, m_conv_w, m_conv_b, m_dt_bias, m_a_log, m_d_skip, m_gnorm_g, m_pool_mix_w, m_pool_mix_b, m_pool_scale, m_w_branch_a, m_w_branch_b, m_w_out, m_ple_norm_g, m_w_ple_gate, m_w_ple_up, m_final_g, v_norm_g, v_w_in, v_conv_w, v_conv_b, v_dt_bias, v_a_log, v_d_skip, v_gnorm_g, v_pool_mix_w, v_pool_mix_b, v_pool_scale, v_w_branch_a, v_w_branch_b, v_w_out, v_ple_norm_g, v_w_ple_gate, v_w_ple_up, v_final_g):
    nb, s, d = x.shape
    t = nb * s
    n_heads = dt_bias.shape[1]
    d_inner = n_heads * HEAD_DIM
    n_conv = conv_b.shape[1]
    d_pool = pool_mix_b.shape[1]
    n_in = w_in.shape[2] * N_DEV
    ple_dim = p.shape[-1]
    o_xbc, o_dt = d_inner, d_inner + n_conv
    o_u = o_dt + n_heads
    o_zp = o_u + d_pool
    o_ga, o_gb = o_zp + d_pool, o_zp + d_pool + d
    n_main = n_in - n_heads
    off_ga, off_gb, off_xbc = d_inner, d_inner + d, d_inner + 2 * d
    off_u = off_xbc + n_conv
    off_zp = off_u + d_pool

    sharded = [w_in[0], conv_w[0], pool_mix_w[0], w_branch_a[0], w_branch_b[0], w_out[0], w_ple_gate[0], w_ple_up[0]]
    sharded_m = [m_w_in[0], m_conv_w[0], m_pool_mix_w[0], m_w_branch_a[0], m_w_branch_b[0], m_w_out[0],
                 m_w_ple_gate[0], m_w_ple_up[0]]
    sharded_v = [v_w_in[0], v_conv_w[0], v_pool_mix_w[0], v_w_branch_a[0], v_w_branch_b[0], v_w_out[0],
                 v_w_ple_gate[0], v_w_ple_up[0]]
    ng = pool_mix_w.shape[1]
    pg = d_pool // ng
    two_d = lambda w: w.reshape(-1, w.shape[-1])
    sharded, sharded_m, sharded_v = ([two_d(w) for w in ws] for ws in (sharded, sharded_m, sharded_v))

    send = [w if i == 1 else w.astype(BF16) for i, w in enumerate(sharded)]
    x2 = x.reshape(t, d)
    w_in_all, h = _rms_in_and_gather(x2, norm_g, send[0])
    seg = [w_in_all]
    chunk = n_in // N_DEV
    segments = [((0, o_xbc), 0), ((o_ga, o_gb), off_ga), ((o_gb, n_in), off_gb), ((o_xbc, o_dt), off_xbc),
                ((o_u, o_zp), off_u), ((o_zp, o_ga), off_zp)]

    def gathered_cols(lo, hi):
        out = []
        while lo < hi:
            dev = lo // chunk
            a, b = lo - dev * chunk, min(hi - dev * chunk, chunk)
            out.append(seg[0][dev, :, a:b])
            lo = dev * chunk + b
        return out

    w_main = jnp.concatenate([pc for (lo, hi), _ in segments for pc in gathered_cols(lo, hi)], axis=1)
    w_dt = _pad_lanes(jnp.concatenate(gathered_cols(o_dt, o_u), axis=1))

    proj, dtraw, *rest = _in_proj(h, w_main, w_dt, send[1:])
    seg += rest
    cw_full = seg[1].transpose(1, 0, 2).reshape(CONV_W, n_conv)
    mixw_full = seg[2].reshape(N_DEV, ng, pg // N_DEV, pg).transpose(1, 0, 2, 3).reshape(ng, pg, pg)
    wa_full = seg[3].reshape(d_inner, d)
    wb_full = seg[4].reshape(d_pool, d)
    wout_full = seg[5].reshape(d, d)
    wg_full = seg[6].reshape(d, d)
    wup_full = seg[7].transpose(1, 0, 2).reshape(ple_dim, d)

    p2 = p.reshape(t, ple_dim)
    tgt = loss_target.reshape(t, d)
    bias_p, alog_p = _pad_lanes(dt_bias), _pad_lanes(a_log)
    dskip_full = jnp.repeat(d_skip, HEAD_DIM, axis=1)

    xbc, conv_pre = _conv_fwd(proj, cw_full, conv_b, nb, s, off_xbc, n_conv)
    y, ya3, prev = _ssd_fwd(xbc, proj, dtraw, bias_p, alog_p, dskip_full, gnorm_g, nb, s, d_inner)
    pooled, mixed, yb = _pool_fwd(proj, mixw_full, pool_mix_b, pool_scale, nb, s, off_u, off_zp, d_pool)
    (merged, h2, dx1, dx1b, dgl, dpu, dpa, dpb, dproj, loss_part, dfinal_g, d_ple_g) = _merge_to_grads(
        ya3, wa_full, yb, wb_full, proj, off_ga, off_gb, wout_full, x2, ple_norm_g, wg_full, p2, wup_full, tgt,
        final_g.reshape(1, d), n_main)

    g_wup = _matmul(p2, dpu, "tn", BF16, "dw_ple_up", 256, 512, t)
    g_wg = _matmul(h2, dgl, "tn", BF16, "dw_ple_gate", 512, 512, t)
    g_wout = _matmul(merged, dx1b, "tn", BF16, "dw_out", 512, 512, t)
    g_wa = _matmul(ya3, dpa, "tn", BF16, "dw_branch_a", 512, 512, t)
    g_wb = _matmul(yb, dpb, "tn", BF16, "dw_branch_b", 512, 512, t)
    dya3 = _matmul(dpa, wa_full, "nt", ACT, "d_ya3", 1024, 1024, d)
    dproj, dmixed, d_scale, d_mixb = _yb_bwd(dproj, dpb, wb_full, proj, mixed, pool_mix_b, pool_scale, off_zp)
    g_mixw = _dmixw(pooled, dmixed, ng)
    dproj = _pool_bwd(dproj, dmixed, mixw_full, nb, s, off_u)

    core = lax.axis_index("c").astype(I32).reshape(1)

    def pair_sums(by_dest, names):
        by_dest = [g.astype(BF16) for g in by_dest]
        got = _pair_exchange(by_dest, "rs_pair_exchange_" + names[0])
        return [_pair_add(g, r, core, BF16, "rs_pair_add_" + nm) for g, r, nm in zip(by_dest, got, names)]

    names_a = ["pool_mix_w", "w_branch_a", "w_branch_b", "w_out", "w_ple_gate", "w_ple_up"]
    partials_a = [
        g_mixw.reshape(ng, N_DEV, pg // N_DEV, pg).transpose(1, 0, 2, 3).reshape(N_DEV, ng * pg // N_DEV, pg)
        .astype(BF16),
        g_wa.reshape(N_DEV, d_inner // N_DEV, d), g_wb.reshape(N_DEV, d_pool // N_DEV, d),
        g_wout.reshape(N_DEV, d // N_DEV, d), g_wg.reshape(N_DEV, d // N_DEV, d),
        g_wup.reshape(ple_dim, N_DEV, d // N_DEV).transpose(1, 0, 2)]
    dproj, dxbc, ddt, d_gn, d_dsk, d_alog, d_dtb, *parts_a = _ssd_bwd(
        dproj, dya3, y, xbc, proj, dtraw, prev, bias_p, alog_p, dskip_full, gnorm_g, nb, s, d_inner, partials_a)
    dproj, g_cw, d_cb = _conv_bwd(dproj, dxbc, conv_pre, proj, cw_full, nb, s, off_xbc)
    g_wmain, g_wdt = _dw_in(h, dproj, ddt)

    def dest_cols(dev):
        lo, hi = dev * chunk, (dev + 1) * chunk
        srcs = [((olo, ohi), g_wmain, off) for (olo, ohi), off in segments] + [((o_dt, o_u), g_wdt, 0)]
        out = []
        for (olo, ohi), arr, off in sorted(srcs, key=lambda e: e[0][0]):
            a, b = max(lo, olo), min(hi, ohi)
            if a < b:
                out.append(arr[:, off + a - olo:off + b - olo])
        return jnp.concatenate(out, axis=1)

    names_b = ["w_in", "conv_w"]
    partials_b = pair_sums([jnp.stack([dest_cols(dev) for dev in range(N_DEV)], axis=0),
                            g_cw.reshape(CONV_W, N_DEV, n_conv // N_DEV).transpose(1, 0, 2)], names_b)
    grad_x, d_ng, *parts_b = _in_bwd(dproj, w_main, ddt, w_dt, x2, norm_g, dx1, partials_b)

    names = names_b + names_a
    parts = parts_b + parts_a
    sh_g, sh_d, sh_m, sh_v = [], [], [], []
    shapes = [w_in.shape, conv_w.shape, pool_mix_w.shape, w_branch_a.shape, w_branch_b.shape, w_out.shape,
              w_ple_gate.shape, w_ple_up.shape]
    for pt, w, m, v, nm, shp in zip(parts, sharded, sharded_m, sharded_v, names, shapes):
        res = _adamw(pt, w, m, v, "adamw_" + nm)
        for dst, r in zip((sh_g, sh_d, sh_m, sh_v), res):
            dst.append(r.reshape(shp))

    rep_w = [norm_g, conv_b, gnorm_g, pool_mix_b, pool_scale, ple_norm_g, final_g.reshape(1, d),
             bias_p, alog_p, _pad_lanes(d_skip)]
    rep_m = [m_norm_g, m_conv_b, m_gnorm_g, m_pool_mix_b, m_pool_scale, m_ple_norm_g, m_final_g.reshape(1, d),
             _pad_lanes(m_dt_bias), _pad_lanes(m_a_log), _pad_lanes(m_d_skip)]
    rep_v = [v_norm_g, v_conv_b, v_gnorm_g, v_pool_mix_b, v_pool_scale, v_ple_norm_g, v_final_g.reshape(1, d),
             _pad_lanes(v_dt_bias), _pad_lanes(v_a_log), _pad_lanes(v_d_skip)]
    rep_g = [d_ng, d_cb, d_gn, d_mixb, d_scale, d_ple_g, dfinal_g, d_dtb, d_alog, d_dsk]
    zero = jnp.zeros((1, LANES), F32)
    small = _all_gather_direct(jnp.concatenate(rep_g + [loss_part], axis=1), "small_all_gather")
    pk = lambda ws: jnp.concatenate(ws + [zero], axis=1)
    rg, rd, rm, rv = _adamw(small, pk(rep_w), pk(rep_m), pk(rep_v), "adamw_replicated")
    loss = rg[0, sum(int(w.size) for w in rep_w)]

    def unpack_rep(buf):
        out, pos_ = [], 0
        for w in rep_w:
            out.append(buf[:, pos_:pos_ + w.size])
            pos_ += w.size
        return out

    def arrange(sh, rep):
        rep = list(rep)
        rep[6] = rep[6].reshape(d)
        for i in (7, 8, 9):
            rep[i] = rep[i][:, :n_heads]
        return [rep[0], sh[0], sh[1], rep[1], rep[7], rep[8], rep[9], rep[2], sh[2], rep[3], rep[4],
                sh[3], sh[4], sh[5], rep[5], sh[6], sh[7], rep[6]]

    outs = [loss, grad_x.reshape(nb, s, d)]
    for sh, rep in ((sh_g, rg), (sh_d, rd), (sh_m, rm), (sh_v, rv)):
        outs += arrange(sh, unpack_rep(rep))
    return tuple(outs)
```

```python
import jax
import jax.numpy as jnp
from jax import lax
from jax.experimental import pallas as pl
from jax.experimental.pallas import tpu as pltpu

F32 = jnp.float32
BF16 = jnp.bfloat16
I32 = jnp.int32
MESH = pl.DeviceIdType.MESH
ANY = pl.BlockSpec(memory_space=pl.ANY)

EPS = 1e-6
CHUNK = 128
HEAD_DIM = 64
D_STATE = 128
GROUPS = 4
CONV_W = 4
POOL_WINDOWS = (2, 4, 8, 16)
ACT = jnp.bfloat16
LANES = 128
SUBLANES = 8
CONV_ROWS = 64
HEAD_PAD = 128
NEG = -1e30
N_DEV = 8

ADAM_LR = 0.001
ADAM_B1 = 0.9
ADAM_B2 = 0.999
ADAM_EPS = 1e-08
ADAM_WD = 0.01
ADAM_STEP = 10

NN = (((1,), (0,)), ((), ()))
NT = (((1,), (1,)), ((), ()))
TN = (((0,), (0,)), ((), ()))


def _pallas(body, **kw):
    return pl.pallas_call(body, **kw)


def _cparams(dims=None, vmem_mb=None):
    kw = {}
    if dims is not None:
        kw["dimension_semantics"] = dims
    if vmem_mb is not None:
        kw["vmem_limit_bytes"] = vmem_mb << 20
    return pltpu.CompilerParams(**kw)


def _dot(a, b, dims=NN):
    return lax.dot_general(a, b, dims, preferred_element_type=F32)


def _split3(v):
    hi = v.astype(BF16)
    r = v - hi.astype(F32)
    mid = r.astype(BF16)
    lo = (r - mid.astype(F32)).astype(BF16)
    return hi, mid, lo


def _exact_r(v, m, dims=NN):
    hi, mid, lo = _split3(v)
    return _dot(hi, m, dims) + _dot(mid, m, dims) + _dot(lo, m, dims)


def _exact_l(m, v, dims=NN):
    hi, mid, lo = _split3(v)
    return _dot(m, hi, dims) + _dot(m, mid, dims) + _dot(m, lo, dims)


def _row_tile(rows, target):
    best = None
    for cand in range(8, min(rows, target) + 1, 8):
        if rows % cand == 0:
            best = cand
    assert best is not None, rows
    return best


def _sum_all(v):
    return jnp.sum(jnp.sum(v, axis=1, keepdims=True), axis=0, keepdims=True)


def _iota(shape, dim):
    return lax.broadcasted_iota(I32, shape, dim)


def _sigmoid(x):
    return 0.5 * jnp.tanh(0.5 * x) + 0.5


def _softplus(x):
    return jnp.maximum(x, 0.0) + jnp.log(1.0 + jnp.exp(-jnp.abs(x)))


def _shift_down(v, j, row):
    return jnp.where(row >= j, pltpu.roll(v, j, 0), 0.0)


def _shift_up(v, j, row):
    n = v.shape[0]
    return jnp.where(row < n - j, pltpu.roll(v, n - j, 0), 0.0)


def _matmul(a, b, mode, out_dtype, name, tm, tn, tk):
    if mode == "nn":
        (m, k), n = a.shape, b.shape[1]
    elif mode == "nt":
        (m, k), n = a.shape, b.shape[0]
    else:
        (k, m), n = a.shape, b.shape[1]
    tm, tn, tk = min(tm, m), min(tn, n), min(tk, k)
    assert m % tm == 0 and n % tn == 0 and k % tk == 0, (name, m, n, k)
    nk = k // tk
    dims = {"nn": NN, "nt": NT, "tn": TN}[mode]

    def body_acc(a_ref, b_ref, o_ref, acc_ref):
        kk = pl.program_id(2)

        @pl.when(kk == 0)
        def _():
            acc_ref[...] = jnp.zeros_like(acc_ref)

        acc_ref[...] += _dot(a_ref[...].astype(BF16), b_ref[...].astype(BF16), dims)

        @pl.when(kk == nk - 1)
        def _():
            o_ref[...] = acc_ref[...].astype(o_ref.dtype)

    def body_one(a_ref, b_ref, o_ref):
        o_ref[...] = _dot(a_ref[...].astype(BF16), b_ref[...].astype(BF16), dims).astype(o_ref.dtype)

    body = body_one if nk == 1 else body_acc
    if mode == "tn":
        a_spec = pl.BlockSpec((tk, tm), lambda i, j, kk: (kk, i))
    else:
        a_spec = pl.BlockSpec((tm, tk), lambda i, j, kk: (i, kk))
    if mode == "nt":
        b_spec = pl.BlockSpec((tn, tk), lambda i, j, kk: (j, kk))
    else:
        b_spec = pl.BlockSpec((tk, tn), lambda i, j, kk: (kk, j))
    return _pallas(
        body, name=name, grid=(m // tm, n // tn, nk),
        in_specs=[a_spec, b_spec],
        out_specs=pl.BlockSpec((tm, tn), lambda i, j, kk: (i, j)),
        out_shape=jax.ShapeDtypeStruct((m, n), out_dtype),
        scratch_shapes=[] if nk == 1 else [pltpu.VMEM((tm, tn), F32)],
        compiler_params=_cparams(("parallel", "parallel", "arbitrary"), 56),
    )(a, b)


def _in_proj(h, w_main, w_dt, blocks):
    t, d = h.shape
    n = w_main.shape[1]
    tm, tn = min(1024, t), 1024
    ni, nj = t // tm, n // tn
    gather = _TwoLevelGather(blocks)

    def body(*refs):
        a_ref, b_ref, wdt_ref = refs[:3]
        x_refs = refs[3:3 + gather.n]
        o_ref, dt_ref = refs[3 + gather.n:5 + gather.n]
        got_refs = refs[5 + gather.n:5 + 2 * gather.n]
        sems = refs[5 + 2 * gather.n:]
        i, j = pl.program_id(0), pl.program_id(1)

        @pl.when((i == 0) & (j == 0))
        def _():
            gather.start(x_refs, got_refs, sems)

        o_ref[...] = _dot(a_ref[...], b_ref[...]).astype(o_ref.dtype)

        @pl.when(j == 0)
        def _():
            dt_ref[...] = _dot(a_ref[...], wdt_ref[...])

        @pl.when((i == ni - 1) & (j == nj - 1))
        def _():
            gather.finish(x_refs, got_refs, sems)

    return _pallas(
        body, name="in_proj", grid=(ni, nj),
        in_specs=[pl.BlockSpec((tm, d), lambda i, j: (i, 0)), pl.BlockSpec((d, tn), lambda i, j: (0, j)),
                  pl.BlockSpec((d, HEAD_PAD), lambda i, j: (0, 0))] + gather.in_specs,
        out_specs=[pl.BlockSpec((tm, tn), lambda i, j: (i, j)), pl.BlockSpec((tm, HEAD_PAD), lambda i, j: (i, 0))]
        + gather.out_specs,
        out_shape=[jax.ShapeDtypeStruct((t, n), ACT), jax.ShapeDtypeStruct((t, HEAD_PAD), F32)] + gather.out_shape,
        scratch_shapes=gather.scratch,
        compiler_params=_cparams(("arbitrary", "arbitrary"), 56),
    )(h, w_main, w_dt, *blocks)


def _dw_in(h, dproj, ddt):
    t, d = h.shape
    n = dproj.shape[1]
    tm, tn = 512, 1024

    def body(a_ref, b_ref, ddt_ref, o_ref, odt_ref):
        a = a_ref[...]
        o_ref[...] = _dot(a, b_ref[...], TN).astype(o_ref.dtype)

        @pl.when(pl.program_id(1) == 0)
        def _():
            odt_ref[...] = _dot(a, ddt_ref[...], TN).astype(odt_ref.dtype)

    return _pallas(
        body, name="dw_in", grid=(d // tm, n // tn),
        in_specs=[pl.BlockSpec((t, tm), lambda i, j: (0, i)), pl.BlockSpec((t, tn), lambda i, j: (0, j)),
                  pl.BlockSpec((t, HEAD_PAD), lambda i, j: (0, 0))],
        out_specs=[pl.BlockSpec((tm, tn), lambda i, j: (i, j)), pl.BlockSpec((tm, HEAD_PAD), lambda i, j: (i, 0))],
        out_shape=[jax.ShapeDtypeStruct((d, n), BF16), jax.ShapeDtypeStruct((d, HEAD_PAD), BF16)],
        compiler_params=_cparams(("parallel", "arbitrary"), 56),
    )(h, dproj, ddt)


def _conv_fwd(proj, conv_w, conv_b, nb, s, off_xbc, n_conv):
    cw = 256
    rc = CONV_ROWS

    def body(x_ref, w_ref, b_ref, o_ref, pre_ref, pad_ref):
        pad_ref[0:SUBLANES, :] = jnp.zeros((SUBLANES, cw), F32)
        pad_ref[SUBLANES:, :] = x_ref[...].astype(F32)
        taps = [w_ref[k:k + 1, :] for k in range(CONV_W)]
        bias = b_ref[...]
        for r in range(s // rc):
            win = pad_ref[pl.ds(r * rc, rc + SUBLANES), :]
            acc = bias + win[SUBLANES:, :] * taps[CONV_W - 1]
            for j in range(1, CONV_W):
                acc = acc + pltpu.roll(win, j, 0)[SUBLANES:, :] * taps[CONV_W - 1 - j]
            pre_ref[pl.ds(r * rc, rc), :] = acc.astype(pre_ref.dtype)
            o_ref[pl.ds(r * rc, rc), :] = (acc * _sigmoid(acc)).astype(o_ref.dtype)

    out = pl.BlockSpec((s, cw), lambda b, j: (b, j))
    return _pallas(
        body, name="conv_fwd", grid=(nb, n_conv // cw),
        in_specs=[pl.BlockSpec((s, cw), lambda b, j: (b, off_xbc // cw + j)),
                  pl.BlockSpec((CONV_W, cw), lambda b, j: (0, j)),
                  pl.BlockSpec((1, cw), lambda b, j: (0, j))],
        out_specs=[out, out],
        out_shape=[jax.ShapeDtypeStruct((nb * s, n_conv), ACT), jax.ShapeDtypeStruct((nb * s, n_conv), ACT)],
        scratch_shapes=[pltpu.VMEM((s + SUBLANES, cw), F32)],
        compiler_params=_cparams(("parallel", "parallel"), 48),
    )(proj, conv_w, conv_b)


def _ssd_common(dtraw, bias, alog, n_heads):
    l = dtraw.shape[0]
    lane = _iota((l, HEAD_PAD), 1)
    lane1 = _iota((1, HEAD_PAD), 1)
    apad = jnp.where(lane1 < n_heads, -jnp.exp(alog), 0.0)
    dt = jnp.where(lane < n_heads, _softplus(dtraw + bias), 0.0)
    a = dt * apad
    tri = (_iota((l, l), 1) <= _iota((l, l), 0)).astype(BF16)
    acs = _exact_l(tri, a)
    width = n_heads * HEAD_DIM
    ex = (_iota((HEAD_PAD, width), 1) // HEAD_DIM == _iota((HEAD_PAD, width), 0)).astype(BF16)
    dt_full = _exact_r(dt, ex)
    acs_full = _exact_r(acs, ex)
    return dt, apad, acs, dt_full, acs_full, tri, ex


def _decay_mats(acs, acs_t, h, mask):
    col = acs[:, h:h + 1]
    row = acs_t[h:h + 1, :]
    return jnp.exp(jnp.where(mask, col - row, NEG))


def _ssd_fwd(xbc, proj, dtraw, dt_bias, a_log, dskip_full, gnorm, nb, s, d_inner):
    l = CHUNK
    nc = s // l
    n_heads = d_inner // HEAD_DIM
    gw = d_inner // GROUPS
    pairs_per_group = gw // LANES
    gn_cols = GROUPS * D_STATE

    def body(xs_ref, bm_ref, cm_ref, z_ref, dtr_ref, bias_ref, alog_ref, dsk_ref, gn_ref,
             y_ref, ya3_ref, prev_ref, state_ref):
        @pl.when(pl.program_id(1) == 0)
        def _():
            state_ref[...] = jnp.zeros_like(state_ref)

        xs = xs_ref[...].astype(F32)
        dt, apad, acs, dt_full, acs_full, tri, ex = _ssd_common(
            dtr_ref[...], bias_ref[...], alog_ref[...], n_heads)
        acs_t = acs.T
        e_full = jnp.exp(acs_full)
        last = acs_full[l - 1:l, :]
        dte = jnp.exp(last - acs_full)
        cd = jnp.exp(last)
        xdt = xs * dt_full
        xd = xdt * dte
        mask = _iota((l, l), 1) <= _iota((l, l), 0)
        lane = _iota((l, LANES), 1)
        prev_ref[0] = state_ref[...]
        for g in range(GROUPS):
            gsl = slice(g * gw, (g + 1) * gw)
            bg = bm_ref[:, g * D_STATE:(g + 1) * D_STATE].astype(BF16)
            cg = cm_ref[:, g * D_STATE:(g + 1) * D_STATE].astype(BF16)
            cb = _dot(cg, bg, NT)
            outs = []
            for q in range(pairs_per_group):
                pair = g * pairs_per_group + q
                h0, h1 = 2 * pair, 2 * pair + 1
                xpb = xdt[:, pair * LANES:(pair + 1) * LANES].astype(BF16)
                m0 = (cb * _decay_mats(acs, acs_t, h0, mask)).astype(BF16)
                m1 = (cb * _decay_mats(acs, acs_t, h1, mask)).astype(BF16)
                zero = jnp.zeros_like(xpb)
                rhs = jnp.concatenate([jnp.where(lane < HEAD_DIM, xpb, zero),
                                       jnp.where(lane >= HEAD_DIM, xpb, zero)], axis=0)
                outs.append(_dot(jnp.concatenate([m0, m1], axis=1), rhs))
            prev_g = state_ref[:, gsl]
            y_off = _dot(cg, prev_g.astype(BF16)) * e_full[:, gsl]
            y_g = jnp.concatenate(outs, axis=1) + y_off
            y_ref[:, gsl] = y_g.astype(y_ref.dtype)
            st = _dot(bg, xd[:, gsl].astype(BF16), TN)
            state_ref[:, gsl] = prev_g * cd[:, gsl] + st
            zv = z_ref[:, gsl].astype(F32)
            blk = (y_g + xs[:, gsl] * dsk_ref[:, gsl]) * (zv * _sigmoid(zv))
            rg = lax.rsqrt(jnp.mean(blk * blk, axis=-1, keepdims=True) + EPS)
            ya3_ref[:, gsl] = (blk * rg * gn_ref[:, gsl]).astype(ya3_ref.dtype)

    row = lambda b, c: b * nc + c
    small = lambda n: pl.BlockSpec((1, n), lambda b, c: (0, 0))
    return _pallas(
        body, name="ssd_fwd", grid=(nb, nc),
        in_specs=[pl.BlockSpec((l, d_inner), lambda b, c: (row(b, c), 0)),
                  pl.BlockSpec((l, gn_cols), lambda b, c: (row(b, c), d_inner // gn_cols)),
                  pl.BlockSpec((l, gn_cols), lambda b, c: (row(b, c), d_inner // gn_cols + 1)),
                  pl.BlockSpec((l, d_inner), lambda b, c: (row(b, c), 0)),
                  pl.BlockSpec((l, HEAD_PAD), lambda b, c: (row(b, c), 0)),
                  small(HEAD_PAD), small(HEAD_PAD), small(d_inner), small(d_inner)],
        out_specs=[pl.BlockSpec((l, d_inner), lambda b, c: (row(b, c), 0)),
                   pl.BlockSpec((l, d_inner), lambda b, c: (row(b, c), 0)),
                   pl.BlockSpec((1, D_STATE, d_inner), lambda b, c: (row(b, c), 0, 0))],
        out_shape=[jax.ShapeDtypeStruct((nb * s, d_inner), ACT),
                   jax.ShapeDtypeStruct((nb * s, d_inner), BF16),
                   jax.ShapeDtypeStruct((nb * nc, D_STATE, d_inner), F32)],
        scratch_shapes=[pltpu.VMEM((D_STATE, d_inner), F32)],
        compiler_params=_cparams(("parallel", "arbitrary"), 48),
    )(xbc, xbc, xbc, proj, dtraw, dt_bias, a_log, dskip_full, gnorm)


def _pool_fwd(proj, mixw, mix_b, scale, nb, s, off_u, off_zp, d_pool):
    pg = d_pool // len(POOL_WINDOWS)

    def body(u_ref, zp_ref, w_ref, b_ref, s_ref, pooled_ref, mixed_ref, yb_ref):
        g = pl.program_id(1)
        uv = u_ref[...].astype(F32)
        row = _iota(uv.shape, 0)
        pos = (_iota((uv.shape[0], 1), 0) + 1).astype(F32)
        for gi, w in enumerate(POOL_WINDOWS):
            @pl.when(g == gi)
            def _():
                acc, span = uv, 1
                while span < w:
                    acc = acc + _shift_down(acc, span, row)
                    span *= 2
                pooled_ref[...] = (acc / jnp.minimum(pos, float(w)) - uv).astype(pooled_ref.dtype)

        mixed = _dot(pooled_ref[...], w_ref[0])
        mixed_ref[...] = mixed.astype(mixed_ref.dtype)
        zp = zp_ref[...].astype(F32)
        yb_ref[...] = ((mixed + b_ref[...]) * s_ref[...] * (zp * _sigmoid(zp))).astype(yb_ref.dtype)

    seg = lambda off: pl.BlockSpec((s, pg), lambda b, g: (b, off // pg + g))
    vec = pl.BlockSpec((1, pg), lambda b, g: (0, g))
    return _pallas(
        body, name="pool_fwd", grid=(nb, len(POOL_WINDOWS)),
        in_specs=[seg(off_u), seg(off_zp), pl.BlockSpec((1, pg, pg), lambda b, g: (g, 0, 0)), vec, vec],
        out_specs=[seg(0), seg(0), seg(0)],
        out_shape=[jax.ShapeDtypeStruct((nb * s, d_pool), BF16), jax.ShapeDtypeStruct((nb * s, d_pool), ACT),
                   jax.ShapeDtypeStruct((nb * s, d_pool), BF16)],
        compiler_params=_cparams(("parallel", "parallel"), 48),
    )(proj, proj, mixw, mix_b, scale)


def _merge_to_grads(ya3, wa, yb, wb, proj, off_ga, off_gb, wout, x, ple_g, wg, p, wup, target, final_g, n_main):
    t, d = x.shape
    pd = p.shape[1]
    tm = 256
    assert off_gb == off_ga + d and off_ga % (2 * d) == 0

    def body(a_ref, wa_ref, b_ref, wb_ref, ga_ref, gb_ref, wout_ref, x_ref, pg_ref, wg_ref, p_ref, wup_ref,
             tg_ref, fg_ref,
             mg_ref, h2_ref, dx1_ref, dx1b_ref, dgl_ref, dpu_ref, dpa_ref, dpb_ref, dgate_ref,
             loss_ref, dfg_ref, dpg_ref):
        @pl.when(pl.program_id(0) == 0)
        def _():
            loss_ref[...] = jnp.zeros_like(loss_ref)
            dfg_ref[...] = jnp.zeros_like(dfg_ref)
            dpg_ref[...] = jnp.zeros_like(dpg_ref)

        pa = _dot(a_ref[...], wa_ref[...])
        pb = _dot(b_ref[...], wb_ref[...])
        sa = _sigmoid(ga_ref[...].astype(F32))
        sb = _sigmoid(gb_ref[...].astype(F32))
        merged = (sa * pa + sb * pb).astype(BF16)
        mg_ref[...] = merged
        x1 = x_ref[...] + _dot(merged, wout_ref[...])
        r1 = lax.rsqrt(jnp.mean(x1 * x1, axis=-1, keepdims=True) + EPS)
        xh1 = x1 * r1
        pg = pg_ref[...]
        h2 = (xh1 * pg).astype(BF16)
        h2_ref[...] = h2
        gate = _sigmoid(_dot(h2, wg_ref[...]))
        pu = _dot(p_ref[...].astype(BF16), wup_ref[...])
        x2 = x1 + gate * pu
        r2 = lax.rsqrt(jnp.mean(x2 * x2, axis=-1, keepdims=True) + EPS)
        xh2 = x2 * r2
        gfin = fg_ref[...]
        err = xh2 * gfin - tg_ref[...]
        loss_ref[...] += _sum_all(err * err) * (0.5 / d)
        dy = err * (1.0 / d)
        dfg_ref[...] += jnp.sum(dy * xh2, axis=0, keepdims=True)
        dg = dy * gfin
        dx2 = r2 * (dg - xh2 * jnp.mean(dg * xh2, axis=-1, keepdims=True))
        dgl = (dx2 * pu * gate * (1.0 - gate)).astype(BF16)
        dgl_ref[...] = dgl
        dpu_ref[...] = (dx2 * gate).astype(dpu_ref.dtype)
        dh2 = _dot(dgl, wg_ref[...], NT)
        dpg_ref[...] += jnp.sum(dh2 * xh1, axis=0, keepdims=True)
        dg = dh2 * pg
        dx1 = dx2 + r1 * (dg - xh1 * jnp.mean(dg * xh1, axis=-1, keepdims=True))
        dx1_ref[...] = dx1
        dx1b = dx1.astype(BF16)
        dx1b_ref[...] = dx1b
        dm = _dot(dx1b, wout_ref[...], NT)
        dpa_ref[...] = (dm * sa).astype(dpa_ref.dtype)
        dpb_ref[...] = (dm * sb).astype(dpb_ref.dtype)
        dgate_ref[:, :d] = (dm * pa * sa * (1.0 - sa)).astype(dgate_ref.dtype)
        dgate_ref[:, d:] = (dm * pb * sb * (1.0 - sb)).astype(dgate_ref.dtype)

    rows = lambda n: pl.BlockSpec((tm, n), lambda i: (i, 0))
    resident = lambda a: pl.BlockSpec(a.shape, lambda i: (0, 0), pipeline_mode=pl.Buffered(1))
    vec = lambda n: pl.BlockSpec((1, n), lambda i: (0, 0))
    bf = lambda: jax.ShapeDtypeStruct((t, d), BF16)
    return _pallas(
        body, name="merge_to_grads", grid=(t // tm,),
        in_specs=[rows(ya3.shape[1]), resident(wa), rows(yb.shape[1]), resident(wb),
                  pl.BlockSpec((tm, d), lambda i: (i, off_ga // d)), pl.BlockSpec((tm, d), lambda i: (i, off_gb // d)),
                  resident(wout), rows(d), vec(d), resident(wg), rows(pd), resident(wup), rows(d), vec(d)],
        out_specs=[rows(d)] * 8 + [pl.BlockSpec((tm, 2 * d), lambda i: (i, off_ga // (2 * d))),
                                   vec(LANES), vec(d), vec(d)],
        out_shape=[bf(), bf(), jax.ShapeDtypeStruct((t, d), F32), bf(), bf(), bf(), bf(), bf(),
                   jax.ShapeDtypeStruct((t, n_main), BF16), jax.ShapeDtypeStruct((1, LANES), F32),
                   jax.ShapeDtypeStruct((1, d), F32), jax.ShapeDtypeStruct((1, d), F32)],
        compiler_params=_cparams(("arbitrary",), 60),
    )(ya3, wa, yb, wb, proj, proj, wout, x, ple_g, wg, p, wup, target, final_g)


def _yb_bwd(dproj, dpb, wb, proj, mixed, mix_b, scale, off_zp):
    t, dp = mixed.shape
    tm = 512

    def body(_, dpb_ref, w_ref, zp_ref, mx_ref, b_ref, s_ref, dzp_ref, dmx_ref, dsc_ref, db_ref):
        @pl.when(pl.program_id(0) == 0)
        def _():
            dsc_ref[...] = jnp.zeros_like(dsc_ref)
            db_ref[...] = jnp.zeros_like(db_ref)

        dyb = _dot(dpb_ref[...], w_ref[...], NT)
        zp = zp_ref[...].astype(F32)
        sg = _sigmoid(zp)
        mb = mx_ref[...].astype(F32) + b_ref[...]
        yb0 = mb * s_ref[...]
        dyb0 = dyb * (zp * sg)
        dzp_ref[...] = (dyb * yb0 * (sg * (1.0 + zp * (1.0 - sg)))).astype(dzp_ref.dtype)
        dsc_ref[...] += jnp.sum(dyb0 * mb, axis=0, keepdims=True)
        dmx = dyb0 * s_ref[...]
        db_ref[...] += jnp.sum(dmx, axis=0, keepdims=True)
        dmx_ref[...] = dmx.astype(dmx_ref.dtype)

    rowblk = pl.BlockSpec((tm, dp), lambda i: (i, 0))
    vec = pl.BlockSpec((1, dp), lambda i: (0, 0))
    seg = pl.BlockSpec((tm, dp), lambda i: (i, off_zp // dp))
    return _pallas(
        body, name="yb_bwd", grid=(t // tm,),
        in_specs=[ANY, pl.BlockSpec((tm, wb.shape[1]), lambda i: (i, 0)),
                  pl.BlockSpec(wb.shape, lambda i: (0, 0)), seg, rowblk, vec, vec],
        out_specs=[seg, rowblk, vec, vec],
        out_shape=[jax.ShapeDtypeStruct(dproj.shape, dproj.dtype), jax.ShapeDtypeStruct((t, dp), BF16),
                   jax.ShapeDtypeStruct((1, dp), F32), jax.ShapeDtypeStruct((1, dp), F32)],
        input_output_aliases={0: 0},
        compiler_params=_cparams(("arbitrary",), 56),
    )(dproj, dpb, wb, proj, mixed, mix_b, scale)


def _dmixw(pooled, dmixed, ng):
    t, dp = pooled.shape
    pg = dp // ng
    tk = min(2048, t)
    nk = t // tk

    def body(a_ref, b_ref, o_ref):
        @pl.when(pl.program_id(1) == 0)
        def _():
            o_ref[...] = jnp.zeros_like(o_ref)

        o_ref[0] += _dot(a_ref[...], b_ref[...], TN)

    blk = pl.BlockSpec((tk, pg), lambda g, k: (k, g))
    return _pallas(
        body, name="dmixw", grid=(ng, nk),
        in_specs=[blk, blk],
        out_specs=pl.BlockSpec((1, pg, pg), lambda g, k: (g, 0, 0)),
        out_shape=jax.ShapeDtypeStruct((ng, pg, pg), F32),
        compiler_params=_cparams(("parallel", "arbitrary")),
    )(pooled, dmixed)


def _pool_bwd(dproj, dmixed, mixw, nb, s, off_u):
    dp = dmixed.shape[1]
    pg = dp // len(POOL_WINDOWS)

    def body(_, d_ref, w_ref, o_ref):
        g = pl.program_id(1)
        dv = _dot(d_ref[...], w_ref[0], NT)
        row = _iota(dv.shape, 0)
        pos = (_iota((dv.shape[0], 1), 0) + 1).astype(F32)
        for gi, w in enumerate(POOL_WINDOWS):
            @pl.when(g == gi)
            def _():
                acc, span = dv / jnp.minimum(pos, float(w)), 1
                while span < w:
                    acc = acc + _shift_up(acc, span, row)
                    span *= 2
                o_ref[...] = (acc - dv).astype(o_ref.dtype)

    return _pallas(
        body, name="pool_bwd", grid=(nb, len(POOL_WINDOWS)),
        in_specs=[ANY, pl.BlockSpec((s, pg), lambda b, g: (b, g)), pl.BlockSpec((1, pg, pg), lambda b, g: (g, 0, 0))],
        out_specs=pl.BlockSpec((s, pg), lambda b, g: (b, off_u // pg + g)),
        out_shape=jax.ShapeDtypeStruct(dproj.shape, dproj.dtype),
        input_output_aliases={0: 0},
        compiler_params=_cparams(("parallel", "parallel"), 48),
    )(dproj, dmixed, mixw)


def _ssd_bwd(dproj, dya3, y, xbc, proj, dtraw, prev, dt_bias, a_log, dskip_full, gnorm, nb, s, d_inner, partials):
    l = CHUNK
    nc = s // l
    n_heads = d_inner // HEAD_DIM
    gw = d_inner // GROUPS
    pairs_per_group = gw // LANES
    gn_cols = GROUPS * D_STATE
    n_conv = d_inner + 2 * gn_cols
    last_step = nb * nc - 1

    xch = _DirectExchange(partials)
    n_in, n_out = 13, 7

    def body(*refs):
        (_, dya3_ref, y_ref, xs_ref, bm_ref, cm_ref, z_ref, dtr_ref, prev_ref, bias_ref, alog_ref,
         dsk_ref, gn_ref) = refs[:n_in]
        p_refs = refs[n_in:n_in + xch.n]
        outs = refs[n_in + xch.n:]
        dz_ref, dxbc_ref, ddt_ref, dgn_ref, dskh_ref, dalog_ref, dbias_ref = outs[:n_out]
        got_refs = outs[n_out:n_out + xch.n]
        g_ref, dskacc_ref = outs[n_out + xch.n:n_out + xch.n + 2]
        sems = outs[n_out + xch.n + 2:]
        step = pl.program_id(0) * nc + pl.program_id(1)

        @pl.when(pl.program_id(1) == 0)
        def _():
            g_ref[...] = jnp.zeros_like(g_ref)

        @pl.when(step == 0)
        def _():
            xch.start(p_refs, got_refs, sems)
            dgn_ref[...] = jnp.zeros_like(dgn_ref)
            dalog_ref[...] = jnp.zeros_like(dalog_ref)
            dbias_ref[...] = jnp.zeros_like(dbias_ref)
            dskacc_ref[...] = jnp.zeros_like(dskacc_ref)

        xs = xs_ref[...].astype(F32)
        zv = z_ref[...].astype(F32)
        dsk = dsk_ref[...]
        sg = _sigmoid(zv)
        sz = zv * sg
        ya1 = y_ref[...].astype(F32) + xs * dsk
        ya2 = ya1 * sz
        parts = []
        for g in range(GROUPS):
            gsl = slice(g * gw, (g + 1) * gw)
            blk = ya2[:, gsl]
            rg = lax.rsqrt(jnp.mean(blk * blk, axis=-1, keepdims=True) + EPS)
            xh = blk * rg
            d3 = dya3_ref[:, gsl].astype(F32)
            dgn_ref[:, gsl] += jnp.sum(d3 * xh, axis=0, keepdims=True)
            dg = d3 * gn_ref[:, gsl]
            parts.append(rg * (dg - xh * jnp.mean(dg * xh, axis=-1, keepdims=True)))
        dya2 = jnp.concatenate(parts, axis=1)
        dy = dya2 * sz
        dz_ref[...] = (dya2 * ya1 * (sg * (1.0 + zv * (1.0 - sg)))).astype(dz_ref.dtype)
        dskacc_ref[...] += jnp.sum(dy * xs, axis=0, keepdims=True)

        dtraw_v = dtr_ref[...]
        bias = bias_ref[...]
        dt, apad, acs, dt_full, acs_full, tri, ex = _ssd_common(dtraw_v, bias, alog_ref[...], n_heads)
        acs_t = acs.T
        e_full = jnp.exp(acs_full)
        last = acs_full[l - 1:l, :]
        dte = jnp.exp(last - acs_full)
        cd = jnp.exp(last)
        xdt = xs * dt_full
        xd = xdt * dte
        dy_e = dy * e_full
        mask = _iota((l, l), 1) <= _iota((l, l), 0)
        lane = _iota((l, LANES), 1)
        hlane = _iota((l, HEAD_PAD), 1)
        hsub = _iota((HEAD_PAD, l), 0)
        dacs = jnp.zeros((l, HEAD_PAD), F32)
        dacs_t = jnp.zeros((HEAD_PAD, l), F32)
        dacs_full, dlast, dxdt_all, db_all, dc_all = [], [], [], [], []
        for g in range(GROUPS):
            gsl = slice(g * gw, (g + 1) * gw)
            bg = bm_ref[:, g * D_STATE:(g + 1) * D_STATE].astype(BF16)
            cg = cm_ref[:, g * D_STATE:(g + 1) * D_STATE].astype(BF16)
            prev_g = prev_ref[0, :, gsl]
            prev_b = prev_g.astype(BF16)
            gst = g_ref[:, gsl]
            gst_b = gst.astype(BF16)
            dye_b = dy_e[:, gsl].astype(BF16)
            dc = _dot(dye_b, prev_b, NT)
            dprev = _dot(cg, dye_b, TN)
            y_off = _dot(cg, prev_b) * e_full[:, gsl]
            xd_g = xd[:, gsl]
            db = _dot(xd_g.astype(BF16), gst_b, NT)
            dxd = _dot(bg, gst_b)
            wgt = dxd * xd_g
            dacs_full.append(dy[:, gsl] * y_off - wgt)
            dlast.append(jnp.sum(wgt, axis=0, keepdims=True)
                         + jnp.sum(prev_g * gst, axis=0, keepdims=True) * cd[:, gsl])
            g_ref[:, gsl] = dprev + cd[:, gsl] * gst
            dxdt_g = dxd * dte[:, gsl]
            cb = _dot(cg, bg, NT)
            dcb = jnp.zeros((l, l), F32)
            for q in range(pairs_per_group):
                pair = g * pairs_per_group + q
                psl = slice(pair * LANES, (pair + 1) * LANES)
                xpb = xdt[:, psl].astype(BF16)
                dyp = dy[:, psl].astype(BF16)
                zero = jnp.zeros_like(dyp)
                halves = (jnp.where(lane < HEAD_DIM, dyp, zero), jnp.where(lane >= HEAD_DIM, dyp, zero))
                acc = dxdt_g[:, q * LANES:(q + 1) * LANES]
                for k, dyh in enumerate(halves):
                    h = 2 * pair + k
                    dk = _decay_mats(acs, acs_t, h, mask)
                    mm = cb * dk
                    dm = _dot(dyh, xpb, NT)
                    dcb = dcb + dm * dk
                    dseg = dm * mm
                    dacs = dacs + jnp.where(hlane == h, jnp.sum(dseg, axis=1, keepdims=True), 0.0)
                    dacs_t = dacs_t + jnp.where(hsub == h, jnp.sum(dseg, axis=0, keepdims=True), 0.0)
                    acc = acc + _dot(mm.astype(BF16), dyh, TN)
                dxdt_all.append(acc)
            dcb_b = dcb.astype(BF16)
            dc_all.append(dc + _dot(dcb_b, bg))
            db_all.append(db + _dot(dcb_b, cg, TN))

        dxdt = jnp.concatenate(dxdt_all, axis=1)
        dxbc_ref[:, :d_inner] = (dy * dsk + dxdt * dt_full).astype(dxbc_ref.dtype)
        dxbc_ref[:, d_inner:d_inner + gn_cols] = jnp.concatenate(db_all, axis=1).astype(dxbc_ref.dtype)
        dxbc_ref[:, d_inner + gn_cols:] = jnp.concatenate(dc_all, axis=1).astype(dxbc_ref.dtype)
        ddt = _exact_r(dxdt * xs, ex, NT)
        rowi = _iota((l, d_inner), 0)
        dacs_f = jnp.concatenate(dacs_full, axis=1) + jnp.where(rowi == l - 1, jnp.concatenate(dlast, axis=1), 0.0)
        dacs = dacs + _exact_r(dacs_f, ex, NT) - dacs_t.T
        da = _exact_l(tri, dacs, TN)
        ddt = ddt + da * apad
        dalog_ref[...] += jnp.sum(da * dt, axis=0, keepdims=True) * apad
        ddt_raw = ddt * _sigmoid(dtraw_v + bias)
        dbias_ref[...] += jnp.sum(ddt_raw, axis=0, keepdims=True)
        ddt_ref[...] = ddt_raw.astype(ddt_ref.dtype)

        @pl.when(step == last_step)
        def _():
            dskh_ref[...] = _exact_r(dskacc_ref[...], ex, NT)[0:1, :]
            xch.wait(p_refs, got_refs, sems)

    row = lambda b, c: b * nc + (nc - 1 - c)
    small = lambda n: pl.BlockSpec((1, n), lambda b, c: (0, 0))
    wide = pl.BlockSpec((l, d_inner), lambda b, c: (row(b, c), 0))
    return _pallas(
        body, name="ssd_bwd", grid=(nb, nc),
        in_specs=[ANY, wide, wide, wide,
                  pl.BlockSpec((l, gn_cols), lambda b, c: (row(b, c), d_inner // gn_cols)),
                  pl.BlockSpec((l, gn_cols), lambda b, c: (row(b, c), d_inner // gn_cols + 1)),
                  wide, pl.BlockSpec((l, HEAD_PAD), lambda b, c: (row(b, c), 0)),
                  pl.BlockSpec((1, D_STATE, d_inner), lambda b, c: (row(b, c), 0, 0)),
                  small(HEAD_PAD), small(HEAD_PAD), small(d_inner), small(d_inner)] + xch.in_specs,
        out_specs=[wide, pl.BlockSpec((l, n_conv), lambda b, c: (row(b, c), 0)),
                   pl.BlockSpec((l, HEAD_PAD), lambda b, c: (row(b, c), 0)),
                   small(d_inner), small(HEAD_PAD), small(HEAD_PAD), small(HEAD_PAD)] + xch.out_specs,
        out_shape=[jax.ShapeDtypeStruct(dproj.shape, dproj.dtype),
                   jax.ShapeDtypeStruct((nb * s, n_conv), ACT),
                   jax.ShapeDtypeStruct((nb * s, HEAD_PAD), BF16),
                   jax.ShapeDtypeStruct((1, d_inner), F32), jax.ShapeDtypeStruct((1, HEAD_PAD), F32),
                   jax.ShapeDtypeStruct((1, HEAD_PAD), F32), jax.ShapeDtypeStruct((1, HEAD_PAD), F32)]
        + xch.out_shape,
        scratch_shapes=[pltpu.VMEM((D_STATE, d_inner), F32), pltpu.VMEM((8, d_inner), F32)] + xch.scratch,
        input_output_aliases={0: 0},
        compiler_params=_cparams(("arbitrary", "arbitrary"), 56),
    )(dproj, dya3, y, xbc, xbc, xbc, proj, dtraw, prev, dt_bias, a_log, dskip_full, gnorm, *partials)


def _conv_bwd(dproj, dxbc, conv_pre, proj, conv_w, nb, s, off_xbc):
    n_conv = dxbc.shape[1]
    cw = 256

    rc = CONV_ROWS

    def fold(v):
        return jnp.sum(v.reshape(rc // SUBLANES, SUBLANES, cw), axis=0)

    def body(_, d_ref, pre_ref, x_ref, w_ref, dx_ref, dw_ref, db_ref, dcpad_ref):
        @pl.when(pl.program_id(1) == 0)
        def _():
            dw_ref[...] = jnp.zeros_like(dw_ref)
            db_ref[...] = jnp.zeros_like(db_ref)

        dcpad_ref[pl.ds(s, SUBLANES), :] = jnp.zeros((SUBLANES, cw), F32)
        taps = [w_ref[k:k + 1, :] for k in range(CONV_W)]
        db = jnp.zeros((SUBLANES, cw), F32)
        for r in range(s // rc):
            rows = pl.ds(r * rc, rc)
            acc = pre_ref[rows, :].astype(F32)
            sg = _sigmoid(acc)
            dc = d_ref[rows, :].astype(F32) * (sg * (1.0 + acc * (1.0 - sg)))
            dcpad_ref[rows, :] = dc
            db = db + fold(dc)
        db_ref[...] += jnp.sum(db, axis=0, keepdims=True)
        dw = [jnp.zeros((SUBLANES, cw), F32) for _ in range(CONV_W)]
        for r in range(s // rc):
            xv = x_ref[pl.ds(r * rc, rc), :].astype(F32)
            win = dcpad_ref[pl.ds(r * rc, rc + SUBLANES), :]
            dxv = jnp.zeros((rc, cw), F32)
            for j in range(CONV_W):
                up = win[:rc, :] if j == 0 else pltpu.roll(win, rc + SUBLANES - j, 0)[:rc, :]
                dxv = dxv + up * taps[CONV_W - 1 - j]
                dw[CONV_W - 1 - j] = dw[CONV_W - 1 - j] + fold(up * xv)
            dx_ref[pl.ds(r * rc, rc), :] = dxv.astype(dx_ref.dtype)
        for k in range(CONV_W):
            dw_ref[k:k + 1, :] += jnp.sum(dw[k], axis=0, keepdims=True)

    blk = pl.BlockSpec((s, cw), lambda j, b: (b, j))
    return _pallas(
        body, name="conv_bwd", grid=(n_conv // cw, nb),
        in_specs=[ANY, blk, blk, pl.BlockSpec((s, cw), lambda j, b: (b, off_xbc // cw + j)),
                  pl.BlockSpec((CONV_W, cw), lambda j, b: (0, j))],
        out_specs=[pl.BlockSpec((s, cw), lambda j, b: (b, off_xbc // cw + j)),
                   pl.BlockSpec((CONV_W, cw), lambda j, b: (0, j)), pl.BlockSpec((1, cw), lambda j, b: (0, j))],
        out_shape=[jax.ShapeDtypeStruct(dproj.shape, dproj.dtype),
                   jax.ShapeDtypeStruct((CONV_W, n_conv), F32), jax.ShapeDtypeStruct((1, n_conv), F32)],
        scratch_shapes=[pltpu.VMEM((s + SUBLANES, cw), F32)],
        input_output_aliases={0: 0},
        compiler_params=_cparams(("parallel", "arbitrary"), 48),
    )(dproj, dxbc, conv_pre, proj, conv_w)


def _in_bwd(dproj, w_main, ddt, w_dt, x, norm_g, dx1, partials):
    t, d = x.shape
    n_main = dproj.shape[1]
    tm = 256
    ni = t // tm
    xch = _ChipExchange(partials)
    n_in, n_out = 7, 2

    def body(*refs):
        dp_ref, w_ref, ddt_ref, wdt_ref, x_ref, g_ref, dx1_ref = refs[:n_in]
        p_refs = refs[n_in:n_in + xch.n]
        outs = refs[n_in + xch.n:]
        dx_ref, dg_ref = outs[:n_out]
        got_refs = outs[n_out:n_out + xch.n]
        sems = outs[n_out + xch.n:]
        i = pl.program_id(0)

        @pl.when(i == 0)
        def _():
            xch.start(p_refs, got_refs, sems)
            dg_ref[...] = jnp.zeros_like(dg_ref)

        dh = _dot(dp_ref[...], w_ref[...], NT) + _dot(ddt_ref[...], wdt_ref[...], NT)
        xv = x_ref[...]
        r = lax.rsqrt(jnp.mean(xv * xv, axis=-1, keepdims=True) + EPS)
        xh = xv * r
        dg_ref[...] += jnp.sum(dh * xh, axis=0, keepdims=True)
        dg = dh * g_ref[...]
        dx_ref[...] = dx1_ref[...] + r * (dg - xh * jnp.mean(dg * xh, axis=-1, keepdims=True))

        @pl.when(i == ni - 1)
        def _():
            xch.wait(p_refs, got_refs, sems)

    rowblk = pl.BlockSpec((tm, d), lambda i: (i, 0))
    vec = pl.BlockSpec((1, d), lambda i: (0, 0))
    resident = lambda shape: pl.BlockSpec(shape, lambda i: (0, 0), pipeline_mode=pl.Buffered(1))
    return _pallas(
        body, name="in_bwd", grid=(ni,),
        in_specs=[pl.BlockSpec((tm, n_main), lambda i: (i, 0)), resident((d, n_main)),
                  pl.BlockSpec((tm, HEAD_PAD), lambda i: (i, 0)), resident((d, HEAD_PAD)),
                  rowblk, vec, rowblk] + xch.in_specs,
        out_specs=[rowblk, vec] + xch.out_specs,
        out_shape=[jax.ShapeDtypeStruct((t, d), F32), jax.ShapeDtypeStruct((1, d), F32)] + xch.out_shape,
        scratch_shapes=xch.scratch,
        compiler_params=_cparams(("arbitrary",), 56),
    )(dproj, w_main, ddt, w_dt, x, norm_g, dx1, *partials)


def _adamw_math(w, g, m, v):
    m = ADAM_B1 * m + (1.0 - ADAM_B1) * g
    v = ADAM_B2 * v + (1.0 - ADAM_B2) * (g * g)
    m_hat = m / (1.0 - ADAM_B1 ** ADAM_STEP)
    v_hat = v / (1.0 - ADAM_B2 ** ADAM_STEP)
    delta = -ADAM_LR * (m_hat / (jnp.sqrt(v_hat) + ADAM_EPS) + ADAM_WD * w)
    return delta, m, v


def _adamw(parts, w, m, v, name):
    n, rows, cols = parts.shape
    tr = _row_tile(rows, 128) if rows % 8 == 0 else rows

    def body(p_ref, w_ref, m_ref, v_ref, g_ref, d_ref, mo_ref, vo_ref):
        g = p_ref[0].astype(F32)
        for k in range(1, n):
            g = g + p_ref[k].astype(F32)
        d, mn, vn = _adamw_math(w_ref[...], g, m_ref[...], v_ref[...])
        g_ref[...] = g
        d_ref[...] = d
        mo_ref[...] = mn
        vo_ref[...] = vn

    blk = pl.BlockSpec((tr, cols), lambda i: (i, 0))
    shp = jax.ShapeDtypeStruct((rows, cols), F32)
    return _pallas(
        body, name=name, grid=(rows // tr,),
        in_specs=[pl.BlockSpec((n, tr, cols), lambda i: (0, i, 0)), blk, blk, blk],
        out_specs=[blk, blk, blk, blk], out_shape=[shp, shp, shp, shp],
        compiler_params=_cparams(("parallel",), 56),
    )(parts, w, m, v)


def _coords():
    return lax.axis_index("x"), lax.axis_index("y"), lax.axis_index("c")


class _TwoLevelGather:
    def __init__(self, blocks):
        self.n = len(blocks)
        self.in_specs = [ANY] * self.n
        self.out_specs = [ANY] * self.n
        self.out_shape = [jax.ShapeDtypeStruct((N_DEV,) + b.shape, b.dtype) for b in blocks]
        self.scratch = [pltpu.SemaphoreType.DMA((self.n, 7)), pltpu.SemaphoreType.DMA((self.n, 7)),
                        pltpu.SemaphoreType.DMA((self.n,))]

    def _parts(self, x_refs, out_refs, sems):
        send_sems, recv_sems, local_sems = sems
        x, y, c = _coords()
        me, sibling = (x, y, c), (x, y, 1 - c)
        chips = [(1 - x, y), (x, 1 - y), (1 - x, 1 - y)]

        def slot(a, px, py, pc):
            return out_refs[a].at[4 * px + 2 * py + pc]

        def copy(a, k, block, to, src=None):
            return pltpu.make_async_remote_copy(
                src_ref=slot(a, *block) if src is None else src, dst_ref=slot(a, *block),
                send_sem=send_sems.at[a, k], recv_sem=recv_sems.at[a, k], device_id=to, device_id_type=MESH)

        local = [pltpu.make_async_copy(x_refs[a], slot(a, *me), local_sems.at[a]) for a in range(self.n)]
        first = []
        for a in range(self.n):
            first.append(copy(a, 0, me, sibling, src=x_refs[a]))
            first += [copy(a, 1 + j, me, (*chip, c), src=x_refs[a]) for j, chip in enumerate(chips)]
        return copy, local, first, chips, me, sibling, c

    def start(self, x_refs, out_refs, sems):
        _, local, first, *_ = self._parts(x_refs, out_refs, sems)
        for cp in local + first:
            cp.start()

    def finish(self, x_refs, out_refs, sems):
        copy, local, first, chips, me, sibling, c = self._parts(x_refs, out_refs, sems)
        passed = []
        for j, chip in enumerate(chips):
            for a in range(self.n):
                copy(a, 1 + j, (*chip, c), me).wait_recv()
                passed.append(copy(a, 4 + j, (*chip, c), sibling))
                passed[-1].start()
        for a in range(self.n):
            copy(a, 0, sibling, me).wait_recv()
            for j, chip in enumerate(chips):
                copy(a, 4 + j, (*chip, 1 - c), me).wait_recv()
        for cp in first + passed:
            cp.wait_send()
        for cp in local:
            cp.wait()


def _rms_in_and_gather(xin, g, blk):
    rows, cols = blk.shape
    half = rows // 2
    assert half % 16 == 0
    t, d = xin.shape
    tt = 512
    ni = t // tt

    def exchange(x_ref, out_ref, send_sems, recv_sems, local_sem, begin):
        x, y, c = _coords()
        me, sib, xn, yn, dg = (x, y, c), (x, y, 1 - c), (1 - x, y, c), (x, 1 - y, c), (1 - x, 1 - y, c)
        other = lambda p: (p[0], p[1], 1 - c)
        top, bot = pl.ds(0, half), pl.ds(half, half)

        def slot(p, part=None):
            ref = out_ref.at[4 * p[0] + 2 * p[1] + p[2]]
            return ref if part is None else ref.at[part]

        def copy(k, block, to, part=None, src=None):
            return pltpu.make_async_remote_copy(
                src_ref=slot(block, part) if src is None else src, dst_ref=slot(block, part),
                send_sem=send_sems.at[k], recv_sem=recv_sems.at[k], device_id=to, device_id_type=MESH)

        mine = pltpu.make_async_copy(x_ref, slot(me), local_sem)
        own = [copy(1, me, xn, src=x_ref), copy(2, me, yn, src=x_ref), copy(0, me, sib, src=x_ref)]
        if begin:
            for cp in [mine] + own:
                cp.start()
            return
        copy(1, xn, me).wait_recv()
        fwd = [copy(3, xn, yn, top), copy(5, xn, sib)]
        for cp in fwd:
            cp.start()
        copy(2, yn, me).wait_recv()
        fwd += [copy(4, yn, xn, bot), copy(6, yn, sib)]
        for cp in fwd[2:]:
            cp.start()
        copy(3, dg, me, top).wait_recv()
        fwd.append(copy(7, dg, sib, top))
        fwd[-1].start()
        copy(4, dg, me, bot).wait_recv()
        fwd.append(copy(8, dg, sib, bot))
        fwd[-1].start()
        copy(0, sib, me).wait_recv()
        copy(5, other(xn), me).wait_recv()
        copy(6, other(yn), me).wait_recv()
        copy(7, other(dg), me, top).wait_recv()
        copy(8, other(dg), me, bot).wait_recv()
        for cp in own + fwd:
            cp.wait_send()
        mine.wait()

    def body(w_ref, x_ref, g_ref, out_ref, h_ref, send_sems, recv_sems, local_sem):
        i = pl.program_id(0)

        @pl.when(i == 0)
        def _():
            exchange(w_ref, out_ref, send_sems, recv_sems, local_sem, True)

        xv = x_ref[...]
        r = lax.rsqrt(jnp.mean(xv * xv, axis=-1, keepdims=True) + EPS)
        h_ref[...] = (xv * r * g_ref[...]).astype(h_ref.dtype)

        @pl.when(i == ni - 1)
        def _():
            exchange(w_ref, out_ref, send_sems, recv_sems, local_sem, False)

    return _pallas(
        body, name="rms_in_w_in_all_gather", grid=(ni,),
        in_specs=[ANY, pl.BlockSpec((tt, d), lambda i: (i, 0)), pl.BlockSpec((1, d), lambda i: (0, 0))],
        out_specs=[ANY, pl.BlockSpec((tt, d), lambda i: (i, 0))],
        out_shape=[jax.ShapeDtypeStruct((N_DEV, rows, cols), blk.dtype), jax.ShapeDtypeStruct((t, d), BF16)],
        scratch_shapes=[pltpu.SemaphoreType.DMA((9,)), pltpu.SemaphoreType.DMA((9,)), pltpu.SemaphoreType.DMA],
        compiler_params=_cparams(("arbitrary",)),
    )(blk, xin, g)


def _all_gather_direct(blk, name):
    rows, cols = blk.shape

    def body(x_ref, out_ref, send_sems, recv_sems, local_sem):
        x, y, c = _coords()
        me = 4 * x + 2 * y + c
        mine = pltpu.make_async_copy(x_ref, out_ref.at[me], local_sem)
        mine.start()
        copies = []
        for k in range(1, N_DEV):
            fx, fy, fc = (k >> 2) & 1, (k >> 1) & 1, k & 1
            peer = (x ^ fx, y ^ fy, c ^ fc)
            copies.append(pltpu.make_async_remote_copy(
                src_ref=x_ref, dst_ref=out_ref.at[me], send_sem=send_sems.at[k - 1], recv_sem=recv_sems.at[k - 1],
                device_id=peer, device_id_type=MESH))
        for cp in copies:
            cp.start()
        for k in range(1, N_DEV):
            fx, fy, fc = (k >> 2) & 1, (k >> 1) & 1, k & 1
            src = 4 * (x ^ fx) + 2 * (y ^ fy) + (c ^ fc)
            pltpu.make_async_remote_copy(
                src_ref=x_ref, dst_ref=out_ref.at[src], send_sem=send_sems.at[k - 1], recv_sem=recv_sems.at[k - 1],
                device_id=(x, y, c), device_id_type=MESH).wait_recv()
        for cp in copies:
            cp.wait_send()
        mine.wait()

    return _pallas(
        body, name=name, in_specs=[ANY], out_specs=ANY,
        out_shape=jax.ShapeDtypeStruct((N_DEV, rows, cols), blk.dtype),
        scratch_shapes=[pltpu.SemaphoreType.DMA((7,)), pltpu.SemaphoreType.DMA((7,)), pltpu.SemaphoreType.DMA],
    )(blk)


def _pair_exchange(grads, name):
    n = len(grads)

    def body(*refs):
        g_refs, out_refs = refs[:n], refs[n:2 * n]
        send_sems, recv_sems = refs[2 * n:]
        x, y, c = _coords()
        copies = []
        for a in range(n):
            for k in range(4):
                copies.append(pltpu.make_async_remote_copy(
                    src_ref=g_refs[a].at[2 * k + (1 - c)], dst_ref=out_refs[a].at[k],
                    send_sem=send_sems.at[a, k], recv_sem=recv_sems.at[a, k],
                    device_id=(x, y, 1 - c), device_id_type=MESH))
        for cp in copies:
            cp.start()
        for cp in copies:
            cp.wait()

    return _pallas(
        body, name=name, in_specs=[ANY] * n, out_specs=[ANY] * n,
        out_shape=[jax.ShapeDtypeStruct((4,) + g.shape[1:], g.dtype) for g in grads],
        scratch_shapes=[pltpu.SemaphoreType.DMA((n, 4)), pltpu.SemaphoreType.DMA((n, 4))],
    )(*grads)


def _pair_add(grads, got, core, out_dtype, name):
    _, rows, cols = grads.shape
    tr = _row_tile(rows, 256) if rows % 8 == 0 else rows

    def body(c_ref, a_ref, b_ref, o_ref):
        o_ref[...] = (a_ref[...].astype(F32) + b_ref[...].astype(F32)).astype(o_ref.dtype)

    grid_spec = pltpu.PrefetchScalarGridSpec(
        num_scalar_prefetch=1, grid=(4, rows // tr),
        in_specs=[pl.BlockSpec((1, tr, cols), lambda k, i, c_ref: (2 * k + c_ref[0], i, 0)),
                  pl.BlockSpec((1, tr, cols), lambda k, i, c_ref: (k, i, 0))],
        out_specs=pl.BlockSpec((1, tr, cols), lambda k, i, c_ref: (k, i, 0)))
    return _pallas(
        body, name=name, grid_spec=grid_spec,
        out_shape=jax.ShapeDtypeStruct((4, rows, cols), out_dtype),
        compiler_params=_cparams(("parallel", "parallel"), 48),
    )(core, grads, got)


class _ChipExchange:
    def __init__(self, partials):
        self.n = len(partials)
        self.in_specs = [ANY] * self.n
        self.out_specs = [ANY] * self.n
        self.out_shape = [jax.ShapeDtypeStruct(p.shape, p.dtype) for p in partials]
        self.scratch = [pltpu.SemaphoreType.DMA((self.n, 3)), pltpu.SemaphoreType.DMA((self.n, 3)),
                        pltpu.SemaphoreType.DMA((self.n,))]

    def _copies(self, p_refs, out_refs, sems, receiving):
        send_sems, recv_sems, local_sems = sems
        x, y, c = _coords()
        mychip = 2 * x + y
        local = [pltpu.make_async_copy(p_refs[a].at[mychip], out_refs[a].at[mychip], local_sems.at[a])
                 for a in range(self.n)]
        remote = []
        for a in range(self.n):
            for k in range(1, 4):
                px, py = x ^ ((k >> 1) & 1), y ^ (k & 1)
                peer = 2 * px + py
                remote.append(pltpu.make_async_remote_copy(
                    src_ref=p_refs[a].at[peer], dst_ref=out_refs[a].at[peer if receiving else mychip],
                    send_sem=send_sems.at[a, k - 1], recv_sem=recv_sems.at[a, k - 1],
                    device_id=(x, y, c) if receiving else (px, py, c), device_id_type=MESH))
        return local, remote

    def start(self, p_refs, out_refs, sems):
        local, sends = self._copies(p_refs, out_refs, sems, False)
        for cp in local + sends:
            cp.start()

    def wait(self, p_refs, out_refs, sems):
        local, recvs = self._copies(p_refs, out_refs, sems, True)
        for cp in recvs:
            cp.wait_recv()
        for cp in recvs:
            cp.wait_send()
        for cp in local:
            cp.wait()


class _DirectExchange:
    def __init__(self, partials):
        self.n = len(partials)
        self.in_specs = [ANY] * self.n
        self.out_specs = [ANY] * self.n
        self.out_shape = [jax.ShapeDtypeStruct(p.shape, p.dtype) for p in partials]
        self.scratch = [pltpu.SemaphoreType.DMA((self.n, N_DEV - 1)), pltpu.SemaphoreType.DMA((self.n, N_DEV - 1)),
                        pltpu.SemaphoreType.DMA((self.n,))]

    def _copies(self, p_refs, out_refs, sems, receiving):
        send_sems, recv_sems, local_sems = sems
        x, y, c = _coords()
        me = 4 * x + 2 * y + c
        local = [pltpu.make_async_copy(p_refs[a].at[me], out_refs[a].at[me], local_sems.at[a])
                 for a in range(self.n)]
        remote = []
        for a in range(self.n):
            for k in range(1, N_DEV):
                px, py, pc = x ^ ((k >> 2) & 1), y ^ ((k >> 1) & 1), c ^ (k & 1)
                peer = 4 * px + 2 * py + pc
                remote.append(pltpu.make_async_remote_copy(
                    src_ref=p_refs[a].at[peer], dst_ref=out_refs[a].at[peer if receiving else me],
                    send_sem=send_sems.at[a, k - 1], recv_sem=recv_sems.at[a, k - 1],
                    device_id=(x, y, c) if receiving else (px, py, pc), device_id_type=MESH))
        return local, remote

    def start(self, p_refs, out_refs, sems):
        local, sends = self._copies(p_refs, out_refs, sems, False)
        for cp in local + sends:
            cp.start()

    def wait(self, p_refs, out_refs, sems):
        local, recvs = self._copies(p_refs, out_refs, sems, True)
        for cp in recvs:
            cp.wait_recv()
        for cp in recvs:
            cp.wait_send()
        for cp in local:
            cp.wait()


def _pad_lanes(v, n=HEAD_PAD):
    return jnp.pad(v, ((0, 0), (0, n - v.shape[1])))


def kernel(x, p, norm_g, w_in, conv_w, conv_b, dt_bias, a_log, d_skip, gnorm_g, pool_mix_w, pool_mix_b, pool_scale, w_branch_a, w_branch_b, w_out, ple_norm_g, w_ple_gate, w_ple_up, final_g, loss_target, m_norm_g, m_w_in, m_conv_w, m_conv_b, m_dt_bias, m_a_log, m_d_skip, m_gnorm_g, m_pool_mix_w, m_pool_mix_b, m_pool_scale, m_w_branch_a, m_w_branch_b, m_w_out, m_ple_norm_g, m_w_ple_gate, m_w_ple_up, m_final_g, v_norm_g, v_w_in, v_conv_w, v_conv_b, v_dt_bias, v_a_log, v_d_skip, v_gnorm_g, v_pool_mix_w, v_pool_mix_b, v_pool_scale, v_w_branch_a, v_w_branch_b, v_w_out, v_ple_norm_g, v_w_ple_gate, v_w_ple_up, v_final_g):
    nb, s, d = x.shape
    t = nb * s
    n_heads = dt_bias.shape[1]
    d_inner = n_heads * HEAD_DIM
    n_conv = conv_b.shape[1]
    d_pool = pool_mix_b.shape[1]
    n_in = w_in.shape[2] * N_DEV
    ple_dim = p.shape[-1]
    o_xbc, o_dt = d_inner, d_inner + n_conv
    o_u = o_dt + n_heads
    o_zp = o_u + d_pool
    o_ga, o_gb = o_zp + d_pool, o_zp + d_pool + d
    n_main = n_in - n_heads
    off_ga, off_gb, off_xbc = d_inner, d_inner + d, d_inner + 2 * d
    off_u = off_xbc + n_conv
    off_zp = off_u + d_pool

    sharded = [w_in[0], conv_w[0], pool_mix_w[0], w_branch_a[0], w_branch_b[0], w_out[0], w_ple_gate[0], w_ple_up[0]]
    sharded_m = [m_w_in[0], m_conv_w[0], m_pool_mix_w[0], m_w_branch_a[0], m_w_branch_b[0], m_w_out[0],
                 m_w_ple_gate[0], m_w_ple_up[0]]
    sharded_v = [v_w_in[0], v_conv_w[0], v_pool_mix_w[0], v_w_branch_a[0], v_w_branch_b[0], v_w_out[0],
                 v_w_ple_gate[0], v_w_ple_up[0]]
    ng = pool_mix_w.shape[1]
    pg = d_pool // ng
    two_d = lambda w: w.reshape(-1, w.shape[-1])
    sharded, sharded_m, sharded_v = ([two_d(w) for w in ws] for ws in (sharded, sharded_m, sharded_v))

    send = [w if i == 1 else w.astype(BF16) for i, w in enumerate(sharded)]
    x2 = x.reshape(t, d)
    chunk = n_in // N_DEV
    segments = [((0, o_xbc), 0), ((o_ga, o_gb), off_ga), ((o_gb, n_in), off_gb), ((o_xbc, o_dt), off_xbc),
                ((o_u, o_zp), off_u), ((o_zp, o_ga), off_zp)]

    def pieces_of(dev):
        lo, hi = dev * chunk, (dev + 1) * chunk
        out = []
        for (olo, ohi), off in sorted(segments + [((o_dt, o_u), None)], key=lambda e: e[0][0]):
            a, b = max(lo, olo), min(hi, ohi)
            if a < b:
                out.append((a - lo, b - lo, None if off is None else off + a - olo))
        return out

    def lane_cover(a, b, off):
        left = 0 if off is None else off % LANES
        return left, -(-(left + b - a) // LANES) * LANES

    send_width = max(sum(lane_cover(*pc)[1] for pc in pieces_of(dev)) for dev in range(N_DEV))

    def aligned_shard(dev):
        def build(w):
            bufs = []
            for a, b, off in pieces_of(dev):
                left, width = lane_cover(a, b, off)
                bufs.append(jnp.pad(w[:, a:b], ((0, 0), (left, width - left - (b - a)))))
            out = jnp.concatenate(bufs, axis=1)
            return jnp.pad(out, ((0, 0), (0, send_width - out.shape[1])))
        return build

    me_lin = 4 * lax.axis_index("x") + 2 * lax.axis_index("y") + lax.axis_index("c")
    w_in_sent = lax.switch(me_lin, [aligned_shard(dev) for dev in range(N_DEV)], send[0])
    w_in_all, h = _rms_in_and_gather(x2, norm_g, w_in_sent)
    seg = [w_in_all]
    w_main = jnp.zeros((d, n_main), BF16)
    w_dt = None
    for dev in range(N_DEV):
        q = 0
        for a, b, off in pieces_of(dev):
            left, width = lane_cover(a, b, off)
            cover = w_in_all[dev, :, q:q + width]
            q += width
            if off is None:
                w_dt = cover
            else:
                start = off - left
                w_main = w_main + jnp.pad(cover, ((0, 0), (start, n_main - start - width)))

    proj, dtraw, *rest = _in_proj(h, w_main, w_dt, send[1:])
    seg += rest
    cw_full = seg[1].transpose(1, 0, 2).reshape(CONV_W, n_conv)
    mixw_full = seg[2].reshape(N_DEV, ng, pg // N_DEV, pg).transpose(1, 0, 2, 3).reshape(ng, pg, pg)
    wa_full = seg[3].reshape(d_inner, d)
    wb_full = seg[4].reshape(d_pool, d)
    wout_full = seg[5].reshape(d, d)
    wg_full = seg[6].reshape(d, d)
    wup_full = seg[7].transpose(1, 0, 2).reshape(ple_dim, d)

    p2 = p.reshape(t, ple_dim)
    tgt = loss_target.reshape(t, d)
    bias_p, alog_p = _pad_lanes(dt_bias), _pad_lanes(a_log)
    dskip_full = jnp.repeat(d_skip, HEAD_DIM, axis=1)

    xbc, conv_pre = _conv_fwd(proj, cw_full, conv_b, nb, s, off_xbc, n_conv)
    y, ya3, prev = _ssd_fwd(xbc, proj, dtraw, bias_p, alog_p, dskip_full, gnorm_g, nb, s, d_inner)
    pooled, mixed, yb = _pool_fwd(proj, mixw_full, pool_mix_b, pool_scale, nb, s, off_u, off_zp, d_pool)
    (merged, h2, dx1, dx1b, dgl, dpu, dpa, dpb, dproj, loss_part, dfinal_g, d_ple_g) = _merge_to_grads(
        ya3, wa_full, yb, wb_full, proj, off_ga, off_gb, wout_full, x2, ple_norm_g, wg_full, p2, wup_full, tgt,
        final_g.reshape(1, d), n_main)

    g_wup = _matmul(p2, dpu, "tn", BF16, "dw_ple_up", 256, 512, t)
    g_wg = _matmul(h2, dgl, "tn", BF16, "dw_ple_gate", 512, 512, t)
    g_wout = _matmul(merged, dx1b, "tn", BF16, "dw_out", 512, 512, t)
    g_wa = _matmul(ya3, dpa, "tn", BF16, "dw_branch_a", 512, 512, t)
    g_wb = _matmul(yb, dpb, "tn", BF16, "dw_branch_b", 512, 512, t)
    dya3 = _matmul(dpa, wa_full, "nt", ACT, "d_ya3", 1024, 1024, d)
    dproj, dmixed, d_scale, d_mixb = _yb_bwd(dproj, dpb, wb_full, proj, mixed, pool_mix_b, pool_scale, off_zp)
    g_mixw = _dmixw(pooled, dmixed, ng)
    dproj = _pool_bwd(dproj, dmixed, mixw_full, nb, s, off_u)

    core = lax.axis_index("c").astype(I32).reshape(1)

    def pair_sums(by_dest, names):
        by_dest = [g.astype(BF16) for g in by_dest]
        got = _pair_exchange(by_dest, "rs_pair_exchange_" + names[0])
        return [_pair_add(g, r, core, BF16, "rs_pair_add_" + nm) for g, r, nm in zip(by_dest, got, names)]

    names_a = ["pool_mix_w", "w_branch_a", "w_branch_b", "w_out", "w_ple_gate", "w_ple_up"]
    partials_a = [
        g_mixw.reshape(ng, N_DEV, pg // N_DEV, pg).transpose(1, 0, 2, 3).reshape(N_DEV, ng * pg // N_DEV, pg)
        .astype(BF16),
        g_wa.reshape(N_DEV, d_inner // N_DEV, d), g_wb.reshape(N_DEV, d_pool // N_DEV, d),
        g_wout.reshape(N_DEV, d // N_DEV, d), g_wg.reshape(N_DEV, d // N_DEV, d),
        g_wup.reshape(ple_dim, N_DEV, d // N_DEV).transpose(1, 0, 2)]
    dproj, dxbc, ddt, d_gn, d_dsk, d_alog, d_dtb, *parts_a = _ssd_bwd(
        dproj, dya3, y, xbc, proj, dtraw, prev, bias_p, alog_p, dskip_full, gnorm_g, nb, s, d_inner, partials_a)
    dproj, g_cw, d_cb = _conv_bwd(dproj, dxbc, conv_pre, proj, cw_full, nb, s, off_xbc)
    g_wmain, g_wdt = _dw_in(h, dproj, ddt)

    def dest_cols(dev):
        lo, hi = dev * chunk, (dev + 1) * chunk
        srcs = [((olo, ohi), g_wmain, off) for (olo, ohi), off in segments] + [((o_dt, o_u), g_wdt, 0)]
        out = []
        for (olo, ohi), arr, off in sorted(srcs, key=lambda e: e[0][0]):
            a, b = max(lo, olo), min(hi, ohi)
            if a < b:
                out.append(arr[:, off + a - olo:off + b - olo])
        return jnp.concatenate(out, axis=1)

    names_b = ["w_in", "conv_w"]
    partials_b = pair_sums([jnp.stack([dest_cols(dev) for dev in range(N_DEV)], axis=0),
                            g_cw.reshape(CONV_W, N_DEV, n_conv // N_DEV).transpose(1, 0, 2)], names_b)
    grad_x, d_ng, *parts_b = _in_bwd(dproj, w_main, ddt, w_dt, x2, norm_g, dx1, partials_b)

    names = names_b + names_a
    parts = parts_b + parts_a
    sh_g, sh_d, sh_m, sh_v = [], [], [], []
    shapes = [w_in.shape, conv_w.shape, pool_mix_w.shape, w_branch_a.shape, w_branch_b.shape, w_out.shape,
              w_ple_gate.shape, w_ple_up.shape]
    for pt, w, m, v, nm, shp in zip(parts, sharded, sharded_m, sharded_v, names, shapes):
        res = _adamw(pt, w, m, v, "adamw_" + nm)
        for dst, r in zip((sh_g, sh_d, sh_m, sh_v), res):
            dst.append(r.reshape(shp))

    rep_w = [norm_g, conv_b, gnorm_g, pool_mix_b, pool_scale, ple_norm_g, final_g.reshape(1, d),
             bias_p, alog_p, _pad_lanes(d_skip)]
    rep_m = [m_norm_g, m_conv_b, m_gnorm_g, m_pool_mix_b, m_pool_scale, m_ple_norm_g, m_final_g.reshape(1, d),
             _pad_lanes(m_dt_bias), _pad_lanes(m_a_log), _pad_lanes(m_d_skip)]
    rep_v = [v_norm_g, v_conv_b, v_gnorm_g, v_pool_mix_b, v_pool_scale, v_ple_norm_g, v_final_g.reshape(1, d),
             _pad_lanes(v_dt_bias), _pad_lanes(v_a_log), _pad_lanes(v_d_skip)]
    rep_g = [d_ng, d_cb, d_gn, d_mixb, d_scale, d_ple_g, dfinal_g, d_dtb, d_alog, d_dsk]
    zero = jnp.zeros((1, LANES), F32)
    small = _all_gather_direct(jnp.concatenate(rep_g + [loss_part], axis=1), "small_all_gather")
    pk = lambda ws: jnp.concatenate(ws + [zero], axis=1)
    rg, rd, rm, rv = _adamw(small, pk(rep_w), pk(rep_m), pk(rep_v), "adamw_replicated")
    loss = rg[0, sum(int(w.size) for w in rep_w)]

    def unpack_rep(buf):
        out, pos_ = [], 0
        for w in rep_w:
            out.append(buf[:, pos_:pos_ + w.size])
            pos_ += w.size
        return out

    def arrange(sh, rep):
        rep = list(rep)
        rep[6] = rep[6].reshape(d)
        for i in (7, 8, 9):
            rep[i] = rep[i][:, :n_heads]
        return [rep[0], sh[0], sh[1], rep[1], rep[7], rep[8], rep[9], rep[2], sh[2], rep[3], rep[4],
                sh[3], sh[4], sh[5], rep[5], sh[6], sh[7], rep[6]]

    outs = [loss, grad_x.reshape(nb, s, d)]
    for sh, rep in ((sh_g, rg), (sh_d, rd), (sh_m, rm), (sh_v, rv)):
        outs += arrange(sh, unpack_rep(rep))
    return tuple(outs)
```

```python
import jax
import jax.numpy as jnp
from jax import lax
from jax.experimental import pallas as pl
from jax.experimental.pallas import tpu as pltpu

F32 = jnp.float32
BF16 = jnp.bfloat16
I32 = jnp.int32
MESH = pl.DeviceIdType.MESH
ANY = pl.BlockSpec(memory_space=pl.ANY)

EPS = 1e-6
CHUNK = 128
HEAD_DIM = 64
D_STATE = 128
GROUPS = 4
CONV_W = 4
POOL_WINDOWS = (2, 4, 8, 16)
ACT = jnp.bfloat16
LANES = 128
SUBLANES = 8
CONV_ROWS = 64
HEAD_PAD = 128
NEG = -1e30
N_DEV = 8

ADAM_LR = 0.001
ADAM_B1 = 0.9
ADAM_B2 = 0.999
ADAM_EPS = 1e-08
ADAM_WD = 0.01
ADAM_STEP = 10

NN = (((1,), (0,)), ((), ()))
NT = (((1,), (1,)), ((), ()))
TN = (((0,), (0,)), ((), ()))


def _pallas(body, **kw):
    return pl.pallas_call(body, **kw)


def _cparams(dims=None, vmem_mb=None):
    kw = {}
    if dims is not None:
        kw["dimension_semantics"] = dims
    if vmem_mb is not None:
        kw["vmem_limit_bytes"] = vmem_mb << 20
    return pltpu.CompilerParams(**kw)


def _dot(a, b, dims=NN):
    return lax.dot_general(a, b, dims, preferred_element_type=F32)


def _split3(v):
    hi = v.astype(BF16)
    r = v - hi.astype(F32)
    mid = r.astype(BF16)
    lo = (r - mid.astype(F32)).astype(BF16)
    return hi, mid, lo


def _exact_r(v, m, dims=NN):
    hi, mid, lo = _split3(v)
    return _dot(hi, m, dims) + _dot(mid, m, dims) + _dot(lo, m, dims)


def _exact_l(m, v, dims=NN):
    hi, mid, lo = _split3(v)
    return _dot(m, hi, dims) + _dot(m, mid, dims) + _dot(m, lo, dims)


def _row_tile(rows, target):
    best = None
    for cand in range(8, min(rows, target) + 1, 8):
        if rows % cand == 0:
            best = cand
    assert best is not None, rows
    return best


def _sum_all(v):
    return jnp.sum(jnp.sum(v, axis=1, keepdims=True), axis=0, keepdims=True)


def _iota(shape, dim):
    return lax.broadcasted_iota(I32, shape, dim)


def _sigmoid(x):
    return 0.5 * jnp.tanh(0.5 * x) + 0.5


def _softplus(x):
    return jnp.maximum(x, 0.0) + jnp.log(1.0 + jnp.exp(-jnp.abs(x)))


def _shift_down(v, j, row):
    return jnp.where(row >= j, pltpu.roll(v, j, 0), 0.0)


def _shift_up(v, j, row):
    n = v.shape[0]
    return jnp.where(row < n - j, pltpu.roll(v, n - j, 0), 0.0)


def _matmul(a, b, mode, out_dtype, name, tm, tn, tk):
    if mode == "nn":
        (m, k), n = a.shape, b.shape[1]
    elif mode == "nt":
        (m, k), n = a.shape, b.shape[0]
    else:
        (k, m), n = a.shape, b.shape[1]
    tm, tn, tk = min(tm, m), min(tn, n), min(tk, k)
    assert m % tm == 0 and n % tn == 0 and k % tk == 0, (name, m, n, k)
    nk = k // tk
    dims = {"nn": NN, "nt": NT, "tn": TN}[mode]

    def body_acc(a_ref, b_ref, o_ref, acc_ref):
        kk = pl.program_id(2)

        @pl.when(kk == 0)
        def _():
            acc_ref[...] = jnp.zeros_like(acc_ref)

        acc_ref[...] += _dot(a_ref[...].astype(BF16), b_ref[...].astype(BF16), dims)

        @pl.when(kk == nk - 1)
        def _():
            o_ref[...] = acc_ref[...].astype(o_ref.dtype)

    def body_one(a_ref, b_ref, o_ref):
        o_ref[...] = _dot(a_ref[...].astype(BF16), b_ref[...].astype(BF16), dims).astype(o_ref.dtype)

    body = body_one if nk == 1 else body_acc
    if mode == "tn":
        a_spec = pl.BlockSpec((tk, tm), lambda i, j, kk: (kk, i))
    else:
        a_spec = pl.BlockSpec((tm, tk), lambda i, j, kk: (i, kk))
    if mode == "nt":
        b_spec = pl.BlockSpec((tn, tk), lambda i, j, kk: (j, kk))
    else:
        b_spec = pl.BlockSpec((tk, tn), lambda i, j, kk: (kk, j))
    return _pallas(
        body, name=name, grid=(m // tm, n // tn, nk),
        in_specs=[a_spec, b_spec],
        out_specs=pl.BlockSpec((tm, tn), lambda i, j, kk: (i, j)),
        out_shape=jax.ShapeDtypeStruct((m, n), out_dtype),
        scratch_shapes=[] if nk == 1 else [pltpu.VMEM((tm, tn), F32)],
        compiler_params=_cparams(("parallel", "parallel", "arbitrary"), 56),
    )(a, b)


def _in_proj(h, w_main, w_dt, blocks):
    t, d = h.shape
    n = w_main.shape[1]
    tm, tn = min(1024, t), 1024
    ni, nj = t // tm, n // tn
    gather = _TwoLevelGather(blocks)

    def body(*refs):
        a_ref, b_ref, wdt_ref = refs[:3]
        x_refs = refs[3:3 + gather.n]
        o_ref, dt_ref = refs[3 + gather.n:5 + gather.n]
        got_refs = refs[5 + gather.n:5 + 2 * gather.n]
        sems = refs[5 + 2 * gather.n:]
        i, j = pl.program_id(0), pl.program_id(1)

        @pl.when((i == 0) & (j == 0))
        def _():
            gather.start(x_refs, got_refs, sems)

        o_ref[...] = _dot(a_ref[...], b_ref[...]).astype(o_ref.dtype)

        @pl.when(j == 0)
        def _():
            dt_ref[...] = _dot(a_ref[...], wdt_ref[...])

        @pl.when((i == ni - 1) & (j == nj - 1))
        def _():
            gather.finish(x_refs, got_refs, sems)

    return _pallas(
        body, name="in_proj", grid=(ni, nj),
        in_specs=[pl.BlockSpec((tm, d), lambda i, j: (i, 0)), pl.BlockSpec((d, tn), lambda i, j: (0, j)),
                  pl.BlockSpec((d, HEAD_PAD), lambda i, j: (0, 0))] + gather.in_specs,
        out_specs=[pl.BlockSpec((tm, tn), lambda i, j: (i, j)), pl.BlockSpec((tm, HEAD_PAD), lambda i, j: (i, 0))]
        + gather.out_specs,
        out_shape=[jax.ShapeDtypeStruct((t, n), ACT), jax.ShapeDtypeStruct((t, HEAD_PAD), F32)] + gather.out_shape,
        scratch_shapes=gather.scratch,
        compiler_params=_cparams(("arbitrary", "arbitrary"), 56),
    )(h, w_main, w_dt, *blocks)


def _dw_in(h, dproj, ddt):
    t, d = h.shape
    n = dproj.shape[1]
    tm, tn = 512, 1024

    def body(a_ref, b_ref, ddt_ref, o_ref, odt_ref):
        a = a_ref[...]
        o_ref[...] = _dot(a, b_ref[...], TN).astype(o_ref.dtype)

        @pl.when(pl.program_id(1) == 0)
        def _():
            odt_ref[...] = _dot(a, ddt_ref[...], TN).astype(odt_ref.dtype)

    return _pallas(
        body, name="dw_in", grid=(d // tm, n // tn),
        in_specs=[pl.BlockSpec((t, tm), lambda i, j: (0, i)), pl.BlockSpec((t, tn), lambda i, j: (0, j)),
                  pl.BlockSpec((t, HEAD_PAD), lambda i, j: (0, 0))],
        out_specs=[pl.BlockSpec((tm, tn), lambda i, j: (i, j)), pl.BlockSpec((tm, HEAD_PAD), lambda i, j: (i, 0))],
        out_shape=[jax.ShapeDtypeStruct((d, n), BF16), jax.ShapeDtypeStruct((d, HEAD_PAD), BF16)],
        compiler_params=_cparams(("parallel", "arbitrary"), 56),
    )(h, dproj, ddt)


def _conv_fwd(proj, conv_w, conv_b, nb, s, off_xbc, n_conv):
    cw = 256
    rc = CONV_ROWS

    def body(x_ref, w_ref, b_ref, o_ref, pre_ref, pad_ref):
        pad_ref[0:SUBLANES, :] = jnp.zeros((SUBLANES, cw), F32)
        pad_ref[SUBLANES:, :] = x_ref[...].astype(F32)
        taps = [w_ref[k:k + 1, :] for k in range(CONV_W)]
        bias = b_ref[...]
        for r in range(s // rc):
            win = pad_ref[pl.ds(r * rc, rc + SUBLANES), :]
            acc = bias + win[SUBLANES:, :] * taps[CONV_W - 1]
            for j in range(1, CONV_W):
                acc = acc + pltpu.roll(win, j, 0)[SUBLANES:, :] * taps[CONV_W - 1 - j]
            pre_ref[pl.ds(r * rc, rc), :] = acc.astype(pre_ref.dtype)
            o_ref[pl.ds(r * rc, rc), :] = (acc * _sigmoid(acc)).astype(o_ref.dtype)

    out = pl.BlockSpec((s, cw), lambda b, j: (b, j))
    return _pallas(
        body, name="conv_fwd", grid=(nb, n_conv // cw),
        in_specs=[pl.BlockSpec((s, cw), lambda b, j: (b, off_xbc // cw + j)),
                  pl.BlockSpec((CONV_W, cw), lambda b, j: (0, j)),
                  pl.BlockSpec((1, cw), lambda b, j: (0, j))],
        out_specs=[out, out],
        out_shape=[jax.ShapeDtypeStruct((nb * s, n_conv), ACT), jax.ShapeDtypeStruct((nb * s, n_conv), ACT)],
        scratch_shapes=[pltpu.VMEM((s + SUBLANES, cw), F32)],
        compiler_params=_cparams(("parallel", "parallel"), 48),
    )(proj, conv_w, conv_b)


def _ssd_common(dtraw, bias, alog, n_heads):
    l = dtraw.shape[0]
    lane = _iota((l, HEAD_PAD), 1)
    lane1 = _iota((1, HEAD_PAD), 1)
    apad = jnp.where(lane1 < n_heads, -jnp.exp(alog), 0.0)
    dt = jnp.where(lane < n_heads, _softplus(dtraw + bias), 0.0)
    a = dt * apad
    tri = (_iota((l, l), 1) <= _iota((l, l), 0)).astype(BF16)
    acs = _exact_l(tri, a)
    width = n_heads * HEAD_DIM
    ex = (_iota((HEAD_PAD, width), 1) // HEAD_DIM == _iota((HEAD_PAD, width), 0)).astype(BF16)
    dt_full = _exact_r(dt, ex)
    acs_full = _exact_r(acs, ex)
    return dt, apad, acs, dt_full, acs_full, tri, ex


def _decay_mats(acs, acs_t, h, mask):
    col = acs[:, h:h + 1]
    row = acs_t[h:h + 1, :]
    return jnp.exp(jnp.where(mask, col - row, NEG))


def _ssd_fwd(xbc, proj, dtraw, dt_bias, a_log, dskip_full, gnorm, nb, s, d_inner):
    l = CHUNK
    nc = s // l
    n_heads = d_inner // HEAD_DIM
    gw = d_inner // GROUPS
    pairs_per_group = gw // LANES
    gn_cols = GROUPS * D_STATE

    def body(xs_ref, bm_ref, cm_ref, z_ref, dtr_ref, bias_ref, alog_ref, dsk_ref, gn_ref,
             y_ref, ya3_ref, prev_ref, state_ref):
        @pl.when(pl.program_id(1) == 0)
        def _():
            state_ref[...] = jnp.zeros_like(state_ref)

        xs = xs_ref[...].astype(F32)
        dt, apad, acs, dt_full, acs_full, tri, ex = _ssd_common(
            dtr_ref[...], bias_ref[...], alog_ref[...], n_heads)
        acs_t = acs.T
        e_full = jnp.exp(acs_full)
        last = acs_full[l - 1:l, :]
        dte = jnp.exp(last - acs_full)
        cd = jnp.exp(last)
        xdt = xs * dt_full
        xd = xdt * dte
        mask = _iota((l, l), 1) <= _iota((l, l), 0)
        lane = _iota((l, LANES), 1)
        prev_ref[0] = state_ref[...]
        for g in range(GROUPS):
            gsl = slice(g * gw, (g + 1) * gw)
            bg = bm_ref[:, g * D_STATE:(g + 1) * D_STATE].astype(BF16)
            cg = cm_ref[:, g * D_STATE:(g + 1) * D_STATE].astype(BF16)
            cb = _dot(cg, bg, NT)
            outs = []
            for q in range(pairs_per_group):
                pair = g * pairs_per_group + q
                h0, h1 = 2 * pair, 2 * pair + 1
                xpb = xdt[:, pair * LANES:(pair + 1) * LANES].astype(BF16)
                m0 = (cb * _decay_mats(acs, acs_t, h0, mask)).astype(BF16)
                m1 = (cb * _decay_mats(acs, acs_t, h1, mask)).astype(BF16)
                zero = jnp.zeros_like(xpb)
                rhs = jnp.concatenate([jnp.where(lane < HEAD_DIM, xpb, zero),
                                       jnp.where(lane >= HEAD_DIM, xpb, zero)], axis=0)
                outs.append(_dot(jnp.concatenate([m0, m1], axis=1), rhs))
            prev_g = state_ref[:, gsl]
            y_off = _dot(cg, prev_g.astype(BF16)) * e_full[:, gsl]
            y_g = jnp.concatenate(outs, axis=1) + y_off
            y_ref[:, gsl] = y_g.astype(y_ref.dtype)
            st = _dot(bg, xd[:, gsl].astype(BF16), TN)
            state_ref[:, gsl] = prev_g * cd[:, gsl] + st
            zv = z_ref[:, gsl].astype(F32)
            blk = (y_g + xs[:, gsl] * dsk_ref[:, gsl]) * (zv * _sigmoid(zv))
            rg = lax.rsqrt(jnp.mean(blk * blk, axis=-1, keepdims=True) + EPS)
            ya3_ref[:, gsl] = (blk * rg * gn_ref[:, gsl]).astype(ya3_ref.dtype)

    row = lambda b, c: b * nc + c
    small = lambda n: pl.BlockSpec((1, n), lambda b, c: (0, 0))
    return _pallas(
        body, name="ssd_fwd", grid=(nb, nc),
        in_specs=[pl.BlockSpec((l, d_inner), lambda b, c: (row(b, c), 0)),
                  pl.BlockSpec((l, gn_cols), lambda b, c: (row(b, c), d_inner // gn_cols)),
                  pl.BlockSpec((l, gn_cols), lambda b, c: (row(b, c), d_inner // gn_cols + 1)),
                  pl.BlockSpec((l, d_inner), lambda b, c: (row(b, c), 0)),
                  pl.BlockSpec((l, HEAD_PAD), lambda b, c: (row(b, c), 0)),
                  small(HEAD_PAD), small(HEAD_PAD), small(d_inner), small(d_inner)],
        out_specs=[pl.BlockSpec((l, d_inner), lambda b, c: (row(b, c), 0)),
                   pl.BlockSpec((l, d_inner), lambda b, c: (row(b, c), 0)),
                   pl.BlockSpec((1, D_STATE, d_inner), lambda b, c: (row(b, c), 0, 0))],
        out_shape=[jax.ShapeDtypeStruct((nb * s, d_inner), ACT),
                   jax.ShapeDtypeStruct((nb * s, d_inner), BF16),
                   jax.ShapeDtypeStruct((nb * nc, D_STATE, d_inner), F32)],
        scratch_shapes=[pltpu.VMEM((D_STATE, d_inner), F32)],
        compiler_params=_cparams(("parallel", "arbitrary"), 48),
    )(xbc, xbc, xbc, proj, dtraw, dt_bias, a_log, dskip_full, gnorm)


def _pool_fwd(proj, mixw, mix_b, scale, nb, s, off_u, off_zp, d_pool):
    pg = d_pool // len(POOL_WINDOWS)

    def body(u_ref, zp_ref, w_ref, b_ref, s_ref, pooled_ref, mixed_ref, yb_ref):
        g = pl.program_id(1)
        uv = u_ref[...].astype(F32)
        row = _iota(uv.shape, 0)
        pos = (_iota((uv.shape[0], 1), 0) + 1).astype(F32)
        for gi, w in enumerate(POOL_WINDOWS):
            @pl.when(g == gi)
            def _():
                acc, span = uv, 1
                while span < w:
                    acc = acc + _shift_down(acc, span, row)
                    span *= 2
                pooled_ref[...] = (acc / jnp.minimum(pos, float(w)) - uv).astype(pooled_ref.dtype)

        mixed = _dot(pooled_ref[...], w_ref[0])
        mixed_ref[...] = mixed.astype(mixed_ref.dtype)
        zp = zp_ref[...].astype(F32)
        yb_ref[...] = ((mixed + b_ref[...]) * s_ref[...] * (zp * _sigmoid(zp))).astype(yb_ref.dtype)

    seg = lambda off: pl.BlockSpec((s, pg), lambda b, g: (b, off // pg + g))
    vec = pl.BlockSpec((1, pg), lambda b, g: (0, g))
    return _pallas(
        body, name="pool_fwd", grid=(nb, len(POOL_WINDOWS)),
        in_specs=[seg(off_u), seg(off_zp), pl.BlockSpec((1, pg, pg), lambda b, g: (g, 0, 0)), vec, vec],
        out_specs=[seg(0), seg(0), seg(0)],
        out_shape=[jax.ShapeDtypeStruct((nb * s, d_pool), BF16), jax.ShapeDtypeStruct((nb * s, d_pool), ACT),
                   jax.ShapeDtypeStruct((nb * s, d_pool), BF16)],
        compiler_params=_cparams(("parallel", "parallel"), 48),
    )(proj, proj, mixw, mix_b, scale)


def _merge_to_grads(ya3, wa, yb, wb, proj, off_ga, off_gb, wout, x, ple_g, wg, p, wup, target, final_g, n_main):
    t, d = x.shape
    pd = p.shape[1]
    tm = 256
    assert off_gb == off_ga + d and off_ga % (2 * d) == 0

    def body(a_ref, wa_ref, b_ref, wb_ref, ga_ref, gb_ref, wout_ref, x_ref, pg_ref, wg_ref, p_ref, wup_ref,
             tg_ref, fg_ref,
             mg_ref, h2_ref, dx1_ref, dx1b_ref, dgl_ref, dpu_ref, dpa_ref, dpb_ref, dgate_ref,
             loss_ref, dfg_ref, dpg_ref):
        @pl.when(pl.program_id(0) == 0)
        def _():
            loss_ref[...] = jnp.zeros_like(loss_ref)
            dfg_ref[...] = jnp.zeros_like(dfg_ref)
            dpg_ref[...] = jnp.zeros_like(dpg_ref)

        pa = _dot(a_ref[...], wa_ref[...])
        pb = _dot(b_ref[...], wb_ref[...])
        sa = _sigmoid(ga_ref[...].astype(F32))
        sb = _sigmoid(gb_ref[...].astype(F32))
        merged = (sa * pa + sb * pb).astype(BF16)
        mg_ref[...] = merged
        x1 = x_ref[...] + _dot(merged, wout_ref[...])
        r1 = lax.rsqrt(jnp.mean(x1 * x1, axis=-1, keepdims=True) + EPS)
        xh1 = x1 * r1
        pg = pg_ref[...]
        h2 = (xh1 * pg).astype(BF16)
        h2_ref[...] = h2
        gate = _sigmoid(_dot(h2, wg_ref[...]))
        pu = _dot(p_ref[...].astype(BF16), wup_ref[...])
        x2 = x1 + gate * pu
        r2 = lax.rsqrt(jnp.mean(x2 * x2, axis=-1, keepdims=True) + EPS)
        xh2 = x2 * r2
        gfin = fg_ref[...]
        err = xh2 * gfin - tg_ref[...]
        loss_ref[...] += _sum_all(err * err) * (0.5 / d)
        dy = err * (1.0 / d)
        dfg_ref[...] += jnp.sum(dy * xh2, axis=0, keepdims=True)
        dg = dy * gfin
        dx2 = r2 * (dg - xh2 * jnp.mean(dg * xh2, axis=-1, keepdims=True))
        dgl = (dx2 * pu * gate * (1.0 - gate)).astype(BF16)
        dgl_ref[...] = dgl
        dpu_ref[...] = (dx2 * gate).astype(dpu_ref.dtype)
        dh2 = _dot(dgl, wg_ref[...], NT)
        dpg_ref[...] += jnp.sum(dh2 * xh1, axis=0, keepdims=True)
        dg = dh2 * pg
        dx1 = dx2 + r1 * (dg - xh1 * jnp.mean(dg * xh1, axis=-1, keepdims=True))
        dx1_ref[...] = dx1
        dx1b = dx1.astype(BF16)
        dx1b_ref[...] = dx1b
        dm = _dot(dx1b, wout_ref[...], NT)
        dpa_ref[...] = (dm * sa).astype(dpa_ref.dtype)
        dpb_ref[...] = (dm * sb).astype(dpb_ref.dtype)
        dgate_ref[:, :d] = (dm * pa * sa * (1.0 - sa)).astype(dgate_ref.dtype)
        dgate_ref[:, d:] = (dm * pb * sb * (1.0 - sb)).astype(dgate_ref.dtype)

    rows = lambda n: pl.BlockSpec((tm, n), lambda i: (i, 0))
    resident = lambda a: pl.BlockSpec(a.shape, lambda i: (0, 0), pipeline_mode=pl.Buffered(1))
    vec = lambda n: pl.BlockSpec((1, n), lambda i: (0, 0))
    bf = lambda: jax.ShapeDtypeStruct((t, d), BF16)
    return _pallas(
        body, name="merge_to_grads", grid=(t // tm,),
        in_specs=[rows(ya3.shape[1]), resident(wa), rows(yb.shape[1]), resident(wb),
                  pl.BlockSpec((tm, d), lambda i: (i, off_ga // d)), pl.BlockSpec((tm, d), lambda i: (i, off_gb // d)),
                  resident(wout), rows(d), vec(d), resident(wg), rows(pd), resident(wup), rows(d), vec(d)],
        out_specs=[rows(d)] * 8 + [pl.BlockSpec((tm, 2 * d), lambda i: (i, off_ga // (2 * d))),
                                   vec(LANES), vec(d), vec(d)],
        out_shape=[bf(), bf(), jax.ShapeDtypeStruct((t, d), F32), bf(), bf(), bf(), bf(), bf(),
                   jax.ShapeDtypeStruct((t, n_main), BF16), jax.ShapeDtypeStruct((1, LANES), F32),
                   jax.ShapeDtypeStruct((1, d), F32), jax.ShapeDtypeStruct((1, d), F32)],
        compiler_params=_cparams(("arbitrary",), 60),
    )(ya3, wa, yb, wb, proj, proj, wout, x, ple_g, wg, p, wup, target, final_g)


def _yb_bwd(dproj, dpb, wb, proj, mixed, mix_b, scale, off_zp):
    t, dp = mixed.shape
    tm = 512

    def body(_, dpb_ref, w_ref, zp_ref, mx_ref, b_ref, s_ref, dzp_ref, dmx_ref, dsc_ref, db_ref):
        @pl.when(pl.program_id(0) == 0)
        def _():
            dsc_ref[...] = jnp.zeros_like(dsc_ref)
            db_ref[...] = jnp.zeros_like(db_ref)

        dyb = _dot(dpb_ref[...], w_ref[...], NT)
        zp = zp_ref[...].astype(F32)
        sg = _sigmoid(zp)
        mb = mx_ref[...].astype(F32) + b_ref[...]
        yb0 = mb * s_ref[...]
        dyb0 = dyb * (zp * sg)
        dzp_ref[...] = (dyb * yb0 * (sg * (1.0 + zp * (1.0 - sg)))).astype(dzp_ref.dtype)
        dsc_ref[...] += jnp.sum(dyb0 * mb, axis=0, keepdims=True)
        dmx = dyb0 * s_ref[...]
        db_ref[...] += jnp.sum(dmx, axis=0, keepdims=True)
        dmx_ref[...] = dmx.astype(dmx_ref.dtype)

    rowblk = pl.BlockSpec((tm, dp), lambda i: (i, 0))
    vec = pl.BlockSpec((1, dp), lambda i: (0, 0))
    seg = pl.BlockSpec((tm, dp), lambda i: (i, off_zp // dp))
    return _pallas(
        body, name="yb_bwd", grid=(t // tm,),
        in_specs=[ANY, pl.BlockSpec((tm, wb.shape[1]), lambda i: (i, 0)),
                  pl.BlockSpec(wb.shape, lambda i: (0, 0)), seg, rowblk, vec, vec],
        out_specs=[seg, rowblk, vec, vec],
        out_shape=[jax.ShapeDtypeStruct(dproj.shape, dproj.dtype), jax.ShapeDtypeStruct((t, dp), BF16),
                   jax.ShapeDtypeStruct((1, dp), F32), jax.ShapeDtypeStruct((1, dp), F32)],
        input_output_aliases={0: 0},
        compiler_params=_cparams(("arbitrary",), 56),
    )(dproj, dpb, wb, proj, mixed, mix_b, scale)


def _dmixw(pooled, dmixed, ng):
    t, dp = pooled.shape
    pg = dp // ng
    tk = min(2048, t)
    nk = t // tk

    def body(a_ref, b_ref, o_ref):
        @pl.when(pl.program_id(1) == 0)
        def _():
            o_ref[...] = jnp.zeros_like(o_ref)

        o_ref[0] += _dot(a_ref[...], b_ref[...], TN)

    blk = pl.BlockSpec((tk, pg), lambda g, k: (k, g))
    return _pallas(
        body, name="dmixw", grid=(ng, nk),
        in_specs=[blk, blk],
        out_specs=pl.BlockSpec((1, pg, pg), lambda g, k: (g, 0, 0)),
        out_shape=jax.ShapeDtypeStruct((ng, pg, pg), F32),
        compiler_params=_cparams(("parallel", "arbitrary")),
    )(pooled, dmixed)


def _pool_bwd(dproj, dmixed, mixw, nb, s, off_u):
    dp = dmixed.shape[1]
    pg = dp // len(POOL_WINDOWS)

    def body(_, d_ref, w_ref, o_ref):
        g = pl.program_id(1)
        dv = _dot(d_ref[...], w_ref[0], NT)
        row = _iota(dv.shape, 0)
        pos = (_iota((dv.shape[0], 1), 0) + 1).astype(F32)
        for gi, w in enumerate(POOL_WINDOWS):
            @pl.when(g == gi)
            def _():
                acc, span = dv / jnp.minimum(pos, float(w)), 1
                while span < w:
                    acc = acc + _shift_up(acc, span, row)
                    span *= 2
                o_ref[...] = (acc - dv).astype(o_ref.dtype)

    return _pallas(
        body, name="pool_bwd", grid=(nb, len(POOL_WINDOWS)),
        in_specs=[ANY, pl.BlockSpec((s, pg), lambda b, g: (b, g)), pl.BlockSpec((1, pg, pg), lambda b, g: (g, 0, 0))],
        out_specs=pl.BlockSpec((s, pg), lambda b, g: (b, off_u // pg + g)),
        out_shape=jax.ShapeDtypeStruct(dproj.shape, dproj.dtype),
        input_output_aliases={0: 0},
        compiler_params=_cparams(("parallel", "parallel"), 48),
    )(dproj, dmixed, mixw)


def _ssd_bwd(dproj, dya3, y, xbc, proj, dtraw, prev, dt_bias, a_log, dskip_full, gnorm, nb, s, d_inner, partials):
    l = CHUNK
    nc = s // l
    n_heads = d_inner // HEAD_DIM
    gw = d_inner // GROUPS
    pairs_per_group = gw // LANES
    gn_cols = GROUPS * D_STATE
    n_conv = d_inner + 2 * gn_cols
    last_step = nb * nc - 1

    xch = _DirectExchange(partials)
    n_in, n_out = 13, 7

    def body(*refs):
        (_, dya3_ref, y_ref, xs_ref, bm_ref, cm_ref, z_ref, dtr_ref, prev_ref, bias_ref, alog_ref,
         dsk_ref, gn_ref) = refs[:n_in]
        p_refs = refs[n_in:n_in + xch.n]
        outs = refs[n_in + xch.n:]
        dz_ref, dxbc_ref, ddt_ref, dgn_ref, dskh_ref, dalog_ref, dbias_ref = outs[:n_out]
        got_refs = outs[n_out:n_out + xch.n]
        g_ref, dskacc_ref = outs[n_out + xch.n:n_out + xch.n + 2]
        sems = outs[n_out + xch.n + 2:]
        step = pl.program_id(0) * nc + pl.program_id(1)

        @pl.when(pl.program_id(1) == 0)
        def _():
            g_ref[...] = jnp.zeros_like(g_ref)

        @pl.when(step == 0)
        def _():
            xch.start(p_refs, got_refs, sems)
            dgn_ref[...] = jnp.zeros_like(dgn_ref)
            dalog_ref[...] = jnp.zeros_like(dalog_ref)
            dbias_ref[...] = jnp.zeros_like(dbias_ref)
            dskacc_ref[...] = jnp.zeros_like(dskacc_ref)

        xs = xs_ref[...].astype(F32)
        zv = z_ref[...].astype(F32)
        dsk = dsk_ref[...]
        sg = _sigmoid(zv)
        sz = zv * sg
        ya1 = y_ref[...].astype(F32) + xs * dsk
        ya2 = ya1 * sz
        parts = []
        for g in range(GROUPS):
            gsl = slice(g * gw, (g + 1) * gw)
            blk = ya2[:, gsl]
            rg = lax.rsqrt(jnp.mean(blk * blk, axis=-1, keepdims=True) + EPS)
            xh = blk * rg
            d3 = dya3_ref[:, gsl].astype(F32)
            dgn_ref[:, gsl] += jnp.sum(d3 * xh, axis=0, keepdims=True)
            dg = d3 * gn_ref[:, gsl]
            parts.append(rg * (dg - xh * jnp.mean(dg * xh, axis=-1, keepdims=True)))
        dya2 = jnp.concatenate(parts, axis=1)
        dy = dya2 * sz
        dz_ref[...] = (dya2 * ya1 * (sg * (1.0 + zv * (1.0 - sg)))).astype(dz_ref.dtype)
        dskacc_ref[...] += jnp.sum(dy * xs, axis=0, keepdims=True)

        dtraw_v = dtr_ref[...]
        bias = bias_ref[...]
        dt, apad, acs, dt_full, acs_full, tri, ex = _ssd_common(dtraw_v, bias, alog_ref[...], n_heads)
        acs_t = acs.T
        e_full = jnp.exp(acs_full)
        last = acs_full[l - 1:l, :]
        dte = jnp.exp(last - acs_full)
        cd = jnp.exp(last)
        xdt = xs * dt_full
        xd = xdt * dte
        dy_e = dy * e_full
        mask = _iota((l, l), 1) <= _iota((l, l), 0)
        lane = _iota((l, LANES), 1)
        hlane = _iota((l, HEAD_PAD), 1)
        hsub = _iota((HEAD_PAD, l), 0)
        dacs = jnp.zeros((l, HEAD_PAD), F32)
        dacs_t = jnp.zeros((HEAD_PAD, l), F32)
        dacs_full, dlast, dxdt_all, db_all, dc_all = [], [], [], [], []
        for g in range(GROUPS):
            gsl = slice(g * gw, (g + 1) * gw)
            bg = bm_ref[:, g * D_STATE:(g + 1) * D_STATE].astype(BF16)
            cg = cm_ref[:, g * D_STATE:(g + 1) * D_STATE].astype(BF16)
            prev_g = prev_ref[0, :, gsl]
            prev_b = prev_g.astype(BF16)
            gst = g_ref[:, gsl]
            gst_b = gst.astype(BF16)
            dye_b = dy_e[:, gsl].astype(BF16)
            dc = _dot(dye_b, prev_b, NT)
            dprev = _dot(cg, dye_b, TN)
            y_off = _dot(cg, prev_b) * e_full[:, gsl]
            xd_g = xd[:, gsl]
            db = _dot(xd_g.astype(BF16), gst_b, NT)
            dxd = _dot(bg, gst_b)
            wgt = dxd * xd_g
            dacs_full.append(dy[:, gsl] * y_off - wgt)
            dlast.append(jnp.sum(wgt, axis=0, keepdims=True)
                         + jnp.sum(prev_g * gst, axis=0, keepdims=True) * cd[:, gsl])
            g_ref[:, gsl] = dprev + cd[:, gsl] * gst
            dxdt_g = dxd * dte[:, gsl]
            cb = _dot(cg, bg, NT)
            dcb = jnp.zeros((l, l), F32)
            for q in range(pairs_per_group):
                pair = g * pairs_per_group + q
                psl = slice(pair * LANES, (pair + 1) * LANES)
                xpb = xdt[:, psl].astype(BF16)
                dyp = dy[:, psl].astype(BF16)
                zero = jnp.zeros_like(dyp)
                halves = (jnp.where(lane < HEAD_DIM, dyp, zero), jnp.where(lane >= HEAD_DIM, dyp, zero))
                acc = dxdt_g[:, q * LANES:(q + 1) * LANES]
                for k, dyh in enumerate(halves):
                    h = 2 * pair + k
                    dk = _decay_mats(acs, acs_t, h, mask)
                    mm = cb * dk
                    dm = _dot(dyh, xpb, NT)
                    dcb = dcb + dm * dk
                    dseg = dm * mm
                    dacs = dacs + jnp.where(hlane == h, jnp.sum(dseg, axis=1, keepdims=True), 0.0)
                    dacs_t = dacs_t + jnp.where(hsub == h, jnp.sum(dseg, axis=0, keepdims=True), 0.0)
                    acc = acc + _dot(mm.astype(BF16), dyh, TN)
                dxdt_all.append(acc)
            dcb_b = dcb.astype(BF16)
            dc_all.append(dc + _dot(dcb_b, bg))
            db_all.append(db + _dot(dcb_b, cg, TN))

        dxdt = jnp.concatenate(dxdt_all, axis=1)
        dxbc_ref[:, :d_inner] = (dy * dsk + dxdt * dt_full).astype(dxbc_ref.dtype)
        dxbc_ref[:, d_inner:d_inner + gn_cols] = jnp.concatenate(db_all, axis=1).astype(dxbc_ref.dtype)
        dxbc_ref[:, d_inner + gn_cols:] = jnp.concatenate(dc_all, axis=1).astype(dxbc_ref.dtype)
        ddt = _exact_r(dxdt * xs, ex, NT)
        rowi = _iota((l, d_inner), 0)
        dacs_f = jnp.concatenate(dacs_full, axis=1) + jnp.where(rowi == l - 1, jnp.concatenate(dlast, axis=1), 0.0)
        dacs = dacs + _exact_r(dacs_f, ex, NT) - dacs_t.T
        da = _exact_l(tri, dacs, TN)
        ddt = ddt + da * apad
        dalog_ref[...] += jnp.sum(da * dt, axis=0, keepdims=True) * apad
        ddt_raw = ddt * _sigmoid(dtraw_v + bias)
        dbias_ref[...] += jnp.sum(ddt_raw, axis=0, keepdims=True)
        ddt_ref[...] = ddt_raw.astype(ddt_ref.dtype)

        @pl.when(step == last_step)
        def _():
            dskh_ref[...] = _exact_r(dskacc_ref[...], ex, NT)[0:1, :]
            xch.wait(p_refs, got_refs, sems)

    row = lambda b, c: b * nc + (nc - 1 - c)
    small = lambda n: pl.BlockSpec((1, n), lambda b, c: (0, 0))
    wide = pl.BlockSpec((l, d_inner), lambda b, c: (row(b, c), 0))
    return _pallas(
        body, name="ssd_bwd", grid=(nb, nc),
        in_specs=[ANY, wide, wide, wide,
                  pl.BlockSpec((l, gn_cols), lambda b, c: (row(b, c), d_inner // gn_cols)),
                  pl.BlockSpec((l, gn_cols), lambda b, c: (row(b, c), d_inner // gn_cols + 1)),
                  wide, pl.BlockSpec((l, HEAD_PAD), lambda b, c: (row(b, c), 0)),
                  pl.BlockSpec((1, D_STATE, d_inner), lambda b, c: (row(b, c), 0, 0)),
                  small(HEAD_PAD), small(HEAD_PAD), small(d_inner), small(d_inner)] + xch.in_specs,
        out_specs=[wide, pl.BlockSpec((l, n_conv), lambda b, c: (row(b, c), 0)),
                   pl.BlockSpec((l, HEAD_PAD), lambda b, c: (row(b, c), 0)),
                   small(d_inner), small(HEAD_PAD), small(HEAD_PAD), small(HEAD_PAD)] + xch.out_specs,
        out_shape=[jax.ShapeDtypeStruct(dproj.shape, dproj.dtype),
                   jax.ShapeDtypeStruct((nb * s, n_conv), ACT),
                   jax.ShapeDtypeStruct((nb * s, HEAD_PAD), BF16),
                   jax.ShapeDtypeStruct((1, d_inner), F32), jax.ShapeDtypeStruct((1, HEAD_PAD), F32),
                   jax.ShapeDtypeStruct((1, HEAD_PAD), F32), jax.ShapeDtypeStruct((1, HEAD_PAD), F32)]
        + xch.out_shape,
        scratch_shapes=[pltpu.VMEM((D_STATE, d_inner), F32), pltpu.VMEM((8, d_inner), F32)] + xch.scratch,
        input_output_aliases={0: 0},
        compiler_params=_cparams(("arbitrary", "arbitrary"), 56),
    )(dproj, dya3, y, xbc, xbc, xbc, proj, dtraw, prev, dt_bias, a_log, dskip_full, gnorm, *partials)


def _conv_bwd(dproj, dxbc, conv_pre, proj, conv_w, nb, s, off_xbc):
    n_conv = dxbc.shape[1]
    cw = 256

    rc = CONV_ROWS

    def fold(v):
        return jnp.sum(v.reshape(rc // SUBLANES, SUBLANES, cw), axis=0)

    def body(_, d_ref, pre_ref, x_ref, w_ref, dx_ref, dw_ref, db_ref, dcpad_ref):
        @pl.when(pl.program_id(1) == 0)
        def _():
            dw_ref[...] = jnp.zeros_like(dw_ref)
            db_ref[...] = jnp.zeros_like(db_ref)

        dcpad_ref[pl.ds(s, SUBLANES), :] = jnp.zeros((SUBLANES, cw), F32)
        taps = [w_ref[k:k + 1, :] for k in range(CONV_W)]
        db = jnp.zeros((SUBLANES, cw), F32)
        for r in range(s // rc):
            rows = pl.ds(r * rc, rc)
            acc = pre_ref[rows, :].astype(F32)
            sg = _sigmoid(acc)
            dc = d_ref[rows, :].astype(F32) * (sg * (1.0 + acc * (1.0 - sg)))
            dcpad_ref[rows, :] = dc
            db = db + fold(dc)
        db_ref[...] += jnp.sum(db, axis=0, keepdims=True)
        dw = [jnp.zeros((SUBLANES, cw), F32) for _ in range(CONV_W)]
        for r in range(s // rc):
            xv = x_ref[pl.ds(r * rc, rc), :].astype(F32)
            win = dcpad_ref[pl.ds(r * rc, rc + SUBLANES), :]
            dxv = jnp.zeros((rc, cw), F32)
            for j in range(CONV_W):
                up = win[:rc, :] if j == 0 else pltpu.roll(win, rc + SUBLANES - j, 0)[:rc, :]
                dxv = dxv + up * taps[CONV_W - 1 - j]
                dw[CONV_W - 1 - j] = dw[CONV_W - 1 - j] + fold(up * xv)
            dx_ref[pl.ds(r * rc, rc), :] = dxv.astype(dx_ref.dtype)
        for k in range(CONV_W):
            dw_ref[k:k + 1, :] += jnp.sum(dw[k], axis=0, keepdims=True)

    blk = pl.BlockSpec((s, cw), lambda j, b: (b, j))
    return _pallas(
        body, name="conv_bwd", grid=(n_conv // cw, nb),
        in_specs=[ANY, blk, blk, pl.BlockSpec((s, cw), lambda j, b: (b, off_xbc // cw + j)),
                  pl.BlockSpec((CONV_W, cw), lambda j, b: (0, j))],
        out_specs=[pl.BlockSpec((s, cw), lambda j, b: (b, off_xbc // cw + j)),
                   pl.BlockSpec((CONV_W, cw), lambda j, b: (0, j)), pl.BlockSpec((1, cw), lambda j, b: (0, j))],
        out_shape=[jax.ShapeDtypeStruct(dproj.shape, dproj.dtype),
                   jax.ShapeDtypeStruct((CONV_W, n_conv), F32), jax.ShapeDtypeStruct((1, n_conv), F32)],
        scratch_shapes=[pltpu.VMEM((s + SUBLANES, cw), F32)],
        input_output_aliases={0: 0},
        compiler_params=_cparams(("parallel", "arbitrary"), 48),
    )(dproj, dxbc, conv_pre, proj, conv_w)


def _in_bwd(dproj, w_main, ddt, w_dt, x, norm_g, dx1, partials):
    t, d = x.shape
    n_main = dproj.shape[1]
    tm = 256
    ni = t // tm
    xch = _ChipExchange(partials)
    n_in, n_out = 7, 2

    def body(*refs):
        dp_ref, w_ref, ddt_ref, wdt_ref, x_ref, g_ref, dx1_ref = refs[:n_in]
        p_refs = refs[n_in:n_in + xch.n]
        outs = refs[n_in + xch.n:]
        dx_ref, dg_ref = outs[:n_out]
        got_refs = outs[n_out:n_out + xch.n]
        sems = outs[n_out + xch.n:]
        i = pl.program_id(0)

        @pl.when(i == 0)
        def _():
            xch.start(p_refs, got_refs, sems)
            dg_ref[...] = jnp.zeros_like(dg_ref)

        dh = _dot(dp_ref[...], w_ref[...], NT) + _dot(ddt_ref[...], wdt_ref[...], NT)
        xv = x_ref[...]
        r = lax.rsqrt(jnp.mean(xv * xv, axis=-1, keepdims=True) + EPS)
        xh = xv * r
        dg_ref[...] += jnp.sum(dh * xh, axis=0, keepdims=True)
        dg = dh * g_ref[...]
        dx_ref[...] = dx1_ref[...] + r * (dg - xh * jnp.mean(dg * xh, axis=-1, keepdims=True))

        @pl.when(i == ni - 1)
        def _():
            xch.wait(p_refs, got_refs, sems)

    rowblk = pl.BlockSpec((tm, d), lambda i: (i, 0))
    vec = pl.BlockSpec((1, d), lambda i: (0, 0))
    resident = lambda shape: pl.BlockSpec(shape, lambda i: (0, 0), pipeline_mode=pl.Buffered(1))
    return _pallas(
        body, name="in_bwd", grid=(ni,),
        in_specs=[pl.BlockSpec((tm, n_main), lambda i: (i, 0)), resident((d, n_main)),
                  pl.BlockSpec((tm, HEAD_PAD), lambda i: (i, 0)), resident((d, HEAD_PAD)),
                  rowblk, vec, rowblk] + xch.in_specs,
        out_specs=[rowblk, vec] + xch.out_specs,
        out_shape=[jax.ShapeDtypeStruct((t, d), F32), jax.ShapeDtypeStruct((1, d), F32)] + xch.out_shape,
        scratch_shapes=xch.scratch,
        compiler_params=_cparams(("arbitrary",), 56),
    )(dproj, w_main, ddt, w_dt, x, norm_g, dx1, *partials)


def _adamw_math(w, g, m, v):
    m = ADAM_B1 * m + (1.0 - ADAM_B1) * g
    v = ADAM_B2 * v + (1.0 - ADAM_B2) * (g * g)
    m_hat = m / (1.0 - ADAM_B1 ** ADAM_STEP)
    v_hat = v / (1.0 - ADAM_B2 ** ADAM_STEP)
    delta = -ADAM_LR * (m_hat / (jnp.sqrt(v_hat) + ADAM_EPS) + ADAM_WD * w)
    return delta, m, v


def _adamw(parts, w, m, v, name):
    n, rows, cols = parts.shape
    tr = _row_tile(rows, 256) if rows % 8 == 0 else rows

    def body(p_ref, w_ref, m_ref, v_ref, g_ref, d_ref, mo_ref, vo_ref):
        g = p_ref[0].astype(F32)
        for k in range(1, n):
            g = g + p_ref[k].astype(F32)
        d, mn, vn = _adamw_math(w_ref[...], g, m_ref[...], v_ref[...])
        g_ref[...] = g
        d_ref[...] = d
        mo_ref[...] = mn
        vo_ref[...] = vn

    blk = pl.BlockSpec((tr, cols), lambda i: (i, 0))
    shp = jax.ShapeDtypeStruct((rows, cols), F32)
    return _pallas(
        body, name=name, grid=(rows // tr,),
        in_specs=[pl.BlockSpec((n, tr, cols), lambda i: (0, i, 0)), blk, blk, blk],
        out_specs=[blk, blk, blk, blk], out_shape=[shp, shp, shp, shp],
        compiler_params=_cparams(("parallel",), 56),
    )(parts, w, m, v)


def _coords():
    return lax.axis_index("x"), lax.axis_index("y"), lax.axis_index("c")


class _TwoLevelGather:
    def __init__(self, blocks):
        self.n = len(blocks)
        self.in_specs = [ANY] * self.n
        self.out_specs = [ANY] * self.n
        self.out_shape = [jax.ShapeDtypeStruct((N_DEV,) + b.shape, b.dtype) for b in blocks]
        self.scratch = [pltpu.SemaphoreType.DMA((self.n, 7)), pltpu.SemaphoreType.DMA((self.n, 7)),
                        pltpu.SemaphoreType.DMA((self.n,))]

    def _parts(self, x_refs, out_refs, sems):
        send_sems, recv_sems, local_sems = sems
        x, y, c = _coords()
        me, sibling = (x, y, c), (x, y, 1 - c)
        chips = [(1 - x, y), (x, 1 - y), (1 - x, 1 - y)]

        def slot(a, px, py, pc):
            return out_refs[a].at[4 * px + 2 * py + pc]

        def copy(a, k, block, to, src=None):
            return pltpu.make_async_remote_copy(
                src_ref=slot(a, *block) if src is None else src, dst_ref=slot(a, *block),
                send_sem=send_sems.at[a, k], recv_sem=recv_sems.at[a, k], device_id=to, device_id_type=MESH)

        local = [pltpu.make_async_copy(x_refs[a], slot(a, *me), local_sems.at[a]) for a in range(self.n)]
        first = []
        for a in range(self.n):
            first.append(copy(a, 0, me, sibling, src=x_refs[a]))
            first += [copy(a, 1 + j, me, (*chip, c), src=x_refs[a]) for j, chip in enumerate(chips)]
        return copy, local, first, chips, me, sibling, c

    def start(self, x_refs, out_refs, sems):
        _, local, first, *_ = self._parts(x_refs, out_refs, sems)
        for cp in local + first:
            cp.start()

    def finish(self, x_refs, out_refs, sems):
        copy, local, first, chips, me, sibling, c = self._parts(x_refs, out_refs, sems)
        passed = []
        for j, chip in enumerate(chips):
            for a in range(self.n):
                copy(a, 1 + j, (*chip, c), me).wait_recv()
                passed.append(copy(a, 4 + j, (*chip, c), sibling))
                passed[-1].start()
        for a in range(self.n):
            copy(a, 0, sibling, me).wait_recv()
            for j, chip in enumerate(chips):
                copy(a, 4 + j, (*chip, 1 - c), me).wait_recv()
        for cp in first + passed:
            cp.wait_send()
        for cp in local:
            cp.wait()


def _rms_in_and_gather(xin, g, blk):
    rows, cols = blk.shape
    half = rows // 2
    assert half % 16 == 0
    t, d = xin.shape
    tt = 512
    ni = t // tt

    def exchange(x_ref, out_ref, send_sems, recv_sems, local_sem, begin):
        x, y, c = _coords()
        me, sib, xn, yn, dg = (x, y, c), (x, y, 1 - c), (1 - x, y, c), (x, 1 - y, c), (1 - x, 1 - y, c)
        other = lambda p: (p[0], p[1], 1 - c)
        top, bot = pl.ds(0, half), pl.ds(half, half)

        def slot(p, part=None):
            ref = out_ref.at[4 * p[0] + 2 * p[1] + p[2]]
            return ref if part is None else ref.at[part]

        def copy(k, block, to, part=None, src=None):
            return pltpu.make_async_remote_copy(
                src_ref=slot(block, part) if src is None else src, dst_ref=slot(block, part),
                send_sem=send_sems.at[k], recv_sem=recv_sems.at[k], device_id=to, device_id_type=MESH)

        mine = pltpu.make_async_copy(x_ref, slot(me), local_sem)
        own = [copy(1, me, xn, src=x_ref), copy(2, me, yn, src=x_ref), copy(0, me, sib, src=x_ref)]
        if begin:
            for cp in [mine] + own:
                cp.start()
            return
        copy(1, xn, me).wait_recv()
        fwd = [copy(3, xn, yn, top), copy(5, xn, sib)]
        for cp in fwd:
            cp.start()
        copy(2, yn, me).wait_recv()
        fwd += [copy(4, yn, xn, bot), copy(6, yn, sib)]
        for cp in fwd[2:]:
            cp.start()
        copy(3, dg, me, top).wait_recv()
        fwd.append(copy(7, dg, sib, top))
        fwd[-1].start()
        copy(4, dg, me, bot).wait_recv()
        fwd.append(copy(8, dg, sib, bot))
        fwd[-1].start()
        copy(0, sib, me).wait_recv()
        copy(5, other(xn), me).wait_recv()
        copy(6, other(yn), me).wait_recv()
        copy(7, other(dg), me, top).wait_recv()
        copy(8, other(dg), me, bot).wait_recv()
        for cp in own + fwd:
            cp.wait_send()
        mine.wait()

    def body(w_ref, x_ref, g_ref, out_ref, h_ref, send_sems, recv_sems, local_sem):
        i = pl.program_id(0)

        @pl.when(i == 0)
        def _():
            exchange(w_ref, out_ref, send_sems, recv_sems, local_sem, True)

        xv = x_ref[...]
        r = lax.rsqrt(jnp.mean(xv * xv, axis=-1, keepdims=True) + EPS)
        h_ref[...] = (xv * r * g_ref[...]).astype(h_ref.dtype)

        @pl.when(i == ni - 1)
        def _():
            exchange(w_ref, out_ref, send_sems, recv_sems, local_sem, False)

    return _pallas(
        body, name="rms_in_w_in_all_gather", grid=(ni,),
        in_specs=[ANY, pl.BlockSpec((tt, d), lambda i: (i, 0)), pl.BlockSpec((1, d), lambda i: (0, 0))],
        out_specs=[ANY, pl.BlockSpec((tt, d), lambda i: (i, 0))],
        out_shape=[jax.ShapeDtypeStruct((N_DEV, rows, cols), blk.dtype), jax.ShapeDtypeStruct((t, d), BF16)],
        scratch_shapes=[pltpu.SemaphoreType.DMA((9,)), pltpu.SemaphoreType.DMA((9,)), pltpu.SemaphoreType.DMA],
        compiler_params=_cparams(("arbitrary",)),
    )(blk, xin, g)


def _all_gather_direct(blk, name):
    rows, cols = blk.shape

    def body(x_ref, out_ref, send_sems, recv_sems, local_sem):
        x, y, c = _coords()
        me = 4 * x + 2 * y + c
        mine = pltpu.make_async_copy(x_ref, out_ref.at[me], local_sem)
        mine.start()
        copies = []
        for k in range(1, N_DEV):
            fx, fy, fc = (k >> 2) & 1, (k >> 1) & 1, k & 1
            peer = (x ^ fx, y ^ fy, c ^ fc)
            copies.append(pltpu.make_async_remote_copy(
                src_ref=x_ref, dst_ref=out_ref.at[me], send_sem=send_sems.at[k - 1], recv_sem=recv_sems.at[k - 1],
                device_id=peer, device_id_type=MESH))
        for cp in copies:
            cp.start()
        for k in range(1, N_DEV):
            fx, fy, fc = (k >> 2) & 1, (k >> 1) & 1, k & 1
            src = 4 * (x ^ fx) + 2 * (y ^ fy) + (c ^ fc)
            pltpu.make_async_remote_copy(
                src_ref=x_ref, dst_ref=out_ref.at[src], send_sem=send_sems.at[k - 1], recv_sem=recv_sems.at[k - 1],
                device_id=(x, y, c), device_id_type=MESH).wait_recv()
        for cp in copies:
            cp.wait_send()
        mine.wait()

    return _pallas(
        body, name=name, in_specs=[ANY], out_specs=ANY,
        out_shape=jax.ShapeDtypeStruct((N_DEV, rows, cols), blk.dtype),
        scratch_shapes=[pltpu.SemaphoreType.DMA((7,)), pltpu.SemaphoreType.DMA((7,)), pltpu.SemaphoreType.DMA],
    )(blk)


def _pair_exchange(grads, name):
    n = len(grads)

    def body(*refs):
        g_refs, out_refs = refs[:n], refs[n:2 * n]
        send_sems, recv_sems = refs[2 * n:]
        x, y, c = _coords()
        copies = []
        for a in range(n):
            for k in range(4):
                copies.append(pltpu.make_async_remote_copy(
                    src_ref=g_refs[a].at[2 * k + (1 - c)], dst_ref=out_refs[a].at[k],
                    send_sem=send_sems.at[a, k], recv_sem=recv_sems.at[a, k],
                    device_id=(x, y, 1 - c), device_id_type=MESH))
        for cp in copies:
            cp.start()
        for cp in copies:
            cp.wait()

    return _pallas(
        body, name=name, in_specs=[ANY] * n, out_specs=[ANY] * n,
        out_shape=[jax.ShapeDtypeStruct((4,) + g.shape[1:], g.dtype) for g in grads],
        scratch_shapes=[pltpu.SemaphoreType.DMA((n, 4)), pltpu.SemaphoreType.DMA((n, 4))],
    )(*grads)


def _pair_add(grads, got, core, out_dtype, name):
    _, rows, cols = grads.shape
    tr = _row_tile(rows, 256) if rows % 8 == 0 else rows

    def body(c_ref, a_ref, b_ref, o_ref):
        o_ref[...] = (a_ref[...].astype(F32) + b_ref[...].astype(F32)).astype(o_ref.dtype)

    grid_spec = pltpu.PrefetchScalarGridSpec(
        num_scalar_prefetch=1, grid=(4, rows // tr),
        in_specs=[pl.BlockSpec((1, tr, cols), lambda k, i, c_ref: (2 * k + c_ref[0], i, 0)),
                  pl.BlockSpec((1, tr, cols), lambda k, i, c_ref: (k, i, 0))],
        out_specs=pl.BlockSpec((1, tr, cols), lambda k, i, c_ref: (k, i, 0)))
    return _pallas(
        body, name=name, grid_spec=grid_spec,
        out_shape=jax.ShapeDtypeStruct((4, rows, cols), out_dtype),
        compiler_params=_cparams(("parallel", "parallel"), 48),
    )(core, grads, got)


class _ChipExchange:
    def __init__(self, partials):
        self.n = len(partials)
        self.in_specs = [ANY] * self.n
        self.out_specs = [ANY] * self.n
        self.out_shape = [jax.ShapeDtypeStruct(p.shape, p.dtype) for p in partials]
        self.scratch = [pltpu.SemaphoreType.DMA((self.n, 3)), pltpu.SemaphoreType.DMA((self.n, 3)),
                        pltpu.SemaphoreType.DMA((self.n,))]

    def _copies(self, p_refs, out_refs, sems, receiving):
        send_sems, recv_sems, local_sems = sems
        x, y, c = _coords()
        mychip = 2 * x + y
        local = [pltpu.make_async_copy(p_refs[a].at[mychip], out_refs[a].at[mychip], local_sems.at[a])
                 for a in range(self.n)]
        remote = []
        for a in range(self.n):
            for k in range(1, 4):
                px, py = x ^ ((k >> 1) & 1), y ^ (k & 1)
                peer = 2 * px + py
                remote.append(pltpu.make_async_remote_copy(
                    src_ref=p_refs[a].at[peer], dst_ref=out_refs[a].at[peer if receiving else mychip],
                    send_sem=send_sems.at[a, k - 1], recv_sem=recv_sems.at[a, k - 1],
                    device_id=(x, y, c) if receiving else (px, py, c), device_id_type=MESH))
        return local, remote

    def start(self, p_refs, out_refs, sems):
        local, sends = self._copies(p_refs, out_refs, sems, False)
        for cp in local + sends:
            cp.start()

    def wait(self, p_refs, out_refs, sems):
        local, recvs = self._copies(p_refs, out_refs, sems, True)
        for cp in recvs:
            cp.wait_recv()
        for cp in recvs:
            cp.wait_send()
        for cp in local:
            cp.wait()


class _DirectExchange:
    def __init__(self, partials):
        self.n = len(partials)
        self.in_specs = [ANY] * self.n
        self.out_specs = [ANY] * self.n
        self.out_shape = [jax.ShapeDtypeStruct(p.shape, p.dtype) for p in partials]
        self.scratch = [pltpu.SemaphoreType.DMA((self.n, N_DEV - 1)), pltpu.SemaphoreType.DMA((self.n, N_DEV - 1)),
                        pltpu.SemaphoreType.DMA((self.n,))]

    def _copies(self, p_refs, out_refs, sems, receiving):
        send_sems, recv_sems, local_sems = sems
        x, y, c = _coords()
        me = 4 * x + 2 * y + c
        local = [pltpu.make_async_copy(p_refs[a].at[me], out_refs[a].at[me], local_sems.at[a])
                 for a in range(self.n)]
        remote = []
        for a in range(self.n):
            for k in range(1, N_DEV):
                px, py, pc = x ^ ((k >> 2) & 1), y ^ ((k >> 1) & 1), c ^ (k & 1)
                peer = 4 * px + 2 * py + pc
                remote.append(pltpu.make_async_remote_copy(
                    src_ref=p_refs[a].at[peer], dst_ref=out_refs[a].at[peer if receiving else me],
                    send_sem=send_sems.at[a, k - 1], recv_sem=recv_sems.at[a, k - 1],
                    device_id=(x, y, c) if receiving else (px, py, pc), device_id_type=MESH))
        return local, remote

    def start(self, p_refs, out_refs, sems):
        local, sends = self._copies(p_refs, out_refs, sems, False)
        for cp in local + sends:
            cp.start()

    def wait(self, p_refs, out_refs, sems):
        local, recvs = self._copies(p_refs, out_refs, sems, True)
        for cp in recvs:
            cp.wait_recv()
        for cp in recvs:
            cp.wait_send()
        for cp in local:
            cp.wait()


def _pad_lanes(v, n=HEAD_PAD):
    return jnp.pad(v, ((0, 0), (0, n - v.shape[1])))


def kernel(x, p, norm_g, w_in, conv_w, conv_b, dt_bias, a_log, d_skip, gnorm_g, pool_mix_w, pool_mix_b, pool_scale, w_branch_a, w_branch_b, w_out, ple_norm_g, w_ple_gate, w_ple_up, final_g, loss_target, m_norm_g, m_w_in, m_conv_w, m_conv_b, m_dt_bias, m_a_log, m_d_skip, m_gnorm_g, m_pool_mix_w, m_pool_mix_b, m_pool_scale, m_w_branch_a, m_w_branch_b, m_w_out, m_ple_norm_g, m_w_ple_gate, m_w_ple_up, m_final_g, v_norm_g, v_w_in, v_conv_w, v_conv_b, v_dt_bias, v_a_log, v_d_skip, v_gnorm_g, v_pool_mix_w, v_pool_mix_b, v_pool_scale, v_w_branch_a, v_w_branch_b, v_w_out, v_ple_norm_g, v_w_ple_gate, v_w_ple_up, v_final_g):
    nb, s, d = x.shape
    t = nb * s
    n_heads = dt_bias.shape[1]
    d_inner = n_heads * HEAD_DIM
    n_conv = conv_b.shape[1]
    d_pool = pool_mix_b.shape[1]
    n_in = w_in.shape[2] * N_DEV
    ple_dim = p.shape[-1]
    o_xbc, o_dt = d_inner, d_inner + n_conv
    o_u = o_dt + n_heads
    o_zp = o_u + d_pool
    o_ga, o_gb = o_zp + d_pool, o_zp + d_pool + d
    n_main = n_in - n_heads
    off_ga, off_gb, off_xbc = d_inner, d_inner + d, d_inner + 2 * d
    off_u = off_xbc + n_conv
    off_zp = off_u + d_pool

    sharded = [w_in[0], conv_w[0], pool_mix_w[0], w_branch_a[0], w_branch_b[0], w_out[0], w_ple_gate[0], w_ple_up[0]]
    sharded_m = [m_w_in[0], m_conv_w[0], m_pool_mix_w[0], m_w_branch_a[0], m_w_branch_b[0], m_w_out[0],
                 m_w_ple_gate[0], m_w_ple_up[0]]
    sharded_v = [v_w_in[0], v_conv_w[0], v_pool_mix_w[0], v_w_branch_a[0], v_w_branch_b[0], v_w_out[0],
                 v_w_ple_gate[0], v_w_ple_up[0]]
    ng = pool_mix_w.shape[1]
    pg = d_pool // ng
    two_d = lambda w: w.reshape(-1, w.shape[-1])
    sharded, sharded_m, sharded_v = ([two_d(w) for w in ws] for ws in (sharded, sharded_m, sharded_v))

    send = [w if i == 1 else w.astype(BF16) for i, w in enumerate(sharded)]
    x2 = x.reshape(t, d)
    chunk = n_in // N_DEV
    segments = [((0, o_xbc), 0), ((o_ga, o_gb), off_ga), ((o_gb, n_in), off_gb), ((o_xbc, o_dt), off_xbc),
                ((o_u, o_zp), off_u), ((o_zp, o_ga), off_zp)]

    def pieces_of(dev):
        lo, hi = dev * chunk, (dev + 1) * chunk
        out = []
        for (olo, ohi), off in sorted(segments + [((o_dt, o_u), None)], key=lambda e: e[0][0]):
            a, b = max(lo, olo), min(hi, ohi)
            if a < b:
                out.append((a - lo, b - lo, None if off is None else off + a - olo))
        return out

    def lane_cover(a, b, off):
        left = 0 if off is None else off % LANES
        return left, -(-(left + b - a) // LANES) * LANES

    send_width = max(sum(lane_cover(*pc)[1] for pc in pieces_of(dev)) for dev in range(N_DEV))

    def aligned_shard(dev):
        def build(w):
            bufs = []
            for a, b, off in pieces_of(dev):
                left, width = lane_cover(a, b, off)
                bufs.append(jnp.pad(w[:, a:b], ((0, 0), (left, width - left - (b - a)))))
            out = jnp.concatenate(bufs, axis=1)
            return jnp.pad(out, ((0, 0), (0, send_width - out.shape[1])))
        return build

    me_lin = 4 * lax.axis_index("x") + 2 * lax.axis_index("y") + lax.axis_index("c")
    w_in_sent = lax.switch(me_lin, [aligned_shard(dev) for dev in range(N_DEV)], send[0])
    w_in_all, h = _rms_in_and_gather(x2, norm_g, w_in_sent)
    seg = [w_in_all]
    w_main = jnp.zeros((d, n_main), BF16)
    w_dt = None
    for dev in range(N_DEV):
        q = 0
        for a, b, off in pieces_of(dev):
            left, width = lane_cover(a, b, off)
            cover = w_in_all[dev, :, q:q + width]
            q += width
            if off is None:
                w_dt = cover
            else:
                start = off - left
                w_main = w_main + jnp.pad(cover, ((0, 0), (start, n_main - start - width)))

    proj, dtraw, *rest = _in_proj(h, w_main, w_dt, send[1:])
    seg += rest
    cw_full = seg[1].transpose(1, 0, 2).reshape(CONV_W, n_conv)
    mixw_full = seg[2].reshape(N_DEV, ng, pg // N_DEV, pg).transpose(1, 0, 2, 3).reshape(ng, pg, pg)
    wa_full = seg[3].reshape(d_inner, d)
    wb_full = seg[4].reshape(d_pool, d)
    wout_full = seg[5].reshape(d, d)
    wg_full = seg[6].reshape(d, d)
    wup_full = seg[7].transpose(1, 0, 2).reshape(ple_dim, d)

    p2 = p.reshape(t, ple_dim)
    tgt = loss_target.reshape(t, d)
    bias_p, alog_p = _pad_lanes(dt_bias), _pad_lanes(a_log)
    dskip_full = jnp.repeat(d_skip, HEAD_DIM, axis=1)

    xbc, conv_pre = _conv_fwd(proj, cw_full, conv_b, nb, s, off_xbc, n_conv)
    y, ya3, prev = _ssd_fwd(xbc, proj, dtraw, bias_p, alog_p, dskip_full, gnorm_g, nb, s, d_inner)
    pooled, mixed, yb = _pool_fwd(proj, mixw_full, pool_mix_b, pool_scale, nb, s, off_u, off_zp, d_pool)
    (merged, h2, dx1, dx1b, dgl, dpu, dpa, dpb, dproj, loss_part, dfinal_g, d_ple_g) = _merge_to_grads(
        ya3, wa_full, yb, wb_full, proj, off_ga, off_gb, wout_full, x2, ple_norm_g, wg_full, p2, wup_full, tgt,
        final_g.reshape(1, d), n_main)

    g_wup = _matmul(p2, dpu, "tn", BF16, "dw_ple_up", 256, 512, t)
    g_wg = _matmul(h2, dgl, "tn", BF16, "dw_ple_gate", 512, 1024, t)
    g_wout = _matmul(merged, dx1b, "tn", BF16, "dw_out", 512, 1024, t)
    g_wa = _matmul(ya3, dpa, "tn", BF16, "dw_branch_a", 512, 1024, t)
    g_wb = _matmul(yb, dpb, "tn", BF16, "dw_branch_b", 512, 1024, t)
    dya3 = _matmul(dpa, wa_full, "nt", ACT, "d_ya3", 1024, 1024, d)
    dproj, dmixed, d_scale, d_mixb = _yb_bwd(dproj, dpb, wb_full, proj, mixed, pool_mix_b, pool_scale, off_zp)
    g_mixw = _dmixw(pooled, dmixed, ng)
    dproj = _pool_bwd(dproj, dmixed, mixw_full, nb, s, off_u)

    core = lax.axis_index("c").astype(I32).reshape(1)

    def pair_sums(by_dest, names):
        by_dest = [g.astype(BF16) for g in by_dest]
        got = _pair_exchange(by_dest, "rs_pair_exchange_" + names[0])
        return [_pair_add(g, r, core, BF16, "rs_pair_add_" + nm) for g, r, nm in zip(by_dest, got, names)]

    names_a = ["pool_mix_w", "w_branch_a", "w_branch_b", "w_out", "w_ple_gate", "w_ple_up"]
    partials_a = [
        g_mixw.reshape(ng, N_DEV, pg // N_DEV, pg).transpose(1, 0, 2, 3).reshape(N_DEV, ng * pg // N_DEV, pg)
        .astype(BF16),
        g_wa.reshape(N_DEV, d_inner // N_DEV, d), g_wb.reshape(N_DEV, d_pool // N_DEV, d),
        g_wout.reshape(N_DEV, d // N_DEV, d), g_wg.reshape(N_DEV, d // N_DEV, d),
        g_wup.reshape(ple_dim, N_DEV, d // N_DEV).transpose(1, 0, 2)]
    dproj, dxbc, ddt, d_gn, d_dsk, d_alog, d_dtb, *parts_a = _ssd_bwd(
        dproj, dya3, y, xbc, proj, dtraw, prev, bias_p, alog_p, dskip_full, gnorm_g, nb, s, d_inner, partials_a)
    dproj, g_cw, d_cb = _conv_bwd(dproj, dxbc, conv_pre, proj, cw_full, nb, s, off_xbc)
    g_wmain, g_wdt = _dw_in(h, dproj, ddt)

    def dest_cols(dev):
        lo, hi = dev * chunk, (dev + 1) * chunk
        srcs = [((olo, ohi), g_wmain, off) for (olo, ohi), off in segments] + [((o_dt, o_u), g_wdt, 0)]
        out = []
        for (olo, ohi), arr, off in sorted(srcs, key=lambda e: e[0][0]):
            a, b = max(lo, olo), min(hi, ohi)
            if a < b:
                out.append(arr[:, off + a - olo:off + b - olo])
        return jnp.concatenate(out, axis=1)

    names_b = ["w_in", "conv_w"]
    partials_b = pair_sums([jnp.stack([dest_cols(dev) for dev in range(N_DEV)], axis=0),
                            g_cw.reshape(CONV_W, N_DEV, n_conv // N_DEV).transpose(1, 0, 2)], names_b)
    grad_x, d_ng, *parts_b = _in_bwd(dproj, w_main, ddt, w_dt, x2, norm_g, dx1, partials_b)

    names = names_b + names_a
    parts = parts_b + parts_a
    sh_g, sh_d, sh_m, sh_v = [], [], [], []
    shapes = [w_in.shape, conv_w.shape, pool_mix_w.shape, w_branch_a.shape, w_branch_b.shape, w_out.shape,
              w_ple_gate.shape, w_ple_up.shape]
    for pt, w, m, v, nm, shp in zip(parts, sharded, sharded_m, sharded_v, names, shapes):
        res = _adamw(pt, w, m, v, "adamw_" + nm)
        for dst, r in zip((sh_g, sh_d, sh_m, sh_v), res):
            dst.append(r.reshape(shp))

    rep_w = [norm_g, conv_b, gnorm_g, pool_mix_b, pool_scale, ple_norm_g, final_g.reshape(1, d),
             bias_p, alog_p, _pad_lanes(d_skip)]
    rep_m = [m_norm_g, m_conv_b, m_gnorm_g, m_pool_mix_b, m_pool_scale, m_ple_norm_g, m_final_g.reshape(1, d),
             _pad_lanes(m_dt_bias), _pad_lanes(m_a_log), _pad_lanes(m_d_skip)]
    rep_v = [v_norm_g, v_conv_b, v_gnorm_g, v_pool_mix_b, v_pool_scale, v_ple_norm_g, v_final_g.reshape(1, d),
             _pad_lanes(v_dt_bias), _pad_lanes(v_a_log), _pad_lanes(v_d_skip)]
    rep_g = [d_ng, d_cb, d_gn, d_mixb, d_scale, d_ple_g, dfinal_g, d_dtb, d_alog, d_dsk]
    zero = jnp.zeros((1, LANES), F32)
    small = _all_gather_direct(jnp.concatenate(rep_g + [loss_part], axis=1), "small_all_gather")
    pk = lambda ws: jnp.concatenate(ws + [zero], axis=1)
    rg, rd, rm, rv = _adamw(small, pk(rep_w), pk(rep_m), pk(rep_v), "adamw_replicated")
    loss = rg[0, sum(int(w.size) for w in rep_w)]

    def unpack_rep(buf):
        out, pos_ = [], 0
        for w in rep_w:
            out.append(buf[:, pos_:pos_ + w.size])
            pos_ += w.size
        return out

    def arrange(sh, rep):
        rep = list(rep)
        rep[6] = rep[6].reshape(d)
        for i in (7, 8, 9):
            rep[i] = rep[i][:, :n_heads]
        return [rep[0], sh[0], sh[1], rep[1], rep[7], rep[8], rep[9], rep[2], sh[2], rep[3], rep[4],
                sh[3], sh[4], sh[5], rep[5], sh[6], sh[7], rep[6]]

    outs = [loss, grad_x.reshape(nb, s, d)]
    for sh, rep in ((sh_g, rg), (sh_d, rd), (sh_m, rm), (sh_v, rv)):
        outs += arrange(sh, unpack_rep(rep))
    return tuple(outs)
```

```python
import jax
import jax.numpy as jnp
from jax import lax
from jax.experimental import pallas as pl
from jax.experimental.pallas import tpu as pltpu

F32 = jnp.float32
BF16 = jnp.bfloat16
I32 = jnp.int32
MESH = pl.DeviceIdType.MESH
ANY = pl.BlockSpec(memory_space=pl.ANY)

EPS = 1e-6
CHUNK = 128
HEAD_DIM = 64
D_STATE = 128
GROUPS = 4
CONV_W = 4
POOL_WINDOWS = (2, 4, 8, 16)
ACT = jnp.bfloat16
LANES = 128
SUBLANES = 8
CONV_ROWS = 64
HEAD_PAD = 128
NEG = -1e30
N_DEV = 8

ADAM_LR = 0.001
ADAM_B1 = 0.9
ADAM_B2 = 0.999
ADAM_EPS = 1e-08
ADAM_WD = 0.01
ADAM_STEP = 10

NN = (((1,), (0,)), ((), ()))
NT = (((1,), (1,)), ((), ()))
TN = (((0,), (0,)), ((), ()))


def _pallas(body, **kw):
    return pl.pallas_call(body, **kw)


def _cparams(dims=None, vmem_mb=None):
    kw = {}
    if dims is not None:
        kw["dimension_semantics"] = dims
    if vmem_mb is not None:
        kw["vmem_limit_bytes"] = vmem_mb << 20
    return pltpu.CompilerParams(**kw)


def _dot(a, b, dims=NN):
    return lax.dot_general(a, b, dims, preferred_element_type=F32)


def _split3(v):
    hi = v.astype(BF16)
    r = v - hi.astype(F32)
    mid = r.astype(BF16)
    lo = (r - mid.astype(F32)).astype(BF16)
    return hi, mid, lo


def _exact_r(v, m, dims=NN):
    hi, mid, lo = _split3(v)
    return _dot(hi, m, dims) + _dot(mid, m, dims) + _dot(lo, m, dims)


def _exact_l(m, v, dims=NN):
    hi, mid, lo = _split3(v)
    return _dot(m, hi, dims) + _dot(m, mid, dims) + _dot(m, lo, dims)


def _row_tile(rows, target):
    best = None
    for cand in range(8, min(rows, target) + 1, 8):
        if rows % cand == 0:
            best = cand
    assert best is not None, rows
    return best


def _sum_all(v):
    return jnp.sum(jnp.sum(v, axis=1, keepdims=True), axis=0, keepdims=True)


def _iota(shape, dim):
    return lax.broadcasted_iota(I32, shape, dim)


def _sigmoid(x):
    return 0.5 * jnp.tanh(0.5 * x) + 0.5


def _softplus(x):
    return jnp.maximum(x, 0.0) + jnp.log(1.0 + jnp.exp(-jnp.abs(x)))


def _shift_down(v, j, row):
    return jnp.where(row >= j, pltpu.roll(v, j, 0), 0.0)


def _shift_up(v, j, row):
    n = v.shape[0]
    return jnp.where(row < n - j, pltpu.roll(v, n - j, 0), 0.0)


def _matmul(a, b, mode, out_dtype, name, tm, tn, tk):
    if mode == "nn":
        (m, k), n = a.shape, b.shape[1]
    elif mode == "nt":
        (m, k), n = a.shape, b.shape[0]
    else:
        (k, m), n = a.shape, b.shape[1]
    tm, tn, tk = min(tm, m), min(tn, n), min(tk, k)
    assert m % tm == 0 and n % tn == 0 and k % tk == 0, (name, m, n, k)
    nk = k // tk
    dims = {"nn": NN, "nt": NT, "tn": TN}[mode]

    def body_acc(a_ref, b_ref, o_ref, acc_ref):
        kk = pl.program_id(2)

        @pl.when(kk == 0)
        def _():
            acc_ref[...] = jnp.zeros_like(acc_ref)

        acc_ref[...] += _dot(a_ref[...].astype(BF16), b_ref[...].astype(BF16), dims)

        @pl.when(kk == nk - 1)
        def _():
            o_ref[...] = acc_ref[...].astype(o_ref.dtype)

    def body_one(a_ref, b_ref, o_ref):
        o_ref[...] = _dot(a_ref[...].astype(BF16), b_ref[...].astype(BF16), dims).astype(o_ref.dtype)

    body = body_one if nk == 1 else body_acc
    if mode == "tn":
        a_spec = pl.BlockSpec((tk, tm), lambda i, j, kk: (kk, i))
    else:
        a_spec = pl.BlockSpec((tm, tk), lambda i, j, kk: (i, kk))
    if mode == "nt":
        b_spec = pl.BlockSpec((tn, tk), lambda i, j, kk: (j, kk))
    else:
        b_spec = pl.BlockSpec((tk, tn), lambda i, j, kk: (kk, j))
    return _pallas(
        body, name=name, grid=(m // tm, n // tn, nk),
        in_specs=[a_spec, b_spec],
        out_specs=pl.BlockSpec((tm, tn), lambda i, j, kk: (i, j)),
        out_shape=jax.ShapeDtypeStruct((m, n), out_dtype),
        scratch_shapes=[] if nk == 1 else [pltpu.VMEM((tm, tn), F32)],
        compiler_params=_cparams(("parallel", "parallel", "arbitrary"), 56),
    )(a, b)


def _in_proj(h, w_main, w_dt, blocks):
    t, d = h.shape
    n = w_main.shape[1]
    tm, tn = min(1024, t), 1024
    ni, nj = t // tm, n // tn
    gather = _TwoLevelGather(blocks)

    def body(*refs):
        a_ref, b_ref, wdt_ref = refs[:3]
        x_refs = refs[3:3 + gather.n]
        o_ref, dt_ref = refs[3 + gather.n:5 + gather.n]
        got_refs = refs[5 + gather.n:5 + 2 * gather.n]
        sems = refs[5 + 2 * gather.n:]
        i, j = pl.program_id(0), pl.program_id(1)

        @pl.when((i == 0) & (j == 0))
        def _():
            gather.start(x_refs, got_refs, sems)

        o_ref[...] = _dot(a_ref[...], b_ref[...]).astype(o_ref.dtype)

        @pl.when(j == 0)
        def _():
            dt_ref[...] = _dot(a_ref[...], wdt_ref[...])

        @pl.when((i == ni - 1) & (j == max(nj - 5, 0)))
        def _():
            gather.forward(x_refs, got_refs, sems)

        @pl.when((i == ni - 1) & (j == nj - 1))
        def _():
            gather.finish(x_refs, got_refs, sems)

    return _pallas(
        body, name="in_proj", grid=(ni, nj),
        in_specs=[pl.BlockSpec((tm, d), lambda i, j: (i, 0)), pl.BlockSpec((d, tn), lambda i, j: (0, j)),
                  pl.BlockSpec((d, HEAD_PAD), lambda i, j: (0, 0))] + gather.in_specs,
        out_specs=[pl.BlockSpec((tm, tn), lambda i, j: (i, j)), pl.BlockSpec((tm, HEAD_PAD), lambda i, j: (i, 0))]
        + gather.out_specs,
        out_shape=[jax.ShapeDtypeStruct((t, n), ACT), jax.ShapeDtypeStruct((t, HEAD_PAD), F32)] + gather.out_shape,
        scratch_shapes=gather.scratch,
        compiler_params=_cparams(("arbitrary", "arbitrary"), 56),
    )(h, w_main, w_dt, *blocks)


def _dw_in(h, dproj, ddt):
    t, d = h.shape
    n = dproj.shape[1]
    tm, tn = 512, 1024

    def body(a_ref, b_ref, ddt_ref, o_ref, odt_ref):
        a = a_ref[...]
        o_ref[...] = _dot(a, b_ref[...], TN).astype(o_ref.dtype)

        @pl.when(pl.program_id(1) == 0)
        def _():
            odt_ref[...] = _dot(a, ddt_ref[...], TN).astype(odt_ref.dtype)

    return _pallas(
        body, name="dw_in", grid=(d // tm, n // tn),
        in_specs=[pl.BlockSpec((t, tm), lambda i, j: (0, i)), pl.BlockSpec((t, tn), lambda i, j: (0, j)),
                  pl.BlockSpec((t, HEAD_PAD), lambda i, j: (0, 0))],
        out_specs=[pl.BlockSpec((tm, tn), lambda i, j: (i, j)), pl.BlockSpec((tm, HEAD_PAD), lambda i, j: (i, 0))],
        out_shape=[jax.ShapeDtypeStruct((d, n), BF16), jax.ShapeDtypeStruct((d, HEAD_PAD), BF16)],
        compiler_params=_cparams(("parallel", "arbitrary"), 56),
    )(h, dproj, ddt)


def _conv_fwd(proj, conv_w, conv_b, nb, s, off_xbc, n_conv):
    cw = 256
    rc = CONV_ROWS

    def body(x_ref, w_ref, b_ref, o_ref, pre_ref, pad_ref):
        pad_ref[0:SUBLANES, :] = jnp.zeros((SUBLANES, cw), F32)
        pad_ref[SUBLANES:, :] = x_ref[...].astype(F32)
        taps = [w_ref[k:k + 1, :] for k in range(CONV_W)]
        bias = b_ref[...]
        for r in range(s // rc):
            win = pad_ref[pl.ds(r * rc, rc + SUBLANES), :]
            acc = bias + win[SUBLANES:, :] * taps[CONV_W - 1]
            for j in range(1, CONV_W):
                acc = acc + pltpu.roll(win, j, 0)[SUBLANES:, :] * taps[CONV_W - 1 - j]
            pre_ref[pl.ds(r * rc, rc), :] = acc.astype(pre_ref.dtype)
            o_ref[pl.ds(r * rc, rc), :] = (acc * _sigmoid(acc)).astype(o_ref.dtype)

    out = pl.BlockSpec((s, cw), lambda b, j: (b, j))
    return _pallas(
        body, name="conv_fwd", grid=(nb, n_conv // cw),
        in_specs=[pl.BlockSpec((s, cw), lambda b, j: (b, off_xbc // cw + j)),
                  pl.BlockSpec((CONV_W, cw), lambda b, j: (0, j)),
                  pl.BlockSpec((1, cw), lambda b, j: (0, j))],
        out_specs=[out, out],
        out_shape=[jax.ShapeDtypeStruct((nb * s, n_conv), ACT), jax.ShapeDtypeStruct((nb * s, n_conv), ACT)],
        scratch_shapes=[pltpu.VMEM((s + SUBLANES, cw), F32)],
        compiler_params=_cparams(("parallel", "parallel"), 48),
    )(proj, conv_w, conv_b)


def _ssd_common(dtraw, bias, alog, n_heads):
    l = dtraw.shape[0]
    lane = _iota((l, HEAD_PAD), 1)
    lane1 = _iota((1, HEAD_PAD), 1)
    apad = jnp.where(lane1 < n_heads, -jnp.exp(alog), 0.0)
    dt = jnp.where(lane < n_heads, _softplus(dtraw + bias), 0.0)
    a = dt * apad
    tri = (_iota((l, l), 1) <= _iota((l, l), 0)).astype(BF16)
    acs = _exact_l(tri, a)
    width = n_heads * HEAD_DIM
    ex = (_iota((HEAD_PAD, width), 1) // HEAD_DIM == _iota((HEAD_PAD, width), 0)).astype(BF16)
    dt_full = _exact_r(dt, ex)
    acs_full = _exact_r(acs, ex)
    return dt, apad, acs, dt_full, acs_full, tri, ex


def _decay_mats(acs, acs_t, h, mask):
    col = acs[:, h:h + 1]
    row = acs_t[h:h + 1, :]
    return jnp.exp(jnp.where(mask, col - row, NEG))


def _ssd_fwd(xbc, proj, dtraw, dt_bias, a_log, dskip_full, gnorm, nb, s, d_inner):
    l = CHUNK
    nc = s // l
    n_heads = d_inner // HEAD_DIM
    gw = d_inner // GROUPS
    pairs_per_group = gw // LANES
    gn_cols = GROUPS * D_STATE

    def body(xs_ref, bm_ref, cm_ref, z_ref, dtr_ref, bias_ref, alog_ref, dsk_ref, gn_ref,
             y_ref, ya3_ref, prev_ref, state_ref):
        @pl.when(pl.program_id(1) == 0)
        def _():
            state_ref[...] = jnp.zeros_like(state_ref)

        xs = xs_ref[...].astype(F32)
        dt, apad, acs, dt_full, acs_full, tri, ex = _ssd_common(
            dtr_ref[...], bias_ref[...], alog_ref[...], n_heads)
        acs_t = acs.T
        e_full = jnp.exp(acs_full)
        last = acs_full[l - 1:l, :]
        dte = jnp.exp(last - acs_full)
        cd = jnp.exp(last)
        xdt = xs * dt_full
        xd = xdt * dte
        mask = _iota((l, l), 1) <= _iota((l, l), 0)
        lane = _iota((l, LANES), 1)
        prev_ref[0] = state_ref[...]
        for g in range(GROUPS):
            gsl = slice(g * gw, (g + 1) * gw)
            bg = bm_ref[:, g * D_STATE:(g + 1) * D_STATE].astype(BF16)
            cg = cm_ref[:, g * D_STATE:(g + 1) * D_STATE].astype(BF16)
            cb = _dot(cg, bg, NT)
            outs = []
            for q in range(pairs_per_group):
                pair = g * pairs_per_group + q
                h0, h1 = 2 * pair, 2 * pair + 1
                xpb = xdt[:, pair * LANES:(pair + 1) * LANES].astype(BF16)
                m0 = (cb * _decay_mats(acs, acs_t, h0, mask)).astype(BF16)
                m1 = (cb * _decay_mats(acs, acs_t, h1, mask)).astype(BF16)
                zero = jnp.zeros_like(xpb)
                rhs = jnp.concatenate([jnp.where(lane < HEAD_DIM, xpb, zero),
                                       jnp.where(lane >= HEAD_DIM, xpb, zero)], axis=0)
                outs.append(_dot(jnp.concatenate([m0, m1], axis=1), rhs))
            prev_g = state_ref[:, gsl]
            y_off = _dot(cg, prev_g.astype(BF16)) * e_full[:, gsl]
            y_g = jnp.concatenate(outs, axis=1) + y_off
            y_ref[:, gsl] = y_g.astype(y_ref.dtype)
            st = _dot(bg, xd[:, gsl].astype(BF16), TN)
            state_ref[:, gsl] = prev_g * cd[:, gsl] + st
            zv = z_ref[:, gsl].astype(F32)
            blk = (y_g + xs[:, gsl] * dsk_ref[:, gsl]) * (zv * _sigmoid(zv))
            rg = lax.rsqrt(jnp.mean(blk * blk, axis=-1, keepdims=True) + EPS)
            ya3_ref[:, gsl] = (blk * rg * gn_ref[:, gsl]).astype(ya3_ref.dtype)

    row = lambda b, c: b * nc + c
    small = lambda n: pl.BlockSpec((1, n), lambda b, c: (0, 0))
    return _pallas(
        body, name="ssd_fwd", grid=(nb, nc),
        in_specs=[pl.BlockSpec((l, d_inner), lambda b, c: (row(b, c), 0)),
                  pl.BlockSpec((l, gn_cols), lambda b, c: (row(b, c), d_inner // gn_cols)),
                  pl.BlockSpec((l, gn_cols), lambda b, c: (row(b, c), d_inner // gn_cols + 1)),
                  pl.BlockSpec((l, d_inner), lambda b, c: (row(b, c), 0)),
                  pl.BlockSpec((l, HEAD_PAD), lambda b, c: (row(b, c), 0)),
                  small(HEAD_PAD), small(HEAD_PAD), small(d_inner), small(d_inner)],
        out_specs=[pl.BlockSpec((l, d_inner), lambda b, c: (row(b, c), 0)),
                   pl.BlockSpec((l, d_inner), lambda b, c: (row(b, c), 0)),
                   pl.BlockSpec((1, D_STATE, d_inner), lambda b, c: (row(b, c), 0, 0))],
        out_shape=[jax.ShapeDtypeStruct((nb * s, d_inner), ACT),
                   jax.ShapeDtypeStruct((nb * s, d_inner), BF16),
                   jax.ShapeDtypeStruct((nb * nc, D_STATE, d_inner), F32)],
        scratch_shapes=[pltpu.VMEM((D_STATE, d_inner), F32)],
        compiler_params=_cparams(("parallel", "arbitrary"), 48),
    )(xbc, xbc, xbc, proj, dtraw, dt_bias, a_log, dskip_full, gnorm)


def _pool_fwd(proj, mixw, mix_b, scale, nb, s, off_u, off_zp, d_pool):
    pg = d_pool // len(POOL_WINDOWS)

    def body(u_ref, zp_ref, w_ref, b_ref, s_ref, pooled_ref, mixed_ref, yb_ref):
        g = pl.program_id(1)
        uv = u_ref[...].astype(F32)
        row = _iota(uv.shape, 0)
        pos = (_iota((uv.shape[0], 1), 0) + 1).astype(F32)
        for gi, w in enumerate(POOL_WINDOWS):
            @pl.when(g == gi)
            def _():
                acc, span = uv, 1
                while span < w:
                    acc = acc + _shift_down(acc, span, row)
                    span *= 2
                pooled_ref[...] = (acc / jnp.minimum(pos, float(w)) - uv).astype(pooled_ref.dtype)

        mixed = _dot(pooled_ref[...], w_ref[0])
        mixed_ref[...] = mixed.astype(mixed_ref.dtype)
        zp = zp_ref[...].astype(F32)
        yb_ref[...] = ((mixed + b_ref[...]) * s_ref[...] * (zp * _sigmoid(zp))).astype(yb_ref.dtype)

    seg = lambda off: pl.BlockSpec((s, pg), lambda b, g: (b, off // pg + g))
    vec = pl.BlockSpec((1, pg), lambda b, g: (0, g))
    return _pallas(
        body, name="pool_fwd", grid=(nb, len(POOL_WINDOWS)),
        in_specs=[seg(off_u), seg(off_zp), pl.BlockSpec((1, pg, pg), lambda b, g: (g, 0, 0)), vec, vec],
        out_specs=[seg(0), seg(0), seg(0)],
        out_shape=[jax.ShapeDtypeStruct((nb * s, d_pool), BF16), jax.ShapeDtypeStruct((nb * s, d_pool), ACT),
                   jax.ShapeDtypeStruct((nb * s, d_pool), BF16)],
        compiler_params=_cparams(("parallel", "parallel"), 48),
    )(proj, proj, mixw, mix_b, scale)


def _merge_to_grads(ya3, wa, yb, wb, proj, off_ga, off_gb, wout, x, ple_g, wg, p, wup, target, final_g, n_main):
    t, d = x.shape
    pd = p.shape[1]
    tm = 256
    assert off_gb == off_ga + d and off_ga % (2 * d) == 0

    def body(a_ref, wa_ref, b_ref, wb_ref, ga_ref, gb_ref, wout_ref, x_ref, pg_ref, wg_ref, p_ref, wup_ref,
             tg_ref, fg_ref,
             mg_ref, h2_ref, dx1_ref, dx1b_ref, dgl_ref, dpu_ref, dpa_ref, dpb_ref, dgate_ref,
             loss_ref, dfg_ref, dpg_ref):
        @pl.when(pl.program_id(0) == 0)
        def _():
            loss_ref[...] = jnp.zeros_like(loss_ref)
            dfg_ref[...] = jnp.zeros_like(dfg_ref)
            dpg_ref[...] = jnp.zeros_like(dpg_ref)

        pa = _dot(a_ref[...], wa_ref[...])
        pb = _dot(b_ref[...], wb_ref[...])
        sa = _sigmoid(ga_ref[...].astype(F32))
        sb = _sigmoid(gb_ref[...].astype(F32))
        merged = (sa * pa + sb * pb).astype(BF16)
        mg_ref[...] = merged
        x1 = x_ref[...] + _dot(merged, wout_ref[...])
        r1 = lax.rsqrt(jnp.mean(x1 * x1, axis=-1, keepdims=True) + EPS)
        xh1 = x1 * r1
        pg = pg_ref[...]
        h2 = (xh1 * pg).astype(BF16)
        h2_ref[...] = h2
        gate = _sigmoid(_dot(h2, wg_ref[...]))
        pu = _dot(p_ref[...].astype(BF16), wup_ref[...])
        x2 = x1 + gate * pu
        r2 = lax.rsqrt(jnp.mean(x2 * x2, axis=-1, keepdims=True) + EPS)
        xh2 = x2 * r2
        gfin = fg_ref[...]
        err = xh2 * gfin - tg_ref[...]
        loss_ref[...] += _sum_all(err * err) * (0.5 / d)
        dy = err * (1.0 / d)
        dfg_ref[...] += jnp.sum(dy * xh2, axis=0, keepdims=True)
        dg = dy * gfin
        dx2 = r2 * (dg - xh2 * jnp.mean(dg * xh2, axis=-1, keepdims=True))
        dgl = (dx2 * pu * gate * (1.0 - gate)).astype(BF16)
        dgl_ref[...] = dgl
        dpu_ref[...] = (dx2 * gate).astype(dpu_ref.dtype)
        dh2 = _dot(dgl, wg_ref[...], NT)
        dpg_ref[...] += jnp.sum(dh2 * xh1, axis=0, keepdims=True)
        dg = dh2 * pg
        dx1 = dx2 + r1 * (dg - xh1 * jnp.mean(dg * xh1, axis=-1, keepdims=True))
        dx1_ref[...] = dx1
        dx1b = dx1.astype(BF16)
        dx1b_ref[...] = dx1b
        dm = _dot(dx1b, wout_ref[...], NT)
        dpa_ref[...] = (dm * sa).astype(dpa_ref.dtype)
        dpb_ref[...] = (dm * sb).astype(dpb_ref.dtype)
        dgate_ref[:, :d] = (dm * pa * sa * (1.0 - sa)).astype(dgate_ref.dtype)
        dgate_ref[:, d:] = (dm * pb * sb * (1.0 - sb)).astype(dgate_ref.dtype)

    rows = lambda n: pl.BlockSpec((tm, n), lambda i: (i, 0))
    resident = lambda a: pl.BlockSpec(a.shape, lambda i: (0, 0), pipeline_mode=pl.Buffered(1))
    vec = lambda n: pl.BlockSpec((1, n), lambda i: (0, 0))
    bf = lambda: jax.ShapeDtypeStruct((t, d), BF16)
    return _pallas(
        body, name="merge_to_grads", grid=(t // tm,),
        in_specs=[rows(ya3.shape[1]), resident(wa), rows(yb.shape[1]), resident(wb),
                  pl.BlockSpec((tm, d), lambda i: (i, off_ga // d)), pl.BlockSpec((tm, d), lambda i: (i, off_gb // d)),
                  resident(wout), rows(d), vec(d), resident(wg), rows(pd), resident(wup), rows(d), vec(d)],
        out_specs=[rows(d)] * 8 + [pl.BlockSpec((tm, 2 * d), lambda i: (i, off_ga // (2 * d))),
                                   vec(LANES), vec(d), vec(d)],
        out_shape=[bf(), bf(), jax.ShapeDtypeStruct((t, d), F32), bf(), bf(), bf(), bf(), bf(),
                   jax.ShapeDtypeStruct((t, n_main), BF16), jax.ShapeDtypeStruct((1, LANES), F32),
                   jax.ShapeDtypeStruct((1, d), F32), jax.ShapeDtypeStruct((1, d), F32)],
        compiler_params=_cparams(("arbitrary",), 60),
    )(ya3, wa, yb, wb, proj, proj, wout, x, ple_g, wg, p, wup, target, final_g)


def _yb_bwd(dproj, dpb, wb, proj, mixed, mix_b, scale, off_zp):
    t, dp = mixed.shape
    tm = 512

    def body(_, dpb_ref, w_ref, zp_ref, mx_ref, b_ref, s_ref, dzp_ref, dmx_ref, dsc_ref, db_ref):
        @pl.when(pl.program_id(0) == 0)
        def _():
            dsc_ref[...] = jnp.zeros_like(dsc_ref)
            db_ref[...] = jnp.zeros_like(db_ref)

        dyb = _dot(dpb_ref[...], w_ref[...], NT)
        zp = zp_ref[...].astype(F32)
        sg = _sigmoid(zp)
        mb = mx_ref[...].astype(F32) + b_ref[...]
        yb0 = mb * s_ref[...]
        dyb0 = dyb * (zp * sg)
        dzp_ref[...] = (dyb * yb0 * (sg * (1.0 + zp * (1.0 - sg)))).astype(dzp_ref.dtype)
        dsc_ref[...] += jnp.sum(dyb0 * mb, axis=0, keepdims=True)
        dmx = dyb0 * s_ref[...]
        db_ref[...] += jnp.sum(dmx, axis=0, keepdims=True)
        dmx_ref[...] = dmx.astype(dmx_ref.dtype)

    rowblk = pl.BlockSpec((tm, dp), lambda i: (i, 0))
    vec = pl.BlockSpec((1, dp), lambda i: (0, 0))
    seg = pl.BlockSpec((tm, dp), lambda i: (i, off_zp // dp))
    return _pallas(
        body, name="yb_bwd", grid=(t // tm,),
        in_specs=[ANY, pl.BlockSpec((tm, wb.shape[1]), lambda i: (i, 0)),
                  pl.BlockSpec(wb.shape, lambda i: (0, 0)), seg, rowblk, vec, vec],
        out_specs=[seg, rowblk, vec, vec],
        out_shape=[jax.ShapeDtypeStruct(dproj.shape, dproj.dtype), jax.ShapeDtypeStruct((t, dp), BF16),
                   jax.ShapeDtypeStruct((1, dp), F32), jax.ShapeDtypeStruct((1, dp), F32)],
        input_output_aliases={0: 0},
        compiler_params=_cparams(("arbitrary",), 56),
    )(dproj, dpb, wb, proj, mixed, mix_b, scale)


def _dmixw(pooled, dmixed, ng):
    t, dp = pooled.shape
    pg = dp // ng
    tk = min(2048, t)
    nk = t // tk

    def body(a_ref, b_ref, o_ref):
        @pl.when(pl.program_id(1) == 0)
        def _():
            o_ref[...] = jnp.zeros_like(o_ref)

        o_ref[0] += _dot(a_ref[...], b_ref[...], TN)

    blk = pl.BlockSpec((tk, pg), lambda g, k: (k, g))
    return _pallas(
        body, name="dmixw", grid=(ng, nk),
        in_specs=[blk, blk],
        out_specs=pl.BlockSpec((1, pg, pg), lambda g, k: (g, 0, 0)),
        out_shape=jax.ShapeDtypeStruct((ng, pg, pg), F32),
        compiler_params=_cparams(("parallel", "arbitrary")),
    )(pooled, dmixed)


def _pool_bwd(dproj, dmixed, mixw, nb, s, off_u):
    dp = dmixed.shape[1]
    pg = dp // len(POOL_WINDOWS)

    def body(_, d_ref, w_ref, o_ref):
        g = pl.program_id(1)
        dv = _dot(d_ref[...], w_ref[0], NT)
        row = _iota(dv.shape, 0)
        pos = (_iota((dv.shape[0], 1), 0) + 1).astype(F32)
        for gi, w in enumerate(POOL_WINDOWS):
            @pl.when(g == gi)
            def _():
                acc, span = dv / jnp.minimum(pos, float(w)), 1
                while span < w:
                    acc = acc + _shift_up(acc, span, row)
                    span *= 2
                o_ref[...] = (acc - dv).astype(o_ref.dtype)

    return _pallas(
        body, name="pool_bwd", grid=(nb, len(POOL_WINDOWS)),
        in_specs=[ANY, pl.BlockSpec((s, pg), lambda b, g: (b, g)), pl.BlockSpec((1, pg, pg), lambda b, g: (g, 0, 0))],
        out_specs=pl.BlockSpec((s, pg), lambda b, g: (b, off_u // pg + g)),
        out_shape=jax.ShapeDtypeStruct(dproj.shape, dproj.dtype),
        input_output_aliases={0: 0},
        compiler_params=_cparams(("parallel", "parallel"), 48),
    )(dproj, dmixed, mixw)


def _ssd_bwd(dproj, dya3, y, xbc, proj, dtraw, prev, dt_bias, a_log, dskip_full, gnorm, nb, s, d_inner, partials):
    l = CHUNK
    nc = s // l
    n_heads = d_inner // HEAD_DIM
    gw = d_inner // GROUPS
    pairs_per_group = gw // LANES
    gn_cols = GROUPS * D_STATE
    n_conv = d_inner + 2 * gn_cols
    last_step = nb * nc - 1

    xch = _DirectExchange(partials)
    n_in, n_out = 13, 7

    def body(*refs):
        (_, dya3_ref, y_ref, xs_ref, bm_ref, cm_ref, z_ref, dtr_ref, prev_ref, bias_ref, alog_ref,
         dsk_ref, gn_ref) = refs[:n_in]
        p_refs = refs[n_in:n_in + xch.n]
        outs = refs[n_in + xch.n:]
        dz_ref, dxbc_ref, ddt_ref, dgn_ref, dskh_ref, dalog_ref, dbias_ref = outs[:n_out]
        got_refs = outs[n_out:n_out + xch.n]
        g_ref, dskacc_ref = outs[n_out + xch.n:n_out + xch.n + 2]
        sems = outs[n_out + xch.n + 2:]
        step = pl.program_id(0) * nc + pl.program_id(1)

        @pl.when(pl.program_id(1) == 0)
        def _():
            g_ref[...] = jnp.zeros_like(g_ref)

        @pl.when(step == 0)
        def _():
            xch.start(p_refs, got_refs, sems)
            dgn_ref[...] = jnp.zeros_like(dgn_ref)
            dalog_ref[...] = jnp.zeros_like(dalog_ref)
            dbias_ref[...] = jnp.zeros_like(dbias_ref)
            dskacc_ref[...] = jnp.zeros_like(dskacc_ref)

        xs = xs_ref[...].astype(F32)
        zv = z_ref[...].astype(F32)
        dsk = dsk_ref[...]
        sg = _sigmoid(zv)
        sz = zv * sg
        ya1 = y_ref[...].astype(F32) + xs * dsk
        ya2 = ya1 * sz
        parts = []
        for g in range(GROUPS):
            gsl = slice(g * gw, (g + 1) * gw)
            blk = ya2[:, gsl]
            rg = lax.rsqrt(jnp.mean(blk * blk, axis=-1, keepdims=True) + EPS)
            xh = blk * rg
            d3 = dya3_ref[:, gsl].astype(F32)
            dgn_ref[:, gsl] += jnp.sum(d3 * xh, axis=0, keepdims=True)
            dg = d3 * gn_ref[:, gsl]
            parts.append(rg * (dg - xh * jnp.mean(dg * xh, axis=-1, keepdims=True)))
        dya2 = jnp.concatenate(parts, axis=1)
        dy = dya2 * sz
        dz_ref[...] = (dya2 * ya1 * (sg * (1.0 + zv * (1.0 - sg)))).astype(dz_ref.dtype)
        dskacc_ref[...] += jnp.sum(dy * xs, axis=0, keepdims=True)

        dtraw_v = dtr_ref[...]
        bias = bias_ref[...]
        dt, apad, acs, dt_full, acs_full, tri, ex = _ssd_common(dtraw_v, bias, alog_ref[...], n_heads)
        acs_t = acs.T
        e_full = jnp.exp(acs_full)
        last = acs_full[l - 1:l, :]
        dte = jnp.exp(last - acs_full)
        cd = jnp.exp(last)
        xdt = xs * dt_full
        xd = xdt * dte
        dy_e = dy * e_full
        mask = _iota((l, l), 1) <= _iota((l, l), 0)
        lane = _iota((l, LANES), 1)
        hlane = _iota((l, HEAD_PAD), 1)
        hsub = _iota((HEAD_PAD, l), 0)
        dacs = jnp.zeros((l, HEAD_PAD), F32)
        dacs_t = jnp.zeros((HEAD_PAD, l), F32)
        dacs_full, dlast, dxdt_all, db_all, dc_all = [], [], [], [], []
        for g in range(GROUPS):
            gsl = slice(g * gw, (g + 1) * gw)
            bg = bm_ref[:, g * D_STATE:(g + 1) * D_STATE].astype(BF16)
            cg = cm_ref[:, g * D_STATE:(g + 1) * D_STATE].astype(BF16)
            prev_g = prev_ref[0, :, gsl]
            prev_b = prev_g.astype(BF16)
            gst = g_ref[:, gsl]
            gst_b = gst.astype(BF16)
            dye_b = dy_e[:, gsl].astype(BF16)
            dc = _dot(dye_b, prev_b, NT)
            dprev = _dot(cg, dye_b, TN)
            y_off = _dot(cg, prev_b) * e_full[:, gsl]
            xd_g = xd[:, gsl]
            db = _dot(xd_g.astype(BF16), gst_b, NT)
            dxd = _dot(bg, gst_b)
            wgt = dxd * xd_g
            dacs_full.append(dy[:, gsl] * y_off - wgt)
            dlast.append(jnp.sum(wgt, axis=0, keepdims=True)
                         + jnp.sum(prev_g * gst, axis=0, keepdims=True) * cd[:, gsl])
            g_ref[:, gsl] = dprev + cd[:, gsl] * gst
            dxdt_g = dxd * dte[:, gsl]
            cb = _dot(cg, bg, NT)
            dcb = jnp.zeros((l, l), F32)
            for q in range(pairs_per_group):
                pair = g * pairs_per_group + q
                psl = slice(pair * LANES, (pair + 1) * LANES)
                xpb = xdt[:, psl].astype(BF16)
                dyp = dy[:, psl].astype(BF16)
                zero = jnp.zeros_like(dyp)
                halves = (jnp.where(lane < HEAD_DIM, dyp, zero), jnp.where(lane >= HEAD_DIM, dyp, zero))
                acc = dxdt_g[:, q * LANES:(q + 1) * LANES]
                for k, dyh in enumerate(halves):
                    h = 2 * pair + k
                    dk = _decay_mats(acs, acs_t, h, mask)
                    mm = cb * dk
                    dm = _dot(dyh, xpb, NT)
                    dcb = dcb + dm * dk
                    dseg = dm * mm
                    dacs = dacs + jnp.where(hlane == h, jnp.sum(dseg, axis=1, keepdims=True), 0.0)
                    dacs_t = dacs_t + jnp.where(hsub == h, jnp.sum(dseg, axis=0, keepdims=True), 0.0)
                    acc = acc + _dot(mm.astype(BF16), dyh, TN)
                dxdt_all.append(acc)
            dcb_b = dcb.astype(BF16)
            dc_all.append(dc + _dot(dcb_b, bg))
            db_all.append(db + _dot(dcb_b, cg, TN))

        dxdt = jnp.concatenate(dxdt_all, axis=1)
        dxbc_ref[:, :d_inner] = (dy * dsk + dxdt * dt_full).astype(dxbc_ref.dtype)
        dxbc_ref[:, d_inner:d_inner + gn_cols] = jnp.concatenate(db_all, axis=1).astype(dxbc_ref.dtype)
        dxbc_ref[:, d_inner + gn_cols:] = jnp.concatenate(dc_all, axis=1).astype(dxbc_ref.dtype)
        ddt = _exact_r(dxdt * xs, ex, NT)
        rowi = _iota((l, d_inner), 0)
        dacs_f = jnp.concatenate(dacs_full, axis=1) + jnp.where(rowi == l - 1, jnp.concatenate(dlast, axis=1), 0.0)
        dacs = dacs + _exact_r(dacs_f, ex, NT) - dacs_t.T
        da = _exact_l(tri, dacs, TN)
        ddt = ddt + da * apad
        dalog_ref[...] += jnp.sum(da * dt, axis=0, keepdims=True) * apad
        ddt_raw = ddt * _sigmoid(dtraw_v + bias)
        dbias_ref[...] += jnp.sum(ddt_raw, axis=0, keepdims=True)
        ddt_ref[...] = ddt_raw.astype(ddt_ref.dtype)

        @pl.when(step == last_step)
        def _():
            dskh_ref[...] = _exact_r(dskacc_ref[...], ex, NT)[0:1, :]
            xch.wait(p_refs, got_refs, sems)

    row = lambda b, c: b * nc + (nc - 1 - c)
    small = lambda n: pl.BlockSpec((1, n), lambda b, c: (0, 0))
    wide = pl.BlockSpec((l, d_inner), lambda b, c: (row(b, c), 0))
    return _pallas(
        body, name="ssd_bwd", grid=(nb, nc),
        in_specs=[ANY, wide, wide, wide,
                  pl.BlockSpec((l, gn_cols), lambda b, c: (row(b, c), d_inner // gn_cols)),
                  pl.BlockSpec((l, gn_cols), lambda b, c: (row(b, c), d_inner // gn_cols + 1)),
                  wide, pl.BlockSpec((l, HEAD_PAD), lambda b, c: (row(b, c), 0)),
                  pl.BlockSpec((1, D_STATE, d_inner), lambda b, c: (row(b, c), 0, 0)),
                  small(HEAD_PAD), small(HEAD_PAD), small(d_inner), small(d_inner)] + xch.in_specs,
        out_specs=[wide, pl.BlockSpec((l, n_conv), lambda b, c: (row(b, c), 0)),
                   pl.BlockSpec((l, HEAD_PAD), lambda b, c: (row(b, c), 0)),
                   small(d_inner), small(HEAD_PAD), small(HEAD_PAD), small(HEAD_PAD)] + xch.out_specs,
        out_shape=[jax.ShapeDtypeStruct(dproj.shape, dproj.dtype),
                   jax.ShapeDtypeStruct((nb * s, n_conv), ACT),
                   jax.ShapeDtypeStruct((nb * s, HEAD_PAD), BF16),
                   jax.ShapeDtypeStruct((1, d_inner), F32), jax.ShapeDtypeStruct((1, HEAD_PAD), F32),
                   jax.ShapeDtypeStruct((1, HEAD_PAD), F32), jax.ShapeDtypeStruct((1, HEAD_PAD), F32)]
        + xch.out_shape,
        scratch_shapes=[pltpu.VMEM((D_STATE, d_inner), F32), pltpu.VMEM((8, d_inner), F32)] + xch.scratch,
        input_output_aliases={0: 0},
        compiler_params=_cparams(("arbitrary", "arbitrary"), 56),
    )(dproj, dya3, y, xbc, xbc, xbc, proj, dtraw, prev, dt_bias, a_log, dskip_full, gnorm, *partials)


def _conv_bwd(dproj, dxbc, conv_pre, proj, conv_w, nb, s, off_xbc):
    n_conv = dxbc.shape[1]
    cw = 256

    rc = CONV_ROWS

    def fold(v):
        return jnp.sum(v.reshape(rc // SUBLANES, SUBLANES, cw), axis=0)

    def body(_, d_ref, pre_ref, x_ref, w_ref, dx_ref, dw_ref, db_ref, dcpad_ref):
        @pl.when(pl.program_id(1) == 0)
        def _():
            dw_ref[...] = jnp.zeros_like(dw_ref)
            db_ref[...] = jnp.zeros_like(db_ref)

        dcpad_ref[pl.ds(s, SUBLANES), :] = jnp.zeros((SUBLANES, cw), F32)
        taps = [w_ref[k:k + 1, :] for k in range(CONV_W)]
        db = jnp.zeros((SUBLANES, cw), F32)
        for r in range(s // rc):
            rows = pl.ds(r * rc, rc)
            acc = pre_ref[rows, :].astype(F32)
            sg = _sigmoid(acc)
            dc = d_ref[rows, :].astype(F32) * (sg * (1.0 + acc * (1.0 - sg)))
            dcpad_ref[rows, :] = dc
            db = db + fold(dc)
        db_ref[...] += jnp.sum(db, axis=0, keepdims=True)
        dw = [jnp.zeros((SUBLANES, cw), F32) for _ in range(CONV_W)]
        for r in range(s // rc):
            xv = x_ref[pl.ds(r * rc, rc), :].astype(F32)
            win = dcpad_ref[pl.ds(r * rc, rc + SUBLANES), :]
            dxv = jnp.zeros((rc, cw), F32)
            for j in range(CONV_W):
                up = win[:rc, :] if j == 0 else pltpu.roll(win, rc + SUBLANES - j, 0)[:rc, :]
                dxv = dxv + up * taps[CONV_W - 1 - j]
                dw[CONV_W - 1 - j] = dw[CONV_W - 1 - j] + fold(up * xv)
            dx_ref[pl.ds(r * rc, rc), :] = dxv.astype(dx_ref.dtype)
        for k in range(CONV_W):
            dw_ref[k:k + 1, :] += jnp.sum(dw[k], axis=0, keepdims=True)

    blk = pl.BlockSpec((s, cw), lambda j, b: (b, j))
    return _pallas(
        body, name="conv_bwd", grid=(n_conv // cw, nb),
        in_specs=[ANY, blk, blk, pl.BlockSpec((s, cw), lambda j, b: (b, off_xbc // cw + j)),
                  pl.BlockSpec((CONV_W, cw), lambda j, b: (0, j))],
        out_specs=[pl.BlockSpec((s, cw), lambda j, b: (b, off_xbc // cw + j)),
                   pl.BlockSpec((CONV_W, cw), lambda j, b: (0, j)), pl.BlockSpec((1, cw), lambda j, b: (0, j))],
        out_shape=[jax.ShapeDtypeStruct(dproj.shape, dproj.dtype),
                   jax.ShapeDtypeStruct((CONV_W, n_conv), F32), jax.ShapeDtypeStruct((1, n_conv), F32)],
        scratch_shapes=[pltpu.VMEM((s + SUBLANES, cw), F32)],
        input_output_aliases={0: 0},
        compiler_params=_cparams(("parallel", "arbitrary"), 48),
    )(dproj, dxbc, conv_pre, proj, conv_w)


def _in_bwd(dproj, w_main, ddt, w_dt, x, norm_g, dx1, partials):
    t, d = x.shape
    n_main = dproj.shape[1]
    tm = 256
    ni = t // tm
    xch = _ChipExchange(partials)
    n_in, n_out = 7, 2

    def body(*refs):
        dp_ref, w_ref, ddt_ref, wdt_ref, x_ref, g_ref, dx1_ref = refs[:n_in]
        p_refs = refs[n_in:n_in + xch.n]
        outs = refs[n_in + xch.n:]
        dx_ref, dg_ref = outs[:n_out]
        got_refs = outs[n_out:n_out + xch.n]
        sems = outs[n_out + xch.n:]
        i = pl.program_id(0)

        @pl.when(i == 0)
        def _():
            xch.start(p_refs, got_refs, sems)
            dg_ref[...] = jnp.zeros_like(dg_ref)

        dh = _dot(dp_ref[...], w_ref[...], NT) + _dot(ddt_ref[...], wdt_ref[...], NT)
        xv = x_ref[...]
        r = lax.rsqrt(jnp.mean(xv * xv, axis=-1, keepdims=True) + EPS)
        xh = xv * r
        dg_ref[...] += jnp.sum(dh * xh, axis=0, keepdims=True)
        dg = dh * g_ref[...]
        dx_ref[...] = dx1_ref[...] + r * (dg - xh * jnp.mean(dg * xh, axis=-1, keepdims=True))

        @pl.when(i == ni - 1)
        def _():
            xch.wait(p_refs, got_refs, sems)

    rowblk = pl.BlockSpec((tm, d), lambda i: (i, 0))
    vec = pl.BlockSpec((1, d), lambda i: (0, 0))
    resident = lambda shape: pl.BlockSpec(shape, lambda i: (0, 0), pipeline_mode=pl.Buffered(1))
    return _pallas(
        body, name="in_bwd", grid=(ni,),
        in_specs=[pl.BlockSpec((tm, n_main), lambda i: (i, 0)), resident((d, n_main)),
                  pl.BlockSpec((tm, HEAD_PAD), lambda i: (i, 0)), resident((d, HEAD_PAD)),
                  rowblk, vec, rowblk] + xch.in_specs,
        out_specs=[rowblk, vec] + xch.out_specs,
        out_shape=[jax.ShapeDtypeStruct((t, d), F32), jax.ShapeDtypeStruct((1, d), F32)] + xch.out_shape,
        scratch_shapes=xch.scratch,
        compiler_params=_cparams(("arbitrary",), 56),
    )(dproj, w_main, ddt, w_dt, x, norm_g, dx1, *partials)


def _adamw_math(w, g, m, v):
    m = ADAM_B1 * m + (1.0 - ADAM_B1) * g
    v = ADAM_B2 * v + (1.0 - ADAM_B2) * (g * g)
    m_hat = m / (1.0 - ADAM_B1 ** ADAM_STEP)
    v_hat = v / (1.0 - ADAM_B2 ** ADAM_STEP)
    delta = -ADAM_LR * (m_hat / (jnp.sqrt(v_hat) + ADAM_EPS) + ADAM_WD * w)
    return delta, m, v


def _adamw(parts, w, m, v, name):
    n, rows, cols = parts.shape
    tr = _row_tile(rows, 256) if rows % 8 == 0 else rows

    def body(p_ref, w_ref, m_ref, v_ref, g_ref, d_ref, mo_ref, vo_ref):
        g = p_ref[0].astype(F32)
        for k in range(1, n):
            g = g + p_ref[k].astype(F32)
        d, mn, vn = _adamw_math(w_ref[...], g, m_ref[...], v_ref[...])
        g_ref[...] = g
        d_ref[...] = d
        mo_ref[...] = mn
        vo_ref[...] = vn

    blk = pl.BlockSpec((tr, cols), lambda i: (i, 0))
    shp = jax.ShapeDtypeStruct((rows, cols), F32)
    return _pallas(
        body, name=name, grid=(rows // tr,),
        in_specs=[pl.BlockSpec((n, tr, cols), lambda i: (0, i, 0)), blk, blk, blk],
        out_specs=[blk, blk, blk, blk], out_shape=[shp, shp, shp, shp],
        compiler_params=_cparams(("parallel",), 56),
    )(parts, w, m, v)


def _coords():
    return lax.axis_index("x"), lax.axis_index("y"), lax.axis_index("c")


class _TwoLevelGather:
    def __init__(self, blocks):
        self.n = len(blocks)
        self.in_specs = [ANY] * self.n
        self.out_specs = [ANY] * self.n
        self.out_shape = [jax.ShapeDtypeStruct((N_DEV,) + b.shape, b.dtype) for b in blocks]
        self.scratch = [pltpu.SemaphoreType.DMA((self.n, 7)), pltpu.SemaphoreType.DMA((self.n, 7)),
                        pltpu.SemaphoreType.DMA((self.n,))]

    def _parts(self, x_refs, out_refs, sems):
        send_sems, recv_sems, local_sems = sems
        x, y, c = _coords()
        me, sibling = (x, y, c), (x, y, 1 - c)
        chips = [(1 - x, y), (x, 1 - y), (1 - x, 1 - y)]

        def slot(a, px, py, pc):
            return out_refs[a].at[4 * px + 2 * py + pc]

        def copy(a, k, block, to, src=None):
            return pltpu.make_async_remote_copy(
                src_ref=slot(a, *block) if src is None else src, dst_ref=slot(a, *block),
                send_sem=send_sems.at[a, k], recv_sem=recv_sems.at[a, k], device_id=to, device_id_type=MESH)

        local = [pltpu.make_async_copy(x_refs[a], slot(a, *me), local_sems.at[a]) for a in range(self.n)]
        first = []
        for a in range(self.n):
            first.append(copy(a, 0, me, sibling, src=x_refs[a]))
            first += [copy(a, 1 + j, me, (*chip, c), src=x_refs[a]) for j, chip in enumerate(chips)]
        return copy, local, first, chips, me, sibling, c

    def start(self, x_refs, out_refs, sems):
        _, local, first, *_ = self._parts(x_refs, out_refs, sems)
        for cp in local + first:
            cp.start()

    def forward(self, x_refs, out_refs, sems):
        copy, _, _, chips, me, sibling, c = self._parts(x_refs, out_refs, sems)
        for j, chip in enumerate(chips):
            for a in range(self.n):
                copy(a, 1 + j, (*chip, c), me).wait_recv()
                copy(a, 4 + j, (*chip, c), sibling).start()

    def finish(self, x_refs, out_refs, sems):
        copy, local, first, chips, me, sibling, c = self._parts(x_refs, out_refs, sems)
        passed = [copy(a, 4 + j, (*chip, c), sibling) for j, chip in enumerate(chips) for a in range(self.n)]
        for a in range(self.n):
            copy(a, 0, sibling, me).wait_recv()
            for j, chip in enumerate(chips):
                copy(a, 4 + j, (*chip, 1 - c), me).wait_recv()
        for cp in first + passed:
            cp.wait_send()
        for cp in local:
            cp.wait()


def _rms_in_and_gather(xin, g, blk):
    rows, cols = blk.shape
    half = rows // 2
    assert half % 16 == 0
    t, d = xin.shape
    tt = 512
    ni = t // tt

    def exchange(x_ref, out_ref, send_sems, recv_sems, local_sem, begin):
        x, y, c = _coords()
        me, sib, xn, yn, dg = (x, y, c), (x, y, 1 - c), (1 - x, y, c), (x, 1 - y, c), (1 - x, 1 - y, c)
        other = lambda p: (p[0], p[1], 1 - c)
        top, bot = pl.ds(0, half), pl.ds(half, half)

        def slot(p, part=None):
            ref = out_ref.at[4 * p[0] + 2 * p[1] + p[2]]
            return ref if part is None else ref.at[part]

        def copy(k, block, to, part=None, src=None):
            return pltpu.make_async_remote_copy(
                src_ref=slot(block, part) if src is None else src, dst_ref=slot(block, part),
                send_sem=send_sems.at[k], recv_sem=recv_sems.at[k], device_id=to, device_id_type=MESH)

        mine = pltpu.make_async_copy(x_ref, slot(me), local_sem)
        own = [copy(1, me, xn, src=x_ref), copy(2, me, yn, src=x_ref), copy(0, me, sib, src=x_ref)]
        if begin:
            for cp in [mine] + own:
                cp.start()
            return
        copy(1, xn, me).wait_recv()
        fwd = [copy(3, xn, yn, top), copy(5, xn, sib)]
        for cp in fwd:
            cp.start()
        copy(2, yn, me).wait_recv()
        fwd += [copy(4, yn, xn, bot), copy(6, yn, sib)]
        for cp in fwd[2:]:
            cp.start()
        copy(3, dg, me, top).wait_recv()
        fwd.append(copy(7, dg, sib, top))
        fwd[-1].start()
        copy(4, dg, me, bot).wait_recv()
        fwd.append(copy(8, dg, sib, bot))
        fwd[-1].start()
        copy(0, sib, me).wait_recv()
        copy(5, other(xn), me).wait_recv()
        copy(6, other(yn), me).wait_recv()
        copy(7, other(dg), me, top).wait_recv()
        copy(8, other(dg), me, bot).wait_recv()
        for cp in own + fwd:
            cp.wait_send()
        mine.wait()

    def body(w_ref, x_ref, g_ref, out_ref, h_ref, send_sems, recv_sems, local_sem):
        i = pl.program_id(0)

        @pl.when(i == 0)
        def _():
            exchange(w_ref, out_ref, send_sems, recv_sems, local_sem, True)

        xv = x_ref[...]
        r = lax.rsqrt(jnp.mean(xv * xv, axis=-1, keepdims=True) + EPS)
        h_ref[...] = (xv * r * g_ref[...]).astype(h_ref.dtype)

        @pl.when(i == ni - 1)
        def _():
            exchange(w_ref, out_ref, send_sems, recv_sems, local_sem, False)

    return _pallas(
        body, name="rms_in_w_in_all_gather", grid=(ni,),
        in_specs=[ANY, pl.BlockSpec((tt, d), lambda i: (i, 0)), pl.BlockSpec((1, d), lambda i: (0, 0))],
        out_specs=[ANY, pl.BlockSpec((tt, d), lambda i: (i, 0))],
        out_shape=[jax.ShapeDtypeStruct((N_DEV, rows, cols), blk.dtype), jax.ShapeDtypeStruct((t, d), BF16)],
        scratch_shapes=[pltpu.SemaphoreType.DMA((9,)), pltpu.SemaphoreType.DMA((9,)), pltpu.SemaphoreType.DMA],
        compiler_params=_cparams(("arbitrary",)),
    )(blk, xin, g)


def _all_gather_direct(blk, name):
    rows, cols = blk.shape

    def body(x_ref, out_ref, send_sems, recv_sems, local_sem):
        x, y, c = _coords()
        me = 4 * x + 2 * y + c
        mine = pltpu.make_async_copy(x_ref, out_ref.at[me], local_sem)
        mine.start()
        copies = []
        for k in range(1, N_DEV):
            fx, fy, fc = (k >> 2) & 1, (k >> 1) & 1, k & 1
            peer = (x ^ fx, y ^ fy, c ^ fc)
            copies.append(pltpu.make_async_remote_copy(
                src_ref=x_ref, dst_ref=out_ref.at[me], send_sem=send_sems.at[k - 1], recv_sem=recv_sems.at[k - 1],
                device_id=peer, device_id_type=MESH))
        for cp in copies:
            cp.start()
        for k in range(1, N_DEV):
            fx, fy, fc = (k >> 2) & 1, (k >> 1) & 1, k & 1
            src = 4 * (x ^ fx) + 2 * (y ^ fy) + (c ^ fc)
            pltpu.make_async_remote_copy(
                src_ref=x_ref, dst_ref=out_ref.at[src], send_sem=send_sems.at[k - 1], recv_sem=recv_sems.at[k - 1],
                device_id=(x, y, c), device_id_type=MESH).wait_recv()
        for cp in copies:
            cp.wait_send()
        mine.wait()

    return _pallas(
        body, name=name, in_specs=[ANY], out_specs=ANY,
        out_shape=jax.ShapeDtypeStruct((N_DEV, rows, cols), blk.dtype),
        scratch_shapes=[pltpu.SemaphoreType.DMA((7,)), pltpu.SemaphoreType.DMA((7,)), pltpu.SemaphoreType.DMA],
    )(blk)


def _pair_exchange(grads, name):
    n = len(grads)

    def body(*refs):
        g_refs, out_refs = refs[:n], refs[n:2 * n]
        send_sems, recv_sems = refs[2 * n:]
        x, y, c = _coords()
        copies = []
        for a in range(n):
            for k in range(4):
                copies.append(pltpu.make_async_remote_copy(
                    src_ref=g_refs[a].at[2 * k + (1 - c)], dst_ref=out_refs[a].at[k],
                    send_sem=send_sems.at[a, k], recv_sem=recv_sems.at[a, k],
                    device_id=(x, y, 1 - c), device_id_type=MESH))
        for cp in copies:
            cp.start()
        for cp in copies:
            cp.wait()

    return _pallas(
        body, name=name, in_specs=[ANY] * n, out_specs=[ANY] * n,
        out_shape=[jax.ShapeDtypeStruct((4,) + g.shape[1:], g.dtype) for g in grads],
        scratch_shapes=[pltpu.SemaphoreType.DMA((n, 4)), pltpu.SemaphoreType.DMA((n, 4))],
    )(*grads)


def _pair_add(grads, got, core, out_dtype, name):
    _, rows, cols = grads.shape
    tr = _row_tile(rows, 256) if rows % 8 == 0 else rows

    def body(c_ref, a_ref, b_ref, o_ref):
        o_ref[...] = (a_ref[...].astype(F32) + b_ref[...].astype(F32)).astype(o_ref.dtype)

    grid_spec = pltpu.PrefetchScalarGridSpec(
        num_scalar_prefetch=1, grid=(4, rows // tr),
        in_specs=[pl.BlockSpec((1, tr, cols), lambda k, i, c_ref: (2 * k + c_ref[0], i, 0)),
                  pl.BlockSpec((1, tr, cols), lambda k, i, c_ref: (k, i, 0))],
        out_specs=pl.BlockSpec((1, tr, cols), lambda k, i, c_ref: (k, i, 0)))
    return _pallas(
        body, name=name, grid_spec=grid_spec,
        out_shape=jax.ShapeDtypeStruct((4, rows, cols), out_dtype),
        compiler_params=_cparams(("parallel", "parallel"), 48),
    )(core, grads, got)


class _ChipExchange:
    def __init__(self, partials):
        self.n = len(partials)
        self.in_specs = [ANY] * self.n
        self.out_specs = [ANY] * self.n
        self.out_shape = [jax.ShapeDtypeStruct(p.shape, p.dtype) for p in partials]
        self.scratch = [pltpu.SemaphoreType.DMA((self.n, 3)), pltpu.SemaphoreType.DMA((self.n, 3)),
                        pltpu.SemaphoreType.DMA((self.n,))]

    def _copies(self, p_refs, out_refs, sems, receiving):
        send_sems, recv_sems, local_sems = sems
        x, y, c = _coords()
        mychip = 2 * x + y
        local = [pltpu.make_async_copy(p_refs[a].at[mychip], out_refs[a].at[mychip], local_sems.at[a])
                 for a in range(self.n)]
        remote = []
        for a in range(self.n):
            for k in range(1, 4):
                px, py = x ^ ((k >> 1) & 1), y ^ (k & 1)
                peer = 2 * px + py
                remote.append(pltpu.make_async_remote_copy(
                    src_ref=p_refs[a].at[peer], dst_ref=out_refs[a].at[peer if receiving else mychip],
                    send_sem=send_sems.at[a, k - 1], recv_sem=recv_sems.at[a, k - 1],
                    device_id=(x, y, c) if receiving else (px, py, c), device_id_type=MESH))
        return local, remote

    def start(self, p_refs, out_refs, sems):
        local, sends = self._copies(p_refs, out_refs, sems, False)
        for cp in local + sends:
            cp.start()

    def wait(self, p_refs, out_refs, sems):
        local, recvs = self._copies(p_refs, out_refs, sems, True)
        for cp in recvs:
            cp.wait_recv()
        for cp in recvs:
            cp.wait_send()
        for cp in local:
            cp.wait()


class _DirectExchange:
    def __init__(self, partials):
        self.n = len(partials)
        self.in_specs = [ANY] * self.n
        self.out_specs = [ANY] * self.n
        self.out_shape = [jax.ShapeDtypeStruct(p.shape, p.dtype) for p in partials]
        self.scratch = [pltpu.SemaphoreType.DMA((self.n, N_DEV - 1)), pltpu.SemaphoreType.DMA((self.n, N_DEV - 1)),
                        pltpu.SemaphoreType.DMA((self.n,))]

    def _copies(self, p_refs, out_refs, sems, receiving):
        send_sems, recv_sems, local_sems = sems
        x, y, c = _coords()
        me = 4 * x + 2 * y + c
        local = [pltpu.make_async_copy(p_refs[a].at[me], out_refs[a].at[me], local_sems.at[a])
                 for a in range(self.n)]
        remote = []
        for a in range(self.n):
            for k in range(1, N_DEV):
                px, py, pc = x ^ ((k >> 2) & 1), y ^ ((k >> 1) & 1), c ^ (k & 1)
                peer = 4 * px + 2 * py + pc
                remote.append(pltpu.make_async_remote_copy(
                    src_ref=p_refs[a].at[peer], dst_ref=out_refs[a].at[peer if receiving else me],
                    send_sem=send_sems.at[a, k - 1], recv_sem=recv_sems.at[a, k - 1],
                    device_id=(x, y, c) if receiving else (px, py, pc), device_id_type=MESH))
        return local, remote

    def start(self, p_refs, out_refs, sems):
        local, sends = self._copies(p_refs, out_refs, sems, False)
        for cp in local + sends:
            cp.start()

    def wait(self, p_refs, out_refs, sems):
        local, recvs = self._copies(p_refs, out_refs, sems, True)
        for cp in recvs:
            cp.wait_recv()
        for cp in recvs:
            cp.wait_send()
        for cp in local:
            cp.wait()


def _pad_lanes(v, n=HEAD_PAD):
    return jnp.pad(v, ((0, 0), (0, n - v.shape[1])))


def kernel(x, p, norm_g, w_in, conv_w, conv_b, dt_bias, a_log, d_skip, gnorm_g, pool_mix_w, pool_mix_b, pool_scale, w_branch_a, w_branch_b, w_out, ple_norm_g, w_ple_gate, w_ple_up, final_g, loss_target, m_norm_g, m_w_in, m_conv_w, m_conv_b, m_dt_bias, m_a_log, m_d_skip, m_gnorm_g, m_pool_mix_w, m_pool_mix_b, m_pool_scale, m_w_branch_a, m_w_branch_b, m_w_out, m_ple_norm_g, m_w_ple_gate, m_w_ple_up, m_final_g, v_norm_g, v_w_in, v_conv_w, v_conv_b, v_dt_bias, v_a_log, v_d_skip, v_gnorm_g, v_pool_mix_w, v_pool_mix_b, v_pool_scale, v_w_branch_a, v_w_branch_b, v_w_out, v_ple_norm_g, v_w_ple_gate, v_w_ple_up, v_final_g):
    nb, s, d = x.shape
    t = nb * s
    n_heads = dt_bias.shape[1]
    d_inner = n_heads * HEAD_DIM
    n_conv = conv_b.shape[1]
    d_pool = pool_mix_b.shape[1]
    n_in = w_in.shape[2] * N_DEV
    ple_dim = p.shape[-1]
    o_xbc, o_dt = d_inner, d_inner + n_conv
    o_u = o_dt + n_heads
    o_zp = o_u + d_pool
    o_ga, o_gb = o_zp + d_pool, o_zp + d_pool + d
    n_main = n_in - n_heads
    off_ga, off_gb, off_xbc = d_inner, d_inner + d, d_inner + 2 * d
    off_u = off_xbc + n_conv
    off_zp = off_u + d_pool

    sharded = [w_in[0], conv_w[0], pool_mix_w[0], w_branch_a[0], w_branch_b[0], w_out[0], w_ple_gate[0], w_ple_up[0]]
    sharded_m = [m_w_in[0], m_conv_w[0], m_pool_mix_w[0], m_w_branch_a[0], m_w_branch_b[0], m_w_out[0],
                 m_w_ple_gate[0], m_w_ple_up[0]]
    sharded_v = [v_w_in[0], v_conv_w[0], v_pool_mix_w[0], v_w_branch_a[0], v_w_branch_b[0], v_w_out[0],
                 v_w_ple_gate[0], v_w_ple_up[0]]
    ng = pool_mix_w.shape[1]
    pg = d_pool // ng
    two_d = lambda w: w.reshape(-1, w.shape[-1])
    sharded, sharded_m, sharded_v = ([two_d(w) for w in ws] for ws in (sharded, sharded_m, sharded_v))

    send = [w if i == 1 else w.astype(BF16) for i, w in enumerate(sharded)]
    x2 = x.reshape(t, d)
    chunk = n_in // N_DEV
    segments = [((0, o_xbc), 0), ((o_ga, o_gb), off_ga), ((o_gb, n_in), off_gb), ((o_xbc, o_dt), off_xbc),
                ((o_u, o_zp), off_u), ((o_zp, o_ga), off_zp)]

    def pieces_of(dev):
        lo, hi = dev * chunk, (dev + 1) * chunk
        out = []
        for (olo, ohi), off in sorted(segments + [((o_dt, o_u), None)], key=lambda e: e[0][0]):
            a, b = max(lo, olo), min(hi, ohi)
            if a < b:
                out.append((a - lo, b - lo, None if off is None else off + a - olo))
        return out

    def lane_cover(a, b, off):
        left = 0 if off is None else off % LANES
        return left, -(-(left + b - a) // LANES) * LANES

    send_width = max(sum(lane_cover(*pc)[1] for pc in pieces_of(dev)) for dev in range(N_DEV))

    def aligned_shard(dev):
        def build(w):
            bufs = []
            for a, b, off in pieces_of(dev):
                left, width = lane_cover(a, b, off)
                bufs.append(jnp.pad(w[:, a:b], ((0, 0), (left, width - left - (b - a)))))
            out = jnp.concatenate(bufs, axis=1)
            return jnp.pad(out, ((0, 0), (0, send_width - out.shape[1])))
        return build

    me_lin = 4 * lax.axis_index("x") + 2 * lax.axis_index("y") + lax.axis_index("c")
    w_in_sent = lax.switch(me_lin, [aligned_shard(dev) for dev in range(N_DEV)], send[0])
    w_in_all, h = _rms_in_and_gather(x2, norm_g, w_in_sent)
    seg = [w_in_all]
    w_main = jnp.zeros((d, n_main), BF16)
    w_dt = None
    for dev in range(N_DEV):
        q = 0
        for a, b, off in pieces_of(dev):
            left, width = lane_cover(a, b, off)
            cover = w_in_all[dev, :, q:q + width]
            q += width
            if off is None:
                w_dt = cover
            else:
                start = off - left
                w_main = w_main + jnp.pad(cover, ((0, 0), (start, n_main - start - width)))

    proj, dtraw, *rest = _in_proj(h, w_main, w_dt, send[1:])
    seg += rest
    cw_full = seg[1].transpose(1, 0, 2).reshape(CONV_W, n_conv)
    mixw_full = seg[2].reshape(N_DEV, ng, pg // N_DEV, pg).transpose(1, 0, 2, 3).reshape(ng, pg, pg)
    wa_full = seg[3].reshape(d_inner, d)
    wb_full = seg[4].reshape(d_pool, d)
    wout_full = seg[5].reshape(d, d)
    wg_full = seg[6].reshape(d, d)
    wup_full = seg[7].transpose(1, 0, 2).reshape(ple_dim, d)

    p2 = p.reshape(t, ple_dim)
    tgt = loss_target.reshape(t, d)
    bias_p, alog_p = _pad_lanes(dt_bias), _pad_lanes(a_log)
    dskip_full = jnp.repeat(d_skip, HEAD_DIM, axis=1)

    xbc, conv_pre = _conv_fwd(proj, cw_full, conv_b, nb, s, off_xbc, n_conv)
    y, ya3, prev = _ssd_fwd(xbc, proj, dtraw, bias_p, alog_p, dskip_full, gnorm_g, nb, s, d_inner)
    pooled, mixed, yb = _pool_fwd(proj, mixw_full, pool_mix_b, pool_scale, nb, s, off_u, off_zp, d_pool)
    (merged, h2, dx1, dx1b, dgl, dpu, dpa, dpb, dproj, loss_part, dfinal_g, d_ple_g) = _merge_to_grads(
        ya3, wa_full, yb, wb_full, proj, off_ga, off_gb, wout_full, x2, ple_norm_g, wg_full, p2, wup_full, tgt,
        final_g.reshape(1, d), n_main)

    g_wup = _matmul(p2, dpu, "tn", BF16, "dw_ple_up", 256, 512, t)
    g_wg = _matmul(h2, dgl, "tn", BF16, "dw_ple_gate", 512, 1024, t)
    g_wout = _matmul(merged, dx1b, "tn", BF16, "dw_out", 512, 1024, t)
    g_wa = _matmul(ya3, dpa, "tn", BF16, "dw_branch_a", 512, 1024, t)
    g_wb = _matmul(yb, dpb, "tn", BF16, "dw_branch_b", 512, 1024, t)
    dya3 = _matmul(dpa, wa_full, "nt", ACT, "d_ya3", 1024, 1024, d)
    dproj, dmixed, d_scale, d_mixb = _yb_bwd(dproj, dpb, wb_full, proj, mixed, pool_mix_b, pool_scale, off_zp)
    g_mixw = _dmixw(pooled, dmixed, ng)
    dproj = _pool_bwd(dproj, dmixed, mixw_full, nb, s, off_u)

    core = lax.axis_index("c").astype(I32).reshape(1)

    def pair_sums(by_dest, names):
        by_dest = [g.astype(BF16) for g in by_dest]
        got = _pair_exchange(by_dest, "rs_pair_exchange_" + names[0])
        return [_pair_add(g, r, core, BF16, "rs_pair_add_" + nm) for g, r, nm in zip(by_dest, got, names)]

    names_a = ["pool_mix_w", "w_branch_a", "w_branch_b", "w_out", "w_ple_gate", "w_ple_up"]
    partials_a = [
        g_mixw.reshape(ng, N_DEV, pg // N_DEV, pg).transpose(1, 0, 2, 3).reshape(N_DEV, ng * pg // N_DEV, pg)
        .astype(BF16),
        g_wa.reshape(N_DEV, d_inner // N_DEV, d), g_wb.reshape(N_DEV, d_pool // N_DEV, d),
        g_wout.reshape(N_DEV, d // N_DEV, d), g_wg.reshape(N_DEV, d // N_DEV, d),
        g_wup.reshape(ple_dim, N_DEV, d // N_DEV).transpose(1, 0, 2)]
    dproj, dxbc, ddt, d_gn, d_dsk, d_alog, d_dtb, *parts_a = _ssd_bwd(
        dproj, dya3, y, xbc, proj, dtraw, prev, bias_p, alog_p, dskip_full, gnorm_g, nb, s, d_inner, partials_a)
    dproj, g_cw, d_cb = _conv_bwd(dproj, dxbc, conv_pre, proj, cw_full, nb, s, off_xbc)
    g_wmain, g_wdt = _dw_in(h, dproj, ddt)

    def dest_cols(dev):
        lo, hi = dev * chunk, (dev + 1) * chunk
        srcs = [((olo, ohi), g_wmain, off) for (olo, ohi), off in segments] + [((o_dt, o_u), g_wdt, 0)]
        out = []
        for (olo, ohi), arr, off in sorted(srcs, key=lambda e: e[0][0]):
            a, b = max(lo, olo), min(hi, ohi)
            if a < b:
                out.append(arr[:, off + a - olo:off + b - olo])
        return jnp.concatenate(out, axis=1)

    names_b = ["w_in", "conv_w"]
    partials_b = pair_sums([jnp.stack([dest_cols(dev) for dev in range(N_DEV)], axis=0),
                            g_cw.reshape(CONV_W, N_DEV, n_conv // N_DEV).transpose(1, 0, 2)], names_b)
    grad_x, d_ng, *parts_b = _in_bwd(dproj, w_main, ddt, w_dt, x2, norm_g, dx1, partials_b)

    names = names_b + names_a
    parts = parts_b + parts_a
    sh_g, sh_d, sh_m, sh_v = [], [], [], []
    shapes = [w_in.shape, conv_w.shape, pool_mix_w.shape, w_branch_a.shape, w_branch_b.shape, w_out.shape,
              w_ple_gate.shape, w_ple_up.shape]
    for pt, w, m, v, nm, shp in zip(parts, sharded, sharded_m, sharded_v, names, shapes):
        res = _adamw(pt, w, m, v, "adamw_" + nm)
        for dst, r in zip((sh_g, sh_d, sh_m, sh_v), res):
            dst.append(r.reshape(shp))

    rep_w = [norm_g, conv_b, gnorm_g, pool_mix_b, pool_scale, ple_norm_g, final_g.reshape(1, d),
             bias_p, alog_p, _pad_lanes(d_skip)]
    rep_m = [m_norm_g, m_conv_b, m_gnorm_g, m_pool_mix_b, m_pool_scale, m_ple_norm_g, m_final_g.reshape(1, d),
             _pad_lanes(m_dt_bias), _pad_lanes(m_a_log), _pad_lanes(m_d_skip)]
    rep_v = [v_norm_g, v_conv_b, v_gnorm_g, v_pool_mix_b, v_pool_scale, v_ple_norm_g, v_final_g.reshape(1, d),
             _pad_lanes(v_dt_bias), _pad_lanes(v_a_log), _pad_lanes(v_d_skip)]
    rep_g = [d_ng, d_cb, d_gn, d_mixb, d_scale, d_ple_g, dfinal_g, d_dtb, d_alog, d_dsk]
    zero = jnp.zeros((1, LANES), F32)
    small = _all_gather_direct(jnp.concatenate(rep_g + [loss_part], axis=1), "small_all_gather")
    pk = lambda ws: jnp.concatenate(ws + [zero], axis=1)
    rg, rd, rm, rv = _adamw(small, pk(rep_w), pk(rep_m), pk(rep_v), "adamw_replicated")
    loss = rg[0, sum(int(w.size) for w in rep_w)]

    def unpack_rep(buf):
        out, pos_ = [], 0
        for w in rep_w:
            out.append(buf[:, pos_:pos_ + w.size])
            pos_ += w.size
        return out

    def arrange(sh, rep):
        rep = list(rep)
        rep[6] = rep[6].reshape(d)
        for i in (7, 8, 9):
            rep[i] = rep[i][:, :n_heads]
        return [rep[0], sh[0], sh[1], rep[1], rep[7], rep[8], rep[9], rep[2], sh[2], rep[3], rep[4],
                sh[3], sh[4], sh[5], rep[5], sh[6], sh[7], rep[6]]

    outs = [loss, grad_x.reshape(nb, s, d)]
    for sh, rep in ((sh_g, rg), (sh_d, rd), (sh_m, rm), (sh_v, rv)):
        outs += arrange(sh, unpack_rep(rep))
    return tuple(outs)
```

```python
import jax
import jax.numpy as jnp
from jax import lax
from jax.experimental import pallas as pl
from jax.experimental.pallas import tpu as pltpu

F32 = jnp.float32
BF16 = jnp.bfloat16
I32 = jnp.int32
MESH = pl.DeviceIdType.MESH
ANY = pl.BlockSpec(memory_space=pl.ANY)

EPS = 1e-6
CHUNK = 128
HEAD_DIM = 64
D_STATE = 128
GROUPS = 4
CONV_W = 4
POOL_WINDOWS = (2, 4, 8, 16)
ACT = jnp.bfloat16
LANES = 128
SUBLANES = 8
CONV_ROWS = 64
HEAD_PAD = 128
NEG = -1e30
N_DEV = 8

ADAM_LR = 0.001
ADAM_B1 = 0.9
ADAM_B2 = 0.999
ADAM_EPS = 1e-08
ADAM_WD = 0.01
ADAM_STEP = 10

NN = (((1,), (0,)), ((), ()))
NT = (((1,), (1,)), ((), ()))
TN = (((0,), (0,)), ((), ()))


def _pallas(body, **kw):
    return pl.pallas_call(body, **kw)


def _cparams(dims=None, vmem_mb=None):
    kw = {}
    if dims is not None:
        kw["dimension_semantics"] = dims
    if vmem_mb is not None:
        kw["vmem_limit_bytes"] = vmem_mb << 20
    return pltpu.CompilerParams(**kw)


def _dot(a, b, dims=NN):
    return lax.dot_general(a, b, dims, preferred_element_type=F32)


def _split3(v):
    hi = v.astype(BF16)
    r = v - hi.astype(F32)
    mid = r.astype(BF16)
    lo = (r - mid.astype(F32)).astype(BF16)
    return hi, mid, lo


def _exact_r(v, m, dims=NN):
    hi, mid, lo = _split3(v)
    return _dot(hi, m, dims) + _dot(mid, m, dims) + _dot(lo, m, dims)


def _exact_l(m, v, dims=NN):
    hi, mid, lo = _split3(v)
    return _dot(m, hi, dims) + _dot(m, mid, dims) + _dot(m, lo, dims)


def _row_tile(rows, target):
    best = None
    for cand in range(8, min(rows, target) + 1, 8):
        if rows % cand == 0:
            best = cand
    assert best is not None, rows
    return best


def _sum_all(v):
    return jnp.sum(jnp.sum(v, axis=1, keepdims=True), axis=0, keepdims=True)


def _iota(shape, dim):
    return lax.broadcasted_iota(I32, shape, dim)


def _sigmoid(x):
    return 0.5 * jnp.tanh(0.5 * x) + 0.5


def _softplus(x):
    return jnp.maximum(x, 0.0) + jnp.log(1.0 + jnp.exp(-jnp.abs(x)))


def _shift_down(v, j, row):
    return jnp.where(row >= j, pltpu.roll(v, j, 0), 0.0)


def _shift_up(v, j, row):
    n = v.shape[0]
    return jnp.where(row < n - j, pltpu.roll(v, n - j, 0), 0.0)


def _matmul(a, b, mode, out_dtype, name, tm, tn, tk):
    if mode == "nn":
        (m, k), n = a.shape, b.shape[1]
    elif mode == "nt":
        (m, k), n = a.shape, b.shape[0]
    else:
        (k, m), n = a.shape, b.shape[1]
    tm, tn, tk = min(tm, m), min(tn, n), min(tk, k)
    assert m % tm == 0 and n % tn == 0 and k % tk == 0, (name, m, n, k)
    nk = k // tk
    dims = {"nn": NN, "nt": NT, "tn": TN}[mode]

    def body_acc(a_ref, b_ref, o_ref, acc_ref):
        kk = pl.program_id(2)

        @pl.when(kk == 0)
        def _():
            acc_ref[...] = jnp.zeros_like(acc_ref)

        acc_ref[...] += _dot(a_ref[...].astype(BF16), b_ref[...].astype(BF16), dims)

        @pl.when(kk == nk - 1)
        def _():
            o_ref[...] = acc_ref[...].astype(o_ref.dtype)

    def body_one(a_ref, b_ref, o_ref):
        o_ref[...] = _dot(a_ref[...].astype(BF16), b_ref[...].astype(BF16), dims).astype(o_ref.dtype)

    body = body_one if nk == 1 else body_acc
    if mode == "tn":
        a_spec = pl.BlockSpec((tk, tm), lambda i, j, kk: (kk, i))
    else:
        a_spec = pl.BlockSpec((tm, tk), lambda i, j, kk: (i, kk))
    if mode == "nt":
        b_spec = pl.BlockSpec((tn, tk), lambda i, j, kk: (j, kk))
    else:
        b_spec = pl.BlockSpec((tk, tn), lambda i, j, kk: (kk, j))
    return _pallas(
        body, name=name, grid=(m // tm, n // tn, nk),
        in_specs=[a_spec, b_spec],
        out_specs=pl.BlockSpec((tm, tn), lambda i, j, kk: (i, j)),
        out_shape=jax.ShapeDtypeStruct((m, n), out_dtype),
        scratch_shapes=[] if nk == 1 else [pltpu.VMEM((tm, tn), F32)],
        compiler_params=_cparams(("parallel", "parallel", "arbitrary"), 56),
    )(a, b)


def _in_proj(h, w_main, w_dt, blocks):
    t, d = h.shape
    n = w_main.shape[1]
    tm, tn = min(1024, t), 1024
    ni, nj = t // tm, n // tn
    gather = _TwoLevelGather(blocks)

    def body(*refs):
        a_ref, b_ref, wdt_ref = refs[:3]
        x_refs = refs[3:3 + gather.n]
        o_ref, dt_ref = refs[3 + gather.n:5 + gather.n]
        got_refs = refs[5 + gather.n:5 + 2 * gather.n]
        sems = refs[5 + 2 * gather.n:]
        i, j = pl.program_id(0), pl.program_id(1)

        @pl.when((i == 0) & (j == 0))
        def _():
            gather.start(x_refs, got_refs, sems)

        o_ref[...] = _dot(a_ref[...], b_ref[...]).astype(o_ref.dtype)

        @pl.when(j == 0)
        def _():
            dt_ref[...] = _dot(a_ref[...], wdt_ref[...])

        @pl.when((i == ni - 1) & (j == max(nj - 5, 0)))
        def _():
            gather.forward(x_refs, got_refs, sems)

        @pl.when((i == ni - 1) & (j == nj - 1))
        def _():
            gather.finish(x_refs, got_refs, sems)

    return _pallas(
        body, name="in_proj", grid=(ni, nj),
        in_specs=[pl.BlockSpec((tm, d), lambda i, j: (i, 0)), pl.BlockSpec((d, tn), lambda i, j: (0, j)),
                  pl.BlockSpec((d, HEAD_PAD), lambda i, j: (0, 0))] + gather.in_specs,
        out_specs=[pl.BlockSpec((tm, tn), lambda i, j: (i, j)), pl.BlockSpec((tm, HEAD_PAD), lambda i, j: (i, 0))]
        + gather.out_specs,
        out_shape=[jax.ShapeDtypeStruct((t, n), ACT), jax.ShapeDtypeStruct((t, HEAD_PAD), F32)] + gather.out_shape,
        scratch_shapes=gather.scratch,
        compiler_params=_cparams(("arbitrary", "arbitrary"), 56),
    )(h, w_main, w_dt, *blocks)


def _dw_in(h, dproj, ddt):
    t, d = h.shape
    n = dproj.shape[1]
    tm, tn = 512, 1024

    def body(a_ref, b_ref, ddt_ref, o_ref, odt_ref):
        a = a_ref[...]
        o_ref[...] = _dot(a, b_ref[...], TN).astype(o_ref.dtype)

        @pl.when(pl.program_id(1) == 0)
        def _():
            odt_ref[...] = _dot(a, ddt_ref[...], TN).astype(odt_ref.dtype)

    return _pallas(
        body, name="dw_in", grid=(d // tm, n // tn),
        in_specs=[pl.BlockSpec((t, tm), lambda i, j: (0, i)), pl.BlockSpec((t, tn), lambda i, j: (0, j)),
                  pl.BlockSpec((t, HEAD_PAD), lambda i, j: (0, 0))],
        out_specs=[pl.BlockSpec((tm, tn), lambda i, j: (i, j)), pl.BlockSpec((tm, HEAD_PAD), lambda i, j: (i, 0))],
        out_shape=[jax.ShapeDtypeStruct((d, n), BF16), jax.ShapeDtypeStruct((d, HEAD_PAD), BF16)],
        compiler_params=_cparams(("parallel", "arbitrary"), 56),
    )(h, dproj, ddt)


def _conv_fwd(proj, conv_w, conv_b, nb, s, off_xbc, n_conv):
    cw = 256
    rc = CONV_ROWS

    def body(x_ref, w_ref, b_ref, o_ref, pre_ref, pad_ref):
        pad_ref[0:SUBLANES, :] = jnp.zeros((SUBLANES, cw), F32)
        pad_ref[SUBLANES:, :] = x_ref[...].astype(F32)
        taps = [w_ref[k:k + 1, :] for k in range(CONV_W)]
        bias = b_ref[...]
        for r in range(s // rc):
            win = pad_ref[pl.ds(r * rc, rc + SUBLANES), :]
            acc = bias + win[SUBLANES:, :] * taps[CONV_W - 1]
            for j in range(1, CONV_W):
                acc = acc + pltpu.roll(win, j, 0)[SUBLANES:, :] * taps[CONV_W - 1 - j]
            pre_ref[pl.ds(r * rc, rc), :] = acc.astype(pre_ref.dtype)
            o_ref[pl.ds(r * rc, rc), :] = (acc * _sigmoid(acc)).astype(o_ref.dtype)

    out = pl.BlockSpec((s, cw), lambda b, j: (b, j))
    return _pallas(
        body, name="conv_fwd", grid=(nb, n_conv // cw),
        in_specs=[pl.BlockSpec((s, cw), lambda b, j: (b, off_xbc // cw + j)),
                  pl.BlockSpec((CONV_W, cw), lambda b, j: (0, j)),
                  pl.BlockSpec((1, cw), lambda b, j: (0, j))],
        out_specs=[out, out],
        out_shape=[jax.ShapeDtypeStruct((nb * s, n_conv), ACT), jax.ShapeDtypeStruct((nb * s, n_conv), ACT)],
        scratch_shapes=[pltpu.VMEM((s + SUBLANES, cw), F32)],
        compiler_params=_cparams(("parallel", "parallel"), 48),
    )(proj, conv_w, conv_b)


def _ssd_common(dtraw, bias, alog, n_heads):
    l = dtraw.shape[0]
    lane = _iota((l, HEAD_PAD), 1)
    lane1 = _iota((1, HEAD_PAD), 1)
    apad = jnp.where(lane1 < n_heads, -jnp.exp(alog), 0.0)
    dt = jnp.where(lane < n_heads, _softplus(dtraw + bias), 0.0)
    a = dt * apad
    tri = (_iota((l, l), 1) <= _iota((l, l), 0)).astype(BF16)
    acs = _exact_l(tri, a)
    width = n_heads * HEAD_DIM
    ex = (_iota((HEAD_PAD, width), 1) // HEAD_DIM == _iota((HEAD_PAD, width), 0)).astype(BF16)
    dt_full = _exact_r(dt, ex)
    acs_full = _exact_r(acs, ex)
    return dt, apad, acs, dt_full, acs_full, tri, ex


def _decay_mats(acs, acs_t, h, mask):
    col = acs[:, h:h + 1]
    row = acs_t[h:h + 1, :]
    return jnp.exp(jnp.where(mask, col - row, NEG))


def _ssd_fwd(xbc, proj, dtraw, dt_bias, a_log, dskip_full, gnorm, nb, s, d_inner):
    l = CHUNK
    nc = s // l
    n_heads = d_inner // HEAD_DIM
    gw = d_inner // GROUPS
    pairs_per_group = gw // LANES
    gn_cols = GROUPS * D_STATE

    def body(xs_ref, bm_ref, cm_ref, z_ref, dtr_ref, bias_ref, alog_ref, dsk_ref, gn_ref,
             y_ref, ya3_ref, prev_ref, state_ref):
        @pl.when(pl.program_id(1) == 0)
        def _():
            state_ref[...] = jnp.zeros_like(state_ref)

        xs = xs_ref[...].astype(F32)
        dt, apad, acs, dt_full, acs_full, tri, ex = _ssd_common(
            dtr_ref[...], bias_ref[...], alog_ref[...], n_heads)
        acs_t = acs.T
        e_full = jnp.exp(acs_full)
        last = acs_full[l - 1:l, :]
        dte = jnp.exp(last - acs_full)
        cd = jnp.exp(last)
        xdt = xs * dt_full
        xd = xdt * dte
        mask = _iota((l, l), 1) <= _iota((l, l), 0)
        lane = _iota((l, LANES), 1)
        prev_ref[0] = state_ref[...]
        for g in range(GROUPS):
            gsl = slice(g * gw, (g + 1) * gw)
            bg = bm_ref[:, g * D_STATE:(g + 1) * D_STATE].astype(BF16)
            cg = cm_ref[:, g * D_STATE:(g + 1) * D_STATE].astype(BF16)
            cb = _dot(cg, bg, NT)
            outs = []
            for q in range(pairs_per_group):
                pair = g * pairs_per_group + q
                h0, h1 = 2 * pair, 2 * pair + 1
                xpb = xdt[:, pair * LANES:(pair + 1) * LANES].astype(BF16)
                m0 = (cb * _decay_mats(acs, acs_t, h0, mask)).astype(BF16)
                m1 = (cb * _decay_mats(acs, acs_t, h1, mask)).astype(BF16)
                zero = jnp.zeros_like(xpb)
                rhs = jnp.concatenate([jnp.where(lane < HEAD_DIM, xpb, zero),
                                       jnp.where(lane >= HEAD_DIM, xpb, zero)], axis=0)
                outs.append(_dot(jnp.concatenate([m0, m1], axis=1), rhs))
            prev_g = state_ref[:, gsl]
            y_off = _dot(cg, prev_g.astype(BF16)) * e_full[:, gsl]
            y_g = jnp.concatenate(outs, axis=1) + y_off
            y_ref[:, gsl] = y_g.astype(y_ref.dtype)
            st = _dot(bg, xd[:, gsl].astype(BF16), TN)
            state_ref[:, gsl] = prev_g * cd[:, gsl] + st
            zv = z_ref[:, gsl].astype(F32)
            blk = (y_g + xs[:, gsl] * dsk_ref[:, gsl]) * (zv * _sigmoid(zv))
            rg = lax.rsqrt(jnp.mean(blk * blk, axis=-1, keepdims=True) + EPS)
            ya3_ref[:, gsl] = (blk * rg * gn_ref[:, gsl]).astype(ya3_ref.dtype)

    row = lambda b, c: b * nc + c
    small = lambda n: pl.BlockSpec((1, n), lambda b, c: (0, 0))
    return _pallas(
        body, name="ssd_fwd", grid=(nb, nc),
        in_specs=[pl.BlockSpec((l, d_inner), lambda b, c: (row(b, c), 0)),
                  pl.BlockSpec((l, gn_cols), lambda b, c: (row(b, c), d_inner // gn_cols)),
                  pl.BlockSpec((l, gn_cols), lambda b, c: (row(b, c), d_inner // gn_cols + 1)),
                  pl.BlockSpec((l, d_inner), lambda b, c: (row(b, c), 0)),
                  pl.BlockSpec((l, HEAD_PAD), lambda b, c: (row(b, c), 0)),
                  small(HEAD_PAD), small(HEAD_PAD), small(d_inner), small(d_inner)],
        out_specs=[pl.BlockSpec((l, d_inner), lambda b, c: (row(b, c), 0)),
                   pl.BlockSpec((l, d_inner), lambda b, c: (row(b, c), 0)),
                   pl.BlockSpec((1, D_STATE, d_inner), lambda b, c: (row(b, c), 0, 0))],
        out_shape=[jax.ShapeDtypeStruct((nb * s, d_inner), ACT),
                   jax.ShapeDtypeStruct((nb * s, d_inner), BF16),
                   jax.ShapeDtypeStruct((nb * nc, D_STATE, d_inner), F32)],
        scratch_shapes=[pltpu.VMEM((D_STATE, d_inner), F32)],
        compiler_params=_cparams(("parallel", "arbitrary"), 48),
    )(xbc, xbc, xbc, proj, dtraw, dt_bias, a_log, dskip_full, gnorm)


def _pool_fwd(proj, mixw, mix_b, scale, nb, s, off_u, off_zp, d_pool):
    pg = d_pool // len(POOL_WINDOWS)

    def body(u_ref, zp_ref, w_ref, b_ref, s_ref, pooled_ref, mixed_ref, yb_ref):
        g = pl.program_id(1)
        uv = u_ref[...].astype(F32)
        row = _iota(uv.shape, 0)
        pos = (_iota((uv.shape[0], 1), 0) + 1).astype(F32)
        for gi, w in enumerate(POOL_WINDOWS):
            @pl.when(g == gi)
            def _():
                acc, span = uv, 1
                while span < w:
                    acc = acc + _shift_down(acc, span, row)
                    span *= 2
                pooled_ref[...] = (acc / jnp.minimum(pos, float(w)) - uv).astype(pooled_ref.dtype)

        mixed = _dot(pooled_ref[...], w_ref[0])
        mixed_ref[...] = mixed.astype(mixed_ref.dtype)
        zp = zp_ref[...].astype(F32)
        yb_ref[...] = ((mixed + b_ref[...]) * s_ref[...] * (zp * _sigmoid(zp))).astype(yb_ref.dtype)

    seg = lambda off: pl.BlockSpec((s, pg), lambda b, g: (b, off // pg + g))
    vec = pl.BlockSpec((1, pg), lambda b, g: (0, g))
    return _pallas(
        body, name="pool_fwd", grid=(nb, len(POOL_WINDOWS)),
        in_specs=[seg(off_u), seg(off_zp), pl.BlockSpec((1, pg, pg), lambda b, g: (g, 0, 0)), vec, vec],
        out_specs=[seg(0), seg(0), seg(0)],
        out_shape=[jax.ShapeDtypeStruct((nb * s, d_pool), BF16), jax.ShapeDtypeStruct((nb * s, d_pool), ACT),
                   jax.ShapeDtypeStruct((nb * s, d_pool), BF16)],
        compiler_params=_cparams(("parallel", "parallel"), 48),
    )(proj, proj, mixw, mix_b, scale)


def _merge_to_grads(ya3, wa, yb, wb, proj, off_ga, off_gb, off_zp, wout, x, ple_g, wg, p, wup, target, final_g,
                    mixed, mix_b, scale, n_main):
    t, d = x.shape
    pd = p.shape[1]
    tm = 256
    assert off_gb == off_ga + d and off_zp == off_gb + d and off_ga % (3 * d) == 0 and mixed.shape[1] == d

    def body(a_ref, wa_ref, b_ref, wb_ref, ga_ref, gb_ref, wout_ref, x_ref, pg_ref, wg_ref, p_ref, wup_ref,
             tg_ref, fg_ref, zp_ref, mx_ref, mb_ref, sc_ref,
             mg_ref, h2_ref, dx1_ref, dx1b_ref, dgl_ref, dpu_ref, dpa_ref, dpb_ref, dmx_ref, dgate_ref,
             loss_ref, dfg_ref, dpg_ref, dsc_ref, dmb_ref):
        @pl.when(pl.program_id(0) == 0)
        def _():
            loss_ref[...] = jnp.zeros_like(loss_ref)
            dfg_ref[...] = jnp.zeros_like(dfg_ref)
            dpg_ref[...] = jnp.zeros_like(dpg_ref)
            dsc_ref[...] = jnp.zeros_like(dsc_ref)
            dmb_ref[...] = jnp.zeros_like(dmb_ref)

        pa = _dot(a_ref[...], wa_ref[...])
        pb = _dot(b_ref[...], wb_ref[...])
        sa = _sigmoid(ga_ref[...].astype(F32))
        sb = _sigmoid(gb_ref[...].astype(F32))
        merged = (sa * pa + sb * pb).astype(BF16)
        mg_ref[...] = merged
        x1 = x_ref[...] + _dot(merged, wout_ref[...])
        r1 = lax.rsqrt(jnp.mean(x1 * x1, axis=-1, keepdims=True) + EPS)
        xh1 = x1 * r1
        pg = pg_ref[...]
        h2 = (xh1 * pg).astype(BF16)
        h2_ref[...] = h2
        gate = _sigmoid(_dot(h2, wg_ref[...]))
        pu = _dot(p_ref[...].astype(BF16), wup_ref[...])
        x2 = x1 + gate * pu
        r2 = lax.rsqrt(jnp.mean(x2 * x2, axis=-1, keepdims=True) + EPS)
        xh2 = x2 * r2
        gfin = fg_ref[...]
        err = xh2 * gfin - tg_ref[...]
        loss_ref[...] += _sum_all(err * err) * (0.5 / d)
        dy = err * (1.0 / d)
        dfg_ref[...] += jnp.sum(dy * xh2, axis=0, keepdims=True)
        dg = dy * gfin
        dx2 = r2 * (dg - xh2 * jnp.mean(dg * xh2, axis=-1, keepdims=True))
        dgl = (dx2 * pu * gate * (1.0 - gate)).astype(BF16)
        dgl_ref[...] = dgl
        dpu_ref[...] = (dx2 * gate).astype(dpu_ref.dtype)
        dh2 = _dot(dgl, wg_ref[...], NT)
        dpg_ref[...] += jnp.sum(dh2 * xh1, axis=0, keepdims=True)
        dg = dh2 * pg
        dx1 = dx2 + r1 * (dg - xh1 * jnp.mean(dg * xh1, axis=-1, keepdims=True))
        dx1_ref[...] = dx1
        dx1b = dx1.astype(BF16)
        dx1b_ref[...] = dx1b
        dm = _dot(dx1b, wout_ref[...], NT)
        dpa_ref[...] = (dm * sa).astype(dpa_ref.dtype)
        dpb = (dm * sb).astype(BF16)
        dpb_ref[...] = dpb
        dgate_ref[:, :d] = (dm * pa * sa * (1.0 - sa)).astype(dgate_ref.dtype)
        dgate_ref[:, d:2 * d] = (dm * pb * sb * (1.0 - sb)).astype(dgate_ref.dtype)
        dyb = _dot(dpb, wb_ref[...], NT)
        zp = zp_ref[...].astype(F32)
        sg = _sigmoid(zp)
        mb = mx_ref[...].astype(F32) + mb_ref[...]
        scale = sc_ref[...]
        dyb0 = dyb * (zp * sg)
        dgate_ref[:, 2 * d:] = (dyb * (mb * scale) * (sg * (1.0 + zp * (1.0 - sg)))).astype(dgate_ref.dtype)
        dsc_ref[...] += jnp.sum(dyb0 * mb, axis=0, keepdims=True)
        dmx = dyb0 * scale
        dmb_ref[...] += jnp.sum(dmx, axis=0, keepdims=True)
        dmx_ref[...] = dmx.astype(dmx_ref.dtype)

    rows = lambda n: pl.BlockSpec((tm, n), lambda i: (i, 0))
    resident = lambda a: pl.BlockSpec(a.shape, lambda i: (0, 0), pipeline_mode=pl.Buffered(1))
    vec = lambda n: pl.BlockSpec((1, n), lambda i: (0, 0))
    bf = lambda: jax.ShapeDtypeStruct((t, d), BF16)
    return _pallas(
        body, name="merge_to_grads", grid=(t // tm,),
        in_specs=[rows(ya3.shape[1]), resident(wa), rows(yb.shape[1]), resident(wb),
                  pl.BlockSpec((tm, d), lambda i: (i, off_ga // d)), pl.BlockSpec((tm, d), lambda i: (i, off_gb // d)),
                  resident(wout), rows(d), vec(d), resident(wg), rows(pd), resident(wup), rows(d), vec(d),
                  pl.BlockSpec((tm, d), lambda i: (i, off_zp // d)), rows(d), vec(d), vec(d)],
        out_specs=[rows(d)] * 9 + [pl.BlockSpec((tm, 3 * d), lambda i: (i, off_ga // (3 * d))),
                                   vec(LANES), vec(d), vec(d), vec(d), vec(d)],
        out_shape=[bf(), bf(), jax.ShapeDtypeStruct((t, d), F32), bf(), bf(), bf(), bf(), bf(), bf(),
                   jax.ShapeDtypeStruct((t, n_main), BF16), jax.ShapeDtypeStruct((1, LANES), F32),
                   jax.ShapeDtypeStruct((1, d), F32), jax.ShapeDtypeStruct((1, d), F32),
                   jax.ShapeDtypeStruct((1, d), F32), jax.ShapeDtypeStruct((1, d), F32)],
        compiler_params=_cparams(("arbitrary",), 60),
    )(ya3, wa, yb, wb, proj, proj, wout, x, ple_g, wg, p, wup, target, final_g, proj, mixed, mix_b, scale)


def _dmixw(pooled, dmixed, ng):
    t, dp = pooled.shape
    pg = dp // ng
    tk = min(2048, t)
    nk = t // tk

    def body(a_ref, b_ref, o_ref):
        @pl.when(pl.program_id(1) == 0)
        def _():
            o_ref[...] = jnp.zeros_like(o_ref)

        o_ref[0] += _dot(a_ref[...], b_ref[...], TN)

    blk = pl.BlockSpec((tk, pg), lambda g, k: (k, g))
    return _pallas(
        body, name="dmixw", grid=(ng, nk),
        in_specs=[blk, blk],
        out_specs=pl.BlockSpec((1, pg, pg), lambda g, k: (g, 0, 0)),
        out_shape=jax.ShapeDtypeStruct((ng, pg, pg), F32),
        compiler_params=_cparams(("parallel", "arbitrary")),
    )(pooled, dmixed)


def _pool_bwd(dproj, dmixed, mixw, nb, s, off_u):
    dp = dmixed.shape[1]
    pg = dp // len(POOL_WINDOWS)

    def body(_, d_ref, w_ref, o_ref):
        g = pl.program_id(1)
        dv = _dot(d_ref[...], w_ref[0], NT)
        row = _iota(dv.shape, 0)
        pos = (_iota((dv.shape[0], 1), 0) + 1).astype(F32)
        for gi, w in enumerate(POOL_WINDOWS):
            @pl.when(g == gi)
            def _():
                acc, span = dv / jnp.minimum(pos, float(w)), 1
                while span < w:
                    acc = acc + _shift_up(acc, span, row)
                    span *= 2
                o_ref[...] = (acc - dv).astype(o_ref.dtype)

    return _pallas(
        body, name="pool_bwd", grid=(nb, len(POOL_WINDOWS)),
        in_specs=[ANY, pl.BlockSpec((s, pg), lambda b, g: (b, g)), pl.BlockSpec((1, pg, pg), lambda b, g: (g, 0, 0))],
        out_specs=pl.BlockSpec((s, pg), lambda b, g: (b, off_u // pg + g)),
        out_shape=jax.ShapeDtypeStruct(dproj.shape, dproj.dtype),
        input_output_aliases={0: 0},
        compiler_params=_cparams(("parallel", "parallel"), 48),
    )(dproj, dmixed, mixw)


def _ssd_bwd(dproj, dya3, y, xbc, proj, dtraw, prev, dt_bias, a_log, dskip_full, gnorm, nb, s, d_inner, partials):
    l = CHUNK
    nc = s // l
    n_heads = d_inner // HEAD_DIM
    gw = d_inner // GROUPS
    pairs_per_group = gw // LANES
    gn_cols = GROUPS * D_STATE
    n_conv = d_inner + 2 * gn_cols
    last_step = nb * nc - 1

    xch = _DirectExchange(partials)
    n_in, n_out = 13, 7

    def body(*refs):
        (_, dya3_ref, y_ref, xs_ref, bm_ref, cm_ref, z_ref, dtr_ref, prev_ref, bias_ref, alog_ref,
         dsk_ref, gn_ref) = refs[:n_in]
        p_refs = refs[n_in:n_in + xch.n]
        outs = refs[n_in + xch.n:]
        dz_ref, dxbc_ref, ddt_ref, dgn_ref, dskh_ref, dalog_ref, dbias_ref = outs[:n_out]
        got_refs = outs[n_out:n_out + xch.n]
        g_ref, dskacc_ref = outs[n_out + xch.n:n_out + xch.n + 2]
        sems = outs[n_out + xch.n + 2:]
        step = pl.program_id(0) * nc + pl.program_id(1)

        @pl.when(pl.program_id(1) == 0)
        def _():
            g_ref[...] = jnp.zeros_like(g_ref)

        @pl.when(step == 0)
        def _():
            xch.start(p_refs, got_refs, sems)
            dgn_ref[...] = jnp.zeros_like(dgn_ref)
            dalog_ref[...] = jnp.zeros_like(dalog_ref)
            dbias_ref[...] = jnp.zeros_like(dbias_ref)
            dskacc_ref[...] = jnp.zeros_like(dskacc_ref)

        xs = xs_ref[...].astype(F32)
        zv = z_ref[...].astype(F32)
        dsk = dsk_ref[...]
        sg = _sigmoid(zv)
        sz = zv * sg
        ya1 = y_ref[...].astype(F32) + xs * dsk
        ya2 = ya1 * sz
        parts = []
        for g in range(GROUPS):
            gsl = slice(g * gw, (g + 1) * gw)
            blk = ya2[:, gsl]
            rg = lax.rsqrt(jnp.mean(blk * blk, axis=-1, keepdims=True) + EPS)
            xh = blk * rg
            d3 = dya3_ref[:, gsl].astype(F32)
            dgn_ref[:, gsl] += jnp.sum(d3 * xh, axis=0, keepdims=True)
            dg = d3 * gn_ref[:, gsl]
            parts.append(rg * (dg - xh * jnp.mean(dg * xh, axis=-1, keepdims=True)))
        dya2 = jnp.concatenate(parts, axis=1)
        dy = dya2 * sz
        dz_ref[...] = (dya2 * ya1 * (sg * (1.0 + zv * (1.0 - sg)))).astype(dz_ref.dtype)
        dskacc_ref[...] += jnp.sum(dy * xs, axis=0, keepdims=True)

        dtraw_v = dtr_ref[...]
        bias = bias_ref[...]
        dt, apad, acs, dt_full, acs_full, tri, ex = _ssd_common(dtraw_v, bias, alog_ref[...], n_heads)
        acs_t = acs.T
        e_full = jnp.exp(acs_full)
        last = acs_full[l - 1:l, :]
        dte = jnp.exp(last - acs_full)
        cd = jnp.exp(last)
        xdt = xs * dt_full
        xd = xdt * dte
        dy_e = dy * e_full
        mask = _iota((l, l), 1) <= _iota((l, l), 0)
        lane = _iota((l, LANES), 1)
        hlane = _iota((l, HEAD_PAD), 1)
        hsub = _iota((HEAD_PAD, l), 0)
        dacs = jnp.zeros((l, HEAD_PAD), F32)
        dacs_t = jnp.zeros((HEAD_PAD, l), F32)
        dacs_full, dlast, dxdt_all, db_all, dc_all = [], [], [], [], []
        for g in range(GROUPS):
            gsl = slice(g * gw, (g + 1) * gw)
            bg = bm_ref[:, g * D_STATE:(g + 1) * D_STATE].astype(BF16)
            cg = cm_ref[:, g * D_STATE:(g + 1) * D_STATE].astype(BF16)
            prev_g = prev_ref[0, :, gsl]
            prev_b = prev_g.astype(BF16)
            gst = g_ref[:, gsl]
            gst_b = gst.astype(BF16)
            dye_b = dy_e[:, gsl].astype(BF16)
            dc = _dot(dye_b, prev_b, NT)
            dprev = _dot(cg, dye_b, TN)
            y_off = _dot(cg, prev_b) * e_full[:, gsl]
            xd_g = xd[:, gsl]
            db = _dot(xd_g.astype(BF16), gst_b, NT)
            dxd = _dot(bg, gst_b)
            wgt = dxd * xd_g
            dacs_full.append(dy[:, gsl] * y_off - wgt)
            dlast.append(jnp.sum(wgt, axis=0, keepdims=True)
                         + jnp.sum(prev_g * gst, axis=0, keepdims=True) * cd[:, gsl])
            g_ref[:, gsl] = dprev + cd[:, gsl] * gst
            dxdt_g = dxd * dte[:, gsl]
            cb = _dot(cg, bg, NT)
            dcb = jnp.zeros((l, l), F32)
            for q in range(pairs_per_group):
                pair = g * pairs_per_group + q
                psl = slice(pair * LANES, (pair + 1) * LANES)
                xpb = xdt[:, psl].astype(BF16)
                dyp = dy[:, psl].astype(BF16)
                zero = jnp.zeros_like(dyp)
                halves = (jnp.where(lane < HEAD_DIM, dyp, zero), jnp.where(lane >= HEAD_DIM, dyp, zero))
                acc = dxdt_g[:, q * LANES:(q + 1) * LANES]
                for k, dyh in enumerate(halves):
                    h = 2 * pair + k
                    dk = _decay_mats(acs, acs_t, h, mask)
                    mm = cb * dk
                    dm = _dot(dyh, xpb, NT)
                    dcb = dcb + dm * dk
                    dseg = dm * mm
                    dacs = dacs + jnp.where(hlane == h, jnp.sum(dseg, axis=1, keepdims=True), 0.0)
                    dacs_t = dacs_t + jnp.where(hsub == h, jnp.sum(dseg, axis=0, keepdims=True), 0.0)
                    acc = acc + _dot(mm.astype(BF16), dyh, TN)
                dxdt_all.append(acc)
            dcb_b = dcb.astype(BF16)
            dc_all.append(dc + _dot(dcb_b, bg))
            db_all.append(db + _dot(dcb_b, cg, TN))

        dxdt = jnp.concatenate(dxdt_all, axis=1)
        dxbc_ref[:, :d_inner] = (dy * dsk + dxdt * dt_full).astype(dxbc_ref.dtype)
        dxbc_ref[:, d_inner:d_inner + gn_cols] = jnp.concatenate(db_all, axis=1).astype(dxbc_ref.dtype)
        dxbc_ref[:, d_inner + gn_cols:] = jnp.concatenate(dc_all, axis=1).astype(dxbc_ref.dtype)
        ddt = _exact_r(dxdt * xs, ex, NT)
        rowi = _iota((l, d_inner), 0)
        dacs_f = jnp.concatenate(dacs_full, axis=1) + jnp.where(rowi == l - 1, jnp.concatenate(dlast, axis=1), 0.0)
        dacs = dacs + _exact_r(dacs_f, ex, NT) - dacs_t.T
        da = _exact_l(tri, dacs, TN)
        ddt = ddt + da * apad
        dalog_ref[...] += jnp.sum(da * dt, axis=0, keepdims=True) * apad
        ddt_raw = ddt * _sigmoid(dtraw_v + bias)
        dbias_ref[...] += jnp.sum(ddt_raw, axis=0, keepdims=True)
        ddt_ref[...] = ddt_raw.astype(ddt_ref.dtype)

        @pl.when(step == last_step)
        def _():
            dskh_ref[...] = _exact_r(dskacc_ref[...], ex, NT)[0:1, :]
            xch.wait(p_refs, got_refs, sems)

    row = lambda b, c: b * nc + (nc - 1 - c)
    small = lambda n: pl.BlockSpec((1, n), lambda b, c: (0, 0))
    wide = pl.BlockSpec((l, d_inner), lambda b, c: (row(b, c), 0))
    return _pallas(
        body, name="ssd_bwd", grid=(nb, nc),
        in_specs=[ANY, wide, wide, wide,
                  pl.BlockSpec((l, gn_cols), lambda b, c: (row(b, c), d_inner // gn_cols)),
                  pl.BlockSpec((l, gn_cols), lambda b, c: (row(b, c), d_inner // gn_cols + 1)),
                  wide, pl.BlockSpec((l, HEAD_PAD), lambda b, c: (row(b, c), 0)),
                  pl.BlockSpec((1, D_STATE, d_inner), lambda b, c: (row(b, c), 0, 0)),
                  small(HEAD_PAD), small(HEAD_PAD), small(d_inner), small(d_inner)] + xch.in_specs,
        out_specs=[wide, pl.BlockSpec((l, n_conv), lambda b, c: (row(b, c), 0)),
                   pl.BlockSpec((l, HEAD_PAD), lambda b, c: (row(b, c), 0)),
                   small(d_inner), small(HEAD_PAD), small(HEAD_PAD), small(HEAD_PAD)] + xch.out_specs,
        out_shape=[jax.ShapeDtypeStruct(dproj.shape, dproj.dtype),
                   jax.ShapeDtypeStruct((nb * s, n_conv), ACT),
                   jax.ShapeDtypeStruct((nb * s, HEAD_PAD), BF16),
                   jax.ShapeDtypeStruct((1, d_inner), F32), jax.ShapeDtypeStruct((1, HEAD_PAD), F32),
                   jax.ShapeDtypeStruct((1, HEAD_PAD), F32), jax.ShapeDtypeStruct((1, HEAD_PAD), F32)]
        + xch.out_shape,
        scratch_shapes=[pltpu.VMEM((D_STATE, d_inner), F32), pltpu.VMEM((8, d_inner), F32)] + xch.scratch,
        input_output_aliases={0: 0},
        compiler_params=_cparams(("arbitrary", "arbitrary"), 56),
    )(dproj, dya3, y, xbc, xbc, xbc, proj, dtraw, prev, dt_bias, a_log, dskip_full, gnorm, *partials)


def _conv_bwd(dproj, dxbc, conv_pre, proj, conv_w, nb, s, off_xbc):
    n_conv = dxbc.shape[1]
    cw = 256

    rc = CONV_ROWS

    def fold(v):
        return jnp.sum(v.reshape(rc // SUBLANES, SUBLANES, cw), axis=0)

    def body(_, d_ref, pre_ref, x_ref, w_ref, dx_ref, dw_ref, db_ref, dcpad_ref):
        @pl.when(pl.program_id(1) == 0)
        def _():
            dw_ref[...] = jnp.zeros_like(dw_ref)
            db_ref[...] = jnp.zeros_like(db_ref)

        dcpad_ref[pl.ds(s, SUBLANES), :] = jnp.zeros((SUBLANES, cw), F32)
        taps = [w_ref[k:k + 1, :] for k in range(CONV_W)]
        db = jnp.zeros((SUBLANES, cw), F32)
        for r in range(s // rc):
            rows = pl.ds(r * rc, rc)
            acc = pre_ref[rows, :].astype(F32)
            sg = _sigmoid(acc)
            dc = d_ref[rows, :].astype(F32) * (sg * (1.0 + acc * (1.0 - sg)))
            dcpad_ref[rows, :] = dc
            db = db + fold(dc)
        db_ref[...] += jnp.sum(db, axis=0, keepdims=True)
        dw = [jnp.zeros((SUBLANES, cw), F32) for _ in range(CONV_W)]
        for r in range(s // rc):
            xv = x_ref[pl.ds(r * rc, rc), :].astype(F32)
            win = dcpad_ref[pl.ds(r * rc, rc + SUBLANES), :]
            dxv = jnp.zeros((rc, cw), F32)
            for j in range(CONV_W):
                up = win[:rc, :] if j == 0 else pltpu.roll(win, rc + SUBLANES - j, 0)[:rc, :]
                dxv = dxv + up * taps[CONV_W - 1 - j]
                dw[CONV_W - 1 - j] = dw[CONV_W - 1 - j] + fold(up * xv)
            dx_ref[pl.ds(r * rc, rc), :] = dxv.astype(dx_ref.dtype)
        for k in range(CONV_W):
            dw_ref[k:k + 1, :] += jnp.sum(dw[k], axis=0, keepdims=True)

    blk = pl.BlockSpec((s, cw), lambda j, b: (b, j))
    return _pallas(
        body, name="conv_bwd", grid=(n_conv // cw, nb),
        in_specs=[ANY, blk, blk, pl.BlockSpec((s, cw), lambda j, b: (b, off_xbc // cw + j)),
                  pl.BlockSpec((CONV_W, cw), lambda j, b: (0, j))],
        out_specs=[pl.BlockSpec((s, cw), lambda j, b: (b, off_xbc // cw + j)),
                   pl.BlockSpec((CONV_W, cw), lambda j, b: (0, j)), pl.BlockSpec((1, cw), lambda j, b: (0, j))],
        out_shape=[jax.ShapeDtypeStruct(dproj.shape, dproj.dtype),
                   jax.ShapeDtypeStruct((CONV_W, n_conv), F32), jax.ShapeDtypeStruct((1, n_conv), F32)],
        scratch_shapes=[pltpu.VMEM((s + SUBLANES, cw), F32)],
        input_output_aliases={0: 0},
        compiler_params=_cparams(("parallel", "arbitrary"), 48),
    )(dproj, dxbc, conv_pre, proj, conv_w)


def _in_bwd(dproj, w_main, ddt, w_dt, x, norm_g, dx1, partials):
    t, d = x.shape
    n_main = dproj.shape[1]
    tm = 256
    ni = t // tm
    xch = _ChipExchange(partials)
    n_in, n_out = 7, 2

    def body(*refs):
        dp_ref, w_ref, ddt_ref, wdt_ref, x_ref, g_ref, dx1_ref = refs[:n_in]
        p_refs = refs[n_in:n_in + xch.n]
        outs = refs[n_in + xch.n:]
        dx_ref, dg_ref = outs[:n_out]
        got_refs = outs[n_out:n_out + xch.n]
        sems = outs[n_out + xch.n:]
        i = pl.program_id(0)

        @pl.when(i == 0)
        def _():
            xch.start(p_refs, got_refs, sems)
            dg_ref[...] = jnp.zeros_like(dg_ref)

        dh = _dot(dp_ref[...], w_ref[...], NT) + _dot(ddt_ref[...], wdt_ref[...], NT)
        xv = x_ref[...]
        r = lax.rsqrt(jnp.mean(xv * xv, axis=-1, keepdims=True) + EPS)
        xh = xv * r
        dg_ref[...] += jnp.sum(dh * xh, axis=0, keepdims=True)
        dg = dh * g_ref[...]
        dx_ref[...] = dx1_ref[...] + r * (dg - xh * jnp.mean(dg * xh, axis=-1, keepdims=True))

        @pl.when(i == ni - 1)
        def _():
            xch.wait(p_refs, got_refs, sems)

    rowblk = pl.BlockSpec((tm, d), lambda i: (i, 0))
    vec = pl.BlockSpec((1, d), lambda i: (0, 0))
    resident = lambda shape: pl.BlockSpec(shape, lambda i: (0, 0), pipeline_mode=pl.Buffered(1))
    return _pallas(
        body, name="in_bwd", grid=(ni,),
        in_specs=[pl.BlockSpec((tm, n_main), lambda i: (i, 0)), resident((d, n_main)),
                  pl.BlockSpec((tm, HEAD_PAD), lambda i: (i, 0)), resident((d, HEAD_PAD)),
                  rowblk, vec, rowblk] + xch.in_specs,
        out_specs=[rowblk, vec] + xch.out_specs,
        out_shape=[jax.ShapeDtypeStruct((t, d), F32), jax.ShapeDtypeStruct((1, d), F32)] + xch.out_shape,
        scratch_shapes=xch.scratch,
        compiler_params=_cparams(("arbitrary",), 56),
    )(dproj, w_main, ddt, w_dt, x, norm_g, dx1, *partials)


def _adamw_math(w, g, m, v):
    m = ADAM_B1 * m + (1.0 - ADAM_B1) * g
    v = ADAM_B2 * v + (1.0 - ADAM_B2) * (g * g)
    m_hat = m / (1.0 - ADAM_B1 ** ADAM_STEP)
    v_hat = v / (1.0 - ADAM_B2 ** ADAM_STEP)
    delta = -ADAM_LR * (m_hat / (jnp.sqrt(v_hat) + ADAM_EPS) + ADAM_WD * w)
    return delta, m, v


def _adamw(parts, w, m, v, name):
    n, rows, cols = parts.shape
    tr = _row_tile(rows, 256) if rows % 8 == 0 else rows

    def body(p_ref, w_ref, m_ref, v_ref, g_ref, d_ref, mo_ref, vo_ref):
        g = p_ref[0].astype(F32)
        for k in range(1, n):
            g = g + p_ref[k].astype(F32)
        d, mn, vn = _adamw_math(w_ref[...], g, m_ref[...], v_ref[...])
        g_ref[...] = g
        d_ref[...] = d
        mo_ref[...] = mn
        vo_ref[...] = vn

    blk = pl.BlockSpec((tr, cols), lambda i: (i, 0))
    shp = jax.ShapeDtypeStruct((rows, cols), F32)
    return _pallas(
        body, name=name, grid=(rows // tr,),
        in_specs=[pl.BlockSpec((n, tr, cols), lambda i: (0, i, 0)), blk, blk, blk],
        out_specs=[blk, blk, blk, blk], out_shape=[shp, shp, shp, shp],
        compiler_params=_cparams(("parallel",), 56),
    )(parts, w, m, v)


def _coords():
    return lax.axis_index("x"), lax.axis_index("y"), lax.axis_index("c")


class _TwoLevelGather:
    def __init__(self, blocks):
        self.n = len(blocks)
        self.in_specs = [ANY] * self.n
        self.out_specs = [ANY] * self.n
        self.out_shape = [jax.ShapeDtypeStruct((N_DEV,) + b.shape, b.dtype) for b in blocks]
        self.scratch = [pltpu.SemaphoreType.DMA((self.n, 7)), pltpu.SemaphoreType.DMA((self.n, 7)),
                        pltpu.SemaphoreType.DMA((self.n,))]

    def _parts(self, x_refs, out_refs, sems):
        send_sems, recv_sems, local_sems = sems
        x, y, c = _coords()
        me, sibling = (x, y, c), (x, y, 1 - c)
        chips = [(1 - x, y), (x, 1 - y), (1 - x, 1 - y)]

        def slot(a, px, py, pc):
            return out_refs[a].at[4 * px + 2 * py + pc]

        def copy(a, k, block, to, src=None):
            return pltpu.make_async_remote_copy(
                src_ref=slot(a, *block) if src is None else src, dst_ref=slot(a, *block),
                send_sem=send_sems.at[a, k], recv_sem=recv_sems.at[a, k], device_id=to, device_id_type=MESH)

        local = [pltpu.make_async_copy(x_refs[a], slot(a, *me), local_sems.at[a]) for a in range(self.n)]
        first = []
        for a in range(self.n):
            first.append(copy(a, 0, me, sibling, src=x_refs[a]))
            first += [copy(a, 1 + j, me, (*chip, c), src=x_refs[a]) for j, chip in enumerate(chips)]
        return copy, local, first, chips, me, sibling, c

    def start(self, x_refs, out_refs, sems):
        _, local, first, *_ = self._parts(x_refs, out_refs, sems)
        for cp in local + first:
            cp.start()

    def forward(self, x_refs, out_refs, sems):
        copy, _, _, chips, me, sibling, c = self._parts(x_refs, out_refs, sems)
        for j, chip in enumerate(chips):
            for a in range(self.n):
                copy(a, 1 + j, (*chip, c), me).wait_recv()
                copy(a, 4 + j, (*chip, c), sibling).start()

    def finish(self, x_refs, out_refs, sems):
        copy, local, first, chips, me, sibling, c = self._parts(x_refs, out_refs, sems)
        passed = [copy(a, 4 + j, (*chip, c), sibling) for j, chip in enumerate(chips) for a in range(self.n)]
        for a in range(self.n):
            copy(a, 0, sibling, me).wait_recv()
            for j, chip in enumerate(chips):
                copy(a, 4 + j, (*chip, 1 - c), me).wait_recv()
        for cp in first + passed:
            cp.wait_send()
        for cp in local:
            cp.wait()


def _rms_in_and_gather(xin, g, blk):
    rows, cols = blk.shape
    half = rows // 2
    assert half % 16 == 0
    t, d = xin.shape
    tt = 512
    ni = t // tt

    def exchange(x_ref, out_ref, send_sems, recv_sems, local_sem, begin):
        x, y, c = _coords()
        me, sib, xn, yn, dg = (x, y, c), (x, y, 1 - c), (1 - x, y, c), (x, 1 - y, c), (1 - x, 1 - y, c)
        other = lambda p: (p[0], p[1], 1 - c)
        top, bot = pl.ds(0, half), pl.ds(half, half)

        def slot(p, part=None):
            ref = out_ref.at[4 * p[0] + 2 * p[1] + p[2]]
            return ref if part is None else ref.at[part]

        def copy(k, block, to, part=None, src=None):
            return pltpu.make_async_remote_copy(
                src_ref=slot(block, part) if src is None else src, dst_ref=slot(block, part),
                send_sem=send_sems.at[k], recv_sem=recv_sems.at[k], device_id=to, device_id_type=MESH)

        mine = pltpu.make_async_copy(x_ref, slot(me), local_sem)
        own = [copy(1, me, xn, src=x_ref), copy(2, me, yn, src=x_ref), copy(0, me, sib, src=x_ref)]
        if begin:
            for cp in [mine] + own:
                cp.start()
            return
        copy(1, xn, me).wait_recv()
        fwd = [copy(3, xn, yn, top), copy(5, xn, sib)]
        for cp in fwd:
            cp.start()
        copy(2, yn, me).wait_recv()
        fwd += [copy(4, yn, xn, bot), copy(6, yn, sib)]
        for cp in fwd[2:]:
            cp.start()
        copy(3, dg, me, top).wait_recv()
        fwd.append(copy(7, dg, sib, top))
        fwd[-1].start()
        copy(4, dg, me, bot).wait_recv()
        fwd.append(copy(8, dg, sib, bot))
        fwd[-1].start()
        copy(0, sib, me).wait_recv()
        copy(5, other(xn), me).wait_recv()
        copy(6, other(yn), me).wait_recv()
        copy(7, other(dg), me, top).wait_recv()
        copy(8, other(dg), me, bot).wait_recv()
        for cp in own + fwd:
            cp.wait_send()
        mine.wait()

    def body(w_ref, x_ref, g_ref, out_ref, h_ref, send_sems, recv_sems, local_sem):
        i = pl.program_id(0)

        @pl.when(i == 0)
        def _():
            exchange(w_ref, out_ref, send_sems, recv_sems, local_sem, True)

        xv = x_ref[...]
        r = lax.rsqrt(jnp.mean(xv * xv, axis=-1, keepdims=True) + EPS)
        h_ref[...] = (xv * r * g_ref[...]).astype(h_ref.dtype)

        @pl.when(i == ni - 1)
        def _():
            exchange(w_ref, out_ref, send_sems, recv_sems, local_sem, False)

    return _pallas(
        body, name="rms_in_w_in_all_gather", grid=(ni,),
        in_specs=[ANY, pl.BlockSpec((tt, d), lambda i: (i, 0)), pl.BlockSpec((1, d), lambda i: (0, 0))],
        out_specs=[ANY, pl.BlockSpec((tt, d), lambda i: (i, 0))],
        out_shape=[jax.ShapeDtypeStruct((N_DEV, rows, cols), blk.dtype), jax.ShapeDtypeStruct((t, d), BF16)],
        scratch_shapes=[pltpu.SemaphoreType.DMA((9,)), pltpu.SemaphoreType.DMA((9,)), pltpu.SemaphoreType.DMA],
        compiler_params=_cparams(("arbitrary",)),
    )(blk, xin, g)


def _all_gather_direct(blk, name):
    rows, cols = blk.shape

    def body(x_ref, out_ref, send_sems, recv_sems, local_sem):
        x, y, c = _coords()
        me = 4 * x + 2 * y + c
        mine = pltpu.make_async_copy(x_ref, out_ref.at[me], local_sem)
        mine.start()
        copies = []
        for k in range(1, N_DEV):
            fx, fy, fc = (k >> 2) & 1, (k >> 1) & 1, k & 1
            peer = (x ^ fx, y ^ fy, c ^ fc)
            copies.append(pltpu.make_async_remote_copy(
                src_ref=x_ref, dst_ref=out_ref.at[me], send_sem=send_sems.at[k - 1], recv_sem=recv_sems.at[k - 1],
                device_id=peer, device_id_type=MESH))
        for cp in copies:
            cp.start()
        for k in range(1, N_DEV):
            fx, fy, fc = (k >> 2) & 1, (k >> 1) & 1, k & 1
            src = 4 * (x ^ fx) + 2 * (y ^ fy) + (c ^ fc)
            pltpu.make_async_remote_copy(
                src_ref=x_ref, dst_ref=out_ref.at[src], send_sem=send_sems.at[k - 1], recv_sem=recv_sems.at[k - 1],
                device_id=(x, y, c), device_id_type=MESH).wait_recv()
        for cp in copies:
            cp.wait_send()
        mine.wait()

    return _pallas(
        body, name=name, in_specs=[ANY], out_specs=ANY,
        out_shape=jax.ShapeDtypeStruct((N_DEV, rows, cols), blk.dtype),
        scratch_shapes=[pltpu.SemaphoreType.DMA((7,)), pltpu.SemaphoreType.DMA((7,)), pltpu.SemaphoreType.DMA],
    )(blk)


def _pair_exchange(grads, name):
    n = len(grads)

    def body(*refs):
        g_refs, out_refs = refs[:n], refs[n:2 * n]
        send_sems, recv_sems = refs[2 * n:]
        x, y, c = _coords()
        copies = []
        for a in range(n):
            for k in range(4):
                copies.append(pltpu.make_async_remote_copy(
                    src_ref=g_refs[a].at[2 * k + (1 - c)], dst_ref=out_refs[a].at[k],
                    send_sem=send_sems.at[a, k], recv_sem=recv_sems.at[a, k],
                    device_id=(x, y, 1 - c), device_id_type=MESH))
        for cp in copies:
            cp.start()
        for cp in copies:
            cp.wait()

    return _pallas(
        body, name=name, in_specs=[ANY] * n, out_specs=[ANY] * n,
        out_shape=[jax.ShapeDtypeStruct((4,) + g.shape[1:], g.dtype) for g in grads],
        scratch_shapes=[pltpu.SemaphoreType.DMA((n, 4)), pltpu.SemaphoreType.DMA((n, 4))],
    )(*grads)


def _pair_add(grads, got, core, out_dtype, name):
    _, rows, cols = grads.shape
    tr = _row_tile(rows, 256) if rows % 8 == 0 else rows

    def body(c_ref, a_ref, b_ref, o_ref):
        o_ref[...] = (a_ref[...].astype(F32) + b_ref[...].astype(F32)).astype(o_ref.dtype)

    grid_spec = pltpu.PrefetchScalarGridSpec(
        num_scalar_prefetch=1, grid=(4, rows // tr),
        in_specs=[pl.BlockSpec((1, tr, cols), lambda k, i, c_ref: (2 * k + c_ref[0], i, 0)),
                  pl.BlockSpec((1, tr, cols), lambda k, i, c_ref: (k, i, 0))],
        out_specs=pl.BlockSpec((1, tr, cols), lambda k, i, c_ref: (k, i, 0)))
    return _pallas(
        body, name=name, grid_spec=grid_spec,
        out_shape=jax.ShapeDtypeStruct((4, rows, cols), out_dtype),
        compiler_params=_cparams(("parallel", "parallel"), 48),
    )(core, grads, got)


class _ChipExchange:
    def __init__(self, partials):
        self.n = len(partials)
        self.in_specs = [ANY] * self.n
        self.out_specs = [ANY] * self.n
        self.out_shape = [jax.ShapeDtypeStruct(p.shape, p.dtype) for p in partials]
        self.scratch = [pltpu.SemaphoreType.DMA((self.n, 3)), pltpu.SemaphoreType.DMA((self.n, 3)),
                        pltpu.SemaphoreType.DMA((self.n,))]

    def _copies(self, p_refs, out_refs, sems, receiving):
        send_sems, recv_sems, local_sems = sems
        x, y, c = _coords()
        mychip = 2 * x + y
        local = [pltpu.make_async_copy(p_refs[a].at[mychip], out_refs[a].at[mychip], local_sems.at[a])
                 for a in range(self.n)]
        remote = []
        for a in range(self.n):
            for k in range(1, 4):
                px, py = x ^ ((k >> 1) & 1), y ^ (k & 1)
                peer = 2 * px + py
                remote.append(pltpu.make_async_remote_copy(
                    src_ref=p_refs[a].at[peer], dst_ref=out_refs[a].at[peer if receiving else mychip],
                    send_sem=send_sems.at[a, k - 1], recv_sem=recv_sems.at[a, k - 1],
                    device_id=(x, y, c) if receiving else (px, py, c), device_id_type=MESH))
        return local, remote

    def start(self, p_refs, out_refs, sems):
        local, sends = self._copies(p_refs, out_refs, sems, False)
        for cp in local + sends:
            cp.start()

    def wait(self, p_refs, out_refs, sems):
        local, recvs = self._copies(p_refs, out_refs, sems, True)
        for cp in recvs:
            cp.wait_recv()
        for cp in recvs:
            cp.wait_send()
        for cp in local:
            cp.wait()


class _DirectExchange:
    def __init__(self, partials):
        self.n = len(partials)
        self.in_specs = [ANY] * self.n
        self.out_specs = [ANY] * self.n
        self.out_shape = [jax.ShapeDtypeStruct(p.shape, p.dtype) for p in partials]
        self.scratch = [pltpu.SemaphoreType.DMA((self.n, N_DEV - 1)), pltpu.SemaphoreType.DMA((self.n, N_DEV - 1)),
                        pltpu.SemaphoreType.DMA((self.n,))]

    def _copies(self, p_refs, out_refs, sems, receiving):
        send_sems, recv_sems, local_sems = sems
        x, y, c = _coords()
        me = 4 * x + 2 * y + c
        local = [pltpu.make_async_copy(p_refs[a].at[me], out_refs[a].at[me], local_sems.at[a])
                 for a in range(self.n)]
        remote = []
        for a in range(self.n):
            for k in range(1, N_DEV):
                px, py, pc = x ^ ((k >> 2) & 1), y ^ ((k >> 1) & 1), c ^ (k & 1)
                peer = 4 * px + 2 * py + pc
                remote.append(pltpu.make_async_remote_copy(
                    src_ref=p_refs[a].at[peer], dst_ref=out_refs[a].at[peer if receiving else me],
                    send_sem=send_sems.at[a, k - 1], recv_sem=recv_sems.at[a, k - 1],
                    device_id=(x, y, c) if receiving else (px, py, pc), device_id_type=MESH))
        return local, remote

    def start(self, p_refs, out_refs, sems):
        local, sends = self._copies(p_refs, out_refs, sems, False)
        for cp in local + sends:
            cp.start()

    def wait(self, p_refs, out_refs, sems):
        local, recvs = self._copies(p_refs, out_refs, sems, True)
        for cp in recvs:
            cp.wait_recv()
        for cp in recvs:
            cp.wait_send()
        for cp in local:
            cp.wait()


def _pad_lanes(v, n=HEAD_PAD):
    return jnp.pad(v, ((0, 0), (0, n - v.shape[1])))


def kernel(x, p, norm_g, w_in, conv_w, conv_b, dt_bias, a_log, d_skip, gnorm_g, pool_mix_w, pool_mix_b, pool_scale, w_branch_a, w_branch_b, w_out, ple_norm_g, w_ple_gate, w_ple_up, final_g, loss_target, m_norm_g, m_w_in, m_conv_w, m_conv_b, m_dt_bias, m_a_log, m_d_skip, m_gnorm_g, m_pool_mix_w, m_pool_mix_b, m_pool_scale, m_w_branch_a, m_w_branch_b, m_w_out, m_ple_norm_g, m_w_ple_gate, m_w_ple_up, m_final_g, v_norm_g, v_w_in, v_conv_w, v_conv_b, v_dt_bias, v_a_log, v_d_skip, v_gnorm_g, v_pool_mix_w, v_pool_mix_b, v_pool_scale, v_w_branch_a, v_w_branch_b, v_w_out, v_ple_norm_g, v_w_ple_gate, v_w_ple_up, v_final_g):
    nb, s, d = x.shape
    t = nb * s
    n_heads = dt_bias.shape[1]
    d_inner = n_heads * HEAD_DIM
    n_conv = conv_b.shape[1]
    d_pool = pool_mix_b.shape[1]
    n_in = w_in.shape[2] * N_DEV
    ple_dim = p.shape[-1]
    o_xbc, o_dt = d_inner, d_inner + n_conv
    o_u = o_dt + n_heads
    o_zp = o_u + d_pool
    o_ga, o_gb = o_zp + d_pool, o_zp + d_pool + d
    n_main = n_in - n_heads
    off_u = d_inner
    off_ga = off_u + d_pool
    off_gb, off_zp = off_ga + d, off_ga + 2 * d
    off_xbc = off_zp + d_pool

    sharded = [w_in[0], conv_w[0], pool_mix_w[0], w_branch_a[0], w_branch_b[0], w_out[0], w_ple_gate[0], w_ple_up[0]]
    sharded_m = [m_w_in[0], m_conv_w[0], m_pool_mix_w[0], m_w_branch_a[0], m_w_branch_b[0], m_w_out[0],
                 m_w_ple_gate[0], m_w_ple_up[0]]
    sharded_v = [v_w_in[0], v_conv_w[0], v_pool_mix_w[0], v_w_branch_a[0], v_w_branch_b[0], v_w_out[0],
                 v_w_ple_gate[0], v_w_ple_up[0]]
    ng = pool_mix_w.shape[1]
    pg = d_pool // ng
    two_d = lambda w: w.reshape(-1, w.shape[-1])
    sharded, sharded_m, sharded_v = ([two_d(w) for w in ws] for ws in (sharded, sharded_m, sharded_v))

    send = [w if i == 1 else w.astype(BF16) for i, w in enumerate(sharded)]
    x2 = x.reshape(t, d)
    chunk = n_in // N_DEV
    segments = [((0, o_xbc), 0), ((o_u, o_zp), off_u), ((o_ga, o_gb), off_ga), ((o_gb, n_in), off_gb),
                ((o_zp, o_ga), off_zp), ((o_xbc, o_dt), off_xbc)]

    def pieces_of(dev):
        lo, hi = dev * chunk, (dev + 1) * chunk
        out = []
        for (olo, ohi), off in sorted(segments + [((o_dt, o_u), None)], key=lambda e: e[0][0]):
            a, b = max(lo, olo), min(hi, ohi)
            if a < b:
                out.append((a - lo, b - lo, None if off is None else off + a - olo))
        return out

    def lane_cover(a, b, off):
        left = 0 if off is None else off % LANES
        return left, -(-(left + b - a) // LANES) * LANES

    send_width = max(sum(lane_cover(*pc)[1] for pc in pieces_of(dev)) for dev in range(N_DEV))

    def aligned_shard(dev):
        def build(w):
            bufs = []
            for a, b, off in pieces_of(dev):
                left, width = lane_cover(a, b, off)
                bufs.append(jnp.pad(w[:, a:b], ((0, 0), (left, width - left - (b - a)))))
            out = jnp.concatenate(bufs, axis=1)
            return jnp.pad(out, ((0, 0), (0, send_width - out.shape[1])))
        return build

    me_lin = 4 * lax.axis_index("x") + 2 * lax.axis_index("y") + lax.axis_index("c")
    w_in_sent = lax.switch(me_lin, [aligned_shard(dev) for dev in range(N_DEV)], send[0])
    w_in_all, h = _rms_in_and_gather(x2, norm_g, w_in_sent)
    seg = [w_in_all]
    w_main = jnp.zeros((d, n_main), BF16)
    w_dt = None
    for dev in range(N_DEV):
        q = 0
        for a, b, off in pieces_of(dev):
            left, width = lane_cover(a, b, off)
            cover = w_in_all[dev, :, q:q + width]
            q += width
            if off is None:
                w_dt = cover
            else:
                start = off - left
                w_main = w_main + jnp.pad(cover, ((0, 0), (start, n_main - start - width)))

    proj, dtraw, *rest = _in_proj(h, w_main, w_dt, send[1:])
    seg += rest
    cw_full = seg[1].transpose(1, 0, 2).reshape(CONV_W, n_conv)
    mixw_full = seg[2].reshape(N_DEV, ng, pg // N_DEV, pg).transpose(1, 0, 2, 3).reshape(ng, pg, pg)
    wa_full = seg[3].reshape(d_inner, d)
    wb_full = seg[4].reshape(d_pool, d)
    wout_full = seg[5].reshape(d, d)
    wg_full = seg[6].reshape(d, d)
    wup_full = seg[7].transpose(1, 0, 2).reshape(ple_dim, d)

    p2 = p.reshape(t, ple_dim)
    tgt = loss_target.reshape(t, d)
    bias_p, alog_p = _pad_lanes(dt_bias), _pad_lanes(a_log)
    dskip_full = jnp.repeat(d_skip, HEAD_DIM, axis=1)

    xbc, conv_pre = _conv_fwd(proj, cw_full, conv_b, nb, s, off_xbc, n_conv)
    y, ya3, prev = _ssd_fwd(xbc, proj, dtraw, bias_p, alog_p, dskip_full, gnorm_g, nb, s, d_inner)
    pooled, mixed, yb = _pool_fwd(proj, mixw_full, pool_mix_b, pool_scale, nb, s, off_u, off_zp, d_pool)
    (merged, h2, dx1, dx1b, dgl, dpu, dpa, dpb, dmixed, dproj, loss_part, dfinal_g, d_ple_g, d_scale,
     d_mixb) = _merge_to_grads(
        ya3, wa_full, yb, wb_full, proj, off_ga, off_gb, off_zp, wout_full, x2, ple_norm_g, wg_full, p2, wup_full,
        tgt, final_g.reshape(1, d), mixed, pool_mix_b, pool_scale, n_main)

    g_wup = _matmul(p2, dpu, "tn", BF16, "dw_ple_up", 256, 512, t)
    g_wg = _matmul(h2, dgl, "tn", BF16, "dw_ple_gate", 512, 1024, t)
    g_wout = _matmul(merged, dx1b, "tn", BF16, "dw_out", 512, 1024, t)
    g_wa = _matmul(ya3, dpa, "tn", BF16, "dw_branch_a", 512, 1024, t)
    g_wb = _matmul(yb, dpb, "tn", BF16, "dw_branch_b", 512, 1024, t)
    dya3 = _matmul(dpa, wa_full, "nt", ACT, "d_ya3", 1024, 1024, d)
    g_mixw = _dmixw(pooled, dmixed, ng)
    dproj = _pool_bwd(dproj, dmixed, mixw_full, nb, s, off_u)

    core = lax.axis_index("c").astype(I32).reshape(1)

    def pair_sums(by_dest, names):
        by_dest = [g.astype(BF16) for g in by_dest]
        got = _pair_exchange(by_dest, "rs_pair_exchange_" + names[0])
        return [_pair_add(g, r, core, BF16, "rs_pair_add_" + nm) for g, r, nm in zip(by_dest, got, names)]

    names_a = ["pool_mix_w", "w_branch_a", "w_branch_b", "w_out", "w_ple_gate", "w_ple_up"]
    partials_a = [
        g_mixw.reshape(ng, N_DEV, pg // N_DEV, pg).transpose(1, 0, 2, 3).reshape(N_DEV, ng * pg // N_DEV, pg)
        .astype(BF16),
        g_wa.reshape(N_DEV, d_inner // N_DEV, d), g_wb.reshape(N_DEV, d_pool // N_DEV, d),
        g_wout.reshape(N_DEV, d // N_DEV, d), g_wg.reshape(N_DEV, d // N_DEV, d),
        g_wup.reshape(ple_dim, N_DEV, d // N_DEV).transpose(1, 0, 2)]
    dproj, dxbc, ddt, d_gn, d_dsk, d_alog, d_dtb, *parts_a = _ssd_bwd(
        dproj, dya3, y, xbc, proj, dtraw, prev, bias_p, alog_p, dskip_full, gnorm_g, nb, s, d_inner, partials_a)
    dproj, g_cw, d_cb = _conv_bwd(dproj, dxbc, conv_pre, proj, cw_full, nb, s, off_xbc)
    g_wmain, g_wdt = _dw_in(h, dproj, ddt)

    def dest_cols(dev):
        lo, hi = dev * chunk, (dev + 1) * chunk
        srcs = [((olo, ohi), g_wmain, off) for (olo, ohi), off in segments] + [((o_dt, o_u), g_wdt, 0)]
        out = []
        for (olo, ohi), arr, off in sorted(srcs, key=lambda e: e[0][0]):
            a, b = max(lo, olo), min(hi, ohi)
            if a < b:
                out.append(arr[:, off + a - olo:off + b - olo])
        return jnp.concatenate(out, axis=1)

    names_b = ["w_in", "conv_w"]
    partials_b = pair_sums([jnp.stack([dest_cols(dev) for dev in range(N_DEV)], axis=0),
                            g_cw.reshape(CONV_W, N_DEV, n_conv // N_DEV).transpose(1, 0, 2)], names_b)
    grad_x, d_ng, *parts_b = _in_bwd(dproj, w_main, ddt, w_dt, x2, norm_g, dx1, partials_b)

    names = names_b + names_a
    parts = parts_b + parts_a
    sh_g, sh_d, sh_m, sh_v = [], [], [], []
    shapes = [w_in.shape, conv_w.shape, pool_mix_w.shape, w_branch_a.shape, w_branch_b.shape, w_out.shape,
              w_ple_gate.shape, w_ple_up.shape]
    for pt, w, m, v, nm, shp in zip(parts, sharded, sharded_m, sharded_v, names, shapes):
        res = _adamw(pt, w, m, v, "adamw_" + nm)
        for dst, r in zip((sh_g, sh_d, sh_m, sh_v), res):
            dst.append(r.reshape(shp))

    rep_w = [norm_g, conv_b, gnorm_g, pool_mix_b, pool_scale, ple_norm_g, final_g.reshape(1, d),
             bias_p, alog_p, _pad_lanes(d_skip)]
    rep_m = [m_norm_g, m_conv_b, m_gnorm_g, m_pool_mix_b, m_pool_scale, m_ple_norm_g, m_final_g.reshape(1, d),
             _pad_lanes(m_dt_bias), _pad_lanes(m_a_log), _pad_lanes(m_d_skip)]
    rep_v = [v_norm_g, v_conv_b, v_gnorm_g, v_pool_mix_b, v_pool_scale, v_ple_norm_g, v_final_g.reshape(1, d),
             _pad_lanes(v_dt_bias), _pad_lanes(v_a_log), _pad_lanes(v_d_skip)]
    rep_g = [d_ng, d_cb, d_gn, d_mixb, d_scale, d_ple_g, dfinal_g, d_dtb, d_alog, d_dsk]
    zero = jnp.zeros((1, LANES), F32)
    small = _all_gather_direct(jnp.concatenate(rep_g + [loss_part], axis=1), "small_all_gather")
    pk = lambda ws: jnp.concatenate(ws + [zero], axis=1)
    rg, rd, rm, rv = _adamw(small, pk(rep_w), pk(rep_m), pk(rep_v), "adamw_replicated")
    loss = rg[0, sum(int(w.size) for w in rep_w)]

    def unpack_rep(buf):
        out, pos_ = [], 0
        for w in rep_w:
            out.append(buf[:, pos_:pos_ + w.size])
            pos_ += w.size
        return out

    def arrange(sh, rep):
        rep = list(rep)
        rep[6] = rep[6].reshape(d)
        for i in (7, 8, 9):
            rep[i] = rep[i][:, :n_heads]
        return [rep[0], sh[0], sh[1], rep[1], rep[7], rep[8], rep[9], rep[2], sh[2], rep[3], rep[4],
                sh[3], sh[4], sh[5], rep[5], sh[6], sh[7], rep[6]]

    outs = [loss, grad_x.reshape(nb, s, d)]
    for sh, rep in ((sh_g, rg), (sh_d, rd), (sh_m, rm), (sh_v, rv)):
        outs += arrange(sh, unpack_rep(rep))
    return tuple(outs)
```

```python
import jax
import jax.numpy as jnp
from jax import lax
from jax.experimental import pallas as pl
from jax.experimental.pallas import tpu as pltpu

F32 = jnp.float32
BF16 = jnp.bfloat16
I32 = jnp.int32
MESH = pl.DeviceIdType.MESH
ANY = pl.BlockSpec(memory_space=pl.ANY)

EPS = 1e-6
CHUNK = 128
HEAD_DIM = 64
D_STATE = 128
GROUPS = 4
CONV_W = 4
POOL_WINDOWS = (2, 4, 8, 16)
ACT = jnp.bfloat16
LANES = 128
SUBLANES = 8
CONV_ROWS = 64
HEAD_PAD = 128
NEG = -1e30
N_DEV = 8

ADAM_LR = 0.001
ADAM_B1 = 0.9
ADAM_B2 = 0.999
ADAM_EPS = 1e-08
ADAM_WD = 0.01
ADAM_STEP = 10

NN = (((1,), (0,)), ((), ()))
NT = (((1,), (1,)), ((), ()))
TN = (((0,), (0,)), ((), ()))


def _pallas(body, **kw):
    return pl.pallas_call(body, **kw)


def _cparams(dims=None, vmem_mb=None):
    kw = {}
    if dims is not None:
        kw["dimension_semantics"] = dims
    if vmem_mb is not None:
        kw["vmem_limit_bytes"] = vmem_mb << 20
    return pltpu.CompilerParams(**kw)


def _dot(a, b, dims=NN):
    return lax.dot_general(a, b, dims, preferred_element_type=F32)


def _split3(v):
    hi = v.astype(BF16)
    r = v - hi.astype(F32)
    mid = r.astype(BF16)
    lo = (r - mid.astype(F32)).astype(BF16)
    return hi, mid, lo


def _exact_r(v, m, dims=NN):
    hi, mid, lo = _split3(v)
    return _dot(hi, m, dims) + _dot(mid, m, dims) + _dot(lo, m, dims)


def _exact_l(m, v, dims=NN):
    hi, mid, lo = _split3(v)
    return _dot(m, hi, dims) + _dot(m, mid, dims) + _dot(m, lo, dims)


def _row_tile(rows, target):
    best = None
    for cand in range(8, min(rows, target) + 1, 8):
        if rows % cand == 0:
            best = cand
    assert best is not None, rows
    return best


def _sum_all(v):
    return jnp.sum(jnp.sum(v, axis=1, keepdims=True), axis=0, keepdims=True)


def _iota(shape, dim):
    return lax.broadcasted_iota(I32, shape, dim)


def _sigmoid(x):
    return 0.5 * jnp.tanh(0.5 * x) + 0.5


def _softplus(x):
    return jnp.maximum(x, 0.0) + jnp.log(1.0 + jnp.exp(-jnp.abs(x)))


def _shift_down(v, j, row):
    return jnp.where(row >= j, pltpu.roll(v, j, 0), 0.0)


def _shift_up(v, j, row):
    n = v.shape[0]
    return jnp.where(row < n - j, pltpu.roll(v, n - j, 0), 0.0)


def _matmul(a, b, mode, out_dtype, name, tm, tn, tk):
    if mode == "nn":
        (m, k), n = a.shape, b.shape[1]
    elif mode == "nt":
        (m, k), n = a.shape, b.shape[0]
    else:
        (k, m), n = a.shape, b.shape[1]
    tm, tn, tk = min(tm, m), min(tn, n), min(tk, k)
    assert m % tm == 0 and n % tn == 0 and k % tk == 0, (name, m, n, k)
    nk = k // tk
    dims = {"nn": NN, "nt": NT, "tn": TN}[mode]

    def body_acc(a_ref, b_ref, o_ref, acc_ref):
        kk = pl.program_id(2)

        @pl.when(kk == 0)
        def _():
            acc_ref[...] = jnp.zeros_like(acc_ref)

        acc_ref[...] += _dot(a_ref[...].astype(BF16), b_ref[...].astype(BF16), dims)

        @pl.when(kk == nk - 1)
        def _():
            o_ref[...] = acc_ref[...].astype(o_ref.dtype)

    def body_one(a_ref, b_ref, o_ref):
        o_ref[...] = _dot(a_ref[...].astype(BF16), b_ref[...].astype(BF16), dims).astype(o_ref.dtype)

    body = body_one if nk == 1 else body_acc
    if mode == "tn":
        a_spec = pl.BlockSpec((tk, tm), lambda i, j, kk: (kk, i))
    else:
        a_spec = pl.BlockSpec((tm, tk), lambda i, j, kk: (i, kk))
    if mode == "nt":
        b_spec = pl.BlockSpec((tn, tk), lambda i, j, kk: (j, kk))
    else:
        b_spec = pl.BlockSpec((tk, tn), lambda i, j, kk: (kk, j))
    return _pallas(
        body, name=name, grid=(m // tm, n // tn, nk),
        in_specs=[a_spec, b_spec],
        out_specs=pl.BlockSpec((tm, tn), lambda i, j, kk: (i, j)),
        out_shape=jax.ShapeDtypeStruct((m, n), out_dtype),
        scratch_shapes=[] if nk == 1 else [pltpu.VMEM((tm, tn), F32)],
        compiler_params=_cparams(("parallel", "parallel", "arbitrary"), 56),
    )(a, b)


def _in_proj(h, w_main, w_dt, blocks):
    t, d = h.shape
    n = w_main.shape[1]
    tm, tn = min(1024, t), 1024
    ni, nj = t // tm, n // tn
    gather = _TwoLevelGather(blocks)

    def body(*refs):
        a_ref, b_ref, wdt_ref = refs[:3]
        x_refs = refs[3:3 + gather.n]
        o_ref, dt_ref = refs[3 + gather.n:5 + gather.n]
        got_refs = refs[5 + gather.n:5 + 2 * gather.n]
        sems = refs[5 + 2 * gather.n:]
        i, j = pl.program_id(0), pl.program_id(1)

        @pl.when((i == 0) & (j == 0))
        def _():
            gather.start(x_refs, got_refs, sems)

        o_ref[...] = _dot(a_ref[...], b_ref[...]).astype(o_ref.dtype)

        @pl.when(j == 0)
        def _():
            dt_ref[...] = _dot(a_ref[...], wdt_ref[...])

        @pl.when((i == ni - 1) & (j == max(nj - 5, 0)))
        def _():
            gather.forward(x_refs, got_refs, sems)

        @pl.when((i == ni - 1) & (j == nj - 1))
        def _():
            gather.finish(x_refs, got_refs, sems)

    return _pallas(
        body, name="in_proj", grid=(ni, nj),
        in_specs=[pl.BlockSpec((tm, d), lambda i, j: (i, 0)), pl.BlockSpec((d, tn), lambda i, j: (0, j)),
                  pl.BlockSpec((d, HEAD_PAD), lambda i, j: (0, 0))] + gather.in_specs,
        out_specs=[pl.BlockSpec((tm, tn), lambda i, j: (i, j)), pl.BlockSpec((tm, HEAD_PAD), lambda i, j: (i, 0))]
        + gather.out_specs,
        out_shape=[jax.ShapeDtypeStruct((t, n), ACT), jax.ShapeDtypeStruct((t, HEAD_PAD), F32)] + gather.out_shape,
        scratch_shapes=gather.scratch,
        compiler_params=_cparams(("arbitrary", "arbitrary"), 56),
    )(h, w_main, w_dt, *blocks)


def _dw_in(h, dproj, ddt):
    t, d = h.shape
    n = dproj.shape[1]
    tm, tn = 512, 1024

    def body(a_ref, b_ref, ddt_ref, o_ref, odt_ref):
        a = a_ref[...]
        o_ref[...] = _dot(a, b_ref[...], TN).astype(o_ref.dtype)

        @pl.when(pl.program_id(1) == 0)
        def _():
            odt_ref[...] = _dot(a, ddt_ref[...], TN).astype(odt_ref.dtype)

    return _pallas(
        body, name="dw_in", grid=(d // tm, n // tn),
        in_specs=[pl.BlockSpec((t, tm), lambda i, j: (0, i)), pl.BlockSpec((t, tn), lambda i, j: (0, j)),
                  pl.BlockSpec((t, HEAD_PAD), lambda i, j: (0, 0))],
        out_specs=[pl.BlockSpec((tm, tn), lambda i, j: (i, j)), pl.BlockSpec((tm, HEAD_PAD), lambda i, j: (i, 0))],
        out_shape=[jax.ShapeDtypeStruct((d, n), BF16), jax.ShapeDtypeStruct((d, HEAD_PAD), BF16)],
        compiler_params=_cparams(("parallel", "arbitrary"), 56),
    )(h, dproj, ddt)


def _conv_fwd(proj, conv_w, conv_b, nb, s, off_xbc, n_conv):
    cw = 256
    rc = CONV_ROWS

    def body(x_ref, w_ref, b_ref, o_ref, pre_ref, pad_ref):
        pad_ref[0:SUBLANES, :] = jnp.zeros((SUBLANES, cw), F32)
        pad_ref[SUBLANES:, :] = x_ref[...].astype(F32)
        taps = [w_ref[k:k + 1, :] for k in range(CONV_W)]
        bias = b_ref[...]
        for r in range(s // rc):
            win = pad_ref[pl.ds(r * rc, rc + SUBLANES), :]
            acc = bias + win[SUBLANES:, :] * taps[CONV_W - 1]
            for j in range(1, CONV_W):
                acc = acc + pltpu.roll(win, j, 0)[SUBLANES:, :] * taps[CONV_W - 1 - j]
            pre_ref[pl.ds(r * rc, rc), :] = acc.astype(pre_ref.dtype)
            o_ref[pl.ds(r * rc, rc), :] = (acc * _sigmoid(acc)).astype(o_ref.dtype)

    out = pl.BlockSpec((s, cw), lambda b, j: (b, j))
    return _pallas(
        body, name="conv_fwd", grid=(nb, n_conv // cw),
        in_specs=[pl.BlockSpec((s, cw), lambda b, j: (b, off_xbc // cw + j)),
                  pl.BlockSpec((CONV_W, cw), lambda b, j: (0, j)),
                  pl.BlockSpec((1, cw), lambda b, j: (0, j))],
        out_specs=[out, out],
        out_shape=[jax.ShapeDtypeStruct((nb * s, n_conv), ACT), jax.ShapeDtypeStruct((nb * s, n_conv), ACT)],
        scratch_shapes=[pltpu.VMEM((s + SUBLANES, cw), F32)],
        compiler_params=_cparams(("parallel", "parallel"), 48),
    )(proj, conv_w, conv_b)


def _ssd_common(dtraw, bias, alog, n_heads):
    l = dtraw.shape[0]
    lane = _iota((l, HEAD_PAD), 1)
    lane1 = _iota((1, HEAD_PAD), 1)
    apad = jnp.where(lane1 < n_heads, -jnp.exp(alog), 0.0)
    dt = jnp.where(lane < n_heads, _softplus(dtraw + bias), 0.0)
    a = dt * apad
    tri = (_iota((l, l), 1) <= _iota((l, l), 0)).astype(BF16)
    acs = _exact_l(tri, a)
    width = n_heads * HEAD_DIM
    ex = (_iota((HEAD_PAD, width), 1) // HEAD_DIM == _iota((HEAD_PAD, width), 0)).astype(BF16)
    dt_full = _exact_r(dt, ex)
    acs_full = _exact_r(acs, ex)
    return dt, apad, acs, dt_full, acs_full, tri, ex


def _decay_mats(acs, acs_t, h, mask):
    col = acs[:, h:h + 1]
    row = acs_t[h:h + 1, :]
    return jnp.exp(jnp.where(mask, col - row, NEG))


def _ssd_fwd(xbc, proj, dtraw, dt_bias, a_log, dskip_full, gnorm, nb, s, d_inner):
    l = CHUNK
    nc = s // l
    n_heads = d_inner // HEAD_DIM
    gw = d_inner // GROUPS
    pairs_per_group = gw // LANES
    gn_cols = GROUPS * D_STATE

    def body(xs_ref, bm_ref, cm_ref, z_ref, dtr_ref, bias_ref, alog_ref, dsk_ref, gn_ref,
             y_ref, ya3_ref, prev_ref, state_ref):
        @pl.when(pl.program_id(1) == 0)
        def _():
            state_ref[...] = jnp.zeros_like(state_ref)

        xs = xs_ref[...].astype(F32)
        dt, apad, acs, dt_full, acs_full, tri, ex = _ssd_common(
            dtr_ref[...], bias_ref[...], alog_ref[...], n_heads)
        acs_t = acs.T
        e_full = jnp.exp(acs_full)
        last = acs_full[l - 1:l, :]
        dte = jnp.exp(last - acs_full)
        cd = jnp.exp(last)
        xdt = xs * dt_full
        xd = xdt * dte
        mask = _iota((l, l), 1) <= _iota((l, l), 0)
        lane = _iota((l, LANES), 1)
        prev_ref[0] = state_ref[...]
        for g in range(GROUPS):
            gsl = slice(g * gw, (g + 1) * gw)
            bg = bm_ref[:, g * D_STATE:(g + 1) * D_STATE].astype(BF16)
            cg = cm_ref[:, g * D_STATE:(g + 1) * D_STATE].astype(BF16)
            cb = _dot(cg, bg, NT)
            outs = []
            for q in range(pairs_per_group):
                pair = g * pairs_per_group + q
                h0, h1 = 2 * pair, 2 * pair + 1
                xpb = xdt[:, pair * LANES:(pair + 1) * LANES].astype(BF16)
                m0 = (cb * _decay_mats(acs, acs_t, h0, mask)).astype(BF16)
                m1 = (cb * _decay_mats(acs, acs_t, h1, mask)).astype(BF16)
                zero = jnp.zeros_like(xpb)
                rhs = jnp.concatenate([jnp.where(lane < HEAD_DIM, xpb, zero),
                                       jnp.where(lane >= HEAD_DIM, xpb, zero)], axis=0)
                outs.append(_dot(jnp.concatenate([m0, m1], axis=1), rhs))
            prev_g = state_ref[:, gsl]
            y_off = _dot(cg, prev_g.astype(BF16)) * e_full[:, gsl]
            y_g = jnp.concatenate(outs, axis=1) + y_off
            y_ref[:, gsl] = y_g.astype(y_ref.dtype)
            st = _dot(bg, xd[:, gsl].astype(BF16), TN)
            state_ref[:, gsl] = prev_g * cd[:, gsl] + st
            zv = z_ref[:, gsl].astype(F32)
            blk = (y_g + xs[:, gsl] * dsk_ref[:, gsl]) * (zv * _sigmoid(zv))
            rg = lax.rsqrt(jnp.mean(blk * blk, axis=-1, keepdims=True) + EPS)
            ya3_ref[:, gsl] = (blk * rg * gn_ref[:, gsl]).astype(ya3_ref.dtype)

    row = lambda b, c: b * nc + c
    small = lambda n: pl.BlockSpec((1, n), lambda b, c: (0, 0))
    return _pallas(
        body, name="ssd_fwd", grid=(nb, nc),
        in_specs=[pl.BlockSpec((l, d_inner), lambda b, c: (row(b, c), 0)),
                  pl.BlockSpec((l, gn_cols), lambda b, c: (row(b, c), d_inner // gn_cols)),
                  pl.BlockSpec((l, gn_cols), lambda b, c: (row(b, c), d_inner // gn_cols + 1)),
                  pl.BlockSpec((l, d_inner), lambda b, c: (row(b, c), 0)),
                  pl.BlockSpec((l, HEAD_PAD), lambda b, c: (row(b, c), 0)),
                  small(HEAD_PAD), small(HEAD_PAD), small(d_inner), small(d_inner)],
        out_specs=[pl.BlockSpec((l, d_inner), lambda b, c: (row(b, c), 0)),
                   pl.BlockSpec((l, d_inner), lambda b, c: (row(b, c), 0)),
                   pl.BlockSpec((1, D_STATE, d_inner), lambda b, c: (row(b, c), 0, 0))],
        out_shape=[jax.ShapeDtypeStruct((nb * s, d_inner), ACT),
                   jax.ShapeDtypeStruct((nb * s, d_inner), BF16),
                   jax.ShapeDtypeStruct((nb * nc, D_STATE, d_inner), F32)],
        scratch_shapes=[pltpu.VMEM((D_STATE, d_inner), F32)],
        compiler_params=_cparams(("parallel", "arbitrary"), 48),
    )(xbc, xbc, xbc, proj, dtraw, dt_bias, a_log, dskip_full, gnorm)


def _pool_fwd(proj, mixw, mix_b, scale, nb, s, off_u, off_zp, d_pool):
    pg = d_pool // len(POOL_WINDOWS)

    def body(u_ref, zp_ref, w_ref, b_ref, s_ref, pooled_ref, mixed_ref, yb_ref):
        g = pl.program_id(1)
        uv = u_ref[...].astype(F32)
        row = _iota(uv.shape, 0)
        pos = (_iota((uv.shape[0], 1), 0) + 1).astype(F32)
        for gi, w in enumerate(POOL_WINDOWS):
            @pl.when(g == gi)
            def _():
                acc, span = uv, 1
                while span < w:
                    acc = acc + _shift_down(acc, span, row)
                    span *= 2
                pooled_ref[...] = (acc / jnp.minimum(pos, float(w)) - uv).astype(pooled_ref.dtype)

        mixed = _dot(pooled_ref[...], w_ref[0])
        mixed_ref[...] = mixed.astype(mixed_ref.dtype)
        zp = zp_ref[...].astype(F32)
        yb_ref[...] = ((mixed + b_ref[...]) * s_ref[...] * (zp * _sigmoid(zp))).astype(yb_ref.dtype)

    seg = lambda off: pl.BlockSpec((s, pg), lambda b, g: (b, off // pg + g))
    vec = pl.BlockSpec((1, pg), lambda b, g: (0, g))
    return _pallas(
        body, name="pool_fwd", grid=(nb, len(POOL_WINDOWS)),
        in_specs=[seg(off_u), seg(off_zp), pl.BlockSpec((1, pg, pg), lambda b, g: (g, 0, 0)), vec, vec],
        out_specs=[seg(0), seg(0), seg(0)],
        out_shape=[jax.ShapeDtypeStruct((nb * s, d_pool), BF16), jax.ShapeDtypeStruct((nb * s, d_pool), ACT),
                   jax.ShapeDtypeStruct((nb * s, d_pool), BF16)],
        compiler_params=_cparams(("parallel", "parallel"), 48),
    )(proj, proj, mixw, mix_b, scale)


def _merge_to_grads(ya3, wa, yb, wb, proj, off_ga, off_gb, off_zp, wout, x, ple_g, wg, p, wup, target, final_g,
                    mixed, mix_b, scale, n_main):
    t, d = x.shape
    pd = p.shape[1]
    tm = 256
    assert off_gb == off_ga + d and off_zp == off_gb + d and off_ga % (3 * d) == 0 and mixed.shape[1] == d

    def body(a_ref, wa_ref, b_ref, wb_ref, ga_ref, gb_ref, wout_ref, x_ref, pg_ref, wg_ref, p_ref, wup_ref,
             tg_ref, fg_ref, zp_ref, mx_ref, mb_ref, sc_ref,
             mg_ref, h2_ref, dx1_ref, dx1b_ref, dgl_ref, dpu_ref, dpa_ref, dpb_ref, dmx_ref, dgate_ref,
             loss_ref, dfg_ref, dpg_ref, dsc_ref, dmb_ref):
        @pl.when(pl.program_id(0) == 0)
        def _():
            loss_ref[...] = jnp.zeros_like(loss_ref)
            dfg_ref[...] = jnp.zeros_like(dfg_ref)
            dpg_ref[...] = jnp.zeros_like(dpg_ref)
            dsc_ref[...] = jnp.zeros_like(dsc_ref)
            dmb_ref[...] = jnp.zeros_like(dmb_ref)

        pa = _dot(a_ref[...], wa_ref[...])
        pb = _dot(b_ref[...], wb_ref[...])
        sa = _sigmoid(ga_ref[...].astype(F32))
        sb = _sigmoid(gb_ref[...].astype(F32))
        merged = (sa * pa + sb * pb).astype(BF16)
        mg_ref[...] = merged
        x1 = x_ref[...] + _dot(merged, wout_ref[...])
        r1 = lax.rsqrt(jnp.mean(x1 * x1, axis=-1, keepdims=True) + EPS)
        xh1 = x1 * r1
        pg = pg_ref[...]
        h2 = (xh1 * pg).astype(BF16)
        h2_ref[...] = h2
        gate = _sigmoid(_dot(h2, wg_ref[...]))
        pu = _dot(p_ref[...].astype(BF16), wup_ref[...])
        x2 = x1 + gate * pu
        r2 = lax.rsqrt(jnp.mean(x2 * x2, axis=-1, keepdims=True) + EPS)
        xh2 = x2 * r2
        gfin = fg_ref[...]
        err = xh2 * gfin - tg_ref[...]
        loss_ref[...] += _sum_all(err * err) * (0.5 / d)
        dy = err * (1.0 / d)
        dfg_ref[...] += jnp.sum(dy * xh2, axis=0, keepdims=True)
        dg = dy * gfin
        dx2 = r2 * (dg - xh2 * jnp.mean(dg * xh2, axis=-1, keepdims=True))
        dgl = (dx2 * pu * gate * (1.0 - gate)).astype(BF16)
        dgl_ref[...] = dgl
        dpu_ref[...] = (dx2 * gate).astype(dpu_ref.dtype)
        dh2 = _dot(dgl, wg_ref[...], NT)
        dpg_ref[...] += jnp.sum(dh2 * xh1, axis=0, keepdims=True)
        dg = dh2 * pg
        dx1 = dx2 + r1 * (dg - xh1 * jnp.mean(dg * xh1, axis=-1, keepdims=True))
        dx1_ref[...] = dx1
        dx1b = dx1.astype(BF16)
        dx1b_ref[...] = dx1b
        dm = _dot(dx1b, wout_ref[...], NT)
        dpa_ref[...] = (dm * sa).astype(dpa_ref.dtype)
        dpb = (dm * sb).astype(BF16)
        dpb_ref[...] = dpb
        dgate_ref[:, :d] = (dm * pa * sa * (1.0 - sa)).astype(dgate_ref.dtype)
        dgate_ref[:, d:2 * d] = (dm * pb * sb * (1.0 - sb)).astype(dgate_ref.dtype)
        dyb = _dot(dpb, wb_ref[...], NT)
        zp = zp_ref[...].astype(F32)
        sg = _sigmoid(zp)
        mb = mx_ref[...].astype(F32) + mb_ref[...]
        scale = sc_ref[...]
        dyb0 = dyb * (zp * sg)
        dgate_ref[:, 2 * d:] = (dyb * (mb * scale) * (sg * (1.0 + zp * (1.0 - sg)))).astype(dgate_ref.dtype)
        dsc_ref[...] += jnp.sum(dyb0 * mb, axis=0, keepdims=True)
        dmx = dyb0 * scale
        dmb_ref[...] += jnp.sum(dmx, axis=0, keepdims=True)
        dmx_ref[...] = dmx.astype(dmx_ref.dtype)

    rows = lambda n: pl.BlockSpec((tm, n), lambda i: (i, 0))
    resident = lambda a: pl.BlockSpec(a.shape, lambda i: (0, 0), pipeline_mode=pl.Buffered(1))
    vec = lambda n: pl.BlockSpec((1, n), lambda i: (0, 0))
    bf = lambda: jax.ShapeDtypeStruct((t, d), BF16)
    return _pallas(
        body, name="merge_to_grads", grid=(t // tm,),
        in_specs=[rows(ya3.shape[1]), resident(wa), rows(yb.shape[1]), resident(wb),
                  pl.BlockSpec((tm, d), lambda i: (i, off_ga // d)), pl.BlockSpec((tm, d), lambda i: (i, off_gb // d)),
                  resident(wout), rows(d), vec(d), resident(wg), rows(pd), resident(wup), rows(d), vec(d),
                  pl.BlockSpec((tm, d), lambda i: (i, off_zp // d)), rows(d), vec(d), vec(d)],
        out_specs=[rows(d)] * 9 + [pl.BlockSpec((tm, 3 * d), lambda i: (i, off_ga // (3 * d))),
                                   vec(LANES), vec(d), vec(d), vec(d), vec(d)],
        out_shape=[bf(), bf(), jax.ShapeDtypeStruct((t, d), F32), bf(), bf(), bf(), bf(), bf(), bf(),
                   jax.ShapeDtypeStruct((t, n_main), BF16), jax.ShapeDtypeStruct((1, LANES), F32),
                   jax.ShapeDtypeStruct((1, d), F32), jax.ShapeDtypeStruct((1, d), F32),
                   jax.ShapeDtypeStruct((1, d), F32), jax.ShapeDtypeStruct((1, d), F32)],
        compiler_params=_cparams(("arbitrary",), 60),
    )(ya3, wa, yb, wb, proj, proj, wout, x, ple_g, wg, p, wup, target, final_g, proj, mixed, mix_b, scale)


def _dmixw(pooled, dmixed, ng):
    t, dp = pooled.shape
    pg = dp // ng
    tk = min(2048, t)
    nk = t // tk

    def body(a_ref, b_ref, o_ref):
        @pl.when(pl.program_id(1) == 0)
        def _():
            o_ref[...] = jnp.zeros_like(o_ref)

        o_ref[0] += _dot(a_ref[...], b_ref[...], TN)

    blk = pl.BlockSpec((tk, pg), lambda g, k: (k, g))
    return _pallas(
        body, name="dmixw", grid=(ng, nk),
        in_specs=[blk, blk],
        out_specs=pl.BlockSpec((1, pg, pg), lambda g, k: (g, 0, 0)),
        out_shape=jax.ShapeDtypeStruct((ng, pg, pg), F32),
        compiler_params=_cparams(("parallel", "arbitrary")),
    )(pooled, dmixed)


def _pool_bwd(dproj, dmixed, mixw, nb, s, off_u):
    dp = dmixed.shape[1]
    pg = dp // len(POOL_WINDOWS)

    def body(_, d_ref, w_ref, o_ref):
        g = pl.program_id(1)
        dv = _dot(d_ref[...], w_ref[0], NT)
        row = _iota(dv.shape, 0)
        pos = (_iota((dv.shape[0], 1), 0) + 1).astype(F32)
        for gi, w in enumerate(POOL_WINDOWS):
            @pl.when(g == gi)
            def _():
                acc, span = dv / jnp.minimum(pos, float(w)), 1
                while span < w:
                    acc = acc + _shift_up(acc, span, row)
                    span *= 2
                o_ref[...] = (acc - dv).astype(o_ref.dtype)

    return _pallas(
        body, name="pool_bwd", grid=(nb, len(POOL_WINDOWS)),
        in_specs=[ANY, pl.BlockSpec((s, pg), lambda b, g: (b, g)), pl.BlockSpec((1, pg, pg), lambda b, g: (g, 0, 0))],
        out_specs=pl.BlockSpec((s, pg), lambda b, g: (b, off_u // pg + g)),
        out_shape=jax.ShapeDtypeStruct(dproj.shape, dproj.dtype),
        input_output_aliases={0: 0},
        compiler_params=_cparams(("parallel", "parallel"), 48),
    )(dproj, dmixed, mixw)


def _ssd_bwd(dproj, dya3, y, xbc, proj, dtraw, prev, dt_bias, a_log, dskip_full, gnorm, nb, s, d_inner, partials):
    l = CHUNK
    nc = s // l
    n_heads = d_inner // HEAD_DIM
    gw = d_inner // GROUPS
    pairs_per_group = gw // LANES
    gn_cols = GROUPS * D_STATE
    n_conv = d_inner + 2 * gn_cols
    last_step = nb * nc - 1

    xch = _DirectExchange(partials)
    n_in, n_out = 13, 7

    def body(*refs):
        (_, dya3_ref, y_ref, xs_ref, bm_ref, cm_ref, z_ref, dtr_ref, prev_ref, bias_ref, alog_ref,
         dsk_ref, gn_ref) = refs[:n_in]
        p_refs = refs[n_in:n_in + xch.n]
        outs = refs[n_in + xch.n:]
        dz_ref, dxbc_ref, ddt_ref, dgn_ref, dskh_ref, dalog_ref, dbias_ref = outs[:n_out]
        got_refs = outs[n_out:n_out + xch.n]
        g_ref, dskacc_ref = outs[n_out + xch.n:n_out + xch.n + 2]
        sems = outs[n_out + xch.n + 2:]
        step = pl.program_id(0) * nc + pl.program_id(1)

        @pl.when(pl.program_id(1) == 0)
        def _():
            g_ref[...] = jnp.zeros_like(g_ref)

        @pl.when(step == 0)
        def _():
            xch.start(p_refs, got_refs, sems)
            dgn_ref[...] = jnp.zeros_like(dgn_ref)
            dalog_ref[...] = jnp.zeros_like(dalog_ref)
            dbias_ref[...] = jnp.zeros_like(dbias_ref)
            dskacc_ref[...] = jnp.zeros_like(dskacc_ref)

        xs = xs_ref[...].astype(F32)
        zv = z_ref[...].astype(F32)
        dsk = dsk_ref[...]
        sg = _sigmoid(zv)
        sz = zv * sg
        ya1 = y_ref[...].astype(F32) + xs * dsk
        ya2 = ya1 * sz
        parts = []
        for g in range(GROUPS):
            gsl = slice(g * gw, (g + 1) * gw)
            blk = ya2[:, gsl]
            rg = lax.rsqrt(jnp.mean(blk * blk, axis=-1, keepdims=True) + EPS)
            xh = blk * rg
            d3 = dya3_ref[:, gsl].astype(F32)
            dgn_ref[:, gsl] += jnp.sum(d3 * xh, axis=0, keepdims=True)
            dg = d3 * gn_ref[:, gsl]
            parts.append(rg * (dg - xh * jnp.mean(dg * xh, axis=-1, keepdims=True)))
        dya2 = jnp.concatenate(parts, axis=1)
        dy = dya2 * sz
        dz_ref[...] = (dya2 * ya1 * (sg * (1.0 + zv * (1.0 - sg)))).astype(dz_ref.dtype)
        dskacc_ref[...] += jnp.sum(dy * xs, axis=0, keepdims=True)

        dtraw_v = dtr_ref[...]
        bias = bias_ref[...]
        dt, apad, acs, dt_full, acs_full, tri, ex = _ssd_common(dtraw_v, bias, alog_ref[...], n_heads)
        acs_t = acs.T
        e_full = jnp.exp(acs_full)
        last = acs_full[l - 1:l, :]
        dte = jnp.exp(last - acs_full)
        cd = jnp.exp(last)
        xdt = xs * dt_full
        xd = xdt * dte
        dy_e = dy * e_full
        mask = _iota((l, l), 1) <= _iota((l, l), 0)
        lane = _iota((l, LANES), 1)
        hlane = _iota((l, HEAD_PAD), 1)
        hsub = _iota((HEAD_PAD, l), 0)
        dacs = jnp.zeros((l, HEAD_PAD), F32)
        dacs_t = jnp.zeros((HEAD_PAD, l), F32)
        dacs_full, dlast, dxdt_all, db_all, dc_all = [], [], [], [], []
        for g in range(GROUPS):
            gsl = slice(g * gw, (g + 1) * gw)
            bg = bm_ref[:, g * D_STATE:(g + 1) * D_STATE].astype(BF16)
            cg = cm_ref[:, g * D_STATE:(g + 1) * D_STATE].astype(BF16)
            prev_g = prev_ref[0, :, gsl]
            prev_b = prev_g.astype(BF16)
            gst = g_ref[:, gsl]
            gst_b = gst.astype(BF16)
            dye_b = dy_e[:, gsl].astype(BF16)
            dc = _dot(dye_b, prev_b, NT)
            dprev = _dot(cg, dye_b, TN)
            y_off = _dot(cg, prev_b) * e_full[:, gsl]
            xd_g = xd[:, gsl]
            db = _dot(xd_g.astype(BF16), gst_b, NT)
            dxd = _dot(bg, gst_b)
            wgt = dxd * xd_g
            dacs_full.append(dy[:, gsl] * y_off - wgt)
            dlast.append(jnp.sum(wgt, axis=0, keepdims=True)
                         + jnp.sum(prev_g * gst, axis=0, keepdims=True) * cd[:, gsl])
            g_ref[:, gsl] = dprev + cd[:, gsl] * gst
            dxdt_g = dxd * dte[:, gsl]
            cb = _dot(cg, bg, NT)
            dcb = jnp.zeros((l, l), F32)
            for q in range(pairs_per_group):
                pair = g * pairs_per_group + q
                psl = slice(pair * LANES, (pair + 1) * LANES)
                xpb = xdt[:, psl].astype(BF16)
                dyp = dy[:, psl].astype(BF16)
                zero = jnp.zeros_like(dyp)
                halves = (jnp.where(lane < HEAD_DIM, dyp, zero), jnp.where(lane >= HEAD_DIM, dyp, zero))
                acc = dxdt_g[:, q * LANES:(q + 1) * LANES]
                for k, dyh in enumerate(halves):
                    h = 2 * pair + k
                    dk = _decay_mats(acs, acs_t, h, mask)
                    mm = cb * dk
                    dm = _dot(dyh, xpb, NT)
                    dcb = dcb + dm * dk
                    dseg = dm * mm
                    dacs = dacs + jnp.where(hlane == h, jnp.sum(dseg, axis=1, keepdims=True), 0.0)
                    dacs_t = dacs_t + jnp.where(hsub == h, jnp.sum(dseg, axis=0, keepdims=True), 0.0)
                    acc = acc + _dot(mm.astype(BF16), dyh, TN)
                dxdt_all.append(acc)
            dcb_b = dcb.astype(BF16)
            dc_all.append(dc + _dot(dcb_b, bg))
            db_all.append(db + _dot(dcb_b, cg, TN))

        dxdt = jnp.concatenate(dxdt_all, axis=1)
        dxbc_ref[:, :d_inner] = (dy * dsk + dxdt * dt_full).astype(dxbc_ref.dtype)
        dxbc_ref[:, d_inner:d_inner + gn_cols] = jnp.concatenate(db_all, axis=1).astype(dxbc_ref.dtype)
        dxbc_ref[:, d_inner + gn_cols:] = jnp.concatenate(dc_all, axis=1).astype(dxbc_ref.dtype)
        ddt = _exact_r(dxdt * xs, ex, NT)
        rowi = _iota((l, d_inner), 0)
        dacs_f = jnp.concatenate(dacs_full, axis=1) + jnp.where(rowi == l - 1, jnp.concatenate(dlast, axis=1), 0.0)
        dacs = dacs + _exact_r(dacs_f, ex, NT) - dacs_t.T
        da = _exact_l(tri, dacs, TN)
        ddt = ddt + da * apad
        dalog_ref[...] += jnp.sum(da * dt, axis=0, keepdims=True) * apad
        ddt_raw = ddt * _sigmoid(dtraw_v + bias)
        dbias_ref[...] += jnp.sum(ddt_raw, axis=0, keepdims=True)
        ddt_ref[...] = ddt_raw.astype(ddt_ref.dtype)

        @pl.when(step == last_step)
        def _():
            dskh_ref[...] = _exact_r(dskacc_ref[...], ex, NT)[0:1, :]
            xch.wait(p_refs, got_refs, sems)

    row = lambda b, c: b * nc + (nc - 1 - c)
    small = lambda n: pl.BlockSpec((1, n), lambda b, c: (0, 0))
    wide = pl.BlockSpec((l, d_inner), lambda b, c: (row(b, c), 0))
    return _pallas(
        body, name="ssd_bwd", grid=(nb, nc),
        in_specs=[ANY, wide, wide, wide,
                  pl.BlockSpec((l, gn_cols), lambda b, c: (row(b, c), d_inner // gn_cols)),
                  pl.BlockSpec((l, gn_cols), lambda b, c: (row(b, c), d_inner // gn_cols + 1)),
                  wide, pl.BlockSpec((l, HEAD_PAD), lambda b, c: (row(b, c), 0)),
                  pl.BlockSpec((1, D_STATE, d_inner), lambda b, c: (row(b, c), 0, 0)),
                  small(HEAD_PAD), small(HEAD_PAD), small(d_inner), small(d_inner)] + xch.in_specs,
        out_specs=[wide, pl.BlockSpec((l, n_conv), lambda b, c: (row(b, c), 0)),
                   pl.BlockSpec((l, HEAD_PAD), lambda b, c: (row(b, c), 0)),
                   small(d_inner), small(HEAD_PAD), small(HEAD_PAD), small(HEAD_PAD)] + xch.out_specs,
        out_shape=[jax.ShapeDtypeStruct(dproj.shape, dproj.dtype),
                   jax.ShapeDtypeStruct((nb * s, n_conv), ACT),
                   jax.ShapeDtypeStruct((nb * s, HEAD_PAD), BF16),
                   jax.ShapeDtypeStruct((1, d_inner), F32), jax.ShapeDtypeStruct((1, HEAD_PAD), F32),
                   jax.ShapeDtypeStruct((1, HEAD_PAD), F32), jax.ShapeDtypeStruct((1, HEAD_PAD), F32)]
        + xch.out_shape,
        scratch_shapes=[pltpu.VMEM((D_STATE, d_inner), F32), pltpu.VMEM((8, d_inner), F32)] + xch.scratch,
        input_output_aliases={0: 0},
        compiler_params=_cparams(("arbitrary", "arbitrary"), 56),
    )(dproj, dya3, y, xbc, xbc, xbc, proj, dtraw, prev, dt_bias, a_log, dskip_full, gnorm, *partials)


def _conv_bwd(dproj, dxbc, conv_pre, proj, conv_w, nb, s, off_xbc):
    n_conv = dxbc.shape[1]
    cw = 256

    rc = CONV_ROWS

    def fold(v):
        return jnp.sum(v.reshape(rc // SUBLANES, SUBLANES, cw), axis=0)

    def body(_, d_ref, pre_ref, x_ref, w_ref, dx_ref, dw_ref, db_ref, dcpad_ref):
        @pl.when(pl.program_id(1) == 0)
        def _():
            dw_ref[...] = jnp.zeros_like(dw_ref)
            db_ref[...] = jnp.zeros_like(db_ref)

        dcpad_ref[pl.ds(s, SUBLANES), :] = jnp.zeros((SUBLANES, cw), F32)
        taps = [w_ref[k:k + 1, :] for k in range(CONV_W)]
        db = jnp.zeros((SUBLANES, cw), F32)
        for r in range(s // rc):
            rows = pl.ds(r * rc, rc)
            acc = pre_ref[rows, :].astype(F32)
            sg = _sigmoid(acc)
            dc = d_ref[rows, :].astype(F32) * (sg * (1.0 + acc * (1.0 - sg)))
            dcpad_ref[rows, :] = dc
            db = db + fold(dc)
        db_ref[...] += jnp.sum(db, axis=0, keepdims=True)
        dw = [jnp.zeros((SUBLANES, cw), F32) for _ in range(CONV_W)]
        for r in range(s // rc):
            xv = x_ref[pl.ds(r * rc, rc), :].astype(F32)
            win = dcpad_ref[pl.ds(r * rc, rc + SUBLANES), :]
            dxv = jnp.zeros((rc, cw), F32)
            for j in range(CONV_W):
                up = win[:rc, :] if j == 0 else pltpu.roll(win, rc + SUBLANES - j, 0)[:rc, :]
                dxv = dxv + up * taps[CONV_W - 1 - j]
                dw[CONV_W - 1 - j] = dw[CONV_W - 1 - j] + fold(up * xv)
            dx_ref[pl.ds(r * rc, rc), :] = dxv.astype(dx_ref.dtype)
        for k in range(CONV_W):
            dw_ref[k:k + 1, :] += jnp.sum(dw[k], axis=0, keepdims=True)

    blk = pl.BlockSpec((s, cw), lambda j, b: (b, j))
    return _pallas(
        body, name="conv_bwd", grid=(n_conv // cw, nb),
        in_specs=[ANY, blk, blk, pl.BlockSpec((s, cw), lambda j, b: (b, off_xbc // cw + j)),
                  pl.BlockSpec((CONV_W, cw), lambda j, b: (0, j))],
        out_specs=[pl.BlockSpec((s, cw), lambda j, b: (b, off_xbc // cw + j)),
                   pl.BlockSpec((CONV_W, cw), lambda j, b: (0, j)), pl.BlockSpec((1, cw), lambda j, b: (0, j))],
        out_shape=[jax.ShapeDtypeStruct(dproj.shape, dproj.dtype),
                   jax.ShapeDtypeStruct((CONV_W, n_conv), F32), jax.ShapeDtypeStruct((1, n_conv), F32)],
        scratch_shapes=[pltpu.VMEM((s + SUBLANES, cw), F32)],
        input_output_aliases={0: 0},
        compiler_params=_cparams(("parallel", "arbitrary"), 48),
    )(dproj, dxbc, conv_pre, proj, conv_w)


def _in_bwd(dproj, w_main, ddt, w_dt, x, norm_g, dx1, partials):
    t, d = x.shape
    n_main = dproj.shape[1]
    tm = 256
    ni = t // tm
    xch = _ChipExchange(partials)
    n_in, n_out = 7, 2

    def body(*refs):
        dp_ref, w_ref, ddt_ref, wdt_ref, x_ref, g_ref, dx1_ref = refs[:n_in]
        p_refs = refs[n_in:n_in + xch.n]
        outs = refs[n_in + xch.n:]
        dx_ref, dg_ref = outs[:n_out]
        got_refs = outs[n_out:n_out + xch.n]
        sems = outs[n_out + xch.n:]
        i = pl.program_id(0)

        @pl.when(i == 0)
        def _():
            xch.start(p_refs, got_refs, sems)
            dg_ref[...] = jnp.zeros_like(dg_ref)

        dh = _dot(dp_ref[...], w_ref[...], NT) + _dot(ddt_ref[...], wdt_ref[...], NT)
        xv = x_ref[...]
        r = lax.rsqrt(jnp.mean(xv * xv, axis=-1, keepdims=True) + EPS)
        xh = xv * r
        dg_ref[...] += jnp.sum(dh * xh, axis=0, keepdims=True)
        dg = dh * g_ref[...]
        dx_ref[...] = dx1_ref[...] + r * (dg - xh * jnp.mean(dg * xh, axis=-1, keepdims=True))

        @pl.when(i == ni - 1)
        def _():
            xch.wait(p_refs, got_refs, sems)

    rowblk = pl.BlockSpec((tm, d), lambda i: (i, 0))
    vec = pl.BlockSpec((1, d), lambda i: (0, 0))
    resident = lambda shape: pl.BlockSpec(shape, lambda i: (0, 0), pipeline_mode=pl.Buffered(1))
    return _pallas(
        body, name="in_bwd", grid=(ni,),
        in_specs=[pl.BlockSpec((tm, n_main), lambda i: (i, 0)), resident((d, n_main)),
                  pl.BlockSpec((tm, HEAD_PAD), lambda i: (i, 0)), resident((d, HEAD_PAD)),
                  rowblk, vec, rowblk] + xch.in_specs,
        out_specs=[rowblk, vec] + xch.out_specs,
        out_shape=[jax.ShapeDtypeStruct((t, d), F32), jax.ShapeDtypeStruct((1, d), F32)] + xch.out_shape,
        scratch_shapes=xch.scratch,
        compiler_params=_cparams(("arbitrary",), 56),
    )(dproj, w_main, ddt, w_dt, x, norm_g, dx1, *partials)


def _adamw_math(w, g, m, v):
    m = ADAM_B1 * m + (1.0 - ADAM_B1) * g
    v = ADAM_B2 * v + (1.0 - ADAM_B2) * (g * g)
    m_hat = m / (1.0 - ADAM_B1 ** ADAM_STEP)
    v_hat = v / (1.0 - ADAM_B2 ** ADAM_STEP)
    delta = -ADAM_LR * (m_hat / (jnp.sqrt(v_hat) + ADAM_EPS) + ADAM_WD * w)
    return delta, m, v


def _adamw(parts, w, m, v, name):
    n, rows, cols = parts.shape
    tr = _row_tile(rows, 256) if rows % 8 == 0 else rows

    def body(p_ref, w_ref, m_ref, v_ref, g_ref, d_ref, mo_ref, vo_ref):
        g = p_ref[0].astype(F32)
        for k in range(1, n):
            g = g + p_ref[k].astype(F32)
        d, mn, vn = _adamw_math(w_ref[...], g, m_ref[...], v_ref[...])
        g_ref[...] = g
        d_ref[...] = d
        mo_ref[...] = mn
        vo_ref[...] = vn

    blk = pl.BlockSpec((tr, cols), lambda i: (i, 0))
    shp = jax.ShapeDtypeStruct((rows, cols), F32)
    return _pallas(
        body, name=name, grid=(rows // tr,),
        in_specs=[pl.BlockSpec((n, tr, cols), lambda i: (0, i, 0)), blk, blk, blk],
        out_specs=[blk, blk, blk, blk], out_shape=[shp, shp, shp, shp],
        compiler_params=_cparams(("parallel",), 56),
    )(parts, w, m, v)


def _coords():
    return lax.axis_index("x"), lax.axis_index("y"), lax.axis_index("c")


class _TwoLevelGather:
    def __init__(self, blocks):
        self.n = len(blocks)
        self.in_specs = [ANY] * self.n
        self.out_specs = [ANY] * self.n
        self.out_shape = [jax.ShapeDtypeStruct((N_DEV,) + b.shape, b.dtype) for b in blocks]
        self.scratch = [pltpu.SemaphoreType.DMA((self.n, 7)), pltpu.SemaphoreType.DMA((self.n, 7)),
                        pltpu.SemaphoreType.DMA((self.n,))]

    def _parts(self, x_refs, out_refs, sems):
        send_sems, recv_sems, local_sems = sems
        x, y, c = _coords()
        me, sibling = (x, y, c), (x, y, 1 - c)
        chips = [(1 - x, y), (x, 1 - y), (1 - x, 1 - y)]

        def slot(a, px, py, pc):
            return out_refs[a].at[4 * px + 2 * py + pc]

        def copy(a, k, block, to, src=None):
            return pltpu.make_async_remote_copy(
                src_ref=slot(a, *block) if src is None else src, dst_ref=slot(a, *block),
                send_sem=send_sems.at[a, k], recv_sem=recv_sems.at[a, k], device_id=to, device_id_type=MESH)

        local = [pltpu.make_async_copy(x_refs[a], slot(a, *me), local_sems.at[a]) for a in range(self.n)]
        first = []
        for a in range(self.n):
            first.append(copy(a, 0, me, sibling, src=x_refs[a]))
            first += [copy(a, 1 + j, me, (*chip, c), src=x_refs[a]) for j, chip in enumerate(chips)]
        return copy, local, first, chips, me, sibling, c

    def start(self, x_refs, out_refs, sems):
        _, local, first, *_ = self._parts(x_refs, out_refs, sems)
        for cp in local + first:
            cp.start()

    def forward(self, x_refs, out_refs, sems):
        copy, _, _, chips, me, sibling, c = self._parts(x_refs, out_refs, sems)
        for j, chip in enumerate(chips):
            for a in range(self.n):
                copy(a, 1 + j, (*chip, c), me).wait_recv()
                copy(a, 4 + j, (*chip, c), sibling).start()

    def finish(self, x_refs, out_refs, sems):
        copy, local, first, chips, me, sibling, c = self._parts(x_refs, out_refs, sems)
        passed = [copy(a, 4 + j, (*chip, c), sibling) for j, chip in enumerate(chips) for a in range(self.n)]
        for a in range(self.n):
            copy(a, 0, sibling, me).wait_recv()
            for j, chip in enumerate(chips):
                copy(a, 4 + j, (*chip, 1 - c), me).wait_recv()
        for cp in first + passed:
            cp.wait_send()
        for cp in local:
            cp.wait()


def _rms_in_and_gather(xin, g, blk):
    rows, cols = blk.shape
    half = rows // 2
    assert half % 16 == 0
    t, d = xin.shape
    tt = 512
    ni = t // tt

    def exchange(x_ref, out_ref, send_sems, recv_sems, local_sem, begin):
        x, y, c = _coords()
        me, sib, xn, yn, dg = (x, y, c), (x, y, 1 - c), (1 - x, y, c), (x, 1 - y, c), (1 - x, 1 - y, c)
        other = lambda p: (p[0], p[1], 1 - c)
        top, bot = pl.ds(0, half), pl.ds(half, half)

        def slot(p, part=None):
            ref = out_ref.at[4 * p[0] + 2 * p[1] + p[2]]
            return ref if part is None else ref.at[part]

        def copy(k, block, to, part=None, src=None):
            return pltpu.make_async_remote_copy(
                src_ref=slot(block, part) if src is None else src, dst_ref=slot(block, part),
                send_sem=send_sems.at[k], recv_sem=recv_sems.at[k], device_id=to, device_id_type=MESH)

        mine = pltpu.make_async_copy(x_ref, slot(me), local_sem)
        own = [copy(1, me, xn, src=x_ref), copy(2, me, yn, src=x_ref), copy(0, me, sib, src=x_ref)]
        if begin:
            for cp in [mine] + own:
                cp.start()
            return
        copy(1, xn, me).wait_recv()
        fwd = [copy(3, xn, yn, top), copy(5, xn, sib)]
        for cp in fwd:
            cp.start()
        copy(2, yn, me).wait_recv()
        fwd += [copy(4, yn, xn, bot), copy(6, yn, sib)]
        for cp in fwd[2:]:
            cp.start()
        copy(3, dg, me, top).wait_recv()
        fwd.append(copy(7, dg, sib, top))
        fwd[-1].start()
        copy(4, dg, me, bot).wait_recv()
        fwd.append(copy(8, dg, sib, bot))
        fwd[-1].start()
        copy(0, sib, me).wait_recv()
        copy(5, other(xn), me).wait_recv()
        copy(6, other(yn), me).wait_recv()
        copy(7, other(dg), me, top).wait_recv()
        copy(8, other(dg), me, bot).wait_recv()
        for cp in own + fwd:
            cp.wait_send()
        mine.wait()

    def body(w_ref, x_ref, g_ref, out_ref, h_ref, send_sems, recv_sems, local_sem):
        i = pl.program_id(0)

        @pl.when(i == 0)
        def _():
            exchange(w_ref, out_ref, send_sems, recv_sems, local_sem, True)

        xv = x_ref[...]
        r = lax.rsqrt(jnp.mean(xv * xv, axis=-1, keepdims=True) + EPS)
        h_ref[...] = (xv * r * g_ref[...]).astype(h_ref.dtype)

        @pl.when(i == ni - 1)
        def _():
            exchange(w_ref, out_ref, send_sems, recv_sems, local_sem, False)

    return _pallas(
        body, name="rms_in_w_in_all_gather", grid=(ni,),
        in_specs=[ANY, pl.BlockSpec((tt, d), lambda i: (i, 0)), pl.BlockSpec((1, d), lambda i: (0, 0))],
        out_specs=[ANY, pl.BlockSpec((tt, d), lambda i: (i, 0))],
        out_shape=[jax.ShapeDtypeStruct((N_DEV, rows, cols), blk.dtype), jax.ShapeDtypeStruct((t, d), BF16)],
        scratch_shapes=[pltpu.SemaphoreType.DMA((9,)), pltpu.SemaphoreType.DMA((9,)), pltpu.SemaphoreType.DMA],
        compiler_params=_cparams(("arbitrary",)),
    )(blk, xin, g)


def _all_gather_direct(blk, name):
    rows, cols = blk.shape

    def body(x_ref, out_ref, send_sems, recv_sems, local_sem):
        x, y, c = _coords()
        me = 4 * x + 2 * y + c
        mine = pltpu.make_async_copy(x_ref, out_ref.at[me], local_sem)
        mine.start()
        copies = []
        for k in range(1, N_DEV):
            fx, fy, fc = (k >> 2) & 1, (k >> 1) & 1, k & 1
            peer = (x ^ fx, y ^ fy, c ^ fc)
            copies.append(pltpu.make_async_remote_copy(
                src_ref=x_ref, dst_ref=out_ref.at[me], send_sem=send_sems.at[k - 1], recv_sem=recv_sems.at[k - 1],
                device_id=peer, device_id_type=MESH))
        for cp in copies:
            cp.start()
        for k in range(1, N_DEV):
            fx, fy, fc = (k >> 2) & 1, (k >> 1) & 1, k & 1
            src = 4 * (x ^ fx) + 2 * (y ^ fy) + (c ^ fc)
            pltpu.make_async_remote_copy(
                src_ref=x_ref, dst_ref=out_ref.at[src], send_sem=send_sems.at[k - 1], recv_sem=recv_sems.at[k - 1],
                device_id=(x, y, c), device_id_type=MESH).wait_recv()
        for cp in copies:
            cp.wait_send()
        mine.wait()

    return _pallas(
        body, name=name, in_specs=[ANY], out_specs=ANY,
        out_shape=jax.ShapeDtypeStruct((N_DEV, rows, cols), blk.dtype),
        scratch_shapes=[pltpu.SemaphoreType.DMA((7,)), pltpu.SemaphoreType.DMA((7,)), pltpu.SemaphoreType.DMA],
    )(blk)


def _pair_exchange(grads, name):
    n = len(grads)

    def body(*refs):
        g_refs, out_refs = refs[:n], refs[n:2 * n]
        send_sems, recv_sems = refs[2 * n:]
        x, y, c = _coords()
        copies = []
        for a in range(n):
            for k in range(4):
                copies.append(pltpu.make_async_remote_copy(
                    src_ref=g_refs[a].at[2 * k + (1 - c)], dst_ref=out_refs[a].at[k],
                    send_sem=send_sems.at[a, k], recv_sem=recv_sems.at[a, k],
                    device_id=(x, y, 1 - c), device_id_type=MESH))
        for cp in copies:
            cp.start()
        for cp in copies:
            cp.wait()

    return _pallas(
        body, name=name, in_specs=[ANY] * n, out_specs=[ANY] * n,
        out_shape=[jax.ShapeDtypeStruct((4,) + g.shape[1:], g.dtype) for g in grads],
        scratch_shapes=[pltpu.SemaphoreType.DMA((n, 4)), pltpu.SemaphoreType.DMA((n, 4))],
    )(*grads)


def _pair_add(grads, got, core, out_dtype, name):
    _, rows, cols = grads.shape
    tr = _row_tile(rows, 1024) if rows % 8 == 0 else rows

    def body(c_ref, a_ref, b_ref, o_ref):
        o_ref[...] = (a_ref[...].astype(F32) + b_ref[...].astype(F32)).astype(o_ref.dtype)

    grid_spec = pltpu.PrefetchScalarGridSpec(
        num_scalar_prefetch=1, grid=(4, rows // tr),
        in_specs=[pl.BlockSpec((1, tr, cols), lambda k, i, c_ref: (2 * k + c_ref[0], i, 0)),
                  pl.BlockSpec((1, tr, cols), lambda k, i, c_ref: (k, i, 0))],
        out_specs=pl.BlockSpec((1, tr, cols), lambda k, i, c_ref: (k, i, 0)))
    return _pallas(
        body, name=name, grid_spec=grid_spec,
        out_shape=jax.ShapeDtypeStruct((4, rows, cols), out_dtype),
        compiler_params=_cparams(("parallel", "parallel"), 48),
    )(core, grads, got)


class _ChipExchange:
    def __init__(self, partials):
        self.n = len(partials)
        self.in_specs = [ANY] * self.n
        self.out_specs = [ANY] * self.n
        self.out_shape = [jax.ShapeDtypeStruct(p.shape, p.dtype) for p in partials]
        self.scratch = [pltpu.SemaphoreType.DMA((self.n, 3)), pltpu.SemaphoreType.DMA((self.n, 3)),
                        pltpu.SemaphoreType.DMA((self.n,))]

    def _copies(self, p_refs, out_refs, sems, receiving):
        send_sems, recv_sems, local_sems = sems
        x, y, c = _coords()
        mychip = 2 * x + y
        local = [pltpu.make_async_copy(p_refs[a].at[mychip], out_refs[a].at[mychip], local_sems.at[a])
                 for a in range(self.n)]
        remote = []
        for a in range(self.n):
            for k in range(1, 4):
                px, py = x ^ ((k >> 1) & 1), y ^ (k & 1)
                peer = 2 * px + py
                remote.append(pltpu.make_async_remote_copy(
                    src_ref=p_refs[a].at[peer], dst_ref=out_refs[a].at[peer if receiving else mychip],
                    send_sem=send_sems.at[a, k - 1], recv_sem=recv_sems.at[a, k - 1],
                    device_id=(x, y, c) if receiving else (px, py, c), device_id_type=MESH))
        return local, remote

    def start(self, p_refs, out_refs, sems):
        local, sends = self._copies(p_refs, out_refs, sems, False)
        for cp in local + sends:
            cp.start()

    def wait(self, p_refs, out_refs, sems):
        local, recvs = self._copies(p_refs, out_refs, sems, True)
        for cp in recvs:
            cp.wait_recv()
        for cp in recvs:
            cp.wait_send()
        for cp in local:
            cp.wait()


class _DirectExchange:
    def __init__(self, partials):
        self.n = len(partials)
        self.in_specs = [ANY] * self.n
        self.out_specs = [ANY] * self.n
        self.out_shape = [jax.ShapeDtypeStruct(p.shape, p.dtype) for p in partials]
        self.scratch = [pltpu.SemaphoreType.DMA((self.n, N_DEV - 1)), pltpu.SemaphoreType.DMA((self.n, N_DEV - 1)),
                        pltpu.SemaphoreType.DMA((self.n,))]

    def _copies(self, p_refs, out_refs, sems, receiving):
        send_sems, recv_sems, local_sems = sems
        x, y, c = _coords()
        me = 4 * x + 2 * y + c
        local = [pltpu.make_async_copy(p_refs[a].at[me], out_refs[a].at[me], local_sems.at[a])
                 for a in range(self.n)]
        remote = []
        for a in range(self.n):
            for k in range(1, N_DEV):
                px, py, pc = x ^ ((k >> 2) & 1), y ^ ((k >> 1) & 1), c ^ (k & 1)
                peer = 4 * px + 2 * py + pc
                remote.append(pltpu.make_async_remote_copy(
                    src_ref=p_refs[a].at[peer], dst_ref=out_refs[a].at[peer if receiving else me],
                    send_sem=send_sems.at[a, k - 1], recv_sem=recv_sems.at[a, k - 1],
                    device_id=(x, y, c) if receiving else (px, py, pc), device_id_type=MESH))
        return local, remote

    def start(self, p_refs, out_refs, sems):
        local, sends = self._copies(p_refs, out_refs, sems, False)
        for cp in local + sends:
            cp.start()

    def wait(self, p_refs, out_refs, sems):
        local, recvs = self._copies(p_refs, out_refs, sems, True)
        for cp in recvs:
            cp.wait_recv()
        for cp in recvs:
            cp.wait_send()
        for cp in local:
            cp.wait()


def _pad_lanes(v, n=HEAD_PAD):
    return jnp.pad(v, ((0, 0), (0, n - v.shape[1])))


def kernel(x, p, norm_g, w_in, conv_w, conv_b, dt_bias, a_log, d_skip, gnorm_g, pool_mix_w, pool_mix_b, pool_scale, w_branch_a, w_branch_b, w_out, ple_norm_g, w_ple_gate, w_ple_up, final_g, loss_target, m_norm_g, m_w_in, m_conv_w, m_conv_b, m_dt_bias, m_a_log, m_d_skip, m_gnorm_g, m_pool_mix_w, m_pool_mix_b, m_pool_scale, m_w_branch_a, m_w_branch_b, m_w_out, m_ple_norm_g, m_w_ple_gate, m_w_ple_up, m_final_g, v_norm_g, v_w_in, v_conv_w, v_conv_b, v_dt_bias, v_a_log, v_d_skip, v_gnorm_g, v_pool_mix_w, v_pool_mix_b, v_pool_scale, v_w_branch_a, v_w_branch_b, v_w_out, v_ple_norm_g, v_w_ple_gate, v_w_ple_up, v_final_g):
    nb, s, d = x.shape
    t = nb * s
    n_heads = dt_bias.shape[1]
    d_inner = n_heads * HEAD_DIM
    n_conv = conv_b.shape[1]
    d_pool = pool_mix_b.shape[1]
    n_in = w_in.shape[2] * N_DEV
    ple_dim = p.shape[-1]
    o_xbc, o_dt = d_inner, d_inner + n_conv
    o_u = o_dt + n_heads
    o_zp = o_u + d_pool
    o_ga, o_gb = o_zp + d_pool, o_zp + d_pool + d
    n_main = n_in - n_heads
    off_u = d_inner
    off_ga = off_u + d_pool
    off_gb, off_zp = off_ga + d, off_ga + 2 * d
    off_xbc = off_zp + d_pool

    sharded = [w_in[0], conv_w[0], pool_mix_w[0], w_branch_a[0], w_branch_b[0], w_out[0], w_ple_gate[0], w_ple_up[0]]
    sharded_m = [m_w_in[0], m_conv_w[0], m_pool_mix_w[0], m_w_branch_a[0], m_w_branch_b[0], m_w_out[0],
                 m_w_ple_gate[0], m_w_ple_up[0]]
    sharded_v = [v_w_in[0], v_conv_w[0], v_pool_mix_w[0], v_w_branch_a[0], v_w_branch_b[0], v_w_out[0],
                 v_w_ple_gate[0], v_w_ple_up[0]]
    ng = pool_mix_w.shape[1]
    pg = d_pool // ng
    two_d = lambda w: w.reshape(-1, w.shape[-1])
    sharded, sharded_m, sharded_v = ([two_d(w) for w in ws] for ws in (sharded, sharded_m, sharded_v))

    send = [w if i == 1 else w.astype(BF16) for i, w in enumerate(sharded)]
    x2 = x.reshape(t, d)
    chunk = n_in // N_DEV
    segments = [((0, o_xbc), 0), ((o_u, o_zp), off_u), ((o_ga, o_gb), off_ga), ((o_gb, n_in), off_gb),
                ((o_zp, o_ga), off_zp), ((o_xbc, o_dt), off_xbc)]

    def pieces_of(dev):
        lo, hi = dev * chunk, (dev + 1) * chunk
        out = []
        for (olo, ohi), off in sorted(segments + [((o_dt, o_u), None)], key=lambda e: e[0][0]):
            a, b = max(lo, olo), min(hi, ohi)
            if a < b:
                out.append((a - lo, b - lo, None if off is None else off + a - olo))
        return out

    def lane_cover(a, b, off):
        left = 0 if off is None else off % LANES
        return left, -(-(left + b - a) // LANES) * LANES

    send_width = max(sum(lane_cover(*pc)[1] for pc in pieces_of(dev)) for dev in range(N_DEV))

    def aligned_shard(dev):
        def build(w):
            bufs = []
            for a, b, off in pieces_of(dev):
                left, width = lane_cover(a, b, off)
                bufs.append(jnp.pad(w[:, a:b], ((0, 0), (left, width - left - (b - a)))))
            out = jnp.concatenate(bufs, axis=1)
            return jnp.pad(out, ((0, 0), (0, send_width - out.shape[1])))
        return build

    me_lin = 4 * lax.axis_index("x") + 2 * lax.axis_index("y") + lax.axis_index("c")
    w_in_sent = lax.switch(me_lin, [aligned_shard(dev) for dev in range(N_DEV)], send[0])
    w_in_all, h = _rms_in_and_gather(x2, norm_g, w_in_sent)
    seg = [w_in_all]
    w_main = jnp.zeros((d, n_main), BF16)
    w_dt = None
    for dev in range(N_DEV):
        q = 0
        for a, b, off in pieces_of(dev):
            left, width = lane_cover(a, b, off)
            cover = w_in_all[dev, :, q:q + width]
            q += width
            if off is None:
                w_dt = cover
            else:
                start = off - left
                w_main = w_main + jnp.pad(cover, ((0, 0), (start, n_main - start - width)))

    proj, dtraw, *rest = _in_proj(h, w_main, w_dt, send[1:])
    seg += rest
    cw_full = seg[1].transpose(1, 0, 2).reshape(CONV_W, n_conv)
    mixw_full = seg[2].reshape(N_DEV, ng, pg // N_DEV, pg).transpose(1, 0, 2, 3).reshape(ng, pg, pg)
    wa_full = seg[3].reshape(d_inner, d)
    wb_full = seg[4].reshape(d_pool, d)
    wout_full = seg[5].reshape(d, d)
    wg_full = seg[6].reshape(d, d)
    wup_full = seg[7].transpose(1, 0, 2).reshape(ple_dim, d)

    p2 = p.reshape(t, ple_dim)
    tgt = loss_target.reshape(t, d)
    bias_p, alog_p = _pad_lanes(dt_bias), _pad_lanes(a_log)
    dskip_full = jnp.repeat(d_skip, HEAD_DIM, axis=1)

    xbc, conv_pre = _conv_fwd(proj, cw_full, conv_b, nb, s, off_xbc, n_conv)
    y, ya3, prev = _ssd_fwd(xbc, proj, dtraw, bias_p, alog_p, dskip_full, gnorm_g, nb, s, d_inner)
    pooled, mixed, yb = _pool_fwd(proj, mixw_full, pool_mix_b, pool_scale, nb, s, off_u, off_zp, d_pool)
    (merged, h2, dx1, dx1b, dgl, dpu, dpa, dpb, dmixed, dproj, loss_part, dfinal_g, d_ple_g, d_scale,
     d_mixb) = _merge_to_grads(
        ya3, wa_full, yb, wb_full, proj, off_ga, off_gb, off_zp, wout_full, x2, ple_norm_g, wg_full, p2, wup_full,
        tgt, final_g.reshape(1, d), mixed, pool_mix_b, pool_scale, n_main)

    g_wup = _matmul(p2, dpu, "tn", BF16, "dw_ple_up", 256, 512, t)
    g_wg = _matmul(h2, dgl, "tn", BF16, "dw_ple_gate", 512, 1024, t)
    g_wout = _matmul(merged, dx1b, "tn", BF16, "dw_out", 512, 1024, t)
    g_wa = _matmul(ya3, dpa, "tn", BF16, "dw_branch_a", 512, 1024, t)
    g_wb = _matmul(yb, dpb, "tn", BF16, "dw_branch_b", 512, 1024, t)
    dya3 = _matmul(dpa, wa_full, "nt", ACT, "d_ya3", 1024, 1024, d)
    g_mixw = _dmixw(pooled, dmixed, ng)
    dproj = _pool_bwd(dproj, dmixed, mixw_full, nb, s, off_u)

    core = lax.axis_index("c").astype(I32).reshape(1)

    def pair_sums(by_dest, names):
        by_dest = [g.astype(BF16) for g in by_dest]
        got = _pair_exchange(by_dest, "rs_pair_exchange_" + names[0])
        return [_pair_add(g, r, core, BF16, "rs_pair_add_" + nm) for g, r, nm in zip(by_dest, got, names)]

    names_a = ["pool_mix_w", "w_branch_a", "w_branch_b", "w_out", "w_ple_gate", "w_ple_up"]
    partials_a = [
        g_mixw.reshape(ng, N_DEV, pg // N_DEV, pg).transpose(1, 0, 2, 3).reshape(N_DEV, ng * pg // N_DEV, pg)
        .astype(BF16),
        g_wa.reshape(N_DEV, d_inner // N_DEV, d), g_wb.reshape(N_DEV, d_pool // N_DEV, d),
        g_wout.reshape(N_DEV, d // N_DEV, d), g_wg.reshape(N_DEV, d // N_DEV, d),
        g_wup.reshape(ple_dim, N_DEV, d // N_DEV).transpose(1, 0, 2)]
    dproj, dxbc, ddt, d_gn, d_dsk, d_alog, d_dtb, *parts_a = _ssd_bwd(
        dproj, dya3, y, xbc, proj, dtraw, prev, bias_p, alog_p, dskip_full, gnorm_g, nb, s, d_inner, partials_a)
    dproj, g_cw, d_cb = _conv_bwd(dproj, dxbc, conv_pre, proj, cw_full, nb, s, off_xbc)
    g_wmain, g_wdt = _dw_in(h, dproj, ddt)

    def dest_cols(dev):
        lo, hi = dev * chunk, (dev + 1) * chunk
        srcs = [((olo, ohi), g_wmain, off) for (olo, ohi), off in segments] + [((o_dt, o_u), g_wdt, 0)]
        out = []
        for (olo, ohi), arr, off in sorted(srcs, key=lambda e: e[0][0]):
            a, b = max(lo, olo), min(hi, ohi)
            if a < b:
                out.append(arr[:, off + a - olo:off + b - olo])
        return jnp.concatenate(out, axis=1)

    names_b = ["w_in", "conv_w"]
    partials_b = pair_sums([jnp.stack([dest_cols(dev) for dev in range(N_DEV)], axis=0),
                            g_cw.reshape(CONV_W, N_DEV, n_conv // N_DEV).transpose(1, 0, 2)], names_b)
    grad_x, d_ng, *parts_b = _in_bwd(dproj, w_main, ddt, w_dt, x2, norm_g, dx1, partials_b)

    names = names_b + names_a
    parts = parts_b + parts_a
    sh_g, sh_d, sh_m, sh_v = [], [], [], []
    shapes = [w_in.shape, conv_w.shape, pool_mix_w.shape, w_branch_a.shape, w_branch_b.shape, w_out.shape,
              w_ple_gate.shape, w_ple_up.shape]
    for pt, w, m, v, nm, shp in zip(parts, sharded, sharded_m, sharded_v, names, shapes):
        res = _adamw(pt, w, m, v, "adamw_" + nm)
        for dst, r in zip((sh_g, sh_d, sh_m, sh_v), res):
            dst.append(r.reshape(shp))

    rep_w = [norm_g, conv_b, gnorm_g, pool_mix_b, pool_scale, ple_norm_g, final_g.reshape(1, d),
             bias_p, alog_p, _pad_lanes(d_skip)]
    rep_m = [m_norm_g, m_conv_b, m_gnorm_g, m_pool_mix_b, m_pool_scale, m_ple_norm_g, m_final_g.reshape(1, d),
             _pad_lanes(m_dt_bias), _pad_lanes(m_a_log), _pad_lanes(m_d_skip)]
    rep_v = [v_norm_g, v_conv_b, v_gnorm_g, v_pool_mix_b, v_pool_scale, v_ple_norm_g, v_final_g.reshape(1, d),
             _pad_lanes(v_dt_bias), _pad_lanes(v_a_log), _pad_lanes(v_d_skip)]
    rep_g = [d_ng, d_cb, d_gn, d_mixb, d_scale, d_ple_g, dfinal_g, d_dtb, d_alog, d_dsk]
    zero = jnp.zeros((1, LANES), F32)
    small = _all_gather_direct(jnp.concatenate(rep_g + [loss_part], axis=1), "small_all_gather")
    pk = lambda ws: jnp.concatenate(ws + [zero], axis=1)
    rg, rd, rm, rv = _adamw(small, pk(rep_w), pk(rep_m), pk(rep_v), "adamw_replicated")
    loss = rg[0, sum(int(w.size) for w in rep_w)]

    def unpack_rep(buf):
        out, pos_ = [], 0
        for w in rep_w:
            out.append(buf[:, pos_:pos_ + w.size])
            pos_ += w.size
        return out

    def arrange(sh, rep):
        rep = list(rep)
        rep[6] = rep[6].reshape(d)
        for i in (7, 8, 9):
            rep[i] = rep[i][:, :n_heads]
        return [rep[0], sh[0], sh[1], rep[1], rep[7], rep[8], rep[9], rep[2], sh[2], rep[3], rep[4],
                sh[3], sh[4], sh[5], rep[5], sh[6], sh[7], rep[6]]

    outs = [loss, grad_x.reshape(nb, s, d)]
    for sh, rep in ((sh_g, rg), (sh_d, rd), (sh_m, rm), (sh_v, rv)):
        outs += arrange(sh, unpack_rep(rep))
    return tuple(outs)
```
